```python
import jax, jax.numpy as jnp
from jax import lax
import numpy as np

D_MODEL = 2048
BATCH = 8
SEQ = 4096
DEPTH = 2

N_META = 16
D_LRU = D_MODEL // 2
LRU_HEADS = 8
LRU_HEAD_DIM = D_LRU // LRU_HEADS
CONV_WIDTH = 4
LRU_C = 8.0
D_POOL = D_MODEL // 2
POOL_WINDOWS = (2, 4, 8, 16)
POOL_GROUPS = len(POOL_WINDOWS)
POOL_GROUP_DIM = D_POOL // POOL_GROUPS
D_MIX = D_LRU + D_POOL
D_IN = 2 * D_LRU + D_POOL
D_FF = ((8 * D_MODEL // 3 + 255) // 256) * 256
RMS_EPS = 1e-6

kernel_name = "hymba_rglru_pool_macaron"


def rms_norm(x, g):
    xf = x.astype(jnp.float32)
    y = xf * lax.rsqrt(jnp.mean(xf * xf, axis=-1, keepdims=True) + RMS_EPS)
    return (y * g.astype(jnp.float32)).astype(x.dtype)


def swiglu(h, w_in, w_out):
    g, u = jnp.split(h @ w_in, 2, axis=-1)
    return (jax.nn.silu(g) * u) @ w_out


def causal_dwconv(x, w, b):
    T = x.shape[1]
    xp = jnp.pad(x, ((0, 0), (CONV_WIDTH - 1, 0), (0, 0)))
    y = b
    for k in range(CONV_WIDTH):
        y = y + xp[:, k:k + T] * w[k]
    return y


def rg_lru(x, wa, ba, wx, bx, a_param):
    B, T, _ = x.shape
    xh = x.reshape(B, T, LRU_HEADS, LRU_HEAD_DIM)
    r = jax.nn.sigmoid((jnp.einsum('bthi,hij->bthj', xh, wa).reshape(B, T, D_LRU) + ba).astype(jnp.float32))
    i = jax.nn.sigmoid((jnp.einsum('bthi,hij->bthj', xh, wx).reshape(B, T, D_LRU) + bx).astype(jnp.float32))
    log_a = -LRU_C * r * jax.nn.softplus(-a_param.astype(jnp.float32))
    a = jnp.exp(log_a)
    mult = jnp.sqrt(-jnp.expm1(2.0 * log_a))
    bt = mult * i * x.astype(jnp.float32)

    def combine(left, right):
        a1, b1 = left
        a2, b2 = right
        return a1 * a2, a2 * b1 + b2

    _, h = lax.associative_scan(combine, (a, bt), axis=1)
    return h.astype(x.dtype)


def multiscale_pool(u, w, b, scale):
    B, T, _ = u.shape
    ug = u.reshape(B, T, POOL_GROUPS, POOL_GROUP_DIM).astype(jnp.float32)
    cs = jnp.cumsum(ug, axis=1)
    t1 = jnp.arange(1, T + 1, dtype=jnp.float32)
    outs = []
    for g, win in enumerate(POOL_WINDOWS):
        c = cs[:, :, g]
        lag = jnp.pad(c[:, :T - win], ((0, 0), (win, 0), (0, 0)))
        cnt = jnp.minimum(t1, float(win))[None, :, None]
        outs.append((c - lag) / cnt - ug[:, :, g])
    d = jnp.stack(outs, axis=2).astype(u.dtype)
    y = jnp.einsum('btgi,gij->btgj', d, w).reshape(B, T, D_POOL) + b
    return y * scale


def _fwd_setup_inputs(seed: int = 0) -> dict:
    key = jax.random.key(seed)
    ks = iter(jax.random.split(key, 32))
    f32 = jnp.float32

    def nrm(shape, fan_in):
        return jax.random.normal(next(ks), shape, f32) * (fan_in ** -0.5)

    def gain(shape):
        return 1.0 + 0.02 * jax.random.normal(next(ks), shape, f32)

    def bias(shape):
        return 0.01 * jax.random.normal(next(ks), shape, f32)

    L = DEPTH
    x = jax.random.normal(next(ks), (BATCH, SEQ, D_MODEL), f32)
    meta_tokens = jax.random.normal(next(ks), (N_META, D_MODEL), f32)
    ffn1_norm = gain((L, D_MODEL))
    ffn1_w_in = nrm((L, D_MODEL, 2 * D_FF), D_MODEL)
    ffn1_w_out = nrm((L, D_FF, D_MODEL), D_FF)
    mix_norm = gain((L, D_MODEL))
    w_in = nrm((L, D_MODEL, D_IN), D_MODEL)
    conv_w = nrm((L, CONV_WIDTH, D_LRU), CONV_WIDTH)
    conv_b = bias((L, D_LRU))
    lru_wa = nrm((L, LRU_HEADS, LRU_HEAD_DIM, LRU_HEAD_DIM), LRU_HEAD_DIM)
    lru_ba = bias((L, D_LRU))
    lru_wx = nrm((L, LRU_HEADS, LRU_HEAD_DIM, LRU_HEAD_DIM), LRU_HEAD_DIM)
    lru_bx = bias((L, D_LRU))
    a8 = jax.random.uniform(next(ks), (L, D_LRU), f32, 0.9, 0.999)
    a0 = a8 ** (1.0 / LRU_C)
    lru_a_param = jnp.log(a0) - jnp.log1p(-a0)
    pool_w = nrm((L, POOL_GROUPS, POOL_GROUP_DIM, POOL_GROUP_DIM), POOL_GROUP_DIM)
    pool_b = bias((L, D_POOL))
    pool_scale = 1.0 + 0.1 * jax.random.normal(next(ks), (L, D_POOL), f32)
    w_out = nrm((L, D_MIX, D_MODEL), D_MIX)
    ffn2_norm = gain((L, D_MODEL))
    ffn2_w_in = nrm((L, D_MODEL, 2 * D_FF), D_MODEL)
    ffn2_w_out = nrm((L, D_FF, D_MODEL), D_FF)
    final_norm = gain((D_MODEL,))
    return {"x": x, "meta_tokens": meta_tokens,
            "ffn1_norm": ffn1_norm, "ffn1_w_in": ffn1_w_in, "ffn1_w_out": ffn1_w_out,
            "mix_norm": mix_norm, "w_in": w_in, "conv_w": conv_w, "conv_b": conv_b,
            "lru_wa": lru_wa, "lru_ba": lru_ba, "lru_wx": lru_wx, "lru_bx": lru_bx,
            "lru_a_param": lru_a_param, "pool_w": pool_w, "pool_b": pool_b,
            "pool_scale": pool_scale, "w_out": w_out,
            "ffn2_norm": ffn2_norm, "ffn2_w_in": ffn2_w_in, "ffn2_w_out": ffn2_w_out,
            "final_norm": final_norm}


def _fwd_reference(x, meta_tokens, ffn1_norm, ffn1_w_in, ffn1_w_out, mix_norm, w_in, conv_w, conv_b,
              lru_wa, lru_ba, lru_wx, lru_bx, lru_a_param, pool_w, pool_b, pool_scale, w_out,
              ffn2_norm, ffn2_w_in, ffn2_w_out, final_norm):
    B = x.shape[0]
    meta = jnp.broadcast_to(meta_tokens.astype(x.dtype)[None], (B, N_META, D_MODEL))
    h = jnp.concatenate([meta, x], axis=1)
    for l in range(DEPTH):
        h = h + 0.5 * swiglu(rms_norm(h, ffn1_norm[l]), ffn1_w_in[l], ffn1_w_out[l])
        z = rms_norm(h, mix_norm[l]) @ w_in[l]
        zx, zg, zp = jnp.split(z, [D_LRU, 2 * D_LRU], axis=-1)
        ya = rg_lru(causal_dwconv(zx, conv_w[l], conv_b[l]),
                    lru_wa[l], lru_ba[l], lru_wx[l], lru_bx[l], lru_a_param[l]) * jax.nn.gelu(zg)
        yb = multiscale_pool(zp, pool_w[l], pool_b[l], pool_scale[l])
        h = h + jnp.concatenate([ya, yb], axis=-1) @ w_out[l]
        h = h + 0.5 * swiglu(rms_norm(h, ffn2_norm[l]), ffn2_w_in[l], ffn2_w_out[l])
    return rms_norm(h, final_norm)[:, N_META:]


import jax as _jax
import jax.numpy as _jnp

TWIN_FORMAT = 'train_step'
FWD_PARAMS = ['x', 'meta_tokens', 'ffn1_norm', 'ffn1_w_in', 'ffn1_w_out', 'mix_norm', 'w_in', 'conv_w', 'conv_b', 'lru_wa', 'lru_ba', 'lru_wx', 'lru_bx', 'lru_a_param', 'pool_w', 'pool_b', 'pool_scale', 'w_out', 'ffn2_norm', 'ffn2_w_in', 'ffn2_w_out', 'final_norm']
TWIN_WEIGHTS = ['meta_tokens', 'ffn1_norm', 'ffn1_w_in', 'ffn1_w_out', 'mix_norm', 'w_in', 'conv_w', 'conv_b', 'lru_wa', 'lru_ba', 'lru_wx', 'lru_bx', 'lru_a_param', 'pool_w', 'pool_b', 'pool_scale', 'w_out', 'ffn2_norm', 'ffn2_w_in', 'ffn2_w_out', 'final_norm']
TWIN_DIFF_INPUT = 'x'
TWIN_INPUTS = ['x', 'meta_tokens', 'ffn1_norm', 'ffn1_w_in', 'ffn1_w_out', 'mix_norm', 'w_in', 'conv_w', 'conv_b', 'lru_wa', 'lru_ba', 'lru_wx', 'lru_bx', 'lru_a_param', 'pool_w', 'pool_b', 'pool_scale', 'w_out', 'ffn2_norm', 'ffn2_w_in', 'ffn2_w_out', 'final_norm', 'loss_target', 'm_meta_tokens', 'm_ffn1_norm', 'm_ffn1_w_in', 'm_ffn1_w_out', 'm_mix_norm', 'm_w_in', 'm_conv_w', 'm_conv_b', 'm_lru_wa', 'm_lru_ba', 'm_lru_wx', 'm_lru_bx', 'm_lru_a_param', 'm_pool_w', 'm_pool_b', 'm_pool_scale', 'm_w_out', 'm_ffn2_norm', 'm_ffn2_w_in', 'm_ffn2_w_out', 'm_final_norm', 'v_meta_tokens', 'v_ffn1_norm', 'v_ffn1_w_in', 'v_ffn1_w_out', 'v_mix_norm', 'v_w_in', 'v_conv_w', 'v_conv_b', 'v_lru_wa', 'v_lru_ba', 'v_lru_wx', 'v_lru_bx', 'v_lru_a_param', 'v_pool_w', 'v_pool_b', 'v_pool_scale', 'v_w_out', 'v_ffn2_norm', 'v_ffn2_w_in', 'v_ffn2_w_out', 'v_final_norm']
TWIN_OUTPUTS = ['loss', 'grad_x', 'grad_meta_tokens', 'grad_ffn1_norm', 'grad_ffn1_w_in', 'grad_ffn1_w_out', 'grad_mix_norm', 'grad_w_in', 'grad_conv_w', 'grad_conv_b', 'grad_lru_wa', 'grad_lru_ba', 'grad_lru_wx', 'grad_lru_bx', 'grad_lru_a_param', 'grad_pool_w', 'grad_pool_b', 'grad_pool_scale', 'grad_w_out', 'grad_ffn2_norm', 'grad_ffn2_w_in', 'grad_ffn2_w_out', 'grad_final_norm', 'delta_meta_tokens', 'delta_ffn1_norm', 'delta_ffn1_w_in', 'delta_ffn1_w_out', 'delta_mix_norm', 'delta_w_in', 'delta_conv_w', 'delta_conv_b', 'delta_lru_wa', 'delta_lru_ba', 'delta_lru_wx', 'delta_lru_bx', 'delta_lru_a_param', 'delta_pool_w', 'delta_pool_b', 'delta_pool_scale', 'delta_w_out', 'delta_ffn2_norm', 'delta_ffn2_w_in', 'delta_ffn2_w_out', 'delta_final_norm', 'new_m_meta_tokens', 'new_m_ffn1_norm', 'new_m_ffn1_w_in', 'new_m_ffn1_w_out', 'new_m_mix_norm', 'new_m_w_in', 'new_m_conv_w', 'new_m_conv_b', 'new_m_lru_wa', 'new_m_lru_ba', 'new_m_lru_wx', 'new_m_lru_bx', 'new_m_lru_a_param', 'new_m_pool_w', 'new_m_pool_b', 'new_m_pool_scale', 'new_m_w_out', 'new_m_ffn2_norm', 'new_m_ffn2_w_in', 'new_m_ffn2_w_out', 'new_m_final_norm', 'new_v_meta_tokens', 'new_v_ffn1_norm', 'new_v_ffn1_w_in', 'new_v_ffn1_w_out', 'new_v_mix_norm', 'new_v_w_in', 'new_v_conv_w', 'new_v_conv_b', 'new_v_lru_wa', 'new_v_lru_ba', 'new_v_lru_wx', 'new_v_lru_bx', 'new_v_lru_a_param', 'new_v_pool_w', 'new_v_pool_b', 'new_v_pool_scale', 'new_v_w_out', 'new_v_ffn2_norm', 'new_v_ffn2_w_in', 'new_v_ffn2_w_out', 'new_v_final_norm']
TWIN_LEAF_KINDS = {'loss': 'loss', 'grad_x': 'grad_x', 'grad_meta_tokens': 'grad_w', 'grad_ffn1_norm': 'grad_w', 'grad_ffn1_w_in': 'grad_w', 'grad_ffn1_w_out': 'grad_w', 'grad_mix_norm': 'grad_w', 'grad_w_in': 'grad_w', 'grad_conv_w': 'grad_w', 'grad_conv_b': 'grad_w', 'grad_lru_wa': 'grad_w', 'grad_lru_ba': 'grad_w', 'grad_lru_wx': 'grad_w', 'grad_lru_bx': 'grad_w', 'grad_lru_a_param': 'grad_w', 'grad_pool_w': 'grad_w', 'grad_pool_b': 'grad_w', 'grad_pool_scale': 'grad_w', 'grad_w_out': 'grad_w', 'grad_ffn2_norm': 'grad_w', 'grad_ffn2_w_in': 'grad_w', 'grad_ffn2_w_out': 'grad_w', 'grad_final_norm': 'grad_w', 'delta_meta_tokens': 'delta_w', 'delta_ffn1_norm': 'delta_w', 'delta_ffn1_w_in': 'delta_w', 'delta_ffn1_w_out': 'delta_w', 'delta_mix_norm': 'delta_w', 'delta_w_in': 'delta_w', 'delta_conv_w': 'delta_w', 'delta_conv_b': 'delta_w', 'delta_lru_wa': 'delta_w', 'delta_lru_ba': 'delta_w', 'delta_lru_wx': 'delta_w', 'delta_lru_bx': 'delta_w', 'delta_lru_a_param': 'delta_w', 'delta_pool_w': 'delta_w', 'delta_pool_b': 'delta_w', 'delta_pool_scale': 'delta_w', 'delta_w_out': 'delta_w', 'delta_ffn2_norm': 'delta_w', 'delta_ffn2_w_in': 'delta_w', 'delta_ffn2_w_out': 'delta_w', 'delta_final_norm': 'delta_w', 'new_m_meta_tokens': 'new_m', 'new_m_ffn1_norm': 'new_m', 'new_m_ffn1_w_in': 'new_m', 'new_m_ffn1_w_out': 'new_m', 'new_m_mix_norm': 'new_m', 'new_m_w_in': 'new_m', 'new_m_conv_w': 'new_m', 'new_m_conv_b': 'new_m', 'new_m_lru_wa': 'new_m', 'new_m_lru_ba': 'new_m', 'new_m_lru_wx': 'new_m', 'new_m_lru_bx': 'new_m', 'new_m_lru_a_param': 'new_m', 'new_m_pool_w': 'new_m', 'new_m_pool_b': 'new_m', 'new_m_pool_scale': 'new_m', 'new_m_w_out': 'new_m', 'new_m_ffn2_norm': 'new_m', 'new_m_ffn2_w_in': 'new_m', 'new_m_ffn2_w_out': 'new_m', 'new_m_final_norm': 'new_m', 'new_v_meta_tokens': 'new_v', 'new_v_ffn1_norm': 'new_v', 'new_v_ffn1_w_in': 'new_v', 'new_v_ffn1_w_out': 'new_v', 'new_v_mix_norm': 'new_v', 'new_v_w_in': 'new_v', 'new_v_conv_w': 'new_v', 'new_v_conv_b': 'new_v', 'new_v_lru_wa': 'new_v', 'new_v_lru_ba': 'new_v', 'new_v_lru_wx': 'new_v', 'new_v_lru_bx': 'new_v', 'new_v_lru_a_param': 'new_v', 'new_v_pool_w': 'new_v', 'new_v_pool_b': 'new_v', 'new_v_pool_scale': 'new_v', 'new_v_w_out': 'new_v', 'new_v_ffn2_norm': 'new_v', 'new_v_ffn2_w_in': 'new_v', 'new_v_ffn2_w_out': 'new_v', 'new_v_final_norm': 'new_v'}


def _forward(args):
    return _fwd_reference(*[args[k] for k in FWD_PARAMS])


def _output_shape():
    out = _jax.eval_shape(lambda: _forward(_fwd_setup_inputs(0)))
    return out.shape, out.dtype

N_MICROBATCH = 1
ADAM_LR = 0.001
ADAM_B1 = 0.9
ADAM_B2 = 0.999
ADAM_EPS = 1e-08
ADAM_WD = 0.01
ADAM_STEP = 10
PER_EXAMPLE_BATCH_AXIS = {'x': 0, 'loss_target': 0}
SHARED_INPUTS = []
_WEIGHT_DTYPES = {'meta_tokens': _jnp.float32, 'ffn1_norm': _jnp.float32, 'ffn1_w_in': _jnp.float32, 'ffn1_w_out': _jnp.float32, 'mix_norm': _jnp.float32, 'w_in': _jnp.float32, 'conv_w': _jnp.float32, 'conv_b': _jnp.float32, 'lru_wa': _jnp.float32, 'lru_ba': _jnp.float32, 'lru_wx': _jnp.float32, 'lru_bx': _jnp.float32, 'lru_a_param': _jnp.float32, 'pool_w': _jnp.float32, 'pool_b': _jnp.float32, 'pool_scale': _jnp.float32, 'w_out': _jnp.float32, 'ffn2_norm': _jnp.float32, 'ffn2_w_in': _jnp.float32, 'ffn2_w_out': _jnp.float32, 'final_norm': _jnp.float32}
MOMENT_SCALE = {'meta_tokens': 1.709567e-03, 'ffn1_norm': 3.696501e-02, 'ffn1_w_in': 1.537972e-02, 'ffn1_w_out': 2.508319e-02, 'mix_norm': 5.278639e-02, 'w_in': 4.401828e-02, 'conv_w': 3.452654e-02, 'conv_b': 3.848902e-01, 'lru_wa': 1.053332e-02, 'lru_ba': 9.181912e-03, 'lru_wx': 1.910262e-02, 'lru_bx': 1.227158e-02, 'lru_a_param': 1.881833e-02, 'pool_w': 6.164023e-02, 'pool_b': 1.234131e-01, 'pool_scale': 6.179225e-02, 'w_out': 4.949603e-02, 'ffn2_norm': 2.908585e-02, 'ffn2_w_in': 1.220511e-02, 'ffn2_w_out': 1.992144e-02, 'final_norm': 1.600578e+01}


def _to_microbatches(a, axis):
    t = _jnp.moveaxis(a, axis, 0)
    t = t.reshape((N_MICROBATCH, t.shape[0] // N_MICROBATCH) + t.shape[1:])
    return _jnp.moveaxis(t, 1, axis + 1)


def setup_inputs(seed: int = 0) -> dict:
    inp = _fwd_setup_inputs(seed)
    key = _jax.random.fold_in(_jax.random.key(seed), 7919)
    shape, _ = _output_shape()
    out = dict(inp)
    out["loss_target"] = _jax.random.normal(_jax.random.fold_in(key, 0), shape, _jnp.float32)
    for i, name in enumerate(TWIN_WEIGHTS):
        w = inp[name].astype(_jnp.float32)
        if MOMENT_SCALE is None:
            s = _jnp.sqrt(_jnp.mean(_jnp.square(w)) + 1e-30)
        else:
            s = MOMENT_SCALE[name]
        km, kv = _jax.random.split(_jax.random.fold_in(key, i + 1))
        out[name] = w
        out["m_" + name] = s * _jax.random.normal(km, w.shape, _jnp.float32)
        out["v_" + name] = (s * s) * _jax.random.uniform(kv, w.shape, _jnp.float32, 0.5, 1.5)
    if N_MICROBATCH > 1:
        for name, axis in PER_EXAMPLE_BATCH_AXIS.items():
            out[name] = _to_microbatches(out[name], axis)
    return {'x': out['x'], 'meta_tokens': out['meta_tokens'], 'ffn1_norm': out['ffn1_norm'], 'ffn1_w_in': out['ffn1_w_in'], 'ffn1_w_out': out['ffn1_w_out'], 'mix_norm': out['mix_norm'], 'w_in': out['w_in'], 'conv_w': out['conv_w'], 'conv_b': out['conv_b'], 'lru_wa': out['lru_wa'], 'lru_ba': out['lru_ba'], 'lru_wx': out['lru_wx'], 'lru_bx': out['lru_bx'], 'lru_a_param': out['lru_a_param'], 'pool_w': out['pool_w'], 'pool_b': out['pool_b'], 'pool_scale': out['pool_scale'], 'w_out': out['w_out'], 'ffn2_norm': out['ffn2_norm'], 'ffn2_w_in': out['ffn2_w_in'], 'ffn2_w_out': out['ffn2_w_out'], 'final_norm': out['final_norm'], 'loss_target': out['loss_target'], 'm_meta_tokens': out['m_meta_tokens'], 'm_ffn1_norm': out['m_ffn1_norm'], 'm_ffn1_w_in': out['m_ffn1_w_in'], 'm_ffn1_w_out': out['m_ffn1_w_out'], 'm_mix_norm': out['m_mix_norm'], 'm_w_in': out['m_w_in'], 'm_conv_w': out['m_conv_w'], 'm_conv_b': out['m_conv_b'], 'm_lru_wa': out['m_lru_wa'], 'm_lru_ba': out['m_lru_ba'], 'm_lru_wx': out['m_lru_wx'], 'm_lru_bx': out['m_lru_bx'], 'm_lru_a_param': out['m_lru_a_param'], 'm_pool_w': out['m_pool_w'], 'm_pool_b': out['m_pool_b'], 'm_pool_scale': out['m_pool_scale'], 'm_w_out': out['m_w_out'], 'm_ffn2_norm': out['m_ffn2_norm'], 'm_ffn2_w_in': out['m_ffn2_w_in'], 'm_ffn2_w_out': out['m_ffn2_w_out'], 'm_final_norm': out['m_final_norm'], 'v_meta_tokens': out['v_meta_tokens'], 'v_ffn1_norm': out['v_ffn1_norm'], 'v_ffn1_w_in': out['v_ffn1_w_in'], 'v_ffn1_w_out': out['v_ffn1_w_out'], 'v_mix_norm': out['v_mix_norm'], 'v_w_in': out['v_w_in'], 'v_conv_w': out['v_conv_w'], 'v_conv_b': out['v_conv_b'], 'v_lru_wa': out['v_lru_wa'], 'v_lru_ba': out['v_lru_ba'], 'v_lru_wx': out['v_lru_wx'], 'v_lru_bx': out['v_lru_bx'], 'v_lru_a_param': out['v_lru_a_param'], 'v_pool_w': out['v_pool_w'], 'v_pool_b': out['v_pool_b'], 'v_pool_scale': out['v_pool_scale'], 'v_w_out': out['v_w_out'], 'v_ffn2_norm': out['v_ffn2_norm'], 'v_ffn2_w_in': out['v_ffn2_w_in'], 'v_ffn2_w_out': out['v_ffn2_w_out'], 'v_final_norm': out['v_final_norm']}


def _loss(weights, diff, rest, loss_target):
    with _jax.named_scope("forward"):
        args = {**rest, TWIN_DIFF_INPUT: diff, **{k: w.astype(_WEIGHT_DTYPES[k]) for k, w in weights.items()}}
        y = _forward(args)
    with _jax.named_scope("loss_head"):
        err = _jnp.square(y.astype(_jnp.float32) - loss_target)
        return 0.5 * _jnp.sum(_jnp.mean(err, axis=-1)) if err.ndim else 0.5 * err


def _adamw(w, g, m, v):
    m = ADAM_B1 * m + (1.0 - ADAM_B1) * g
    v = ADAM_B2 * v + (1.0 - ADAM_B2) * _jnp.square(g)
    m_hat = m / (1.0 - ADAM_B1 ** ADAM_STEP)
    v_hat = v / (1.0 - ADAM_B2 ** ADAM_STEP)
    delta = -ADAM_LR * (m_hat / (_jnp.sqrt(v_hat) + ADAM_EPS) + ADAM_WD * w)
    return delta, m, v


def reference(x, meta_tokens, ffn1_norm, ffn1_w_in, ffn1_w_out, mix_norm, w_in, conv_w, conv_b, lru_wa, lru_ba, lru_wx, lru_bx, lru_a_param, pool_w, pool_b, pool_scale, w_out, ffn2_norm, ffn2_w_in, ffn2_w_out, final_norm, loss_target, m_meta_tokens, m_ffn1_norm, m_ffn1_w_in, m_ffn1_w_out, m_mix_norm, m_w_in, m_conv_w, m_conv_b, m_lru_wa, m_lru_ba, m_lru_wx, m_lru_bx, m_lru_a_param, m_pool_w, m_pool_b, m_pool_scale, m_w_out, m_ffn2_norm, m_ffn2_w_in, m_ffn2_w_out, m_final_norm, v_meta_tokens, v_ffn1_norm, v_ffn1_w_in, v_ffn1_w_out, v_mix_norm, v_w_in, v_conv_w, v_conv_b, v_lru_wa, v_lru_ba, v_lru_wx, v_lru_bx, v_lru_a_param, v_pool_w, v_pool_b, v_pool_scale, v_w_out, v_ffn2_norm, v_ffn2_w_in, v_ffn2_w_out, v_final_norm):
    given = dict(x=x, meta_tokens=meta_tokens, ffn1_norm=ffn1_norm, ffn1_w_in=ffn1_w_in, ffn1_w_out=ffn1_w_out, mix_norm=mix_norm, w_in=w_in, conv_w=conv_w, conv_b=conv_b, lru_wa=lru_wa, lru_ba=lru_ba, lru_wx=lru_wx, lru_bx=lru_bx, lru_a_param=lru_a_param, pool_w=pool_w, pool_b=pool_b, pool_scale=pool_scale, w_out=w_out, ffn2_norm=ffn2_norm, ffn2_w_in=ffn2_w_in, ffn2_w_out=ffn2_w_out, final_norm=final_norm, loss_target=loss_target, m_meta_tokens=m_meta_tokens, m_ffn1_norm=m_ffn1_norm, m_ffn1_w_in=m_ffn1_w_in, m_ffn1_w_out=m_ffn1_w_out, m_mix_norm=m_mix_norm, m_w_in=m_w_in, m_conv_w=m_conv_w, m_conv_b=m_conv_b, m_lru_wa=m_lru_wa, m_lru_ba=m_lru_ba, m_lru_wx=m_lru_wx, m_lru_bx=m_lru_bx, m_lru_a_param=m_lru_a_param, m_pool_w=m_pool_w, m_pool_b=m_pool_b, m_pool_scale=m_pool_scale, m_w_out=m_w_out, m_ffn2_norm=m_ffn2_norm, m_ffn2_w_in=m_ffn2_w_in, m_ffn2_w_out=m_ffn2_w_out, m_final_norm=m_final_norm, v_meta_tokens=v_meta_tokens, v_ffn1_norm=v_ffn1_norm, v_ffn1_w_in=v_ffn1_w_in, v_ffn1_w_out=v_ffn1_w_out, v_mix_norm=v_mix_norm, v_w_in=v_w_in, v_conv_w=v_conv_w, v_conv_b=v_conv_b, v_lru_wa=v_lru_wa, v_lru_ba=v_lru_ba, v_lru_wx=v_lru_wx, v_lru_bx=v_lru_bx, v_lru_a_param=v_lru_a_param, v_pool_w=v_pool_w, v_pool_b=v_pool_b, v_pool_scale=v_pool_scale, v_w_out=v_w_out, v_ffn2_norm=v_ffn2_norm, v_ffn2_w_in=v_ffn2_w_in, v_ffn2_w_out=v_ffn2_w_out, v_final_norm=v_final_norm)
    weights = {n: given[n] for n in TWIN_WEIGHTS}
    shared = {n: given[n] for n in SHARED_INPUTS}
    per_example = {n: given[n] for n in ['x']}
    grad_fn = _jax.value_and_grad(_loss, argnums=(0, 1))

    def one_microbatch(ex, loss_target):
        ex = dict(ex)
        diff = ex.pop(TWIN_DIFF_INPUT)
        return grad_fn(weights, diff, {**shared, **ex}, loss_target)

    if N_MICROBATCH == 1:
        loss, (grad_w, grad_x) = one_microbatch(per_example, given["loss_target"])
    else:
        def body(carry, xs):
            loss_sum, grad_sum = carry
            l_k, (gw_k, gx_k) = one_microbatch(xs[0], xs[1])
            with _jax.named_scope("update"):
                return (loss_sum + l_k, _jax.tree.map(_jnp.add, grad_sum, gw_k)), gx_k

        init = (_jnp.zeros((), _jnp.float32), _jax.tree.map(_jnp.zeros_like, weights))
        (loss, grad_w), grad_x = _jax.lax.scan(body, init, (per_example, given["loss_target"]))
    with _jax.named_scope("update"):
        delta_w, new_m, new_v = {}, {}, {}
        for n in TWIN_WEIGHTS:
            delta_w[n], new_m[n], new_v[n] = _adamw(weights[n], grad_w[n], given["m_" + n], given["v_" + n])
    return (loss, grad_x, *[grad_w[n] for n in TWIN_WEIGHTS], *[delta_w[n] for n in TWIN_WEIGHTS],
            *[new_m[n] for n in TWIN_WEIGHTS], *[new_v[n] for n in TWIN_WEIGHTS])
```

```python
import functools

import jax
import jax.numpy as jnp
from jax import lax
from jax.experimental import pallas as pl
from jax.experimental.pallas import tpu as pltpu

F32 = jnp.float32
BF16 = jnp.bfloat16
MESH = pl.DeviceIdType.MESH

N_DEV = 8
N_CHIP = 4
RMS_EPS = 1e-6
LRU_C = 8.0
CONV_WIDTH = 4
POOL_WINDOWS = (2, 4, 8, 16)
HIST = 16
ADAM_LR = 0.001
ADAM_B1 = 0.9
ADAM_B2 = 0.999
ADAM_EPS = 1e-08
ADAM_WD = 0.01
ADAM_STEP = 10
ROW_ALIGN = 128
V7X_VMEM_LIMIT = 56 * 1024 * 1024
BF16_ROWS = 16
LANES = 128


def _tile(n, cap, mult=BF16_ROWS):
    best = None
    d = mult
    while d <= min(n, cap):
        if n % d == 0:
            best = d
        d += mult
    if best is None:
        raise ValueError(f"no tile for {n} (cap {cap}, multiple of {mult})")
    return best


def _row_mult(dtype):
    return 8 * 4 // jnp.dtype(dtype).itemsize


def _params(sem=None):
    return pltpu.CompilerParams(dimension_semantics=sem, vmem_limit_bytes=V7X_VMEM_LIMIT)


def _dot(a, b, mode):
    dims = {"nn": ((1,), (0,)), "nt": ((1,), (1,)), "tn": ((0,), (0,))}[mode]
    return lax.dot_general(a, b, (dims, ((), ())), preferred_element_type=F32)


def _sigmoid(x):
    return 1.0 / (1.0 + jnp.exp(-x))


_GELU_C = 0.7978845608028654
_GELU_K = 0.044715


def _gelu(x):
    return 0.5 * x * (1.0 + jnp.tanh(_GELU_C * (x + _GELU_K * x * x * x)))


def _gelu_grad(x):
    th = jnp.tanh(_GELU_C * (x + _GELU_K * x * x * x))
    return 0.5 * (1.0 + th) + 0.5 * x * (1.0 - th * th) * _GELU_C * (1.0 + 3.0 * _GELU_K * x * x)


def _neg_expm1(x):
    p = 1.0 / 3628800.0
    for c in (1.0 / 362880.0, 1.0 / 40320.0, 1.0 / 5040.0, 1.0 / 720.0, 1.0 / 120.0, 1.0 / 24.0, 1.0 / 6.0, 0.5, 1.0):
        p = p * x + c
    return jnp.where(x > -0.5, -(x * p), 1.0 - jnp.exp(x))


def _softplus_neg(p):
    e = jnp.exp(-jnp.abs(p))
    u = 1.0 + e
    l1p = jnp.where(u == 1.0, e, jnp.log(u) * e / (u - 1.0 + (u == 1.0).astype(F32)))
    return jnp.maximum(-p, 0.0) + l1p


def _operand_spec(arr, br, bc, ridx, cidx, layer, split):
    if split:
        ncb = arr.shape[2] // bc
        return pl.BlockSpec((None, br, bc), lambda i, j, k: (cidx(i, j, k) // ncb, ridx(i, j, k), cidx(i, j, k) % ncb))
    if layer is not None:
        return pl.BlockSpec((None, br, bc), lambda i, j, k: (layer, ridx(i, j, k), cidx(i, j, k)))
    return pl.BlockSpec((br, bc), lambda i, j, k: (ridx(i, j, k), cidx(i, j, k)))


def _matmul(a, b, *, mode, m, n, kdim, tm, tn, tk, out_dtype, name, scale=None, residual=None,
            a_layer=None, b_layer=None, a_split=False, b_split=False):
    nk = kdim // tk
    I = lambda i, j, k: i
    J = lambda i, j, k: j
    K = lambda i, j, k: k
    if mode == "nn":
        a_spec = _operand_spec(a, tm, tk, I, K, a_layer, a_split)
        b_spec = _operand_spec(b, tk, tn, K, J, b_layer, b_split)
    elif mode == "nt":
        a_spec = _operand_spec(a, tm, tk, I, K, a_layer, a_split)
        b_spec = _operand_spec(b, tn, tk, J, K, b_layer, b_split)
    else:
        a_spec = _operand_spec(a, tk, tm, K, I, a_layer, a_split)
        b_spec = _operand_spec(b, tk, tn, K, J, b_layer, b_split)
    in_specs = [a_spec, b_spec]
    operands = [a, b]
    if residual is not None:
        in_specs.append(pl.BlockSpec((tm, tn), lambda i, j, k: (i, j)))
        operands.append(residual)

    def finish(acc, res_ref, o_ref):
        if scale is not None:
            acc = acc * scale
        if res_ref is not None:
            acc = acc + res_ref[...]
        o_ref[...] = acc.astype(o_ref.dtype)

    def body(*refs):
        a_ref, b_ref = refs[0], refs[1]
        res_ref = refs[2] if residual is not None else None
        o_ref = refs[3] if residual is not None else refs[2]
        part = _dot(a_ref[...], b_ref[...], mode)
        if nk == 1:
            finish(part, res_ref, o_ref)
        else:
            acc_ref = refs[-1]
            k = pl.program_id(2)

            @pl.when(k == 0)
            def _():
                acc_ref[...] = part

            @pl.when(k > 0)
            def _():
                acc_ref[...] += part

            @pl.when(k == nk - 1)
            def _():
                finish(acc_ref[...], res_ref, o_ref)

    return pl.pallas_call(
        body, name=name,
        out_shape=jax.ShapeDtypeStruct((m, n), out_dtype),
        grid=(m // tm, n // tn, nk),
        in_specs=in_specs,
        out_specs=pl.BlockSpec((tm, tn), lambda i, j, k: (i, j)),
        scratch_shapes=[pltpu.VMEM((tm, tn), F32)] if nk > 1 else [],
        compiler_params=_params(("parallel", "parallel", "arbitrary")),
    )(*operands)


def _ffn_in_fwd(n_act, w_in, layer, *, name):
    tp, d = n_act.shape
    f = w_in.shape[2] // 2
    tm = _tile(tp, 2112)
    tn = _tile(f, 256, LANES)
    nj = f // tn

    def body(n_ref, wg_ref, wu_ref, gu_ref, a_ref):
        x = n_ref[...]
        g = _dot(x, wg_ref[...], "nn")
        u = _dot(x, wu_ref[...], "nn")
        gu_ref[0] = g.astype(BF16)
        gu_ref[1] = u.astype(BF16)
        a_ref[...] = (g * _sigmoid(g) * u).astype(BF16)

    return pl.pallas_call(
        body, name=name,
        out_shape=(jax.ShapeDtypeStruct((2, tp, f), BF16), jax.ShapeDtypeStruct((tp, f), BF16)),
        grid=(tp // tm, nj),
        in_specs=[pl.BlockSpec((tm, d), lambda i, j: (i, 0)),
                  pl.BlockSpec((None, d, tn), lambda i, j: (layer, 0, j)),
                  pl.BlockSpec((None, d, tn), lambda i, j: (layer, 0, j + nj))],
        out_specs=(pl.BlockSpec((2, tm, tn), lambda i, j: (0, i, j)), pl.BlockSpec((tm, tn), lambda i, j: (i, j))),
        compiler_params=_params(("parallel", "arbitrary")),
    )(n_act, w_in, w_in)


def _ffn_bwd_act(dhb, w_out, layer, gu, *, name):
    tp, d = dhb.shape
    f = w_out.shape[1]
    tm = _tile(tp, 2112)
    tn = _tile(f, 256, LANES)

    def body(dh_ref, w_ref, gu_ref, dz_ref, a_ref):
        da = 0.5 * _dot(dh_ref[...], w_ref[...], "nt")
        g = gu_ref[0].astype(F32)
        u = gu_ref[1].astype(F32)
        s = _sigmoid(g)
        sg = g * s
        dz_ref[0] = (da * u * (s * (1.0 + g * (1.0 - s)))).astype(BF16)
        dz_ref[1] = (da * sg).astype(BF16)
        a_ref[...] = (sg * u).astype(BF16)

    return pl.pallas_call(
        body, name=name,
        out_shape=(jax.ShapeDtypeStruct((2, tp, f), BF16), jax.ShapeDtypeStruct((tp, f), BF16)),
        grid=(tp // tm, f // tn),
        in_specs=[pl.BlockSpec((tm, d), lambda i, j: (i, 0)),
                  pl.BlockSpec((None, tn, d), lambda i, j: (layer, j, 0)),
                  pl.BlockSpec((2, tm, tn), lambda i, j: (0, i, j))],
        out_specs=(pl.BlockSpec((2, tm, tn), lambda i, j: (0, i, j)), pl.BlockSpec((tm, tn), lambda i, j: (i, j))),
        compiler_params=_params(("parallel", "arbitrary")),
    )(dhb, w_out, gu)


def _rms_fwd(h, g, layer, *, name):
    tp, d = h.shape
    tr = _tile(tp, 528)

    def body(h_ref, g_ref, n_ref):
        x = h_ref[...]
        r = lax.rsqrt(jnp.mean(x * x, axis=-1, keepdims=True) + RMS_EPS)
        n_ref[...] = (x * r * g_ref[...]).astype(BF16)

    return pl.pallas_call(
        body, name=name, out_shape=jax.ShapeDtypeStruct((tp, d), BF16), grid=(tp // tr,),
        in_specs=[pl.BlockSpec((tr, d), lambda i: (i, 0)), pl.BlockSpec((None, 1, d), lambda i: (layer, 0, 0))],
        out_specs=pl.BlockSpec((tr, d), lambda i: (i, 0)),
        compiler_params=_params(("parallel",)),
    )(h, g)


def _rms_bwd(h, g, layer, dn, dres, *, name):
    tp, d = h.shape
    tr = _tile(tp, 528)

    def body(h_ref, g_ref, dn_ref, dres_ref, dh_ref, dhb_ref, dg_ref):
        x = h_ref[...]
        r = lax.rsqrt(jnp.mean(x * x, axis=-1, keepdims=True) + RMS_EPS)
        xhat = x * r
        dn_v = dn_ref[...]
        dxhat = dn_v * g_ref[...]
        dh = dres_ref[...] + r * (dxhat - xhat * jnp.mean(dxhat * xhat, axis=-1, keepdims=True))
        dh_ref[...] = dh
        dhb_ref[...] = dh.astype(BF16)
        part = jnp.sum(dn_v * xhat, axis=0, keepdims=True)

        @pl.when(pl.program_id(0) == 0)
        def _():
            dg_ref[...] = part

        @pl.when(pl.program_id(0) > 0)
        def _():
            dg_ref[...] += part

    return pl.pallas_call(
        body, name=name,
        out_shape=(jax.ShapeDtypeStruct((tp, d), F32), jax.ShapeDtypeStruct((tp, d), BF16), jax.ShapeDtypeStruct((1, d), F32)),
        grid=(tp // tr,),
        in_specs=[pl.BlockSpec((tr, d), lambda i: (i, 0)), pl.BlockSpec((None, 1, d), lambda i: (layer, 0, 0)),
                  pl.BlockSpec((tr, d), lambda i: (i, 0)), pl.BlockSpec((tr, d), lambda i: (i, 0))],
        out_specs=(pl.BlockSpec((tr, d), lambda i: (i, 0)), pl.BlockSpec((tr, d), lambda i: (i, 0)),
                   pl.BlockSpec((1, d), lambda i: (0, 0))),
        compiler_params=_params(("arbitrary",)),
    )(h, g, dn, dres)


def _final_loss(h, g, tgt, n_meta, t_valid, *, name):
    tp, d = h.shape
    tr = _tile(tp, 528)

    def body(h_ref, g_ref, t_ref, dh_ref, dhb_ref, dg_ref, loss_ref):
        i = pl.program_id(0)
        x = h_ref[...]
        r = lax.rsqrt(jnp.mean(x * x, axis=-1, keepdims=True) + RMS_EPS)
        xhat = x * r
        gv = g_ref[...]
        row = i * tr + lax.broadcasted_iota(jnp.int32, (tr, 1), 0)
        valid = jnp.logical_and(row >= n_meta, row < t_valid)
        err = jnp.where(valid, xhat * gv - t_ref[...], 0.0)
        dy = err * (1.0 / d)
        dxhat = dy * gv
        dh = r * (dxhat - xhat * jnp.mean(dxhat * xhat, axis=-1, keepdims=True))
        dh_ref[...] = dh
        dhb_ref[...] = dh.astype(BF16)
        dg_part = jnp.sum(dy * xhat, axis=0, keepdims=True)
        loss_part = jnp.sum(jnp.sum(err * err, axis=1, keepdims=True), axis=0, keepdims=True) * (0.5 / d)

        @pl.when(i == 0)
        def _():
            dg_ref[...] = dg_part
            loss_ref[...] = loss_part

        @pl.when(i > 0)
        def _():
            dg_ref[...] += dg_part
            loss_ref[...] += loss_part

    return pl.pallas_call(
        body, name=name,
        out_shape=(jax.ShapeDtypeStruct((tp, d), F32), jax.ShapeDtypeStruct((tp, d), BF16),
                   jax.ShapeDtypeStruct((1, d), F32), jax.ShapeDtypeStruct((1, 1), F32)),
        grid=(tp // tr,),
        in_specs=[pl.BlockSpec((tr, d), lambda i: (i, 0)), pl.BlockSpec((1, d), lambda i: (0, 0)),
                  pl.BlockSpec((tr, d), lambda i: (i, 0))],
        out_specs=(pl.BlockSpec((tr, d), lambda i: (i, 0)), pl.BlockSpec((tr, d), lambda i: (i, 0)),
                   pl.BlockSpec((1, d), lambda i: (0, 0)), pl.BlockSpec((1, 1), lambda i: (0, 0))),
        compiler_params=_params(("arbitrary",)),
    )(h, g, tgt)


def _mix_param_specs(mp, layer, imap):
    def vec(a):
        return pl.BlockSpec((None,) + a.shape[1:], lambda *g: (layer,) + (0,) * (a.ndim - 1))
    return [vec(mp[k]) for k in ("conv_w", "conv_b", "wa", "ba", "wx", "bx", "a_param", "pool_w", "pool_b", "pool_scale")]


def _mix_param_list(mp):
    return [mp[k] for k in ("conv_w", "conv_b", "wa", "ba", "wx", "bx", "a_param", "pool_w", "pool_b", "pool_scale")]


def _lru_gates(xc, wa_h, ba_h, wx_h, bx_h, sp_h):
    xb = xc.astype(BF16)
    ra = _sigmoid(_dot(xb, wa_h, "nn") + ba_h)
    ii = _sigmoid(_dot(xb, wx_h, "nn") + bx_h)
    la = -LRU_C * ra * sp_h
    a = jnp.exp(la)
    mult = jnp.sqrt(_neg_expm1(2.0 * la))
    return xb, ra, ii, a, mult


def _conv_fwd(xbuf, cw_ref, cb_ref, sl, tc):
    xc = cb_ref[:, sl]
    for k in range(CONV_WIDTH):
        xc = xc + cw_ref[k:k + 1, sl] * xbuf[pl.ds(HIST - (CONV_WIDTH - 1) + k, tc), sl]
    return xc


def _pool_delta(pbuf, cols, win, t0, tc):
    u = pbuf[pl.ds(HIST, tc), cols]
    s = u
    for k in range(1, win):
        s = s + pbuf[pl.ds(HIST - k, tc), cols]
    t = t0 + lax.broadcasted_iota(jnp.int32, (tc, 1), 0)
    cnt = jnp.minimum(t + 1, win).astype(F32)
    return s / cnt - u, cnt


def _mix_fwd(z, mp, layer, *, name):
    tp = z.shape[0]
    dl = z.shape[1] // 3
    n_heads, hd = mp["wa"].shape[1], mp["wa"].shape[2]
    n_groups, gd = mp["pool_w"].shape[1], mp["pool_w"].shape[2]
    tc = _tile(tp, 384)

    def body(z_ref, cw_ref, cb_ref, wa_ref, ba_ref, wx_ref, bx_ref, ap_ref, pw_ref, pb_ref, ps_ref,
             m_ref, hs_ref, xbuf, pbuf, a_s, b_s, hcar):
        i = pl.program_id(0)

        @pl.when(i == 0)
        def _():
            xbuf[pl.ds(0, HIST), :] = jnp.zeros((HIST, dl), F32)
            pbuf[pl.ds(0, HIST), :] = jnp.zeros((HIST, dl), F32)
            hcar[...] = jnp.zeros((1, dl), F32)

        @pl.when(i > 0)
        def _():
            xbuf[pl.ds(0, HIST), :] = xbuf[pl.ds(tc, HIST), :]
            pbuf[pl.ds(0, HIST), :] = pbuf[pl.ds(tc, HIST), :]

        xbuf[pl.ds(HIST, tc), :] = z_ref[:, pl.ds(0, dl)]
        pbuf[pl.ds(HIST, tc), :] = z_ref[:, pl.ds(2 * dl, dl)]
        sp = _softplus_neg(ap_ref[...])
        for h in range(n_heads):
            sl = pl.ds(h * hd, hd)
            xc = _conv_fwd(xbuf, cw_ref, cb_ref, sl, tc)
            _, _, ii, a, mult = _lru_gates(xc, wa_ref[h], ba_ref[:, sl], wx_ref[h], bx_ref[:, sl], sp[:, h * hd:(h + 1) * hd])
            a_s[:, sl] = a
            b_s[:, sl] = mult * ii * xc

        def step(t, hprev):
            hnew = a_s[pl.ds(t, 1), :] * hprev + b_s[pl.ds(t, 1), :]
            hs_ref[pl.ds(t, 1), :] = hnew
            return hnew

        hcar[...] = lax.fori_loop(0, tc, step, hcar[...])
        for h in range(n_heads):
            sl = pl.ds(h * hd, hd)
            m_ref[:, sl] = (hs_ref[:, sl] * _gelu(z_ref[:, pl.ds(dl + h * hd, hd)])).astype(BF16)
        for g in range(n_groups):
            cols = pl.ds(g * gd, gd)
            dlt, _ = _pool_delta(pbuf, cols, POOL_WINDOWS[g], i * tc, tc)
            q = _dot(dlt.astype(BF16), pw_ref[g], "nn") + pb_ref[:, cols]
            m_ref[:, pl.ds(dl + g * gd, gd)] = (q * ps_ref[:, cols]).astype(BF16)

    return pl.pallas_call(
        body, name=name,
        out_shape=(jax.ShapeDtypeStruct((tp, 2 * dl), BF16), jax.ShapeDtypeStruct((tp, dl), F32)),
        grid=(tp // tc,),
        in_specs=[pl.BlockSpec((tc, 3 * dl), lambda i: (i, 0))] + _mix_param_specs(mp, layer, None),
        out_specs=(pl.BlockSpec((tc, 2 * dl), lambda i: (i, 0)), pl.BlockSpec((tc, dl), lambda i: (i, 0))),
        scratch_shapes=[pltpu.VMEM((HIST + tc, dl), F32), pltpu.VMEM((HIST + tc, dl), F32),
                        pltpu.VMEM((tc, dl), F32), pltpu.VMEM((tc, dl), F32), pltpu.VMEM((1, dl), F32)],
        compiler_params=_params(("arbitrary",)),
    )(z, *_mix_param_list(mp))


_MIX_GRADS = ("conv_w", "conv_b", "wa", "ba", "wx", "bx", "a_param", "pool_w", "pool_b", "pool_scale")


def _mix_bwd(z, hs, dm, mp, layer, *, name):
    tp = z.shape[0]
    dl = z.shape[1] // 3
    n_heads, hd = mp["wa"].shape[1], mp["wa"].shape[2]
    n_groups, gd = mp["pool_w"].shape[1], mp["pool_w"].shape[2]
    tc = _tile(tp, 384)
    nc = tp // tc
    per = tc // HIST

    def body(z_ref, zp_ref, hs_ref, hsp_ref, dm_ref, cw_ref, cb_ref, wa_ref, ba_ref, wx_ref, bx_ref, ap_ref, pw_ref, pb_ref,
             ps_ref, dz_ref, dcw_ref, dcb_ref, dwa_ref, dba_ref, dwx_ref, dbx_ref, dap_ref, dpw_ref, dpb_ref, dps_ref,
             xbuf, pbuf, hbuf, dxbuf, ddbuf, a_s, lam_s, ra_s, ii_s, xc_s, ccar):
        i = pl.program_id(0)
        ci = nc - 1 - i

        @pl.when(i == 0)
        def _():
            dxbuf[pl.ds(tc, HIST), :] = jnp.zeros((HIST, dl), F32)
            ddbuf[pl.ds(tc, HIST), :] = jnp.zeros((HIST, dl), F32)
            ccar[...] = jnp.zeros((1, dl), F32)
            for ref in (dcw_ref, dcb_ref, dwa_ref, dba_ref, dwx_ref, dbx_ref, dap_ref, dpw_ref, dpb_ref, dps_ref):
                ref[...] = jnp.zeros(ref.shape, F32)

        @pl.when(ci == 0)
        def _():
            xbuf[pl.ds(0, HIST), :] = jnp.zeros((HIST, dl), F32)
            pbuf[pl.ds(0, HIST), :] = jnp.zeros((HIST, dl), F32)
            hbuf[pl.ds(0, HIST), :] = jnp.zeros((HIST, dl), F32)

        @pl.when(ci > 0)
        def _():
            xbuf[pl.ds(0, HIST), :] = zp_ref[:, pl.ds(0, dl)]
            pbuf[pl.ds(0, HIST), :] = zp_ref[:, pl.ds(2 * dl, dl)]
            hbuf[pl.ds(0, HIST), :] = hsp_ref[...]

        xbuf[pl.ds(HIST, tc), :] = z_ref[:, pl.ds(0, dl)]
        pbuf[pl.ds(HIST, tc), :] = z_ref[:, pl.ds(2 * dl, dl)]
        hbuf[pl.ds(HIST, tc), :] = hs_ref[...]
        sp = _softplus_neg(ap_ref[...])

        for h in range(n_heads):
            sl = pl.ds(h * hd, hd)
            xc = _conv_fwd(xbuf, cw_ref, cb_ref, sl, tc)
            _, ra, ii, a, _ = _lru_gates(xc, wa_ref[h], ba_ref[:, sl], wx_ref[h], bx_ref[:, sl], sp[:, h * hd:(h + 1) * hd])
            a_s[:, sl] = a
            ra_s[:, sl] = ra
            ii_s[:, sl] = ii
            xc_s[:, sl] = xc
            lam_s[:, sl] = dm_ref[:, sl] * _gelu(z_ref[:, pl.ds(dl + h * hd, hd)])

        def step(r, carry):
            t = tc - 1 - r
            lam = lam_s[pl.ds(t, 1), :] + carry
            lam_s[pl.ds(t, 1), :] = lam
            return a_s[pl.ds(t, 1), :] * lam

        ccar[...] = lax.fori_loop(0, tc, step, ccar[...])

        for h in range(n_heads):
            sl = pl.ds(h * hd, hd)
            sp_h = sp[:, h * hd:(h + 1) * hd]
            lam = lam_s[:, sl]
            a = a_s[:, sl]
            ra = ra_s[:, sl]
            ii = ii_s[:, sl]
            xc = xc_s[:, sl]
            la = -LRU_C * ra * sp_h
            mult = jnp.sqrt(_neg_expm1(2.0 * la))
            hprev = hbuf[pl.ds(HIST - 1, tc), sl]
            dmult = lam * ii * xc
            dla = lam * hprev * a - dmult * (a * a) / mult
            dap_ref[:, sl] += jnp.sum(dla * (-LRU_C) * ra, axis=0, keepdims=True)
            dpa = dla * (-LRU_C) * sp_h * ra * (1.0 - ra)
            dpx = lam * mult * xc * ii * (1.0 - ii)
            dba_ref[:, sl] += jnp.sum(dpa, axis=0, keepdims=True)
            dbx_ref[:, sl] += jnp.sum(dpx, axis=0, keepdims=True)
            xb = xc.astype(BF16)
            dpa_b = dpa.astype(BF16)
            dpx_b = dpx.astype(BF16)
            dwa_ref[h] += _dot(xb, dpa_b, "tn")
            dwx_ref[h] += _dot(xb, dpx_b, "tn")
            dxc = lam * mult * ii + _dot(dpa_b, wa_ref[h], "nt") + _dot(dpx_b, wx_ref[h], "nt")
            dxbuf[pl.ds(0, tc), sl] = dxc
            dcb_ref[:, sl] += jnp.sum(dxc, axis=0, keepdims=True)
            dzx = jnp.zeros((tc, hd), F32)
            for k in range(CONV_WIDTH):
                dcw_ref[k:k + 1, sl] += jnp.sum(dxc * xbuf[pl.ds(HIST - (CONV_WIDTH - 1) + k, tc), sl], axis=0, keepdims=True)
                dzx = dzx + cw_ref[k:k + 1, sl] * dxbuf[pl.ds(CONV_WIDTH - 1 - k, tc), sl]
            dz_ref[:, sl] = dzx.astype(BF16)
            dxbuf[pl.ds(tc, HIST), sl] = dxbuf[pl.ds(0, HIST), sl]
            zg = z_ref[:, pl.ds(dl + h * hd, hd)]
            dz_ref[:, pl.ds(dl + h * hd, hd)] = (dm_ref[:, sl] * hs_ref[:, sl] * _gelu_grad(zg)).astype(BF16)

        for g in range(n_groups):
            cols = pl.ds(g * gd, gd)
            win = POOL_WINDOWS[g]
            dlt, cnt = _pool_delta(pbuf, cols, win, ci * tc, tc)
            db = dlt.astype(BF16)
            q = _dot(db, pw_ref[g], "nn") + pb_ref[:, cols]
            dyb = dm_ref[:, pl.ds(dl + g * gd, gd)]
            dps_ref[:, cols] += jnp.sum(dyb * q, axis=0, keepdims=True)
            dq = dyb * ps_ref[:, cols]
            dpb_ref[:, cols] += jnp.sum(dq, axis=0, keepdims=True)
            dqb = dq.astype(BF16)
            dpw_ref[g] += _dot(db, dqb, "tn")
            dd = _dot(dqb, pw_ref[g], "nt")
            ddbuf[pl.ds(0, tc), cols] = dd / cnt
            dzp = -dd
            for k in range(win):
                dzp = dzp + ddbuf[pl.ds(k, tc), cols]
            dz_ref[:, pl.ds(2 * dl + g * gd, gd)] = dzp.astype(BF16)
            ddbuf[pl.ds(tc, HIST), cols] = ddbuf[pl.ds(0, HIST), cols]

        @pl.when(i == nc - 1)
        def _():
            dap_ref[...] = dap_ref[...] * (-_sigmoid(-ap_ref[...]))

    rev = lambda i: (nc - 1 - i, 0)
    prev = lambda i: (jnp.maximum((nc - 1 - i) * per - 1, 0), 0)
    const = lambda a: pl.BlockSpec(a.shape[1:], lambda i: (0,) * (a.ndim - 1))
    plist = _mix_param_list(mp)
    grad_shapes = [jax.ShapeDtypeStruct(a.shape[1:], F32) for a in plist]
    buf = lambda rows: pltpu.VMEM((rows, dl), F32)
    outs = pl.pallas_call(
        body, name=name,
        out_shape=[jax.ShapeDtypeStruct((tp, 3 * dl), BF16)] + grad_shapes,
        grid=(nc,),
        in_specs=[pl.BlockSpec((tc, 3 * dl), rev), pl.BlockSpec((HIST, 3 * dl), prev),
                  pl.BlockSpec((tc, dl), rev), pl.BlockSpec((HIST, dl), prev),
                  pl.BlockSpec((tc, 2 * dl), rev)] + _mix_param_specs(mp, layer, None),
        out_specs=[pl.BlockSpec((tc, 3 * dl), rev)] + [const(a) for a in plist],
        scratch_shapes=[buf(HIST + tc), buf(HIST + tc), buf(HIST + tc), buf(tc + HIST), buf(tc + HIST),
                        buf(tc), buf(tc), buf(tc), buf(tc), buf(tc), buf(1)],
        compiler_params=_params(("arbitrary",)),
    )(z, z, hs, hs, dm, *plist)
    return outs[0], dict(zip(_MIX_GRADS, outs[1:]))


def _ffn_fwd(h, norm, w_in, w_out, layer, tag):
    tp, d = h.shape
    f = w_out.shape[1]
    n_act = _rms_fwd(h, norm, layer, name=f"{tag}_rms")
    gu, act = _ffn_in_fwd(n_act, w_in, layer, name=f"{tag}_in")
    h_out = _matmul(act, w_out, mode="nn", m=tp, n=d, kdim=f, tm=_tile(tp, 1056), tn=_tile(d, 1024, LANES),
                    tk=_tile(f, 512, LANES), out_dtype=F32, scale=0.5, residual=h, b_layer=layer, name=f"{tag}_out")
    return h_out, (h, n_act, gu)


def _ffn_bwd(dh, dhb, saved, norm, w_in, w_out, layer, tag):
    h, n_act, gu = saved
    tp, d = h.shape
    f = w_out.shape[1]
    dz, act = _ffn_bwd_act(dhb, w_out, layer, gu, name=f"{tag}_bact")
    g_w_out = _matmul(act, dhb, mode="tn", m=f, n=d, kdim=tp, tm=_tile(f, 512, LANES), tn=_tile(d, 1024, LANES), tk=tp,
                      out_dtype=BF16, scale=0.5, name=f"{tag}_dwout")
    g_w_in = _matmul(n_act, dz, mode="tn", m=d, n=2 * f, kdim=tp, tm=_tile(d, 512, LANES), tn=_tile(f, 512, LANES), tk=tp,
                     out_dtype=BF16, b_split=True, name=f"{tag}_dwin")
    dn = _matmul(dz, w_in, mode="nt", m=tp, n=d, kdim=2 * f, tm=_tile(tp, 1056), tn=d, tk=_tile(f, 512, LANES),
                 out_dtype=F32, a_split=True, b_layer=layer, name=f"{tag}_dn")
    dh_in, dhb_in, g_norm = _rms_bwd(h, norm, layer, dn, dh, name=f"{tag}_brms")
    return dh_in, dhb_in, g_norm, g_w_in, g_w_out


def _mix_block_fwd(h, norm, w_in, w_out, mp, layer, tag):
    tp, d = h.shape
    d_in = w_in.shape[2]
    n_act = _rms_fwd(h, norm, layer, name=f"{tag}_rms")
    z = _matmul(n_act, w_in, mode="nn", m=tp, n=d_in, kdim=d, tm=_tile(tp, 2112), tn=_tile(d_in, 512, LANES), tk=d,
                out_dtype=F32, b_layer=layer, name=f"{tag}_in")
    m_act, hs = _mix_fwd(z, mp, layer, name=f"{tag}_mix")
    h_out = _matmul(m_act, w_out, mode="nn", m=tp, n=d, kdim=d, tm=_tile(tp, 2112), tn=_tile(d, 512, LANES), tk=d,
                    out_dtype=F32, residual=h, b_layer=layer, name=f"{tag}_out")
    return h_out, (h, n_act, z, hs, m_act)


def _mix_block_bwd(dh, dhb, saved, norm, w_in, w_out, mp, layer, tag):
    h, n_act, z, hs, m_act = saved
    tp, d = h.shape
    d_in = w_in.shape[2]
    dm = _matmul(dhb, w_out, mode="nt", m=tp, n=d, kdim=d, tm=_tile(tp, 2112), tn=_tile(d, 512, LANES), tk=d,
                 out_dtype=F32, b_layer=layer, name=f"{tag}_dm")
    g_w_out = _matmul(m_act, dhb, mode="tn", m=d, n=d, kdim=tp, tm=_tile(d, 512, LANES), tn=_tile(d, 1024, LANES), tk=tp,
                      out_dtype=BF16, name=f"{tag}_dwout")
    dz, g_mix = _mix_bwd(z, hs, dm, mp, layer, name=f"{tag}_bmix")
    g_w_in = _matmul(n_act, dz, mode="tn", m=d, n=d_in, kdim=tp, tm=_tile(d, 512, LANES), tn=_tile(d_in, 512, LANES), tk=tp,
                     out_dtype=BF16, name=f"{tag}_dwin")
    dn = _matmul(dz, w_in, mode="nt", m=tp, n=d, kdim=d_in, tm=_tile(tp, 1056), tn=d, tk=_tile(d_in, 512, LANES),
                 out_dtype=F32, b_layer=layer, name=f"{tag}_dn")
    dh_in, dhb_in, g_norm = _rms_bwd(h, norm, layer, dn, dh, name=f"{tag}_brms")
    return dh_in, dhb_in, g_norm, g_w_in, g_w_out, g_mix


def _local_step(h0, tgt, gw, n_meta, t_valid):
    n_layers = gw["ffn1_w_in"].shape[0]
    h = h0
    saved = []
    for l in range(n_layers):
        h, s1 = _ffn_fwd(h, gw["ffn1_norm"], gw["ffn1_w_in"], gw["ffn1_w_out"], l, f"l{l}f1")
        h, s2 = _mix_block_fwd(h, gw["mix_norm"], gw["w_in"], gw["w_out"], gw["mix"], l, f"l{l}mx")
        h, s3 = _ffn_fwd(h, gw["ffn2_norm"], gw["ffn2_w_in"], gw["ffn2_w_out"], l, f"l{l}f2")
        saved.append((s1, s2, s3))
    dh, dhb, g_final, loss = _final_loss(h, gw["final_norm"], tgt, n_meta, t_valid, name="final_loss")
    grads = {k: [None] * n_layers for k in
             ("ffn1_norm", "ffn1_w_in", "ffn1_w_out", "mix_norm", "w_in", "w_out", "ffn2_norm", "ffn2_w_in", "ffn2_w_out", "mix")}
    for l in reversed(range(n_layers)):
        s1, s2, s3 = saved[l]
        dh, dhb, gn, gwi, gwo = _ffn_bwd(dh, dhb, s3, gw["ffn2_norm"], gw["ffn2_w_in"], gw["ffn2_w_out"], l, f"l{l}f2")
        grads["ffn2_norm"][l], grads["ffn2_w_in"][l], grads["ffn2_w_out"][l] = gn, gwi, gwo
        dh, dhb, gn, gwi, gwo, gmix = _mix_block_bwd(dh, dhb, s2, gw["mix_norm"], gw["w_in"], gw["w_out"], gw["mix"], l, f"l{l}mx")
        grads["mix_norm"][l], grads["w_in"][l], grads["w_out"][l], grads["mix"][l] = gn, gwi, gwo, gmix
        dh, dhb, gn, gwi, gwo = _ffn_bwd(dh, dhb, s1, gw["ffn1_norm"], gw["ffn1_w_in"], gw["ffn1_w_out"], l, f"l{l}f1")
        grads["ffn1_norm"][l], grads["ffn1_w_in"][l], grads["ffn1_w_out"][l] = gn, gwi, gwo
    return loss, dh, g_final, grads


def _mesh_pos():
    x, y, c = lax.axis_index("x"), lax.axis_index("y"), lax.axis_index("c")
    chips = [(1 - x, y), (x, 1 - y), (1 - x, 1 - y)]
    return x, y, c, chips


def _block(ref, axis, j, size):
    idx = [slice(None)] * len(ref.shape)
    idx[axis] = pl.ds(j * size, size)
    return ref.at[tuple(idx)]


def _all_gather(shards, axes, *, name):
    n = len(shards)
    sizes = [s.shape[ax] for s, ax in zip(shards, axes)]
    full_shapes = [s.shape[:ax] + (N_DEV * s.shape[ax],) + s.shape[ax + 1:] for s, ax in zip(shards, axes)]

    def body(*refs):
        srcs, fulls = refs[:n], refs[n:2 * n]
        send_sems, recv_sems, local_sems = refs[2 * n:]
        x, y, c, chips = _mesh_pos()
        me, sib = (x, y, c), (x, y, 1 - c)

        def blk(a, dev):
            return _block(fulls[a], axes[a], 4 * dev[0] + 2 * dev[1] + dev[2], sizes[a])

        def copy(a, k, block_dev, to, src=None):
            return pltpu.make_async_remote_copy(
                src_ref=blk(a, block_dev) if src is None else src, dst_ref=blk(a, block_dev),
                send_sem=send_sems.at[a, k], recv_sem=recv_sems.at[a, k], device_id=to, device_id_type=MESH)

        mine = [pltpu.make_async_copy(srcs[a], blk(a, me), local_sems.at[a]) for a in range(n)]
        for cp in mine:
            cp.start()
        first = []
        for a in range(n):
            first.append(copy(a, 0, me, sib, src=srcs[a]))
            first += [copy(a, 1 + j, me, (*chip, c), src=srcs[a]) for j, chip in enumerate(chips)]
        for cp in first:
            cp.start()
        passed = []
        for j, chip in enumerate(chips):
            for a in range(n):
                copy(a, 1 + j, (*chip, c), me).wait_recv()
                fwd = copy(a, 4 + j, (*chip, c), sib)
                fwd.start()
                passed.append(fwd)
        for a in range(n):
            copy(a, 0, sib, me).wait_recv()
            for j, chip in enumerate(chips):
                copy(a, 4 + j, (*chip, 1 - c), me).wait_recv()
        for cp in first + passed:
            cp.wait_send()
        for cp in mine:
            cp.wait()

    any_spec = pl.BlockSpec(memory_space=pl.ANY)
    return pl.pallas_call(
        body, name=name,
        out_shape=[jax.ShapeDtypeStruct(fs, s.dtype) for fs, s in zip(full_shapes, shards)],
        in_specs=[any_spec] * n, out_specs=[any_spec] * n,
        scratch_shapes=[pltpu.SemaphoreType.DMA((n, 7)), pltpu.SemaphoreType.DMA((n, 7)), pltpu.SemaphoreType.DMA((n,))],
    )(*shards)


def _rs_exchange_sibling(tensors, axes, *, name):
    n = len(tensors)
    n_layers = [len(t) for t in tensors]
    sizes = [t[0].shape[ax] // N_DEV for t, ax in zip(tensors, axes)]
    blk_shapes = [t[0].shape[:ax] + (sz,) + t[0].shape[ax + 1:] for t, ax, sz in zip(tensors, axes, sizes)]
    flat = [g for t in tensors for g in t]
    offs = [sum(n_layers[:a]) for a in range(n)]

    def body(*refs):
        srcs = refs[:len(flat)]
        lands = refs[len(flat):len(flat) + n]
        send_sems, recv_sems = refs[len(flat) + n:]
        x, y, c, _ = _mesh_pos()
        sib = (x, y, 1 - c)
        for a in range(n):
            for l in range(n_layers[a]):
                for i in range(N_CHIP):
                    pltpu.make_async_remote_copy(
                        src_ref=_block(srcs[offs[a] + l], axes[a], 2 * i + (1 - c), sizes[a]), dst_ref=lands[a].at[i, l],
                        send_sem=send_sems.at[a], recv_sem=recv_sems.at[a], device_id=sib, device_id_type=MESH).start()
        for a in range(n):
            pltpu.make_async_remote_copy(src_ref=lands[a], dst_ref=lands[a], send_sem=send_sems.at[a], recv_sem=recv_sems.at[a],
                                         device_id=sib, device_id_type=MESH).wait()

    any_spec = pl.BlockSpec(memory_space=pl.ANY)
    return pl.pallas_call(
        body, name=name,
        out_shape=[jax.ShapeDtypeStruct((N_CHIP, nl) + bs, t[0].dtype) for nl, bs, t in zip(n_layers, blk_shapes, tensors)],
        in_specs=[any_spec] * len(flat), out_specs=[any_spec] * n,
        scratch_shapes=[pltpu.SemaphoreType.DMA((n,)), pltpu.SemaphoreType.DMA((n,))],
    )(*flat)


def _rs_add_sibling(layers, axis, land, my_c, *, name):
    n_layers = len(layers)
    shape = layers[0].shape
    size = shape[axis] // N_DEV
    blk_shape = shape[:axis] + (size,) + shape[axis + 1:]
    nd = len(shape)
    rows = blk_shape[-2]
    tr = _tile(rows, 512, _row_mult(layers[0].dtype))
    inner = (tr, blk_shape[-1])
    lead = blk_shape[:-2]
    if lead:
        raise ValueError("blocked gradients are 2-D")
    nrb = rows // tr

    def src_map(l):
        def imap(i, r, c_ref):
            j = 2 * i + c_ref[0]
            return (j * nrb + r, 0) if axis == 0 else (r, j)
        return imap

    def body(c_ref, *refs):
        srcs = refs[:n_layers]
        land_ref = refs[n_layers]
        out_ref = refs[n_layers + 1]
        for l in range(n_layers):
            out_ref[l] = (srcs[l][...].astype(F32) + land_ref[l].astype(F32)).astype(out_ref.dtype)

    grid_spec = pltpu.PrefetchScalarGridSpec(
        num_scalar_prefetch=1, grid=(N_CHIP, nrb),
        in_specs=[pl.BlockSpec(inner, src_map(l)) for l in range(n_layers)]
        + [pl.BlockSpec((None, n_layers) + inner, lambda i, r, c_ref: (i, 0, r, 0))],
        out_specs=pl.BlockSpec((None, n_layers) + inner, lambda i, r, c_ref: (i, 0, r, 0)),
    )
    return pl.pallas_call(
        body, name=name, grid_spec=grid_spec,
        out_shape=jax.ShapeDtypeStruct((N_CHIP, n_layers) + blk_shape, layers[0].dtype),
        compiler_params=_params(("arbitrary", "arbitrary")),
    )(my_c, *layers, land)


def _rs_exchange_chips(parts, *, name):
    n = len(parts)

    def body(*refs):
        srcs, lands = refs[:n], refs[n:2 * n]
        send_sems, recv_sems, local_sems = refs[2 * n:]
        x, y, c, chips = _mesh_pos()
        mine = 2 * x + y
        local = [pltpu.make_async_copy(srcs[a].at[mine], lands[a].at[mine], local_sems.at[a]) for a in range(n)]
        for cp in local:
            cp.start()

        def copy(a, j, chip):
            return pltpu.make_async_remote_copy(
                src_ref=srcs[a].at[2 * chip[0] + chip[1]], dst_ref=lands[a].at[mine],
                send_sem=send_sems.at[a, j], recv_sem=recv_sems.at[a, j], device_id=(*chip, c), device_id_type=MESH)

        sends = [copy(a, j, chip) for a in range(n) for j, chip in enumerate(chips)]
        for cp in sends:
            cp.start()
        for a in range(n):
            for j, chip in enumerate(chips):
                pltpu.make_async_remote_copy(
                    src_ref=srcs[a].at[mine], dst_ref=lands[a].at[2 * chip[0] + chip[1]],
                    send_sem=send_sems.at[a, j], recv_sem=recv_sems.at[a, j], device_id=(*chip, c), device_id_type=MESH).wait_recv()
        for cp in sends:
            cp.wait_send()
        for cp in local:
            cp.wait()

    any_spec = pl.BlockSpec(memory_space=pl.ANY)
    return pl.pallas_call(
        body, name=name,
        out_shape=[jax.ShapeDtypeStruct(p.shape, p.dtype) for p in parts],
        in_specs=[any_spec] * n, out_specs=[any_spec] * n,
        scratch_shapes=[pltpu.SemaphoreType.DMA((n, 3)), pltpu.SemaphoreType.DMA((n, 3)), pltpu.SemaphoreType.DMA((n,))],
    )(*parts)


def _adam_math(w, g, m, v):
    m_new = ADAM_B1 * m + (1.0 - ADAM_B1) * g
    v_new = ADAM_B2 * v + (1.0 - ADAM_B2) * (g * g)
    m_hat = m_new / (1.0 - ADAM_B1 ** ADAM_STEP)
    v_hat = v_new / (1.0 - ADAM_B2 ** ADAM_STEP)
    delta = -ADAM_LR * (m_hat / (jnp.sqrt(v_hat) + ADAM_EPS) + ADAM_WD * w)
    return delta, m_new, v_new


def _sum_adam(land, w, m, v, *, name):
    _, n_layers, rows, cols = land.shape
    tr = _tile(rows, 256, _row_mult(land.dtype))

    def body(land_ref, w_ref, m_ref, v_ref, g_ref, d_ref, mo_ref, vo_ref):
        g = land_ref[0].astype(F32)
        for i in range(1, N_CHIP):
            g = g + land_ref[i].astype(F32)
        delta, m_new, v_new = _adam_math(w_ref[...], g, m_ref[...], v_ref[...])
        g_ref[...] = g
        d_ref[...] = delta
        mo_ref[...] = m_new
        vo_ref[...] = v_new

    blk = pl.BlockSpec((None, tr, cols), lambda l, r: (l, r, 0))
    shp = jax.ShapeDtypeStruct((n_layers, rows, cols), F32)
    return pl.pallas_call(
        body, name=name, out_shape=(shp, shp, shp, shp), grid=(n_layers, rows // tr),
        in_specs=[pl.BlockSpec((N_CHIP, None, tr, cols), lambda l, r: (0, l, r, 0)), blk, blk, blk],
        out_specs=(blk, blk, blk, blk),
        compiler_params=_params(("parallel", "parallel")),
    )(land, w, m, v)


def _sum_slots(land, *, name):
    _, _, rows, cols = land.shape
    tr = _tile(rows, 256, 8)

    def body(land_ref, g_ref):
        g = land_ref[0]
        for i in range(1, N_CHIP):
            g = g + land_ref[i]
        g_ref[...] = g

    return pl.pallas_call(
        body, name=name, out_shape=jax.ShapeDtypeStruct((rows, cols), F32), grid=(rows // tr,),
        in_specs=[pl.BlockSpec((N_CHIP, None, tr, cols), lambda r: (0, 0, r, 0))],
        out_specs=pl.BlockSpec((tr, cols), lambda r: (r, 0)),
        compiler_params=_params(("parallel",)),
    )(land)


def _adam_flat(w, g, m, v, *, name):
    rows, cols = w.shape
    tr = _tile(rows, 256, 8)

    def body(w_ref, g_ref, m_ref, v_ref, d_ref, mo_ref, vo_ref):
        delta, m_new, v_new = _adam_math(w_ref[...], g_ref[...], m_ref[...], v_ref[...])
        d_ref[...] = delta
        mo_ref[...] = m_new
        vo_ref[...] = v_new

    blk = pl.BlockSpec((tr, cols), lambda r: (r, 0))
    shp = jax.ShapeDtypeStruct((rows, cols), F32)
    return pl.pallas_call(
        body, name=name, out_shape=(shp, shp, shp), grid=(rows // tr,), in_specs=[blk] * 4, out_specs=(blk, blk, blk),
        compiler_params=_params(("parallel",)),
    )(w, g, m, v)


_BIG = (("ffn1_w_in", 1), ("ffn1_w_out", 0), ("w_in", 1), ("w_out", 0), ("ffn2_w_in", 1), ("ffn2_w_out", 0))
_SMALL = ("meta_tokens", "ffn1_norm", "mix_norm", "conv_w", "conv_b", "lru_wa", "lru_ba", "lru_wx", "lru_bx", "lru_a_param",
          "pool_w", "pool_b", "pool_scale", "ffn2_norm", "final_norm")
_SMALL_SHARD_AXIS = {"meta_tokens": 1, "conv_w": 2, "pool_w": 2}
_PACK_COLS = 1024


def _pack(arrs):
    flat = jnp.concatenate([a.reshape(-1) for a in arrs])
    unit = N_DEV * 8 * _PACK_COLS
    total = -(-flat.shape[0] // unit) * unit
    flat = jnp.pad(flat, (0, total - flat.shape[0]))
    return flat.reshape(total // _PACK_COLS, _PACK_COLS)


def _unpack(packed, shapes):
    flat = packed.reshape(-1)
    out, off = [], 0
    for s in shapes:
        size = 1
        for v in s:
            size *= v
        out.append(flat[off:off + size].reshape(s))
        off += size
    return out


def _my_shard(full, axis, dev):
    size = full.shape[axis] // N_DEV
    return lax.dynamic_slice_in_dim(full, dev * size, size, axis)


def kernel(x, meta_tokens, ffn1_norm, ffn1_w_in, ffn1_w_out, mix_norm, w_in, conv_w, conv_b, lru_wa, lru_ba, lru_wx, lru_bx, lru_a_param, pool_w, pool_b, pool_scale, w_out, ffn2_norm, ffn2_w_in, ffn2_w_out, final_norm, loss_target, m_meta_tokens, m_ffn1_norm, m_ffn1_w_in, m_ffn1_w_out, m_mix_norm, m_w_in, m_conv_w, m_conv_b, m_lru_wa, m_lru_ba, m_lru_wx, m_lru_bx, m_lru_a_param, m_pool_w, m_pool_b, m_pool_scale, m_w_out, m_ffn2_norm, m_ffn2_w_in, m_ffn2_w_out, m_final_norm, v_meta_tokens, v_ffn1_norm, v_ffn1_w_in, v_ffn1_w_out, v_mix_norm, v_w_in, v_conv_w, v_conv_b, v_lru_wa, v_lru_ba, v_lru_wx, v_lru_bx, v_lru_a_param, v_pool_w, v_pool_b, v_pool_scale, v_w_out, v_ffn2_norm, v_ffn2_w_in, v_ffn2_w_out, v_final_norm):
    names = ("meta_tokens", "ffn1_norm", "ffn1_w_in", "ffn1_w_out", "mix_norm", "w_in", "conv_w", "conv_b", "lru_wa", "lru_ba",
             "lru_wx", "lru_bx", "lru_a_param", "pool_w", "pool_b", "pool_scale", "w_out", "ffn2_norm", "ffn2_w_in", "ffn2_w_out",
             "final_norm")
    w = dict(zip(names, (meta_tokens, ffn1_norm, ffn1_w_in, ffn1_w_out, mix_norm, w_in, conv_w, conv_b, lru_wa, lru_ba, lru_wx,
                         lru_bx, lru_a_param, pool_w, pool_b, pool_scale, w_out, ffn2_norm, ffn2_w_in, ffn2_w_out, final_norm)))
    mom = dict(zip(names, (m_meta_tokens, m_ffn1_norm, m_ffn1_w_in, m_ffn1_w_out, m_mix_norm, m_w_in, m_conv_w, m_conv_b, m_lru_wa,
                           m_lru_ba, m_lru_wx, m_lru_bx, m_lru_a_param, m_pool_w, m_pool_b, m_pool_scale, m_w_out, m_ffn2_norm,
                           m_ffn2_w_in, m_ffn2_w_out, m_final_norm)))
    vel = dict(zip(names, (v_meta_tokens, v_ffn1_norm, v_ffn1_w_in, v_ffn1_w_out, v_mix_norm, v_w_in, v_conv_w, v_conv_b, v_lru_wa,
                           v_lru_ba, v_lru_wx, v_lru_bx, v_lru_a_param, v_pool_w, v_pool_b, v_pool_scale, v_w_out, v_ffn2_norm,
                           v_ffn2_w_in, v_ffn2_w_out, v_final_norm)))
    n_layers, d = ffn1_norm.shape
    n_meta = meta_tokens.shape[0]
    seq = x.shape[1]
    t_valid = n_meta + seq
    tp = -(-t_valid // ROW_ALIGN) * ROW_ALIGN
    dev = 4 * lax.axis_index("x") + 2 * lax.axis_index("y") + lax.axis_index("c")
    my_c = lax.axis_index("c").astype(jnp.int32).reshape(1)

    big_axes = dict(_BIG)
    shards = [w[k].astype(BF16) for k, _ in _BIG] + [meta_tokens, conv_w, pool_w]
    axes = [ax + 1 for _, ax in _BIG] + [1, 2, 2]
    gathered = _all_gather(shards, axes, name="gather_weights")
    gw = dict(zip([k for k, _ in _BIG], gathered[:len(_BIG)]))
    meta_full, conv_w_full, pool_w_full = gathered[len(_BIG):]
    vec = lambda a: a.reshape(a.shape[0], 1, a.shape[1])
    gw.update(ffn1_norm=vec(ffn1_norm), mix_norm=vec(mix_norm), ffn2_norm=vec(ffn2_norm), final_norm=final_norm.reshape(1, d))
    gw["mix"] = dict(conv_w=conv_w_full, conv_b=vec(conv_b), wa=lru_wa.astype(BF16), ba=vec(lru_ba), wx=lru_wx.astype(BF16),
                     bx=vec(lru_bx), a_param=vec(lru_a_param), pool_w=pool_w_full.astype(BF16), pool_b=vec(pool_b),
                     pool_scale=vec(pool_scale))

    pad = jnp.zeros((tp - t_valid, d), F32)
    h0 = jnp.concatenate([meta_full, x[0], pad], axis=0)
    tgt = jnp.concatenate([jnp.zeros((n_meta, d), F32), loss_target[0], pad], axis=0)
    loss_local, dh0, g_final, grads = _local_step(h0, tgt, gw, n_meta, t_valid)
    loss = lax.psum(loss_local[0, 0], ("x", "y", "c"))
    grad_x = dh0[n_meta:t_valid][None]

    mix_key = dict(conv_w="conv_w", conv_b="conv_b", lru_wa="wa", lru_ba="ba", lru_wx="wx", lru_bx="bx", lru_a_param="a_param",
                   pool_w="pool_w", pool_b="pool_b", pool_scale="pool_scale")
    small_local = {"meta_tokens": dh0[:n_meta], "final_norm": g_final.reshape(d)}
    for k in ("ffn1_norm", "mix_norm", "ffn2_norm"):
        small_local[k] = jnp.stack([g.reshape(d) for g in grads[k]])
    for k, mk in mix_key.items():
        small_local[k] = jnp.stack([grads["mix"][l][mk] for l in range(n_layers)]).reshape(
            w[k].shape if k not in _SMALL_SHARD_AXIS else small_shape_full(w[k], _SMALL_SHARD_AXIS[k]))
    small_shapes = [small_local[k].shape for k in _SMALL]
    packed = _pack([small_local[k] for k in _SMALL])

    tensors = [grads[k] for k, _ in _BIG] + [[packed]]
    rs_axes = [ax for _, ax in _BIG] + [0]
    lands = _rs_exchange_sibling(tensors, rs_axes, name="rs_sibling")
    parts = [_rs_add_sibling(t, ax, land, my_c, name=f"rs_add_{i}") for i, (t, ax, land) in enumerate(zip(tensors, rs_axes, lands))]
    lands2 = _rs_exchange_chips(parts, name="rs_chips")

    out_g, out_d, out_m, out_v = {}, {}, {}, {}
    for (k, _), land in zip(_BIG, lands2[:len(_BIG)]):
        out_g[k], out_d[k], out_m[k], out_v[k] = _sum_adam(land, w[k], mom[k], vel[k], name=f"adam_{k}")
    small_block = _sum_slots(lands2[-1], name="rs_sum_small")
    small_full = _all_gather([small_block], [0], name="gather_small")[0]
    small_g = dict(zip(_SMALL, _unpack(small_full, small_shapes)))
    for k, ax in _SMALL_SHARD_AXIS.items():
        small_g[k] = _my_shard(small_g[k], ax, dev)
    shapes_local = [w[k].shape for k in _SMALL]
    d_p, m_p, v_p = _adam_flat(_pack([w[k] for k in _SMALL]), _pack([small_g[k] for k in _SMALL]),
                               _pack([mom[k] for k in _SMALL]), _pack([vel[k] for k in _SMALL]), name="adam_small")
    for k, dd, mm, vv in zip(_SMALL, _unpack(d_p, shapes_local), _unpack(m_p, shapes_local), _unpack(v_p, shapes_local)):
        out_g[k], out_d[k], out_m[k], out_v[k] = small_g[k], dd, mm, vv

    return (loss, grad_x, *[out_g[k] for k in names], *[out_d[k] for k in names], *[out_m[k] for k in names],
            *[out_v[k] for k in names])


def small_shape_full(w_shard, axis):
    return w_shard.shape[:axis] + (N_DEV * w_shard.shape[axis],) + w_shard.shape[axis + 1:]
```

```python
import functools

import jax
import jax.numpy as jnp
from jax import lax
from jax.experimental import pallas as pl
from jax.experimental.pallas import tpu as pltpu

F32 = jnp.float32
BF16 = jnp.bfloat16
MESH = pl.DeviceIdType.MESH

N_DEV = 8
N_CHIP = 4
RMS_EPS = 1e-6
LRU_C = 8.0
CONV_WIDTH = 4
POOL_WINDOWS = (2, 4, 8, 16)
HIST = 16
ADAM_LR = 0.001
ADAM_B1 = 0.9
ADAM_B2 = 0.999
ADAM_EPS = 1e-08
ADAM_WD = 0.01
ADAM_STEP = 10
ROW_ALIGN = 128
V7X_VMEM_LIMIT = 56 * 1024 * 1024
BF16_ROWS = 16
LANES = 128


def _tile(n, cap, mult=BF16_ROWS):
    best = None
    d = mult
    while d <= min(n, cap):
        if n % d == 0:
            best = d
        d += mult
    if best is None:
        raise ValueError(f"no tile for {n} (cap {cap}, multiple of {mult})")
    return best


def _row_mult(dtype):
    return 8 * 4 // jnp.dtype(dtype).itemsize


def _params(sem=None):
    return pltpu.CompilerParams(dimension_semantics=sem, vmem_limit_bytes=V7X_VMEM_LIMIT)


def _dot(a, b, mode):
    dims = {"nn": ((1,), (0,)), "nt": ((1,), (1,)), "tn": ((0,), (0,))}[mode]
    return lax.dot_general(a, b, (dims, ((), ())), preferred_element_type=F32)


def _sigmoid(x):
    return 1.0 / (1.0 + jnp.exp(-x))


_GELU_C = 0.7978845608028654
_GELU_K = 0.044715


def _gelu(x):
    return 0.5 * x * (1.0 + jnp.tanh(_GELU_C * (x + _GELU_K * x * x * x)))


def _gelu_grad(x):
    th = jnp.tanh(_GELU_C * (x + _GELU_K * x * x * x))
    return 0.5 * (1.0 + th) + 0.5 * x * (1.0 - th * th) * _GELU_C * (1.0 + 3.0 * _GELU_K * x * x)


def _neg_expm1(x):
    p = 1.0 / 3628800.0
    for c in (1.0 / 362880.0, 1.0 / 40320.0, 1.0 / 5040.0, 1.0 / 720.0, 1.0 / 120.0, 1.0 / 24.0, 1.0 / 6.0, 0.5, 1.0):
        p = p * x + c
    return jnp.where(x > -0.5, -(x * p), 1.0 - jnp.exp(x))


def _softplus_neg(p):
    e = jnp.exp(-jnp.abs(p))
    u = 1.0 + e
    l1p = jnp.where(u == 1.0, e, jnp.log(u) * e / (u - 1.0 + (u == 1.0).astype(F32)))
    return jnp.maximum(-p, 0.0) + l1p


def _operand_spec(arr, br, bc, ridx, cidx, layer, split):
    if split:
        ncb = arr.shape[2] // bc
        return pl.BlockSpec((None, br, bc), lambda i, j, k: (cidx(i, j, k) // ncb, ridx(i, j, k), cidx(i, j, k) % ncb))
    if layer is not None:
        return pl.BlockSpec((None, br, bc), lambda i, j, k: (layer, ridx(i, j, k), cidx(i, j, k)))
    return pl.BlockSpec((br, bc), lambda i, j, k: (ridx(i, j, k), cidx(i, j, k)))


def _matmul(a, b, *, mode, m, n, kdim, tm, tn, tk, out_dtype, name, scale=None, residual=None,
            a_layer=None, b_layer=None, a_split=False, b_split=False, deps=()):
    nk = kdim // tk
    I = lambda i, j, k: i
    J = lambda i, j, k: j
    K = lambda i, j, k: k
    if mode == "nn":
        a_spec = _operand_spec(a, tm, tk, I, K, a_layer, a_split)
        b_spec = _operand_spec(b, tk, tn, K, J, b_layer, b_split)
    elif mode == "nt":
        a_spec = _operand_spec(a, tm, tk, I, K, a_layer, a_split)
        b_spec = _operand_spec(b, tn, tk, J, K, b_layer, b_split)
    else:
        a_spec = _operand_spec(a, tk, tm, K, I, a_layer, a_split)
        b_spec = _operand_spec(b, tk, tn, K, J, b_layer, b_split)
    in_specs = [a_spec, b_spec]
    operands = [a, b]
    if residual is not None:
        in_specs.append(pl.BlockSpec((tm, tn), lambda i, j, k: (i, j)))
        operands.append(residual)
    in_specs += [pl.BlockSpec(memory_space=pl.ANY)] * len(deps)
    operands += list(deps)
    n_in = len(operands)

    def finish(acc, res_ref, o_ref):
        if scale is not None:
            acc = acc * scale
        if res_ref is not None:
            acc = acc + res_ref[...]
        o_ref[...] = acc.astype(o_ref.dtype)

    def body(*refs):
        a_ref, b_ref = refs[0], refs[1]
        res_ref = refs[2] if residual is not None else None
        o_ref = refs[n_in]
        part = _dot(a_ref[...], b_ref[...], mode)
        if nk == 1:
            finish(part, res_ref, o_ref)
        else:
            acc_ref = refs[-1]
            k = pl.program_id(2)

            @pl.when(k == 0)
            def _():
                acc_ref[...] = part

            @pl.when(k > 0)
            def _():
                acc_ref[...] += part

            @pl.when(k == nk - 1)
            def _():
                finish(acc_ref[...], res_ref, o_ref)

    return pl.pallas_call(
        body, name=name,
        out_shape=jax.ShapeDtypeStruct((m, n), out_dtype),
        grid=(m // tm, n // tn, nk),
        in_specs=in_specs,
        out_specs=pl.BlockSpec((tm, tn), lambda i, j, k: (i, j)),
        scratch_shapes=[pltpu.VMEM((tm, tn), F32)] if nk > 1 else [],
        compiler_params=_params(("parallel", "parallel", "arbitrary")),
    )(*operands)


def _ffn_in_fwd(n_act, w_in, layer, *, name):
    tp, d = n_act.shape
    f = w_in.shape[2] // 2
    tm = _tile(tp, 2112)
    tn = _tile(f, 256, LANES)
    nj = f // tn

    def body(n_ref, wg_ref, wu_ref, gu_ref, a_ref):
        x = n_ref[...]
        g = _dot(x, wg_ref[...], "nn")
        u = _dot(x, wu_ref[...], "nn")
        gu_ref[0] = g.astype(BF16)
        gu_ref[1] = u.astype(BF16)
        a_ref[...] = (g * _sigmoid(g) * u).astype(BF16)

    return pl.pallas_call(
        body, name=name,
        out_shape=(jax.ShapeDtypeStruct((2, tp, f), BF16), jax.ShapeDtypeStruct((tp, f), BF16)),
        grid=(tp // tm, nj),
        in_specs=[pl.BlockSpec((tm, d), lambda i, j: (i, 0)),
                  pl.BlockSpec((None, d, tn), lambda i, j: (layer, 0, j)),
                  pl.BlockSpec((None, d, tn), lambda i, j: (layer, 0, j + nj))],
        out_specs=(pl.BlockSpec((2, tm, tn), lambda i, j: (0, i, j)), pl.BlockSpec((tm, tn), lambda i, j: (i, j))),
        compiler_params=_params(("parallel", "arbitrary")),
    )(n_act, w_in, w_in)


def _ffn_bwd_act(dhb, w_out, layer, gu, *, name, deps=()):
    tp, d = dhb.shape
    f = w_out.shape[1]
    tm = _tile(tp, 2112)
    tn = _tile(f, 256, LANES)

    def body(dh_ref, w_ref, gu_ref, *rest):
        dz_ref, a_ref = rest[len(deps):]
        da = 0.5 * _dot(dh_ref[...], w_ref[...], "nt")
        g = gu_ref[0].astype(F32)
        u = gu_ref[1].astype(F32)
        s = _sigmoid(g)
        sg = g * s
        dz_ref[0] = (da * u * (s * (1.0 + g * (1.0 - s)))).astype(BF16)
        dz_ref[1] = (da * sg).astype(BF16)
        a_ref[...] = (sg * u).astype(BF16)

    return pl.pallas_call(
        body, name=name,
        out_shape=(jax.ShapeDtypeStruct((2, tp, f), BF16), jax.ShapeDtypeStruct((tp, f), BF16)),
        grid=(tp // tm, f // tn),
        in_specs=[pl.BlockSpec((tm, d), lambda i, j: (i, 0)),
                  pl.BlockSpec((None, tn, d), lambda i, j: (layer, j, 0)),
                  pl.BlockSpec((2, tm, tn), lambda i, j: (0, i, j))] + [pl.BlockSpec(memory_space=pl.ANY)] * len(deps),
        out_specs=(pl.BlockSpec((2, tm, tn), lambda i, j: (0, i, j)), pl.BlockSpec((tm, tn), lambda i, j: (i, j))),
        compiler_params=_params(("parallel", "arbitrary")),
    )(dhb, w_out, gu, *deps)


def _rms_fwd(h, g, layer, *, name, deps=()):
    tp, d = h.shape
    tr = _tile(tp, 528)

    def body(h_ref, g_ref, *rest):
        n_ref = rest[len(deps)]
        x = h_ref[...]
        r = lax.rsqrt(jnp.mean(x * x, axis=-1, keepdims=True) + RMS_EPS)
        n_ref[...] = (x * r * g_ref[...]).astype(BF16)

    return pl.pallas_call(
        body, name=name, out_shape=jax.ShapeDtypeStruct((tp, d), BF16), grid=(tp // tr,),
        in_specs=[pl.BlockSpec((tr, d), lambda i: (i, 0)), pl.BlockSpec((None, 1, d), lambda i: (layer, 0, 0))]
        + [pl.BlockSpec(memory_space=pl.ANY)] * len(deps),
        out_specs=pl.BlockSpec((tr, d), lambda i: (i, 0)),
        compiler_params=_params(("parallel",)),
    )(h, g, *deps)


def _rms_bwd(h, g, layer, dn, dres, *, name):
    tp, d = h.shape
    tr = _tile(tp, 528)

    def body(h_ref, g_ref, dn_ref, dres_ref, dh_ref, dhb_ref, dg_ref):
        x = h_ref[...]
        r = lax.rsqrt(jnp.mean(x * x, axis=-1, keepdims=True) + RMS_EPS)
        xhat = x * r
        dn_v = dn_ref[...]
        dxhat = dn_v * g_ref[...]
        dh = dres_ref[...] + r * (dxhat - xhat * jnp.mean(dxhat * xhat, axis=-1, keepdims=True))
        dh_ref[...] = dh
        dhb_ref[...] = dh.astype(BF16)
        part = jnp.sum(dn_v * xhat, axis=0, keepdims=True)

        @pl.when(pl.program_id(0) == 0)
        def _():
            dg_ref[...] = part

        @pl.when(pl.program_id(0) > 0)
        def _():
            dg_ref[...] += part

    return pl.pallas_call(
        body, name=name,
        out_shape=(jax.ShapeDtypeStruct((tp, d), F32), jax.ShapeDtypeStruct((tp, d), BF16), jax.ShapeDtypeStruct((1, d), F32)),
        grid=(tp // tr,),
        in_specs=[pl.BlockSpec((tr, d), lambda i: (i, 0)), pl.BlockSpec((None, 1, d), lambda i: (layer, 0, 0)),
                  pl.BlockSpec((tr, d), lambda i: (i, 0)), pl.BlockSpec((tr, d), lambda i: (i, 0))],
        out_specs=(pl.BlockSpec((tr, d), lambda i: (i, 0)), pl.BlockSpec((tr, d), lambda i: (i, 0)),
                   pl.BlockSpec((1, d), lambda i: (0, 0))),
        compiler_params=_params(("arbitrary",)),
    )(h, g, dn, dres)


def _final_loss(h, g, tgt, n_meta, t_valid, *, name):
    tp, d = h.shape
    tr = _tile(tp, 528)

    def body(h_ref, g_ref, t_ref, dh_ref, dhb_ref, dg_ref, loss_ref):
        i = pl.program_id(0)
        x = h_ref[...]
        r = lax.rsqrt(jnp.mean(x * x, axis=-1, keepdims=True) + RMS_EPS)
        xhat = x * r
        gv = g_ref[...]
        row = i * tr + lax.broadcasted_iota(jnp.int32, (tr, 1), 0)
        valid = jnp.logical_and(row >= n_meta, row < t_valid)
        err = jnp.where(valid, xhat * gv - t_ref[...], 0.0)
        dy = err * (1.0 / d)
        dxhat = dy * gv
        dh = r * (dxhat - xhat * jnp.mean(dxhat * xhat, axis=-1, keepdims=True))
        dh_ref[...] = dh
        dhb_ref[...] = dh.astype(BF16)
        dg_part = jnp.sum(dy * xhat, axis=0, keepdims=True)
        loss_part = jnp.sum(jnp.sum(err * err, axis=1, keepdims=True), axis=0, keepdims=True) * (0.5 / d)

        @pl.when(i == 0)
        def _():
            dg_ref[...] = dg_part
            loss_ref[...] = loss_part

        @pl.when(i > 0)
        def _():
            dg_ref[...] += dg_part
            loss_ref[...] += loss_part

    return pl.pallas_call(
        body, name=name,
        out_shape=(jax.ShapeDtypeStruct((tp, d), F32), jax.ShapeDtypeStruct((tp, d), BF16),
                   jax.ShapeDtypeStruct((1, d), F32), jax.ShapeDtypeStruct((1, 1), F32)),
        grid=(tp // tr,),
        in_specs=[pl.BlockSpec((tr, d), lambda i: (i, 0)), pl.BlockSpec((1, d), lambda i: (0, 0)),
                  pl.BlockSpec((tr, d), lambda i: (i, 0))],
        out_specs=(pl.BlockSpec((tr, d), lambda i: (i, 0)), pl.BlockSpec((tr, d), lambda i: (i, 0)),
                   pl.BlockSpec((1, d), lambda i: (0, 0)), pl.BlockSpec((1, 1), lambda i: (0, 0))),
        compiler_params=_params(("arbitrary",)),
    )(h, g, tgt)


def _mix_param_specs(mp, layer, imap):
    def vec(a):
        return pl.BlockSpec((None,) + a.shape[1:], lambda *g: (layer,) + (0,) * (a.ndim - 1))
    return [vec(mp[k]) for k in ("conv_w", "conv_b", "wa", "ba", "wx", "bx", "a_param", "pool_w", "pool_b", "pool_scale")]


def _mix_param_list(mp):
    return [mp[k] for k in ("conv_w", "conv_b", "wa", "ba", "wx", "bx", "a_param", "pool_w", "pool_b", "pool_scale")]


def _lru_gates(xc, wa_h, ba_h, wx_h, bx_h, sp_h):
    xb = xc.astype(BF16)
    ra = _sigmoid(_dot(xb, wa_h, "nn") + ba_h)
    ii = _sigmoid(_dot(xb, wx_h, "nn") + bx_h)
    la = -LRU_C * ra * sp_h
    a = jnp.exp(la)
    mult = jnp.sqrt(_neg_expm1(2.0 * la))
    return xb, ra, ii, a, mult


def _conv_fwd(xbuf, cw_ref, cb_ref, sl, tc):
    xc = cb_ref[:, sl]
    for k in range(CONV_WIDTH):
        xc = xc + cw_ref[k:k + 1, sl] * xbuf[pl.ds(HIST - (CONV_WIDTH - 1) + k, tc), sl]
    return xc


def _pool_delta(pbuf, cols, win, t0, tc):
    u = pbuf[pl.ds(HIST, tc), cols]
    s = u
    for k in range(1, win):
        s = s + pbuf[pl.ds(HIST - k, tc), cols]
    t = t0 + lax.broadcasted_iota(jnp.int32, (tc, 1), 0)
    cnt = jnp.minimum(t + 1, win).astype(F32)
    return s / cnt - u, cnt


def _mix_fwd(z, mp, layer, *, name):
    tp = z.shape[0]
    dl = z.shape[1] // 3
    n_heads, hd = mp["wa"].shape[1], mp["wa"].shape[2]
    n_groups, gd = mp["pool_w"].shape[1], mp["pool_w"].shape[2]
    tc = _tile(tp, 384)

    def body(z_ref, cw_ref, cb_ref, wa_ref, ba_ref, wx_ref, bx_ref, ap_ref, pw_ref, pb_ref, ps_ref,
             m_ref, hs_ref, xbuf, pbuf, a_s, b_s, hcar):
        i = pl.program_id(0)

        @pl.when(i == 0)
        def _():
            xbuf[pl.ds(0, HIST), :] = jnp.zeros((HIST, dl), F32)
            pbuf[pl.ds(0, HIST), :] = jnp.zeros((HIST, dl), F32)
            hcar[...] = jnp.zeros((1, dl), F32)

        @pl.when(i > 0)
        def _():
            xbuf[pl.ds(0, HIST), :] = xbuf[pl.ds(tc, HIST), :]
            pbuf[pl.ds(0, HIST), :] = pbuf[pl.ds(tc, HIST), :]

        xbuf[pl.ds(HIST, tc), :] = z_ref[:, pl.ds(0, dl)]
        pbuf[pl.ds(HIST, tc), :] = z_ref[:, pl.ds(2 * dl, dl)]
        sp = _softplus_neg(ap_ref[...])
        for h in range(n_heads):
            sl = pl.ds(h * hd, hd)
            xc = _conv_fwd(xbuf, cw_ref, cb_ref, sl, tc)
            _, _, ii, a, mult = _lru_gates(xc, wa_ref[h], ba_ref[:, sl], wx_ref[h], bx_ref[:, sl], sp[:, h * hd:(h + 1) * hd])
            a_s[:, sl] = a
            b_s[:, sl] = mult * ii * xc

        def step(t, hprev):
            hnew = a_s[pl.ds(t, 1), :] * hprev + b_s[pl.ds(t, 1), :]
            hs_ref[pl.ds(t, 1), :] = hnew
            return hnew

        hcar[...] = lax.fori_loop(0, tc, step, hcar[...])
        for h in range(n_heads):
            sl = pl.ds(h * hd, hd)
            m_ref[:, sl] = (hs_ref[:, sl] * _gelu(z_ref[:, pl.ds(dl + h * hd, hd)])).astype(BF16)
        for g in range(n_groups):
            cols = pl.ds(g * gd, gd)
            dlt, _ = _pool_delta(pbuf, cols, POOL_WINDOWS[g], i * tc, tc)
            q = _dot(dlt.astype(BF16), pw_ref[g], "nn") + pb_ref[:, cols]
            m_ref[:, pl.ds(dl + g * gd, gd)] = (q * ps_ref[:, cols]).astype(BF16)

    return pl.pallas_call(
        body, name=name,
        out_shape=(jax.ShapeDtypeStruct((tp, 2 * dl), BF16), jax.ShapeDtypeStruct((tp, dl), F32)),
        grid=(tp // tc,),
        in_specs=[pl.BlockSpec((tc, 3 * dl), lambda i: (i, 0))] + _mix_param_specs(mp, layer, None),
        out_specs=(pl.BlockSpec((tc, 2 * dl), lambda i: (i, 0)), pl.BlockSpec((tc, dl), lambda i: (i, 0))),
        scratch_shapes=[pltpu.VMEM((HIST + tc, dl), F32), pltpu.VMEM((HIST + tc, dl), F32),
                        pltpu.VMEM((tc, dl), F32), pltpu.VMEM((tc, dl), F32), pltpu.VMEM((1, dl), F32)],
        compiler_params=_params(("arbitrary",)),
    )(z, *_mix_param_list(mp))


_MIX_GRADS = ("conv_w", "conv_b", "wa", "ba", "wx", "bx", "a_param", "pool_w", "pool_b", "pool_scale")


def _mix_bwd(z, hs, dm, mp, layer, *, name):
    tp = z.shape[0]
    dl = z.shape[1] // 3
    n_heads, hd = mp["wa"].shape[1], mp["wa"].shape[2]
    n_groups, gd = mp["pool_w"].shape[1], mp["pool_w"].shape[2]
    tc = _tile(tp, 384)
    nc = tp // tc
    per = tc // HIST

    def body(z_ref, zp_ref, hs_ref, hsp_ref, dm_ref, cw_ref, cb_ref, wa_ref, ba_ref, wx_ref, bx_ref, ap_ref, pw_ref, pb_ref,
             ps_ref, dz_ref, dcw_ref, dcb_ref, dwa_ref, dba_ref, dwx_ref, dbx_ref, dap_ref, dpw_ref, dpb_ref, dps_ref,
             xbuf, pbuf, hbuf, dxbuf, ddbuf, a_s, lam_s, ra_s, ii_s, xc_s, ccar):
        i = pl.program_id(0)
        ci = nc - 1 - i

        @pl.when(i == 0)
        def _():
            dxbuf[pl.ds(tc, HIST), :] = jnp.zeros((HIST, dl), F32)
            ddbuf[pl.ds(tc, HIST), :] = jnp.zeros((HIST, dl), F32)
            ccar[...] = jnp.zeros((1, dl), F32)
            for ref in (dcw_ref, dcb_ref, dwa_ref, dba_ref, dwx_ref, dbx_ref, dap_ref, dpw_ref, dpb_ref, dps_ref):
                ref[...] = jnp.zeros(ref.shape, F32)

        @pl.when(ci == 0)
        def _():
            xbuf[pl.ds(0, HIST), :] = jnp.zeros((HIST, dl), F32)
            pbuf[pl.ds(0, HIST), :] = jnp.zeros((HIST, dl), F32)
            hbuf[pl.ds(0, HIST), :] = jnp.zeros((HIST, dl), F32)

        @pl.when(ci > 0)
        def _():
            xbuf[pl.ds(0, HIST), :] = zp_ref[:, pl.ds(0, dl)]
            pbuf[pl.ds(0, HIST), :] = zp_ref[:, pl.ds(2 * dl, dl)]
            hbuf[pl.ds(0, HIST), :] = hsp_ref[...]

        xbuf[pl.ds(HIST, tc), :] = z_ref[:, pl.ds(0, dl)]
        pbuf[pl.ds(HIST, tc), :] = z_ref[:, pl.ds(2 * dl, dl)]
        hbuf[pl.ds(HIST, tc), :] = hs_ref[...]
        sp = _softplus_neg(ap_ref[...])

        for h in range(n_heads):
            sl = pl.ds(h * hd, hd)
            xc = _conv_fwd(xbuf, cw_ref, cb_ref, sl, tc)
            _, ra, ii, a, _ = _lru_gates(xc, wa_ref[h], ba_ref[:, sl], wx_ref[h], bx_ref[:, sl], sp[:, h * hd:(h + 1) * hd])
            a_s[:, sl] = a
            ra_s[:, sl] = ra
            ii_s[:, sl] = ii
            xc_s[:, sl] = xc
            lam_s[:, sl] = dm_ref[:, sl] * _gelu(z_ref[:, pl.ds(dl + h * hd, hd)])

        def step(r, carry):
            t = tc - 1 - r
            lam = lam_s[pl.ds(t, 1), :] + carry
            lam_s[pl.ds(t, 1), :] = lam
            return a_s[pl.ds(t, 1), :] * lam

        ccar[...] = lax.fori_loop(0, tc, step, ccar[...])

        for h in range(n_heads):
            sl = pl.ds(h * hd, hd)
            sp_h = sp[:, h * hd:(h + 1) * hd]
            lam = lam_s[:, sl]
            a = a_s[:, sl]
            ra = ra_s[:, sl]
            ii = ii_s[:, sl]
            xc = xc_s[:, sl]
            la = -LRU_C * ra * sp_h
            mult = jnp.sqrt(_neg_expm1(2.0 * la))
            hprev = hbuf[pl.ds(HIST - 1, tc), sl]
            dmult = lam * ii * xc
            dla = lam * hprev * a - dmult * (a * a) / mult
            dap_ref[:, sl] += jnp.sum(dla * (-LRU_C) * ra, axis=0, keepdims=True)
            dpa = dla * (-LRU_C) * sp_h * ra * (1.0 - ra)
            dpx = lam * mult * xc * ii * (1.0 - ii)
            dba_ref[:, sl] += jnp.sum(dpa, axis=0, keepdims=True)
            dbx_ref[:, sl] += jnp.sum(dpx, axis=0, keepdims=True)
            xb = xc.astype(BF16)
            dpa_b = dpa.astype(BF16)
            dpx_b = dpx.astype(BF16)
            dwa_ref[h] += _dot(xb, dpa_b, "tn")
            dwx_ref[h] += _dot(xb, dpx_b, "tn")
            dxc = lam * mult * ii + _dot(dpa_b, wa_ref[h], "nt") + _dot(dpx_b, wx_ref[h], "nt")
            dxbuf[pl.ds(0, tc), sl] = dxc
            dcb_ref[:, sl] += jnp.sum(dxc, axis=0, keepdims=True)
            dzx = jnp.zeros((tc, hd), F32)
            for k in range(CONV_WIDTH):
                dcw_ref[k:k + 1, sl] += jnp.sum(dxc * xbuf[pl.ds(HIST - (CONV_WIDTH - 1) + k, tc), sl], axis=0, keepdims=True)
                dzx = dzx + cw_ref[k:k + 1, sl] * dxbuf[pl.ds(CONV_WIDTH - 1 - k, tc), sl]
            dz_ref[:, sl] = dzx.astype(BF16)
            dxbuf[pl.ds(tc, HIST), sl] = dxbuf[pl.ds(0, HIST), sl]
            zg = z_ref[:, pl.ds(dl + h * hd, hd)]
            dz_ref[:, pl.ds(dl + h * hd, hd)] = (dm_ref[:, sl] * hs_ref[:, sl] * _gelu_grad(zg)).astype(BF16)

        for g in range(n_groups):
            cols = pl.ds(g * gd, gd)
            win = POOL_WINDOWS[g]
            dlt, cnt = _pool_delta(pbuf, cols, win, ci * tc, tc)
            db = dlt.astype(BF16)
            q = _dot(db, pw_ref[g], "nn") + pb_ref[:, cols]
            dyb = dm_ref[:, pl.ds(dl + g * gd, gd)]
            dps_ref[:, cols] += jnp.sum(dyb * q, axis=0, keepdims=True)
            dq = dyb * ps_ref[:, cols]
            dpb_ref[:, cols] += jnp.sum(dq, axis=0, keepdims=True)
            dqb = dq.astype(BF16)
            dpw_ref[g] += _dot(db, dqb, "tn")
            dd = _dot(dqb, pw_ref[g], "nt")
            ddbuf[pl.ds(0, tc), cols] = dd / cnt
            dzp = -dd
            for k in range(win):
                dzp = dzp + ddbuf[pl.ds(k, tc), cols]
            dz_ref[:, pl.ds(2 * dl + g * gd, gd)] = dzp.astype(BF16)
            ddbuf[pl.ds(tc, HIST), cols] = ddbuf[pl.ds(0, HIST), cols]

        @pl.when(i == nc - 1)
        def _():
            dap_ref[...] = dap_ref[...] * (-_sigmoid(-ap_ref[...]))

    rev = lambda i: (nc - 1 - i, 0)
    prev = lambda i: (jnp.maximum((nc - 1 - i) * per - 1, 0), 0)
    const = lambda a: pl.BlockSpec(a.shape[1:], lambda i: (0,) * (a.ndim - 1))
    plist = _mix_param_list(mp)
    grad_shapes = [jax.ShapeDtypeStruct(a.shape[1:], F32) for a in plist]
    buf = lambda rows: pltpu.VMEM((rows, dl), F32)
    outs = pl.pallas_call(
        body, name=name,
        out_shape=[jax.ShapeDtypeStruct((tp, 3 * dl), BF16)] + grad_shapes,
        grid=(nc,),
        in_specs=[pl.BlockSpec((tc, 3 * dl), rev), pl.BlockSpec((HIST, 3 * dl), prev),
                  pl.BlockSpec((tc, dl), rev), pl.BlockSpec((HIST, dl), prev),
                  pl.BlockSpec((tc, 2 * dl), rev)] + _mix_param_specs(mp, layer, None),
        out_specs=[pl.BlockSpec((tc, 3 * dl), rev)] + [const(a) for a in plist],
        scratch_shapes=[buf(HIST + tc), buf(HIST + tc), buf(HIST + tc), buf(tc + HIST), buf(tc + HIST),
                        buf(tc), buf(tc), buf(tc), buf(tc), buf(tc), buf(1)],
        compiler_params=_params(("arbitrary",)),
    )(z, z, hs, hs, dm, *plist)
    return outs[0], dict(zip(_MIX_GRADS, outs[1:]))


def _ffn_fwd_in(h, norm, layer, w_in, tag, deps):
    n_act = _rms_fwd(h, norm, layer, name=f"{tag}_rms", deps=deps)
    gu, act = _ffn_in_fwd(n_act, w_in, 0, name=f"{tag}_in")
    return n_act, gu, act


def _ffn_fwd_out(h, act, w_out, tag):
    tp, d = h.shape
    f = w_out.shape[1]
    return _matmul(act, w_out, mode="nn", m=tp, n=d, kdim=f, tm=_tile(tp, 1056), tn=_tile(d, 1024, LANES),
                   tk=_tile(f, 512, LANES), out_dtype=F32, scale=0.5, residual=h, b_layer=0, name=f"{tag}_out")


def _ffn_bwd(dh, dhb, saved, norm, layer, w_in, w_out, tag, emit, deps):
    h, n_act, gu = saved
    tp, d = h.shape
    f = w_out.shape[1]
    dz, act = _ffn_bwd_act(dhb, w_out, 0, gu, name=f"{tag}_bact", deps=deps)
    tok = emit("w_out", _matmul(act, dhb, mode="tn", m=f, n=d, kdim=tp, tm=_tile(f, 512, LANES), tn=_tile(d, 1024, LANES), tk=tp,
                                out_dtype=BF16, scale=0.5, name=f"{tag}_dwout"))
    tok = emit("w_in", _matmul(n_act, dz, mode="tn", m=d, n=2 * f, kdim=tp, tm=_tile(d, 512, LANES), tn=_tile(f, 512, LANES),
                               tk=tp, out_dtype=BF16, b_split=True, name=f"{tag}_dwin", deps=[tok]))
    dn = _matmul(dz, w_in, mode="nt", m=tp, n=d, kdim=2 * f, tm=_tile(tp, 1056), tn=d, tk=_tile(f, 512, LANES),
                 out_dtype=F32, a_split=True, b_layer=0, name=f"{tag}_dn", deps=[tok])
    return _rms_bwd(h, norm, layer, dn, dh, name=f"{tag}_brms")


def _mix_block_fwd(h, norm, layer, w_in, w_out, mp, tag, deps):
    tp, d = h.shape
    d_in = w_in.shape[2]
    n_act = _rms_fwd(h, norm, layer, name=f"{tag}_rms", deps=deps)
    z = _matmul(n_act, w_in, mode="nn", m=tp, n=d_in, kdim=d, tm=_tile(tp, 2112), tn=_tile(d_in, 512, LANES), tk=d,
                out_dtype=F32, b_layer=0, name=f"{tag}_in")
    m_act, hs = _mix_fwd(z, mp, layer, name=f"{tag}_mix")
    h_out = _matmul(m_act, w_out, mode="nn", m=tp, n=d, kdim=d, tm=_tile(tp, 2112), tn=_tile(d, 512, LANES), tk=d,
                    out_dtype=F32, residual=h, b_layer=0, name=f"{tag}_out")
    return h_out, (h, n_act, z, hs, m_act)


def _mix_block_bwd(dh, dhb, saved, norm, layer, w_in, w_out, mp, tag, emit, deps):
    h, n_act, z, hs, m_act = saved
    tp, d = h.shape
    d_in = w_in.shape[2]
    dm = _matmul(dhb, w_out, mode="nt", m=tp, n=d, kdim=d, tm=_tile(tp, 2112), tn=_tile(d, 512, LANES), tk=d,
                 out_dtype=F32, b_layer=0, name=f"{tag}_dm", deps=deps)
    tok = emit("w_out", _matmul(m_act, dhb, mode="tn", m=d, n=d, kdim=tp, tm=_tile(d, 512, LANES), tn=_tile(d, 1024, LANES), tk=tp,
                                out_dtype=BF16, name=f"{tag}_dwout"))
    dz, g_mix = _mix_bwd(z, hs, dm, mp, layer, name=f"{tag}_bmix")
    tok = emit("w_in", _matmul(n_act, dz, mode="tn", m=d, n=d_in, kdim=tp, tm=_tile(d, 512, LANES), tn=_tile(d_in, 512, LANES),
                               tk=tp, out_dtype=BF16, name=f"{tag}_dwin", deps=[tok]))
    dn = _matmul(dz, w_in, mode="nt", m=tp, n=d, kdim=d_in, tm=_tile(tp, 1056), tn=d, tk=_tile(d_in, 512, LANES),
                 out_dtype=F32, b_layer=0, name=f"{tag}_dn", deps=[tok])
    dh_in, dhb_in, g_norm = _rms_bwd(h, norm, layer, dn, dh, name=f"{tag}_brms")
    return dh_in, dhb_in, g_norm, g_mix


def _mesh_pos():
    x, y, c = lax.axis_index("x"), lax.axis_index("y"), lax.axis_index("c")
    chips = [(1 - x, y), (x, 1 - y), (1 - x, 1 - y)]
    return x, y, c, chips


def _block(ref, axis, j, size):
    idx = [slice(None)] * len(ref.shape)
    idx[axis] = pl.ds(j * size, size)
    return ref.at[tuple(idx)]


def _all_gather(shards, axes, *, name):
    n = len(shards)
    sizes = [s.shape[ax] for s, ax in zip(shards, axes)]
    full_shapes = [s.shape[:ax] + (N_DEV * s.shape[ax],) + s.shape[ax + 1:] for s, ax in zip(shards, axes)]

    def body(*refs):
        srcs, fulls = refs[:n], refs[n:2 * n]
        send_sems, recv_sems, local_sems = refs[2 * n:]
        x, y, c, chips = _mesh_pos()
        me, sib = (x, y, c), (x, y, 1 - c)

        def blk(a, dev):
            return _block(fulls[a], axes[a], 4 * dev[0] + 2 * dev[1] + dev[2], sizes[a])

        def copy(a, k, block_dev, to, src=None):
            return pltpu.make_async_remote_copy(
                src_ref=blk(a, block_dev) if src is None else src, dst_ref=blk(a, block_dev),
                send_sem=send_sems.at[a, k], recv_sem=recv_sems.at[a, k], device_id=to, device_id_type=MESH)

        mine = [pltpu.make_async_copy(srcs[a], blk(a, me), local_sems.at[a]) for a in range(n)]
        for cp in mine:
            cp.start()
        first = []
        for a in range(n):
            first.append(copy(a, 0, me, sib, src=srcs[a]))
            first += [copy(a, 1 + j, me, (*chip, c), src=srcs[a]) for j, chip in enumerate(chips)]
        for cp in first:
            cp.start()
        passed = []
        for j, chip in enumerate(chips):
            for a in range(n):
                copy(a, 1 + j, (*chip, c), me).wait_recv()
                fwd = copy(a, 4 + j, (*chip, c), sib)
                fwd.start()
                passed.append(fwd)
        for a in range(n):
            copy(a, 0, sib, me).wait_recv()
            for j, chip in enumerate(chips):
                copy(a, 4 + j, (*chip, 1 - c), me).wait_recv()
        for cp in first + passed:
            cp.wait_send()
        for cp in mine:
            cp.wait()

    any_spec = pl.BlockSpec(memory_space=pl.ANY)
    return pl.pallas_call(
        body, name=name,
        out_shape=[jax.ShapeDtypeStruct(fs, s.dtype) for fs, s in zip(full_shapes, shards)],
        in_specs=[any_spec] * n, out_specs=[any_spec] * n,
        scratch_shapes=[pltpu.SemaphoreType.DMA((n, 7)), pltpu.SemaphoreType.DMA((n, 7)), pltpu.SemaphoreType.DMA((n,))],
    )(*shards)


_HBM = pl.BlockSpec(memory_space=pltpu.HBM)
_SEM = pl.BlockSpec(memory_space=pltpu.SEMAPHORE)
_ANY = pl.BlockSpec(memory_space=pl.ANY)
_EFFECT = pltpu.SideEffectType.DATAFLOW_SIDE_EFFECTING
_N_OUT = 4


def _in_hbm(a):
    return pltpu.with_memory_space_constraint(a, pltpu.HBM)


def _gather_start(shards, axes, deps, *, name):
    n = len(shards)
    sizes = [s.shape[ax] for s, ax in zip(shards, axes)]
    full_shapes = [s.shape[:ax] + (N_DEV * s.shape[ax],) + s.shape[ax + 1:] for s, ax in zip(shards, axes)]

    def body(*refs):
        srcs, lands = refs[:n], refs[n:2 * n]
        send_sems, recv_sems = refs[2 * n + len(deps)], refs[2 * n + len(deps) + 1]
        token = refs[-1]
        x, y, c, chips = _mesh_pos()
        targets = [(x, y, 1 - c)] + [(*chip, c) for chip in chips]
        for a in range(n):
            dst = _block(lands[a], axes[a], 4 * x + 2 * y + c, sizes[a])
            for k, to in enumerate(targets):
                pltpu.make_async_remote_copy(src_ref=srcs[a], dst_ref=dst, send_sem=send_sems.at[_N_OUT * a + k],
                                             recv_sem=recv_sems.at[_N_OUT * a + k],
                                             device_id=to, device_id_type=MESH).start()
        token[...] = jnp.zeros(token.shape, F32)

    lands0 = [lax.empty(fs, s.dtype) for fs, s in zip(full_shapes, shards)]
    outs = pl.pallas_call(
        body, name=name,
        out_shape=(pltpu.SemaphoreType.DMA((n * _N_OUT,)), pltpu.SemaphoreType.DMA((n * _N_OUT,)),
                   *[pltpu.HBM(s.shape, s.dtype) for s in shards], *[pltpu.HBM(fs, s.dtype) for fs, s in zip(full_shapes, shards)],
                   jax.ShapeDtypeStruct((8, LANES), F32)),
        in_specs=[_HBM] * (2 * n) + [_ANY] * len(deps),
        out_specs=(_SEM, _SEM, *[_HBM] * (2 * n), pl.BlockSpec(memory_space=pltpu.VMEM)),
        input_output_aliases={a: 2 + a for a in range(2 * n)},
        compiler_params=pltpu.CompilerParams(has_side_effects=_EFFECT),
    )(*[_in_hbm(s) for s in shards], *[_in_hbm(z) for z in lands0], *deps)
    handle = dict(send_sems=outs[0], recv_sems=outs[1], srcs=list(outs[2:2 + n]), lands=list(outs[2 + n:2 + 2 * n]), axes=list(axes))
    return handle, outs[-1]


def _gather_wait(handle, after, *, name):
    srcs, lands, axes = handle["srcs"], handle["lands"], handle["axes"]
    n = len(srcs)
    sizes = [s.shape[ax] for s, ax in zip(srcs, axes)]

    def body(*refs):
        src_refs, land_refs = refs[:n], refs[n:2 * n]
        send_sems, recv_sems = refs[2 * n], refs[2 * n + 1]
        x, y, c, chips = _mesh_pos()
        peers = [(x, y, 1 - c)] + [(*chip, c) for chip in chips]
        for a in range(n):
            for k, dev in enumerate(peers):
                cp = pltpu.make_async_remote_copy(
                    src_ref=src_refs[a], dst_ref=_block(land_refs[a], axes[a], 4 * dev[0] + 2 * dev[1] + dev[2], sizes[a]),
                    send_sem=send_sems.at[_N_OUT * a + k], recv_sem=recv_sems.at[_N_OUT * a + k], device_id=dev,
                    device_id_type=MESH)
                cp.wait_send()
                cp.wait_recv()

    outs = pl.pallas_call(
        body, name=name,
        out_shape=(*[pltpu.HBM(s.shape, s.dtype) for s in srcs], *[pltpu.HBM(z.shape, z.dtype) for z in lands]),
        in_specs=[_HBM] * (2 * n) + [_SEM, _SEM] + [_ANY] * len(after),
        out_specs=tuple([_HBM] * (2 * n)),
        input_output_aliases={a: a for a in range(2 * n)},
        compiler_params=pltpu.CompilerParams(has_side_effects=_EFFECT),
    )(*srcs, *lands, handle["send_sems"], handle["recv_sems"], *after)
    return list(outs[:n]), list(outs[n:])


def _gather_forward(shards, fulls, axes, *, name):
    n = len(shards)
    sizes = [s.shape[ax] for s, ax in zip(shards, axes)]

    def body(*refs):
        srcs = refs[:n]
        outs = refs[2 * n:3 * n]
        send_sems, recv_sems, local_sems = refs[3 * n:]
        x, y, c, chips = _mesh_pos()
        sib = (x, y, 1 - c)

        def blk(a, dev):
            return _block(outs[a], axes[a], 4 * dev[0] + 2 * dev[1] + dev[2], sizes[a])

        mine = [pltpu.make_async_copy(srcs[a], blk(a, (x, y, c)), local_sems.at[a]) for a in range(n)]
        for cp in mine:
            cp.start()
        passed = [pltpu.make_async_remote_copy(src_ref=blk(a, (*chip, c)), dst_ref=blk(a, (*chip, c)), send_sem=send_sems.at[a, j],
                                               recv_sem=recv_sems.at[a, j], device_id=sib, device_id_type=MESH)
                  for a in range(n) for j, chip in enumerate(chips)]
        for cp in passed:
            cp.start()
        for a in range(n):
            for j, chip in enumerate(chips):
                pltpu.make_async_remote_copy(src_ref=blk(a, (*chip, c)), dst_ref=blk(a, (*chip, 1 - c)), send_sem=send_sems.at[a, j],
                                             recv_sem=recv_sems.at[a, j], device_id=sib, device_id_type=MESH).wait_recv()
        for cp in passed:
            cp.wait_send()
        for cp in mine:
            cp.wait()

    return pl.pallas_call(
        body, name=name,
        out_shape=[jax.ShapeDtypeStruct(z.shape, z.dtype) for z in fulls],
        in_specs=[_ANY] * (2 * n), out_specs=[_ANY] * n,
        input_output_aliases={n + a: a for a in range(n)},
        scratch_shapes=[pltpu.SemaphoreType.DMA((n, 3)), pltpu.SemaphoreType.DMA((n, 3)), pltpu.SemaphoreType.DMA((n,))],
    )(*shards, *fulls)


def _rs_exchange_sibling(tensors, axes, *, name):
    n = len(tensors)
    n_layers = [len(t) for t in tensors]
    sizes = [t[0].shape[ax] // N_DEV for t, ax in zip(tensors, axes)]
    blk_shapes = [t[0].shape[:ax] + (sz,) + t[0].shape[ax + 1:] for t, ax, sz in zip(tensors, axes, sizes)]
    flat = [g for t in tensors for g in t]
    offs = [sum(n_layers[:a]) for a in range(n)]

    def body(*refs):
        srcs = refs[:len(flat)]
        lands = refs[len(flat):len(flat) + n]
        send_sems, recv_sems = refs[len(flat) + n:]
        x, y, c, _ = _mesh_pos()
        sib = (x, y, 1 - c)
        for a in range(n):
            for l in range(n_layers[a]):
                for i in range(N_CHIP):
                    pltpu.make_async_remote_copy(
                        src_ref=_block(srcs[offs[a] + l], axes[a], 2 * i + (1 - c), sizes[a]), dst_ref=lands[a].at[i, l],
                        send_sem=send_sems.at[a], recv_sem=recv_sems.at[a], device_id=sib, device_id_type=MESH).start()
        for a in range(n):
            pltpu.make_async_remote_copy(src_ref=lands[a], dst_ref=lands[a], send_sem=send_sems.at[a], recv_sem=recv_sems.at[a],
                                         device_id=sib, device_id_type=MESH).wait()

    any_spec = pl.BlockSpec(memory_space=pl.ANY)
    return pl.pallas_call(
        body, name=name,
        out_shape=[jax.ShapeDtypeStruct((N_CHIP, nl) + bs, t[0].dtype) for nl, bs, t in zip(n_layers, blk_shapes, tensors)],
        in_specs=[any_spec] * len(flat), out_specs=[any_spec] * n,
        scratch_shapes=[pltpu.SemaphoreType.DMA((n,)), pltpu.SemaphoreType.DMA((n,))],
    )(*flat)


def _rs_add_sibling(layers, axis, land, my_c, *, name):
    n_layers = len(layers)
    shape = layers[0].shape
    size = shape[axis] // N_DEV
    blk_shape = shape[:axis] + (size,) + shape[axis + 1:]
    nd = len(shape)
    rows = blk_shape[-2]
    tr = _tile(rows, 512, _row_mult(layers[0].dtype))
    inner = (tr, blk_shape[-1])
    lead = blk_shape[:-2]
    if lead:
        raise ValueError("blocked gradients are 2-D")
    nrb = rows // tr

    def src_map(l):
        def imap(i, r, c_ref):
            j = 2 * i + c_ref[0]
            return (j * nrb + r, 0) if axis == 0 else (r, j)
        return imap

    def body(c_ref, *refs):
        srcs = refs[:n_layers]
        land_ref = refs[n_layers]
        out_ref = refs[n_layers + 1]
        for l in range(n_layers):
            out_ref[l] = (srcs[l][...].astype(F32) + land_ref[l].astype(F32)).astype(out_ref.dtype)

    grid_spec = pltpu.PrefetchScalarGridSpec(
        num_scalar_prefetch=1, grid=(N_CHIP, nrb),
        in_specs=[pl.BlockSpec(inner, src_map(l)) for l in range(n_layers)]
        + [pl.BlockSpec((None, n_layers) + inner, lambda i, r, c_ref: (i, 0, r, 0))],
        out_specs=pl.BlockSpec((None, n_layers) + inner, lambda i, r, c_ref: (i, 0, r, 0)),
    )
    return pl.pallas_call(
        body, name=name, grid_spec=grid_spec,
        out_shape=jax.ShapeDtypeStruct((N_CHIP, n_layers) + blk_shape, layers[0].dtype),
        compiler_params=_params(("arbitrary", "arbitrary")),
    )(my_c, *layers, land)


def _rs_exchange_chips(parts, *, name):
    n = len(parts)

    def body(*refs):
        srcs, lands = refs[:n], refs[n:2 * n]
        send_sems, recv_sems, local_sems = refs[2 * n:]
        x, y, c, chips = _mesh_pos()
        mine = 2 * x + y
        local = [pltpu.make_async_copy(srcs[a].at[mine], lands[a].at[mine], local_sems.at[a]) for a in range(n)]
        for cp in local:
            cp.start()

        def copy(a, j, chip):
            return pltpu.make_async_remote_copy(
                src_ref=srcs[a].at[2 * chip[0] + chip[1]], dst_ref=lands[a].at[mine],
                send_sem=send_sems.at[a, j], recv_sem=recv_sems.at[a, j], device_id=(*chip, c), device_id_type=MESH)

        sends = [copy(a, j, chip) for a in range(n) for j, chip in enumerate(chips)]
        for cp in sends:
            cp.start()
        for a in range(n):
            for j, chip in enumerate(chips):
                pltpu.make_async_remote_copy(
                    src_ref=srcs[a].at[mine], dst_ref=lands[a].at[2 * chip[0] + chip[1]],
                    send_sem=send_sems.at[a, j], recv_sem=recv_sems.at[a, j], device_id=(*chip, c), device_id_type=MESH).wait_recv()
        for cp in sends:
            cp.wait_send()
        for cp in local:
            cp.wait()

    any_spec = pl.BlockSpec(memory_space=pl.ANY)
    return pl.pallas_call(
        body, name=name,
        out_shape=[jax.ShapeDtypeStruct(p.shape, p.dtype) for p in parts],
        in_specs=[any_spec] * n, out_specs=[any_spec] * n,
        scratch_shapes=[pltpu.SemaphoreType.DMA((n, 3)), pltpu.SemaphoreType.DMA((n, 3)), pltpu.SemaphoreType.DMA((n,))],
    )(*parts)


def _rs_place_own(part, *, name):
    def body(src, land, sem):
        x, y, _, _ = _mesh_pos()
        cp = pltpu.make_async_copy(src.at[2 * x + y], land.at[2 * x + y], sem)
        cp.start()
        cp.wait()

    return pl.pallas_call(body, name=name, out_shape=jax.ShapeDtypeStruct(part.shape, part.dtype), in_specs=[_ANY], out_specs=_ANY,
                          scratch_shapes=[pltpu.SemaphoreType.DMA])(part)


def _rs_chips_start(part, land, *, name):
    def body(src, dst, send_sems, recv_sems, src_thru, dst_thru, token):
        x, y, c, chips = _mesh_pos()
        for j, chip in enumerate(chips):
            pltpu.make_async_remote_copy(src_ref=src.at[2 * chip[0] + chip[1]], dst_ref=dst.at[2 * x + y], send_sem=send_sems.at[j],
                                         recv_sem=recv_sems.at[j], device_id=(*chip, c), device_id_type=MESH).start()
        token[...] = jnp.zeros(token.shape, F32)

    outs = pl.pallas_call(
        body, name=name,
        out_shape=(pltpu.SemaphoreType.DMA((3,)), pltpu.SemaphoreType.DMA((3,)), pltpu.HBM(part.shape, part.dtype),
                   pltpu.HBM(land.shape, land.dtype), jax.ShapeDtypeStruct((8, LANES), F32)),
        in_specs=[_HBM, _HBM], out_specs=(_SEM, _SEM, _HBM, _HBM, pl.BlockSpec(memory_space=pltpu.VMEM)),
        input_output_aliases={0: 2, 1: 3},
        compiler_params=pltpu.CompilerParams(has_side_effects=_EFFECT),
    )(_in_hbm(part), _in_hbm(land))
    return dict(send_sems=outs[0], recv_sems=outs[1], src=outs[2], land=outs[3]), outs[4]


def _rs_chips_wait(handle, after, *, name):
    def body(src, dst, send_sems, recv_sems, *rest):
        x, y, c, chips = _mesh_pos()
        for j, chip in enumerate(chips):
            cp = pltpu.make_async_remote_copy(src_ref=src.at[2 * chip[0] + chip[1]], dst_ref=dst.at[2 * chip[0] + chip[1]],
                                              send_sem=send_sems.at[j], recv_sem=recv_sems.at[j], device_id=(*chip, c),
                                              device_id_type=MESH)
            cp.wait_send()
            cp.wait_recv()

    src, land = handle["src"], handle["land"]
    outs = pl.pallas_call(
        body, name=name,
        out_shape=(pltpu.HBM(src.shape, src.dtype), pltpu.HBM(land.shape, land.dtype)),
        in_specs=[_HBM, _HBM, _SEM, _SEM] + [_ANY] * len(after), out_specs=(_HBM, _HBM),
        input_output_aliases={0: 0, 1: 1},
        compiler_params=pltpu.CompilerParams(has_side_effects=_EFFECT),
    )(src, land, handle["send_sems"], handle["recv_sems"], *after)
    return outs[1]


def _adam_math(w, g, m, v):
    m_new = ADAM_B1 * m + (1.0 - ADAM_B1) * g
    v_new = ADAM_B2 * v + (1.0 - ADAM_B2) * (g * g)
    m_hat = m_new / (1.0 - ADAM_B1 ** ADAM_STEP)
    v_hat = v_new / (1.0 - ADAM_B2 ** ADAM_STEP)
    delta = -ADAM_LR * (m_hat / (jnp.sqrt(v_hat) + ADAM_EPS) + ADAM_WD * w)
    return delta, m_new, v_new


def _sum_adam(land, w, m, v, layer, prev, *, name):
    n_layers, rows, cols = w.shape
    tr = _tile(rows, 256, _row_mult(land.dtype))
    n_prev = 0 if prev is None else 4

    def body(land_ref, w_ref, m_ref, v_ref, *rest):
        g_ref, d_ref, mo_ref, vo_ref = rest[n_prev:]
        g = land_ref[0].astype(F32)
        for i in range(1, N_CHIP):
            g = g + land_ref[i].astype(F32)
        delta, m_new, v_new = _adam_math(w_ref[...], g, m_ref[...], v_ref[...])
        g_ref[...] = g
        d_ref[...] = delta
        mo_ref[...] = m_new
        vo_ref[...] = v_new

    blk = pl.BlockSpec((None, tr, cols), lambda r: (layer, r, 0))
    shp = jax.ShapeDtypeStruct((n_layers, rows, cols), F32)
    return pl.pallas_call(
        body, name=name, out_shape=(shp, shp, shp, shp), grid=(rows // tr,),
        in_specs=[pl.BlockSpec((N_CHIP, None, tr, cols), lambda r: (0, 0, r, 0)), blk, blk, blk] + [_ANY] * n_prev,
        out_specs=(blk, blk, blk, blk),
        input_output_aliases={4 + i: i for i in range(n_prev)},
        compiler_params=_params(("parallel",)),
    )(land, w, m, v, *(prev or ()))


def _sum_slots(land, *, name):
    _, _, rows, cols = land.shape
    tr = _tile(rows, 256, 8)

    def body(land_ref, g_ref):
        g = land_ref[0]
        for i in range(1, N_CHIP):
            g = g + land_ref[i]
        g_ref[...] = g

    return pl.pallas_call(
        body, name=name, out_shape=jax.ShapeDtypeStruct((rows, cols), F32), grid=(rows // tr,),
        in_specs=[pl.BlockSpec((N_CHIP, None, tr, cols), lambda r: (0, 0, r, 0))],
        out_specs=pl.BlockSpec((tr, cols), lambda r: (r, 0)),
        compiler_params=_params(("parallel",)),
    )(land)


def _adam_flat(w, g, m, v, *, name):
    rows, cols = w.shape
    tr = _tile(rows, 256, 8)

    def body(w_ref, g_ref, m_ref, v_ref, d_ref, mo_ref, vo_ref):
        delta, m_new, v_new = _adam_math(w_ref[...], g_ref[...], m_ref[...], v_ref[...])
        d_ref[...] = delta
        mo_ref[...] = m_new
        vo_ref[...] = v_new

    blk = pl.BlockSpec((tr, cols), lambda r: (r, 0))
    shp = jax.ShapeDtypeStruct((rows, cols), F32)
    return pl.pallas_call(
        body, name=name, out_shape=(shp, shp, shp), grid=(rows // tr,), in_specs=[blk] * 4, out_specs=(blk, blk, blk),
        compiler_params=_params(("parallel",)),
    )(w, g, m, v)


_BIG = (("ffn1_w_in", 1), ("ffn1_w_out", 0), ("w_in", 1), ("w_out", 0), ("ffn2_w_in", 1), ("ffn2_w_out", 0))
_SMALL = ("meta_tokens", "ffn1_norm", "mix_norm", "conv_w", "conv_b", "lru_wa", "lru_ba", "lru_wx", "lru_bx", "lru_a_param",
          "pool_w", "pool_b", "pool_scale", "ffn2_norm", "final_norm")
_SMALL_SHARD_AXIS = {"meta_tokens": 1, "conv_w": 2, "pool_w": 2}
_PACK_COLS = 1024


def _pack(arrs):
    flat = jnp.concatenate([a.reshape(-1) for a in arrs])
    unit = N_DEV * 8 * _PACK_COLS
    total = -(-flat.shape[0] // unit) * unit
    flat = jnp.pad(flat, (0, total - flat.shape[0]))
    return flat.reshape(total // _PACK_COLS, _PACK_COLS)


def _unpack(packed, shapes):
    flat = packed.reshape(-1)
    out, off = [], 0
    for s in shapes:
        size = 1
        for v in s:
            size *= v
        out.append(flat[off:off + size].reshape(s))
        off += size
    return out


def _my_shard(full, axis, dev):
    size = full.shape[axis] // N_DEV
    return lax.dynamic_slice_in_dim(full, dev * size, size, axis)


def kernel(x, meta_tokens, ffn1_norm, ffn1_w_in, ffn1_w_out, mix_norm, w_in, conv_w, conv_b, lru_wa, lru_ba, lru_wx, lru_bx, lru_a_param, pool_w, pool_b, pool_scale, w_out, ffn2_norm, ffn2_w_in, ffn2_w_out, final_norm, loss_target, m_meta_tokens, m_ffn1_norm, m_ffn1_w_in, m_ffn1_w_out, m_mix_norm, m_w_in, m_conv_w, m_conv_b, m_lru_wa, m_lru_ba, m_lru_wx, m_lru_bx, m_lru_a_param, m_pool_w, m_pool_b, m_pool_scale, m_w_out, m_ffn2_norm, m_ffn2_w_in, m_ffn2_w_out, m_final_norm, v_meta_tokens, v_ffn1_norm, v_ffn1_w_in, v_ffn1_w_out, v_mix_norm, v_w_in, v_conv_w, v_conv_b, v_lru_wa, v_lru_ba, v_lru_wx, v_lru_bx, v_lru_a_param, v_pool_w, v_pool_b, v_pool_scale, v_w_out, v_ffn2_norm, v_ffn2_w_in, v_ffn2_w_out, v_final_norm):
    names = ("meta_tokens", "ffn1_norm", "ffn1_w_in", "ffn1_w_out", "mix_norm", "w_in", "conv_w", "conv_b", "lru_wa", "lru_ba",
             "lru_wx", "lru_bx", "lru_a_param", "pool_w", "pool_b", "pool_scale", "w_out", "ffn2_norm", "ffn2_w_in", "ffn2_w_out",
             "final_norm")
    w = dict(zip(names, (meta_tokens, ffn1_norm, ffn1_w_in, ffn1_w_out, mix_norm, w_in, conv_w, conv_b, lru_wa, lru_ba, lru_wx,
                         lru_bx, lru_a_param, pool_w, pool_b, pool_scale, w_out, ffn2_norm, ffn2_w_in, ffn2_w_out, final_norm)))
    mom = dict(zip(names, (m_meta_tokens, m_ffn1_norm, m_ffn1_w_in, m_ffn1_w_out, m_mix_norm, m_w_in, m_conv_w, m_conv_b, m_lru_wa,
                           m_lru_ba, m_lru_wx, m_lru_bx, m_lru_a_param, m_pool_w, m_pool_b, m_pool_scale, m_w_out, m_ffn2_norm,
                           m_ffn2_w_in, m_ffn2_w_out, m_final_norm)))
    vel = dict(zip(names, (v_meta_tokens, v_ffn1_norm, v_ffn1_w_in, v_ffn1_w_out, v_mix_norm, v_w_in, v_conv_w, v_conv_b, v_lru_wa,
                           v_lru_ba, v_lru_wx, v_lru_bx, v_lru_a_param, v_pool_w, v_pool_b, v_pool_scale, v_w_out, v_ffn2_norm,
                           v_ffn2_w_in, v_ffn2_w_out, v_final_norm)))
    n_layers, d = ffn1_norm.shape
    n_meta = meta_tokens.shape[0]
    seq = x.shape[1]
    t_valid = n_meta + seq
    tp = -(-t_valid // ROW_ALIGN) * ROW_ALIGN
    dev = 4 * lax.axis_index("x") + 2 * lax.axis_index("y") + lax.axis_index("c")
    my_c = lax.axis_index("c").astype(jnp.int32).reshape(1)

    big_axis = dict(_BIG)
    vec = lambda a: a.reshape(a.shape[0], 1, a.shape[1])
    norms = dict(ffn1=vec(ffn1_norm), mix=vec(mix_norm), ffn2=vec(ffn2_norm))
    units = [(kind, l) for l in range(n_layers) for kind in ("ffn1", "mix", "ffn2")]

    def shard(k, l):
        return w[k][l:l + 1].astype(BF16), big_axis[k] + 1

    groups = [[shard("ffn1_w_in", 0), (meta_tokens, 1)], [shard("ffn1_w_out", 0)]]
    for kind, l in units[1:]:
        if kind == "mix":
            groups.append([shard("w_in", l), shard("w_out", l)] + ([(conv_w, 2), (pool_w, 2)] if l == 0 else []))
        else:
            groups.append([shard(f"{kind}_w_in", l), shard(f"{kind}_w_out", l)])
    handles = {}
    state = dict(token=None, started=0)

    def order():
        return [] if state["token"] is None else [state["token"]]

    def start_next():
        i = state["started"]
        if i < len(groups):
            handles[i], state["token"] = _gather_start([s for s, _ in groups[i]], [ax for _, ax in groups[i]], order(),
                                                       name=f"gather{i}_start")
            state["started"] = i + 1

    def finish(i, after):
        handle = handles.pop(i)
        srcs, fulls = _gather_wait(handle, after, name=f"gather{i}_wait")
        fulls = _gather_forward(srcs, fulls, handle["axes"], name=f"gather{i}_pass")
        start_next()
        return fulls

    start_next()
    start_next()
    w_in0, meta_full = finish(0, [])
    pad = jnp.zeros((tp - t_valid, d), F32)
    h0 = jnp.concatenate([meta_full, x[0], pad], axis=0)
    tgt = jnp.concatenate([jnp.zeros((n_meta, d), F32), loss_target[0], pad], axis=0)
    n_act, gu, act = _ffn_fwd_in(h0, norms["ffn1"], 0, w_in0, "l0ffn1", order())
    (w_out0,) = finish(1, [act])
    h = _ffn_fwd_out(h0, act, w_out0, "l0ffn1")
    gathered = {units[0]: (w_in0, w_out0)}
    saved = {units[0]: (h0, n_act, gu)}
    mp = None
    for ui, (kind, l) in enumerate(units[1:], start=1):
        fulls = finish(ui + 1, [h])
        tag = f"l{l}{kind}"
        if kind == "mix":
            if l == 0:
                mp = dict(conv_w=fulls[2], conv_b=vec(conv_b), wa=lru_wa.astype(BF16), ba=vec(lru_ba), wx=lru_wx.astype(BF16),
                          bx=vec(lru_bx), a_param=vec(lru_a_param), pool_w=fulls[3].astype(BF16), pool_b=vec(pool_b),
                          pool_scale=vec(pool_scale))
            h, saved[(kind, l)] = _mix_block_fwd(h, norms["mix"], l, fulls[0], fulls[1], mp, tag, order())
        else:
            n_act, gu, act = _ffn_fwd_in(h, norms[kind], l, fulls[0], tag, order())
            saved[(kind, l)] = (h, n_act, gu)
            h = _ffn_fwd_out(h, act, fulls[1], tag)
        gathered[(kind, l)] = (fulls[0], fulls[1])
    dh, dhb, g_final, loss_local = _final_loss(h, final_norm.reshape(1, d), tgt, n_meta, t_valid, name="final_loss")
    loss = lax.psum(loss_local[0, 0], ("x", "y", "c"))

    pending = []
    big_out = {}

    def drain(after):
        while pending:
            k, l, handle = pending.pop(0)
            land = _rs_chips_wait(handle, after, name=f"rs_{k}{l}_wait")
            big_out[k] = _sum_adam(land, w[k], mom[k], vel[k], l, big_out.get(k), name=f"adam_{k}{l}")

    def emitter(kind, l):
        def emit(which, g):
            k = which if kind == "mix" else f"{kind}_{which}"
            if which == "w_out":
                drain([g])
            (land,) = _rs_exchange_sibling([[g]], [big_axis[k]], name=f"rs_{k}{l}_sib")
            part = _rs_add_sibling([g], big_axis[k], land, my_c, name=f"rs_{k}{l}_add")
            handle, token = _rs_chips_start(part, _rs_place_own(part, name=f"rs_{k}{l}_own"), name=f"rs_{k}{l}_start")
            pending.append((k, l, handle))
            return token
        return emit

    g_norm = {k: [None] * n_layers for k in ("ffn1", "mix", "ffn2")}
    g_mix = [None] * n_layers
    for kind, l in reversed(units):
        w_i, w_o = gathered[(kind, l)]
        tag = f"l{l}{kind}"
        if kind == "mix":
            dh, dhb, g_norm[kind][l], g_mix[l] = _mix_block_bwd(dh, dhb, saved[(kind, l)], norms[kind], l, w_i, w_o, mp, tag,
                                                                 emitter(kind, l), [])
        else:
            dh, dhb, g_norm[kind][l] = _ffn_bwd(dh, dhb, saved[(kind, l)], norms[kind], l, w_i, w_o, tag, emitter(kind, l), [])
    drain([dh])
    dh0 = dh
    grad_x = dh0[n_meta:t_valid][None]

    mix_key = dict(conv_w="conv_w", conv_b="conv_b", lru_wa="wa", lru_ba="ba", lru_wx="wx", lru_bx="bx", lru_a_param="a_param",
                   pool_w="pool_w", pool_b="pool_b", pool_scale="pool_scale")
    small_local = {"meta_tokens": dh0[:n_meta], "final_norm": g_final.reshape(d)}
    for k in ("ffn1", "mix", "ffn2"):
        small_local[f"{k}_norm"] = jnp.stack([g.reshape(d) for g in g_norm[k]])
    for k, mk in mix_key.items():
        small_local[k] = jnp.stack([g_mix[l][mk] for l in range(n_layers)]).reshape(
            w[k].shape if k not in _SMALL_SHARD_AXIS else small_shape_full(w[k], _SMALL_SHARD_AXIS[k]))
    small_shapes = [small_local[k].shape for k in _SMALL]
    packed = _pack([small_local[k] for k in _SMALL])
    (land,) = _rs_exchange_sibling([[packed]], [0], name="rs_small_sib")
    part = _rs_add_sibling([packed], 0, land, my_c, name="rs_small_add")
    (land2,) = _rs_exchange_chips([part], name="rs_small_chips")

    out_g, out_d, out_m, out_v = {}, {}, {}, {}
    for k, _ in _BIG:
        out_g[k], out_d[k], out_m[k], out_v[k] = big_out[k]
    small_block = _sum_slots(land2, name="rs_sum_small")
    small_full = _all_gather([small_block], [0], name="gather_small")[0]
    small_g = dict(zip(_SMALL, _unpack(small_full, small_shapes)))
    for k, ax in _SMALL_SHARD_AXIS.items():
        small_g[k] = _my_shard(small_g[k], ax, dev)
    shapes_local = [w[k].shape for k in _SMALL]
    d_p, m_p, v_p = _adam_flat(_pack([w[k] for k in _SMALL]), _pack([small_g[k] for k in _SMALL]),
                               _pack([mom[k] for k in _SMALL]), _pack([vel[k] for k in _SMALL]), name="adam_small")
    for k, dd, mm, vv in zip(_SMALL, _unpack(d_p, shapes_local), _unpack(m_p, shapes_local), _unpack(v_p, shapes_local)):
        out_g[k], out_d[k], out_m[k], out_v[k] = small_g[k], dd, mm, vv

    return (loss, grad_x, *[out_g[k] for k in names], *[out_d[k] for k in names], *[out_m[k] for k in names],
            *[out_v[k] for k in names])


def small_shape_full(w_shard, axis):
    return w_shard.shape[:axis] + (N_DEV * w_shard.shape[axis],) + w_shard.shape[axis + 1:]
```

```python
import functools

import jax
import jax.numpy as jnp
from jax import lax
from jax.experimental import pallas as pl
from jax.experimental.pallas import tpu as pltpu

F32 = jnp.float32
BF16 = jnp.bfloat16
MESH = pl.DeviceIdType.MESH

N_DEV = 8
N_CHIP = 4
RMS_EPS = 1e-6
LRU_C = 8.0
CONV_WIDTH = 4
POOL_WINDOWS = (2, 4, 8, 16)
HIST = 16
ADAM_LR = 0.001
ADAM_B1 = 0.9
ADAM_B2 = 0.999
ADAM_EPS = 1e-08
ADAM_WD = 0.01
ADAM_STEP = 10
ROW_ALIGN = 128
V7X_VMEM_LIMIT = 56 * 1024 * 1024
BF16_ROWS = 16
LANES = 128


def _tile(n, cap, mult=BF16_ROWS):
    best = None
    d = mult
    while d <= min(n, cap):
        if n % d == 0:
            best = d
        d += mult
    if best is None:
        raise ValueError(f"no tile for {n} (cap {cap}, multiple of {mult})")
    return best


def _row_mult(dtype):
    return 8 * 4 // jnp.dtype(dtype).itemsize


def _params(sem=None):
    return pltpu.CompilerParams(dimension_semantics=sem, vmem_limit_bytes=V7X_VMEM_LIMIT)


def _dot(a, b, mode):
    dims = {"nn": ((1,), (0,)), "nt": ((1,), (1,)), "tn": ((0,), (0,))}[mode]
    return lax.dot_general(a, b, (dims, ((), ())), preferred_element_type=F32)


def _sigmoid(x):
    return 1.0 / (1.0 + jnp.exp(-x))


_GELU_C = 0.7978845608028654
_GELU_K = 0.044715


def _gelu(x):
    return 0.5 * x * (1.0 + jnp.tanh(_GELU_C * (x + _GELU_K * x * x * x)))


def _gelu_grad(x):
    th = jnp.tanh(_GELU_C * (x + _GELU_K * x * x * x))
    return 0.5 * (1.0 + th) + 0.5 * x * (1.0 - th * th) * _GELU_C * (1.0 + 3.0 * _GELU_K * x * x)


def _neg_expm1(x):
    p = 1.0 / 3628800.0
    for c in (1.0 / 362880.0, 1.0 / 40320.0, 1.0 / 5040.0, 1.0 / 720.0, 1.0 / 120.0, 1.0 / 24.0, 1.0 / 6.0, 0.5, 1.0):
        p = p * x + c
    return jnp.where(x > -0.5, -(x * p), 1.0 - jnp.exp(x))


def _softplus_neg(p):
    e = jnp.exp(-jnp.abs(p))
    u = 1.0 + e
    l1p = jnp.where(u == 1.0, e, jnp.log(u) * e / (u - 1.0 + (u == 1.0).astype(F32)))
    return jnp.maximum(-p, 0.0) + l1p


def _operand_spec(arr, br, bc, ridx, cidx, layer, split):
    if split:
        ncb = arr.shape[2] // bc
        return pl.BlockSpec((None, br, bc), lambda i, j, k: (cidx(i, j, k) // ncb, ridx(i, j, k), cidx(i, j, k) % ncb))
    if layer is not None:
        return pl.BlockSpec((None, br, bc), lambda i, j, k: (layer, ridx(i, j, k), cidx(i, j, k)))
    return pl.BlockSpec((br, bc), lambda i, j, k: (ridx(i, j, k), cidx(i, j, k)))


def _matmul(a, b, *, mode, m, n, kdim, tm, tn, tk, out_dtype, name, scale=None, residual=None,
            a_layer=None, b_layer=None, a_split=False, b_split=False, deps=()):
    nk = kdim // tk
    I = lambda i, j, k: i
    J = lambda i, j, k: j
    K = lambda i, j, k: k
    if mode == "nn":
        a_spec = _operand_spec(a, tm, tk, I, K, a_layer, a_split)
        b_spec = _operand_spec(b, tk, tn, K, J, b_layer, b_split)
    elif mode == "nt":
        a_spec = _operand_spec(a, tm, tk, I, K, a_layer, a_split)
        b_spec = _operand_spec(b, tn, tk, J, K, b_layer, b_split)
    else:
        a_spec = _operand_spec(a, tk, tm, K, I, a_layer, a_split)
        b_spec = _operand_spec(b, tk, tn, K, J, b_layer, b_split)
    in_specs = [a_spec, b_spec]
    operands = [a, b]
    if residual is not None:
        in_specs.append(pl.BlockSpec((tm, tn), lambda i, j, k: (i, j)))
        operands.append(residual)
    in_specs += [pl.BlockSpec(memory_space=pl.ANY)] * len(deps)
    operands += list(deps)
    n_in = len(operands)

    def finish(acc, res_ref, o_ref):
        if scale is not None:
            acc = acc * scale
        if res_ref is not None:
            acc = acc + res_ref[...]
        o_ref[...] = acc.astype(o_ref.dtype)

    def body(*refs):
        a_ref, b_ref = refs[0], refs[1]
        res_ref = refs[2] if residual is not None else None
        o_ref = refs[n_in]
        part = _dot(a_ref[...], b_ref[...], mode)
        if nk == 1:
            finish(part, res_ref, o_ref)
        else:
            acc_ref = refs[-1]
            k = pl.program_id(2)

            @pl.when(k == 0)
            def _():
                acc_ref[...] = part

            @pl.when(k > 0)
            def _():
                acc_ref[...] += part

            @pl.when(k == nk - 1)
            def _():
                finish(acc_ref[...], res_ref, o_ref)

    return pl.pallas_call(
        body, name=name,
        out_shape=jax.ShapeDtypeStruct((m, n), out_dtype),
        grid=(m // tm, n // tn, nk),
        in_specs=in_specs,
        out_specs=pl.BlockSpec((tm, tn), lambda i, j, k: (i, j)),
        scratch_shapes=[pltpu.VMEM((tm, tn), F32)] if nk > 1 else [],
        compiler_params=_params(("parallel", "parallel", "arbitrary")),
    )(*operands)


def _ffn_in_fwd(n_act, w_in, layer, *, name):
    tp, d = n_act.shape
    f = w_in.shape[2] // 2
    tm = _tile(tp, 2112)
    tn = _tile(f, 256, LANES)
    nj = f // tn

    def body(n_ref, wg_ref, wu_ref, gu_ref, a_ref):
        x = n_ref[...]
        g = _dot(x, wg_ref[...], "nn")
        u = _dot(x, wu_ref[...], "nn")
        gu_ref[0] = g.astype(BF16)
        gu_ref[1] = u.astype(BF16)
        a_ref[...] = (g * _sigmoid(g) * u).astype(BF16)

    return pl.pallas_call(
        body, name=name,
        out_shape=(jax.ShapeDtypeStruct((2, tp, f), BF16), jax.ShapeDtypeStruct((tp, f), BF16)),
        grid=(tp // tm, nj),
        in_specs=[pl.BlockSpec((tm, d), lambda i, j: (i, 0)),
                  pl.BlockSpec((None, d, tn), lambda i, j: (layer, 0, j)),
                  pl.BlockSpec((None, d, tn), lambda i, j: (layer, 0, j + nj))],
        out_specs=(pl.BlockSpec((2, tm, tn), lambda i, j: (0, i, j)), pl.BlockSpec((tm, tn), lambda i, j: (i, j))),
        compiler_params=_params(("parallel", "arbitrary")),
    )(n_act, w_in, w_in)


def _ffn_bwd_act(dhb, w_out, layer, gu, *, name, deps=()):
    tp, d = dhb.shape
    f = w_out.shape[1]
    tm = _tile(tp, 2112)
    tn = _tile(f, 256, LANES)

    def body(dh_ref, w_ref, gu_ref, *rest):
        dz_ref, a_ref = rest[len(deps):]
        da = 0.5 * _dot(dh_ref[...], w_ref[...], "nt")
        g = gu_ref[0].astype(F32)
        u = gu_ref[1].astype(F32)
        s = _sigmoid(g)
        sg = g * s
        dz_ref[0] = (da * u * (s * (1.0 + g * (1.0 - s)))).astype(BF16)
        dz_ref[1] = (da * sg).astype(BF16)
        a_ref[...] = (sg * u).astype(BF16)

    return pl.pallas_call(
        body, name=name,
        out_shape=(jax.ShapeDtypeStruct((2, tp, f), BF16), jax.ShapeDtypeStruct((tp, f), BF16)),
        grid=(tp // tm, f // tn),
        in_specs=[pl.BlockSpec((tm, d), lambda i, j: (i, 0)),
                  pl.BlockSpec((None, tn, d), lambda i, j: (layer, j, 0)),
                  pl.BlockSpec((2, tm, tn), lambda i, j: (0, i, j))] + [pl.BlockSpec(memory_space=pl.ANY)] * len(deps),
        out_specs=(pl.BlockSpec((2, tm, tn), lambda i, j: (0, i, j)), pl.BlockSpec((tm, tn), lambda i, j: (i, j))),
        compiler_params=_params(("parallel", "arbitrary")),
    )(dhb, w_out, gu, *deps)


def _rms_fwd(h, g, layer, *, name, deps=()):
    tp, d = h.shape
    tr = _tile(tp, 528)

    def body(h_ref, g_ref, *rest):
        n_ref = rest[len(deps)]
        x = h_ref[...]
        r = lax.rsqrt(jnp.mean(x * x, axis=-1, keepdims=True) + RMS_EPS)
        n_ref[...] = (x * r * g_ref[...]).astype(BF16)

    return pl.pallas_call(
        body, name=name, out_shape=jax.ShapeDtypeStruct((tp, d), BF16), grid=(tp // tr,),
        in_specs=[pl.BlockSpec((tr, d), lambda i: (i, 0)), pl.BlockSpec((None, 1, d), lambda i: (layer, 0, 0))]
        + [pl.BlockSpec(memory_space=pl.ANY)] * len(deps),
        out_specs=pl.BlockSpec((tr, d), lambda i: (i, 0)),
        compiler_params=_params(("parallel",)),
    )(h, g, *deps)


def _rms_bwd(h, g, layer, dn, dres, *, name):
    tp, d = h.shape
    tr = _tile(tp, 528)

    def body(h_ref, g_ref, dn_ref, dres_ref, dh_ref, dhb_ref, dg_ref):
        x = h_ref[...]
        r = lax.rsqrt(jnp.mean(x * x, axis=-1, keepdims=True) + RMS_EPS)
        xhat = x * r
        dn_v = dn_ref[...]
        dxhat = dn_v * g_ref[...]
        dh = dres_ref[...] + r * (dxhat - xhat * jnp.mean(dxhat * xhat, axis=-1, keepdims=True))
        dh_ref[...] = dh
        dhb_ref[...] = dh.astype(BF16)
        part = jnp.sum(dn_v * xhat, axis=0, keepdims=True)

        @pl.when(pl.program_id(0) == 0)
        def _():
            dg_ref[...] = part

        @pl.when(pl.program_id(0) > 0)
        def _():
            dg_ref[...] += part

    return pl.pallas_call(
        body, name=name,
        out_shape=(jax.ShapeDtypeStruct((tp, d), F32), jax.ShapeDtypeStruct((tp, d), BF16), jax.ShapeDtypeStruct((1, d), F32)),
        grid=(tp // tr,),
        in_specs=[pl.BlockSpec((tr, d), lambda i: (i, 0)), pl.BlockSpec((None, 1, d), lambda i: (layer, 0, 0)),
                  pl.BlockSpec((tr, d), lambda i: (i, 0)), pl.BlockSpec((tr, d), lambda i: (i, 0))],
        out_specs=(pl.BlockSpec((tr, d), lambda i: (i, 0)), pl.BlockSpec((tr, d), lambda i: (i, 0)),
                   pl.BlockSpec((1, d), lambda i: (0, 0))),
        compiler_params=_params(("arbitrary",)),
    )(h, g, dn, dres)


def _final_loss(h, g, tgt, n_meta, t_valid, *, name):
    tp, d = h.shape
    tr = _tile(tp, 528)

    def body(h_ref, g_ref, t_ref, dh_ref, dhb_ref, dg_ref, loss_ref):
        i = pl.program_id(0)
        x = h_ref[...]
        r = lax.rsqrt(jnp.mean(x * x, axis=-1, keepdims=True) + RMS_EPS)
        xhat = x * r
        gv = g_ref[...]
        row = i * tr + lax.broadcasted_iota(jnp.int32, (tr, 1), 0)
        valid = jnp.logical_and(row >= n_meta, row < t_valid)
        err = jnp.where(valid, xhat * gv - t_ref[...], 0.0)
        dy = err * (1.0 / d)
        dxhat = dy * gv
        dh = r * (dxhat - xhat * jnp.mean(dxhat * xhat, axis=-1, keepdims=True))
        dh_ref[...] = dh
        dhb_ref[...] = dh.astype(BF16)
        dg_part = jnp.sum(dy * xhat, axis=0, keepdims=True)
        loss_part = jnp.sum(jnp.sum(err * err, axis=1, keepdims=True), axis=0, keepdims=True) * (0.5 / d)

        @pl.when(i == 0)
        def _():
            dg_ref[...] = dg_part
            loss_ref[...] = loss_part

        @pl.when(i > 0)
        def _():
            dg_ref[...] += dg_part
            loss_ref[...] += loss_part

    return pl.pallas_call(
        body, name=name,
        out_shape=(jax.ShapeDtypeStruct((tp, d), F32), jax.ShapeDtypeStruct((tp, d), BF16),
                   jax.ShapeDtypeStruct((1, d), F32), jax.ShapeDtypeStruct((1, 1), F32)),
        grid=(tp // tr,),
        in_specs=[pl.BlockSpec((tr, d), lambda i: (i, 0)), pl.BlockSpec((1, d), lambda i: (0, 0)),
                  pl.BlockSpec((tr, d), lambda i: (i, 0))],
        out_specs=(pl.BlockSpec((tr, d), lambda i: (i, 0)), pl.BlockSpec((tr, d), lambda i: (i, 0)),
                   pl.BlockSpec((1, d), lambda i: (0, 0)), pl.BlockSpec((1, 1), lambda i: (0, 0))),
        compiler_params=_params(("arbitrary",)),
    )(h, g, tgt)


def _mix_param_specs(mp, layer, imap):
    def vec(a):
        return pl.BlockSpec((None,) + a.shape[1:], lambda *g: (layer,) + (0,) * (a.ndim - 1))
    return [vec(mp[k]) for k in ("conv_w", "conv_b", "wa", "ba", "wx", "bx", "a_param", "pool_w", "pool_b", "pool_scale")]


def _mix_param_list(mp):
    return [mp[k] for k in ("conv_w", "conv_b", "wa", "ba", "wx", "bx", "a_param", "pool_w", "pool_b", "pool_scale")]


def _lru_gates(xc, wa_h, ba_h, wx_h, bx_h, sp_h):
    xb = xc.astype(BF16)
    ra = _sigmoid(_dot(xb, wa_h, "nn") + ba_h)
    ii = _sigmoid(_dot(xb, wx_h, "nn") + bx_h)
    la = -LRU_C * ra * sp_h
    a = jnp.exp(la)
    mult = jnp.sqrt(_neg_expm1(2.0 * la))
    return xb, ra, ii, a, mult


def _conv_fwd(xbuf, cw_ref, cb_ref, sl, tc):
    xc = cb_ref[:, sl]
    for k in range(CONV_WIDTH):
        xc = xc + cw_ref[k:k + 1, sl] * xbuf[pl.ds(HIST - (CONV_WIDTH - 1) + k, tc), sl]
    return xc


def _pool_delta(pbuf, cols, win, t0, tc):
    u = pbuf[pl.ds(HIST, tc), cols]
    s = u
    for k in range(1, win):
        s = s + pbuf[pl.ds(HIST - k, tc), cols]
    t = t0 + lax.broadcasted_iota(jnp.int32, (tc, 1), 0)
    cnt = jnp.minimum(t + 1, win).astype(F32)
    return s / cnt - u, cnt


def _mix_fwd(z, mp, layer, *, name):
    tp = z.shape[0]
    dl = z.shape[1] // 3
    n_heads, hd = mp["wa"].shape[1], mp["wa"].shape[2]
    n_groups, gd = mp["pool_w"].shape[1], mp["pool_w"].shape[2]
    tc = _tile(tp, 384)

    def body(z_ref, cw_ref, cb_ref, wa_ref, ba_ref, wx_ref, bx_ref, ap_ref, pw_ref, pb_ref, ps_ref,
             m_ref, hs_ref, xbuf, pbuf, a_s, b_s, hcar):
        i = pl.program_id(0)

        @pl.when(i == 0)
        def _():
            xbuf[pl.ds(0, HIST), :] = jnp.zeros((HIST, dl), F32)
            pbuf[pl.ds(0, HIST), :] = jnp.zeros((HIST, dl), F32)
            hcar[...] = jnp.zeros((1, dl), F32)

        @pl.when(i > 0)
        def _():
            xbuf[pl.ds(0, HIST), :] = xbuf[pl.ds(tc, HIST), :]
            pbuf[pl.ds(0, HIST), :] = pbuf[pl.ds(tc, HIST), :]

        xbuf[pl.ds(HIST, tc), :] = z_ref[:, pl.ds(0, dl)]
        pbuf[pl.ds(HIST, tc), :] = z_ref[:, pl.ds(2 * dl, dl)]
        sp = _softplus_neg(ap_ref[...])
        for h in range(n_heads):
            sl = pl.ds(h * hd, hd)
            xc = _conv_fwd(xbuf, cw_ref, cb_ref, sl, tc)
            _, _, ii, a, mult = _lru_gates(xc, wa_ref[h], ba_ref[:, sl], wx_ref[h], bx_ref[:, sl], sp[:, h * hd:(h + 1) * hd])
            a_s[:, sl] = a
            b_s[:, sl] = mult * ii * xc

        def step(t, hprev):
            hnew = a_s[pl.ds(t, 1), :] * hprev + b_s[pl.ds(t, 1), :]
            hs_ref[pl.ds(t, 1), :] = hnew
            return hnew

        hcar[...] = lax.fori_loop(0, tc, step, hcar[...])
        for h in range(n_heads):
            sl = pl.ds(h * hd, hd)
            m_ref[:, sl] = (hs_ref[:, sl] * _gelu(z_ref[:, pl.ds(dl + h * hd, hd)])).astype(BF16)
        for g in range(n_groups):
            cols = pl.ds(g * gd, gd)
            dlt, _ = _pool_delta(pbuf, cols, POOL_WINDOWS[g], i * tc, tc)
            q = _dot(dlt.astype(BF16), pw_ref[g], "nn") + pb_ref[:, cols]
            m_ref[:, pl.ds(dl + g * gd, gd)] = (q * ps_ref[:, cols]).astype(BF16)

    return pl.pallas_call(
        body, name=name,
        out_shape=(jax.ShapeDtypeStruct((tp, 2 * dl), BF16), jax.ShapeDtypeStruct((tp, dl), F32)),
        grid=(tp // tc,),
        in_specs=[pl.BlockSpec((tc, 3 * dl), lambda i: (i, 0))] + _mix_param_specs(mp, layer, None),
        out_specs=(pl.BlockSpec((tc, 2 * dl), lambda i: (i, 0)), pl.BlockSpec((tc, dl), lambda i: (i, 0))),
        scratch_shapes=[pltpu.VMEM((HIST + tc, dl), F32), pltpu.VMEM((HIST + tc, dl), F32),
                        pltpu.VMEM((tc, dl), F32), pltpu.VMEM((tc, dl), F32), pltpu.VMEM((1, dl), F32)],
        compiler_params=_params(("arbitrary",)),
    )(z, *_mix_param_list(mp))


_MIX_GRADS = ("conv_w", "conv_b", "wa", "ba", "wx", "bx", "a_param", "pool_w", "pool_b", "pool_scale")


def _mix_bwd(z, hs, dm, mp, layer, *, name):
    tp = z.shape[0]
    dl = z.shape[1] // 3
    n_heads, hd = mp["wa"].shape[1], mp["wa"].shape[2]
    n_groups, gd = mp["pool_w"].shape[1], mp["pool_w"].shape[2]
    tc = _tile(tp, 384)
    nc = tp // tc
    per = tc // HIST

    def body(z_ref, zp_ref, hs_ref, hsp_ref, dm_ref, cw_ref, cb_ref, wa_ref, ba_ref, wx_ref, bx_ref, ap_ref, pw_ref, pb_ref,
             ps_ref, dz_ref, dcw_ref, dcb_ref, dwa_ref, dba_ref, dwx_ref, dbx_ref, dap_ref, dpw_ref, dpb_ref, dps_ref,
             xbuf, pbuf, hbuf, dxbuf, ddbuf, a_s, lam_s, ra_s, ii_s, xc_s, ccar):
        i = pl.program_id(0)
        ci = nc - 1 - i

        @pl.when(i == 0)
        def _():
            dxbuf[pl.ds(tc, HIST), :] = jnp.zeros((HIST, dl), F32)
            ddbuf[pl.ds(tc, HIST), :] = jnp.zeros((HIST, dl), F32)
            ccar[...] = jnp.zeros((1, dl), F32)
            for ref in (dcw_ref, dcb_ref, dwa_ref, dba_ref, dwx_ref, dbx_ref, dap_ref, dpw_ref, dpb_ref, dps_ref):
                ref[...] = jnp.zeros(ref.shape, F32)

        @pl.when(ci == 0)
        def _():
            xbuf[pl.ds(0, HIST), :] = jnp.zeros((HIST, dl), F32)
            pbuf[pl.ds(0, HIST), :] = jnp.zeros((HIST, dl), F32)
            hbuf[pl.ds(0, HIST), :] = jnp.zeros((HIST, dl), F32)

        @pl.when(ci > 0)
        def _():
            xbuf[pl.ds(0, HIST), :] = zp_ref[:, pl.ds(0, dl)]
            pbuf[pl.ds(0, HIST), :] = zp_ref[:, pl.ds(2 * dl, dl)]
            hbuf[pl.ds(0, HIST), :] = hsp_ref[...]

        xbuf[pl.ds(HIST, tc), :] = z_ref[:, pl.ds(0, dl)]
        pbuf[pl.ds(HIST, tc), :] = z_ref[:, pl.ds(2 * dl, dl)]
        hbuf[pl.ds(HIST, tc), :] = hs_ref[...]
        sp = _softplus_neg(ap_ref[...])

        for h in range(n_heads):
            sl = pl.ds(h * hd, hd)
            xc = _conv_fwd(xbuf, cw_ref, cb_ref, sl, tc)
            _, ra, ii, a, _ = _lru_gates(xc, wa_ref[h], ba_ref[:, sl], wx_ref[h], bx_ref[:, sl], sp[:, h * hd:(h + 1) * hd])
            a_s[:, sl] = a
            ra_s[:, sl] = ra
            ii_s[:, sl] = ii
            xc_s[:, sl] = xc
            lam_s[:, sl] = dm_ref[:, sl] * _gelu(z_ref[:, pl.ds(dl + h * hd, hd)])

        def step(r, carry):
            t = tc - 1 - r
            lam = lam_s[pl.ds(t, 1), :] + carry
            lam_s[pl.ds(t, 1), :] = lam
            return a_s[pl.ds(t, 1), :] * lam

        ccar[...] = lax.fori_loop(0, tc, step, ccar[...])

        for h in range(n_heads):
            sl = pl.ds(h * hd, hd)
            sp_h = sp[:, h * hd:(h + 1) * hd]
            lam = lam_s[:, sl]
            a = a_s[:, sl]
            ra = ra_s[:, sl]
            ii = ii_s[:, sl]
            xc = xc_s[:, sl]
            la = -LRU_C * ra * sp_h
            mult = jnp.sqrt(_neg_expm1(2.0 * la))
            hprev = hbuf[pl.ds(HIST - 1, tc), sl]
            dmult = lam * ii * xc
            dla = lam * hprev * a - dmult * (a * a) / mult
            dap_ref[:, sl] += jnp.sum(dla * (-LRU_C) * ra, axis=0, keepdims=True)
            dpa = dla * (-LRU_C) * sp_h * ra * (1.0 - ra)
            dpx = lam * mult * xc * ii * (1.0 - ii)
            dba_ref[:, sl] += jnp.sum(dpa, axis=0, keepdims=True)
            dbx_ref[:, sl] += jnp.sum(dpx, axis=0, keepdims=True)
            xb = xc.astype(BF16)
            dpa_b = dpa.astype(BF16)
            dpx_b = dpx.astype(BF16)
            dwa_ref[h] += _dot(xb, dpa_b, "tn")
            dwx_ref[h] += _dot(xb, dpx_b, "tn")
            dxc = lam * mult * ii + _dot(dpa_b, wa_ref[h], "nt") + _dot(dpx_b, wx_ref[h], "nt")
            dxbuf[pl.ds(0, tc), sl] = dxc
            dcb_ref[:, sl] += jnp.sum(dxc, axis=0, keepdims=True)
            dzx = jnp.zeros((tc, hd), F32)
            for k in range(CONV_WIDTH):
                dcw_ref[k:k + 1, sl] += jnp.sum(dxc * xbuf[pl.ds(HIST - (CONV_WIDTH - 1) + k, tc), sl], axis=0, keepdims=True)
                dzx = dzx + cw_ref[k:k + 1, sl] * dxbuf[pl.ds(CONV_WIDTH - 1 - k, tc), sl]
            dz_ref[:, sl] = dzx.astype(BF16)
            dxbuf[pl.ds(tc, HIST), sl] = dxbuf[pl.ds(0, HIST), sl]
            zg = z_ref[:, pl.ds(dl + h * hd, hd)]
            dz_ref[:, pl.ds(dl + h * hd, hd)] = (dm_ref[:, sl] * hs_ref[:, sl] * _gelu_grad(zg)).astype(BF16)

        for g in range(n_groups):
            cols = pl.ds(g * gd, gd)
            win = POOL_WINDOWS[g]
            dlt, cnt = _pool_delta(pbuf, cols, win, ci * tc, tc)
            db = dlt.astype(BF16)
            q = _dot(db, pw_ref[g], "nn") + pb_ref[:, cols]
            dyb = dm_ref[:, pl.ds(dl + g * gd, gd)]
            dps_ref[:, cols] += jnp.sum(dyb * q, axis=0, keepdims=True)
            dq = dyb * ps_ref[:, cols]
            dpb_ref[:, cols] += jnp.sum(dq, axis=0, keepdims=True)
            dqb = dq.astype(BF16)
            dpw_ref[g] += _dot(db, dqb, "tn")
            dd = _dot(dqb, pw_ref[g], "nt")
            ddbuf[pl.ds(0, tc), cols] = dd / cnt
            dzp = -dd
            for k in range(win):
                dzp = dzp + ddbuf[pl.ds(k, tc), cols]
            dz_ref[:, pl.ds(2 * dl + g * gd, gd)] = dzp.astype(BF16)
            ddbuf[pl.ds(tc, HIST), cols] = ddbuf[pl.ds(0, HIST), cols]

        @pl.when(i == nc - 1)
        def _():
            dap_ref[...] = dap_ref[...] * (-_sigmoid(-ap_ref[...]))

    rev = lambda i: (nc - 1 - i, 0)
    prev = lambda i: (jnp.maximum((nc - 1 - i) * per - 1, 0), 0)
    const = lambda a: pl.BlockSpec(a.shape[1:], lambda i: (0,) * (a.ndim - 1))
    plist = _mix_param_list(mp)
    grad_shapes = [jax.ShapeDtypeStruct(a.shape[1:], F32) for a in plist]
    buf = lambda rows: pltpu.VMEM((rows, dl), F32)
    outs = pl.pallas_call(
        body, name=name,
        out_shape=[jax.ShapeDtypeStruct((tp, 3 * dl), BF16)] + grad_shapes,
        grid=(nc,),
        in_specs=[pl.BlockSpec((tc, 3 * dl), rev), pl.BlockSpec((HIST, 3 * dl), prev),
                  pl.BlockSpec((tc, dl), rev), pl.BlockSpec((HIST, dl), prev),
                  pl.BlockSpec((tc, 2 * dl), rev)] + _mix_param_specs(mp, layer, None),
        out_specs=[pl.BlockSpec((tc, 3 * dl), rev)] + [const(a) for a in plist],
        scratch_shapes=[buf(HIST + tc), buf(HIST + tc), buf(HIST + tc), buf(tc + HIST), buf(tc + HIST),
                        buf(tc), buf(tc), buf(tc), buf(tc), buf(tc), buf(1)],
        compiler_params=_params(("arbitrary",)),
    )(z, z, hs, hs, dm, *plist)
    return outs[0], dict(zip(_MIX_GRADS, outs[1:]))


def _ffn_fwd_in(h, norm, layer, w_in, tag, deps):
    n_act = _rms_fwd(h, norm, layer, name=f"{tag}_rms", deps=deps)
    gu, act = _ffn_in_fwd(n_act, w_in, 0, name=f"{tag}_in")
    return n_act, gu, act


def _ffn_fwd_out(h, act, w_out, tag):
    tp, d = h.shape
    f = w_out.shape[1]
    return _matmul(act, w_out, mode="nn", m=tp, n=d, kdim=f, tm=_tile(tp, 1056), tn=_tile(d, 1024, LANES),
                   tk=_tile(f, 512, LANES), out_dtype=F32, scale=0.5, residual=h, b_layer=0, name=f"{tag}_out")


def _ffn_bwd(dh, dhb, saved, norm, layer, w_in, w_out, tag, emit, deps):
    h, n_act, gu = saved
    tp, d = h.shape
    f = w_out.shape[1]
    dz, act = _ffn_bwd_act(dhb, w_out, 0, gu, name=f"{tag}_bact", deps=deps)
    tok = emit("w_out", _matmul(act, dhb, mode="tn", m=f, n=d, kdim=tp, tm=_tile(f, 512, LANES), tn=_tile(d, 1024, LANES), tk=tp,
                                out_dtype=BF16, scale=0.5, name=f"{tag}_dwout"))
    tok = emit("w_in", _matmul(n_act, dz, mode="tn", m=d, n=2 * f, kdim=tp, tm=_tile(d, 512, LANES), tn=_tile(f, 512, LANES),
                               tk=tp, out_dtype=BF16, b_split=True, name=f"{tag}_dwin", deps=[tok]))
    dn = _matmul(dz, w_in, mode="nt", m=tp, n=d, kdim=2 * f, tm=_tile(tp, 1056), tn=d, tk=_tile(f, 512, LANES),
                 out_dtype=F32, a_split=True, b_layer=0, name=f"{tag}_dn", deps=[tok])
    return _rms_bwd(h, norm, layer, dn, dh, name=f"{tag}_brms")


def _mix_block_fwd(h, norm, layer, w_in, w_out, mp, tag, deps):
    tp, d = h.shape
    d_in = w_in.shape[2]
    n_act = _rms_fwd(h, norm, layer, name=f"{tag}_rms", deps=deps)
    z = _matmul(n_act, w_in, mode="nn", m=tp, n=d_in, kdim=d, tm=_tile(tp, 2112), tn=_tile(d_in, 512, LANES), tk=d,
                out_dtype=F32, b_layer=0, name=f"{tag}_in")
    m_act, hs = _mix_fwd(z, mp, layer, name=f"{tag}_mix")
    h_out = _matmul(m_act, w_out, mode="nn", m=tp, n=d, kdim=d, tm=_tile(tp, 2112), tn=_tile(d, 512, LANES), tk=d,
                    out_dtype=F32, residual=h, b_layer=0, name=f"{tag}_out")
    return h_out, (h, n_act, z, hs, m_act)


def _mix_block_bwd(dh, dhb, saved, norm, layer, w_in, w_out, mp, tag, emit, deps):
    h, n_act, z, hs, m_act = saved
    tp, d = h.shape
    d_in = w_in.shape[2]
    dm = _matmul(dhb, w_out, mode="nt", m=tp, n=d, kdim=d, tm=_tile(tp, 2112), tn=_tile(d, 512, LANES), tk=d,
                 out_dtype=F32, b_layer=0, name=f"{tag}_dm", deps=deps)
    tok = emit("w_out", _matmul(m_act, dhb, mode="tn", m=d, n=d, kdim=tp, tm=_tile(d, 512, LANES), tn=_tile(d, 1024, LANES), tk=tp,
                                out_dtype=BF16, name=f"{tag}_dwout"))
    dz, g_mix = _mix_bwd(z, hs, dm, mp, layer, name=f"{tag}_bmix")
    tok = emit("w_in", _matmul(n_act, dz, mode="tn", m=d, n=d_in, kdim=tp, tm=_tile(d, 512, LANES), tn=_tile(d_in, 512, LANES),
                               tk=tp, out_dtype=BF16, name=f"{tag}_dwin", deps=[tok]))
    dn = _matmul(dz, w_in, mode="nt", m=tp, n=d, kdim=d_in, tm=_tile(tp, 1056), tn=d, tk=_tile(d_in, 512, LANES),
                 out_dtype=F32, b_layer=0, name=f"{tag}_dn", deps=[tok])
    dh_in, dhb_in, g_norm = _rms_bwd(h, norm, layer, dn, dh, name=f"{tag}_brms")
    return dh_in, dhb_in, g_norm, g_mix


def _mesh_pos():
    x, y, c = lax.axis_index("x"), lax.axis_index("y"), lax.axis_index("c")
    chips = [(1 - x, y), (x, 1 - y), (1 - x, 1 - y)]
    return x, y, c, chips


def _block(ref, axis, j, size):
    idx = [slice(None)] * len(ref.shape)
    idx[axis] = pl.ds(j * size, size)
    return ref.at[tuple(idx)]


def _all_gather(shards, axes, *, name):
    n = len(shards)
    sizes = [s.shape[ax] for s, ax in zip(shards, axes)]
    full_shapes = [s.shape[:ax] + (N_DEV * s.shape[ax],) + s.shape[ax + 1:] for s, ax in zip(shards, axes)]

    def body(*refs):
        srcs, fulls = refs[:n], refs[n:2 * n]
        send_sems, recv_sems, local_sems = refs[2 * n:]
        x, y, c, chips = _mesh_pos()
        me, sib = (x, y, c), (x, y, 1 - c)

        def blk(a, dev):
            return _block(fulls[a], axes[a], 4 * dev[0] + 2 * dev[1] + dev[2], sizes[a])

        def copy(a, k, block_dev, to, src=None):
            return pltpu.make_async_remote_copy(
                src_ref=blk(a, block_dev) if src is None else src, dst_ref=blk(a, block_dev),
                send_sem=send_sems.at[a, k], recv_sem=recv_sems.at[a, k], device_id=to, device_id_type=MESH)

        mine = [pltpu.make_async_copy(srcs[a], blk(a, me), local_sems.at[a]) for a in range(n)]
        for cp in mine:
            cp.start()
        first = []
        for a in range(n):
            first.append(copy(a, 0, me, sib, src=srcs[a]))
            first += [copy(a, 1 + j, me, (*chip, c), src=srcs[a]) for j, chip in enumerate(chips)]
        for cp in first:
            cp.start()
        passed = []
        for j, chip in enumerate(chips):
            for a in range(n):
                copy(a, 1 + j, (*chip, c), me).wait_recv()
                fwd = copy(a, 4 + j, (*chip, c), sib)
                fwd.start()
                passed.append(fwd)
        for a in range(n):
            copy(a, 0, sib, me).wait_recv()
            for j, chip in enumerate(chips):
                copy(a, 4 + j, (*chip, 1 - c), me).wait_recv()
        for cp in first + passed:
            cp.wait_send()
        for cp in mine:
            cp.wait()

    any_spec = pl.BlockSpec(memory_space=pl.ANY)
    return pl.pallas_call(
        body, name=name,
        out_shape=[jax.ShapeDtypeStruct(fs, s.dtype) for fs, s in zip(full_shapes, shards)],
        in_specs=[any_spec] * n, out_specs=[any_spec] * n,
        scratch_shapes=[pltpu.SemaphoreType.DMA((n, 7)), pltpu.SemaphoreType.DMA((n, 7)), pltpu.SemaphoreType.DMA((n,))],
    )(*shards)


_HBM = pl.BlockSpec(memory_space=pltpu.HBM)
_SEM = pl.BlockSpec(memory_space=pltpu.SEMAPHORE)
_ANY = pl.BlockSpec(memory_space=pl.ANY)
_EFFECT = pltpu.SideEffectType.DATAFLOW_SIDE_EFFECTING
_N_OUT = 4


def _in_hbm(a):
    return pltpu.with_memory_space_constraint(a, pltpu.HBM)


def _gather_start(shards, axes, deps, *, name):
    n = len(shards)
    sizes = [s.shape[ax] for s, ax in zip(shards, axes)]
    full_shapes = [s.shape[:ax] + (N_DEV * s.shape[ax],) + s.shape[ax + 1:] for s, ax in zip(shards, axes)]

    def body(*refs):
        srcs, lands = refs[:n], refs[n:2 * n]
        send_sems, recv_sems = refs[2 * n + len(deps)], refs[2 * n + len(deps) + 1]
        token = refs[-1]
        x, y, c, chips = _mesh_pos()
        targets = [(x, y, 1 - c)] + [(*chip, c) for chip in chips]
        for a in range(n):
            dst = _block(lands[a], axes[a], 4 * x + 2 * y + c, sizes[a])
            for k, to in enumerate(targets):
                pltpu.make_async_remote_copy(src_ref=srcs[a], dst_ref=dst, send_sem=send_sems.at[_N_OUT * a + k],
                                             recv_sem=recv_sems.at[_N_OUT * a + k],
                                             device_id=to, device_id_type=MESH).start()
        token[...] = jnp.zeros(token.shape, F32)

    lands0 = [lax.empty(fs, s.dtype) for fs, s in zip(full_shapes, shards)]
    outs = pl.pallas_call(
        body, name=name,
        out_shape=(pltpu.SemaphoreType.DMA((n * _N_OUT,)), pltpu.SemaphoreType.DMA((n * _N_OUT,)),
                   *[pltpu.HBM(s.shape, s.dtype) for s in shards], *[pltpu.HBM(fs, s.dtype) for fs, s in zip(full_shapes, shards)],
                   jax.ShapeDtypeStruct((8, LANES), F32)),
        in_specs=[_HBM] * (2 * n) + [_ANY] * len(deps),
        out_specs=(_SEM, _SEM, *[_HBM] * (2 * n), pl.BlockSpec(memory_space=pltpu.VMEM)),
        input_output_aliases={a: 2 + a for a in range(2 * n)},
        compiler_params=pltpu.CompilerParams(has_side_effects=_EFFECT),
    )(*[_in_hbm(s) for s in shards], *[_in_hbm(z) for z in lands0], *deps)
    handle = dict(send_sems=outs[0], recv_sems=outs[1], srcs=list(outs[2:2 + n]), lands=list(outs[2 + n:2 + 2 * n]), axes=list(axes))
    return handle, outs[-1]


def _gather_wait(handle, after, *, name):
    srcs, lands, axes = handle["srcs"], handle["lands"], handle["axes"]
    n = len(srcs)
    sizes = [s.shape[ax] for s, ax in zip(srcs, axes)]

    def body(*refs):
        src_refs, land_refs = refs[:n], refs[n:2 * n]
        send_sems, recv_sems = refs[2 * n], refs[2 * n + 1]
        x, y, c, chips = _mesh_pos()
        peers = [(x, y, 1 - c)] + [(*chip, c) for chip in chips]
        for a in range(n):
            for k, dev in enumerate(peers):
                cp = pltpu.make_async_remote_copy(
                    src_ref=src_refs[a], dst_ref=_block(land_refs[a], axes[a], 4 * dev[0] + 2 * dev[1] + dev[2], sizes[a]),
                    send_sem=send_sems.at[_N_OUT * a + k], recv_sem=recv_sems.at[_N_OUT * a + k], device_id=dev,
                    device_id_type=MESH)
                cp.wait_send()
                cp.wait_recv()

    outs = pl.pallas_call(
        body, name=name,
        out_shape=(*[pltpu.HBM(s.shape, s.dtype) for s in srcs], *[pltpu.HBM(z.shape, z.dtype) for z in lands]),
        in_specs=[_HBM] * (2 * n) + [_SEM, _SEM] + [_ANY] * len(after),
        out_specs=tuple([_HBM] * (2 * n)),
        input_output_aliases={a: a for a in range(2 * n)},
        compiler_params=pltpu.CompilerParams(has_side_effects=_EFFECT),
    )(*srcs, *lands, handle["send_sems"], handle["recv_sems"], *after)
    return list(outs[:n]), list(outs[n:])


def _place_own(shard, full, axis, dev, *, name):
    nd = shard.ndim
    rows, cols = shard.shape[-2:]
    tr = _tile(rows, 512, _row_mult(shard.dtype)) if rows % _row_mult(shard.dtype) == 0 else rows
    nrb = rows // tr
    block = shard.shape[:-2] + (tr, cols)

    def in_map(r, dev_ref):
        return (0,) * (nd - 2) + (r, 0)

    def out_map(r, dev_ref):
        idx = [0] * nd
        idx[nd - 2] = r
        idx[axis] = dev_ref[0] * (nrb if axis == nd - 2 else 1) + idx[axis]
        return tuple(idx)

    def body(dev_ref, src_ref, full_ref, out_ref):
        out_ref[...] = src_ref[...]

    grid_spec = pltpu.PrefetchScalarGridSpec(
        num_scalar_prefetch=1, grid=(nrb,),
        in_specs=[pl.BlockSpec(block, in_map), _ANY], out_specs=pl.BlockSpec(block, out_map))
    return pl.pallas_call(body, name=name, grid_spec=grid_spec, out_shape=jax.ShapeDtypeStruct(full.shape, full.dtype),
                          input_output_aliases={2: 0}, compiler_params=_params(("arbitrary",)))(dev, shard, full)


def _gather_forward(fulls, axes, *, name):
    n = len(fulls)
    sizes = [z.shape[ax] // N_DEV for z, ax in zip(fulls, axes)]

    def body(*refs):
        outs = refs[n:2 * n]
        send_sems, recv_sems = refs[2 * n:]
        x, y, c, chips = _mesh_pos()
        sib = (x, y, 1 - c)

        def blk(a, dev):
            return _block(outs[a], axes[a], 4 * dev[0] + 2 * dev[1] + dev[2], sizes[a])

        passed = [pltpu.make_async_remote_copy(src_ref=blk(a, (*chip, c)), dst_ref=blk(a, (*chip, c)), send_sem=send_sems.at[a, j],
                                               recv_sem=recv_sems.at[a, j], device_id=sib, device_id_type=MESH)
                  for a in range(n) for j, chip in enumerate(chips)]
        for cp in passed:
            cp.start()
        for a in range(n):
            for j, chip in enumerate(chips):
                pltpu.make_async_remote_copy(src_ref=blk(a, (*chip, c)), dst_ref=blk(a, (*chip, 1 - c)), send_sem=send_sems.at[a, j],
                                             recv_sem=recv_sems.at[a, j], device_id=sib, device_id_type=MESH).wait_recv()
        for cp in passed:
            cp.wait_send()

    return pl.pallas_call(
        body, name=name,
        out_shape=[jax.ShapeDtypeStruct(z.shape, z.dtype) for z in fulls],
        in_specs=[_ANY] * n, out_specs=[_ANY] * n,
        input_output_aliases={a: a for a in range(n)},
        scratch_shapes=[pltpu.SemaphoreType.DMA((n, 3)), pltpu.SemaphoreType.DMA((n, 3))],
    )(*fulls)


def _rs_exchange_sibling(tensors, axes, *, name):
    n = len(tensors)
    n_layers = [len(t) for t in tensors]
    sizes = [t[0].shape[ax] // N_DEV for t, ax in zip(tensors, axes)]
    blk_shapes = [t[0].shape[:ax] + (sz,) + t[0].shape[ax + 1:] for t, ax, sz in zip(tensors, axes, sizes)]
    flat = [g for t in tensors for g in t]
    offs = [sum(n_layers[:a]) for a in range(n)]

    def body(*refs):
        srcs = refs[:len(flat)]
        lands = refs[len(flat):len(flat) + n]
        send_sems, recv_sems = refs[len(flat) + n:]
        x, y, c, _ = _mesh_pos()
        sib = (x, y, 1 - c)
        for a in range(n):
            for l in range(n_layers[a]):
                for i in range(N_CHIP):
                    pltpu.make_async_remote_copy(
                        src_ref=_block(srcs[offs[a] + l], axes[a], 2 * i + (1 - c), sizes[a]), dst_ref=lands[a].at[i, l],
                        send_sem=send_sems.at[a], recv_sem=recv_sems.at[a], device_id=sib, device_id_type=MESH).start()
        for a in range(n):
            pltpu.make_async_remote_copy(src_ref=lands[a], dst_ref=lands[a], send_sem=send_sems.at[a], recv_sem=recv_sems.at[a],
                                         device_id=sib, device_id_type=MESH).wait()

    any_spec = pl.BlockSpec(memory_space=pl.ANY)
    return pl.pallas_call(
        body, name=name,
        out_shape=[jax.ShapeDtypeStruct((N_CHIP, nl) + bs, t[0].dtype) for nl, bs, t in zip(n_layers, blk_shapes, tensors)],
        in_specs=[any_spec] * len(flat), out_specs=[any_spec] * n,
        scratch_shapes=[pltpu.SemaphoreType.DMA((n,)), pltpu.SemaphoreType.DMA((n,))],
    )(*flat)


def _rs_add_sibling(layers, axis, land, my_c, *, name):
    n_layers = len(layers)
    shape = layers[0].shape
    size = shape[axis] // N_DEV
    blk_shape = shape[:axis] + (size,) + shape[axis + 1:]
    nd = len(shape)
    rows = blk_shape[-2]
    tr = _tile(rows, 512, _row_mult(layers[0].dtype))
    inner = (tr, blk_shape[-1])
    lead = blk_shape[:-2]
    if lead:
        raise ValueError("blocked gradients are 2-D")
    nrb = rows // tr

    def src_map(l):
        def imap(i, r, c_ref):
            j = 2 * i + c_ref[0]
            return (j * nrb + r, 0) if axis == 0 else (r, j)
        return imap

    def body(c_ref, *refs):
        srcs = refs[:n_layers]
        land_ref = refs[n_layers]
        out_ref = refs[n_layers + 1]
        for l in range(n_layers):
            out_ref[l] = (srcs[l][...].astype(F32) + land_ref[l].astype(F32)).astype(out_ref.dtype)

    grid_spec = pltpu.PrefetchScalarGridSpec(
        num_scalar_prefetch=1, grid=(N_CHIP, nrb),
        in_specs=[pl.BlockSpec(inner, src_map(l)) for l in range(n_layers)]
        + [pl.BlockSpec((None, n_layers) + inner, lambda i, r, c_ref: (i, 0, r, 0))],
        out_specs=pl.BlockSpec((None, n_layers) + inner, lambda i, r, c_ref: (i, 0, r, 0)),
    )
    return pl.pallas_call(
        body, name=name, grid_spec=grid_spec,
        out_shape=jax.ShapeDtypeStruct((N_CHIP, n_layers) + blk_shape, layers[0].dtype),
        compiler_params=_params(("arbitrary", "arbitrary")),
    )(my_c, *layers, land)


def _rs_exchange_chips(parts, *, name):
    n = len(parts)

    def body(*refs):
        srcs, lands = refs[:n], refs[n:2 * n]
        send_sems, recv_sems, local_sems = refs[2 * n:]
        x, y, c, chips = _mesh_pos()
        mine = 2 * x + y
        local = [pltpu.make_async_copy(srcs[a].at[mine], lands[a].at[mine], local_sems.at[a]) for a in range(n)]
        for cp in local:
            cp.start()

        def copy(a, j, chip):
            return pltpu.make_async_remote_copy(
                src_ref=srcs[a].at[2 * chip[0] + chip[1]], dst_ref=lands[a].at[mine],
                send_sem=send_sems.at[a, j], recv_sem=recv_sems.at[a, j], device_id=(*chip, c), device_id_type=MESH)

        sends = [copy(a, j, chip) for a in range(n) for j, chip in enumerate(chips)]
        for cp in sends:
            cp.start()
        for a in range(n):
            for j, chip in enumerate(chips):
                pltpu.make_async_remote_copy(
                    src_ref=srcs[a].at[mine], dst_ref=lands[a].at[2 * chip[0] + chip[1]],
                    send_sem=send_sems.at[a, j], recv_sem=recv_sems.at[a, j], device_id=(*chip, c), device_id_type=MESH).wait_recv()
        for cp in sends:
            cp.wait_send()
        for cp in local:
            cp.wait()

    any_spec = pl.BlockSpec(memory_space=pl.ANY)
    return pl.pallas_call(
        body, name=name,
        out_shape=[jax.ShapeDtypeStruct(p.shape, p.dtype) for p in parts],
        in_specs=[any_spec] * n, out_specs=[any_spec] * n,
        scratch_shapes=[pltpu.SemaphoreType.DMA((n, 3)), pltpu.SemaphoreType.DMA((n, 3)), pltpu.SemaphoreType.DMA((n,))],
    )(*parts)


def _rs_chips_start(part, *, name):
    land = lax.empty(part.shape, part.dtype)

    def body(src, dst, send_sems, recv_sems, src_thru, dst_thru, token):
        x, y, c, chips = _mesh_pos()
        for j, chip in enumerate(chips):
            pltpu.make_async_remote_copy(src_ref=src.at[2 * chip[0] + chip[1]], dst_ref=dst.at[2 * x + y], send_sem=send_sems.at[j],
                                         recv_sem=recv_sems.at[j], device_id=(*chip, c), device_id_type=MESH).start()
        token[...] = jnp.zeros(token.shape, F32)

    outs = pl.pallas_call(
        body, name=name,
        out_shape=(pltpu.SemaphoreType.DMA((3,)), pltpu.SemaphoreType.DMA((3,)), pltpu.HBM(part.shape, part.dtype),
                   pltpu.HBM(land.shape, land.dtype), jax.ShapeDtypeStruct((8, LANES), F32)),
        in_specs=[_HBM, _HBM], out_specs=(_SEM, _SEM, _HBM, _HBM, pl.BlockSpec(memory_space=pltpu.VMEM)),
        input_output_aliases={0: 2, 1: 3},
        compiler_params=pltpu.CompilerParams(has_side_effects=_EFFECT),
    )(_in_hbm(part), _in_hbm(land))
    return dict(send_sems=outs[0], recv_sems=outs[1], src=outs[2], land=outs[3]), outs[4]


def _rs_chips_wait(handle, after, *, name):
    def body(src, dst, send_sems, recv_sems, *rest):
        x, y, c, chips = _mesh_pos()
        for j, chip in enumerate(chips):
            cp = pltpu.make_async_remote_copy(src_ref=src.at[2 * chip[0] + chip[1]], dst_ref=dst.at[2 * chip[0] + chip[1]],
                                              send_sem=send_sems.at[j], recv_sem=recv_sems.at[j], device_id=(*chip, c),
                                              device_id_type=MESH)
            cp.wait_send()
            cp.wait_recv()

    src, land = handle["src"], handle["land"]
    outs = pl.pallas_call(
        body, name=name,
        out_shape=(pltpu.HBM(src.shape, src.dtype), pltpu.HBM(land.shape, land.dtype)),
        in_specs=[_HBM, _HBM, _SEM, _SEM] + [_ANY] * len(after), out_specs=(_HBM, _HBM),
        input_output_aliases={0: 0, 1: 1},
        compiler_params=pltpu.CompilerParams(has_side_effects=_EFFECT),
    )(src, land, handle["send_sems"], handle["recv_sems"], *after)
    return outs[0], outs[1]


def _adam_math(w, g, m, v):
    m_new = ADAM_B1 * m + (1.0 - ADAM_B1) * g
    v_new = ADAM_B2 * v + (1.0 - ADAM_B2) * (g * g)
    m_hat = m_new / (1.0 - ADAM_B1 ** ADAM_STEP)
    v_hat = v_new / (1.0 - ADAM_B2 ** ADAM_STEP)
    delta = -ADAM_LR * (m_hat / (jnp.sqrt(v_hat) + ADAM_EPS) + ADAM_WD * w)
    return delta, m_new, v_new


def _sum_adam(part, land, slots, w, m, v, layer, prev, *, name):
    n_layers, rows, cols = w.shape
    tr = _tile(rows, 256, _row_mult(land.dtype))
    n_prev = 0 if prev is None else 4

    def body(slots_ref, p0_ref, p1_ref, p2_ref, p3_ref, w_ref, m_ref, v_ref, *rest):
        g_ref, d_ref, mo_ref, vo_ref = rest[n_prev:]
        g = p0_ref[...].astype(F32)
        for ref in (p1_ref, p2_ref, p3_ref):
            g = g + ref[...].astype(F32)
        delta, m_new, v_new = _adam_math(w_ref[...], g, m_ref[...], v_ref[...])
        g_ref[...] = g
        d_ref[...] = delta
        mo_ref[...] = m_new
        vo_ref[...] = v_new

    blk = pl.BlockSpec((None, tr, cols), lambda r, s: (layer, r, 0))
    slot = lambda j: pl.BlockSpec((None, None, tr, cols), lambda r, s: (s[j], 0, r, 0))
    shp = jax.ShapeDtypeStruct((n_layers, rows, cols), F32)
    grid_spec = pltpu.PrefetchScalarGridSpec(
        num_scalar_prefetch=1, grid=(rows // tr,),
        in_specs=[slot(0), slot(1), slot(2), slot(3), blk, blk, blk] + [_ANY] * n_prev,
        out_specs=(blk, blk, blk, blk))
    return pl.pallas_call(
        body, name=name, out_shape=(shp, shp, shp, shp), grid_spec=grid_spec,
        input_output_aliases={8 + i: i for i in range(n_prev)},
        compiler_params=_params(("parallel",)),
    )(slots, part, land, land, land, w, m, v, *(prev or ()))


def _sum_slots(land, *, name):
    _, _, rows, cols = land.shape
    tr = _tile(rows, 256, 8)

    def body(land_ref, g_ref):
        g = land_ref[0]
        for i in range(1, N_CHIP):
            g = g + land_ref[i]
        g_ref[...] = g

    return pl.pallas_call(
        body, name=name, out_shape=jax.ShapeDtypeStruct((rows, cols), F32), grid=(rows // tr,),
        in_specs=[pl.BlockSpec((N_CHIP, None, tr, cols), lambda r: (0, 0, r, 0))],
        out_specs=pl.BlockSpec((tr, cols), lambda r: (r, 0)),
        compiler_params=_params(("parallel",)),
    )(land)


def _adam_flat(w, g, m, v, *, name):
    rows, cols = w.shape
    tr = _tile(rows, 256, 8)

    def body(w_ref, g_ref, m_ref, v_ref, d_ref, mo_ref, vo_ref):
        delta, m_new, v_new = _adam_math(w_ref[...], g_ref[...], m_ref[...], v_ref[...])
        d_ref[...] = delta
        mo_ref[...] = m_new
        vo_ref[...] = v_new

    blk = pl.BlockSpec((tr, cols), lambda r: (r, 0))
    shp = jax.ShapeDtypeStruct((rows, cols), F32)
    return pl.pallas_call(
        body, name=name, out_shape=(shp, shp, shp), grid=(rows // tr,), in_specs=[blk] * 4, out_specs=(blk, blk, blk),
        compiler_params=_params(("parallel",)),
    )(w, g, m, v)


_BIG = (("ffn1_w_in", 1), ("ffn1_w_out", 0), ("w_in", 1), ("w_out", 0), ("ffn2_w_in", 1), ("ffn2_w_out", 0))
_SMALL = ("meta_tokens", "ffn1_norm", "mix_norm", "conv_w", "conv_b", "lru_wa", "lru_ba", "lru_wx", "lru_bx", "lru_a_param",
          "pool_w", "pool_b", "pool_scale", "ffn2_norm", "final_norm")
_SMALL_SHARD_AXIS = {"meta_tokens": 1, "conv_w": 2, "pool_w": 2}
_PACK_COLS = 1024


def _pack(arrs):
    flat = jnp.concatenate([a.reshape(-1) for a in arrs])
    unit = N_DEV * 8 * _PACK_COLS
    total = -(-flat.shape[0] // unit) * unit
    flat = jnp.pad(flat, (0, total - flat.shape[0]))
    return flat.reshape(total // _PACK_COLS, _PACK_COLS)


def _unpack(packed, shapes):
    flat = packed.reshape(-1)
    out, off = [], 0
    for s in shapes:
        size = 1
        for v in s:
            size *= v
        out.append(flat[off:off + size].reshape(s))
        off += size
    return out


def _my_shard(full, axis, dev):
    size = full.shape[axis] // N_DEV
    return lax.dynamic_slice_in_dim(full, dev * size, size, axis)


def kernel(x, meta_tokens, ffn1_norm, ffn1_w_in, ffn1_w_out, mix_norm, w_in, conv_w, conv_b, lru_wa, lru_ba, lru_wx, lru_bx, lru_a_param, pool_w, pool_b, pool_scale, w_out, ffn2_norm, ffn2_w_in, ffn2_w_out, final_norm, loss_target, m_meta_tokens, m_ffn1_norm, m_ffn1_w_in, m_ffn1_w_out, m_mix_norm, m_w_in, m_conv_w, m_conv_b, m_lru_wa, m_lru_ba, m_lru_wx, m_lru_bx, m_lru_a_param, m_pool_w, m_pool_b, m_pool_scale, m_w_out, m_ffn2_norm, m_ffn2_w_in, m_ffn2_w_out, m_final_norm, v_meta_tokens, v_ffn1_norm, v_ffn1_w_in, v_ffn1_w_out, v_mix_norm, v_w_in, v_conv_w, v_conv_b, v_lru_wa, v_lru_ba, v_lru_wx, v_lru_bx, v_lru_a_param, v_pool_w, v_pool_b, v_pool_scale, v_w_out, v_ffn2_norm, v_ffn2_w_in, v_ffn2_w_out, v_final_norm):
    names = ("meta_tokens", "ffn1_norm", "ffn1_w_in", "ffn1_w_out", "mix_norm", "w_in", "conv_w", "conv_b", "lru_wa", "lru_ba",
             "lru_wx", "lru_bx", "lru_a_param", "pool_w", "pool_b", "pool_scale", "w_out", "ffn2_norm", "ffn2_w_in", "ffn2_w_out",
             "final_norm")
    w = dict(zip(names, (meta_tokens, ffn1_norm, ffn1_w_in, ffn1_w_out, mix_norm, w_in, conv_w, conv_b, lru_wa, lru_ba, lru_wx,
                         lru_bx, lru_a_param, pool_w, pool_b, pool_scale, w_out, ffn2_norm, ffn2_w_in, ffn2_w_out, final_norm)))
    mom = dict(zip(names, (m_meta_tokens, m_ffn1_norm, m_ffn1_w_in, m_ffn1_w_out, m_mix_norm, m_w_in, m_conv_w, m_conv_b, m_lru_wa,
                           m_lru_ba, m_lru_wx, m_lru_bx, m_lru_a_param, m_pool_w, m_pool_b, m_pool_scale, m_w_out, m_ffn2_norm,
                           m_ffn2_w_in, m_ffn2_w_out, m_final_norm)))
    vel = dict(zip(names, (v_meta_tokens, v_ffn1_norm, v_ffn1_w_in, v_ffn1_w_out, v_mix_norm, v_w_in, v_conv_w, v_conv_b, v_lru_wa,
                           v_lru_ba, v_lru_wx, v_lru_bx, v_lru_a_param, v_pool_w, v_pool_b, v_pool_scale, v_w_out, v_ffn2_norm,
                           v_ffn2_w_in, v_ffn2_w_out, v_final_norm)))
    n_layers, d = ffn1_norm.shape
    n_meta = meta_tokens.shape[0]
    seq = x.shape[1]
    t_valid = n_meta + seq
    tp = -(-t_valid // ROW_ALIGN) * ROW_ALIGN
    dev = 4 * lax.axis_index("x") + 2 * lax.axis_index("y") + lax.axis_index("c")
    my_c = lax.axis_index("c").astype(jnp.int32).reshape(1)
    dev1 = dev.astype(jnp.int32).reshape(1)
    mx, my = lax.axis_index("x"), lax.axis_index("y")
    slots = jnp.stack([2 * mx + my, 2 * (1 - mx) + my, 2 * mx + (1 - my), 2 * (1 - mx) + (1 - my)]).astype(jnp.int32)

    big_axis = dict(_BIG)
    vec = lambda a: a.reshape(a.shape[0], 1, a.shape[1])
    norms = dict(ffn1=vec(ffn1_norm), mix=vec(mix_norm), ffn2=vec(ffn2_norm))
    units = [(kind, l) for l in range(n_layers) for kind in ("ffn1", "mix", "ffn2")]

    def shard(k, l):
        return w[k][l:l + 1].astype(BF16), big_axis[k] + 1

    groups = [[shard("ffn1_w_in", 0), (meta_tokens, 1)], [shard("ffn1_w_out", 0)]]
    for kind, l in units[1:]:
        if kind == "mix":
            groups.append([shard("w_in", l), shard("w_out", l)] + ([(conv_w, 2), (pool_w, 2)] if l == 0 else []))
        else:
            groups.append([shard(f"{kind}_w_in", l), shard(f"{kind}_w_out", l)])
    handles = {}
    state = dict(token=None, started=0)

    def order():
        return [] if state["token"] is None else [state["token"]]

    def start_next():
        i = state["started"]
        if i < len(groups):
            handles[i], state["token"] = _gather_start([s for s, _ in groups[i]], [ax for _, ax in groups[i]], order(),
                                                       name=f"gather{i}_start")
            state["started"] = i + 1

    def finish(i, after):
        handle = handles.pop(i)
        srcs, fulls = _gather_wait(handle, after, name=f"gather{i}_wait")
        fulls = [_place_own(s, z, ax, dev1, name=f"gather{i}_own{a}") for a, (s, z, ax) in enumerate(zip(srcs, fulls, handle["axes"]))]
        fulls = _gather_forward(fulls, handle["axes"], name=f"gather{i}_pass")
        start_next()
        return fulls

    start_next()
    start_next()
    w_in0, meta_full = finish(0, [])
    pad = jnp.zeros((tp - t_valid, d), F32)
    h0 = jnp.concatenate([meta_full, x[0], pad], axis=0)
    tgt = jnp.concatenate([jnp.zeros((n_meta, d), F32), loss_target[0], pad], axis=0)
    n_act, gu, act = _ffn_fwd_in(h0, norms["ffn1"], 0, w_in0, "l0ffn1", order())
    (w_out0,) = finish(1, [act])
    h = _ffn_fwd_out(h0, act, w_out0, "l0ffn1")
    gathered = {units[0]: (w_in0, w_out0)}
    saved = {units[0]: (h0, n_act, gu)}
    mp = None
    for ui, (kind, l) in enumerate(units[1:], start=1):
        fulls = finish(ui + 1, [h])
        tag = f"l{l}{kind}"
        if kind == "mix":
            if l == 0:
                mp = dict(conv_w=fulls[2], conv_b=vec(conv_b), wa=lru_wa.astype(BF16), ba=vec(lru_ba), wx=lru_wx.astype(BF16),
                          bx=vec(lru_bx), a_param=vec(lru_a_param), pool_w=fulls[3].astype(BF16), pool_b=vec(pool_b),
                          pool_scale=vec(pool_scale))
            h, saved[(kind, l)] = _mix_block_fwd(h, norms["mix"], l, fulls[0], fulls[1], mp, tag, order())
        else:
            n_act, gu, act = _ffn_fwd_in(h, norms[kind], l, fulls[0], tag, order())
            saved[(kind, l)] = (h, n_act, gu)
            h = _ffn_fwd_out(h, act, fulls[1], tag)
        gathered[(kind, l)] = (fulls[0], fulls[1])
    dh, dhb, g_final, loss_local = _final_loss(h, final_norm.reshape(1, d), tgt, n_meta, t_valid, name="final_loss")
    loss = lax.psum(loss_local[0, 0], ("x", "y", "c"))

    pending = []
    big_out = {}

    def drain(after):
        while pending:
            k, l, handle = pending.pop(0)
            part, land = _rs_chips_wait(handle, after, name=f"rs_{k}{l}_wait")
            big_out[k] = _sum_adam(part, land, slots, w[k], mom[k], vel[k], l, big_out.get(k), name=f"adam_{k}{l}")

    def emitter(kind, l):
        def emit(which, g):
            k = which if kind == "mix" else f"{kind}_{which}"
            if which == "w_out":
                drain([g])
            (land,) = _rs_exchange_sibling([[g]], [big_axis[k]], name=f"rs_{k}{l}_sib")
            part = _rs_add_sibling([g], big_axis[k], land, my_c, name=f"rs_{k}{l}_add")
            handle, token = _rs_chips_start(part, name=f"rs_{k}{l}_start")
            pending.append((k, l, handle))
            return token
        return emit

    g_norm = {k: [None] * n_layers for k in ("ffn1", "mix", "ffn2")}
    g_mix = [None] * n_layers
    for kind, l in reversed(units):
        w_i, w_o = gathered[(kind, l)]
        tag = f"l{l}{kind}"
        if kind == "mix":
            dh, dhb, g_norm[kind][l], g_mix[l] = _mix_block_bwd(dh, dhb, saved[(kind, l)], norms[kind], l, w_i, w_o, mp, tag,
                                                                 emitter(kind, l), [])
        else:
            dh, dhb, g_norm[kind][l] = _ffn_bwd(dh, dhb, saved[(kind, l)], norms[kind], l, w_i, w_o, tag, emitter(kind, l), [])
    drain([dh])
    dh0 = dh
    grad_x = dh0[n_meta:t_valid][None]

    mix_key = dict(conv_w="conv_w", conv_b="conv_b", lru_wa="wa", lru_ba="ba", lru_wx="wx", lru_bx="bx", lru_a_param="a_param",
                   pool_w="pool_w", pool_b="pool_b", pool_scale="pool_scale")
    small_local = {"meta_tokens": dh0[:n_meta], "final_norm": g_final.reshape(d)}
    for k in ("ffn1", "mix", "ffn2"):
        small_local[f"{k}_norm"] = jnp.stack([g.reshape(d) for g in g_norm[k]])
    for k, mk in mix_key.items():
        small_local[k] = jnp.stack([g_mix[l][mk] for l in range(n_layers)]).reshape(
            w[k].shape if k not in _SMALL_SHARD_AXIS else small_shape_full(w[k], _SMALL_SHARD_AXIS[k]))
    small_shapes = [small_local[k].shape for k in _SMALL]
    packed = _pack([small_local[k] for k in _SMALL])
    (land,) = _rs_exchange_sibling([[packed]], [0], name="rs_small_sib")
    part = _rs_add_sibling([packed], 0, land, my_c, name="rs_small_add")
    (land2,) = _rs_exchange_chips([part], name="rs_small_chips")

    out_g, out_d, out_m, out_v = {}, {}, {}, {}
    for k, _ in _BIG:
        out_g[k], out_d[k], out_m[k], out_v[k] = big_out[k]
    small_block = _sum_slots(land2, name="rs_sum_small")
    small_full = _all_gather([small_block], [0], name="gather_small")[0]
    small_g = dict(zip(_SMALL, _unpack(small_full, small_shapes)))
    for k, ax in _SMALL_SHARD_AXIS.items():
        small_g[k] = _my_shard(small_g[k], ax, dev)
    shapes_local = [w[k].shape for k in _SMALL]
    d_p, m_p, v_p = _adam_flat(_pack([w[k] for k in _SMALL]), _pack([small_g[k] for k in _SMALL]),
                               _pack([mom[k] for k in _SMALL]), _pack([vel[k] for k in _SMALL]), name="adam_small")
    for k, dd, mm, vv in zip(_SMALL, _unpack(d_p, shapes_local), _unpack(m_p, shapes_local), _unpack(v_p, shapes_local)):
        out_g[k], out_d[k], out_m[k], out_v[k] = small_g[k], dd, mm, vv

    return (loss, grad_x, *[out_g[k] for k in names], *[out_d[k] for k in names], *[out_m[k] for k in names],
            *[out_v[k] for k in names])


def small_shape_full(w_shard, axis):
    return w_shard.shape[:axis] + (N_DEV * w_shard.shape[axis],) + w_shard.shape[axis + 1:]
```

```python
import functools

import jax
import jax.numpy as jnp
from jax import lax
from jax.experimental import pallas as pl
from jax.experimental.pallas import tpu as pltpu

F32 = jnp.float32
BF16 = jnp.bfloat16
MESH = pl.DeviceIdType.MESH

N_DEV = 8
N_CHIP = 4
RMS_EPS = 1e-6
LRU_C = 8.0
CONV_WIDTH = 4
POOL_WINDOWS = (2, 4, 8, 16)
HIST = 16
ADAM_LR = 0.001
ADAM_B1 = 0.9
ADAM_B2 = 0.999
ADAM_EPS = 1e-08
ADAM_WD = 0.01
ADAM_STEP = 10
ROW_ALIGN = 128
V7X_VMEM_LIMIT = 56 * 1024 * 1024
BF16_ROWS = 16
K_TILE = 2816
EPILOGUE_ROWS = 528
LANES = 128


def _tile(n, cap, mult=BF16_ROWS):
    best = None
    d = mult
    while d <= min(n, cap):
        if n % d == 0:
            best = d
        d += mult
    if best is None:
        raise ValueError(f"no tile for {n} (cap {cap}, multiple of {mult})")
    return best


def _row_mult(dtype):
    return 8 * 4 // jnp.dtype(dtype).itemsize


def _params(sem=None):
    return pltpu.CompilerParams(dimension_semantics=sem, vmem_limit_bytes=V7X_VMEM_LIMIT)


def _dot(a, b, mode):
    dims = {"nn": ((1,), (0,)), "nt": ((1,), (1,)), "tn": ((0,), (0,))}[mode]
    return lax.dot_general(a, b, (dims, ((), ())), preferred_element_type=F32)


def _sigmoid(x):
    return 1.0 / (1.0 + jnp.exp(-x))


_GELU_C = 0.7978845608028654
_GELU_K = 0.044715


def _gelu(x):
    return 0.5 * x * (1.0 + jnp.tanh(_GELU_C * (x + _GELU_K * x * x * x)))


def _gelu_grad(x):
    th = jnp.tanh(_GELU_C * (x + _GELU_K * x * x * x))
    return 0.5 * (1.0 + th) + 0.5 * x * (1.0 - th * th) * _GELU_C * (1.0 + 3.0 * _GELU_K * x * x)


def _neg_expm1(x):
    p = 1.0 / 3628800.0
    for c in (1.0 / 362880.0, 1.0 / 40320.0, 1.0 / 5040.0, 1.0 / 720.0, 1.0 / 120.0, 1.0 / 24.0, 1.0 / 6.0, 0.5, 1.0):
        p = p * x + c
    return jnp.where(x > -0.5, -(x * p), 1.0 - jnp.exp(x))


def _softplus_neg(p):
    e = jnp.exp(-jnp.abs(p))
    u = 1.0 + e
    l1p = jnp.where(u == 1.0, e, jnp.log(u) * e / (u - 1.0 + (u == 1.0).astype(F32)))
    return jnp.maximum(-p, 0.0) + l1p


def _operand_spec(arr, br, bc, ridx, cidx, layer, split):
    if split:
        ncb = arr.shape[2] // bc
        return pl.BlockSpec((None, br, bc), lambda i, j, k: (cidx(i, j, k) // ncb, ridx(i, j, k), cidx(i, j, k) % ncb))
    if layer is not None:
        return pl.BlockSpec((None, br, bc), lambda i, j, k: (layer, ridx(i, j, k), cidx(i, j, k)))
    return pl.BlockSpec((br, bc), lambda i, j, k: (ridx(i, j, k), cidx(i, j, k)))


def _matmul(a, b, *, mode, m, n, kdim, tm, tn, tk, out_dtype, name, scale=None, residual=None,
            a_layer=None, b_layer=None, a_split=False, b_split=False, deps=()):
    nk = kdim // tk
    I = lambda i, j, k: i
    J = lambda i, j, k: j
    K = lambda i, j, k: k
    if mode == "nn":
        a_spec = _operand_spec(a, tm, tk, I, K, a_layer, a_split)
        b_spec = _operand_spec(b, tk, tn, K, J, b_layer, b_split)
    elif mode == "nt":
        a_spec = _operand_spec(a, tm, tk, I, K, a_layer, a_split)
        b_spec = _operand_spec(b, tn, tk, J, K, b_layer, b_split)
    else:
        a_spec = _operand_spec(a, tk, tm, K, I, a_layer, a_split)
        b_spec = _operand_spec(b, tk, tn, K, J, b_layer, b_split)
    in_specs = [a_spec, b_spec]
    operands = [a, b]
    if residual is not None:
        in_specs.append(pl.BlockSpec((tm, tn), lambda i, j, k: (i, j)))
        operands.append(residual)
    in_specs += [pl.BlockSpec(memory_space=pl.ANY)] * len(deps)
    operands += list(deps)
    n_in = len(operands)

    def finish(acc, res_ref, o_ref):
        if scale is not None:
            acc = acc * scale
        if res_ref is not None:
            acc = acc + res_ref[...]
        o_ref[...] = acc.astype(o_ref.dtype)

    def body(*refs):
        a_ref, b_ref = refs[0], refs[1]
        res_ref = refs[2] if residual is not None else None
        o_ref = refs[n_in]
        part = _dot(a_ref[...], b_ref[...], mode)
        if nk == 1:
            finish(part, res_ref, o_ref)
        else:
            acc_ref = refs[-1]
            k = pl.program_id(2)

            @pl.when(k == 0)
            def _():
                acc_ref[...] = part

            @pl.when(k > 0)
            def _():
                acc_ref[...] += part

            @pl.when(k == nk - 1)
            def _():
                finish(acc_ref[...], res_ref, o_ref)

    return pl.pallas_call(
        body, name=name,
        out_shape=jax.ShapeDtypeStruct((m, n), out_dtype),
        grid=(m // tm, n // tn, nk),
        in_specs=in_specs,
        out_specs=pl.BlockSpec((tm, tn), lambda i, j, k: (i, j)),
        scratch_shapes=[pltpu.VMEM((tm, tn), F32)] if nk > 1 else [],
        compiler_params=_params(("parallel", "parallel", "arbitrary")),
    )(*operands)


def _ffn_in_fwd(n_act, w_in, layer, *, name):
    tp, d = n_act.shape
    f = w_in.shape[2] // 2
    tm = _tile(tp, 2112)
    tn = _tile(f, 256, LANES)
    nj = f // tn

    ch = _tile(tm, EPILOGUE_ROWS)

    def body(n_ref, wg_ref, wu_ref, gu_ref, a_ref):
        for r in range(tm // ch):
            rows = pl.ds(r * ch, ch)
            x = n_ref[rows, :]
            g = _dot(x, wg_ref[...], "nn")
            u = _dot(x, wu_ref[...], "nn")
            gu_ref[0, rows, :] = g.astype(BF16)
            gu_ref[1, rows, :] = u.astype(BF16)
            a_ref[rows, :] = (g * _sigmoid(g) * u).astype(BF16)

    return pl.pallas_call(
        body, name=name,
        out_shape=(jax.ShapeDtypeStruct((2, tp, f), BF16), jax.ShapeDtypeStruct((tp, f), BF16)),
        grid=(tp // tm, nj),
        in_specs=[pl.BlockSpec((tm, d), lambda i, j: (i, 0)),
                  pl.BlockSpec((None, d, tn), lambda i, j: (layer, 0, j)),
                  pl.BlockSpec((None, d, tn), lambda i, j: (layer, 0, j + nj))],
        out_specs=(pl.BlockSpec((2, tm, tn), lambda i, j: (0, i, j)), pl.BlockSpec((tm, tn), lambda i, j: (i, j))),
        compiler_params=_params(("parallel", "arbitrary")),
    )(n_act, w_in, w_in)


def _ffn_bwd_act(dhb, w_out, layer, gu, *, name, deps=()):
    tp, d = dhb.shape
    f = w_out.shape[1]
    tm = _tile(tp, 2112)
    tn = _tile(f, 256, LANES)

    ch = _tile(tm, EPILOGUE_ROWS)

    def body(dh_ref, w_ref, gu_ref, *rest):
        dz_ref, a_ref = rest[len(deps):]
        for r in range(tm // ch):
            rows = pl.ds(r * ch, ch)
            da = 0.5 * _dot(dh_ref[rows, :], w_ref[...], "nt")
            g = gu_ref[0, rows, :].astype(F32)
            u = gu_ref[1, rows, :].astype(F32)
            s = _sigmoid(g)
            sg = g * s
            dz_ref[0, rows, :] = (da * u * (s * (1.0 + g * (1.0 - s)))).astype(BF16)
            dz_ref[1, rows, :] = (da * sg).astype(BF16)
            a_ref[rows, :] = (sg * u).astype(BF16)

    return pl.pallas_call(
        body, name=name,
        out_shape=(jax.ShapeDtypeStruct((2, tp, f), BF16), jax.ShapeDtypeStruct((tp, f), BF16)),
        grid=(tp // tm, f // tn),
        in_specs=[pl.BlockSpec((tm, d), lambda i, j: (i, 0)),
                  pl.BlockSpec((None, tn, d), lambda i, j: (layer, j, 0)),
                  pl.BlockSpec((2, tm, tn), lambda i, j: (0, i, j))] + [pl.BlockSpec(memory_space=pl.ANY)] * len(deps),
        out_specs=(pl.BlockSpec((2, tm, tn), lambda i, j: (0, i, j)), pl.BlockSpec((tm, tn), lambda i, j: (i, j))),
        compiler_params=_params(("parallel", "arbitrary")),
    )(dhb, w_out, gu, *deps)


def _rms_fwd(h, g, layer, *, name, deps=()):
    tp, d = h.shape
    tr = _tile(tp, 528)

    def body(h_ref, g_ref, *rest):
        n_ref = rest[len(deps)]
        x = h_ref[...]
        r = lax.rsqrt(jnp.mean(x * x, axis=-1, keepdims=True) + RMS_EPS)
        n_ref[...] = (x * r * g_ref[...]).astype(BF16)

    return pl.pallas_call(
        body, name=name, out_shape=jax.ShapeDtypeStruct((tp, d), BF16), grid=(tp // tr,),
        in_specs=[pl.BlockSpec((tr, d), lambda i: (i, 0)), pl.BlockSpec((None, 1, d), lambda i: (layer, 0, 0))]
        + [pl.BlockSpec(memory_space=pl.ANY)] * len(deps),
        out_specs=pl.BlockSpec((tr, d), lambda i: (i, 0)),
        compiler_params=_params(("parallel",)),
    )(h, g, *deps)


def _rms_bwd(h, g, layer, dn, dres, *, name):
    tp, d = h.shape
    tr = _tile(tp, 528)

    def body(h_ref, g_ref, dn_ref, dres_ref, dh_ref, dhb_ref, dg_ref):
        x = h_ref[...]
        r = lax.rsqrt(jnp.mean(x * x, axis=-1, keepdims=True) + RMS_EPS)
        xhat = x * r
        dn_v = dn_ref[...]
        dxhat = dn_v * g_ref[...]
        dh = dres_ref[...] + r * (dxhat - xhat * jnp.mean(dxhat * xhat, axis=-1, keepdims=True))
        dh_ref[...] = dh
        dhb_ref[...] = dh.astype(BF16)
        part = jnp.sum(dn_v * xhat, axis=0, keepdims=True)

        @pl.when(pl.program_id(0) == 0)
        def _():
            dg_ref[...] = part

        @pl.when(pl.program_id(0) > 0)
        def _():
            dg_ref[...] += part

    return pl.pallas_call(
        body, name=name,
        out_shape=(jax.ShapeDtypeStruct((tp, d), F32), jax.ShapeDtypeStruct((tp, d), BF16), jax.ShapeDtypeStruct((1, d), F32)),
        grid=(tp // tr,),
        in_specs=[pl.BlockSpec((tr, d), lambda i: (i, 0)), pl.BlockSpec((None, 1, d), lambda i: (layer, 0, 0)),
                  pl.BlockSpec((tr, d), lambda i: (i, 0)), pl.BlockSpec((tr, d), lambda i: (i, 0))],
        out_specs=(pl.BlockSpec((tr, d), lambda i: (i, 0)), pl.BlockSpec((tr, d), lambda i: (i, 0)),
                   pl.BlockSpec((1, d), lambda i: (0, 0))),
        compiler_params=_params(("arbitrary",)),
    )(h, g, dn, dres)


def _final_loss(h, g, tgt, n_meta, t_valid, *, name):
    tp, d = h.shape
    tr = _tile(tp, 528)

    def body(h_ref, g_ref, t_ref, dh_ref, dhb_ref, dg_ref, loss_ref):
        i = pl.program_id(0)
        x = h_ref[...]
        r = lax.rsqrt(jnp.mean(x * x, axis=-1, keepdims=True) + RMS_EPS)
        xhat = x * r
        gv = g_ref[...]
        row = i * tr + lax.broadcasted_iota(jnp.int32, (tr, 1), 0)
        valid = jnp.logical_and(row >= n_meta, row < t_valid)
        err = jnp.where(valid, xhat * gv - t_ref[...], 0.0)
        dy = err * (1.0 / d)
        dxhat = dy * gv
        dh = r * (dxhat - xhat * jnp.mean(dxhat * xhat, axis=-1, keepdims=True))
        dh_ref[...] = dh
        dhb_ref[...] = dh.astype(BF16)
        dg_part = jnp.sum(dy * xhat, axis=0, keepdims=True)
        loss_part = jnp.sum(jnp.sum(err * err, axis=1, keepdims=True), axis=0, keepdims=True) * (0.5 / d)

        @pl.when(i == 0)
        def _():
            dg_ref[...] = dg_part
            loss_ref[...] = loss_part

        @pl.when(i > 0)
        def _():
            dg_ref[...] += dg_part
            loss_ref[...] += loss_part

    return pl.pallas_call(
        body, name=name,
        out_shape=(jax.ShapeDtypeStruct((tp, d), F32), jax.ShapeDtypeStruct((tp, d), BF16),
                   jax.ShapeDtypeStruct((1, d), F32), jax.ShapeDtypeStruct((1, 1), F32)),
        grid=(tp // tr,),
        in_specs=[pl.BlockSpec((tr, d), lambda i: (i, 0)), pl.BlockSpec((1, d), lambda i: (0, 0)),
                  pl.BlockSpec((tr, d), lambda i: (i, 0))],
        out_specs=(pl.BlockSpec((tr, d), lambda i: (i, 0)), pl.BlockSpec((tr, d), lambda i: (i, 0)),
                   pl.BlockSpec((1, d), lambda i: (0, 0)), pl.BlockSpec((1, 1), lambda i: (0, 0))),
        compiler_params=_params(("arbitrary",)),
    )(h, g, tgt)


def _mix_param_specs(mp, layer, imap):
    def vec(a):
        return pl.BlockSpec((None,) + a.shape[1:], lambda *g: (layer,) + (0,) * (a.ndim - 1))
    return [vec(mp[k]) for k in ("conv_w", "conv_b", "wa", "ba", "wx", "bx", "a_param", "pool_w", "pool_b", "pool_scale")]


def _mix_param_list(mp):
    return [mp[k] for k in ("conv_w", "conv_b", "wa", "ba", "wx", "bx", "a_param", "pool_w", "pool_b", "pool_scale")]


def _lru_gates(xc, wa_h, ba_h, wx_h, bx_h, sp_h):
    xb = xc.astype(BF16)
    ra = _sigmoid(_dot(xb, wa_h, "nn") + ba_h)
    ii = _sigmoid(_dot(xb, wx_h, "nn") + bx_h)
    la = -LRU_C * ra * sp_h
    a = jnp.exp(la)
    mult = jnp.sqrt(_neg_expm1(2.0 * la))
    return xb, ra, ii, a, mult


def _conv_fwd(xbuf, cw_ref, cb_ref, sl, tc):
    xc = cb_ref[:, sl]
    for k in range(CONV_WIDTH):
        xc = xc + cw_ref[k:k + 1, sl] * xbuf[pl.ds(HIST - (CONV_WIDTH - 1) + k, tc), sl]
    return xc


def _pool_delta(pbuf, cols, win, t0, tc):
    u = pbuf[pl.ds(HIST, tc), cols]
    s = u
    for k in range(1, win):
        s = s + pbuf[pl.ds(HIST - k, tc), cols]
    t = t0 + lax.broadcasted_iota(jnp.int32, (tc, 1), 0)
    cnt = jnp.minimum(t + 1, win).astype(F32)
    return s / cnt - u, cnt


def _mix_fwd(z, mp, layer, *, name):
    tp = z.shape[0]
    dl = z.shape[1] // 3
    n_heads, hd = mp["wa"].shape[1], mp["wa"].shape[2]
    n_groups, gd = mp["pool_w"].shape[1], mp["pool_w"].shape[2]
    tc = _tile(tp, 384)

    def body(z_ref, cw_ref, cb_ref, wa_ref, ba_ref, wx_ref, bx_ref, ap_ref, pw_ref, pb_ref, ps_ref,
             m_ref, hs_ref, xbuf, pbuf, a_s, b_s, hcar):
        i = pl.program_id(0)

        @pl.when(i == 0)
        def _():
            xbuf[pl.ds(0, HIST), :] = jnp.zeros((HIST, dl), F32)
            pbuf[pl.ds(0, HIST), :] = jnp.zeros((HIST, dl), F32)
            hcar[...] = jnp.zeros((1, dl), F32)

        @pl.when(i > 0)
        def _():
            xbuf[pl.ds(0, HIST), :] = xbuf[pl.ds(tc, HIST), :]
            pbuf[pl.ds(0, HIST), :] = pbuf[pl.ds(tc, HIST), :]

        xbuf[pl.ds(HIST, tc), :] = z_ref[:, pl.ds(0, dl)]
        pbuf[pl.ds(HIST, tc), :] = z_ref[:, pl.ds(2 * dl, dl)]
        sp = _softplus_neg(ap_ref[...])
        for h in range(n_heads):
            sl = pl.ds(h * hd, hd)
            xc = _conv_fwd(xbuf, cw_ref, cb_ref, sl, tc)
            _, _, ii, a, mult = _lru_gates(xc, wa_ref[h], ba_ref[:, sl], wx_ref[h], bx_ref[:, sl], sp[:, h * hd:(h + 1) * hd])
            a_s[:, sl] = a
            b_s[:, sl] = mult * ii * xc

        def step(t, hprev):
            hnew = a_s[pl.ds(t, 1), :] * hprev + b_s[pl.ds(t, 1), :]
            hs_ref[pl.ds(t, 1), :] = hnew
            return hnew

        hcar[...] = lax.fori_loop(0, tc, step, hcar[...])
        for h in range(n_heads):
            sl = pl.ds(h * hd, hd)
            m_ref[:, sl] = (hs_ref[:, sl] * _gelu(z_ref[:, pl.ds(dl + h * hd, hd)])).astype(BF16)
        for g in range(n_groups):
            cols = pl.ds(g * gd, gd)
            dlt, _ = _pool_delta(pbuf, cols, POOL_WINDOWS[g], i * tc, tc)
            q = _dot(dlt.astype(BF16), pw_ref[g], "nn") + pb_ref[:, cols]
            m_ref[:, pl.ds(dl + g * gd, gd)] = (q * ps_ref[:, cols]).astype(BF16)

    return pl.pallas_call(
        body, name=name,
        out_shape=(jax.ShapeDtypeStruct((tp, 2 * dl), BF16), jax.ShapeDtypeStruct((tp, dl), F32)),
        grid=(tp // tc,),
        in_specs=[pl.BlockSpec((tc, 3 * dl), lambda i: (i, 0))] + _mix_param_specs(mp, layer, None),
        out_specs=(pl.BlockSpec((tc, 2 * dl), lambda i: (i, 0)), pl.BlockSpec((tc, dl), lambda i: (i, 0))),
        scratch_shapes=[pltpu.VMEM((HIST + tc, dl), F32), pltpu.VMEM((HIST + tc, dl), F32),
                        pltpu.VMEM((tc, dl), F32), pltpu.VMEM((tc, dl), F32), pltpu.VMEM((1, dl), F32)],
        compiler_params=_params(("arbitrary",)),
    )(z, *_mix_param_list(mp))


_MIX_GRADS = ("conv_w", "conv_b", "wa", "ba", "wx", "bx", "a_param", "pool_w", "pool_b", "pool_scale")


def _mix_bwd(z, hs, dm, mp, layer, *, name):
    tp = z.shape[0]
    dl = z.shape[1] // 3
    n_heads, hd = mp["wa"].shape[1], mp["wa"].shape[2]
    n_groups, gd = mp["pool_w"].shape[1], mp["pool_w"].shape[2]
    tc = _tile(tp, 384)
    nc = tp // tc
    per = tc // HIST

    def body(z_ref, zp_ref, hs_ref, hsp_ref, dm_ref, cw_ref, cb_ref, wa_ref, ba_ref, wx_ref, bx_ref, ap_ref, pw_ref, pb_ref,
             ps_ref, dz_ref, dcw_ref, dcb_ref, dwa_ref, dba_ref, dwx_ref, dbx_ref, dap_ref, dpw_ref, dpb_ref, dps_ref,
             xbuf, pbuf, hbuf, dxbuf, ddbuf, a_s, lam_s, ra_s, ii_s, xc_s, ccar):
        i = pl.program_id(0)
        ci = nc - 1 - i

        @pl.when(i == 0)
        def _():
            dxbuf[pl.ds(tc, HIST), :] = jnp.zeros((HIST, dl), F32)
            ddbuf[pl.ds(tc, HIST), :] = jnp.zeros((HIST, dl), F32)
            ccar[...] = jnp.zeros((1, dl), F32)
            for ref in (dcw_ref, dcb_ref, dwa_ref, dba_ref, dwx_ref, dbx_ref, dap_ref, dpw_ref, dpb_ref, dps_ref):
                ref[...] = jnp.zeros(ref.shape, F32)

        @pl.when(ci == 0)
        def _():
            xbuf[pl.ds(0, HIST), :] = jnp.zeros((HIST, dl), F32)
            pbuf[pl.ds(0, HIST), :] = jnp.zeros((HIST, dl), F32)
            hbuf[pl.ds(0, HIST), :] = jnp.zeros((HIST, dl), F32)

        @pl.when(ci > 0)
        def _():
            xbuf[pl.ds(0, HIST), :] = zp_ref[:, pl.ds(0, dl)]
            pbuf[pl.ds(0, HIST), :] = zp_ref[:, pl.ds(2 * dl, dl)]
            hbuf[pl.ds(0, HIST), :] = hsp_ref[...]

        xbuf[pl.ds(HIST, tc), :] = z_ref[:, pl.ds(0, dl)]
        pbuf[pl.ds(HIST, tc), :] = z_ref[:, pl.ds(2 * dl, dl)]
        hbuf[pl.ds(HIST, tc), :] = hs_ref[...]
        sp = _softplus_neg(ap_ref[...])

        for h in range(n_heads):
            sl = pl.ds(h * hd, hd)
            xc = _conv_fwd(xbuf, cw_ref, cb_ref, sl, tc)
            _, ra, ii, a, _ = _lru_gates(xc, wa_ref[h], ba_ref[:, sl], wx_ref[h], bx_ref[:, sl], sp[:, h * hd:(h + 1) * hd])
            a_s[:, sl] = a
            ra_s[:, sl] = ra
            ii_s[:, sl] = ii
            xc_s[:, sl] = xc
            lam_s[:, sl] = dm_ref[:, sl] * _gelu(z_ref[:, pl.ds(dl + h * hd, hd)])

        def step(r, carry):
            t = tc - 1 - r
            lam = lam_s[pl.ds(t, 1), :] + carry
            lam_s[pl.ds(t, 1), :] = lam
            return a_s[pl.ds(t, 1), :] * lam

        ccar[...] = lax.fori_loop(0, tc, step, ccar[...])

        for h in range(n_heads):
            sl = pl.ds(h * hd, hd)
            sp_h = sp[:, h * hd:(h + 1) * hd]
            lam = lam_s[:, sl]
            a = a_s[:, sl]
            ra = ra_s[:, sl]
            ii = ii_s[:, sl]
            xc = xc_s[:, sl]
            la = -LRU_C * ra * sp_h
            mult = jnp.sqrt(_neg_expm1(2.0 * la))
            hprev = hbuf[pl.ds(HIST - 1, tc), sl]
            dmult = lam * ii * xc
            dla = lam * hprev * a - dmult * (a * a) / mult
            dap_ref[:, sl] += jnp.sum(dla * (-LRU_C) * ra, axis=0, keepdims=True)
            dpa = dla * (-LRU_C) * sp_h * ra * (1.0 - ra)
            dpx = lam * mult * xc * ii * (1.0 - ii)
            dba_ref[:, sl] += jnp.sum(dpa, axis=0, keepdims=True)
            dbx_ref[:, sl] += jnp.sum(dpx, axis=0, keepdims=True)
            xb = xc.astype(BF16)
            dpa_b = dpa.astype(BF16)
            dpx_b = dpx.astype(BF16)
            dwa_ref[h] += _dot(xb, dpa_b, "tn")
            dwx_ref[h] += _dot(xb, dpx_b, "tn")
            dxc = lam * mult * ii + _dot(dpa_b, wa_ref[h], "nt") + _dot(dpx_b, wx_ref[h], "nt")
            dxbuf[pl.ds(0, tc), sl] = dxc
            dcb_ref[:, sl] += jnp.sum(dxc, axis=0, keepdims=True)
            dzx = jnp.zeros((tc, hd), F32)
            for k in range(CONV_WIDTH):
                dcw_ref[k:k + 1, sl] += jnp.sum(dxc * xbuf[pl.ds(HIST - (CONV_WIDTH - 1) + k, tc), sl], axis=0, keepdims=True)
                dzx = dzx + cw_ref[k:k + 1, sl] * dxbuf[pl.ds(CONV_WIDTH - 1 - k, tc), sl]
            dz_ref[:, sl] = dzx.astype(BF16)
            dxbuf[pl.ds(tc, HIST), sl] = dxbuf[pl.ds(0, HIST), sl]
            zg = z_ref[:, pl.ds(dl + h * hd, hd)]
            dz_ref[:, pl.ds(dl + h * hd, hd)] = (dm_ref[:, sl] * hs_ref[:, sl] * _gelu_grad(zg)).astype(BF16)

        for g in range(n_groups):
            cols = pl.ds(g * gd, gd)
            win = POOL_WINDOWS[g]
            dlt, cnt = _pool_delta(pbuf, cols, win, ci * tc, tc)
            db = dlt.astype(BF16)
            q = _dot(db, pw_ref[g], "nn") + pb_ref[:, cols]
            dyb = dm_ref[:, pl.ds(dl + g * gd, gd)]
            dps_ref[:, cols] += jnp.sum(dyb * q, axis=0, keepdims=True)
            dq = dyb * ps_ref[:, cols]
            dpb_ref[:, cols] += jnp.sum(dq, axis=0, keepdims=True)
            dqb = dq.astype(BF16)
            dpw_ref[g] += _dot(db, dqb, "tn")
            dd = _dot(dqb, pw_ref[g], "nt")
            ddbuf[pl.ds(0, tc), cols] = dd / cnt
            dzp = -dd
            for k in range(win):
                dzp = dzp + ddbuf[pl.ds(k, tc), cols]
            dz_ref[:, pl.ds(2 * dl + g * gd, gd)] = dzp.astype(BF16)
            ddbuf[pl.ds(tc, HIST), cols] = ddbuf[pl.ds(0, HIST), cols]

        @pl.when(i == nc - 1)
        def _():
            dap_ref[...] = dap_ref[...] * (-_sigmoid(-ap_ref[...]))

    rev = lambda i: (nc - 1 - i, 0)
    prev = lambda i: (jnp.maximum((nc - 1 - i) * per - 1, 0), 0)
    const = lambda a: pl.BlockSpec(a.shape[1:], lambda i: (0,) * (a.ndim - 1))
    plist = _mix_param_list(mp)
    grad_shapes = [jax.ShapeDtypeStruct(a.shape[1:], F32) for a in plist]
    buf = lambda rows: pltpu.VMEM((rows, dl), F32)
    outs = pl.pallas_call(
        body, name=name,
        out_shape=[jax.ShapeDtypeStruct((tp, 3 * dl), BF16)] + grad_shapes,
        grid=(nc,),
        in_specs=[pl.BlockSpec((tc, 3 * dl), rev), pl.BlockSpec((HIST, 3 * dl), prev),
                  pl.BlockSpec((tc, dl), rev), pl.BlockSpec((HIST, dl), prev),
                  pl.BlockSpec((tc, 2 * dl), rev)] + _mix_param_specs(mp, layer, None),
        out_specs=[pl.BlockSpec((tc, 3 * dl), rev)] + [const(a) for a in plist],
        scratch_shapes=[buf(HIST + tc), buf(HIST + tc), buf(HIST + tc), buf(tc + HIST), buf(tc + HIST),
                        buf(tc), buf(tc), buf(tc), buf(tc), buf(tc), buf(1)],
        compiler_params=_params(("arbitrary",)),
    )(z, z, hs, hs, dm, *plist)
    return outs[0], dict(zip(_MIX_GRADS, outs[1:]))


def _ffn_fwd_in(h, norm, layer, w_in, tag, deps):
    n_act = _rms_fwd(h, norm, layer, name=f"{tag}_rms", deps=deps)
    gu, act = _ffn_in_fwd(n_act, w_in, 0, name=f"{tag}_in")
    return n_act, gu, act


def _ffn_fwd_out(h, act, w_out, tag):
    tp, d = h.shape
    f = w_out.shape[1]
    return _matmul(act, w_out, mode="nn", m=tp, n=d, kdim=f, tm=_tile(tp, 1056), tn=_tile(d, 1024, LANES),
                   tk=_tile(f, K_TILE, LANES), out_dtype=F32, scale=0.5, residual=h, b_layer=0, name=f"{tag}_out")


def _ffn_bwd(dh, dhb, saved, norm, layer, w_in, w_out, tag, emit, deps):
    h, n_act, gu = saved
    tp, d = h.shape
    f = w_out.shape[1]
    dz, act = _ffn_bwd_act(dhb, w_out, 0, gu, name=f"{tag}_bact", deps=deps)
    tok = emit("w_out", _matmul(act, dhb, mode="tn", m=f, n=d, kdim=tp, tm=_tile(f, 512, LANES), tn=_tile(d, 1024, LANES), tk=tp,
                                out_dtype=BF16, scale=0.5, name=f"{tag}_dwout"))
    tok = emit("w_in", _matmul(n_act, dz, mode="tn", m=d, n=2 * f, kdim=tp, tm=_tile(d, 512, LANES), tn=_tile(f, 512, LANES),
                               tk=tp, out_dtype=BF16, b_split=True, name=f"{tag}_dwin", deps=[tok]))
    dn = _matmul(dz, w_in, mode="nt", m=tp, n=d, kdim=2 * f, tm=_tile(tp, 1056), tn=_tile(d, 1024, LANES), tk=_tile(f, K_TILE, LANES),
                 out_dtype=F32, a_split=True, b_layer=0, name=f"{tag}_dn", deps=[tok])
    return _rms_bwd(h, norm, layer, dn, dh, name=f"{tag}_brms")


def _mix_block_fwd(h, norm, layer, w_in, w_out, mp, tag, deps):
    tp, d = h.shape
    d_in = w_in.shape[2]
    n_act = _rms_fwd(h, norm, layer, name=f"{tag}_rms", deps=deps)
    z = _matmul(n_act, w_in, mode="nn", m=tp, n=d_in, kdim=d, tm=_tile(tp, 2112), tn=_tile(d_in, 512, LANES), tk=d,
                out_dtype=F32, b_layer=0, name=f"{tag}_in")
    m_act, hs = _mix_fwd(z, mp, layer, name=f"{tag}_mix")
    h_out = _matmul(m_act, w_out, mode="nn", m=tp, n=d, kdim=d, tm=_tile(tp, 2112), tn=_tile(d, 512, LANES), tk=d,
                    out_dtype=F32, residual=h, b_layer=0, name=f"{tag}_out")
    return h_out, (h, n_act, z, hs, m_act)


def _mix_block_bwd(dh, dhb, saved, norm, layer, w_in, w_out, mp, tag, emit, deps):
    h, n_act, z, hs, m_act = saved
    tp, d = h.shape
    d_in = w_in.shape[2]
    dm = _matmul(dhb, w_out, mode="nt", m=tp, n=d, kdim=d, tm=_tile(tp, 2112), tn=_tile(d, 512, LANES), tk=d,
                 out_dtype=F32, b_layer=0, name=f"{tag}_dm", deps=deps)
    tok = emit("w_out", _matmul(m_act, dhb, mode="tn", m=d, n=d, kdim=tp, tm=_tile(d, 512, LANES), tn=_tile(d, 1024, LANES), tk=tp,
                                out_dtype=BF16, name=f"{tag}_dwout"))
    dz, g_mix = _mix_bwd(z, hs, dm, mp, layer, name=f"{tag}_bmix")
    tok = emit("w_in", _matmul(n_act, dz, mode="tn", m=d, n=d_in, kdim=tp, tm=_tile(d, 512, LANES), tn=_tile(d_in, 512, LANES),
                               tk=tp, out_dtype=BF16, name=f"{tag}_dwin", deps=[tok]))
    dn = _matmul(dz, w_in, mode="nt", m=tp, n=d, kdim=d_in, tm=_tile(tp, 1056), tn=_tile(d, 1024, LANES), tk=d_in,
                 out_dtype=F32, b_layer=0, name=f"{tag}_dn", deps=[tok])
    dh_in, dhb_in, g_norm = _rms_bwd(h, norm, layer, dn, dh, name=f"{tag}_brms")
    return dh_in, dhb_in, g_norm, g_mix


def _mesh_pos():
    x, y, c = lax.axis_index("x"), lax.axis_index("y"), lax.axis_index("c")
    chips = [(1 - x, y), (x, 1 - y), (1 - x, 1 - y)]
    return x, y, c, chips


def _block(ref, axis, j, size):
    idx = [slice(None)] * len(ref.shape)
    idx[axis] = pl.ds(j * size, size)
    return ref.at[tuple(idx)]


def _all_gather(shards, axes, *, name):
    n = len(shards)
    sizes = [s.shape[ax] for s, ax in zip(shards, axes)]
    full_shapes = [s.shape[:ax] + (N_DEV * s.shape[ax],) + s.shape[ax + 1:] for s, ax in zip(shards, axes)]

    def body(*refs):
        srcs, fulls = refs[:n], refs[n:2 * n]
        send_sems, recv_sems, local_sems = refs[2 * n:]
        x, y, c, chips = _mesh_pos()
        me, sib = (x, y, c), (x, y, 1 - c)

        def blk(a, dev):
            return _block(fulls[a], axes[a], 4 * dev[0] + 2 * dev[1] + dev[2], sizes[a])

        def copy(a, k, block_dev, to, src=None):
            return pltpu.make_async_remote_copy(
                src_ref=blk(a, block_dev) if src is None else src, dst_ref=blk(a, block_dev),
                send_sem=send_sems.at[a, k], recv_sem=recv_sems.at[a, k], device_id=to, device_id_type=MESH)

        mine = [pltpu.make_async_copy(srcs[a], blk(a, me), local_sems.at[a]) for a in range(n)]
        for cp in mine:
            cp.start()
        first = []
        for a in range(n):
            first.append(copy(a, 0, me, sib, src=srcs[a]))
            first += [copy(a, 1 + j, me, (*chip, c), src=srcs[a]) for j, chip in enumerate(chips)]
        for cp in first:
            cp.start()
        passed = []
        for j, chip in enumerate(chips):
            for a in range(n):
                copy(a, 1 + j, (*chip, c), me).wait_recv()
                fwd = copy(a, 4 + j, (*chip, c), sib)
                fwd.start()
                passed.append(fwd)
        for a in range(n):
            copy(a, 0, sib, me).wait_recv()
            for j, chip in enumerate(chips):
                copy(a, 4 + j, (*chip, 1 - c), me).wait_recv()
        for cp in first + passed:
            cp.wait_send()
        for cp in mine:
            cp.wait()

    any_spec = pl.BlockSpec(memory_space=pl.ANY)
    return pl.pallas_call(
        body, name=name,
        out_shape=[jax.ShapeDtypeStruct(fs, s.dtype) for fs, s in zip(full_shapes, shards)],
        in_specs=[any_spec] * n, out_specs=[any_spec] * n,
        scratch_shapes=[pltpu.SemaphoreType.DMA((n, 7)), pltpu.SemaphoreType.DMA((n, 7)), pltpu.SemaphoreType.DMA((n,))],
    )(*shards)


_HBM = pl.BlockSpec(memory_space=pltpu.HBM)
_SEM = pl.BlockSpec(memory_space=pltpu.SEMAPHORE)
_ANY = pl.BlockSpec(memory_space=pl.ANY)
_EFFECT = pltpu.SideEffectType.DATAFLOW_SIDE_EFFECTING
_N_OUT = 4


def _in_hbm(a):
    return pltpu.with_memory_space_constraint(a, pltpu.HBM)


def _gather_start(shards, axes, deps, *, name):
    n = len(shards)
    sizes = [s.shape[ax] for s, ax in zip(shards, axes)]
    full_shapes = [s.shape[:ax] + (N_DEV * s.shape[ax],) + s.shape[ax + 1:] for s, ax in zip(shards, axes)]

    def body(*refs):
        srcs, lands = refs[:n], refs[n:2 * n]
        send_sems, recv_sems = refs[2 * n + len(deps)], refs[2 * n + len(deps) + 1]
        token = refs[-1]
        x, y, c, chips = _mesh_pos()
        targets = [(x, y, 1 - c)] + [(*chip, c) for chip in chips]
        for a in range(n):
            dst = _block(lands[a], axes[a], 4 * x + 2 * y + c, sizes[a])
            for k, to in enumerate(targets):
                pltpu.make_async_remote_copy(src_ref=srcs[a], dst_ref=dst, send_sem=send_sems.at[_N_OUT * a + k],
                                             recv_sem=recv_sems.at[_N_OUT * a + k],
                                             device_id=to, device_id_type=MESH).start()
        token[...] = jnp.zeros(token.shape, F32)

    lands0 = [lax.empty(fs, s.dtype) for fs, s in zip(full_shapes, shards)]
    outs = pl.pallas_call(
        body, name=name,
        out_shape=(pltpu.SemaphoreType.DMA((n * _N_OUT,)), pltpu.SemaphoreType.DMA((n * _N_OUT,)),
                   *[pltpu.HBM(s.shape, s.dtype) for s in shards], *[pltpu.HBM(fs, s.dtype) for fs, s in zip(full_shapes, shards)],
                   jax.ShapeDtypeStruct((8, LANES), F32)),
        in_specs=[_HBM] * (2 * n) + [_ANY] * len(deps),
        out_specs=(_SEM, _SEM, *[_HBM] * (2 * n), pl.BlockSpec(memory_space=pltpu.VMEM)),
        input_output_aliases={a: 2 + a for a in range(2 * n)},
        compiler_params=pltpu.CompilerParams(has_side_effects=_EFFECT),
    )(*[_in_hbm(s) for s in shards], *[_in_hbm(z) for z in lands0], *deps)
    handle = dict(send_sems=outs[0], recv_sems=outs[1], srcs=list(outs[2:2 + n]), lands=list(outs[2 + n:2 + 2 * n]), axes=list(axes))
    return handle, outs[-1]


def _gather_wait(handle, after, *, name):
    srcs, lands, axes = handle["srcs"], handle["lands"], handle["axes"]
    n = len(srcs)
    sizes = [s.shape[ax] for s, ax in zip(srcs, axes)]

    def body(*refs):
        src_refs, land_refs = refs[:n], refs[n:2 * n]
        send_sems, recv_sems = refs[2 * n], refs[2 * n + 1]
        x, y, c, chips = _mesh_pos()
        peers = [(x, y, 1 - c)] + [(*chip, c) for chip in chips]
        for a in range(n):
            for k, dev in enumerate(peers):
                cp = pltpu.make_async_remote_copy(
                    src_ref=src_refs[a], dst_ref=_block(land_refs[a], axes[a], 4 * dev[0] + 2 * dev[1] + dev[2], sizes[a]),
                    send_sem=send_sems.at[_N_OUT * a + k], recv_sem=recv_sems.at[_N_OUT * a + k], device_id=dev,
                    device_id_type=MESH)
                cp.wait_send()
                cp.wait_recv()

    outs = pl.pallas_call(
        body, name=name,
        out_shape=(*[pltpu.HBM(s.shape, s.dtype) for s in srcs], *[pltpu.HBM(z.shape, z.dtype) for z in lands]),
        in_specs=[_HBM] * (2 * n) + [_SEM, _SEM] + [_ANY] * len(after),
        out_specs=tuple([_HBM] * (2 * n)),
        input_output_aliases={a: a for a in range(2 * n)},
        compiler_params=pltpu.CompilerParams(has_side_effects=_EFFECT),
    )(*srcs, *lands, handle["send_sems"], handle["recv_sems"], *after)
    return list(outs[:n]), list(outs[n:])


def _place_own(shard, full, axis, dev, *, name):
    nd = shard.ndim
    rows, cols = shard.shape[-2:]
    tr = _tile(rows, 512, _row_mult(shard.dtype)) if rows % _row_mult(shard.dtype) == 0 else rows
    nrb = rows // tr
    block = shard.shape[:-2] + (tr, cols)

    def in_map(r, dev_ref):
        return (0,) * (nd - 2) + (r, 0)

    def out_map(r, dev_ref):
        idx = [0] * nd
        idx[nd - 2] = r
        idx[axis] = dev_ref[0] * (nrb if axis == nd - 2 else 1) + idx[axis]
        return tuple(idx)

    def body(dev_ref, src_ref, full_ref, out_ref):
        out_ref[...] = src_ref[...]

    grid_spec = pltpu.PrefetchScalarGridSpec(
        num_scalar_prefetch=1, grid=(nrb,),
        in_specs=[pl.BlockSpec(block, in_map), _ANY], out_specs=pl.BlockSpec(block, out_map))
    return pl.pallas_call(body, name=name, grid_spec=grid_spec, out_shape=jax.ShapeDtypeStruct(full.shape, full.dtype),
                          input_output_aliases={2: 0}, compiler_params=_params(("arbitrary",)))(dev, shard, full)


def _gather_forward(fulls, axes, *, name):
    n = len(fulls)
    sizes = [z.shape[ax] // N_DEV for z, ax in zip(fulls, axes)]

    def body(*refs):
        outs = refs[n:2 * n]
        send_sems, recv_sems = refs[2 * n:]
        x, y, c, chips = _mesh_pos()
        sib = (x, y, 1 - c)

        def blk(a, dev):
            return _block(outs[a], axes[a], 4 * dev[0] + 2 * dev[1] + dev[2], sizes[a])

        passed = [pltpu.make_async_remote_copy(src_ref=blk(a, (*chip, c)), dst_ref=blk(a, (*chip, c)), send_sem=send_sems.at[a, j],
                                               recv_sem=recv_sems.at[a, j], device_id=sib, device_id_type=MESH)
                  for a in range(n) for j, chip in enumerate(chips)]
        for cp in passed:
            cp.start()
        for a in range(n):
            for j, chip in enumerate(chips):
                pltpu.make_async_remote_copy(src_ref=blk(a, (*chip, c)), dst_ref=blk(a, (*chip, 1 - c)), send_sem=send_sems.at[a, j],
                                             recv_sem=recv_sems.at[a, j], device_id=sib, device_id_type=MESH).wait_recv()
        for cp in passed:
            cp.wait_send()

    return pl.pallas_call(
        body, name=name,
        out_shape=[jax.ShapeDtypeStruct(z.shape, z.dtype) for z in fulls],
        in_specs=[_ANY] * n, out_specs=[_ANY] * n,
        input_output_aliases={a: a for a in range(n)},
        scratch_shapes=[pltpu.SemaphoreType.DMA((n, 3)), pltpu.SemaphoreType.DMA((n, 3))],
    )(*fulls)


def _rs_exchange_sibling(tensors, axes, *, name):
    n = len(tensors)
    n_layers = [len(t) for t in tensors]
    sizes = [t[0].shape[ax] // N_DEV for t, ax in zip(tensors, axes)]
    blk_shapes = [t[0].shape[:ax] + (sz,) + t[0].shape[ax + 1:] for t, ax, sz in zip(tensors, axes, sizes)]
    flat = [g for t in tensors for g in t]
    offs = [sum(n_layers[:a]) for a in range(n)]

    def body(*refs):
        srcs = refs[:len(flat)]
        lands = refs[len(flat):len(flat) + n]
        send_sems, recv_sems = refs[len(flat) + n:]
        x, y, c, _ = _mesh_pos()
        sib = (x, y, 1 - c)
        for a in range(n):
            for l in range(n_layers[a]):
                for i in range(N_CHIP):
                    pltpu.make_async_remote_copy(
                        src_ref=_block(srcs[offs[a] + l], axes[a], 2 * i + (1 - c), sizes[a]), dst_ref=lands[a].at[i, l],
                        send_sem=send_sems.at[a], recv_sem=recv_sems.at[a], device_id=sib, device_id_type=MESH).start()
        for a in range(n):
            pltpu.make_async_remote_copy(src_ref=lands[a], dst_ref=lands[a], send_sem=send_sems.at[a], recv_sem=recv_sems.at[a],
                                         device_id=sib, device_id_type=MESH).wait()

    any_spec = pl.BlockSpec(memory_space=pl.ANY)
    return pl.pallas_call(
        body, name=name,
        out_shape=[jax.ShapeDtypeStruct((N_CHIP, nl) + bs, t[0].dtype) for nl, bs, t in zip(n_layers, blk_shapes, tensors)],
        in_specs=[any_spec] * len(flat), out_specs=[any_spec] * n,
        scratch_shapes=[pltpu.SemaphoreType.DMA((n,)), pltpu.SemaphoreType.DMA((n,))],
    )(*flat)


def _rs_add_sibling(layers, axis, land, my_c, *, name):
    n_layers = len(layers)
    shape = layers[0].shape
    size = shape[axis] // N_DEV
    blk_shape = shape[:axis] + (size,) + shape[axis + 1:]
    nd = len(shape)
    rows = blk_shape[-2]
    tr = _tile(rows, 512, _row_mult(layers[0].dtype))
    inner = (tr, blk_shape[-1])
    lead = blk_shape[:-2]
    if lead:
        raise ValueError("blocked gradients are 2-D")
    nrb = rows // tr

    def src_map(l):
        def imap(i, r, c_ref):
            j = 2 * i + c_ref[0]
            return (j * nrb + r, 0) if axis == 0 else (r, j)
        return imap

    def body(c_ref, *refs):
        srcs = refs[:n_layers]
        land_ref = refs[n_layers]
        out_ref = refs[n_layers + 1]
        for l in range(n_layers):
            out_ref[l] = (srcs[l][...].astype(F32) + land_ref[l].astype(F32)).astype(out_ref.dtype)

    grid_spec = pltpu.PrefetchScalarGridSpec(
        num_scalar_prefetch=1, grid=(N_CHIP, nrb),
        in_specs=[pl.BlockSpec(inner, src_map(l)) for l in range(n_layers)]
        + [pl.BlockSpec((None, n_layers) + inner, lambda i, r, c_ref: (i, 0, r, 0))],
        out_specs=pl.BlockSpec((None, n_layers) + inner, lambda i, r, c_ref: (i, 0, r, 0)),
    )
    return pl.pallas_call(
        body, name=name, grid_spec=grid_spec,
        out_shape=jax.ShapeDtypeStruct((N_CHIP, n_layers) + blk_shape, layers[0].dtype),
        compiler_params=_params(("arbitrary", "arbitrary")),
    )(my_c, *layers, land)


def _rs_exchange_chips(parts, *, name):
    n = len(parts)

    def body(*refs):
        srcs, lands = refs[:n], refs[n:2 * n]
        send_sems, recv_sems, local_sems = refs[2 * n:]
        x, y, c, chips = _mesh_pos()
        mine = 2 * x + y
        local = [pltpu.make_async_copy(srcs[a].at[mine], lands[a].at[mine], local_sems.at[a]) for a in range(n)]
        for cp in local:
            cp.start()

        def copy(a, j, chip):
            return pltpu.make_async_remote_copy(
                src_ref=srcs[a].at[2 * chip[0] + chip[1]], dst_ref=lands[a].at[mine],
                send_sem=send_sems.at[a, j], recv_sem=recv_sems.at[a, j], device_id=(*chip, c), device_id_type=MESH)

        sends = [copy(a, j, chip) for a in range(n) for j, chip in enumerate(chips)]
        for cp in sends:
            cp.start()
        for a in range(n):
            for j, chip in enumerate(chips):
                pltpu.make_async_remote_copy(
                    src_ref=srcs[a].at[mine], dst_ref=lands[a].at[2 * chip[0] + chip[1]],
                    send_sem=send_sems.at[a, j], recv_sem=recv_sems.at[a, j], device_id=(*chip, c), device_id_type=MESH).wait_recv()
        for cp in sends:
            cp.wait_send()
        for cp in local:
            cp.wait()

    any_spec = pl.BlockSpec(memory_space=pl.ANY)
    return pl.pallas_call(
        body, name=name,
        out_shape=[jax.ShapeDtypeStruct(p.shape, p.dtype) for p in parts],
        in_specs=[any_spec] * n, out_specs=[any_spec] * n,
        scratch_shapes=[pltpu.SemaphoreType.DMA((n, 3)), pltpu.SemaphoreType.DMA((n, 3)), pltpu.SemaphoreType.DMA((n,))],
    )(*parts)


def _rs_chips_start(part, *, name):
    land = lax.empty(part.shape, part.dtype)

    def body(src, dst, send_sems, recv_sems, src_thru, dst_thru, token):
        x, y, c, chips = _mesh_pos()
        for j, chip in enumerate(chips):
            pltpu.make_async_remote_copy(src_ref=src.at[2 * chip[0] + chip[1]], dst_ref=dst.at[2 * x + y], send_sem=send_sems.at[j],
                                         recv_sem=recv_sems.at[j], device_id=(*chip, c), device_id_type=MESH).start()
        token[...] = jnp.zeros(token.shape, F32)

    outs = pl.pallas_call(
        body, name=name,
        out_shape=(pltpu.SemaphoreType.DMA((3,)), pltpu.SemaphoreType.DMA((3,)), pltpu.HBM(part.shape, part.dtype),
                   pltpu.HBM(land.shape, land.dtype), jax.ShapeDtypeStruct((8, LANES), F32)),
        in_specs=[_HBM, _HBM], out_specs=(_SEM, _SEM, _HBM, _HBM, pl.BlockSpec(memory_space=pltpu.VMEM)),
        input_output_aliases={0: 2, 1: 3},
        compiler_params=pltpu.CompilerParams(has_side_effects=_EFFECT),
    )(_in_hbm(part), _in_hbm(land))
    return dict(send_sems=outs[0], recv_sems=outs[1], src=outs[2], land=outs[3]), outs[4]


def _rs_chips_wait(handle, after, *, name):
    def body(src, dst, send_sems, recv_sems, *rest):
        x, y, c, chips = _mesh_pos()
        for j, chip in enumerate(chips):
            cp = pltpu.make_async_remote_copy(src_ref=src.at[2 * chip[0] + chip[1]], dst_ref=dst.at[2 * chip[0] + chip[1]],
                                              send_sem=send_sems.at[j], recv_sem=recv_sems.at[j], device_id=(*chip, c),
                                              device_id_type=MESH)
            cp.wait_send()
            cp.wait_recv()

    src, land = handle["src"], handle["land"]
    outs = pl.pallas_call(
        body, name=name,
        out_shape=(pltpu.HBM(src.shape, src.dtype), pltpu.HBM(land.shape, land.dtype)),
        in_specs=[_HBM, _HBM, _SEM, _SEM] + [_ANY] * len(after), out_specs=(_HBM, _HBM),
        input_output_aliases={0: 0, 1: 1},
        compiler_params=pltpu.CompilerParams(has_side_effects=_EFFECT),
    )(src, land, handle["send_sems"], handle["recv_sems"], *after)
    return outs[0], outs[1]


def _adam_math(w, g, m, v):
    m_new = ADAM_B1 * m + (1.0 - ADAM_B1) * g
    v_new = ADAM_B2 * v + (1.0 - ADAM_B2) * (g * g)
    m_hat = m_new / (1.0 - ADAM_B1 ** ADAM_STEP)
    v_hat = v_new / (1.0 - ADAM_B2 ** ADAM_STEP)
    delta = -ADAM_LR * (m_hat / (jnp.sqrt(v_hat) + ADAM_EPS) + ADAM_WD * w)
    return delta, m_new, v_new


def _sum_adam(part, land, slots, w, m, v, layer, prev, *, name):
    n_layers, rows, cols = w.shape
    tr = _tile(rows, 256, _row_mult(land.dtype))
    n_prev = 0 if prev is None else 4

    def body(slots_ref, p0_ref, p1_ref, p2_ref, p3_ref, w_ref, m_ref, v_ref, *rest):
        g_ref, d_ref, mo_ref, vo_ref = rest[n_prev:]
        g = p0_ref[...].astype(F32)
        for ref in (p1_ref, p2_ref, p3_ref):
            g = g + ref[...].astype(F32)
        delta, m_new, v_new = _adam_math(w_ref[...], g, m_ref[...], v_ref[...])
        g_ref[...] = g
        d_ref[...] = delta
        mo_ref[...] = m_new
        vo_ref[...] = v_new

    blk = pl.BlockSpec((None, tr, cols), lambda r, s: (layer, r, 0))
    slot = lambda j: pl.BlockSpec((None, None, tr, cols), lambda r, s: (s[j], 0, r, 0))
    shp = jax.ShapeDtypeStruct((n_layers, rows, cols), F32)
    grid_spec = pltpu.PrefetchScalarGridSpec(
        num_scalar_prefetch=1, grid=(rows // tr,),
        in_specs=[slot(0), slot(1), slot(2), slot(3), blk, blk, blk] + [_ANY] * n_prev,
        out_specs=(blk, blk, blk, blk))
    return pl.pallas_call(
        body, name=name, out_shape=(shp, shp, shp, shp), grid_spec=grid_spec,
        input_output_aliases={8 + i: i for i in range(n_prev)},
        compiler_params=_params(("parallel",)),
    )(slots, part, land, land, land, w, m, v, *(prev or ()))


def _sum_slots(land, *, name):
    _, _, rows, cols = land.shape
    tr = _tile(rows, 256, 8)

    def body(land_ref, g_ref):
        g = land_ref[0]
        for i in range(1, N_CHIP):
            g = g + land_ref[i]
        g_ref[...] = g

    return pl.pallas_call(
        body, name=name, out_shape=jax.ShapeDtypeStruct((rows, cols), F32), grid=(rows // tr,),
        in_specs=[pl.BlockSpec((N_CHIP, None, tr, cols), lambda r: (0, 0, r, 0))],
        out_specs=pl.BlockSpec((tr, cols), lambda r: (r, 0)),
        compiler_params=_params(("parallel",)),
    )(land)


def _adam_flat(w, g, m, v, *, name):
    rows, cols = w.shape
    tr = _tile(rows, 256, 8)

    def body(w_ref, g_ref, m_ref, v_ref, d_ref, mo_ref, vo_ref):
        delta, m_new, v_new = _adam_math(w_ref[...], g_ref[...], m_ref[...], v_ref[...])
        d_ref[...] = delta
        mo_ref[...] = m_new
        vo_ref[...] = v_new

    blk = pl.BlockSpec((tr, cols), lambda r: (r, 0))
    shp = jax.ShapeDtypeStruct((rows, cols), F32)
    return pl.pallas_call(
        body, name=name, out_shape=(shp, shp, shp), grid=(rows // tr,), in_specs=[blk] * 4, out_specs=(blk, blk, blk),
        compiler_params=_params(("parallel",)),
    )(w, g, m, v)


_BIG = (("ffn1_w_in", 1), ("ffn1_w_out", 0), ("w_in", 1), ("w_out", 0), ("ffn2_w_in", 1), ("ffn2_w_out", 0))
_SMALL = ("meta_tokens", "ffn1_norm", "mix_norm", "conv_w", "conv_b", "lru_wa", "lru_ba", "lru_wx", "lru_bx", "lru_a_param",
          "pool_w", "pool_b", "pool_scale", "ffn2_norm", "final_norm")
_SMALL_SHARD_AXIS = {"meta_tokens": 1, "conv_w": 2, "pool_w": 2}
_PACK_COLS = 1024


def _pack(arrs):
    flat = jnp.concatenate([a.reshape(-1) for a in arrs])
    unit = N_DEV * 8 * _PACK_COLS
    total = -(-flat.shape[0] // unit) * unit
    flat = jnp.pad(flat, (0, total - flat.shape[0]))
    return flat.reshape(total // _PACK_COLS, _PACK_COLS)


def _unpack(packed, shapes):
    flat = packed.reshape(-1)
    out, off = [], 0
    for s in shapes:
        size = 1
        for v in s:
            size *= v
        out.append(flat[off:off + size].reshape(s))
        off += size
    return out


def _my_shard(full, axis, dev):
    size = full.shape[axis] // N_DEV
    return lax.dynamic_slice_in_dim(full, dev * size, size, axis)


def kernel(x, meta_tokens, ffn1_norm, ffn1_w_in, ffn1_w_out, mix_norm, w_in, conv_w, conv_b, lru_wa, lru_ba, lru_wx, lru_bx, lru_a_param, pool_w, pool_b, pool_scale, w_out, ffn2_norm, ffn2_w_in, ffn2_w_out, final_norm, loss_target, m_meta_tokens, m_ffn1_norm, m_ffn1_w_in, m_ffn1_w_out, m_mix_norm, m_w_in, m_conv_w, m_conv_b, m_lru_wa, m_lru_ba, m_lru_wx, m_lru_bx, m_lru_a_param, m_pool_w, m_pool_b, m_pool_scale, m_w_out, m_ffn2_norm, m_ffn2_w_in, m_ffn2_w_out, m_final_norm, v_meta_tokens, v_ffn1_norm, v_ffn1_w_in, v_ffn1_w_out, v_mix_norm, v_w_in, v_conv_w, v_conv_b, v_lru_wa, v_lru_ba, v_lru_wx, v_lru_bx, v_lru_a_param, v_pool_w, v_pool_b, v_pool_scale, v_w_out, v_ffn2_norm, v_ffn2_w_in, v_ffn2_w_out, v_final_norm):
    names = ("meta_tokens", "ffn1_norm", "ffn1_w_in", "ffn1_w_out", "mix_norm", "w_in", "conv_w", "conv_b", "lru_wa", "lru_ba",
             "lru_wx", "lru_bx", "lru_a_param", "pool_w", "pool_b", "pool_scale", "w_out", "ffn2_norm", "ffn2_w_in", "ffn2_w_out",
             "final_norm")
    w = dict(zip(names, (meta_tokens, ffn1_norm, ffn1_w_in, ffn1_w_out, mix_norm, w_in, conv_w, conv_b, lru_wa, lru_ba, lru_wx,
                         lru_bx, lru_a_param, pool_w, pool_b, pool_scale, w_out, ffn2_norm, ffn2_w_in, ffn2_w_out, final_norm)))
    mom = dict(zip(names, (m_meta_tokens, m_ffn1_norm, m_ffn1_w_in, m_ffn1_w_out, m_mix_norm, m_w_in, m_conv_w, m_conv_b, m_lru_wa,
                           m_lru_ba, m_lru_wx, m_lru_bx, m_lru_a_param, m_pool_w, m_pool_b, m_pool_scale, m_w_out, m_ffn2_norm,
                           m_ffn2_w_in, m_ffn2_w_out, m_final_norm)))
    vel = dict(zip(names, (v_meta_tokens, v_ffn1_norm, v_ffn1_w_in, v_ffn1_w_out, v_mix_norm, v_w_in, v_conv_w, v_conv_b, v_lru_wa,
                           v_lru_ba, v_lru_wx, v_lru_bx, v_lru_a_param, v_pool_w, v_pool_b, v_pool_scale, v_w_out, v_ffn2_norm,
                           v_ffn2_w_in, v_ffn2_w_out, v_final_norm)))
    n_layers, d = ffn1_norm.shape
    n_meta = meta_tokens.shape[0]
    seq = x.shape[1]
    t_valid = n_meta + seq
    tp = -(-t_valid // ROW_ALIGN) * ROW_ALIGN
    dev = 4 * lax.axis_index("x") + 2 * lax.axis_index("y") + lax.axis_index("c")
    my_c = lax.axis_index("c").astype(jnp.int32).reshape(1)
    dev1 = dev.astype(jnp.int32).reshape(1)
    mx, my = lax.axis_index("x"), lax.axis_index("y")
    slots = jnp.stack([2 * mx + my, 2 * (1 - mx) + my, 2 * mx + (1 - my), 2 * (1 - mx) + (1 - my)]).astype(jnp.int32)

    big_axis = dict(_BIG)
    vec = lambda a: a.reshape(a.shape[0], 1, a.shape[1])
    norms = dict(ffn1=vec(ffn1_norm), mix=vec(mix_norm), ffn2=vec(ffn2_norm))
    units = [(kind, l) for l in range(n_layers) for kind in ("ffn1", "mix", "ffn2")]

    def shard(k, l):
        return w[k][l:l + 1].astype(BF16), big_axis[k] + 1

    groups = [[shard("ffn1_w_in", 0), (meta_tokens, 1)], [shard("ffn1_w_out", 0)]]
    for kind, l in units[1:]:
        if kind == "mix":
            groups.append([shard("w_in", l), shard("w_out", l)] + ([(conv_w, 2), (pool_w, 2)] if l == 0 else []))
        else:
            groups.append([shard(f"{kind}_w_in", l), shard(f"{kind}_w_out", l)])
    handles = {}
    state = dict(token=None, started=0)

    def order():
        return [] if state["token"] is None else [state["token"]]

    def start_next():
        i = state["started"]
        if i < len(groups):
            handles[i], state["token"] = _gather_start([s for s, _ in groups[i]], [ax for _, ax in groups[i]], order(),
                                                       name=f"gather{i}_start")
            state["started"] = i + 1

    def finish(i, after):
        handle = handles.pop(i)
        srcs, fulls = _gather_wait(handle, after, name=f"gather{i}_wait")
        fulls = [_place_own(s, z, ax, dev1, name=f"gather{i}_own{a}") for a, (s, z, ax) in enumerate(zip(srcs, fulls, handle["axes"]))]
        fulls = _gather_forward(fulls, handle["axes"], name=f"gather{i}_pass")
        start_next()
        return fulls

    start_next()
    start_next()
    w_in0, meta_full = finish(0, [])
    pad = jnp.zeros((tp - t_valid, d), F32)
    h0 = jnp.concatenate([meta_full, x[0], pad], axis=0)
    tgt = jnp.concatenate([jnp.zeros((n_meta, d), F32), loss_target[0], pad], axis=0)
    n_act, gu, act = _ffn_fwd_in(h0, norms["ffn1"], 0, w_in0, "l0ffn1", order())
    (w_out0,) = finish(1, [act])
    h = _ffn_fwd_out(h0, act, w_out0, "l0ffn1")
    gathered = {units[0]: (w_in0, w_out0)}
    saved = {units[0]: (h0, n_act, gu)}
    mp = None
    for ui, (kind, l) in enumerate(units[1:], start=1):
        fulls = finish(ui + 1, [h])
        tag = f"l{l}{kind}"
        if kind == "mix":
            if l == 0:
                mp = dict(conv_w=fulls[2], conv_b=vec(conv_b), wa=lru_wa.astype(BF16), ba=vec(lru_ba), wx=lru_wx.astype(BF16),
                          bx=vec(lru_bx), a_param=vec(lru_a_param), pool_w=fulls[3].astype(BF16), pool_b=vec(pool_b),
                          pool_scale=vec(pool_scale))
            h, saved[(kind, l)] = _mix_block_fwd(h, norms["mix"], l, fulls[0], fulls[1], mp, tag, order())
        else:
            n_act, gu, act = _ffn_fwd_in(h, norms[kind], l, fulls[0], tag, order())
            saved[(kind, l)] = (h, n_act, gu)
            h = _ffn_fwd_out(h, act, fulls[1], tag)
        gathered[(kind, l)] = (fulls[0], fulls[1])
    dh, dhb, g_final, loss_local = _final_loss(h, final_norm.reshape(1, d), tgt, n_meta, t_valid, name="final_loss")
    loss = lax.psum(loss_local[0, 0], ("x", "y", "c"))

    pending = []
    big_out = {}

    def drain(after):
        while pending:
            k, l, handle = pending.pop(0)
            part, land = _rs_chips_wait(handle, after, name=f"rs_{k}{l}_wait")
            big_out[k] = _sum_adam(part, land, slots, w[k], mom[k], vel[k], l, big_out.get(k), name=f"adam_{k}{l}")

    def emitter(kind, l):
        def emit(which, g):
            k = which if kind == "mix" else f"{kind}_{which}"
            if which == "w_out":
                drain([g])
            (land,) = _rs_exchange_sibling([[g]], [big_axis[k]], name=f"rs_{k}{l}_sib")
            part = _rs_add_sibling([g], big_axis[k], land, my_c, name=f"rs_{k}{l}_add")
            handle, token = _rs_chips_start(part, name=f"rs_{k}{l}_start")
            pending.append((k, l, handle))
            return token
        return emit

    g_norm = {k: [None] * n_layers for k in ("ffn1", "mix", "ffn2")}
    g_mix = [None] * n_layers
    for kind, l in reversed(units):
        w_i, w_o = gathered[(kind, l)]
        tag = f"l{l}{kind}"
        if kind == "mix":
            dh, dhb, g_norm[kind][l], g_mix[l] = _mix_block_bwd(dh, dhb, saved[(kind, l)], norms[kind], l, w_i, w_o, mp, tag,
                                                                 emitter(kind, l), [])
        else:
            dh, dhb, g_norm[kind][l] = _ffn_bwd(dh, dhb, saved[(kind, l)], norms[kind], l, w_i, w_o, tag, emitter(kind, l), [])
    drain([dh])
    dh0 = dh
    grad_x = dh0[n_meta:t_valid][None]

    mix_key = dict(conv_w="conv_w", conv_b="conv_b", lru_wa="wa", lru_ba="ba", lru_wx="wx", lru_bx="bx", lru_a_param="a_param",
                   pool_w="pool_w", pool_b="pool_b", pool_scale="pool_scale")
    small_local = {"meta_tokens": dh0[:n_meta], "final_norm": g_final.reshape(d)}
    for k in ("ffn1", "mix", "ffn2"):
        small_local[f"{k}_norm"] = jnp.stack([g.reshape(d) for g in g_norm[k]])
    for k, mk in mix_key.items():
        small_local[k] = jnp.stack([g_mix[l][mk] for l in range(n_layers)]).reshape(
            w[k].shape if k not in _SMALL_SHARD_AXIS else small_shape_full(w[k], _SMALL_SHARD_AXIS[k]))
    small_shapes = [small_local[k].shape for k in _SMALL]
    packed = _pack([small_local[k] for k in _SMALL])
    (land,) = _rs_exchange_sibling([[packed]], [0], name="rs_small_sib")
    part = _rs_add_sibling([packed], 0, land, my_c, name="rs_small_add")
    (land2,) = _rs_exchange_chips([part], name="rs_small_chips")

    out_g, out_d, out_m, out_v = {}, {}, {}, {}
    for k, _ in _BIG:
        out_g[k], out_d[k], out_m[k], out_v[k] = big_out[k]
    small_block = _sum_slots(land2, name="rs_sum_small")
    small_full = _all_gather([small_block], [0], name="gather_small")[0]
    small_g = dict(zip(_SMALL, _unpack(small_full, small_shapes)))
    for k, ax in _SMALL_SHARD_AXIS.items():
        small_g[k] = _my_shard(small_g[k], ax, dev)
    shapes_local = [w[k].shape for k in _SMALL]
    d_p, m_p, v_p = _adam_flat(_pack([w[k] for k in _SMALL]), _pack([small_g[k] for k in _SMALL]),
                               _pack([mom[k] for k in _SMALL]), _pack([vel[k] for k in _SMALL]), name="adam_small")
    for k, dd, mm, vv in zip(_SMALL, _unpack(d_p, shapes_local), _unpack(m_p, shapes_local), _unpack(v_p, shapes_local)):
        out_g[k], out_d[k], out_m[k], out_v[k] = small_g[k], dd, mm, vv

    return (loss, grad_x, *[out_g[k] for k in names], *[out_d[k] for k in names], *[out_m[k] for k in names],
            *[out_v[k] for k in names])


def small_shape_full(w_shard, axis):
    return w_shard.shape[:axis] + (N_DEV * w_shard.shape[axis],) + w_shard.shape[axis + 1:]
```

```python
import functools

import jax
import jax.numpy as jnp
from jax import lax
from jax.experimental import pallas as pl
from jax.experimental.pallas import tpu as pltpu

F32 = jnp.float32
BF16 = jnp.bfloat16
MESH = pl.DeviceIdType.MESH

N_DEV = 8
N_CHIP = 4
RMS_EPS = 1e-6
LRU_C = 8.0
CONV_WIDTH = 4
POOL_WINDOWS = (2, 4, 8, 16)
HIST = 16
ADAM_LR = 0.001
ADAM_B1 = 0.9
ADAM_B2 = 0.999
ADAM_EPS = 1e-08
ADAM_WD = 0.01
ADAM_STEP = 10
ROW_ALIGN = 128
V7X_VMEM_LIMIT = 56 * 1024 * 1024
BF16_ROWS = 16
K_TILE = 2816
SCAN_UNROLL = 8
EPILOGUE_ROWS = 528
LANES = 128


def _tile(n, cap, mult=BF16_ROWS):
    best = None
    d = mult
    while d <= min(n, cap):
        if n % d == 0:
            best = d
        d += mult
    if best is None:
        raise ValueError(f"no tile for {n} (cap {cap}, multiple of {mult})")
    return best


def _row_mult(dtype):
    return 8 * 4 // jnp.dtype(dtype).itemsize


def _params(sem=None):
    return pltpu.CompilerParams(dimension_semantics=sem, vmem_limit_bytes=V7X_VMEM_LIMIT)


def _dot(a, b, mode):
    dims = {"nn": ((1,), (0,)), "nt": ((1,), (1,)), "tn": ((0,), (0,))}[mode]
    return lax.dot_general(a, b, (dims, ((), ())), preferred_element_type=F32)


def _sigmoid(x):
    return 1.0 / (1.0 + jnp.exp(-x))


_GELU_C = 0.7978845608028654
_GELU_K = 0.044715


def _gelu(x):
    return 0.5 * x * (1.0 + jnp.tanh(_GELU_C * (x + _GELU_K * x * x * x)))


def _gelu_and_grad(x):
    x2 = x * x
    th = jnp.tanh(_GELU_C * x * (1.0 + _GELU_K * x2))
    half = 0.5 * (1.0 + th)
    return x * half, half + 0.5 * x * (1.0 - th * th) * _GELU_C * (1.0 + 3.0 * _GELU_K * x2)


def _neg_expm1(x):
    p = 1.0 / 5040.0
    for c in (1.0 / 720.0, 1.0 / 120.0, 1.0 / 24.0, 1.0 / 6.0, 0.5, 1.0):
        p = p * x + c
    return jnp.where(x > -0.25, -(x * p), 1.0 - jnp.exp(x))


def _softplus_neg(p):
    e = jnp.exp(-jnp.abs(p))
    u = 1.0 + e
    l1p = jnp.where(u == 1.0, e, jnp.log(u) * e / (u - 1.0 + (u == 1.0).astype(F32)))
    return jnp.maximum(-p, 0.0) + l1p


def _operand_spec(arr, br, bc, ridx, cidx, layer, split):
    if split:
        ncb = arr.shape[2] // bc
        return pl.BlockSpec((None, br, bc), lambda i, j, k: (cidx(i, j, k) // ncb, ridx(i, j, k), cidx(i, j, k) % ncb))
    if layer is not None:
        return pl.BlockSpec((None, br, bc), lambda i, j, k: (layer, ridx(i, j, k), cidx(i, j, k)))
    return pl.BlockSpec((br, bc), lambda i, j, k: (ridx(i, j, k), cidx(i, j, k)))


def _matmul(a, b, *, mode, m, n, kdim, tm, tn, tk, out_dtype, name, scale=None, residual=None,
            a_layer=None, b_layer=None, a_split=False, b_split=False, deps=()):
    nk = kdim // tk
    I = lambda i, j, k: i
    J = lambda i, j, k: j
    K = lambda i, j, k: k
    if mode == "nn":
        a_spec = _operand_spec(a, tm, tk, I, K, a_layer, a_split)
        b_spec = _operand_spec(b, tk, tn, K, J, b_layer, b_split)
    elif mode == "nt":
        a_spec = _operand_spec(a, tm, tk, I, K, a_layer, a_split)
        b_spec = _operand_spec(b, tn, tk, J, K, b_layer, b_split)
    else:
        a_spec = _operand_spec(a, tk, tm, K, I, a_layer, a_split)
        b_spec = _operand_spec(b, tk, tn, K, J, b_layer, b_split)
    in_specs = [a_spec, b_spec]
    operands = [a, b]
    if residual is not None:
        in_specs.append(pl.BlockSpec((tm, tn), lambda i, j, k: (i, j)))
        operands.append(residual)
    in_specs += [pl.BlockSpec(memory_space=pl.ANY)] * len(deps)
    operands += list(deps)
    n_in = len(operands)

    def finish(acc, res_ref, o_ref):
        if scale is not None:
            acc = acc * scale
        if res_ref is not None:
            acc = acc + res_ref[...]
        o_ref[...] = acc.astype(o_ref.dtype)

    def body(*refs):
        a_ref, b_ref = refs[0], refs[1]
        res_ref = refs[2] if residual is not None else None
        o_ref = refs[n_in]
        part = _dot(a_ref[...], b_ref[...], mode)
        if nk == 1:
            finish(part, res_ref, o_ref)
        else:
            acc_ref = refs[-1]
            k = pl.program_id(2)

            @pl.when(k == 0)
            def _():
                acc_ref[...] = part

            @pl.when(k > 0)
            def _():
                acc_ref[...] += part

            @pl.when(k == nk - 1)
            def _():
                finish(acc_ref[...], res_ref, o_ref)

    return pl.pallas_call(
        body, name=name,
        out_shape=jax.ShapeDtypeStruct((m, n), out_dtype),
        grid=(m // tm, n // tn, nk),
        in_specs=in_specs,
        out_specs=pl.BlockSpec((tm, tn), lambda i, j, k: (i, j)),
        scratch_shapes=[pltpu.VMEM((tm, tn), F32)] if nk > 1 else [],
        compiler_params=_params(("parallel", "parallel", "arbitrary")),
    )(*operands)


def _ffn_in_fwd(n_act, w_in, layer, *, name):
    tp, d = n_act.shape
    f = w_in.shape[2] // 2
    tm = _tile(tp, 2112)
    tn = _tile(f, 256, LANES)
    nj = f // tn

    ch = _tile(tm, EPILOGUE_ROWS)

    def body(n_ref, wg_ref, wu_ref, gu_ref, a_ref):
        for r in range(tm // ch):
            rows = pl.ds(r * ch, ch)
            x = n_ref[rows, :]
            g = _dot(x, wg_ref[...], "nn")
            u = _dot(x, wu_ref[...], "nn")
            gu_ref[0, rows, :] = g.astype(BF16)
            gu_ref[1, rows, :] = u.astype(BF16)
            a_ref[rows, :] = (g * _sigmoid(g) * u).astype(BF16)

    return pl.pallas_call(
        body, name=name,
        out_shape=(jax.ShapeDtypeStruct((2, tp, f), BF16), jax.ShapeDtypeStruct((tp, f), BF16)),
        grid=(tp // tm, nj),
        in_specs=[pl.BlockSpec((tm, d), lambda i, j: (i, 0)),
                  pl.BlockSpec((None, d, tn), lambda i, j: (layer, 0, j)),
                  pl.BlockSpec((None, d, tn), lambda i, j: (layer, 0, j + nj))],
        out_specs=(pl.BlockSpec((2, tm, tn), lambda i, j: (0, i, j)), pl.BlockSpec((tm, tn), lambda i, j: (i, j))),
        compiler_params=_params(("parallel", "arbitrary")),
    )(n_act, w_in, w_in)


def _ffn_bwd_act(dhb, w_out, layer, gu, *, name, deps=()):
    tp, d = dhb.shape
    f = w_out.shape[1]
    tm = _tile(tp, 2112)
    tn = _tile(f, 256, LANES)

    ch = _tile(tm, EPILOGUE_ROWS)

    def body(dh_ref, w_ref, gu_ref, *rest):
        dz_ref, a_ref = rest[len(deps):]
        for r in range(tm // ch):
            rows = pl.ds(r * ch, ch)
            da = 0.5 * _dot(dh_ref[rows, :], w_ref[...], "nt")
            g = gu_ref[0, rows, :].astype(F32)
            u = gu_ref[1, rows, :].astype(F32)
            s = _sigmoid(g)
            sg = g * s
            dz_ref[0, rows, :] = (da * u * (s * (1.0 + g * (1.0 - s)))).astype(BF16)
            dz_ref[1, rows, :] = (da * sg).astype(BF16)
            a_ref[rows, :] = (sg * u).astype(BF16)

    return pl.pallas_call(
        body, name=name,
        out_shape=(jax.ShapeDtypeStruct((2, tp, f), BF16), jax.ShapeDtypeStruct((tp, f), BF16)),
        grid=(tp // tm, f // tn),
        in_specs=[pl.BlockSpec((tm, d), lambda i, j: (i, 0)),
                  pl.BlockSpec((None, tn, d), lambda i, j: (layer, j, 0)),
                  pl.BlockSpec((2, tm, tn), lambda i, j: (0, i, j))] + [pl.BlockSpec(memory_space=pl.ANY)] * len(deps),
        out_specs=(pl.BlockSpec((2, tm, tn), lambda i, j: (0, i, j)), pl.BlockSpec((tm, tn), lambda i, j: (i, j))),
        compiler_params=_params(("parallel", "arbitrary")),
    )(dhb, w_out, gu, *deps)


def _rms_fwd(h, g, layer, *, name, deps=()):
    tp, d = h.shape
    tr = _tile(tp, 528)

    def body(h_ref, g_ref, *rest):
        n_ref = rest[len(deps)]
        x = h_ref[...]
        r = lax.rsqrt(jnp.mean(x * x, axis=-1, keepdims=True) + RMS_EPS)
        n_ref[...] = (x * r * g_ref[...]).astype(BF16)

    return pl.pallas_call(
        body, name=name, out_shape=jax.ShapeDtypeStruct((tp, d), BF16), grid=(tp // tr,),
        in_specs=[pl.BlockSpec((tr, d), lambda i: (i, 0)), pl.BlockSpec((None, 1, d), lambda i: (layer, 0, 0))]
        + [pl.BlockSpec(memory_space=pl.ANY)] * len(deps),
        out_specs=pl.BlockSpec((tr, d), lambda i: (i, 0)),
        compiler_params=_params(("parallel",)),
    )(h, g, *deps)


def _rms_bwd(h, g, layer, dn, dres, *, name, deps=()):
    tp, d = h.shape
    tr = _tile(tp, 528)

    def body(h_ref, g_ref, dn_ref, dres_ref, *rest):
        dh_ref, dhb_ref, dg_ref = rest[len(deps):]
        x = h_ref[...]
        r = lax.rsqrt(jnp.mean(x * x, axis=-1, keepdims=True) + RMS_EPS)
        xhat = x * r
        dn_v = dn_ref[...]
        dxhat = dn_v * g_ref[...]
        dh = dres_ref[...] + r * (dxhat - xhat * jnp.mean(dxhat * xhat, axis=-1, keepdims=True))
        dh_ref[...] = dh
        dhb_ref[...] = dh.astype(BF16)
        part = jnp.sum(dn_v * xhat, axis=0, keepdims=True)

        @pl.when(pl.program_id(0) == 0)
        def _():
            dg_ref[...] = part

        @pl.when(pl.program_id(0) > 0)
        def _():
            dg_ref[...] += part

    return pl.pallas_call(
        body, name=name,
        out_shape=(jax.ShapeDtypeStruct((tp, d), F32), jax.ShapeDtypeStruct((tp, d), BF16), jax.ShapeDtypeStruct((1, d), F32)),
        grid=(tp // tr,),
        in_specs=[pl.BlockSpec((tr, d), lambda i: (i, 0)), pl.BlockSpec((None, 1, d), lambda i: (layer, 0, 0)),
                  pl.BlockSpec((tr, d), lambda i: (i, 0)), pl.BlockSpec((tr, d), lambda i: (i, 0))]
        + [pl.BlockSpec(memory_space=pl.ANY)] * len(deps),
        out_specs=(pl.BlockSpec((tr, d), lambda i: (i, 0)), pl.BlockSpec((tr, d), lambda i: (i, 0)),
                   pl.BlockSpec((1, d), lambda i: (0, 0))),
        compiler_params=_params(("arbitrary",)),
    )(h, g, dn, dres, *deps)


def _final_loss(h, g, tgt, n_meta, t_valid, *, name):
    tp, d = h.shape
    tr = _tile(tp, 528)

    def body(h_ref, g_ref, t_ref, dh_ref, dhb_ref, dg_ref, loss_ref):
        i = pl.program_id(0)
        x = h_ref[...]
        r = lax.rsqrt(jnp.mean(x * x, axis=-1, keepdims=True) + RMS_EPS)
        xhat = x * r
        gv = g_ref[...]
        row = i * tr + lax.broadcasted_iota(jnp.int32, (tr, 1), 0)
        valid = jnp.logical_and(row >= n_meta, row < t_valid)
        err = jnp.where(valid, xhat * gv - t_ref[...], 0.0)
        dy = err * (1.0 / d)
        dxhat = dy * gv
        dh = r * (dxhat - xhat * jnp.mean(dxhat * xhat, axis=-1, keepdims=True))
        dh_ref[...] = dh
        dhb_ref[...] = dh.astype(BF16)
        dg_part = jnp.sum(dy * xhat, axis=0, keepdims=True)
        loss_part = jnp.sum(jnp.sum(err * err, axis=1, keepdims=True), axis=0, keepdims=True) * (0.5 / d)

        @pl.when(i == 0)
        def _():
            dg_ref[...] = dg_part
            loss_ref[...] = loss_part

        @pl.when(i > 0)
        def _():
            dg_ref[...] += dg_part
            loss_ref[...] += loss_part

    return pl.pallas_call(
        body, name=name,
        out_shape=(jax.ShapeDtypeStruct((tp, d), F32), jax.ShapeDtypeStruct((tp, d), BF16),
                   jax.ShapeDtypeStruct((1, d), F32), jax.ShapeDtypeStruct((1, 1), F32)),
        grid=(tp // tr,),
        in_specs=[pl.BlockSpec((tr, d), lambda i: (i, 0)), pl.BlockSpec((1, d), lambda i: (0, 0)),
                  pl.BlockSpec((tr, d), lambda i: (i, 0))],
        out_specs=(pl.BlockSpec((tr, d), lambda i: (i, 0)), pl.BlockSpec((tr, d), lambda i: (i, 0)),
                   pl.BlockSpec((1, d), lambda i: (0, 0)), pl.BlockSpec((1, 1), lambda i: (0, 0))),
        compiler_params=_params(("arbitrary",)),
    )(h, g, tgt)


def _mix_param_specs(mp, layer, imap):
    def vec(a):
        return pl.BlockSpec((None,) + a.shape[1:], lambda *g: (layer,) + (0,) * (a.ndim - 1))
    return [vec(mp[k]) for k in ("conv_w", "conv_b", "wa", "ba", "wx", "bx", "a_param", "pool_w", "pool_b", "pool_scale")]


def _mix_param_list(mp):
    return [mp[k] for k in ("conv_w", "conv_b", "wa", "ba", "wx", "bx", "a_param", "pool_w", "pool_b", "pool_scale")]


def _lru_gates(xc, wa_h, ba_h, wx_h, bx_h, sp_h):
    xb = xc.astype(BF16)
    ra = _sigmoid(_dot(xb, wa_h, "nn") + ba_h)
    ii = _sigmoid(_dot(xb, wx_h, "nn") + bx_h)
    la = -LRU_C * ra * sp_h
    return ra, ii, la, jnp.exp(la)


def _shifted(x, k):
    return x if k == 0 else pltpu.roll(x, k % x.shape[0], axis=0)


def _conv_taps(xbuf, sl, tc):
    x = xbuf[:, sl]
    return [_shifted(x, CONV_WIDTH - 1 - k)[HIST:HIST + tc] for k in range(CONV_WIDTH)]


def _conv_fwd(taps, cw_ref, cb_ref, sl):
    xc = cb_ref[:, sl]
    for k in range(CONV_WIDTH):
        xc = xc + cw_ref[k:k + 1, sl] * taps[k]
    return xc


def _window_sum(x, win, direction):
    s = x
    step = 1
    while step < win:
        s = s + _shifted(s, direction * step)
        step *= 2
    return s


def _pool_delta(pbuf, cols, win, t0, tc):
    x = pbuf[:, cols]
    u = x[HIST:HIST + tc]
    s = _window_sum(x, win, 1)[HIST:HIST + tc]
    t = t0 + lax.broadcasted_iota(jnp.int32, (tc, 1), 0)
    inv_cnt = 1.0 / jnp.minimum(t + 1, win).astype(F32)
    return s * inv_cnt - u, inv_cnt


def _mix_fwd(z, mp, layer, *, name):
    tp = z.shape[0]
    dl = z.shape[1] // 3
    n_heads, hd = mp["wa"].shape[1], mp["wa"].shape[2]
    n_groups, gd = mp["pool_w"].shape[1], mp["pool_w"].shape[2]
    tc = _tile(tp, 384)

    def body(z_ref, cw_ref, cb_ref, wa_ref, ba_ref, wx_ref, bx_ref, ap_ref, pw_ref, pb_ref, ps_ref,
             m_ref, hs_ref, xbuf, pbuf, a_s, b_s, hcar):
        i = pl.program_id(0)

        @pl.when(i == 0)
        def _():
            xbuf[pl.ds(0, HIST), :] = jnp.zeros((HIST, dl), F32)
            pbuf[pl.ds(0, HIST), :] = jnp.zeros((HIST, dl), F32)
            hcar[...] = jnp.zeros((1, dl), F32)

        @pl.when(i > 0)
        def _():
            xbuf[pl.ds(0, HIST), :] = xbuf[pl.ds(tc, HIST), :]
            pbuf[pl.ds(0, HIST), :] = pbuf[pl.ds(tc, HIST), :]

        xbuf[pl.ds(HIST, tc), :] = z_ref[:, pl.ds(0, dl)]
        pbuf[pl.ds(HIST, tc), :] = z_ref[:, pl.ds(2 * dl, dl)]
        sp = _softplus_neg(ap_ref[...])
        for h in range(n_heads):
            sl = pl.ds(h * hd, hd)
            xc = _conv_fwd(_conv_taps(xbuf, sl, tc), cw_ref, cb_ref, sl)
            _, ii, la, a = _lru_gates(xc, wa_ref[h], ba_ref[:, sl], wx_ref[h], bx_ref[:, sl], sp[:, h * hd:(h + 1) * hd])
            a_s[:, sl] = a
            b_s[:, sl] = jnp.sqrt(_neg_expm1(2.0 * la)) * ii * xc

        def step(t, hprev):
            hnew = a_s[pl.ds(t, 1), :] * hprev + b_s[pl.ds(t, 1), :]
            hs_ref[pl.ds(t, 1), :] = hnew
            return hnew

        hcar[...] = lax.fori_loop(0, tc, step, hcar[...], unroll=SCAN_UNROLL)
        for h in range(n_heads):
            sl = pl.ds(h * hd, hd)
            m_ref[:, sl] = (hs_ref[:, sl] * _gelu(z_ref[:, pl.ds(dl + h * hd, hd)])).astype(BF16)
        for g in range(n_groups):
            cols = pl.ds(g * gd, gd)
            dlt, _ = _pool_delta(pbuf, cols, POOL_WINDOWS[g], i * tc, tc)
            q = _dot(dlt.astype(BF16), pw_ref[g], "nn") + pb_ref[:, cols]
            m_ref[:, pl.ds(dl + g * gd, gd)] = (q * ps_ref[:, cols]).astype(BF16)

    return pl.pallas_call(
        body, name=name,
        out_shape=(jax.ShapeDtypeStruct((tp, 2 * dl), BF16), jax.ShapeDtypeStruct((tp, dl), F32)),
        grid=(tp // tc,),
        in_specs=[pl.BlockSpec((tc, 3 * dl), lambda i: (i, 0))] + _mix_param_specs(mp, layer, None),
        out_specs=(pl.BlockSpec((tc, 2 * dl), lambda i: (i, 0)), pl.BlockSpec((tc, dl), lambda i: (i, 0))),
        scratch_shapes=[pltpu.VMEM((HIST + tc, dl), F32), pltpu.VMEM((HIST + tc, dl), F32),
                        pltpu.VMEM((tc, dl), F32), pltpu.VMEM((tc, dl), F32), pltpu.VMEM((1, dl), F32)],
        compiler_params=_params(("arbitrary",)),
    )(z, *_mix_param_list(mp))


_MIX_GRADS = ("conv_w", "conv_b", "wa", "ba", "wx", "bx", "a_param", "pool_w", "pool_b", "pool_scale")


def _mix_bwd(z, hs, dm, mp, layer, *, name):
    tp = z.shape[0]
    dl = z.shape[1] // 3
    n_heads, hd = mp["wa"].shape[1], mp["wa"].shape[2]
    n_groups, gd = mp["pool_w"].shape[1], mp["pool_w"].shape[2]
    tc = _tile(tp, 384)
    nc = tp // tc
    per = tc // HIST

    def body(z_ref, zp_ref, hs_ref, hsp_ref, dm_ref, cw_ref, cb_ref, wa_ref, ba_ref, wx_ref, bx_ref, ap_ref, pw_ref, pb_ref,
             ps_ref, dz_ref, dcw_ref, dcb_ref, dwa_ref, dba_ref, dwx_ref, dbx_ref, dap_ref, dpw_ref, dpb_ref, dps_ref,
             xbuf, pbuf, hbuf, dxbuf, ddbuf, a_s, lam_s, ra_s, ii_s, xc_s, ccar):
        i = pl.program_id(0)
        ci = nc - 1 - i

        @pl.when(i == 0)
        def _():
            dxbuf[pl.ds(tc, HIST), :] = jnp.zeros((HIST, dl), F32)
            ddbuf[pl.ds(tc, HIST), :] = jnp.zeros((HIST, dl), F32)
            ccar[...] = jnp.zeros((1, dl), F32)
            for ref in (dcw_ref, dcb_ref, dwa_ref, dba_ref, dwx_ref, dbx_ref, dap_ref, dpw_ref, dpb_ref, dps_ref):
                ref[...] = jnp.zeros(ref.shape, F32)

        @pl.when(ci == 0)
        def _():
            xbuf[pl.ds(0, HIST), :] = jnp.zeros((HIST, dl), F32)
            pbuf[pl.ds(0, HIST), :] = jnp.zeros((HIST, dl), F32)
            hbuf[pl.ds(0, HIST), :] = jnp.zeros((HIST, dl), F32)

        @pl.when(ci > 0)
        def _():
            xbuf[pl.ds(0, HIST), :] = zp_ref[:, pl.ds(0, dl)]
            pbuf[pl.ds(0, HIST), :] = zp_ref[:, pl.ds(2 * dl, dl)]
            hbuf[pl.ds(0, HIST), :] = hsp_ref[...]

        xbuf[pl.ds(HIST, tc), :] = z_ref[:, pl.ds(0, dl)]
        pbuf[pl.ds(HIST, tc), :] = z_ref[:, pl.ds(2 * dl, dl)]
        hbuf[pl.ds(HIST, tc), :] = hs_ref[...]
        sp = _softplus_neg(ap_ref[...])

        for h in range(n_heads):
            sl = pl.ds(h * hd, hd)
            xc = _conv_fwd(_conv_taps(xbuf, sl, tc), cw_ref, cb_ref, sl)
            ra, ii, _, a = _lru_gates(xc, wa_ref[h], ba_ref[:, sl], wx_ref[h], bx_ref[:, sl], sp[:, h * hd:(h + 1) * hd])
            a_s[:, sl] = a
            ra_s[:, sl] = ra
            ii_s[:, sl] = ii
            xc_s[:, sl] = xc
            gel, gel_grad = _gelu_and_grad(z_ref[:, pl.ds(dl + h * hd, hd)])
            dya = dm_ref[:, sl]
            lam_s[:, sl] = dya * gel
            dz_ref[:, pl.ds(dl + h * hd, hd)] = (dya * hs_ref[:, sl] * gel_grad).astype(BF16)

        def step(r, carry):
            t = tc - 1 - r
            lam = lam_s[pl.ds(t, 1), :] + carry
            lam_s[pl.ds(t, 1), :] = lam
            return a_s[pl.ds(t, 1), :] * lam

        ccar[...] = lax.fori_loop(0, tc, step, ccar[...], unroll=SCAN_UNROLL)

        for h in range(n_heads):
            sl = pl.ds(h * hd, hd)
            sp_h = sp[:, h * hd:(h + 1) * hd]
            lam = lam_s[:, sl]
            a = a_s[:, sl]
            ra = ra_s[:, sl]
            ii = ii_s[:, sl]
            xc = xc_s[:, sl]
            e = _neg_expm1(-2.0 * LRU_C * ra * sp_h)
            inv_mult = lax.rsqrt(e)
            mult = e * inv_mult
            hprev = _shifted(hbuf[:, sl], 1)[HIST:HIST + tc]
            lam_i = lam * ii
            lam_m = lam * mult
            dla = lam * hprev * a - lam_i * xc * (a * a) * inv_mult
            dla_r = dla * (-LRU_C) * ra
            dap_ref[:, sl] += jnp.sum(dla_r, axis=0, keepdims=True)
            dpa = dla_r * sp_h * (1.0 - ra)
            dpx = lam_m * xc * ii * (1.0 - ii)
            dba_ref[:, sl] += jnp.sum(dpa, axis=0, keepdims=True)
            dbx_ref[:, sl] += jnp.sum(dpx, axis=0, keepdims=True)
            xb = xc.astype(BF16)
            dpa_b = dpa.astype(BF16)
            dpx_b = dpx.astype(BF16)
            dwa_ref[h] += _dot(xb, dpa_b, "tn")
            dwx_ref[h] += _dot(xb, dpx_b, "tn")
            dxc = lam_m * ii + _dot(dpa_b, wa_ref[h], "nt") + _dot(dpx_b, wx_ref[h], "nt")
            dxbuf[pl.ds(0, tc), sl] = dxc
            dcb_ref[:, sl] += jnp.sum(dxc, axis=0, keepdims=True)
            taps = _conv_taps(xbuf, sl, tc)
            dx_all = dxbuf[:, sl]
            dzx = jnp.zeros((tc, hd), F32)
            for k in range(CONV_WIDTH):
                dcw_ref[k:k + 1, sl] += jnp.sum(dxc * taps[k], axis=0, keepdims=True)
                dzx = dzx + cw_ref[k:k + 1, sl] * _shifted(dx_all, k - (CONV_WIDTH - 1))[0:tc]
            dz_ref[:, sl] = dzx.astype(BF16)
            dxbuf[pl.ds(tc, HIST), sl] = dxbuf[pl.ds(0, HIST), sl]

        for g in range(n_groups):
            cols = pl.ds(g * gd, gd)
            win = POOL_WINDOWS[g]
            dlt, inv_cnt = _pool_delta(pbuf, cols, win, ci * tc, tc)
            db = dlt.astype(BF16)
            q = _dot(db, pw_ref[g], "nn") + pb_ref[:, cols]
            dyb = dm_ref[:, pl.ds(dl + g * gd, gd)]
            dps_ref[:, cols] += jnp.sum(dyb * q, axis=0, keepdims=True)
            dq = dyb * ps_ref[:, cols]
            dpb_ref[:, cols] += jnp.sum(dq, axis=0, keepdims=True)
            dqb = dq.astype(BF16)
            dpw_ref[g] += _dot(db, dqb, "tn")
            dd = _dot(dqb, pw_ref[g], "nt")
            ddbuf[pl.ds(0, tc), cols] = dd * inv_cnt
            dzp = _window_sum(ddbuf[:, cols], win, -1)[0:tc] - dd
            dz_ref[:, pl.ds(2 * dl + g * gd, gd)] = dzp.astype(BF16)
            ddbuf[pl.ds(tc, HIST), cols] = ddbuf[pl.ds(0, HIST), cols]

        @pl.when(i == nc - 1)
        def _():
            dap_ref[...] = dap_ref[...] * (-_sigmoid(-ap_ref[...]))

    rev = lambda i: (nc - 1 - i, 0)
    prev = lambda i: (jnp.maximum((nc - 1 - i) * per - 1, 0), 0)
    const = lambda a: pl.BlockSpec(a.shape[1:], lambda i: (0,) * (a.ndim - 1))
    plist = _mix_param_list(mp)
    grad_shapes = [jax.ShapeDtypeStruct(a.shape[1:], F32) for a in plist]
    buf = lambda rows: pltpu.VMEM((rows, dl), F32)
    outs = pl.pallas_call(
        body, name=name,
        out_shape=[jax.ShapeDtypeStruct((tp, 3 * dl), BF16)] + grad_shapes,
        grid=(nc,),
        in_specs=[pl.BlockSpec((tc, 3 * dl), rev), pl.BlockSpec((HIST, 3 * dl), prev),
                  pl.BlockSpec((tc, dl), rev), pl.BlockSpec((HIST, dl), prev),
                  pl.BlockSpec((tc, 2 * dl), rev)] + _mix_param_specs(mp, layer, None),
        out_specs=[pl.BlockSpec((tc, 3 * dl), rev)] + [const(a) for a in plist],
        scratch_shapes=[buf(HIST + tc), buf(HIST + tc), buf(HIST + tc), buf(tc + HIST), buf(tc + HIST),
                        buf(tc), buf(tc), buf(tc), buf(tc), buf(tc), buf(1)],
        compiler_params=_params(("arbitrary",)),
    )(z, z, hs, hs, dm, *plist)
    return outs[0], dict(zip(_MIX_GRADS, outs[1:]))


def _ffn_fwd_in(h, norm, layer, w_in, tag, deps):
    n_act = _rms_fwd(h, norm, layer, name=f"{tag}_rms", deps=deps)
    gu, act = _ffn_in_fwd(n_act, w_in, 0, name=f"{tag}_in")
    return n_act, gu, act


def _ffn_fwd_out(h, act, w_out, tag):
    tp, d = h.shape
    f = w_out.shape[1]
    return _matmul(act, w_out, mode="nn", m=tp, n=d, kdim=f, tm=_tile(tp, 1056), tn=_tile(d, 1024, LANES),
                   tk=_tile(f, K_TILE, LANES), out_dtype=F32, scale=0.5, residual=h, b_layer=0, name=f"{tag}_out")


def _ffn_bwd(dh, dhb, saved, norm, layer, w_in, w_out, tag, emit, deps):
    h, n_act, gu = saved
    tp, d = h.shape
    f = w_out.shape[1]
    dz, act = _ffn_bwd_act(dhb, w_out, 0, gu, name=f"{tag}_bact", deps=deps)
    tok = emit("w_out", _matmul(act, dhb, mode="tn", m=f, n=d, kdim=tp, tm=_tile(f, 512, LANES), tn=_tile(d, 1024, LANES), tk=tp,
                                out_dtype=BF16, scale=0.5, name=f"{tag}_dwout"))
    tok = emit("w_in", _matmul(n_act, dz, mode="tn", m=d, n=2 * f, kdim=tp, tm=_tile(d, 512, LANES), tn=_tile(f, 512, LANES),
                               tk=tp, out_dtype=BF16, b_split=True, name=f"{tag}_dwin", deps=tok))
    dn = _matmul(dz, w_in, mode="nt", m=tp, n=d, kdim=2 * f, tm=_tile(tp, 1056), tn=_tile(d, 1024, LANES), tk=_tile(f, K_TILE, LANES),
                 out_dtype=F32, a_split=True, b_layer=0, name=f"{tag}_dn", deps=tok)
    return _rms_bwd(h, norm, layer, dn, dh, name=f"{tag}_brms", deps=emit("flush", dn))


def _mix_block_fwd(h, norm, layer, w_in, w_out, mp, tag, deps):
    tp, d = h.shape
    d_in = w_in.shape[2]
    n_act = _rms_fwd(h, norm, layer, name=f"{tag}_rms", deps=deps)
    z = _matmul(n_act, w_in, mode="nn", m=tp, n=d_in, kdim=d, tm=_tile(tp, 2112), tn=_tile(d_in, 512, LANES), tk=d,
                out_dtype=F32, b_layer=0, name=f"{tag}_in")
    m_act, hs = _mix_fwd(z, mp, layer, name=f"{tag}_mix")
    h_out = _matmul(m_act, w_out, mode="nn", m=tp, n=d, kdim=d, tm=_tile(tp, 2112), tn=_tile(d, 512, LANES), tk=d,
                    out_dtype=F32, residual=h, b_layer=0, name=f"{tag}_out")
    return h_out, (h, n_act, z, hs, m_act)


def _mix_block_bwd(dh, dhb, saved, norm, layer, w_in, w_out, mp, tag, emit, deps):
    h, n_act, z, hs, m_act = saved
    tp, d = h.shape
    d_in = w_in.shape[2]
    dm = _matmul(dhb, w_out, mode="nt", m=tp, n=d, kdim=d, tm=_tile(tp, 2112), tn=_tile(d, 512, LANES), tk=d,
                 out_dtype=F32, b_layer=0, name=f"{tag}_dm", deps=deps)
    tok = emit("w_out", _matmul(m_act, dhb, mode="tn", m=d, n=d, kdim=tp, tm=_tile(d, 512, LANES), tn=_tile(d, 1024, LANES), tk=tp,
                                out_dtype=BF16, name=f"{tag}_dwout"))
    dz, g_mix = _mix_bwd(z, hs, dm, mp, layer, name=f"{tag}_bmix")
    tok = emit("w_in", _matmul(n_act, dz, mode="tn", m=d, n=d_in, kdim=tp, tm=_tile(d, 512, LANES), tn=_tile(d_in, 512, LANES),
                               tk=tp, out_dtype=BF16, name=f"{tag}_dwin", deps=tok))
    dn = _matmul(dz, w_in, mode="nt", m=tp, n=d, kdim=d_in, tm=_tile(tp, 1056), tn=_tile(d, 1024, LANES), tk=d_in,
                 out_dtype=F32, b_layer=0, name=f"{tag}_dn", deps=tok)
    dh_in, dhb_in, g_norm = _rms_bwd(h, norm, layer, dn, dh, name=f"{tag}_brms", deps=emit("flush", dn))
    return dh_in, dhb_in, g_norm, g_mix


def _mesh_pos():
    x, y, c = lax.axis_index("x"), lax.axis_index("y"), lax.axis_index("c")
    chips = [(1 - x, y), (x, 1 - y), (1 - x, 1 - y)]
    return x, y, c, chips


def _block(ref, axis, j, size):
    idx = [slice(None)] * len(ref.shape)
    idx[axis] = pl.ds(j * size, size)
    return ref.at[tuple(idx)]


def _all_gather(shards, axes, *, name):
    n = len(shards)
    sizes = [s.shape[ax] for s, ax in zip(shards, axes)]
    full_shapes = [s.shape[:ax] + (N_DEV * s.shape[ax],) + s.shape[ax + 1:] for s, ax in zip(shards, axes)]

    def body(*refs):
        srcs, fulls = refs[:n], refs[n:2 * n]
        send_sems, recv_sems, local_sems = refs[2 * n:]
        x, y, c, chips = _mesh_pos()
        me, sib = (x, y, c), (x, y, 1 - c)

        def blk(a, dev):
            return _block(fulls[a], axes[a], 4 * dev[0] + 2 * dev[1] + dev[2], sizes[a])

        def copy(a, k, block_dev, to, src=None):
            return pltpu.make_async_remote_copy(
                src_ref=blk(a, block_dev) if src is None else src, dst_ref=blk(a, block_dev),
                send_sem=send_sems.at[a, k], recv_sem=recv_sems.at[a, k], device_id=to, device_id_type=MESH)

        mine = [pltpu.make_async_copy(srcs[a], blk(a, me), local_sems.at[a]) for a in range(n)]
        for cp in mine:
            cp.start()
        first = []
        for a in range(n):
            first.append(copy(a, 0, me, sib, src=srcs[a]))
            first += [copy(a, 1 + j, me, (*chip, c), src=srcs[a]) for j, chip in enumerate(chips)]
        for cp in first:
            cp.start()
        passed = []
        for j, chip in enumerate(chips):
            for a in range(n):
                copy(a, 1 + j, (*chip, c), me).wait_recv()
                fwd = copy(a, 4 + j, (*chip, c), sib)
                fwd.start()
                passed.append(fwd)
        for a in range(n):
            copy(a, 0, sib, me).wait_recv()
            for j, chip in enumerate(chips):
                copy(a, 4 + j, (*chip, 1 - c), me).wait_recv()
        for cp in first + passed:
            cp.wait_send()
        for cp in mine:
            cp.wait()

    any_spec = pl.BlockSpec(memory_space=pl.ANY)
    return pl.pallas_call(
        body, name=name,
        out_shape=[jax.ShapeDtypeStruct(fs, s.dtype) for fs, s in zip(full_shapes, shards)],
        in_specs=[any_spec] * n, out_specs=[any_spec] * n,
        scratch_shapes=[pltpu.SemaphoreType.DMA((n, 7)), pltpu.SemaphoreType.DMA((n, 7)), pltpu.SemaphoreType.DMA((n,))],
    )(*shards)


_HBM = pl.BlockSpec(memory_space=pltpu.HBM)
_SEM = pl.BlockSpec(memory_space=pltpu.SEMAPHORE)
_ANY = pl.BlockSpec(memory_space=pl.ANY)
_EFFECT = pltpu.SideEffectType.DATAFLOW_SIDE_EFFECTING
_N_OUT = 4


def _in_hbm(a):
    return pltpu.with_memory_space_constraint(a, pltpu.HBM)


def _gather_start(shards, axes, deps, *, name):
    n = len(shards)
    sizes = [s.shape[ax] for s, ax in zip(shards, axes)]
    full_shapes = [s.shape[:ax] + (N_DEV * s.shape[ax],) + s.shape[ax + 1:] for s, ax in zip(shards, axes)]

    def body(*refs):
        srcs, lands = refs[:n], refs[n:2 * n]
        send_sems, recv_sems = refs[2 * n + len(deps)], refs[2 * n + len(deps) + 1]
        token = refs[-1]
        x, y, c, chips = _mesh_pos()
        targets = [(x, y, 1 - c)] + [(*chip, c) for chip in chips]
        for a in range(n):
            dst = _block(lands[a], axes[a], 4 * x + 2 * y + c, sizes[a])
            for k, to in enumerate(targets):
                pltpu.make_async_remote_copy(src_ref=srcs[a], dst_ref=dst, send_sem=send_sems.at[_N_OUT * a + k],
                                             recv_sem=recv_sems.at[_N_OUT * a + k],
                                             device_id=to, device_id_type=MESH).start()
        token[...] = jnp.zeros(token.shape, F32)

    lands0 = [lax.empty(fs, s.dtype) for fs, s in zip(full_shapes, shards)]
    outs = pl.pallas_call(
        body, name=name,
        out_shape=(pltpu.SemaphoreType.DMA((n * _N_OUT,)), pltpu.SemaphoreType.DMA((n * _N_OUT,)),
                   *[pltpu.HBM(s.shape, s.dtype) for s in shards], *[pltpu.HBM(fs, s.dtype) for fs, s in zip(full_shapes, shards)],
                   jax.ShapeDtypeStruct((8, LANES), F32)),
        in_specs=[_HBM] * (2 * n) + [_ANY] * len(deps),
        out_specs=(_SEM, _SEM, *[_HBM] * (2 * n), pl.BlockSpec(memory_space=pltpu.VMEM)),
        input_output_aliases={a: 2 + a for a in range(2 * n)},
        compiler_params=pltpu.CompilerParams(has_side_effects=_EFFECT),
    )(*[_in_hbm(s) for s in shards], *[_in_hbm(z) for z in lands0], *deps)
    handle = dict(send_sems=outs[0], recv_sems=outs[1], srcs=list(outs[2:2 + n]), lands=list(outs[2 + n:2 + 2 * n]), axes=list(axes))
    return handle, outs[-1]


def _gather_wait(handle, after, *, name):
    srcs, lands, axes = handle["srcs"], handle["lands"], handle["axes"]
    n = len(srcs)
    sizes = [s.shape[ax] for s, ax in zip(srcs, axes)]

    def body(*refs):
        src_refs, land_refs = refs[:n], refs[n:2 * n]
        send_sems, recv_sems = refs[2 * n], refs[2 * n + 1]
        x, y, c, chips = _mesh_pos()
        peers = [(x, y, 1 - c)] + [(*chip, c) for chip in chips]
        for a in range(n):
            for k, dev in enumerate(peers):
                cp = pltpu.make_async_remote_copy(
                    src_ref=src_refs[a], dst_ref=_block(land_refs[a], axes[a], 4 * dev[0] + 2 * dev[1] + dev[2], sizes[a]),
                    send_sem=send_sems.at[_N_OUT * a + k], recv_sem=recv_sems.at[_N_OUT * a + k], device_id=dev,
                    device_id_type=MESH)
                cp.wait_send()
                cp.wait_recv()

    outs = pl.pallas_call(
        body, name=name,
        out_shape=(*[pltpu.HBM(s.shape, s.dtype) for s in srcs], *[pltpu.HBM(z.shape, z.dtype) for z in lands]),
        in_specs=[_HBM] * (2 * n) + [_SEM, _SEM] + [_ANY] * len(after),
        out_specs=tuple([_HBM] * (2 * n)),
        input_output_aliases={a: a for a in range(2 * n)},
        compiler_params=pltpu.CompilerParams(has_side_effects=_EFFECT),
    )(*srcs, *lands, handle["send_sems"], handle["recv_sems"], *after)
    return list(outs[:n]), list(outs[n:])


def _place_own(shard, full, axis, dev, *, name):
    nd = shard.ndim
    rows, cols = shard.shape[-2:]
    tr = _tile(rows, 512, _row_mult(shard.dtype)) if rows % _row_mult(shard.dtype) == 0 else rows
    nrb = rows // tr
    block = shard.shape[:-2] + (tr, cols)

    def in_map(r, dev_ref):
        return (0,) * (nd - 2) + (r, 0)

    def out_map(r, dev_ref):
        idx = [0] * nd
        idx[nd - 2] = r
        idx[axis] = dev_ref[0] * (nrb if axis == nd - 2 else 1) + idx[axis]
        return tuple(idx)

    def body(dev_ref, src_ref, full_ref, out_ref):
        out_ref[...] = src_ref[...]

    grid_spec = pltpu.PrefetchScalarGridSpec(
        num_scalar_prefetch=1, grid=(nrb,),
        in_specs=[pl.BlockSpec(block, in_map), _ANY], out_specs=pl.BlockSpec(block, out_map))
    return pl.pallas_call(body, name=name, grid_spec=grid_spec, out_shape=jax.ShapeDtypeStruct(full.shape, full.dtype),
                          input_output_aliases={2: 0}, compiler_params=_params(("arbitrary",)))(dev, shard, full)


def _gather_forward(fulls, axes, *, name):
    n = len(fulls)
    sizes = [z.shape[ax] // N_DEV for z, ax in zip(fulls, axes)]

    def body(*refs):
        outs = refs[n:2 * n]
        send_sems, recv_sems = refs[2 * n:]
        x, y, c, chips = _mesh_pos()
        sib = (x, y, 1 - c)

        def blk(a, dev):
            return _block(outs[a], axes[a], 4 * dev[0] + 2 * dev[1] + dev[2], sizes[a])

        passed = [pltpu.make_async_remote_copy(src_ref=blk(a, (*chip, c)), dst_ref=blk(a, (*chip, c)), send_sem=send_sems.at[a, j],
                                               recv_sem=recv_sems.at[a, j], device_id=sib, device_id_type=MESH)
                  for a in range(n) for j, chip in enumerate(chips)]
        for cp in passed:
            cp.start()
        for a in range(n):
            for j, chip in enumerate(chips):
                pltpu.make_async_remote_copy(src_ref=blk(a, (*chip, c)), dst_ref=blk(a, (*chip, 1 - c)), send_sem=send_sems.at[a, j],
                                             recv_sem=recv_sems.at[a, j], device_id=sib, device_id_type=MESH).wait_recv()
        for cp in passed:
            cp.wait_send()

    return pl.pallas_call(
        body, name=name,
        out_shape=[jax.ShapeDtypeStruct(z.shape, z.dtype) for z in fulls],
        in_specs=[_ANY] * n, out_specs=[_ANY] * n,
        input_output_aliases={a: a for a in range(n)},
        scratch_shapes=[pltpu.SemaphoreType.DMA((n, 3)), pltpu.SemaphoreType.DMA((n, 3))],
    )(*fulls)


def _rs_exchange_sibling(tensors, axes, *, name):
    n = len(tensors)
    n_layers = [len(t) for t in tensors]
    sizes = [t[0].shape[ax] // N_DEV for t, ax in zip(tensors, axes)]
    blk_shapes = [t[0].shape[:ax] + (sz,) + t[0].shape[ax + 1:] for t, ax, sz in zip(tensors, axes, sizes)]
    flat = [g for t in tensors for g in t]
    offs = [sum(n_layers[:a]) for a in range(n)]

    def body(*refs):
        srcs = refs[:len(flat)]
        lands = refs[len(flat):len(flat) + n]
        send_sems, recv_sems = refs[len(flat) + n:]
        x, y, c, _ = _mesh_pos()
        sib = (x, y, 1 - c)
        for a in range(n):
            for l in range(n_layers[a]):
                for i in range(N_CHIP):
                    pltpu.make_async_remote_copy(
                        src_ref=_block(srcs[offs[a] + l], axes[a], 2 * i + (1 - c), sizes[a]), dst_ref=lands[a].at[i, l],
                        send_sem=send_sems.at[a], recv_sem=recv_sems.at[a], device_id=sib, device_id_type=MESH).start()
        for a in range(n):
            pltpu.make_async_remote_copy(src_ref=lands[a], dst_ref=lands[a], send_sem=send_sems.at[a], recv_sem=recv_sems.at[a],
                                         device_id=sib, device_id_type=MESH).wait()

    any_spec = pl.BlockSpec(memory_space=pl.ANY)
    return pl.pallas_call(
        body, name=name,
        out_shape=[jax.ShapeDtypeStruct((N_CHIP, nl) + bs, t[0].dtype) for nl, bs, t in zip(n_layers, blk_shapes, tensors)],
        in_specs=[any_spec] * len(flat), out_specs=[any_spec] * n,
        scratch_shapes=[pltpu.SemaphoreType.DMA((n,)), pltpu.SemaphoreType.DMA((n,))],
    )(*flat)


def _rs_add_sibling(layers, axis, land, my_c, *, name):
    n_layers = len(layers)
    shape = layers[0].shape
    size = shape[axis] // N_DEV
    blk_shape = shape[:axis] + (size,) + shape[axis + 1:]
    nd = len(shape)
    rows = blk_shape[-2]
    tr = _tile(rows, 512, _row_mult(layers[0].dtype))
    inner = (tr, blk_shape[-1])
    lead = blk_shape[:-2]
    if lead:
        raise ValueError("blocked gradients are 2-D")
    nrb = rows // tr

    def src_map(l):
        def imap(i, r, c_ref):
            j = 2 * i + c_ref[0]
            return (j * nrb + r, 0) if axis == 0 else (r, j)
        return imap

    def body(c_ref, *refs):
        srcs = refs[:n_layers]
        land_ref = refs[n_layers]
        out_ref = refs[n_layers + 1]
        for l in range(n_layers):
            out_ref[l] = (srcs[l][...].astype(F32) + land_ref[l].astype(F32)).astype(out_ref.dtype)

    grid_spec = pltpu.PrefetchScalarGridSpec(
        num_scalar_prefetch=1, grid=(N_CHIP, nrb),
        in_specs=[pl.BlockSpec(inner, src_map(l)) for l in range(n_layers)]
        + [pl.BlockSpec((None, n_layers) + inner, lambda i, r, c_ref: (i, 0, r, 0))],
        out_specs=pl.BlockSpec((None, n_layers) + inner, lambda i, r, c_ref: (i, 0, r, 0)),
    )
    return pl.pallas_call(
        body, name=name, grid_spec=grid_spec,
        out_shape=jax.ShapeDtypeStruct((N_CHIP, n_layers) + blk_shape, layers[0].dtype),
        compiler_params=_params(("arbitrary", "arbitrary")),
    )(my_c, *layers, land)


def _rs_exchange_chips(parts, *, name):
    n = len(parts)

    def body(*refs):
        srcs, lands = refs[:n], refs[n:2 * n]
        send_sems, recv_sems, local_sems = refs[2 * n:]
        x, y, c, chips = _mesh_pos()
        mine = 2 * x + y
        local = [pltpu.make_async_copy(srcs[a].at[mine], lands[a].at[mine], local_sems.at[a]) for a in range(n)]
        for cp in local:
            cp.start()

        def copy(a, j, chip):
            return pltpu.make_async_remote_copy(
                src_ref=srcs[a].at[2 * chip[0] + chip[1]], dst_ref=lands[a].at[mine],
                send_sem=send_sems.at[a, j], recv_sem=recv_sems.at[a, j], device_id=(*chip, c), device_id_type=MESH)

        sends = [copy(a, j, chip) for a in range(n) for j, chip in enumerate(chips)]
        for cp in sends:
            cp.start()
        for a in range(n):
            for j, chip in enumerate(chips):
                pltpu.make_async_remote_copy(
                    src_ref=srcs[a].at[mine], dst_ref=lands[a].at[2 * chip[0] + chip[1]],
                    send_sem=send_sems.at[a, j], recv_sem=recv_sems.at[a, j], device_id=(*chip, c), device_id_type=MESH).wait_recv()
        for cp in sends:
            cp.wait_send()
        for cp in local:
            cp.wait()

    any_spec = pl.BlockSpec(memory_space=pl.ANY)
    return pl.pallas_call(
        body, name=name,
        out_shape=[jax.ShapeDtypeStruct(p.shape, p.dtype) for p in parts],
        in_specs=[any_spec] * n, out_specs=[any_spec] * n,
        scratch_shapes=[pltpu.SemaphoreType.DMA((n, 3)), pltpu.SemaphoreType.DMA((n, 3)), pltpu.SemaphoreType.DMA((n,))],
    )(*parts)


def _rs_sibling_start(g, axis, *, name):
    size = g.shape[axis] // N_DEV
    land = lax.empty((N_CHIP, 1) + g.shape[:axis] + (size,) + g.shape[axis + 1:], g.dtype)

    def body(src, dst, send_sems, recv_sems, src_thru, dst_thru, token):
        x, y, c, _ = _mesh_pos()
        for i in range(N_CHIP):
            pltpu.make_async_remote_copy(src_ref=_block(src, axis, 2 * i + (1 - c), size), dst_ref=dst.at[i, 0], send_sem=send_sems.at[i],
                                         recv_sem=recv_sems.at[i], device_id=(x, y, 1 - c), device_id_type=MESH).start()
        token[...] = jnp.zeros(token.shape, F32)

    outs = pl.pallas_call(
        body, name=name,
        out_shape=(pltpu.SemaphoreType.DMA((N_CHIP,)), pltpu.SemaphoreType.DMA((N_CHIP,)), pltpu.HBM(g.shape, g.dtype),
                   pltpu.HBM(land.shape, land.dtype), jax.ShapeDtypeStruct((8, LANES), F32)),
        in_specs=[_HBM, _HBM], out_specs=(_SEM, _SEM, _HBM, _HBM, pl.BlockSpec(memory_space=pltpu.VMEM)),
        input_output_aliases={0: 2, 1: 3},
        compiler_params=pltpu.CompilerParams(has_side_effects=_EFFECT),
    )(_in_hbm(g), _in_hbm(land))
    return dict(send_sems=outs[0], recv_sems=outs[1], src=outs[2], land=outs[3], axis=axis), outs[4]


def _rs_sibling_wait(handle, after, *, name):
    src, land, axis = handle["src"], handle["land"], handle["axis"]
    size = src.shape[axis] // N_DEV

    def body(src_ref, dst_ref, send_sems, recv_sems, *rest):
        x, y, c, _ = _mesh_pos()
        for i in range(N_CHIP):
            cp = pltpu.make_async_remote_copy(src_ref=_block(src_ref, axis, 2 * i + (1 - c), size), dst_ref=dst_ref.at[i, 0],
                                              send_sem=send_sems.at[i], recv_sem=recv_sems.at[i], device_id=(x, y, 1 - c),
                                              device_id_type=MESH)
            cp.wait_send()
            cp.wait_recv()

    outs = pl.pallas_call(
        body, name=name,
        out_shape=(pltpu.HBM(src.shape, src.dtype), pltpu.HBM(land.shape, land.dtype)),
        in_specs=[_HBM, _HBM, _SEM, _SEM] + [_ANY] * len(after), out_specs=(_HBM, _HBM),
        input_output_aliases={0: 0, 1: 1},
        compiler_params=pltpu.CompilerParams(has_side_effects=_EFFECT),
    )(src, land, handle["send_sems"], handle["recv_sems"], *after)
    return outs[0], outs[1]


def _rs_chips_start(part, *, name):
    land = lax.empty(part.shape, part.dtype)

    def body(src, dst, send_sems, recv_sems, src_thru, dst_thru, token):
        x, y, c, chips = _mesh_pos()
        for j, chip in enumerate(chips):
            pltpu.make_async_remote_copy(src_ref=src.at[2 * chip[0] + chip[1]], dst_ref=dst.at[2 * x + y], send_sem=send_sems.at[j],
                                         recv_sem=recv_sems.at[j], device_id=(*chip, c), device_id_type=MESH).start()
        token[...] = jnp.zeros(token.shape, F32)

    outs = pl.pallas_call(
        body, name=name,
        out_shape=(pltpu.SemaphoreType.DMA((3,)), pltpu.SemaphoreType.DMA((3,)), pltpu.HBM(part.shape, part.dtype),
                   pltpu.HBM(land.shape, land.dtype), jax.ShapeDtypeStruct((8, LANES), F32)),
        in_specs=[_HBM, _HBM], out_specs=(_SEM, _SEM, _HBM, _HBM, pl.BlockSpec(memory_space=pltpu.VMEM)),
        input_output_aliases={0: 2, 1: 3},
        compiler_params=pltpu.CompilerParams(has_side_effects=_EFFECT),
    )(_in_hbm(part), _in_hbm(land))
    return dict(send_sems=outs[0], recv_sems=outs[1], src=outs[2], land=outs[3]), outs[4]


def _rs_chips_wait(handle, after, *, name):
    def body(src, dst, send_sems, recv_sems, *rest):
        x, y, c, chips = _mesh_pos()
        for j, chip in enumerate(chips):
            cp = pltpu.make_async_remote_copy(src_ref=src.at[2 * chip[0] + chip[1]], dst_ref=dst.at[2 * chip[0] + chip[1]],
                                              send_sem=send_sems.at[j], recv_sem=recv_sems.at[j], device_id=(*chip, c),
                                              device_id_type=MESH)
            cp.wait_send()
            cp.wait_recv()

    src, land = handle["src"], handle["land"]
    outs = pl.pallas_call(
        body, name=name,
        out_shape=(pltpu.HBM(src.shape, src.dtype), pltpu.HBM(land.shape, land.dtype)),
        in_specs=[_HBM, _HBM, _SEM, _SEM] + [_ANY] * len(after), out_specs=(_HBM, _HBM),
        input_output_aliases={0: 0, 1: 1},
        compiler_params=pltpu.CompilerParams(has_side_effects=_EFFECT),
    )(src, land, handle["send_sems"], handle["recv_sems"], *after)
    return outs[0], outs[1]


def _adam_math(w, g, m, v):
    m_new = ADAM_B1 * m + (1.0 - ADAM_B1) * g
    v_new = ADAM_B2 * v + (1.0 - ADAM_B2) * (g * g)
    m_hat = m_new / (1.0 - ADAM_B1 ** ADAM_STEP)
    v_hat = v_new / (1.0 - ADAM_B2 ** ADAM_STEP)
    delta = -ADAM_LR * (m_hat / (jnp.sqrt(v_hat) + ADAM_EPS) + ADAM_WD * w)
    return delta, m_new, v_new


def _sum_adam(part, land, slots, w, m, v, layer, prev, *, name):
    n_layers, rows, cols = w.shape
    tr = _tile(rows, 256, _row_mult(land.dtype))
    n_prev = 0 if prev is None else 4

    def body(slots_ref, p0_ref, p1_ref, p2_ref, p3_ref, w_ref, m_ref, v_ref, *rest):
        g_ref, d_ref, mo_ref, vo_ref = rest[n_prev:]
        g = p0_ref[...].astype(F32)
        for ref in (p1_ref, p2_ref, p3_ref):
            g = g + ref[...].astype(F32)
        delta, m_new, v_new = _adam_math(w_ref[...], g, m_ref[...], v_ref[...])
        g_ref[...] = g
        d_ref[...] = delta
        mo_ref[...] = m_new
        vo_ref[...] = v_new

    blk = pl.BlockSpec((None, tr, cols), lambda r, s: (layer, r, 0))
    slot = lambda j: pl.BlockSpec((None, None, tr, cols), lambda r, s: (s[j], 0, r, 0))
    shp = jax.ShapeDtypeStruct((n_layers, rows, cols), F32)
    grid_spec = pltpu.PrefetchScalarGridSpec(
        num_scalar_prefetch=1, grid=(rows // tr,),
        in_specs=[slot(0), slot(1), slot(2), slot(3), blk, blk, blk] + [_ANY] * n_prev,
        out_specs=(blk, blk, blk, blk))
    return pl.pallas_call(
        body, name=name, out_shape=(shp, shp, shp, shp), grid_spec=grid_spec,
        input_output_aliases={8 + i: i for i in range(n_prev)},
        compiler_params=_params(("parallel",)),
    )(slots, part, land, land, land, w, m, v, *(prev or ()))


def _sum_slots(land, *, name):
    _, _, rows, cols = land.shape
    tr = _tile(rows, 256, 8)

    def body(land_ref, g_ref):
        g = land_ref[0]
        for i in range(1, N_CHIP):
            g = g + land_ref[i]
        g_ref[...] = g

    return pl.pallas_call(
        body, name=name, out_shape=jax.ShapeDtypeStruct((rows, cols), F32), grid=(rows // tr,),
        in_specs=[pl.BlockSpec((N_CHIP, None, tr, cols), lambda r: (0, 0, r, 0))],
        out_specs=pl.BlockSpec((tr, cols), lambda r: (r, 0)),
        compiler_params=_params(("parallel",)),
    )(land)


def _adam_flat(w, g, m, v, *, name):
    rows, cols = w.shape
    tr = _tile(rows, 256, 8)

    def body(w_ref, g_ref, m_ref, v_ref, d_ref, mo_ref, vo_ref):
        delta, m_new, v_new = _adam_math(w_ref[...], g_ref[...], m_ref[...], v_ref[...])
        d_ref[...] = delta
        mo_ref[...] = m_new
        vo_ref[...] = v_new

    blk = pl.BlockSpec((tr, cols), lambda r: (r, 0))
    shp = jax.ShapeDtypeStruct((rows, cols), F32)
    return pl.pallas_call(
        body, name=name, out_shape=(shp, shp, shp), grid=(rows // tr,), in_specs=[blk] * 4, out_specs=(blk, blk, blk),
        compiler_params=_params(("parallel",)),
    )(w, g, m, v)


_BIG = (("ffn1_w_in", 1), ("ffn1_w_out", 0), ("w_in", 1), ("w_out", 0), ("ffn2_w_in", 1), ("ffn2_w_out", 0))
_SMALL = ("meta_tokens", "ffn1_norm", "mix_norm", "conv_w", "conv_b", "lru_wa", "lru_ba", "lru_wx", "lru_bx", "lru_a_param",
          "pool_w", "pool_b", "pool_scale", "ffn2_norm", "final_norm")
_SMALL_SHARD_AXIS = {"meta_tokens": 1, "conv_w": 2, "pool_w": 2}
_PACK_COLS = 1024


def _pack(arrs):
    flat = jnp.concatenate([a.reshape(-1) for a in arrs])
    unit = N_DEV * 8 * _PACK_COLS
    total = -(-flat.shape[0] // unit) * unit
    flat = jnp.pad(flat, (0, total - flat.shape[0]))
    return flat.reshape(total // _PACK_COLS, _PACK_COLS)


def _unpack(packed, shapes):
    flat = packed.reshape(-1)
    out, off = [], 0
    for s in shapes:
        size = 1
        for v in s:
            size *= v
        out.append(flat[off:off + size].reshape(s))
        off += size
    return out


def _my_shard(full, axis, dev):
    size = full.shape[axis] // N_DEV
    return lax.dynamic_slice_in_dim(full, dev * size, size, axis)


def kernel(x, meta_tokens, ffn1_norm, ffn1_w_in, ffn1_w_out, mix_norm, w_in, conv_w, conv_b, lru_wa, lru_ba, lru_wx, lru_bx, lru_a_param, pool_w, pool_b, pool_scale, w_out, ffn2_norm, ffn2_w_in, ffn2_w_out, final_norm, loss_target, m_meta_tokens, m_ffn1_norm, m_ffn1_w_in, m_ffn1_w_out, m_mix_norm, m_w_in, m_conv_w, m_conv_b, m_lru_wa, m_lru_ba, m_lru_wx, m_lru_bx, m_lru_a_param, m_pool_w, m_pool_b, m_pool_scale, m_w_out, m_ffn2_norm, m_ffn2_w_in, m_ffn2_w_out, m_final_norm, v_meta_tokens, v_ffn1_norm, v_ffn1_w_in, v_ffn1_w_out, v_mix_norm, v_w_in, v_conv_w, v_conv_b, v_lru_wa, v_lru_ba, v_lru_wx, v_lru_bx, v_lru_a_param, v_pool_w, v_pool_b, v_pool_scale, v_w_out, v_ffn2_norm, v_ffn2_w_in, v_ffn2_w_out, v_final_norm):
    names = ("meta_tokens", "ffn1_norm", "ffn1_w_in", "ffn1_w_out", "mix_norm", "w_in", "conv_w", "conv_b", "lru_wa", "lru_ba",
             "lru_wx", "lru_bx", "lru_a_param", "pool_w", "pool_b", "pool_scale", "w_out", "ffn2_norm", "ffn2_w_in", "ffn2_w_out",
             "final_norm")
    w = dict(zip(names, (meta_tokens, ffn1_norm, ffn1_w_in, ffn1_w_out, mix_norm, w_in, conv_w, conv_b, lru_wa, lru_ba, lru_wx,
                         lru_bx, lru_a_param, pool_w, pool_b, pool_scale, w_out, ffn2_norm, ffn2_w_in, ffn2_w_out, final_norm)))
    mom = dict(zip(names, (m_meta_tokens, m_ffn1_norm, m_ffn1_w_in, m_ffn1_w_out, m_mix_norm, m_w_in, m_conv_w, m_conv_b, m_lru_wa,
                           m_lru_ba, m_lru_wx, m_lru_bx, m_lru_a_param, m_pool_w, m_pool_b, m_pool_scale, m_w_out, m_ffn2_norm,
                           m_ffn2_w_in, m_ffn2_w_out, m_final_norm)))
    vel = dict(zip(names, (v_meta_tokens, v_ffn1_norm, v_ffn1_w_in, v_ffn1_w_out, v_mix_norm, v_w_in, v_conv_w, v_conv_b, v_lru_wa,
                           v_lru_ba, v_lru_wx, v_lru_bx, v_lru_a_param, v_pool_w, v_pool_b, v_pool_scale, v_w_out, v_ffn2_norm,
                           v_ffn2_w_in, v_ffn2_w_out, v_final_norm)))
    n_layers, d = ffn1_norm.shape
    n_meta = meta_tokens.shape[0]
    seq = x.shape[1]
    t_valid = n_meta + seq
    tp = -(-t_valid // ROW_ALIGN) * ROW_ALIGN
    dev = 4 * lax.axis_index("x") + 2 * lax.axis_index("y") + lax.axis_index("c")
    my_c = lax.axis_index("c").astype(jnp.int32).reshape(1)
    dev1 = dev.astype(jnp.int32).reshape(1)
    mx, my = lax.axis_index("x"), lax.axis_index("y")
    slots = jnp.stack([2 * mx + my, 2 * (1 - mx) + my, 2 * mx + (1 - my), 2 * (1 - mx) + (1 - my)]).astype(jnp.int32)

    big_axis = dict(_BIG)
    vec = lambda a: a.reshape(a.shape[0], 1, a.shape[1])
    norms = dict(ffn1=vec(ffn1_norm), mix=vec(mix_norm), ffn2=vec(ffn2_norm))
    units = [(kind, l) for l in range(n_layers) for kind in ("ffn1", "mix", "ffn2")]

    def shard(k, l):
        return w[k][l:l + 1].astype(BF16), big_axis[k] + 1

    groups = [[shard("ffn1_w_in", 0), (meta_tokens, 1)], [shard("ffn1_w_out", 0)]]
    for kind, l in units[1:]:
        if kind == "mix":
            groups.append([shard("w_in", l), shard("w_out", l)] + ([(conv_w, 2), (pool_w, 2)] if l == 0 else []))
        else:
            groups.append([shard(f"{kind}_w_in", l), shard(f"{kind}_w_out", l)])
    handles = {}
    state = dict(token=None, started=0)

    def order():
        return [] if state["token"] is None else [state["token"]]

    def start_next():
        i = state["started"]
        if i < len(groups):
            handles[i], state["token"] = _gather_start([s for s, _ in groups[i]], [ax for _, ax in groups[i]], order(),
                                                       name=f"gather{i}_start")
            state["started"] = i + 1

    def finish(i, after):
        handle = handles.pop(i)
        srcs, fulls = _gather_wait(handle, after, name=f"gather{i}_wait")
        fulls = [_place_own(s, z, ax, dev1, name=f"gather{i}_own{a}") for a, (s, z, ax) in enumerate(zip(srcs, fulls, handle["axes"]))]
        fulls = _gather_forward(fulls, handle["axes"], name=f"gather{i}_pass")
        start_next()
        return fulls

    start_next()
    start_next()
    w_in0, meta_full = finish(0, [])
    pad = jnp.zeros((tp - t_valid, d), F32)
    h0 = jnp.concatenate([meta_full, x[0], pad], axis=0)
    tgt = jnp.concatenate([jnp.zeros((n_meta, d), F32), loss_target[0], pad], axis=0)
    n_act, gu, act = _ffn_fwd_in(h0, norms["ffn1"], 0, w_in0, "l0ffn1", order())
    (w_out0,) = finish(1, [act])
    h = _ffn_fwd_out(h0, act, w_out0, "l0ffn1")
    gathered = {units[0]: (w_in0, w_out0)}
    saved = {units[0]: (h0, n_act, gu)}
    mp = None
    for ui, (kind, l) in enumerate(units[1:], start=1):
        fulls = finish(ui + 1, [h])
        tag = f"l{l}{kind}"
        if kind == "mix":
            if l == 0:
                mp = dict(conv_w=fulls[2], conv_b=vec(conv_b), wa=lru_wa.astype(BF16), ba=vec(lru_ba), wx=lru_wx.astype(BF16),
                          bx=vec(lru_bx), a_param=vec(lru_a_param), pool_w=fulls[3].astype(BF16), pool_b=vec(pool_b),
                          pool_scale=vec(pool_scale))
            h, saved[(kind, l)] = _mix_block_fwd(h, norms["mix"], l, fulls[0], fulls[1], mp, tag, order())
        else:
            n_act, gu, act = _ffn_fwd_in(h, norms[kind], l, fulls[0], tag, order())
            saved[(kind, l)] = (h, n_act, gu)
            h = _ffn_fwd_out(h, act, fulls[1], tag)
        gathered[(kind, l)] = (fulls[0], fulls[1])
    dh, dhb, g_final, loss_local = _final_loss(h, final_norm.reshape(1, d), tgt, n_meta, t_valid, name="final_loss")
    loss = lax.psum(loss_local[0, 0], ("x", "y", "c"))

    pending = []
    big_out = {}

    def drain(after):
        while pending:
            k, l, handle = pending.pop(0)
            part, land = _rs_chips_wait(handle, after, name=f"rs_{k}{l}_wait")
            big_out[k] = _sum_adam(part, land, slots, w[k], mom[k], vel[k], l, big_out.get(k), name=f"adam_{k}{l}")

    sib_pending = []

    def to_chips(k, l, g, land):
        part = _rs_add_sibling([g], big_axis[k], land, my_c, name=f"rs_{k}{l}_add")
        handle, token = _rs_chips_start(part, name=f"rs_{k}{l}_start")
        pending.append((k, l, handle))
        return token

    def finish_sibling(after):
        token = None
        while sib_pending:
            k, l, handle = sib_pending.pop(0)
            g, land = _rs_sibling_wait(handle, after, name=f"rs_{k}{l}_sibwait")
            token = to_chips(k, l, g, land)
        return token

    def emitter(kind, l):
        last_unit = (kind, l) == units[0]

        def emit(which, g):
            if which == "w_out":
                drain([g])
            token = finish_sibling([g])
            if which == "flush":
                return [] if token is None else [token]
            k = which if kind == "mix" else f"{kind}_{which}"
            if last_unit and which == "w_in":
                (land,) = _rs_exchange_sibling([[g]], [big_axis[k]], name=f"rs_{k}{l}_sib")
                return [to_chips(k, l, g, land)]
            handle, token = _rs_sibling_start(g, big_axis[k], name=f"rs_{k}{l}_sibstart")
            sib_pending.append((k, l, handle))
            return [token]
        return emit

    g_norm = {k: [None] * n_layers for k in ("ffn1", "mix", "ffn2")}
    g_mix = [None] * n_layers
    for kind, l in reversed(units):
        w_i, w_o = gathered[(kind, l)]
        tag = f"l{l}{kind}"
        if kind == "mix":
            dh, dhb, g_norm[kind][l], g_mix[l] = _mix_block_bwd(dh, dhb, saved[(kind, l)], norms[kind], l, w_i, w_o, mp, tag,
                                                                 emitter(kind, l), [])
        else:
            dh, dhb, g_norm[kind][l] = _ffn_bwd(dh, dhb, saved[(kind, l)], norms[kind], l, w_i, w_o, tag, emitter(kind, l), [])
    drain([dh])
    dh0 = dh
    grad_x = dh0[n_meta:t_valid][None]

    mix_key = dict(conv_w="conv_w", conv_b="conv_b", lru_wa="wa", lru_ba="ba", lru_wx="wx", lru_bx="bx", lru_a_param="a_param",
                   pool_w="pool_w", pool_b="pool_b", pool_scale="pool_scale")
    small_local = {"meta_tokens": dh0[:n_meta], "final_norm": g_final.reshape(d)}
    for k in ("ffn1", "mix", "ffn2"):
        small_local[f"{k}_norm"] = jnp.stack([g.reshape(d) for g in g_norm[k]])
    for k, mk in mix_key.items():
        small_local[k] = jnp.stack([g_mix[l][mk] for l in range(n_layers)]).reshape(
            w[k].shape if k not in _SMALL_SHARD_AXIS else small_shape_full(w[k], _SMALL_SHARD_AXIS[k]))
    small_shapes = [small_local[k].shape for k in _SMALL]
    packed = _pack([small_local[k] for k in _SMALL])
    (land,) = _rs_exchange_sibling([[packed]], [0], name="rs_small_sib")
    part = _rs_add_sibling([packed], 0, land, my_c, name="rs_small_add")
    (land2,) = _rs_exchange_chips([part], name="rs_small_chips")

    out_g, out_d, out_m, out_v = {}, {}, {}, {}
    for k, _ in _BIG:
        out_g[k], out_d[k], out_m[k], out_v[k] = big_out[k]
    small_block = _sum_slots(land2, name="rs_sum_small")
    small_full = _all_gather([small_block], [0], name="gather_small")[0]
    small_g = dict(zip(_SMALL, _unpack(small_full, small_shapes)))
    for k, ax in _SMALL_SHARD_AXIS.items():
        small_g[k] = _my_shard(small_g[k], ax, dev)
    shapes_local = [w[k].shape for k in _SMALL]
    d_p, m_p, v_p = _adam_flat(_pack([w[k] for k in _SMALL]), _pack([small_g[k] for k in _SMALL]),
                               _pack([mom[k] for k in _SMALL]), _pack([vel[k] for k in _SMALL]), name="adam_small")
    for k, dd, mm, vv in zip(_SMALL, _unpack(d_p, shapes_local), _unpack(m_p, shapes_local), _unpack(v_p, shapes_local)):
        out_g[k], out_d[k], out_m[k], out_v[k] = small_g[k], dd, mm, vv

    return (loss, grad_x, *[out_g[k] for k in names], *[out_d[k] for k in names], *[out_m[k] for k in names],
            *[out_v[k] for k in names])


def small_shape_full(w_shard, axis):
    return w_shard.shape[:axis] + (N_DEV * w_shard.shape[axis],) + w_shard.shape[axis + 1:]
```

```python
import functools

import jax
import jax.numpy as jnp
from jax import lax
from jax.experimental import pallas as pl
from jax.experimental.pallas import tpu as pltpu

F32 = jnp.float32
BF16 = jnp.bfloat16
MESH = pl.DeviceIdType.MESH

N_DEV = 8
N_CHIP = 4
RMS_EPS = 1e-6
LRU_C = 8.0
CONV_WIDTH = 4
POOL_WINDOWS = (2, 4, 8, 16)
HIST = 16
ADAM_LR = 0.001
ADAM_B1 = 0.9
ADAM_B2 = 0.999
ADAM_EPS = 1e-08
ADAM_WD = 0.01
ADAM_STEP = 10
ROW_ALIGN = 128
V7X_VMEM_LIMIT = 56 * 1024 * 1024
BF16_ROWS = 16
K_TILE = 2816
SCAN_UNROLL = 8
EPILOGUE_ROWS = 528
LANES = 128


def _tile(n, cap, mult=BF16_ROWS):
    best = None
    d = mult
    while d <= min(n, cap):
        if n % d == 0:
            best = d
        d += mult
    if best is None:
        raise ValueError(f"no tile for {n} (cap {cap}, multiple of {mult})")
    return best


def _row_mult(dtype):
    return 8 * 4 // jnp.dtype(dtype).itemsize


def _params(sem=None):
    return pltpu.CompilerParams(dimension_semantics=sem, vmem_limit_bytes=V7X_VMEM_LIMIT)


def _dot(a, b, mode):
    dims = {"nn": ((1,), (0,)), "nt": ((1,), (1,)), "tn": ((0,), (0,))}[mode]
    return lax.dot_general(a, b, (dims, ((), ())), preferred_element_type=F32)


def _sigmoid(x):
    return 1.0 / (1.0 + jnp.exp(-x))


_GELU_C = 0.7978845608028654
_GELU_K = 0.044715


def _gelu(x):
    return 0.5 * x * (1.0 + jnp.tanh(_GELU_C * (x + _GELU_K * x * x * x)))


def _gelu_and_grad(x):
    x2 = x * x
    th = jnp.tanh(_GELU_C * x * (1.0 + _GELU_K * x2))
    half = 0.5 * (1.0 + th)
    return x * half, half + 0.5 * x * (1.0 - th * th) * _GELU_C * (1.0 + 3.0 * _GELU_K * x2)


def _neg_expm1(x):
    p = 1.0 / 5040.0
    for c in (1.0 / 720.0, 1.0 / 120.0, 1.0 / 24.0, 1.0 / 6.0, 0.5, 1.0):
        p = p * x + c
    return jnp.where(x > -0.25, -(x * p), 1.0 - jnp.exp(x))


def _softplus_neg(p):
    e = jnp.exp(-jnp.abs(p))
    u = 1.0 + e
    l1p = jnp.where(u == 1.0, e, jnp.log(u) * e / (u - 1.0 + (u == 1.0).astype(F32)))
    return jnp.maximum(-p, 0.0) + l1p


def _operand_spec(arr, br, bc, ridx, cidx, layer, split):
    if split:
        ncb = arr.shape[2] // bc
        return pl.BlockSpec((None, br, bc), lambda i, j, k: (cidx(i, j, k) // ncb, ridx(i, j, k), cidx(i, j, k) % ncb))
    if layer is not None:
        return pl.BlockSpec((None, br, bc), lambda i, j, k: (layer, ridx(i, j, k), cidx(i, j, k)))
    return pl.BlockSpec((br, bc), lambda i, j, k: (ridx(i, j, k), cidx(i, j, k)))


def _matmul(a, b, *, mode, m, n, kdim, tm, tn, tk, out_dtype, name, scale=None, residual=None,
            a_layer=None, b_layer=None, a_split=False, b_split=False, deps=()):
    nk = kdim // tk
    I = lambda i, j, k: i
    J = lambda i, j, k: j
    K = lambda i, j, k: k
    if mode == "nn":
        a_spec = _operand_spec(a, tm, tk, I, K, a_layer, a_split)
        b_spec = _operand_spec(b, tk, tn, K, J, b_layer, b_split)
    elif mode == "nt":
        a_spec = _operand_spec(a, tm, tk, I, K, a_layer, a_split)
        b_spec = _operand_spec(b, tn, tk, J, K, b_layer, b_split)
    else:
        a_spec = _operand_spec(a, tk, tm, K, I, a_layer, a_split)
        b_spec = _operand_spec(b, tk, tn, K, J, b_layer, b_split)
    in_specs = [a_spec, b_spec]
    operands = [a, b]
    if residual is not None:
        in_specs.append(pl.BlockSpec((tm, tn), lambda i, j, k: (i, j)))
        operands.append(residual)
    in_specs += [pl.BlockSpec(memory_space=pl.ANY)] * len(deps)
    operands += list(deps)
    n_in = len(operands)

    def finish(acc, res_ref, o_ref):
        if scale is not None:
            acc = acc * scale
        if res_ref is not None:
            acc = acc + res_ref[...]
        o_ref[...] = acc.astype(o_ref.dtype)

    def body(*refs):
        a_ref, b_ref = refs[0], refs[1]
        res_ref = refs[2] if residual is not None else None
        o_ref = refs[n_in]
        part = _dot(a_ref[...], b_ref[...], mode)
        if nk == 1:
            finish(part, res_ref, o_ref)
        else:
            acc_ref = refs[-1]
            k = pl.program_id(2)

            @pl.when(k == 0)
            def _():
                acc_ref[...] = part

            @pl.when(k > 0)
            def _():
                acc_ref[...] += part

            @pl.when(k == nk - 1)
            def _():
                finish(acc_ref[...], res_ref, o_ref)

    return pl.pallas_call(
        body, name=name,
        out_shape=jax.ShapeDtypeStruct((m, n), out_dtype),
        grid=(m // tm, n // tn, nk),
        in_specs=in_specs,
        out_specs=pl.BlockSpec((tm, tn), lambda i, j, k: (i, j)),
        scratch_shapes=[pltpu.VMEM((tm, tn), F32)] if nk > 1 else [],
        compiler_params=_params(("parallel", "parallel", "arbitrary")),
    )(*operands)


def _ffn_in_fwd(n_act, w_in, layer, *, name):
    tp, d = n_act.shape
    f = w_in.shape[2] // 2
    tm = _tile(tp, 2112)
    tn = _tile(f, 256, LANES)
    nj = f // tn

    ch = _tile(tm, EPILOGUE_ROWS)

    def body(n_ref, wg_ref, wu_ref, gu_ref, a_ref):
        for r in range(tm // ch):
            rows = pl.ds(r * ch, ch)
            x = n_ref[rows, :]
            g = _dot(x, wg_ref[...], "nn")
            u = _dot(x, wu_ref[...], "nn")
            gu_ref[0, rows, :] = g.astype(BF16)
            gu_ref[1, rows, :] = u.astype(BF16)
            a_ref[rows, :] = (g * _sigmoid(g) * u).astype(BF16)

    return pl.pallas_call(
        body, name=name,
        out_shape=(jax.ShapeDtypeStruct((2, tp, f), BF16), jax.ShapeDtypeStruct((tp, f), BF16)),
        grid=(tp // tm, nj),
        in_specs=[pl.BlockSpec((tm, d), lambda i, j: (i, 0)),
                  pl.BlockSpec((None, d, tn), lambda i, j: (layer, 0, j)),
                  pl.BlockSpec((None, d, tn), lambda i, j: (layer, 0, j + nj))],
        out_specs=(pl.BlockSpec((2, tm, tn), lambda i, j: (0, i, j)), pl.BlockSpec((tm, tn), lambda i, j: (i, j))),
        compiler_params=_params(("parallel", "arbitrary")),
    )(n_act, w_in, w_in)


def _ffn_bwd_act(dhb, w_out, layer, gu, *, name, deps=()):
    tp, d = dhb.shape
    f = w_out.shape[1]
    tm = _tile(tp, 2112)
    tn = _tile(f, 256, LANES)

    ch = _tile(tm, EPILOGUE_ROWS)

    def body(dh_ref, w_ref, gu_ref, *rest):
        dz_ref, a_ref = rest[len(deps):]
        for r in range(tm // ch):
            rows = pl.ds(r * ch, ch)
            da = 0.5 * _dot(dh_ref[rows, :], w_ref[...], "nt")
            g = gu_ref[0, rows, :].astype(F32)
            u = gu_ref[1, rows, :].astype(F32)
            s = _sigmoid(g)
            sg = g * s
            dz_ref[0, rows, :] = (da * u * (s * (1.0 + g * (1.0 - s)))).astype(BF16)
            dz_ref[1, rows, :] = (da * sg).astype(BF16)
            a_ref[rows, :] = (sg * u).astype(BF16)

    return pl.pallas_call(
        body, name=name,
        out_shape=(jax.ShapeDtypeStruct((2, tp, f), BF16), jax.ShapeDtypeStruct((tp, f), BF16)),
        grid=(tp // tm, f // tn),
        in_specs=[pl.BlockSpec((tm, d), lambda i, j: (i, 0)),
                  pl.BlockSpec((None, tn, d), lambda i, j: (layer, j, 0)),
                  pl.BlockSpec((2, tm, tn), lambda i, j: (0, i, j))] + [pl.BlockSpec(memory_space=pl.ANY)] * len(deps),
        out_specs=(pl.BlockSpec((2, tm, tn), lambda i, j: (0, i, j)), pl.BlockSpec((tm, tn), lambda i, j: (i, j))),
        compiler_params=_params(("parallel", "arbitrary")),
    )(dhb, w_out, gu, *deps)


def _rms_fwd(h, g, layer, *, name, deps=()):
    tp, d = h.shape
    tr = _tile(tp, 528)

    def body(h_ref, g_ref, *rest):
        n_ref = rest[len(deps)]
        x = h_ref[...]
        r = lax.rsqrt(jnp.mean(x * x, axis=-1, keepdims=True) + RMS_EPS)
        n_ref[...] = (x * r * g_ref[...]).astype(BF16)

    return pl.pallas_call(
        body, name=name, out_shape=jax.ShapeDtypeStruct((tp, d), BF16), grid=(tp // tr,),
        in_specs=[pl.BlockSpec((tr, d), lambda i: (i, 0)), pl.BlockSpec((None, 1, d), lambda i: (layer, 0, 0))]
        + [pl.BlockSpec(memory_space=pl.ANY)] * len(deps),
        out_specs=pl.BlockSpec((tr, d), lambda i: (i, 0)),
        compiler_params=_params(("parallel",)),
    )(h, g, *deps)


def _rms_bwd(h, g, layer, dn, dres, *, name, deps=()):
    tp, d = h.shape
    tr = _tile(tp, 528)

    def body(h_ref, g_ref, dn_ref, dres_ref, *rest):
        dh_ref, dhb_ref, dg_ref = rest[len(deps):]
        x = h_ref[...]
        r = lax.rsqrt(jnp.mean(x * x, axis=-1, keepdims=True) + RMS_EPS)
        xhat = x * r
        dn_v = dn_ref[...]
        dxhat = dn_v * g_ref[...]
        dh = dres_ref[...] + r * (dxhat - xhat * jnp.mean(dxhat * xhat, axis=-1, keepdims=True))
        dh_ref[...] = dh
        dhb_ref[...] = dh.astype(BF16)
        part = jnp.sum(dn_v * xhat, axis=0, keepdims=True)

        @pl.when(pl.program_id(0) == 0)
        def _():
            dg_ref[...] = part

        @pl.when(pl.program_id(0) > 0)
        def _():
            dg_ref[...] += part

    return pl.pallas_call(
        body, name=name,
        out_shape=(jax.ShapeDtypeStruct((tp, d), F32), jax.ShapeDtypeStruct((tp, d), BF16), jax.ShapeDtypeStruct((1, d), F32)),
        grid=(tp // tr,),
        in_specs=[pl.BlockSpec((tr, d), lambda i: (i, 0)), pl.BlockSpec((None, 1, d), lambda i: (layer, 0, 0)),
                  pl.BlockSpec((tr, d), lambda i: (i, 0)), pl.BlockSpec((tr, d), lambda i: (i, 0))]
        + [pl.BlockSpec(memory_space=pl.ANY)] * len(deps),
        out_specs=(pl.BlockSpec((tr, d), lambda i: (i, 0)), pl.BlockSpec((tr, d), lambda i: (i, 0)),
                   pl.BlockSpec((1, d), lambda i: (0, 0))),
        compiler_params=_params(("arbitrary",)),
    )(h, g, dn, dres, *deps)


def _final_loss(h, g, tgt, n_meta, t_valid, *, name):
    tp, d = h.shape
    tr = _tile(tp, 528)

    def body(h_ref, g_ref, t_ref, dh_ref, dhb_ref, dg_ref, loss_ref):
        i = pl.program_id(0)
        x = h_ref[...]
        r = lax.rsqrt(jnp.mean(x * x, axis=-1, keepdims=True) + RMS_EPS)
        xhat = x * r
        gv = g_ref[...]
        row = i * tr + lax.broadcasted_iota(jnp.int32, (tr, 1), 0)
        valid = jnp.logical_and(row >= n_meta, row < t_valid)
        err = jnp.where(valid, xhat * gv - t_ref[...], 0.0)
        dy = err * (1.0 / d)
        dxhat = dy * gv
        dh = r * (dxhat - xhat * jnp.mean(dxhat * xhat, axis=-1, keepdims=True))
        dh_ref[...] = dh
        dhb_ref[...] = dh.astype(BF16)
        dg_part = jnp.sum(dy * xhat, axis=0, keepdims=True)
        loss_part = jnp.sum(jnp.sum(err * err, axis=1, keepdims=True), axis=0, keepdims=True) * (0.5 / d)

        @pl.when(i == 0)
        def _():
            dg_ref[...] = dg_part
            loss_ref[...] = loss_part

        @pl.when(i > 0)
        def _():
            dg_ref[...] += dg_part
            loss_ref[...] += loss_part

    return pl.pallas_call(
        body, name=name,
        out_shape=(jax.ShapeDtypeStruct((tp, d), F32), jax.ShapeDtypeStruct((tp, d), BF16),
                   jax.ShapeDtypeStruct((1, d), F32), jax.ShapeDtypeStruct((1, 1), F32)),
        grid=(tp // tr,),
        in_specs=[pl.BlockSpec((tr, d), lambda i: (i, 0)), pl.BlockSpec((1, d), lambda i: (0, 0)),
                  pl.BlockSpec((tr, d), lambda i: (i, 0))],
        out_specs=(pl.BlockSpec((tr, d), lambda i: (i, 0)), pl.BlockSpec((tr, d), lambda i: (i, 0)),
                   pl.BlockSpec((1, d), lambda i: (0, 0)), pl.BlockSpec((1, 1), lambda i: (0, 0))),
        compiler_params=_params(("arbitrary",)),
    )(h, g, tgt)


def _mix_param_specs(mp, layer, imap):
    def vec(a):
        return pl.BlockSpec((None,) + a.shape[1:], lambda *g: (layer,) + (0,) * (a.ndim - 1))
    return [vec(mp[k]) for k in ("conv_w", "conv_b", "wa", "ba", "wx", "bx", "a_param", "pool_w", "pool_b", "pool_scale")]


def _mix_param_list(mp):
    return [mp[k] for k in ("conv_w", "conv_b", "wa", "ba", "wx", "bx", "a_param", "pool_w", "pool_b", "pool_scale")]


def _lru_gates(xc, wa_h, ba_h, wx_h, bx_h, sp_h):
    xb = xc.astype(BF16)
    ra = _sigmoid(_dot(xb, wa_h, "nn") + ba_h)
    ii = _sigmoid(_dot(xb, wx_h, "nn") + bx_h)
    la = -LRU_C * ra * sp_h
    return ra, ii, la, jnp.exp(la)


def _shifted(x, k):
    return x if k == 0 else pltpu.roll(x, k % x.shape[0], axis=0)


def _conv_taps(xbuf, sl, tc):
    x = xbuf[:, sl]
    return [_shifted(x, CONV_WIDTH - 1 - k)[HIST:HIST + tc] for k in range(CONV_WIDTH)]


def _conv_fwd(taps, cw_ref, cb_ref, sl):
    xc = cb_ref[:, sl]
    for k in range(CONV_WIDTH):
        xc = xc + cw_ref[k:k + 1, sl] * taps[k]
    return xc


def _window_sum(x, win, direction):
    s = x
    step = 1
    while step < win:
        s = s + _shifted(s, direction * step)
        step *= 2
    return s


def _pool_delta(pbuf, cols, win, t0, tc):
    x = pbuf[:, cols]
    u = x[HIST:HIST + tc]
    s = _window_sum(x, win, 1)[HIST:HIST + tc]
    t = t0 + lax.broadcasted_iota(jnp.int32, (tc, 1), 0)
    inv_cnt = 1.0 / jnp.minimum(t + 1, win).astype(F32)
    return s * inv_cnt - u, inv_cnt


def _mix_fwd(z, mp, layer, *, name):
    tp = z.shape[0]
    dl = z.shape[1] // 3
    n_heads, hd = mp["wa"].shape[1], mp["wa"].shape[2]
    n_groups, gd = mp["pool_w"].shape[1], mp["pool_w"].shape[2]
    tc = _tile(tp, 384)

    def body(z_ref, cw_ref, cb_ref, wa_ref, ba_ref, wx_ref, bx_ref, ap_ref, pw_ref, pb_ref, ps_ref,
             m_ref, hs_ref, xbuf, pbuf, a_s, b_s, hcar):
        i = pl.program_id(0)

        @pl.when(i == 0)
        def _():
            xbuf[pl.ds(0, HIST), :] = jnp.zeros((HIST, dl), F32)
            pbuf[pl.ds(0, HIST), :] = jnp.zeros((HIST, dl), F32)
            hcar[...] = jnp.zeros((1, dl), F32)

        @pl.when(i > 0)
        def _():
            xbuf[pl.ds(0, HIST), :] = xbuf[pl.ds(tc, HIST), :]
            pbuf[pl.ds(0, HIST), :] = pbuf[pl.ds(tc, HIST), :]

        xbuf[pl.ds(HIST, tc), :] = z_ref[:, pl.ds(0, dl)]
        pbuf[pl.ds(HIST, tc), :] = z_ref[:, pl.ds(2 * dl, dl)]
        sp = _softplus_neg(ap_ref[...])
        for h in range(n_heads):
            sl = pl.ds(h * hd, hd)
            xc = _conv_fwd(_conv_taps(xbuf, sl, tc), cw_ref, cb_ref, sl)
            _, ii, la, a = _lru_gates(xc, wa_ref[h], ba_ref[:, sl], wx_ref[h], bx_ref[:, sl], sp[:, h * hd:(h + 1) * hd])
            a_s[:, sl] = a
            b_s[:, sl] = jnp.sqrt(_neg_expm1(2.0 * la)) * ii * xc

        def step(t, hprev):
            hnew = a_s[pl.ds(t, 1), :] * hprev + b_s[pl.ds(t, 1), :]
            hs_ref[pl.ds(t, 1), :] = hnew
            return hnew

        hcar[...] = lax.fori_loop(0, tc, step, hcar[...], unroll=SCAN_UNROLL)
        for h in range(n_heads):
            sl = pl.ds(h * hd, hd)
            m_ref[:, sl] = (hs_ref[:, sl] * _gelu(z_ref[:, pl.ds(dl + h * hd, hd)])).astype(BF16)
        for g in range(n_groups):
            cols = pl.ds(g * gd, gd)
            dlt, _ = _pool_delta(pbuf, cols, POOL_WINDOWS[g], i * tc, tc)
            q = _dot(dlt.astype(BF16), pw_ref[g], "nn") + pb_ref[:, cols]
            m_ref[:, pl.ds(dl + g * gd, gd)] = (q * ps_ref[:, cols]).astype(BF16)

    return pl.pallas_call(
        body, name=name,
        out_shape=(jax.ShapeDtypeStruct((tp, 2 * dl), BF16), jax.ShapeDtypeStruct((tp, dl), F32)),
        grid=(tp // tc,),
        in_specs=[pl.BlockSpec((tc, 3 * dl), lambda i: (i, 0))] + _mix_param_specs(mp, layer, None),
        out_specs=(pl.BlockSpec((tc, 2 * dl), lambda i: (i, 0)), pl.BlockSpec((tc, dl), lambda i: (i, 0))),
        scratch_shapes=[pltpu.VMEM((HIST + tc, dl), F32), pltpu.VMEM((HIST + tc, dl), F32),
                        pltpu.VMEM((tc, dl), F32), pltpu.VMEM((tc, dl), F32), pltpu.VMEM((1, dl), F32)],
        compiler_params=_params(("arbitrary",)),
    )(z, *_mix_param_list(mp))


_MIX_GRADS = ("conv_w", "conv_b", "wa", "ba", "wx", "bx", "a_param", "pool_w", "pool_b", "pool_scale")


def _mix_bwd(z, hs, dm, mp, layer, *, name):
    tp = z.shape[0]
    dl = z.shape[1] // 3
    n_heads, hd = mp["wa"].shape[1], mp["wa"].shape[2]
    n_groups, gd = mp["pool_w"].shape[1], mp["pool_w"].shape[2]
    tc = _tile(tp, 384)
    nc = tp // tc
    per = tc // HIST

    def body(z_ref, zp_ref, hs_ref, hsp_ref, dm_ref, cw_ref, cb_ref, wa_ref, ba_ref, wx_ref, bx_ref, ap_ref, pw_ref, pb_ref,
             ps_ref, dz_ref, dcw_ref, dcb_ref, dwa_ref, dba_ref, dwx_ref, dbx_ref, dap_ref, dpw_ref, dpb_ref, dps_ref,
             xbuf, pbuf, hbuf, dxbuf, ddbuf, a_s, lam_s, ra_s, ii_s, xc_s, ccar):
        i = pl.program_id(0)
        ci = nc - 1 - i

        @pl.when(i == 0)
        def _():
            dxbuf[pl.ds(tc, HIST), :] = jnp.zeros((HIST, dl), F32)
            ddbuf[pl.ds(tc, HIST), :] = jnp.zeros((HIST, dl), F32)
            ccar[...] = jnp.zeros((1, dl), F32)
            for ref in (dcw_ref, dcb_ref, dwa_ref, dba_ref, dwx_ref, dbx_ref, dap_ref, dpw_ref, dpb_ref, dps_ref):
                ref[...] = jnp.zeros(ref.shape, F32)

        @pl.when(ci == 0)
        def _():
            xbuf[pl.ds(0, HIST), :] = jnp.zeros((HIST, dl), F32)
            pbuf[pl.ds(0, HIST), :] = jnp.zeros((HIST, dl), F32)
            hbuf[pl.ds(0, HIST), :] = jnp.zeros((HIST, dl), F32)

        @pl.when(ci > 0)
        def _():
            xbuf[pl.ds(0, HIST), :] = zp_ref[:, pl.ds(0, dl)]
            pbuf[pl.ds(0, HIST), :] = zp_ref[:, pl.ds(2 * dl, dl)]
            hbuf[pl.ds(0, HIST), :] = hsp_ref[...]

        xbuf[pl.ds(HIST, tc), :] = z_ref[:, pl.ds(0, dl)]
        pbuf[pl.ds(HIST, tc), :] = z_ref[:, pl.ds(2 * dl, dl)]
        hbuf[pl.ds(HIST, tc), :] = hs_ref[...]
        sp = _softplus_neg(ap_ref[...])

        for h in range(n_heads):
            sl = pl.ds(h * hd, hd)
            xc = _conv_fwd(_conv_taps(xbuf, sl, tc), cw_ref, cb_ref, sl)
            ra, ii, _, a = _lru_gates(xc, wa_ref[h], ba_ref[:, sl], wx_ref[h], bx_ref[:, sl], sp[:, h * hd:(h + 1) * hd])
            a_s[:, sl] = a
            ra_s[:, sl] = ra
            ii_s[:, sl] = ii
            xc_s[:, sl] = xc
            gel, gel_grad = _gelu_and_grad(z_ref[:, pl.ds(dl + h * hd, hd)])
            dya = dm_ref[:, sl]
            lam_s[:, sl] = dya * gel
            dz_ref[:, pl.ds(dl + h * hd, hd)] = (dya * hs_ref[:, sl] * gel_grad).astype(BF16)

        def step(r, carry):
            t = tc - 1 - r
            lam = lam_s[pl.ds(t, 1), :] + carry
            lam_s[pl.ds(t, 1), :] = lam
            return a_s[pl.ds(t, 1), :] * lam

        ccar[...] = lax.fori_loop(0, tc, step, ccar[...], unroll=SCAN_UNROLL)

        for h in range(n_heads):
            sl = pl.ds(h * hd, hd)
            sp_h = sp[:, h * hd:(h + 1) * hd]
            lam = lam_s[:, sl]
            a = a_s[:, sl]
            ra = ra_s[:, sl]
            ii = ii_s[:, sl]
            xc = xc_s[:, sl]
            e = _neg_expm1(-2.0 * LRU_C * ra * sp_h)
            inv_mult = lax.rsqrt(e)
            mult = e * inv_mult
            hprev = _shifted(hbuf[:, sl], 1)[HIST:HIST + tc]
            lam_i = lam * ii
            lam_m = lam * mult
            dla = lam * hprev * a - lam_i * xc * (a * a) * inv_mult
            dla_r = dla * (-LRU_C) * ra
            dap_ref[:, sl] += jnp.sum(dla_r, axis=0, keepdims=True)
            dpa = dla_r * sp_h * (1.0 - ra)
            dpx = lam_m * xc * ii * (1.0 - ii)
            dba_ref[:, sl] += jnp.sum(dpa, axis=0, keepdims=True)
            dbx_ref[:, sl] += jnp.sum(dpx, axis=0, keepdims=True)
            xb = xc.astype(BF16)
            dpa_b = dpa.astype(BF16)
            dpx_b = dpx.astype(BF16)
            dwa_ref[h] += _dot(xb, dpa_b, "tn")
            dwx_ref[h] += _dot(xb, dpx_b, "tn")
            dxc = lam_m * ii + _dot(dpa_b, wa_ref[h], "nt") + _dot(dpx_b, wx_ref[h], "nt")
            dxbuf[pl.ds(0, tc), sl] = dxc
            dcb_ref[:, sl] += jnp.sum(dxc, axis=0, keepdims=True)
            taps = _conv_taps(xbuf, sl, tc)
            dx_all = dxbuf[:, sl]
            dzx = jnp.zeros((tc, hd), F32)
            for k in range(CONV_WIDTH):
                dcw_ref[k:k + 1, sl] += jnp.sum(dxc * taps[k], axis=0, keepdims=True)
                dzx = dzx + cw_ref[k:k + 1, sl] * _shifted(dx_all, k - (CONV_WIDTH - 1))[0:tc]
            dz_ref[:, sl] = dzx.astype(BF16)
            dxbuf[pl.ds(tc, HIST), sl] = dxbuf[pl.ds(0, HIST), sl]

        for g in range(n_groups):
            cols = pl.ds(g * gd, gd)
            win = POOL_WINDOWS[g]
            dlt, inv_cnt = _pool_delta(pbuf, cols, win, ci * tc, tc)
            db = dlt.astype(BF16)
            q = _dot(db, pw_ref[g], "nn") + pb_ref[:, cols]
            dyb = dm_ref[:, pl.ds(dl + g * gd, gd)]
            dps_ref[:, cols] += jnp.sum(dyb * q, axis=0, keepdims=True)
            dq = dyb * ps_ref[:, cols]
            dpb_ref[:, cols] += jnp.sum(dq, axis=0, keepdims=True)
            dqb = dq.astype(BF16)
            dpw_ref[g] += _dot(db, dqb, "tn")
            dd = _dot(dqb, pw_ref[g], "nt")
            ddbuf[pl.ds(0, tc), cols] = dd * inv_cnt
            dzp = _window_sum(ddbuf[:, cols], win, -1)[0:tc] - dd
            dz_ref[:, pl.ds(2 * dl + g * gd, gd)] = dzp.astype(BF16)
            ddbuf[pl.ds(tc, HIST), cols] = ddbuf[pl.ds(0, HIST), cols]

        @pl.when(i == nc - 1)
        def _():
            dap_ref[...] = dap_ref[...] * (-_sigmoid(-ap_ref[...]))

    rev = lambda i: (nc - 1 - i, 0)
    prev = lambda i: (jnp.maximum((nc - 1 - i) * per - 1, 0), 0)
    const = lambda a: pl.BlockSpec(a.shape[1:], lambda i: (0,) * (a.ndim - 1))
    plist = _mix_param_list(mp)
    grad_shapes = [jax.ShapeDtypeStruct(a.shape[1:], F32) for a in plist]
    buf = lambda rows: pltpu.VMEM((rows, dl), F32)
    outs = pl.pallas_call(
        body, name=name,
        out_shape=[jax.ShapeDtypeStruct((tp, 3 * dl), BF16)] + grad_shapes,
        grid=(nc,),
        in_specs=[pl.BlockSpec((tc, 3 * dl), rev), pl.BlockSpec((HIST, 3 * dl), prev),
                  pl.BlockSpec((tc, dl), rev), pl.BlockSpec((HIST, dl), prev),
                  pl.BlockSpec((tc, 2 * dl), rev)] + _mix_param_specs(mp, layer, None),
        out_specs=[pl.BlockSpec((tc, 3 * dl), rev)] + [const(a) for a in plist],
        scratch_shapes=[buf(HIST + tc), buf(HIST + tc), buf(HIST + tc), buf(tc + HIST), buf(tc + HIST),
                        buf(tc), buf(tc), buf(tc), buf(tc), buf(tc), buf(1)],
        compiler_params=_params(("arbitrary",)),
    )(z, z, hs, hs, dm, *plist)
    return outs[0], dict(zip(_MIX_GRADS, outs[1:]))


def _ffn_fwd_in(h, norm, layer, w_in, tag, deps):
    n_act = _rms_fwd(h, norm, layer, name=f"{tag}_rms", deps=deps)
    gu, act = _ffn_in_fwd(n_act, w_in, 0, name=f"{tag}_in")
    return n_act, gu, act


def _ffn_fwd_out(h, act, w_out, tag):
    tp, d = h.shape
    f = w_out.shape[1]
    return _matmul(act, w_out, mode="nn", m=tp, n=d, kdim=f, tm=_tile(tp, 1056), tn=_tile(d, 1024, LANES),
                   tk=_tile(f, K_TILE, LANES), out_dtype=F32, scale=0.5, residual=h, b_layer=0, name=f"{tag}_out")


def _ffn_bwd(dh, dhb, saved, norm, layer, w_in, w_out, tag, emit, deps):
    h, n_act, gu = saved
    tp, d = h.shape
    f = w_out.shape[1]
    dz, act = _ffn_bwd_act(dhb, w_out, 0, gu, name=f"{tag}_bact", deps=deps)
    tok = emit("w_out", _matmul(act, dhb, mode="tn", m=f, n=d, kdim=tp, tm=_tile(f, 512, LANES), tn=_tile(d, 1024, LANES), tk=tp,
                                out_dtype=BF16, scale=0.5, name=f"{tag}_dwout"))
    tok = emit("w_in", _matmul(n_act, dz, mode="tn", m=d, n=2 * f, kdim=tp, tm=_tile(d, 512, LANES), tn=_tile(f, 512, LANES),
                               tk=tp, out_dtype=BF16, b_split=True, name=f"{tag}_dwin", deps=tok))
    dn = _matmul(dz, w_in, mode="nt", m=tp, n=d, kdim=2 * f, tm=_tile(tp, 1056), tn=_tile(d, 1024, LANES), tk=_tile(f, K_TILE, LANES),
                 out_dtype=F32, a_split=True, b_layer=0, name=f"{tag}_dn", deps=tok)
    return _rms_bwd(h, norm, layer, dn, dh, name=f"{tag}_brms", deps=emit("flush", dn))


def _mix_block_fwd(h, norm, layer, w_in, w_out, mp, tag, deps):
    tp, d = h.shape
    d_in = w_in.shape[2]
    n_act = _rms_fwd(h, norm, layer, name=f"{tag}_rms", deps=deps)
    z = _matmul(n_act, w_in, mode="nn", m=tp, n=d_in, kdim=d, tm=_tile(tp, 2112), tn=_tile(d_in, 512, LANES), tk=d,
                out_dtype=F32, b_layer=0, name=f"{tag}_in")
    m_act, hs = _mix_fwd(z, mp, layer, name=f"{tag}_mix")
    h_out = _matmul(m_act, w_out, mode="nn", m=tp, n=d, kdim=d, tm=_tile(tp, 2112), tn=_tile(d, 512, LANES), tk=d,
                    out_dtype=F32, residual=h, b_layer=0, name=f"{tag}_out")
    return h_out, (h, n_act, z, hs, m_act)


def _mix_block_bwd(dh, dhb, saved, norm, layer, w_in, w_out, mp, tag, emit, deps):
    h, n_act, z, hs, m_act = saved
    tp, d = h.shape
    d_in = w_in.shape[2]
    dm = _matmul(dhb, w_out, mode="nt", m=tp, n=d, kdim=d, tm=_tile(tp, 2112), tn=_tile(d, 512, LANES), tk=d,
                 out_dtype=F32, b_layer=0, name=f"{tag}_dm", deps=deps)
    tok = emit("w_out", _matmul(m_act, dhb, mode="tn", m=d, n=d, kdim=tp, tm=_tile(d, 512, LANES), tn=_tile(d, 1024, LANES), tk=tp,
                                out_dtype=BF16, name=f"{tag}_dwout"))
    dz, g_mix = _mix_bwd(z, hs, dm, mp, layer, name=f"{tag}_bmix")
    tok = emit("w_in", _matmul(n_act, dz, mode="tn", m=d, n=d_in, kdim=tp, tm=_tile(d, 512, LANES), tn=_tile(d_in, 512, LANES),
                               tk=tp, out_dtype=BF16, name=f"{tag}_dwin", deps=tok))
    dn = _matmul(dz, w_in, mode="nt", m=tp, n=d, kdim=d_in, tm=_tile(tp, 1056), tn=_tile(d, 1024, LANES), tk=d_in,
                 out_dtype=F32, b_layer=0, name=f"{tag}_dn", deps=tok)
    dh_in, dhb_in, g_norm = _rms_bwd(h, norm, layer, dn, dh, name=f"{tag}_brms", deps=emit("flush", dn))
    return dh_in, dhb_in, g_norm, g_mix


def _mesh_pos():
    x, y, c = lax.axis_index("x"), lax.axis_index("y"), lax.axis_index("c")
    chips = [(1 - x, y), (x, 1 - y), (1 - x, 1 - y)]
    return x, y, c, chips


def _block(ref, axis, j, size):
    idx = [slice(None)] * len(ref.shape)
    idx[axis] = pl.ds(j * size, size)
    return ref.at[tuple(idx)]


def _all_gather(shards, axes, *, name):
    n = len(shards)
    sizes = [s.shape[ax] for s, ax in zip(shards, axes)]
    full_shapes = [s.shape[:ax] + (N_DEV * s.shape[ax],) + s.shape[ax + 1:] for s, ax in zip(shards, axes)]

    def body(*refs):
        srcs, fulls = refs[:n], refs[n:2 * n]
        send_sems, recv_sems, local_sems = refs[2 * n:]
        x, y, c, chips = _mesh_pos()
        me, sib = (x, y, c), (x, y, 1 - c)

        def blk(a, dev):
            return _block(fulls[a], axes[a], 4 * dev[0] + 2 * dev[1] + dev[2], sizes[a])

        def copy(a, k, block_dev, to, src=None):
            return pltpu.make_async_remote_copy(
                src_ref=blk(a, block_dev) if src is None else src, dst_ref=blk(a, block_dev),
                send_sem=send_sems.at[a, k], recv_sem=recv_sems.at[a, k], device_id=to, device_id_type=MESH)

        mine = [pltpu.make_async_copy(srcs[a], blk(a, me), local_sems.at[a]) for a in range(n)]
        for cp in mine:
            cp.start()
        first = []
        for a in range(n):
            first.append(copy(a, 0, me, sib, src=srcs[a]))
            first += [copy(a, 1 + j, me, (*chip, c), src=srcs[a]) for j, chip in enumerate(chips)]
        for cp in first:
            cp.start()
        passed = []
        for j, chip in enumerate(chips):
            for a in range(n):
                copy(a, 1 + j, (*chip, c), me).wait_recv()
                fwd = copy(a, 4 + j, (*chip, c), sib)
                fwd.start()
                passed.append(fwd)
        for a in range(n):
            copy(a, 0, sib, me).wait_recv()
            for j, chip in enumerate(chips):
                copy(a, 4 + j, (*chip, 1 - c), me).wait_recv()
        for cp in first + passed:
            cp.wait_send()
        for cp in mine:
            cp.wait()

    any_spec = pl.BlockSpec(memory_space=pl.ANY)
    return pl.pallas_call(
        body, name=name,
        out_shape=[jax.ShapeDtypeStruct(fs, s.dtype) for fs, s in zip(full_shapes, shards)],
        in_specs=[any_spec] * n, out_specs=[any_spec] * n,
        scratch_shapes=[pltpu.SemaphoreType.DMA((n, 7)), pltpu.SemaphoreType.DMA((n, 7)), pltpu.SemaphoreType.DMA((n,))],
    )(*shards)


_HBM = pl.BlockSpec(memory_space=pltpu.HBM)
_SEM = pl.BlockSpec(memory_space=pltpu.SEMAPHORE)
_ANY = pl.BlockSpec(memory_space=pl.ANY)
_EFFECT = pltpu.SideEffectType.DATAFLOW_SIDE_EFFECTING
_N_OUT = 4


def _in_hbm(a):
    return pltpu.with_memory_space_constraint(a, pltpu.HBM)


def _gather_start(shards, axes, deps, *, name):
    n = len(shards)
    sizes = [s.shape[ax] for s, ax in zip(shards, axes)]
    full_shapes = [s.shape[:ax] + (N_DEV * s.shape[ax],) + s.shape[ax + 1:] for s, ax in zip(shards, axes)]

    def body(*refs):
        srcs, lands = refs[:n], refs[n:2 * n]
        send_sems, recv_sems = refs[2 * n + len(deps)], refs[2 * n + len(deps) + 1]
        token = refs[-1]
        x, y, c, chips = _mesh_pos()
        targets = [(x, y, 1 - c)] + [(*chip, c) for chip in chips]
        for a in range(n):
            dst = _block(lands[a], axes[a], 4 * x + 2 * y + c, sizes[a])
            for k, to in enumerate(targets):
                pltpu.make_async_remote_copy(src_ref=srcs[a], dst_ref=dst, send_sem=send_sems.at[_N_OUT * a + k],
                                             recv_sem=recv_sems.at[_N_OUT * a + k],
                                             device_id=to, device_id_type=MESH).start()
        token[...] = jnp.zeros(token.shape, F32)

    lands0 = [lax.empty(fs, s.dtype) for fs, s in zip(full_shapes, shards)]
    outs = pl.pallas_call(
        body, name=name,
        out_shape=(pltpu.SemaphoreType.DMA((n * _N_OUT,)), pltpu.SemaphoreType.DMA((n * _N_OUT,)),
                   *[pltpu.HBM(s.shape, s.dtype) for s in shards], *[pltpu.HBM(fs, s.dtype) for fs, s in zip(full_shapes, shards)],
                   jax.ShapeDtypeStruct((8, LANES), F32)),
        in_specs=[_HBM] * (2 * n) + [_ANY] * len(deps),
        out_specs=(_SEM, _SEM, *[_HBM] * (2 * n), pl.BlockSpec(memory_space=pltpu.VMEM)),
        input_output_aliases={a: 2 + a for a in range(2 * n)},
        compiler_params=pltpu.CompilerParams(has_side_effects=_EFFECT),
    )(*[_in_hbm(s) for s in shards], *[_in_hbm(z) for z in lands0], *deps)
    handle = dict(send_sems=outs[0], recv_sems=outs[1], srcs=list(outs[2:2 + n]), lands=list(outs[2 + n:2 + 2 * n]), axes=list(axes))
    return handle, outs[-1]


def _gather_wait(handle, after, *, name):
    srcs, lands, axes = handle["srcs"], handle["lands"], handle["axes"]
    n = len(srcs)
    sizes = [s.shape[ax] for s, ax in zip(srcs, axes)]

    def body(*refs):
        src_refs, land_refs = refs[:n], refs[n:2 * n]
        send_sems, recv_sems = refs[2 * n], refs[2 * n + 1]
        x, y, c, chips = _mesh_pos()
        peers = [(x, y, 1 - c)] + [(*chip, c) for chip in chips]
        for a in range(n):
            for k, dev in enumerate(peers):
                cp = pltpu.make_async_remote_copy(
                    src_ref=src_refs[a], dst_ref=_block(land_refs[a], axes[a], 4 * dev[0] + 2 * dev[1] + dev[2], sizes[a]),
                    send_sem=send_sems.at[_N_OUT * a + k], recv_sem=recv_sems.at[_N_OUT * a + k], device_id=dev,
                    device_id_type=MESH)
                cp.wait_send()
                cp.wait_recv()

    outs = pl.pallas_call(
        body, name=name,
        out_shape=(*[pltpu.HBM(s.shape, s.dtype) for s in srcs], *[pltpu.HBM(z.shape, z.dtype) for z in lands]),
        in_specs=[_HBM] * (2 * n) + [_SEM, _SEM] + [_ANY] * len(after),
        out_specs=tuple([_HBM] * (2 * n)),
        input_output_aliases={a: a for a in range(2 * n)},
        compiler_params=pltpu.CompilerParams(has_side_effects=_EFFECT),
    )(*srcs, *lands, handle["send_sems"], handle["recv_sems"], *after)
    return list(outs[:n]), list(outs[n:])


def _place_own(shard, full, axis, dev, *, name):
    nd = shard.ndim
    rows, cols = shard.shape[-2:]
    tr = _tile(rows, 512, _row_mult(shard.dtype)) if rows % _row_mult(shard.dtype) == 0 else rows
    nrb = rows // tr
    block = shard.shape[:-2] + (tr, cols)

    def in_map(r, dev_ref):
        return (0,) * (nd - 2) + (r, 0)

    def out_map(r, dev_ref):
        idx = [0] * nd
        idx[nd - 2] = r
        idx[axis] = dev_ref[0] * (nrb if axis == nd - 2 else 1) + idx[axis]
        return tuple(idx)

    def body(dev_ref, src_ref, full_ref, out_ref):
        out_ref[...] = src_ref[...]

    grid_spec = pltpu.PrefetchScalarGridSpec(
        num_scalar_prefetch=1, grid=(nrb,),
        in_specs=[pl.BlockSpec(block, in_map), _ANY], out_specs=pl.BlockSpec(block, out_map))
    return pl.pallas_call(body, name=name, grid_spec=grid_spec, out_shape=jax.ShapeDtypeStruct(full.shape, full.dtype),
                          input_output_aliases={2: 0}, compiler_params=_params(("arbitrary",)))(dev, shard, full)


def _gather_forward(fulls, axes, *, name):
    n = len(fulls)
    sizes = [z.shape[ax] // N_DEV for z, ax in zip(fulls, axes)]

    def body(*refs):
        outs = refs[n:2 * n]
        send_sems, recv_sems = refs[2 * n:]
        x, y, c, chips = _mesh_pos()
        sib = (x, y, 1 - c)

        def blk(a, dev):
            return _block(outs[a], axes[a], 4 * dev[0] + 2 * dev[1] + dev[2], sizes[a])

        passed = [pltpu.make_async_remote_copy(src_ref=blk(a, (*chip, c)), dst_ref=blk(a, (*chip, c)), send_sem=send_sems.at[a, j],
                                               recv_sem=recv_sems.at[a, j], device_id=sib, device_id_type=MESH)
                  for a in range(n) for j, chip in enumerate(chips)]
        for cp in passed:
            cp.start()
        for a in range(n):
            for j, chip in enumerate(chips):
                pltpu.make_async_remote_copy(src_ref=blk(a, (*chip, c)), dst_ref=blk(a, (*chip, 1 - c)), send_sem=send_sems.at[a, j],
                                             recv_sem=recv_sems.at[a, j], device_id=sib, device_id_type=MESH).wait_recv()
        for cp in passed:
            cp.wait_send()

    return pl.pallas_call(
        body, name=name,
        out_shape=[jax.ShapeDtypeStruct(z.shape, z.dtype) for z in fulls],
        in_specs=[_ANY] * n, out_specs=[_ANY] * n,
        input_output_aliases={a: a for a in range(n)},
        scratch_shapes=[pltpu.SemaphoreType.DMA((n, 3)), pltpu.SemaphoreType.DMA((n, 3))],
    )(*fulls)


def _rs_exchange_sibling(tensors, axes, *, name):
    n = len(tensors)
    n_layers = [len(t) for t in tensors]
    sizes = [t[0].shape[ax] // N_DEV for t, ax in zip(tensors, axes)]
    blk_shapes = [t[0].shape[:ax] + (sz,) + t[0].shape[ax + 1:] for t, ax, sz in zip(tensors, axes, sizes)]
    flat = [g for t in tensors for g in t]
    offs = [sum(n_layers[:a]) for a in range(n)]

    def body(*refs):
        srcs = refs[:len(flat)]
        lands = refs[len(flat):len(flat) + n]
        send_sems, recv_sems = refs[len(flat) + n:]
        x, y, c, _ = _mesh_pos()
        sib = (x, y, 1 - c)
        for a in range(n):
            for l in range(n_layers[a]):
                for i in range(N_CHIP):
                    pltpu.make_async_remote_copy(
                        src_ref=_block(srcs[offs[a] + l], axes[a], 2 * i + (1 - c), sizes[a]), dst_ref=lands[a].at[i, l],
                        send_sem=send_sems.at[a], recv_sem=recv_sems.at[a], device_id=sib, device_id_type=MESH).start()
        for a in range(n):
            pltpu.make_async_remote_copy(src_ref=lands[a], dst_ref=lands[a], send_sem=send_sems.at[a], recv_sem=recv_sems.at[a],
                                         device_id=sib, device_id_type=MESH).wait()

    any_spec = pl.BlockSpec(memory_space=pl.ANY)
    return pl.pallas_call(
        body, name=name,
        out_shape=[jax.ShapeDtypeStruct((N_CHIP, nl) + bs, t[0].dtype) for nl, bs, t in zip(n_layers, blk_shapes, tensors)],
        in_specs=[any_spec] * len(flat), out_specs=[any_spec] * n,
        scratch_shapes=[pltpu.SemaphoreType.DMA((n,)), pltpu.SemaphoreType.DMA((n,))],
    )(*flat)


def _rs_add_sibling(layers, axis, land, my_c, *, name):
    n_layers = len(layers)
    shape = layers[0].shape
    size = shape[axis] // N_DEV
    blk_shape = shape[:axis] + (size,) + shape[axis + 1:]
    nd = len(shape)
    rows = blk_shape[-2]
    tr = _tile(rows, 512, _row_mult(layers[0].dtype))
    inner = (tr, blk_shape[-1])
    lead = blk_shape[:-2]
    if lead:
        raise ValueError("blocked gradients are 2-D")
    nrb = rows // tr

    def src_map(l):
        def imap(i, r, c_ref):
            j = 2 * i + c_ref[0]
            return (j * nrb + r, 0) if axis == 0 else (r, j)
        return imap

    def body(c_ref, *refs):
        srcs = refs[:n_layers]
        land_ref = refs[n_layers]
        out_ref = refs[n_layers + 1]
        for l in range(n_layers):
            out_ref[l] = (srcs[l][...].astype(F32) + land_ref[l].astype(F32)).astype(out_ref.dtype)

    grid_spec = pltpu.PrefetchScalarGridSpec(
        num_scalar_prefetch=1, grid=(N_CHIP, nrb),
        in_specs=[pl.BlockSpec(inner, src_map(l)) for l in range(n_layers)]
        + [pl.BlockSpec((None, n_layers) + inner, lambda i, r, c_ref: (i, 0, r, 0))],
        out_specs=pl.BlockSpec((None, n_layers) + inner, lambda i, r, c_ref: (i, 0, r, 0)),
    )
    return pl.pallas_call(
        body, name=name, grid_spec=grid_spec,
        out_shape=jax.ShapeDtypeStruct((N_CHIP, n_layers) + blk_shape, layers[0].dtype),
        compiler_params=_params(("arbitrary", "arbitrary")),
    )(my_c, *layers, land)


def _rs_exchange_chips(parts, *, name):
    n = len(parts)

    def body(*refs):
        srcs, lands = refs[:n], refs[n:2 * n]
        send_sems, recv_sems, local_sems = refs[2 * n:]
        x, y, c, chips = _mesh_pos()
        mine = 2 * x + y
        local = [pltpu.make_async_copy(srcs[a].at[mine], lands[a].at[mine], local_sems.at[a]) for a in range(n)]
        for cp in local:
            cp.start()

        def copy(a, j, chip):
            return pltpu.make_async_remote_copy(
                src_ref=srcs[a].at[2 * chip[0] + chip[1]], dst_ref=lands[a].at[mine],
                send_sem=send_sems.at[a, j], recv_sem=recv_sems.at[a, j], device_id=(*chip, c), device_id_type=MESH)

        sends = [copy(a, j, chip) for a in range(n) for j, chip in enumerate(chips)]
        for cp in sends:
            cp.start()
        for a in range(n):
            for j, chip in enumerate(chips):
                pltpu.make_async_remote_copy(
                    src_ref=srcs[a].at[mine], dst_ref=lands[a].at[2 * chip[0] + chip[1]],
                    send_sem=send_sems.at[a, j], recv_sem=recv_sems.at[a, j], device_id=(*chip, c), device_id_type=MESH).wait_recv()
        for cp in sends:
            cp.wait_send()
        for cp in local:
            cp.wait()

    any_spec = pl.BlockSpec(memory_space=pl.ANY)
    return pl.pallas_call(
        body, name=name,
        out_shape=[jax.ShapeDtypeStruct(p.shape, p.dtype) for p in parts],
        in_specs=[any_spec] * n, out_specs=[any_spec] * n,
        scratch_shapes=[pltpu.SemaphoreType.DMA((n, 3)), pltpu.SemaphoreType.DMA((n, 3)), pltpu.SemaphoreType.DMA((n,))],
    )(*parts)


def _rs_sibling_start(g, axis, *, name):
    size = g.shape[axis] // N_DEV
    land = lax.empty((N_CHIP, 1) + g.shape[:axis] + (size,) + g.shape[axis + 1:], g.dtype)

    def body(src, dst, send_sems, recv_sems, src_thru, dst_thru, token):
        x, y, c, _ = _mesh_pos()
        for i in range(N_CHIP):
            pltpu.make_async_remote_copy(src_ref=_block(src, axis, 2 * i + (1 - c), size), dst_ref=dst.at[i, 0], send_sem=send_sems.at[i],
                                         recv_sem=recv_sems.at[i], device_id=(x, y, 1 - c), device_id_type=MESH).start()
        token[...] = jnp.zeros(token.shape, F32)

    outs = pl.pallas_call(
        body, name=name,
        out_shape=(pltpu.SemaphoreType.DMA((N_CHIP,)), pltpu.SemaphoreType.DMA((N_CHIP,)), pltpu.HBM(g.shape, g.dtype),
                   pltpu.HBM(land.shape, land.dtype), jax.ShapeDtypeStruct((8, LANES), F32)),
        in_specs=[_HBM, _HBM], out_specs=(_SEM, _SEM, _HBM, _HBM, pl.BlockSpec(memory_space=pltpu.VMEM)),
        input_output_aliases={0: 2, 1: 3},
        compiler_params=pltpu.CompilerParams(has_side_effects=_EFFECT),
    )(_in_hbm(g), _in_hbm(land))
    return dict(send_sems=outs[0], recv_sems=outs[1], src=outs[2], land=outs[3], axis=axis), outs[4]


def _rs_sibling_wait(handle, after, *, name):
    src, land, axis = handle["src"], handle["land"], handle["axis"]
    size = src.shape[axis] // N_DEV

    def body(src_ref, dst_ref, send_sems, recv_sems, *rest):
        x, y, c, _ = _mesh_pos()
        for i in range(N_CHIP):
            cp = pltpu.make_async_remote_copy(src_ref=_block(src_ref, axis, 2 * i + (1 - c), size), dst_ref=dst_ref.at[i, 0],
                                              send_sem=send_sems.at[i], recv_sem=recv_sems.at[i], device_id=(x, y, 1 - c),
                                              device_id_type=MESH)
            cp.wait_send()
            cp.wait_recv()

    outs = pl.pallas_call(
        body, name=name,
        out_shape=(pltpu.HBM(src.shape, src.dtype), pltpu.HBM(land.shape, land.dtype)),
        in_specs=[_HBM, _HBM, _SEM, _SEM] + [_ANY] * len(after), out_specs=(_HBM, _HBM),
        input_output_aliases={0: 0, 1: 1},
        compiler_params=pltpu.CompilerParams(has_side_effects=_EFFECT),
    )(src, land, handle["send_sems"], handle["recv_sems"], *after)
    return outs[0], outs[1]


def _rs_chips_start(part, *, name):
    land = lax.empty(part.shape, part.dtype)

    def body(src, dst, send_sems, recv_sems, src_thru, dst_thru, token):
        x, y, c, chips = _mesh_pos()
        for j, chip in enumerate(chips):
            pltpu.make_async_remote_copy(src_ref=src.at[2 * chip[0] + chip[1]], dst_ref=dst.at[2 * x + y], send_sem=send_sems.at[j],
                                         recv_sem=recv_sems.at[j], device_id=(*chip, c), device_id_type=MESH).start()
        token[...] = jnp.zeros(token.shape, F32)

    outs = pl.pallas_call(
        body, name=name,
        out_shape=(pltpu.SemaphoreType.DMA((3,)), pltpu.SemaphoreType.DMA((3,)), pltpu.HBM(part.shape, part.dtype),
                   pltpu.HBM(land.shape, land.dtype), jax.ShapeDtypeStruct((8, LANES), F32)),
        in_specs=[_HBM, _HBM], out_specs=(_SEM, _SEM, _HBM, _HBM, pl.BlockSpec(memory_space=pltpu.VMEM)),
        input_output_aliases={0: 2, 1: 3},
        compiler_params=pltpu.CompilerParams(has_side_effects=_EFFECT),
    )(_in_hbm(part), _in_hbm(land))
    return dict(send_sems=outs[0], recv_sems=outs[1], src=outs[2], land=outs[3]), outs[4]


def _rs_chips_wait(handle, after, *, name):
    def body(src, dst, send_sems, recv_sems, *rest):
        x, y, c, chips = _mesh_pos()
        for j, chip in enumerate(chips):
            cp = pltpu.make_async_remote_copy(src_ref=src.at[2 * chip[0] + chip[1]], dst_ref=dst.at[2 * chip[0] + chip[1]],
                                              send_sem=send_sems.at[j], recv_sem=recv_sems.at[j], device_id=(*chip, c),
                                              device_id_type=MESH)
            cp.wait_send()
            cp.wait_recv()

    src, land = handle["src"], handle["land"]
    outs = pl.pallas_call(
        body, name=name,
        out_shape=(pltpu.HBM(src.shape, src.dtype), pltpu.HBM(land.shape, land.dtype)),
        in_specs=[_HBM, _HBM, _SEM, _SEM] + [_ANY] * len(after), out_specs=(_HBM, _HBM),
        input_output_aliases={0: 0, 1: 1},
        compiler_params=pltpu.CompilerParams(has_side_effects=_EFFECT),
    )(src, land, handle["send_sems"], handle["recv_sems"], *after)
    return outs[0], outs[1]


def _adam_math(w, g, m, v):
    m_new = ADAM_B1 * m + (1.0 - ADAM_B1) * g
    v_new = ADAM_B2 * v + (1.0 - ADAM_B2) * (g * g)
    m_hat = m_new / (1.0 - ADAM_B1 ** ADAM_STEP)
    v_hat = v_new / (1.0 - ADAM_B2 ** ADAM_STEP)
    delta = -ADAM_LR * (m_hat / (jnp.sqrt(v_hat) + ADAM_EPS) + ADAM_WD * w)
    return delta, m_new, v_new


def _sum_adam(part, land, slots, w, m, v, layer, prev, *, name):
    n_layers, rows, cols = w.shape
    tr = _tile(rows, 256, _row_mult(land.dtype))
    n_prev = 0 if prev is None else 4

    def body(slots_ref, p0_ref, p1_ref, p2_ref, p3_ref, w_ref, m_ref, v_ref, *rest):
        g_ref, d_ref, mo_ref, vo_ref = rest[n_prev:]
        g = p0_ref[...].astype(F32)
        for ref in (p1_ref, p2_ref, p3_ref):
            g = g + ref[...].astype(F32)
        delta, m_new, v_new = _adam_math(w_ref[...], g, m_ref[...], v_ref[...])
        g_ref[...] = g
        d_ref[...] = delta
        mo_ref[...] = m_new
        vo_ref[...] = v_new

    blk = pl.BlockSpec((None, tr, cols), lambda r, s: (layer, r, 0))
    slot = lambda j: pl.BlockSpec((None, None, tr, cols), lambda r, s: (s[j], 0, r, 0))
    shp = jax.ShapeDtypeStruct((n_layers, rows, cols), F32)
    grid_spec = pltpu.PrefetchScalarGridSpec(
        num_scalar_prefetch=1, grid=(rows // tr,),
        in_specs=[slot(0), slot(1), slot(2), slot(3), blk, blk, blk] + [_ANY] * n_prev,
        out_specs=(blk, blk, blk, blk))
    return pl.pallas_call(
        body, name=name, out_shape=(shp, shp, shp, shp), grid_spec=grid_spec,
        input_output_aliases={8 + i: i for i in range(n_prev)},
        compiler_params=_params(("parallel",)),
    )(slots, part, land, land, land, w, m, v, *(prev or ()))


def _sum_slots(land, *, name):
    _, _, rows, cols = land.shape
    tr = _tile(rows, 256, 8)

    def body(land_ref, g_ref):
        g = land_ref[0]
        for i in range(1, N_CHIP):
            g = g + land_ref[i]
        g_ref[...] = g

    return pl.pallas_call(
        body, name=name, out_shape=jax.ShapeDtypeStruct((rows, cols), F32), grid=(rows // tr,),
        in_specs=[pl.BlockSpec((N_CHIP, None, tr, cols), lambda r: (0, 0, r, 0))],
        out_specs=pl.BlockSpec((tr, cols), lambda r: (r, 0)),
        compiler_params=_params(("parallel",)),
    )(land)


def _adam_flat(w, g, m, v, *, name):
    rows, cols = w.shape
    tr = _tile(rows, 256, 8)

    def body(w_ref, g_ref, m_ref, v_ref, d_ref, mo_ref, vo_ref):
        delta, m_new, v_new = _adam_math(w_ref[...], g_ref[...], m_ref[...], v_ref[...])
        d_ref[...] = delta
        mo_ref[...] = m_new
        vo_ref[...] = v_new

    blk = pl.BlockSpec((tr, cols), lambda r: (r, 0))
    shp = jax.ShapeDtypeStruct((rows, cols), F32)
    return pl.pallas_call(
        body, name=name, out_shape=(shp, shp, shp), grid=(rows // tr,), in_specs=[blk] * 4, out_specs=(blk, blk, blk),
        compiler_params=_params(("parallel",)),
    )(w, g, m, v)


_BIG = (("ffn1_w_in", 1), ("ffn1_w_out", 0), ("w_in", 1), ("w_out", 0), ("ffn2_w_in", 1), ("ffn2_w_out", 0))
_SMALL = ("meta_tokens", "ffn1_norm", "mix_norm", "conv_w", "conv_b", "lru_wa", "lru_ba", "lru_wx", "lru_bx", "lru_a_param",
          "pool_w", "pool_b", "pool_scale", "ffn2_norm", "final_norm")
_SMALL_SHARD_AXIS = {"meta_tokens": 1, "conv_w": 2, "pool_w": 2}
_PACK_COLS = 1024


def _pack(arrs):
    flat = jnp.concatenate([a.reshape(-1) for a in arrs])
    unit = N_DEV * 8 * _PACK_COLS
    total = -(-flat.shape[0] // unit) * unit
    flat = jnp.pad(flat, (0, total - flat.shape[0]))
    return flat.reshape(total // _PACK_COLS, _PACK_COLS)


def _unpack(packed, shapes):
    flat = packed.reshape(-1)
    out, off = [], 0
    for s in shapes:
        size = 1
        for v in s:
            size *= v
        out.append(flat[off:off + size].reshape(s))
        off += size
    return out


def _my_shard(full, axis, dev):
    size = full.shape[axis] // N_DEV
    return lax.dynamic_slice_in_dim(full, dev * size, size, axis)


def kernel(x, meta_tokens, ffn1_norm, ffn1_w_in, ffn1_w_out, mix_norm, w_in, conv_w, conv_b, lru_wa, lru_ba, lru_wx, lru_bx, lru_a_param, pool_w, pool_b, pool_scale, w_out, ffn2_norm, ffn2_w_in, ffn2_w_out, final_norm, loss_target, m_meta_tokens, m_ffn1_norm, m_ffn1_w_in, m_ffn1_w_out, m_mix_norm, m_w_in, m_conv_w, m_conv_b, m_lru_wa, m_lru_ba, m_lru_wx, m_lru_bx, m_lru_a_param, m_pool_w, m_pool_b, m_pool_scale, m_w_out, m_ffn2_norm, m_ffn2_w_in, m_ffn2_w_out, m_final_norm, v_meta_tokens, v_ffn1_norm, v_ffn1_w_in, v_ffn1_w_out, v_mix_norm, v_w_in, v_conv_w, v_conv_b, v_lru_wa, v_lru_ba, v_lru_wx, v_lru_bx, v_lru_a_param, v_pool_w, v_pool_b, v_pool_scale, v_w_out, v_ffn2_norm, v_ffn2_w_in, v_ffn2_w_out, v_final_norm):
    names = ("meta_tokens", "ffn1_norm", "ffn1_w_in", "ffn1_w_out", "mix_norm", "w_in", "conv_w", "conv_b", "lru_wa", "lru_ba",
             "lru_wx", "lru_bx", "lru_a_param", "pool_w", "pool_b", "pool_scale", "w_out", "ffn2_norm", "ffn2_w_in", "ffn2_w_out",
             "final_norm")
    w = dict(zip(names, (meta_tokens, ffn1_norm, ffn1_w_in, ffn1_w_out, mix_norm, w_in, conv_w, conv_b, lru_wa, lru_ba, lru_wx,
                         lru_bx, lru_a_param, pool_w, pool_b, pool_scale, w_out, ffn2_norm, ffn2_w_in, ffn2_w_out, final_norm)))
    mom = dict(zip(names, (m_meta_tokens, m_ffn1_norm, m_ffn1_w_in, m_ffn1_w_out, m_mix_norm, m_w_in, m_conv_w, m_conv_b, m_lru_wa,
                           m_lru_ba, m_lru_wx, m_lru_bx, m_lru_a_param, m_pool_w, m_pool_b, m_pool_scale, m_w_out, m_ffn2_norm,
                           m_ffn2_w_in, m_ffn2_w_out, m_final_norm)))
    vel = dict(zip(names, (v_meta_tokens, v_ffn1_norm, v_ffn1_w_in, v_ffn1_w_out, v_mix_norm, v_w_in, v_conv_w, v_conv_b, v_lru_wa,
                           v_lru_ba, v_lru_wx, v_lru_bx, v_lru_a_param, v_pool_w, v_pool_b, v_pool_scale, v_w_out, v_ffn2_norm,
                           v_ffn2_w_in, v_ffn2_w_out, v_final_norm)))
    n_layers, d = ffn1_norm.shape
    n_meta = meta_tokens.shape[0]
    seq = x.shape[1]
    t_valid = n_meta + seq
    tp = -(-t_valid // ROW_ALIGN) * ROW_ALIGN
    dev = 4 * lax.axis_index("x") + 2 * lax.axis_index("y") + lax.axis_index("c")
    my_c = lax.axis_index("c").astype(jnp.int32).reshape(1)
    dev1 = dev.astype(jnp.int32).reshape(1)
    mx, my = lax.axis_index("x"), lax.axis_index("y")
    slots = jnp.stack([2 * mx + my, 2 * (1 - mx) + my, 2 * mx + (1 - my), 2 * (1 - mx) + (1 - my)]).astype(jnp.int32)

    big_axis = dict(_BIG)
    vec = lambda a: a.reshape(a.shape[0], 1, a.shape[1])
    norms = dict(ffn1=vec(ffn1_norm), mix=vec(mix_norm), ffn2=vec(ffn2_norm))
    units = [(kind, l) for l in range(n_layers) for kind in ("ffn1", "mix", "ffn2")]

    def shard(k, l):
        return w[k][l:l + 1].astype(BF16), big_axis[k] + 1

    groups = [[shard("ffn1_w_in", 0), (meta_tokens, 1)], [shard("ffn1_w_out", 0)]]
    for kind, l in units[1:]:
        if kind == "mix":
            groups.append([shard("w_in", l), shard("w_out", l)] + ([(conv_w, 2), (pool_w, 2)] if l == 0 else []))
        else:
            groups.append([shard(f"{kind}_w_in", l), shard(f"{kind}_w_out", l)])
    handles = {}
    state = dict(token=None, started=0)

    def order():
        return [] if state["token"] is None else [state["token"]]

    def start_next():
        i = state["started"]
        if i < len(groups):
            handles[i], state["token"] = _gather_start([s for s, _ in groups[i]], [ax for _, ax in groups[i]], order(),
                                                       name=f"gather{i}_start")
            state["started"] = i + 1

    def finish(i, after):
        handle = handles.pop(i)
        srcs, fulls = _gather_wait(handle, list(after) + order(), name=f"gather{i}_wait")
        fulls = [_place_own(s, z, ax, dev1, name=f"gather{i}_own{a}") for a, (s, z, ax) in enumerate(zip(srcs, fulls, handle["axes"]))]
        fulls = _gather_forward(fulls, handle["axes"], name=f"gather{i}_pass")
        start_next()
        return fulls

    start_next()
    start_next()
    pad = jnp.zeros((tp - t_valid, d), F32)
    tgt = jnp.concatenate([jnp.zeros((n_meta, d), F32), loss_target[0], pad], axis=0)
    h0_base = jnp.concatenate([jnp.zeros((n_meta, d), F32), x[0], pad], axis=0)
    w_in0, meta_full = finish(0, [tgt, h0_base] + [s for grp in groups[2:] for s, _ in grp])
    h0 = lax.dynamic_update_slice(h0_base, meta_full, (0, 0))
    n_act, gu, act = _ffn_fwd_in(h0, norms["ffn1"], 0, w_in0, "l0ffn1", order())
    (w_out0,) = finish(1, [act])
    h = _ffn_fwd_out(h0, act, w_out0, "l0ffn1")
    gathered = {units[0]: (w_in0, w_out0)}
    saved = {units[0]: (h0, n_act, gu)}
    mp = None
    for ui, (kind, l) in enumerate(units[1:], start=1):
        fulls = finish(ui + 1, [h])
        tag = f"l{l}{kind}"
        if kind == "mix":
            if l == 0:
                mp = dict(conv_w=fulls[2], conv_b=vec(conv_b), wa=lru_wa.astype(BF16), ba=vec(lru_ba), wx=lru_wx.astype(BF16),
                          bx=vec(lru_bx), a_param=vec(lru_a_param), pool_w=fulls[3].astype(BF16), pool_b=vec(pool_b),
                          pool_scale=vec(pool_scale))
            h, saved[(kind, l)] = _mix_block_fwd(h, norms["mix"], l, fulls[0], fulls[1], mp, tag, order())
        else:
            n_act, gu, act = _ffn_fwd_in(h, norms[kind], l, fulls[0], tag, order())
            saved[(kind, l)] = (h, n_act, gu)
            h = _ffn_fwd_out(h, act, fulls[1], tag)
        gathered[(kind, l)] = (fulls[0], fulls[1])
    dh, dhb, g_final, loss_local = _final_loss(h, final_norm.reshape(1, d), tgt, n_meta, t_valid, name="final_loss")
    loss = lax.psum(loss_local[0, 0], ("x", "y", "c"))

    pending = []
    big_out = {}

    def drain(after):
        while pending:
            k, l, handle = pending.pop(0)
            part, land = _rs_chips_wait(handle, after, name=f"rs_{k}{l}_wait")
            big_out[k] = _sum_adam(part, land, slots, w[k], mom[k], vel[k], l, big_out.get(k), name=f"adam_{k}{l}")

    sib_pending = []

    def to_chips(k, l, g, land):
        part = _rs_add_sibling([g], big_axis[k], land, my_c, name=f"rs_{k}{l}_add")
        handle, token = _rs_chips_start(part, name=f"rs_{k}{l}_start")
        pending.append((k, l, handle))
        return token

    def finish_sibling(after):
        token = None
        while sib_pending:
            k, l, handle = sib_pending.pop(0)
            g, land = _rs_sibling_wait(handle, after, name=f"rs_{k}{l}_sibwait")
            token = to_chips(k, l, g, land)
        return token

    def emitter(kind, l):
        last_unit = (kind, l) == units[0]

        def emit(which, g):
            if which == "w_out":
                drain([g])
            token = finish_sibling([g])
            order_after = [] if token is None else [token]
            if which == "flush":
                return order_after
            k = which if kind == "mix" else f"{kind}_{which}"
            if last_unit and which == "w_in":
                (land,) = _rs_exchange_sibling([[g]], [big_axis[k]], name=f"rs_{k}{l}_sib")
                return order_after + [to_chips(k, l, g, land)]
            handle, token = _rs_sibling_start(g, big_axis[k], name=f"rs_{k}{l}_sibstart")
            sib_pending.append((k, l, handle))
            return order_after + [token]
        return emit

    g_norm = {k: [None] * n_layers for k in ("ffn1", "mix", "ffn2")}
    g_mix = [None] * n_layers
    for kind, l in reversed(units):
        w_i, w_o = gathered[(kind, l)]
        tag = f"l{l}{kind}"
        if kind == "mix":
            dh, dhb, g_norm[kind][l], g_mix[l] = _mix_block_bwd(dh, dhb, saved[(kind, l)], norms[kind], l, w_i, w_o, mp, tag,
                                                                 emitter(kind, l), [])
        else:
            dh, dhb, g_norm[kind][l] = _ffn_bwd(dh, dhb, saved[(kind, l)], norms[kind], l, w_i, w_o, tag, emitter(kind, l), [])
    drain([dh])
    dh0 = dh
    grad_x = dh0[n_meta:t_valid][None]

    mix_key = dict(conv_w="conv_w", conv_b="conv_b", lru_wa="wa", lru_ba="ba", lru_wx="wx", lru_bx="bx", lru_a_param="a_param",
                   pool_w="pool_w", pool_b="pool_b", pool_scale="pool_scale")
    small_local = {"meta_tokens": dh0[:n_meta], "final_norm": g_final.reshape(d)}
    for k in ("ffn1", "mix", "ffn2"):
        small_local[f"{k}_norm"] = jnp.stack([g.reshape(d) for g in g_norm[k]])
    for k, mk in mix_key.items():
        small_local[k] = jnp.stack([g_mix[l][mk] for l in range(n_layers)]).reshape(
            w[k].shape if k not in _SMALL_SHARD_AXIS else small_shape_full(w[k], _SMALL_SHARD_AXIS[k]))
    small_shapes = [small_local[k].shape for k in _SMALL]
    packed = _pack([small_local[k] for k in _SMALL])
    (land,) = _rs_exchange_sibling([[packed]], [0], name="rs_small_sib")
    part = _rs_add_sibling([packed], 0, land, my_c, name="rs_small_add")
    (land2,) = _rs_exchange_chips([part], name="rs_small_chips")

    out_g, out_d, out_m, out_v = {}, {}, {}, {}
    for k, _ in _BIG:
        out_g[k], out_d[k], out_m[k], out_v[k] = big_out[k]
    small_block = _sum_slots(land2, name="rs_sum_small")
    small_full = _all_gather([small_block], [0], name="gather_small")[0]
    small_g = dict(zip(_SMALL, _unpack(small_full, small_shapes)))
    for k, ax in _SMALL_SHARD_AXIS.items():
        small_g[k] = _my_shard(small_g[k], ax, dev)
    shapes_local = [w[k].shape for k in _SMALL]
    d_p, m_p, v_p = _adam_flat(_pack([w[k] for k in _SMALL]), _pack([small_g[k] for k in _SMALL]),
                               _pack([mom[k] for k in _SMALL]), _pack([vel[k] for k in _SMALL]), name="adam_small")
    for k, dd, mm, vv in zip(_SMALL, _unpack(d_p, shapes_local), _unpack(m_p, shapes_local), _unpack(v_p, shapes_local)):
        out_g[k], out_d[k], out_m[k], out_v[k] = small_g[k], dd, mm, vv

    return (loss, grad_x, *[out_g[k] for k in names], *[out_d[k] for k in names], *[out_m[k] for k in names],
            *[out_v[k] for k in names])


def small_shape_full(w_shard, axis):
    return w_shard.shape[:axis] + (N_DEV * w_shard.shape[axis],) + w_shard.shape[axis + 1:]
```

```python
import functools

import jax
import jax.numpy as jnp
from jax import lax
from jax.experimental import pallas as pl
from jax.experimental.pallas import tpu as pltpu

F32 = jnp.float32
BF16 = jnp.bfloat16
MESH = pl.DeviceIdType.MESH

N_DEV = 8
N_CHIP = 4
RMS_EPS = 1e-6
LRU_C = 8.0
CONV_WIDTH = 4
POOL_WINDOWS = (2, 4, 8, 16)
HIST = 16
ADAM_LR = 0.001
ADAM_B1 = 0.9
ADAM_B2 = 0.999
ADAM_EPS = 1e-08
ADAM_WD = 0.01
ADAM_STEP = 10
ROW_ALIGN = 128
V7X_VMEM_LIMIT = 56 * 1024 * 1024
BF16_ROWS = 16
K_TILE = 2816
SCAN_UNROLL = 8
EPILOGUE_ROWS = 528
LANES = 128


def _tile(n, cap, mult=BF16_ROWS):
    best = None
    d = mult
    while d <= min(n, cap):
        if n % d == 0:
            best = d
        d += mult
    if best is None:
        raise ValueError(f"no tile for {n} (cap {cap}, multiple of {mult})")
    return best


def _row_mult(dtype):
    return 8 * 4 // jnp.dtype(dtype).itemsize


def _params(sem=None):
    return pltpu.CompilerParams(dimension_semantics=sem, vmem_limit_bytes=V7X_VMEM_LIMIT)


def _dot(a, b, mode):
    dims = {"nn": ((1,), (0,)), "nt": ((1,), (1,)), "tn": ((0,), (0,))}[mode]
    return lax.dot_general(a, b, (dims, ((), ())), preferred_element_type=F32)


def _sigmoid(x):
    return 1.0 / (1.0 + jnp.exp(-x))


_GELU_C = 0.7978845608028654
_GELU_K = 0.044715


def _gelu(x):
    return 0.5 * x * (1.0 + jnp.tanh(_GELU_C * (x + _GELU_K * x * x * x)))


def _gelu_and_grad(x):
    x2 = x * x
    th = jnp.tanh(_GELU_C * x * (1.0 + _GELU_K * x2))
    half = 0.5 * (1.0 + th)
    return x * half, half + 0.5 * x * (1.0 - th * th) * _GELU_C * (1.0 + 3.0 * _GELU_K * x2)


def _neg_expm1(x):
    p = 1.0 / 5040.0
    for c in (1.0 / 720.0, 1.0 / 120.0, 1.0 / 24.0, 1.0 / 6.0, 0.5, 1.0):
        p = p * x + c
    return jnp.where(x > -0.25, -(x * p), 1.0 - jnp.exp(x))


def _softplus_neg(p):
    e = jnp.exp(-jnp.abs(p))
    u = 1.0 + e
    l1p = jnp.where(u == 1.0, e, jnp.log(u) * e / (u - 1.0 + (u == 1.0).astype(F32)))
    return jnp.maximum(-p, 0.0) + l1p


def _operand_spec(arr, br, bc, ridx, cidx, layer, split):
    if split:
        ncb = arr.shape[2] // bc
        return pl.BlockSpec((None, br, bc), lambda i, j, k: (cidx(i, j, k) // ncb, ridx(i, j, k), cidx(i, j, k) % ncb))
    if layer is not None:
        return pl.BlockSpec((None, br, bc), lambda i, j, k: (layer, ridx(i, j, k), cidx(i, j, k)))
    return pl.BlockSpec((br, bc), lambda i, j, k: (ridx(i, j, k), cidx(i, j, k)))


def _matmul(a, b, *, mode, m, n, kdim, tm, tn, tk, out_dtype, name, scale=None, residual=None,
            a_layer=None, b_layer=None, a_split=False, b_split=False, deps=()):
    nk = kdim // tk
    I = lambda i, j, k: i
    J = lambda i, j, k: j
    K = lambda i, j, k: k
    if mode == "nn":
        a_spec = _operand_spec(a, tm, tk, I, K, a_layer, a_split)
        b_spec = _operand_spec(b, tk, tn, K, J, b_layer, b_split)
    elif mode == "nt":
        a_spec = _operand_spec(a, tm, tk, I, K, a_layer, a_split)
        b_spec = _operand_spec(b, tn, tk, J, K, b_layer, b_split)
    else:
        a_spec = _operand_spec(a, tk, tm, K, I, a_layer, a_split)
        b_spec = _operand_spec(b, tk, tn, K, J, b_layer, b_split)
    in_specs = [a_spec, b_spec]
    operands = [a, b]
    if residual is not None:
        in_specs.append(pl.BlockSpec((tm, tn), lambda i, j, k: (i, j)))
        operands.append(residual)
    in_specs += [pl.BlockSpec(memory_space=pl.ANY)] * len(deps)
    operands += list(deps)
    n_in = len(operands)

    def finish(acc, res_ref, o_ref):
        if scale is not None:
            acc = acc * scale
        if res_ref is not None:
            acc = acc + res_ref[...]
        o_ref[...] = acc.astype(o_ref.dtype)

    def body(*refs):
        a_ref, b_ref = refs[0], refs[1]
        res_ref = refs[2] if residual is not None else None
        o_ref = refs[n_in]
        part = _dot(a_ref[...], b_ref[...], mode)
        if nk == 1:
            finish(part, res_ref, o_ref)
        else:
            acc_ref = refs[-1]
            k = pl.program_id(2)

            @pl.when(k == 0)
            def _():
                acc_ref[...] = part

            @pl.when(k > 0)
            def _():
                acc_ref[...] += part

            @pl.when(k == nk - 1)
            def _():
                finish(acc_ref[...], res_ref, o_ref)

    return pl.pallas_call(
        body, name=name,
        out_shape=jax.ShapeDtypeStruct((m, n), out_dtype),
        grid=(m // tm, n // tn, nk),
        in_specs=in_specs,
        out_specs=pl.BlockSpec((tm, tn), lambda i, j, k: (i, j)),
        scratch_shapes=[pltpu.VMEM((tm, tn), F32)] if nk > 1 else [],
        compiler_params=_params(("parallel", "parallel", "arbitrary")),
    )(*operands)


def _ffn_in_fwd(n_act, w_in, layer, *, name, deps=()):
    tp, d = n_act.shape
    f = w_in.shape[2] // 2
    tm = _tile(tp, 2112)
    tn = _tile(f, 256, LANES)
    nj = f // tn

    ch = _tile(tm, EPILOGUE_ROWS)

    def body(n_ref, wg_ref, wu_ref, *rest):
        gu_ref, a_ref = rest[len(deps):]
        for r in range(tm // ch):
            rows = pl.ds(r * ch, ch)
            x = n_ref[rows, :]
            g = _dot(x, wg_ref[...], "nn")
            u = _dot(x, wu_ref[...], "nn")
            gu_ref[0, rows, :] = g.astype(BF16)
            gu_ref[1, rows, :] = u.astype(BF16)
            a_ref[rows, :] = (g * _sigmoid(g) * u).astype(BF16)

    return pl.pallas_call(
        body, name=name,
        out_shape=(jax.ShapeDtypeStruct((2, tp, f), BF16), jax.ShapeDtypeStruct((tp, f), BF16)),
        grid=(tp // tm, nj),
        in_specs=[pl.BlockSpec((tm, d), lambda i, j: (i, 0)),
                  pl.BlockSpec((None, d, tn), lambda i, j: (layer, 0, j)),
                  pl.BlockSpec((None, d, tn), lambda i, j: (layer, 0, j + nj))] + [pl.BlockSpec(memory_space=pl.ANY)] * len(deps),
        out_specs=(pl.BlockSpec((2, tm, tn), lambda i, j: (0, i, j)), pl.BlockSpec((tm, tn), lambda i, j: (i, j))),
        compiler_params=_params(("parallel", "arbitrary")),
    )(n_act, w_in, w_in, *deps)


def _ffn_bwd_act(dhb, w_out, layer, gu, *, name, deps=()):
    tp, d = dhb.shape
    f = w_out.shape[1]
    tm = _tile(tp, 2112)
    tn = _tile(f, 256, LANES)

    ch = _tile(tm, EPILOGUE_ROWS)

    def body(dh_ref, w_ref, gu_ref, *rest):
        dz_ref, a_ref = rest[len(deps):]
        for r in range(tm // ch):
            rows = pl.ds(r * ch, ch)
            da = 0.5 * _dot(dh_ref[rows, :], w_ref[...], "nt")
            g = gu_ref[0, rows, :].astype(F32)
            u = gu_ref[1, rows, :].astype(F32)
            s = _sigmoid(g)
            sg = g * s
            dz_ref[0, rows, :] = (da * u * (s * (1.0 + g * (1.0 - s)))).astype(BF16)
            dz_ref[1, rows, :] = (da * sg).astype(BF16)
            a_ref[rows, :] = (sg * u).astype(BF16)

    return pl.pallas_call(
        body, name=name,
        out_shape=(jax.ShapeDtypeStruct((2, tp, f), BF16), jax.ShapeDtypeStruct((tp, f), BF16)),
        grid=(tp // tm, f // tn),
        in_specs=[pl.BlockSpec((tm, d), lambda i, j: (i, 0)),
                  pl.BlockSpec((None, tn, d), lambda i, j: (layer, j, 0)),
                  pl.BlockSpec((2, tm, tn), lambda i, j: (0, i, j))] + [pl.BlockSpec(memory_space=pl.ANY)] * len(deps),
        out_specs=(pl.BlockSpec((2, tm, tn), lambda i, j: (0, i, j)), pl.BlockSpec((tm, tn), lambda i, j: (i, j))),
        compiler_params=_params(("parallel", "arbitrary")),
    )(dhb, w_out, gu, *deps)


def _rms_fwd(h, g, layer, *, name, deps=()):
    tp, d = h.shape
    tr = _tile(tp, 528)

    def body(h_ref, g_ref, *rest):
        n_ref = rest[len(deps)]
        x = h_ref[...]
        r = lax.rsqrt(jnp.mean(x * x, axis=-1, keepdims=True) + RMS_EPS)
        n_ref[...] = (x * r * g_ref[...]).astype(BF16)

    return pl.pallas_call(
        body, name=name, out_shape=jax.ShapeDtypeStruct((tp, d), BF16), grid=(tp // tr,),
        in_specs=[pl.BlockSpec((tr, d), lambda i: (i, 0)), pl.BlockSpec((None, 1, d), lambda i: (layer, 0, 0))]
        + [pl.BlockSpec(memory_space=pl.ANY)] * len(deps),
        out_specs=pl.BlockSpec((tr, d), lambda i: (i, 0)),
        compiler_params=_params(("parallel",)),
    )(h, g, *deps)


def _rms_bwd(h, g, layer, dn, dres, *, name, deps=()):
    tp, d = h.shape
    tr = _tile(tp, 528)

    def body(h_ref, g_ref, dn_ref, dres_ref, *rest):
        dh_ref, dhb_ref, dg_ref = rest[len(deps):]
        x = h_ref[...]
        r = lax.rsqrt(jnp.mean(x * x, axis=-1, keepdims=True) + RMS_EPS)
        xhat = x * r
        dn_v = dn_ref[...]
        dxhat = dn_v * g_ref[...]
        dh = dres_ref[...] + r * (dxhat - xhat * jnp.mean(dxhat * xhat, axis=-1, keepdims=True))
        dh_ref[...] = dh
        dhb_ref[...] = dh.astype(BF16)
        part = jnp.sum(dn_v * xhat, axis=0, keepdims=True)

        @pl.when(pl.program_id(0) == 0)
        def _():
            dg_ref[...] = part

        @pl.when(pl.program_id(0) > 0)
        def _():
            dg_ref[...] += part

    return pl.pallas_call(
        body, name=name,
        out_shape=(jax.ShapeDtypeStruct((tp, d), F32), jax.ShapeDtypeStruct((tp, d), BF16), jax.ShapeDtypeStruct((1, d), F32)),
        grid=(tp // tr,),
        in_specs=[pl.BlockSpec((tr, d), lambda i: (i, 0)), pl.BlockSpec((None, 1, d), lambda i: (layer, 0, 0)),
                  pl.BlockSpec((tr, d), lambda i: (i, 0)), pl.BlockSpec((tr, d), lambda i: (i, 0))]
        + [pl.BlockSpec(memory_space=pl.ANY)] * len(deps),
        out_specs=(pl.BlockSpec((tr, d), lambda i: (i, 0)), pl.BlockSpec((tr, d), lambda i: (i, 0)),
                   pl.BlockSpec((1, d), lambda i: (0, 0))),
        compiler_params=_params(("arbitrary",)),
    )(h, g, dn, dres, *deps)


def _final_loss(h, g, tgt, n_meta, t_valid, *, name):
    tp, d = h.shape
    tr = _tile(tp, 528)

    def body(h_ref, g_ref, t_ref, dh_ref, dhb_ref, dg_ref, loss_ref):
        i = pl.program_id(0)
        x = h_ref[...]
        r = lax.rsqrt(jnp.mean(x * x, axis=-1, keepdims=True) + RMS_EPS)
        xhat = x * r
        gv = g_ref[...]
        row = i * tr + lax.broadcasted_iota(jnp.int32, (tr, 1), 0)
        valid = jnp.logical_and(row >= n_meta, row < t_valid)
        err = jnp.where(valid, xhat * gv - t_ref[...], 0.0)
        dy = err * (1.0 / d)
        dxhat = dy * gv
        dh = r * (dxhat - xhat * jnp.mean(dxhat * xhat, axis=-1, keepdims=True))
        dh_ref[...] = dh
        dhb_ref[...] = dh.astype(BF16)
        dg_part = jnp.sum(dy * xhat, axis=0, keepdims=True)
        loss_part = jnp.sum(jnp.sum(err * err, axis=1, keepdims=True), axis=0, keepdims=True) * (0.5 / d)

        @pl.when(i == 0)
        def _():
            dg_ref[...] = dg_part
            loss_ref[...] = loss_part

        @pl.when(i > 0)
        def _():
            dg_ref[...] += dg_part
            loss_ref[...] += loss_part

    return pl.pallas_call(
        body, name=name,
        out_shape=(jax.ShapeDtypeStruct((tp, d), F32), jax.ShapeDtypeStruct((tp, d), BF16),
                   jax.ShapeDtypeStruct((1, d), F32), jax.ShapeDtypeStruct((1, 1), F32)),
        grid=(tp // tr,),
        in_specs=[pl.BlockSpec((tr, d), lambda i: (i, 0)), pl.BlockSpec((1, d), lambda i: (0, 0)),
                  pl.BlockSpec((tr, d), lambda i: (i, 0))],
        out_specs=(pl.BlockSpec((tr, d), lambda i: (i, 0)), pl.BlockSpec((tr, d), lambda i: (i, 0)),
                   pl.BlockSpec((1, d), lambda i: (0, 0)), pl.BlockSpec((1, 1), lambda i: (0, 0))),
        compiler_params=_params(("arbitrary",)),
    )(h, g, tgt)


def _mix_param_specs(mp, layer, imap):
    def vec(a):
        return pl.BlockSpec((None,) + a.shape[1:], lambda *g: (layer,) + (0,) * (a.ndim - 1))
    return [vec(mp[k]) for k in ("conv_w", "conv_b", "wa", "ba", "wx", "bx", "a_param", "pool_w", "pool_b", "pool_scale")]


def _mix_param_list(mp):
    return [mp[k] for k in ("conv_w", "conv_b", "wa", "ba", "wx", "bx", "a_param", "pool_w", "pool_b", "pool_scale")]


def _lru_gates(xc, wa_h, ba_h, wx_h, bx_h, sp_h):
    xb = xc.astype(BF16)
    ra = _sigmoid(_dot(xb, wa_h, "nn") + ba_h)
    ii = _sigmoid(_dot(xb, wx_h, "nn") + bx_h)
    la = -LRU_C * ra * sp_h
    return ra, ii, la, jnp.exp(la)


def _shifted(x, k):
    return x if k == 0 else pltpu.roll(x, k % x.shape[0], axis=0)


def _conv_taps(xbuf, sl, tc):
    x = xbuf[:, sl]
    return [_shifted(x, CONV_WIDTH - 1 - k)[HIST:HIST + tc] for k in range(CONV_WIDTH)]


def _conv_fwd(taps, cw_ref, cb_ref, sl):
    xc = cb_ref[:, sl]
    for k in range(CONV_WIDTH):
        xc = xc + cw_ref[k:k + 1, sl] * taps[k]
    return xc


def _window_sum(x, win, direction):
    s = x
    step = 1
    while step < win:
        s = s + _shifted(s, direction * step)
        step *= 2
    return s


def _pool_delta(pbuf, cols, win, t0, tc):
    x = pbuf[:, cols]
    u = x[HIST:HIST + tc]
    s = _window_sum(x, win, 1)[HIST:HIST + tc]
    t = t0 + lax.broadcasted_iota(jnp.int32, (tc, 1), 0)
    inv_cnt = 1.0 / jnp.minimum(t + 1, win).astype(F32)
    return s * inv_cnt - u, inv_cnt


def _mix_fwd(z, mp, layer, *, name):
    tp = z.shape[0]
    dl = z.shape[1] // 3
    n_heads, hd = mp["wa"].shape[1], mp["wa"].shape[2]
    n_groups, gd = mp["pool_w"].shape[1], mp["pool_w"].shape[2]
    tc = _tile(tp, 384)

    def body(z_ref, cw_ref, cb_ref, wa_ref, ba_ref, wx_ref, bx_ref, ap_ref, pw_ref, pb_ref, ps_ref,
             m_ref, hs_ref, xbuf, pbuf, a_s, b_s, hcar):
        i = pl.program_id(0)

        @pl.when(i == 0)
        def _():
            xbuf[pl.ds(0, HIST), :] = jnp.zeros((HIST, dl), F32)
            pbuf[pl.ds(0, HIST), :] = jnp.zeros((HIST, dl), F32)
            hcar[...] = jnp.zeros((1, dl), F32)

        @pl.when(i > 0)
        def _():
            xbuf[pl.ds(0, HIST), :] = xbuf[pl.ds(tc, HIST), :]
            pbuf[pl.ds(0, HIST), :] = pbuf[pl.ds(tc, HIST), :]

        xbuf[pl.ds(HIST, tc), :] = z_ref[:, pl.ds(0, dl)]
        pbuf[pl.ds(HIST, tc), :] = z_ref[:, pl.ds(2 * dl, dl)]
        sp = _softplus_neg(ap_ref[...])
        for h in range(n_heads):
            sl = pl.ds(h * hd, hd)
            xc = _conv_fwd(_conv_taps(xbuf, sl, tc), cw_ref, cb_ref, sl)
            _, ii, la, a = _lru_gates(xc, wa_ref[h], ba_ref[:, sl], wx_ref[h], bx_ref[:, sl], sp[:, h * hd:(h + 1) * hd])
            a_s[:, sl] = a
            b_s[:, sl] = jnp.sqrt(_neg_expm1(2.0 * la)) * ii * xc

        def step(t, hprev):
            hnew = a_s[pl.ds(t, 1), :] * hprev + b_s[pl.ds(t, 1), :]
            hs_ref[pl.ds(t, 1), :] = hnew
            return hnew

        hcar[...] = lax.fori_loop(0, tc, step, hcar[...], unroll=SCAN_UNROLL)
        for h in range(n_heads):
            sl = pl.ds(h * hd, hd)
            m_ref[:, sl] = (hs_ref[:, sl] * _gelu(z_ref[:, pl.ds(dl + h * hd, hd)])).astype(BF16)
        for g in range(n_groups):
            cols = pl.ds(g * gd, gd)
            dlt, _ = _pool_delta(pbuf, cols, POOL_WINDOWS[g], i * tc, tc)
            q = _dot(dlt.astype(BF16), pw_ref[g], "nn") + pb_ref[:, cols]
            m_ref[:, pl.ds(dl + g * gd, gd)] = (q * ps_ref[:, cols]).astype(BF16)

    return pl.pallas_call(
        body, name=name,
        out_shape=(jax.ShapeDtypeStruct((tp, 2 * dl), BF16), jax.ShapeDtypeStruct((tp, dl), F32)),
        grid=(tp // tc,),
        in_specs=[pl.BlockSpec((tc, 3 * dl), lambda i: (i, 0))] + _mix_param_specs(mp, layer, None),
        out_specs=(pl.BlockSpec((tc, 2 * dl), lambda i: (i, 0)), pl.BlockSpec((tc, dl), lambda i: (i, 0))),
        scratch_shapes=[pltpu.VMEM((HIST + tc, dl), F32), pltpu.VMEM((HIST + tc, dl), F32),
                        pltpu.VMEM((tc, dl), F32), pltpu.VMEM((tc, dl), F32), pltpu.VMEM((1, dl), F32)],
        compiler_params=_params(("arbitrary",)),
    )(z, *_mix_param_list(mp))


_MIX_GRADS = ("conv_w", "conv_b", "wa", "ba", "wx", "bx", "a_param", "pool_w", "pool_b", "pool_scale")


def _mix_bwd(z, hs, dm, mp, layer, *, name):
    tp = z.shape[0]
    dl = z.shape[1] // 3
    n_heads, hd = mp["wa"].shape[1], mp["wa"].shape[2]
    n_groups, gd = mp["pool_w"].shape[1], mp["pool_w"].shape[2]
    tc = _tile(tp, 384)
    nc = tp // tc
    per = tc // HIST

    def body(z_ref, zp_ref, hs_ref, hsp_ref, dm_ref, cw_ref, cb_ref, wa_ref, ba_ref, wx_ref, bx_ref, ap_ref, pw_ref, pb_ref,
             ps_ref, dz_ref, dcw_ref, dcb_ref, dwa_ref, dba_ref, dwx_ref, dbx_ref, dap_ref, dpw_ref, dpb_ref, dps_ref,
             xbuf, pbuf, hbuf, dxbuf, ddbuf, a_s, lam_s, ra_s, ii_s, xc_s, ccar):
        i = pl.program_id(0)
        ci = nc - 1 - i

        @pl.when(i == 0)
        def _():
            dxbuf[pl.ds(tc, HIST), :] = jnp.zeros((HIST, dl), F32)
            ddbuf[pl.ds(tc, HIST), :] = jnp.zeros((HIST, dl), F32)
            ccar[...] = jnp.zeros((1, dl), F32)
            for ref in (dcw_ref, dcb_ref, dwa_ref, dba_ref, dwx_ref, dbx_ref, dap_ref, dpw_ref, dpb_ref, dps_ref):
                ref[...] = jnp.zeros(ref.shape, F32)

        @pl.when(ci == 0)
        def _():
            xbuf[pl.ds(0, HIST), :] = jnp.zeros((HIST, dl), F32)
            pbuf[pl.ds(0, HIST), :] = jnp.zeros((HIST, dl), F32)
            hbuf[pl.ds(0, HIST), :] = jnp.zeros((HIST, dl), F32)

        @pl.when(ci > 0)
        def _():
            xbuf[pl.ds(0, HIST), :] = zp_ref[:, pl.ds(0, dl)]
            pbuf[pl.ds(0, HIST), :] = zp_ref[:, pl.ds(2 * dl, dl)]
            hbuf[pl.ds(0, HIST), :] = hsp_ref[...]

        xbuf[pl.ds(HIST, tc), :] = z_ref[:, pl.ds(0, dl)]
        pbuf[pl.ds(HIST, tc), :] = z_ref[:, pl.ds(2 * dl, dl)]
        hbuf[pl.ds(HIST, tc), :] = hs_ref[...]
        sp = _softplus_neg(ap_ref[...])

        for h in range(n_heads):
            sl = pl.ds(h * hd, hd)
            xc = _conv_fwd(_conv_taps(xbuf, sl, tc), cw_ref, cb_ref, sl)
            ra, ii, _, a = _lru_gates(xc, wa_ref[h], ba_ref[:, sl], wx_ref[h], bx_ref[:, sl], sp[:, h * hd:(h + 1) * hd])
            a_s[:, sl] = a
            ra_s[:, sl] = ra
            ii_s[:, sl] = ii
            xc_s[:, sl] = xc
            gel, gel_grad = _gelu_and_grad(z_ref[:, pl.ds(dl + h * hd, hd)])
            dya = dm_ref[:, sl]
            lam_s[:, sl] = dya * gel
            dz_ref[:, pl.ds(dl + h * hd, hd)] = (dya * hs_ref[:, sl] * gel_grad).astype(BF16)

        def step(r, carry):
            t = tc - 1 - r
            lam = lam_s[pl.ds(t, 1), :] + carry
            lam_s[pl.ds(t, 1), :] = lam
            return a_s[pl.ds(t, 1), :] * lam

        ccar[...] = lax.fori_loop(0, tc, step, ccar[...], unroll=SCAN_UNROLL)

        for h in range(n_heads):
            sl = pl.ds(h * hd, hd)
            sp_h = sp[:, h * hd:(h + 1) * hd]
            lam = lam_s[:, sl]
            a = a_s[:, sl]
            ra = ra_s[:, sl]
            ii = ii_s[:, sl]
            xc = xc_s[:, sl]
            e = _neg_expm1(-2.0 * LRU_C * ra * sp_h)
            inv_mult = lax.rsqrt(e)
            mult = e * inv_mult
            hprev = _shifted(hbuf[:, sl], 1)[HIST:HIST + tc]
            lam_i = lam * ii
            lam_m = lam * mult
            dla = lam * hprev * a - lam_i * xc * (a * a) * inv_mult
            dla_r = dla * (-LRU_C) * ra
            dap_ref[:, sl] += jnp.sum(dla_r, axis=0, keepdims=True)
            dpa = dla_r * sp_h * (1.0 - ra)
            dpx = lam_m * xc * ii * (1.0 - ii)
            dba_ref[:, sl] += jnp.sum(dpa, axis=0, keepdims=True)
            dbx_ref[:, sl] += jnp.sum(dpx, axis=0, keepdims=True)
            xb = xc.astype(BF16)
            dpa_b = dpa.astype(BF16)
            dpx_b = dpx.astype(BF16)
            dwa_ref[h] += _dot(xb, dpa_b, "tn")
            dwx_ref[h] += _dot(xb, dpx_b, "tn")
            dxc = lam_m * ii + _dot(dpa_b, wa_ref[h], "nt") + _dot(dpx_b, wx_ref[h], "nt")
            dxbuf[pl.ds(0, tc), sl] = dxc
            dcb_ref[:, sl] += jnp.sum(dxc, axis=0, keepdims=True)
            taps = _conv_taps(xbuf, sl, tc)
            dx_all = dxbuf[:, sl]
            dzx = jnp.zeros((tc, hd), F32)
            for k in range(CONV_WIDTH):
                dcw_ref[k:k + 1, sl] += jnp.sum(dxc * taps[k], axis=0, keepdims=True)
                dzx = dzx + cw_ref[k:k + 1, sl] * _shifted(dx_all, k - (CONV_WIDTH - 1))[0:tc]
            dz_ref[:, sl] = dzx.astype(BF16)
            dxbuf[pl.ds(tc, HIST), sl] = dxbuf[pl.ds(0, HIST), sl]

        for g in range(n_groups):
            cols = pl.ds(g * gd, gd)
            win = POOL_WINDOWS[g]
            dlt, inv_cnt = _pool_delta(pbuf, cols, win, ci * tc, tc)
            db = dlt.astype(BF16)
            q = _dot(db, pw_ref[g], "nn") + pb_ref[:, cols]
            dyb = dm_ref[:, pl.ds(dl + g * gd, gd)]
            dps_ref[:, cols] += jnp.sum(dyb * q, axis=0, keepdims=True)
            dq = dyb * ps_ref[:, cols]
            dpb_ref[:, cols] += jnp.sum(dq, axis=0, keepdims=True)
            dqb = dq.astype(BF16)
            dpw_ref[g] += _dot(db, dqb, "tn")
            dd = _dot(dqb, pw_ref[g], "nt")
            ddbuf[pl.ds(0, tc), cols] = dd * inv_cnt
            dzp = _window_sum(ddbuf[:, cols], win, -1)[0:tc] - dd
            dz_ref[:, pl.ds(2 * dl + g * gd, gd)] = dzp.astype(BF16)
            ddbuf[pl.ds(tc, HIST), cols] = ddbuf[pl.ds(0, HIST), cols]

        @pl.when(i == nc - 1)
        def _():
            dap_ref[...] = dap_ref[...] * (-_sigmoid(-ap_ref[...]))

    rev = lambda i: (nc - 1 - i, 0)
    prev = lambda i: (jnp.maximum((nc - 1 - i) * per - 1, 0), 0)
    const = lambda a: pl.BlockSpec(a.shape[1:], lambda i: (0,) * (a.ndim - 1))
    plist = _mix_param_list(mp)
    grad_shapes = [jax.ShapeDtypeStruct(a.shape[1:], F32) for a in plist]
    buf = lambda rows: pltpu.VMEM((rows, dl), F32)
    outs = pl.pallas_call(
        body, name=name,
        out_shape=[jax.ShapeDtypeStruct((tp, 3 * dl), BF16)] + grad_shapes,
        grid=(nc,),
        in_specs=[pl.BlockSpec((tc, 3 * dl), rev), pl.BlockSpec((HIST, 3 * dl), prev),
                  pl.BlockSpec((tc, dl), rev), pl.BlockSpec((HIST, dl), prev),
                  pl.BlockSpec((tc, 2 * dl), rev)] + _mix_param_specs(mp, layer, None),
        out_specs=[pl.BlockSpec((tc, 3 * dl), rev)] + [const(a) for a in plist],
        scratch_shapes=[buf(HIST + tc), buf(HIST + tc), buf(HIST + tc), buf(tc + HIST), buf(tc + HIST),
                        buf(tc), buf(tc), buf(tc), buf(tc), buf(tc), buf(1)],
        compiler_params=_params(("arbitrary",)),
    )(z, z, hs, hs, dm, *plist)
    return outs[0], dict(zip(_MIX_GRADS, outs[1:]))


def _ffn_fwd_out(h, act, w_out, tag, deps):
    tp, d = h.shape
    f = w_out.shape[1]
    return _matmul(act, w_out, mode="nn", m=tp, n=d, kdim=f, tm=_tile(tp, 1056), tn=_tile(d, 1024, LANES),
                   tk=_tile(f, K_TILE, LANES), out_dtype=F32, scale=0.5, residual=h, b_layer=0, name=f"{tag}_out", deps=deps)


def _ffn_bwd(dh, dhb, saved, norm, layer, w_in, w_out, tag, emit, deps):
    h, n_act, gu = saved
    tp, d = h.shape
    f = w_out.shape[1]
    dz, act = _ffn_bwd_act(dhb, w_out, 0, gu, name=f"{tag}_bact", deps=deps)
    tok = emit("w_out", _matmul(act, dhb, mode="tn", m=f, n=d, kdim=tp, tm=_tile(f, 512, LANES), tn=_tile(d, 1024, LANES), tk=tp,
                                out_dtype=BF16, scale=0.5, name=f"{tag}_dwout"))
    tok = emit("w_in", _matmul(n_act, dz, mode="tn", m=d, n=2 * f, kdim=tp, tm=_tile(d, 512, LANES), tn=_tile(f, 512, LANES),
                               tk=tp, out_dtype=BF16, b_split=True, name=f"{tag}_dwin", deps=tok))
    dn = _matmul(dz, w_in, mode="nt", m=tp, n=d, kdim=2 * f, tm=_tile(tp, 1056), tn=_tile(d, 1024, LANES), tk=_tile(f, K_TILE, LANES),
                 out_dtype=F32, a_split=True, b_layer=0, name=f"{tag}_dn", deps=tok)
    return _rms_bwd(h, norm, layer, dn, dh, name=f"{tag}_brms", deps=emit("flush", dn))


def _mix_block_fwd(h, n_act, layer, w_in, w_out, mp, tag, deps):
    tp, d = h.shape
    d_in = w_in.shape[2]
    z = _matmul(n_act, w_in, mode="nn", m=tp, n=d_in, kdim=d, tm=_tile(tp, 2112), tn=_tile(d_in, 512, LANES), tk=d,
                out_dtype=F32, b_layer=0, name=f"{tag}_in", deps=deps)
    m_act, hs = _mix_fwd(z, mp, layer, name=f"{tag}_mix")
    h_out = _matmul(m_act, w_out, mode="nn", m=tp, n=d, kdim=d, tm=_tile(tp, 2112), tn=_tile(d, 512, LANES), tk=d,
                    out_dtype=F32, residual=h, b_layer=0, name=f"{tag}_out")
    return h_out, (h, n_act, z, hs, m_act)


def _mix_block_bwd(dh, dhb, saved, norm, layer, w_in, w_out, mp, tag, emit, deps):
    h, n_act, z, hs, m_act = saved
    tp, d = h.shape
    d_in = w_in.shape[2]
    dm = _matmul(dhb, w_out, mode="nt", m=tp, n=d, kdim=d, tm=_tile(tp, 2112), tn=_tile(d, 512, LANES), tk=d,
                 out_dtype=F32, b_layer=0, name=f"{tag}_dm", deps=deps)
    tok = emit("w_out", _matmul(m_act, dhb, mode="tn", m=d, n=d, kdim=tp, tm=_tile(d, 512, LANES), tn=_tile(d, 1024, LANES), tk=tp,
                                out_dtype=BF16, name=f"{tag}_dwout"))
    dz, g_mix = _mix_bwd(z, hs, dm, mp, layer, name=f"{tag}_bmix")
    tok = emit("w_in", _matmul(n_act, dz, mode="tn", m=d, n=d_in, kdim=tp, tm=_tile(d, 512, LANES), tn=_tile(d_in, 512, LANES),
                               tk=tp, out_dtype=BF16, name=f"{tag}_dwin", deps=tok))
    dn = _matmul(dz, w_in, mode="nt", m=tp, n=d, kdim=d_in, tm=_tile(tp, 1056), tn=_tile(d, 1024, LANES), tk=d_in,
                 out_dtype=F32, b_layer=0, name=f"{tag}_dn", deps=tok)
    dh_in, dhb_in, g_norm = _rms_bwd(h, norm, layer, dn, dh, name=f"{tag}_brms", deps=emit("flush", dn))
    return dh_in, dhb_in, g_norm, g_mix


def _mesh_pos():
    x, y, c = lax.axis_index("x"), lax.axis_index("y"), lax.axis_index("c")
    chips = [(1 - x, y), (x, 1 - y), (1 - x, 1 - y)]
    return x, y, c, chips


def _block(ref, axis, j, size):
    idx = [slice(None)] * len(ref.shape)
    idx[axis] = pl.ds(j * size, size)
    return ref.at[tuple(idx)]


def _all_gather(shards, axes, *, name):
    n = len(shards)
    sizes = [s.shape[ax] for s, ax in zip(shards, axes)]
    full_shapes = [s.shape[:ax] + (N_DEV * s.shape[ax],) + s.shape[ax + 1:] for s, ax in zip(shards, axes)]

    def body(*refs):
        srcs, fulls = refs[:n], refs[n:2 * n]
        send_sems, recv_sems, local_sems = refs[2 * n:]
        x, y, c, chips = _mesh_pos()
        me, sib = (x, y, c), (x, y, 1 - c)

        def blk(a, dev):
            return _block(fulls[a], axes[a], 4 * dev[0] + 2 * dev[1] + dev[2], sizes[a])

        def copy(a, k, block_dev, to, src=None):
            return pltpu.make_async_remote_copy(
                src_ref=blk(a, block_dev) if src is None else src, dst_ref=blk(a, block_dev),
                send_sem=send_sems.at[a, k], recv_sem=recv_sems.at[a, k], device_id=to, device_id_type=MESH)

        mine = [pltpu.make_async_copy(srcs[a], blk(a, me), local_sems.at[a]) for a in range(n)]
        for cp in mine:
            cp.start()
        first = []
        for a in range(n):
            first.append(copy(a, 0, me, sib, src=srcs[a]))
            first += [copy(a, 1 + j, me, (*chip, c), src=srcs[a]) for j, chip in enumerate(chips)]
        for cp in first:
            cp.start()
        passed = []
        for j, chip in enumerate(chips):
            for a in range(n):
                copy(a, 1 + j, (*chip, c), me).wait_recv()
                fwd = copy(a, 4 + j, (*chip, c), sib)
                fwd.start()
                passed.append(fwd)
        for a in range(n):
            copy(a, 0, sib, me).wait_recv()
            for j, chip in enumerate(chips):
                copy(a, 4 + j, (*chip, 1 - c), me).wait_recv()
        for cp in first + passed:
            cp.wait_send()
        for cp in mine:
            cp.wait()

    any_spec = pl.BlockSpec(memory_space=pl.ANY)
    return pl.pallas_call(
        body, name=name,
        out_shape=[jax.ShapeDtypeStruct(fs, s.dtype) for fs, s in zip(full_shapes, shards)],
        in_specs=[any_spec] * n, out_specs=[any_spec] * n,
        scratch_shapes=[pltpu.SemaphoreType.DMA((n, 7)), pltpu.SemaphoreType.DMA((n, 7)), pltpu.SemaphoreType.DMA((n,))],
    )(*shards)


_HBM = pl.BlockSpec(memory_space=pltpu.HBM)
_SEM = pl.BlockSpec(memory_space=pltpu.SEMAPHORE)
_ANY = pl.BlockSpec(memory_space=pl.ANY)
_EFFECT = pltpu.SideEffectType.DATAFLOW_SIDE_EFFECTING
_N_OUT = 4


def _in_hbm(a):
    return pltpu.with_memory_space_constraint(a, pltpu.HBM)


def _gather_start(shards, axes, deps, *, name):
    n = len(shards)
    sizes = [s.shape[ax] for s, ax in zip(shards, axes)]
    full_shapes = [s.shape[:ax] + (N_DEV * s.shape[ax],) + s.shape[ax + 1:] for s, ax in zip(shards, axes)]

    def body(*refs):
        srcs, lands = refs[:n], refs[n:2 * n]
        send_sems, recv_sems = refs[2 * n + len(deps)], refs[2 * n + len(deps) + 1]
        token = refs[-1]
        x, y, c, chips = _mesh_pos()
        targets = [(x, y, 1 - c)] + [(*chip, c) for chip in chips]
        for a in range(n):
            dst = _block(lands[a], axes[a], 4 * x + 2 * y + c, sizes[a])
            for k, to in enumerate(targets):
                pltpu.make_async_remote_copy(src_ref=srcs[a], dst_ref=dst, send_sem=send_sems.at[_N_OUT * a + k],
                                             recv_sem=recv_sems.at[_N_OUT * a + k],
                                             device_id=to, device_id_type=MESH).start()
        token[...] = jnp.zeros(token.shape, F32)

    lands0 = [lax.empty(fs, s.dtype) for fs, s in zip(full_shapes, shards)]
    outs = pl.pallas_call(
        body, name=name,
        out_shape=(pltpu.SemaphoreType.DMA((n * _N_OUT,)), pltpu.SemaphoreType.DMA((n * _N_OUT,)),
                   *[pltpu.HBM(s.shape, s.dtype) for s in shards], *[pltpu.HBM(fs, s.dtype) for fs, s in zip(full_shapes, shards)],
                   jax.ShapeDtypeStruct((8, LANES), F32)),
        in_specs=[_HBM] * (2 * n) + [_ANY] * len(deps),
        out_specs=(_SEM, _SEM, *[_HBM] * (2 * n), pl.BlockSpec(memory_space=pltpu.VMEM)),
        input_output_aliases={a: 2 + a for a in range(2 * n)},
        compiler_params=pltpu.CompilerParams(has_side_effects=_EFFECT),
    )(*[_in_hbm(s) for s in shards], *[_in_hbm(z) for z in lands0], *deps)
    handle = dict(send_sems=outs[0], recv_sems=outs[1], srcs=list(outs[2:2 + n]), lands=list(outs[2 + n:2 + 2 * n]), axes=list(axes))
    return handle, outs[-1]


def _gather_wait(handle, after, *, name):
    srcs, lands, axes = handle["srcs"], handle["lands"], handle["axes"]
    n = len(srcs)
    sizes = [s.shape[ax] for s, ax in zip(srcs, axes)]

    def body(*refs):
        src_refs, land_refs = refs[:n], refs[n:2 * n]
        send_sems, recv_sems = refs[2 * n], refs[2 * n + 1]
        x, y, c, chips = _mesh_pos()
        peers = [(x, y, 1 - c)] + [(*chip, c) for chip in chips]
        for a in range(n):
            for k, dev in enumerate(peers):
                cp = pltpu.make_async_remote_copy(
                    src_ref=src_refs[a], dst_ref=_block(land_refs[a], axes[a], 4 * dev[0] + 2 * dev[1] + dev[2], sizes[a]),
                    send_sem=send_sems.at[_N_OUT * a + k], recv_sem=recv_sems.at[_N_OUT * a + k], device_id=dev,
                    device_id_type=MESH)
                cp.wait_send()
                cp.wait_recv()

    outs = pl.pallas_call(
        body, name=name,
        out_shape=(*[pltpu.HBM(s.shape, s.dtype) for s in srcs], *[pltpu.HBM(z.shape, z.dtype) for z in lands]),
        in_specs=[_HBM] * (2 * n) + [_SEM, _SEM] + [_ANY] * len(after),
        out_specs=tuple([_HBM] * (2 * n)),
        input_output_aliases={a: a for a in range(2 * n)},
        compiler_params=pltpu.CompilerParams(has_side_effects=_EFFECT),
    )(*srcs, *lands, handle["send_sems"], handle["recv_sems"], *after)
    return list(outs[:n]), list(outs[n:])


def _place_own(shard, full, axis, dev, *, name):
    nd = shard.ndim
    rows, cols = shard.shape[-2:]
    tr = _tile(rows, 512, _row_mult(shard.dtype)) if rows % _row_mult(shard.dtype) == 0 else rows
    nrb = rows // tr
    block = shard.shape[:-2] + (tr, cols)

    def in_map(r, dev_ref):
        return (0,) * (nd - 2) + (r, 0)

    def out_map(r, dev_ref):
        idx = [0] * nd
        idx[nd - 2] = r
        idx[axis] = dev_ref[0] * (nrb if axis == nd - 2 else 1) + idx[axis]
        return tuple(idx)

    def body(dev_ref, src_ref, full_ref, out_ref):
        out_ref[...] = src_ref[...]

    grid_spec = pltpu.PrefetchScalarGridSpec(
        num_scalar_prefetch=1, grid=(nrb,),
        in_specs=[pl.BlockSpec(block, in_map), _ANY], out_specs=pl.BlockSpec(block, out_map))
    return pl.pallas_call(body, name=name, grid_spec=grid_spec, out_shape=jax.ShapeDtypeStruct(full.shape, full.dtype),
                          input_output_aliases={2: 0}, compiler_params=_params(("arbitrary",)))(dev, shard, full)


def _gather_pass_start(fulls, axes, *, name):
    n = len(fulls)
    sizes = [z.shape[ax] // N_DEV for z, ax in zip(fulls, axes)]

    def body(*refs):
        srcs = refs[:n]
        send_sems, recv_sems = refs[n], refs[n + 1]
        token = refs[-1]
        x, y, c, chips = _mesh_pos()
        for a in range(n):
            for j, chip in enumerate(chips):
                blk = _block(srcs[a], axes[a], 4 * chip[0] + 2 * chip[1] + c, sizes[a])
                pltpu.make_async_remote_copy(src_ref=blk, dst_ref=blk, send_sem=send_sems.at[3 * a + j], recv_sem=recv_sems.at[3 * a + j],
                                             device_id=(x, y, 1 - c), device_id_type=MESH).start()
        token[...] = jnp.zeros(token.shape, F32)

    outs = pl.pallas_call(
        body, name=name,
        out_shape=(pltpu.SemaphoreType.DMA((3 * n,)), pltpu.SemaphoreType.DMA((3 * n,)), *[pltpu.HBM(z.shape, z.dtype) for z in fulls],
                   jax.ShapeDtypeStruct((8, LANES), F32)),
        in_specs=[_HBM] * n, out_specs=(_SEM, _SEM, *[_HBM] * n, pl.BlockSpec(memory_space=pltpu.VMEM)),
        input_output_aliases={a: 2 + a for a in range(n)},
        compiler_params=pltpu.CompilerParams(has_side_effects=_EFFECT),
    )(*[_in_hbm(z) for z in fulls])
    return dict(send_sems=outs[0], recv_sems=outs[1], fulls=list(outs[2:2 + n]), axes=list(axes)), outs[-1]


def _gather_pass_wait(handle, after, *, name):
    fulls, axes = handle["fulls"], handle["axes"]
    n = len(fulls)
    sizes = [z.shape[ax] // N_DEV for z, ax in zip(fulls, axes)]

    def body(*refs):
        bufs = refs[:n]
        send_sems, recv_sems = refs[n], refs[n + 1]
        x, y, c, chips = _mesh_pos()
        for a in range(n):
            for j, chip in enumerate(chips):
                cp = pltpu.make_async_remote_copy(
                    src_ref=_block(bufs[a], axes[a], 4 * chip[0] + 2 * chip[1] + c, sizes[a]),
                    dst_ref=_block(bufs[a], axes[a], 4 * chip[0] + 2 * chip[1] + (1 - c), sizes[a]),
                    send_sem=send_sems.at[3 * a + j], recv_sem=recv_sems.at[3 * a + j], device_id=(x, y, 1 - c),
                    device_id_type=MESH)
                cp.wait_send()
                cp.wait_recv()

    outs = pl.pallas_call(
        body, name=name,
        out_shape=tuple(pltpu.HBM(z.shape, z.dtype) for z in fulls),
        in_specs=[_HBM] * n + [_SEM, _SEM] + [_ANY] * len(after), out_specs=tuple([_HBM] * n),
        input_output_aliases={a: a for a in range(n)},
        compiler_params=pltpu.CompilerParams(has_side_effects=_EFFECT),
    )(*fulls, handle["send_sems"], handle["recv_sems"], *after)
    return list(outs)


def _rs_exchange_sibling(tensors, axes, *, name):
    n = len(tensors)
    n_layers = [len(t) for t in tensors]
    sizes = [t[0].shape[ax] // N_DEV for t, ax in zip(tensors, axes)]
    blk_shapes = [t[0].shape[:ax] + (sz,) + t[0].shape[ax + 1:] for t, ax, sz in zip(tensors, axes, sizes)]
    flat = [g for t in tensors for g in t]
    offs = [sum(n_layers[:a]) for a in range(n)]

    def body(*refs):
        srcs = refs[:len(flat)]
        lands = refs[len(flat):len(flat) + n]
        send_sems, recv_sems = refs[len(flat) + n:]
        x, y, c, _ = _mesh_pos()
        sib = (x, y, 1 - c)
        for a in range(n):
            for l in range(n_layers[a]):
                for i in range(N_CHIP):
                    pltpu.make_async_remote_copy(
                        src_ref=_block(srcs[offs[a] + l], axes[a], 2 * i + (1 - c), sizes[a]), dst_ref=lands[a].at[i, l],
                        send_sem=send_sems.at[a], recv_sem=recv_sems.at[a], device_id=sib, device_id_type=MESH).start()
        for a in range(n):
            pltpu.make_async_remote_copy(src_ref=lands[a], dst_ref=lands[a], send_sem=send_sems.at[a], recv_sem=recv_sems.at[a],
                                         device_id=sib, device_id_type=MESH).wait()

    any_spec = pl.BlockSpec(memory_space=pl.ANY)
    return pl.pallas_call(
        body, name=name,
        out_shape=[jax.ShapeDtypeStruct((N_CHIP, nl) + bs, t[0].dtype) for nl, bs, t in zip(n_layers, blk_shapes, tensors)],
        in_specs=[any_spec] * len(flat), out_specs=[any_spec] * n,
        scratch_shapes=[pltpu.SemaphoreType.DMA((n,)), pltpu.SemaphoreType.DMA((n,))],
    )(*flat)


def _rs_add_sibling(layers, axis, land, my_c, *, name):
    n_layers = len(layers)
    shape = layers[0].shape
    size = shape[axis] // N_DEV
    blk_shape = shape[:axis] + (size,) + shape[axis + 1:]
    nd = len(shape)
    rows = blk_shape[-2]
    tr = _tile(rows, 512, _row_mult(layers[0].dtype))
    inner = (tr, blk_shape[-1])
    lead = blk_shape[:-2]
    if lead:
        raise ValueError("blocked gradients are 2-D")
    nrb = rows // tr

    def src_map(l):
        def imap(i, r, c_ref):
            j = 2 * i + c_ref[0]
            return (j * nrb + r, 0) if axis == 0 else (r, j)
        return imap

    def body(c_ref, *refs):
        srcs = refs[:n_layers]
        land_ref = refs[n_layers]
        out_ref = refs[n_layers + 1]
        for l in range(n_layers):
            out_ref[l] = (srcs[l][...].astype(F32) + land_ref[l].astype(F32)).astype(out_ref.dtype)

    grid_spec = pltpu.PrefetchScalarGridSpec(
        num_scalar_prefetch=1, grid=(N_CHIP, nrb),
        in_specs=[pl.BlockSpec(inner, src_map(l)) for l in range(n_layers)]
        + [pl.BlockSpec((None, n_layers) + inner, lambda i, r, c_ref: (i, 0, r, 0))],
        out_specs=pl.BlockSpec((None, n_layers) + inner, lambda i, r, c_ref: (i, 0, r, 0)),
    )
    return pl.pallas_call(
        body, name=name, grid_spec=grid_spec,
        out_shape=jax.ShapeDtypeStruct((N_CHIP, n_layers) + blk_shape, layers[0].dtype),
        compiler_params=_params(("arbitrary", "arbitrary")),
    )(my_c, *layers, land)


def _rs_exchange_chips(parts, *, name):
    n = len(parts)

    def body(*refs):
        srcs, lands = refs[:n], refs[n:2 * n]
        send_sems, recv_sems, local_sems = refs[2 * n:]
        x, y, c, chips = _mesh_pos()
        mine = 2 * x + y
        local = [pltpu.make_async_copy(srcs[a].at[mine], lands[a].at[mine], local_sems.at[a]) for a in range(n)]
        for cp in local:
            cp.start()

        def copy(a, j, chip):
            return pltpu.make_async_remote_copy(
                src_ref=srcs[a].at[2 * chip[0] + chip[1]], dst_ref=lands[a].at[mine],
                send_sem=send_sems.at[a, j], recv_sem=recv_sems.at[a, j], device_id=(*chip, c), device_id_type=MESH)

        sends = [copy(a, j, chip) for a in range(n) for j, chip in enumerate(chips)]
        for cp in sends:
            cp.start()
        for a in range(n):
            for j, chip in enumerate(chips):
                pltpu.make_async_remote_copy(
                    src_ref=srcs[a].at[mine], dst_ref=lands[a].at[2 * chip[0] + chip[1]],
                    send_sem=send_sems.at[a, j], recv_sem=recv_sems.at[a, j], device_id=(*chip, c), device_id_type=MESH).wait_recv()
        for cp in sends:
            cp.wait_send()
        for cp in local:
            cp.wait()

    any_spec = pl.BlockSpec(memory_space=pl.ANY)
    return pl.pallas_call(
        body, name=name,
        out_shape=[jax.ShapeDtypeStruct(p.shape, p.dtype) for p in parts],
        in_specs=[any_spec] * n, out_specs=[any_spec] * n,
        scratch_shapes=[pltpu.SemaphoreType.DMA((n, 3)), pltpu.SemaphoreType.DMA((n, 3)), pltpu.SemaphoreType.DMA((n,))],
    )(*parts)


def _rs_sibling_start(g, axis, *, name):
    size = g.shape[axis] // N_DEV
    land = lax.empty((N_CHIP, 1) + g.shape[:axis] + (size,) + g.shape[axis + 1:], g.dtype)

    def body(src, dst, send_sems, recv_sems, src_thru, dst_thru, token):
        x, y, c, _ = _mesh_pos()
        for i in range(N_CHIP):
            pltpu.make_async_remote_copy(src_ref=_block(src, axis, 2 * i + (1 - c), size), dst_ref=dst.at[i, 0], send_sem=send_sems.at[i],
                                         recv_sem=recv_sems.at[i], device_id=(x, y, 1 - c), device_id_type=MESH).start()
        token[...] = jnp.zeros(token.shape, F32)

    outs = pl.pallas_call(
        body, name=name,
        out_shape=(pltpu.SemaphoreType.DMA((N_CHIP,)), pltpu.SemaphoreType.DMA((N_CHIP,)), pltpu.HBM(g.shape, g.dtype),
                   pltpu.HBM(land.shape, land.dtype), jax.ShapeDtypeStruct((8, LANES), F32)),
        in_specs=[_HBM, _HBM], out_specs=(_SEM, _SEM, _HBM, _HBM, pl.BlockSpec(memory_space=pltpu.VMEM)),
        input_output_aliases={0: 2, 1: 3},
        compiler_params=pltpu.CompilerParams(has_side_effects=_EFFECT),
    )(_in_hbm(g), _in_hbm(land))
    return dict(send_sems=outs[0], recv_sems=outs[1], src=outs[2], land=outs[3], axis=axis), outs[4]


def _rs_sibling_wait(handle, after, *, name):
    src, land, axis = handle["src"], handle["land"], handle["axis"]
    size = src.shape[axis] // N_DEV

    def body(src_ref, dst_ref, send_sems, recv_sems, *rest):
        x, y, c, _ = _mesh_pos()
        for i in range(N_CHIP):
            cp = pltpu.make_async_remote_copy(src_ref=_block(src_ref, axis, 2 * i + (1 - c), size), dst_ref=dst_ref.at[i, 0],
                                              send_sem=send_sems.at[i], recv_sem=recv_sems.at[i], device_id=(x, y, 1 - c),
                                              device_id_type=MESH)
            cp.wait_send()
            cp.wait_recv()

    outs = pl.pallas_call(
        body, name=name,
        out_shape=(pltpu.HBM(src.shape, src.dtype), pltpu.HBM(land.shape, land.dtype)),
        in_specs=[_HBM, _HBM, _SEM, _SEM] + [_ANY] * len(after), out_specs=(_HBM, _HBM),
        input_output_aliases={0: 0, 1: 1},
        compiler_params=pltpu.CompilerParams(has_side_effects=_EFFECT),
    )(src, land, handle["send_sems"], handle["recv_sems"], *after)
    return outs[0], outs[1]


def _rs_chips_start(part, *, name):
    land = lax.empty(part.shape, part.dtype)

    def body(src, dst, send_sems, recv_sems, src_thru, dst_thru, token):
        x, y, c, chips = _mesh_pos()
        for j, chip in enumerate(chips):
            pltpu.make_async_remote_copy(src_ref=src.at[2 * chip[0] + chip[1]], dst_ref=dst.at[2 * x + y], send_sem=send_sems.at[j],
                                         recv_sem=recv_sems.at[j], device_id=(*chip, c), device_id_type=MESH).start()
        token[...] = jnp.zeros(token.shape, F32)

    outs = pl.pallas_call(
        body, name=name,
        out_shape=(pltpu.SemaphoreType.DMA((3,)), pltpu.SemaphoreType.DMA((3,)), pltpu.HBM(part.shape, part.dtype),
                   pltpu.HBM(land.shape, land.dtype), jax.ShapeDtypeStruct((8, LANES), F32)),
        in_specs=[_HBM, _HBM], out_specs=(_SEM, _SEM, _HBM, _HBM, pl.BlockSpec(memory_space=pltpu.VMEM)),
        input_output_aliases={0: 2, 1: 3},
        compiler_params=pltpu.CompilerParams(has_side_effects=_EFFECT),
    )(_in_hbm(part), _in_hbm(land))
    return dict(send_sems=outs[0], recv_sems=outs[1], src=outs[2], land=outs[3]), outs[4]


def _rs_chips_wait(handle, after, *, name):
    def body(src, dst, send_sems, recv_sems, *rest):
        x, y, c, chips = _mesh_pos()
        for j, chip in enumerate(chips):
            cp = pltpu.make_async_remote_copy(src_ref=src.at[2 * chip[0] + chip[1]], dst_ref=dst.at[2 * chip[0] + chip[1]],
                                              send_sem=send_sems.at[j], recv_sem=recv_sems.at[j], device_id=(*chip, c),
                                              device_id_type=MESH)
            cp.wait_send()
            cp.wait_recv()

    src, land = handle["src"], handle["land"]
    outs = pl.pallas_call(
        body, name=name,
        out_shape=(pltpu.HBM(src.shape, src.dtype), pltpu.HBM(land.shape, land.dtype)),
        in_specs=[_HBM, _HBM, _SEM, _SEM] + [_ANY] * len(after), out_specs=(_HBM, _HBM),
        input_output_aliases={0: 0, 1: 1},
        compiler_params=pltpu.CompilerParams(has_side_effects=_EFFECT),
    )(src, land, handle["send_sems"], handle["recv_sems"], *after)
    return outs[0], outs[1]


def _adam_math(w, g, m, v):
    m_new = ADAM_B1 * m + (1.0 - ADAM_B1) * g
    v_new = ADAM_B2 * v + (1.0 - ADAM_B2) * (g * g)
    m_hat = m_new / (1.0 - ADAM_B1 ** ADAM_STEP)
    v_hat = v_new / (1.0 - ADAM_B2 ** ADAM_STEP)
    delta = -ADAM_LR * (m_hat / (jnp.sqrt(v_hat) + ADAM_EPS) + ADAM_WD * w)
    return delta, m_new, v_new


def _sum_adam(part, land, slots, w, m, v, layer, prev, *, name):
    n_layers, rows, cols = w.shape
    tr = _tile(rows, 256, _row_mult(land.dtype))
    n_prev = 0 if prev is None else 4

    def body(slots_ref, p0_ref, p1_ref, p2_ref, p3_ref, w_ref, m_ref, v_ref, *rest):
        g_ref, d_ref, mo_ref, vo_ref = rest[n_prev:]
        g = p0_ref[...].astype(F32)
        for ref in (p1_ref, p2_ref, p3_ref):
            g = g + ref[...].astype(F32)
        delta, m_new, v_new = _adam_math(w_ref[...], g, m_ref[...], v_ref[...])
        g_ref[...] = g
        d_ref[...] = delta
        mo_ref[...] = m_new
        vo_ref[...] = v_new

    blk = pl.BlockSpec((None, tr, cols), lambda r, s: (layer, r, 0))
    slot = lambda j: pl.BlockSpec((None, None, tr, cols), lambda r, s: (s[j], 0, r, 0))
    shp = jax.ShapeDtypeStruct((n_layers, rows, cols), F32)
    grid_spec = pltpu.PrefetchScalarGridSpec(
        num_scalar_prefetch=1, grid=(rows // tr,),
        in_specs=[slot(0), slot(1), slot(2), slot(3), blk, blk, blk] + [_ANY] * n_prev,
        out_specs=(blk, blk, blk, blk))
    return pl.pallas_call(
        body, name=name, out_shape=(shp, shp, shp, shp), grid_spec=grid_spec,
        input_output_aliases={8 + i: i for i in range(n_prev)},
        compiler_params=_params(("parallel",)),
    )(slots, part, land, land, land, w, m, v, *(prev or ()))


def _sum_slots(land, *, name):
    _, _, rows, cols = land.shape
    tr = _tile(rows, 256, 8)

    def body(land_ref, g_ref):
        g = land_ref[0]
        for i in range(1, N_CHIP):
            g = g + land_ref[i]
        g_ref[...] = g

    return pl.pallas_call(
        body, name=name, out_shape=jax.ShapeDtypeStruct((rows, cols), F32), grid=(rows // tr,),
        in_specs=[pl.BlockSpec((N_CHIP, None, tr, cols), lambda r: (0, 0, r, 0))],
        out_specs=pl.BlockSpec((tr, cols), lambda r: (r, 0)),
        compiler_params=_params(("parallel",)),
    )(land)


def _adam_flat(w, g, m, v, *, name):
    rows, cols = w.shape
    tr = _tile(rows, 256, 8)

    def body(w_ref, g_ref, m_ref, v_ref, d_ref, mo_ref, vo_ref):
        delta, m_new, v_new = _adam_math(w_ref[...], g_ref[...], m_ref[...], v_ref[...])
        d_ref[...] = delta
        mo_ref[...] = m_new
        vo_ref[...] = v_new

    blk = pl.BlockSpec((tr, cols), lambda r: (r, 0))
    shp = jax.ShapeDtypeStruct((rows, cols), F32)
    return pl.pallas_call(
        body, name=name, out_shape=(shp, shp, shp), grid=(rows // tr,), in_specs=[blk] * 4, out_specs=(blk, blk, blk),
        compiler_params=_params(("parallel",)),
    )(w, g, m, v)


_BIG = (("ffn1_w_in", 1), ("ffn1_w_out", 0), ("w_in", 1), ("w_out", 0), ("ffn2_w_in", 1), ("ffn2_w_out", 0))
_SMALL = ("meta_tokens", "ffn1_norm", "mix_norm", "conv_w", "conv_b", "lru_wa", "lru_ba", "lru_wx", "lru_bx", "lru_a_param",
          "pool_w", "pool_b", "pool_scale", "ffn2_norm", "final_norm")
_SMALL_SHARD_AXIS = {"meta_tokens": 1, "conv_w": 2, "pool_w": 2}
_PACK_COLS = 1024


def _pack(arrs):
    flat = jnp.concatenate([a.reshape(-1) for a in arrs])
    unit = N_DEV * 8 * _PACK_COLS
    total = -(-flat.shape[0] // unit) * unit
    flat = jnp.pad(flat, (0, total - flat.shape[0]))
    return flat.reshape(total // _PACK_COLS, _PACK_COLS)


def _unpack(packed, shapes):
    flat = packed.reshape(-1)
    out, off = [], 0
    for s in shapes:
        size = 1
        for v in s:
            size *= v
        out.append(flat[off:off + size].reshape(s))
        off += size
    return out


def _my_shard(full, axis, dev):
    size = full.shape[axis] // N_DEV
    return lax.dynamic_slice_in_dim(full, dev * size, size, axis)


def kernel(x, meta_tokens, ffn1_norm, ffn1_w_in, ffn1_w_out, mix_norm, w_in, conv_w, conv_b, lru_wa, lru_ba, lru_wx, lru_bx, lru_a_param, pool_w, pool_b, pool_scale, w_out, ffn2_norm, ffn2_w_in, ffn2_w_out, final_norm, loss_target, m_meta_tokens, m_ffn1_norm, m_ffn1_w_in, m_ffn1_w_out, m_mix_norm, m_w_in, m_conv_w, m_conv_b, m_lru_wa, m_lru_ba, m_lru_wx, m_lru_bx, m_lru_a_param, m_pool_w, m_pool_b, m_pool_scale, m_w_out, m_ffn2_norm, m_ffn2_w_in, m_ffn2_w_out, m_final_norm, v_meta_tokens, v_ffn1_norm, v_ffn1_w_in, v_ffn1_w_out, v_mix_norm, v_w_in, v_conv_w, v_conv_b, v_lru_wa, v_lru_ba, v_lru_wx, v_lru_bx, v_lru_a_param, v_pool_w, v_pool_b, v_pool_scale, v_w_out, v_ffn2_norm, v_ffn2_w_in, v_ffn2_w_out, v_final_norm):
    names = ("meta_tokens", "ffn1_norm", "ffn1_w_in", "ffn1_w_out", "mix_norm", "w_in", "conv_w", "conv_b", "lru_wa", "lru_ba",
             "lru_wx", "lru_bx", "lru_a_param", "pool_w", "pool_b", "pool_scale", "w_out", "ffn2_norm", "ffn2_w_in", "ffn2_w_out",
             "final_norm")
    w = dict(zip(names, (meta_tokens, ffn1_norm, ffn1_w_in, ffn1_w_out, mix_norm, w_in, conv_w, conv_b, lru_wa, lru_ba, lru_wx,
                         lru_bx, lru_a_param, pool_w, pool_b, pool_scale, w_out, ffn2_norm, ffn2_w_in, ffn2_w_out, final_norm)))
    mom = dict(zip(names, (m_meta_tokens, m_ffn1_norm, m_ffn1_w_in, m_ffn1_w_out, m_mix_norm, m_w_in, m_conv_w, m_conv_b, m_lru_wa,
                           m_lru_ba, m_lru_wx, m_lru_bx, m_lru_a_param, m_pool_w, m_pool_b, m_pool_scale, m_w_out, m_ffn2_norm,
                           m_ffn2_w_in, m_ffn2_w_out, m_final_norm)))
    vel = dict(zip(names, (v_meta_tokens, v_ffn1_norm, v_ffn1_w_in, v_ffn1_w_out, v_mix_norm, v_w_in, v_conv_w, v_conv_b, v_lru_wa,
                           v_lru_ba, v_lru_wx, v_lru_bx, v_lru_a_param, v_pool_w, v_pool_b, v_pool_scale, v_w_out, v_ffn2_norm,
                           v_ffn2_w_in, v_ffn2_w_out, v_final_norm)))
    n_layers, d = ffn1_norm.shape
    n_meta = meta_tokens.shape[0]
    seq = x.shape[1]
    t_valid = n_meta + seq
    tp = -(-t_valid // ROW_ALIGN) * ROW_ALIGN
    dev = 4 * lax.axis_index("x") + 2 * lax.axis_index("y") + lax.axis_index("c")
    my_c = lax.axis_index("c").astype(jnp.int32).reshape(1)
    dev1 = dev.astype(jnp.int32).reshape(1)
    mx, my = lax.axis_index("x"), lax.axis_index("y")
    slots = jnp.stack([2 * mx + my, 2 * (1 - mx) + my, 2 * mx + (1 - my), 2 * (1 - mx) + (1 - my)]).astype(jnp.int32)

    big_axis = dict(_BIG)
    vec = lambda a: a.reshape(a.shape[0], 1, a.shape[1])
    norms = dict(ffn1=vec(ffn1_norm), mix=vec(mix_norm), ffn2=vec(ffn2_norm))
    units = [(kind, l) for l in range(n_layers) for kind in ("ffn1", "mix", "ffn2")]

    handles = {}
    state = dict(token=None, started=0, first=None)

    def behind(a):
        return a if state["first"] is None else lax.optimization_barrier((state["first"], a))[1]

    def shard(k, l):
        return behind(w[k][l:l + 1]).astype(BF16), big_axis[k] + 1

    groups = [[shard("ffn1_w_in", 0), (meta_tokens, 1)], [shard("ffn1_w_out", 0)]]

    def order():
        return [] if state["token"] is None else [state["token"]]

    def start_next():
        i = state["started"]
        if i < len(groups):
            handles[i], state["token"] = _gather_start([s for s, _ in groups[i]], [ax for _, ax in groups[i]], order(),
                                                       name=f"gather{i}_start")
            state["started"] = i + 1

    def finish_begin(i, after):
        handle = handles.pop(i)
        srcs, fulls = _gather_wait(handle, list(after) + order(), name=f"gather{i}_wait")
        fulls = [_place_own(s, z, ax, dev1, name=f"gather{i}_own{a}") for a, (s, z, ax) in enumerate(zip(srcs, fulls, handle["axes"]))]
        passing, state["token"] = _gather_pass_start(fulls, handle["axes"], name=f"gather{i}_pass")
        return i, passing

    def finish_end(ctx, after):
        i, passing = ctx
        fulls = _gather_pass_wait(passing, after, name=f"gather{i}_passwait")
        start_next()
        return fulls

    start_next()
    start_next()
    state["first"] = state["token"]
    for kind, l in units[1:]:
        if kind == "mix":
            groups.append([shard("w_in", l), shard("w_out", l)] + ([(behind(conv_w), 2), (behind(pool_w), 2)] if l == 0 else []))
        else:
            groups.append([shard(f"{kind}_w_in", l), shard(f"{kind}_w_out", l)])
    pad = jnp.zeros((tp - t_valid, d), F32)
    tgt = jnp.concatenate([jnp.zeros((n_meta, d), F32), behind(loss_target[0]), pad], axis=0)
    h0_base = jnp.concatenate([jnp.zeros((n_meta, d), F32), behind(x[0]), pad], axis=0)
    ctx = finish_begin(0, [tgt, h0_base] + [s for grp in groups[2:] for s, _ in grp])
    w_in0, meta_full = finish_end(ctx, [])
    h0 = lax.dynamic_update_slice(h0_base, meta_full, (0, 0))
    n_act = _rms_fwd(h0, norms["ffn1"], 0, name="l0ffn1_rms", deps=order())
    gu, act = _ffn_in_fwd(n_act, w_in0, 0, name="l0ffn1_in")
    (w_out0,) = finish_end(finish_begin(1, [act]), [])
    h = _ffn_fwd_out(h0, act, w_out0, "l0ffn1", order())
    gathered = {units[0]: (w_in0, w_out0)}
    saved = {units[0]: (h0, n_act, gu)}
    mp = None
    for ui, (kind, l) in enumerate(units[1:], start=1):
        tag = f"l{l}{kind}"
        ctx = finish_begin(ui + 1, [h])
        n_act = _rms_fwd(h, norms[kind], l, name=f"{tag}_rms", deps=order())
        fulls = finish_end(ctx, [n_act])
        if kind == "mix":
            if l == 0:
                mp = dict(conv_w=fulls[2], conv_b=vec(conv_b), wa=lru_wa.astype(BF16), ba=vec(lru_ba), wx=lru_wx.astype(BF16),
                          bx=vec(lru_bx), a_param=vec(lru_a_param), pool_w=fulls[3].astype(BF16), pool_b=vec(pool_b),
                          pool_scale=vec(pool_scale))
            h, saved[(kind, l)] = _mix_block_fwd(h, n_act, l, fulls[0], fulls[1], mp, tag, order())
        else:
            gu, act = _ffn_in_fwd(n_act, fulls[0], 0, name=f"{tag}_in", deps=order())
            saved[(kind, l)] = (h, n_act, gu)
            h = _ffn_fwd_out(h, act, fulls[1], tag, [])
        gathered[(kind, l)] = (fulls[0], fulls[1])
    dh, dhb, g_final, loss_local = _final_loss(h, final_norm.reshape(1, d), tgt, n_meta, t_valid, name="final_loss")
    loss = lax.psum(loss_local[0, 0], ("x", "y", "c"))

    pending = []
    big_out = {}

    def drain(after):
        while pending:
            k, l, handle = pending.pop(0)
            part, land = _rs_chips_wait(handle, after, name=f"rs_{k}{l}_wait")
            big_out[k] = _sum_adam(part, land, slots, w[k], mom[k], vel[k], l, big_out.get(k), name=f"adam_{k}{l}")

    sib_pending = []

    def to_chips(k, l, g, land):
        part = _rs_add_sibling([g], big_axis[k], land, my_c, name=f"rs_{k}{l}_add")
        handle, token = _rs_chips_start(part, name=f"rs_{k}{l}_start")
        pending.append((k, l, handle))
        return token

    def finish_sibling(after):
        token = None
        while sib_pending:
            k, l, handle = sib_pending.pop(0)
            g, land = _rs_sibling_wait(handle, after, name=f"rs_{k}{l}_sibwait")
            token = to_chips(k, l, g, land)
        return token

    def emitter(kind, l):
        last_unit = (kind, l) == units[0]

        def emit(which, g):
            if which == "w_out":
                drain([g])
            token = finish_sibling([g])
            order_after = [] if token is None else [token]
            if which == "flush":
                return order_after
            k = which if kind == "mix" else f"{kind}_{which}"
            if last_unit and which == "w_in":
                (land,) = _rs_exchange_sibling([[g]], [big_axis[k]], name=f"rs_{k}{l}_sib")
                return order_after + [to_chips(k, l, g, land)]
            handle, token = _rs_sibling_start(g, big_axis[k], name=f"rs_{k}{l}_sibstart")
            sib_pending.append((k, l, handle))
            return order_after + [token]
        return emit

    g_norm = {k: [None] * n_layers for k in ("ffn1", "mix", "ffn2")}
    g_mix = [None] * n_layers
    for kind, l in reversed(units):
        w_i, w_o = gathered[(kind, l)]
        tag = f"l{l}{kind}"
        if kind == "mix":
            dh, dhb, g_norm[kind][l], g_mix[l] = _mix_block_bwd(dh, dhb, saved[(kind, l)], norms[kind], l, w_i, w_o, mp, tag,
                                                                 emitter(kind, l), [])
        else:
            dh, dhb, g_norm[kind][l] = _ffn_bwd(dh, dhb, saved[(kind, l)], norms[kind], l, w_i, w_o, tag, emitter(kind, l), [])
    drain([dh])
    dh0 = dh
    grad_x = dh0[n_meta:t_valid][None]

    mix_key = dict(conv_w="conv_w", conv_b="conv_b", lru_wa="wa", lru_ba="ba", lru_wx="wx", lru_bx="bx", lru_a_param="a_param",
                   pool_w="pool_w", pool_b="pool_b", pool_scale="pool_scale")
    small_local = {"meta_tokens": dh0[:n_meta], "final_norm": g_final.reshape(d)}
    for k in ("ffn1", "mix", "ffn2"):
        small_local[f"{k}_norm"] = jnp.stack([g.reshape(d) for g in g_norm[k]])
    for k, mk in mix_key.items():
        small_local[k] = jnp.stack([g_mix[l][mk] for l in range(n_layers)]).reshape(
            w[k].shape if k not in _SMALL_SHARD_AXIS else small_shape_full(w[k], _SMALL_SHARD_AXIS[k]))
    small_shapes = [small_local[k].shape for k in _SMALL]
    packed = _pack([small_local[k] for k in _SMALL])
    (land,) = _rs_exchange_sibling([[packed]], [0], name="rs_small_sib")
    part = _rs_add_sibling([packed], 0, land, my_c, name="rs_small_add")
    (land2,) = _rs_exchange_chips([part], name="rs_small_chips")

    out_g, out_d, out_m, out_v = {}, {}, {}, {}
    for k, _ in _BIG:
        out_g[k], out_d[k], out_m[k], out_v[k] = big_out[k]
    small_block = _sum_slots(land2, name="rs_sum_small")
    small_full = _all_gather([small_block], [0], name="gather_small")[0]
    small_g = dict(zip(_SMALL, _unpack(small_full, small_shapes)))
    for k, ax in _SMALL_SHARD_AXIS.items():
        small_g[k] = _my_shard(small_g[k], ax, dev)
    shapes_local = [w[k].shape for k in _SMALL]
    d_p, m_p, v_p = _adam_flat(_pack([w[k] for k in _SMALL]), _pack([small_g[k] for k in _SMALL]),
                               _pack([mom[k] for k in _SMALL]), _pack([vel[k] for k in _SMALL]), name="adam_small")
    for k, dd, mm, vv in zip(_SMALL, _unpack(d_p, shapes_local), _unpack(m_p, shapes_local), _unpack(v_p, shapes_local)):
        out_g[k], out_d[k], out_m[k], out_v[k] = small_g[k], dd, mm, vv

    return (loss, grad_x, *[out_g[k] for k in names], *[out_d[k] for k in names], *[out_m[k] for k in names],
            *[out_v[k] for k in names])


def small_shape_full(w_shard, axis):
    return w_shard.shape[:axis] + (N_DEV * w_shard.shape[axis],) + w_shard.shape[axis + 1:]
```

```python
import functools

import jax
import jax.numpy as jnp
from jax import lax
from jax.experimental import pallas as pl
from jax.experimental.pallas import tpu as pltpu

F32 = jnp.float32
BF16 = jnp.bfloat16
MESH = pl.DeviceIdType.MESH

N_DEV = 8
N_CHIP = 4
RMS_EPS = 1e-6
LRU_C = 8.0
CONV_WIDTH = 4
POOL_WINDOWS = (2, 4, 8, 16)
HIST = 16
ADAM_LR = 0.001
ADAM_B1 = 0.9
ADAM_B2 = 0.999
ADAM_EPS = 1e-08
ADAM_WD = 0.01
ADAM_STEP = 10
ROW_ALIGN = 128
V7X_VMEM_LIMIT = 56 * 1024 * 1024
BF16_ROWS = 16
K_TILE = 5632
WGRAD_TILE = 2048
SCAN_UNROLL = 8
EPILOGUE_ROWS = 528
LANES = 128


def _tile(n, cap, mult=BF16_ROWS):
    best = None
    d = mult
    while d <= min(n, cap):
        if n % d == 0:
            best = d
        d += mult
    if best is None:
        raise ValueError(f"no tile for {n} (cap {cap}, multiple of {mult})")
    return best


def _row_mult(dtype):
    return 8 * 4 // jnp.dtype(dtype).itemsize


def _params(sem=None):
    return pltpu.CompilerParams(dimension_semantics=sem, vmem_limit_bytes=V7X_VMEM_LIMIT)


def _dot(a, b, mode):
    dims = {"nn": ((1,), (0,)), "nt": ((1,), (1,)), "tn": ((0,), (0,))}[mode]
    return lax.dot_general(a, b, (dims, ((), ())), preferred_element_type=F32)


def _sigmoid(x):
    return 1.0 / (1.0 + jnp.exp(-x))


_GELU_C = 0.7978845608028654
_GELU_K = 0.044715


def _gelu(x):
    return 0.5 * x * (1.0 + jnp.tanh(_GELU_C * (x + _GELU_K * x * x * x)))


def _gelu_and_grad(x):
    x2 = x * x
    th = jnp.tanh(_GELU_C * x * (1.0 + _GELU_K * x2))
    half = 0.5 * (1.0 + th)
    return x * half, half + 0.5 * x * (1.0 - th * th) * _GELU_C * (1.0 + 3.0 * _GELU_K * x2)


def _neg_expm1(x):
    p = 1.0 / 5040.0
    for c in (1.0 / 720.0, 1.0 / 120.0, 1.0 / 24.0, 1.0 / 6.0, 0.5, 1.0):
        p = p * x + c
    return jnp.where(x > -0.25, -(x * p), 1.0 - jnp.exp(x))


def _softplus_neg(p):
    e = jnp.exp(-jnp.abs(p))
    u = 1.0 + e
    l1p = jnp.where(u == 1.0, e, jnp.log(u) * e / (u - 1.0 + (u == 1.0).astype(F32)))
    return jnp.maximum(-p, 0.0) + l1p


def _operand_spec(arr, br, bc, ridx, cidx, layer, split):
    if split:
        ncb = arr.shape[2] // bc
        return pl.BlockSpec((None, br, bc), lambda i, j, k: (cidx(i, j, k) // ncb, ridx(i, j, k), cidx(i, j, k) % ncb))
    if layer is not None:
        return pl.BlockSpec((None, br, bc), lambda i, j, k: (layer, ridx(i, j, k), cidx(i, j, k)))
    return pl.BlockSpec((br, bc), lambda i, j, k: (ridx(i, j, k), cidx(i, j, k)))


def _matmul(a, b, *, mode, m, n, kdim, tm, tn, tk, out_dtype, name, scale=None, residual=None,
            a_layer=None, b_layer=None, a_split=False, b_split=False, deps=()):
    nk = kdim // tk
    I = lambda i, j, k: i
    J = lambda i, j, k: j
    K = lambda i, j, k: k
    if mode == "nn":
        a_spec = _operand_spec(a, tm, tk, I, K, a_layer, a_split)
        b_spec = _operand_spec(b, tk, tn, K, J, b_layer, b_split)
    elif mode == "nt":
        a_spec = _operand_spec(a, tm, tk, I, K, a_layer, a_split)
        b_spec = _operand_spec(b, tn, tk, J, K, b_layer, b_split)
    else:
        a_spec = _operand_spec(a, tk, tm, K, I, a_layer, a_split)
        b_spec = _operand_spec(b, tk, tn, K, J, b_layer, b_split)
    in_specs = [a_spec, b_spec]
    operands = [a, b]
    if residual is not None:
        in_specs.append(pl.BlockSpec((tm, tn), lambda i, j, k: (i, j)))
        operands.append(residual)
    in_specs += [pl.BlockSpec(memory_space=pl.ANY)] * len(deps)
    operands += list(deps)
    n_in = len(operands)

    def finish(acc, res_ref, o_ref):
        if scale is not None:
            acc = acc * scale
        if res_ref is not None:
            acc = acc + res_ref[...]
        o_ref[...] = acc.astype(o_ref.dtype)

    def body(*refs):
        a_ref, b_ref = refs[0], refs[1]
        res_ref = refs[2] if residual is not None else None
        o_ref = refs[n_in]
        part = _dot(a_ref[...], b_ref[...], mode)
        if nk == 1:
            finish(part, res_ref, o_ref)
        else:
            acc_ref = refs[-1]
            k = pl.program_id(2)

            @pl.when(k == 0)
            def _():
                acc_ref[...] = part

            @pl.when(k > 0)
            def _():
                acc_ref[...] += part

            @pl.when(k == nk - 1)
            def _():
                finish(acc_ref[...], res_ref, o_ref)

    return pl.pallas_call(
        body, name=name,
        out_shape=jax.ShapeDtypeStruct((m, n), out_dtype),
        grid=(m // tm, n // tn, nk),
        in_specs=in_specs,
        out_specs=pl.BlockSpec((tm, tn), lambda i, j, k: (i, j)),
        scratch_shapes=[pltpu.VMEM((tm, tn), F32)] if nk > 1 else [],
        compiler_params=_params(("parallel", "parallel", "arbitrary")),
    )(*operands)


def _ffn_in_fwd(n_act, w_in, layer, *, name, deps=()):
    tp, d = n_act.shape
    f = w_in.shape[2] // 2
    tm = _tile(tp, 2112)
    tn = _tile(f, 512, LANES)
    nj = f // tn

    ch = _tile(tm, EPILOGUE_ROWS)

    def body(n_ref, wg_ref, wu_ref, *rest):
        gu_ref, a_ref = rest[len(deps):]
        for r in range(tm // ch):
            rows = pl.ds(r * ch, ch)
            x = n_ref[rows, :]
            g = _dot(x, wg_ref[...], "nn")
            u = _dot(x, wu_ref[...], "nn")
            gu_ref[0, rows, :] = g.astype(BF16)
            gu_ref[1, rows, :] = u.astype(BF16)
            a_ref[rows, :] = (g * _sigmoid(g) * u).astype(BF16)

    return pl.pallas_call(
        body, name=name,
        out_shape=(jax.ShapeDtypeStruct((2, tp, f), BF16), jax.ShapeDtypeStruct((tp, f), BF16)),
        grid=(tp // tm, nj),
        in_specs=[pl.BlockSpec((tm, d), lambda i, j: (i, 0)),
                  pl.BlockSpec((None, d, tn), lambda i, j: (layer, 0, j)),
                  pl.BlockSpec((None, d, tn), lambda i, j: (layer, 0, j + nj))] + [pl.BlockSpec(memory_space=pl.ANY)] * len(deps),
        out_specs=(pl.BlockSpec((2, tm, tn), lambda i, j: (0, i, j)), pl.BlockSpec((tm, tn), lambda i, j: (i, j))),
        compiler_params=_params(("parallel", "arbitrary")),
    )(n_act, w_in, w_in, *deps)


def _ffn_bwd_act(dhb, w_out, layer, gu, *, name, deps=()):
    tp, d = dhb.shape
    f = w_out.shape[1]
    tm = _tile(tp, 2112)
    tn = _tile(f, 512, LANES)

    ch = _tile(tm, EPILOGUE_ROWS)

    def body(dh_ref, w_ref, gu_ref, *rest):
        dz_ref, a_ref = rest[len(deps):]
        for r in range(tm // ch):
            rows = pl.ds(r * ch, ch)
            da = 0.5 * _dot(dh_ref[rows, :], w_ref[...], "nt")
            g = gu_ref[0, rows, :].astype(F32)
            u = gu_ref[1, rows, :].astype(F32)
            s = _sigmoid(g)
            sg = g * s
            dz_ref[0, rows, :] = (da * u * (s * (1.0 + g * (1.0 - s)))).astype(BF16)
            dz_ref[1, rows, :] = (da * sg).astype(BF16)
            a_ref[rows, :] = (sg * u).astype(BF16)

    return pl.pallas_call(
        body, name=name,
        out_shape=(jax.ShapeDtypeStruct((2, tp, f), BF16), jax.ShapeDtypeStruct((tp, f), BF16)),
        grid=(tp // tm, f // tn),
        in_specs=[pl.BlockSpec((tm, d), lambda i, j: (i, 0)),
                  pl.BlockSpec((None, tn, d), lambda i, j: (layer, j, 0)),
                  pl.BlockSpec((2, tm, tn), lambda i, j: (0, i, j))] + [pl.BlockSpec(memory_space=pl.ANY)] * len(deps),
        out_specs=(pl.BlockSpec((2, tm, tn), lambda i, j: (0, i, j)), pl.BlockSpec((tm, tn), lambda i, j: (i, j))),
        compiler_params=_params(("parallel", "arbitrary")),
    )(dhb, w_out, gu, *deps)


def _rms_fwd(h, g, layer, *, name, deps=()):
    tp, d = h.shape
    tr = _tile(tp, 528)

    def body(h_ref, g_ref, *rest):
        n_ref = rest[len(deps)]
        x = h_ref[...]
        r = lax.rsqrt(jnp.mean(x * x, axis=-1, keepdims=True) + RMS_EPS)
        n_ref[...] = (x * r * g_ref[...]).astype(BF16)

    return pl.pallas_call(
        body, name=name, out_shape=jax.ShapeDtypeStruct((tp, d), BF16), grid=(tp // tr,),
        in_specs=[pl.BlockSpec((tr, d), lambda i: (i, 0)), pl.BlockSpec((None, 1, d), lambda i: (layer, 0, 0))]
        + [pl.BlockSpec(memory_space=pl.ANY)] * len(deps),
        out_specs=pl.BlockSpec((tr, d), lambda i: (i, 0)),
        compiler_params=_params(("parallel",)),
    )(h, g, *deps)


def _rms_bwd(h, g, layer, dn, dres, *, name, deps=()):
    tp, d = h.shape
    tr = _tile(tp, 528)

    def body(h_ref, g_ref, dn_ref, dres_ref, *rest):
        dh_ref, dhb_ref, dg_ref = rest[len(deps):]
        x = h_ref[...]
        r = lax.rsqrt(jnp.mean(x * x, axis=-1, keepdims=True) + RMS_EPS)
        xhat = x * r
        dn_v = dn_ref[...]
        dxhat = dn_v * g_ref[...]
        dh = dres_ref[...] + r * (dxhat - xhat * jnp.mean(dxhat * xhat, axis=-1, keepdims=True))
        dh_ref[...] = dh
        dhb_ref[...] = dh.astype(BF16)
        part = jnp.sum(dn_v * xhat, axis=0, keepdims=True)

        @pl.when(pl.program_id(0) == 0)
        def _():
            dg_ref[...] = part

        @pl.when(pl.program_id(0) > 0)
        def _():
            dg_ref[...] += part

    return pl.pallas_call(
        body, name=name,
        out_shape=(jax.ShapeDtypeStruct((tp, d), F32), jax.ShapeDtypeStruct((tp, d), BF16), jax.ShapeDtypeStruct((1, d), F32)),
        grid=(tp // tr,),
        in_specs=[pl.BlockSpec((tr, d), lambda i: (i, 0)), pl.BlockSpec((None, 1, d), lambda i: (layer, 0, 0)),
                  pl.BlockSpec((tr, d), lambda i: (i, 0)), pl.BlockSpec((tr, d), lambda i: (i, 0))]
        + [pl.BlockSpec(memory_space=pl.ANY)] * len(deps),
        out_specs=(pl.BlockSpec((tr, d), lambda i: (i, 0)), pl.BlockSpec((tr, d), lambda i: (i, 0)),
                   pl.BlockSpec((1, d), lambda i: (0, 0))),
        compiler_params=_params(("arbitrary",)),
    )(h, g, dn, dres, *deps)


def _final_loss(h, g, tgt, n_meta, t_valid, *, name):
    tp, d = h.shape
    tr = _tile(tp, 528)

    def body(h_ref, g_ref, t_ref, dh_ref, dhb_ref, dg_ref, loss_ref):
        i = pl.program_id(0)
        x = h_ref[...]
        r = lax.rsqrt(jnp.mean(x * x, axis=-1, keepdims=True) + RMS_EPS)
        xhat = x * r
        gv = g_ref[...]
        row = i * tr + lax.broadcasted_iota(jnp.int32, (tr, 1), 0)
        valid = jnp.logical_and(row >= n_meta, row < t_valid)
        err = jnp.where(valid, xhat * gv - t_ref[...], 0.0)
        dy = err * (1.0 / d)
        dxhat = dy * gv
        dh = r * (dxhat - xhat * jnp.mean(dxhat * xhat, axis=-1, keepdims=True))
        dh_ref[...] = dh
        dhb_ref[...] = dh.astype(BF16)
        dg_part = jnp.sum(dy * xhat, axis=0, keepdims=True)
        loss_part = jnp.sum(jnp.sum(err * err, axis=1, keepdims=True), axis=0, keepdims=True) * (0.5 / d)

        @pl.when(i == 0)
        def _():
            dg_ref[...] = dg_part
            loss_ref[...] = loss_part

        @pl.when(i > 0)
        def _():
            dg_ref[...] += dg_part
            loss_ref[...] += loss_part

    return pl.pallas_call(
        body, name=name,
        out_shape=(jax.ShapeDtypeStruct((tp, d), F32), jax.ShapeDtypeStruct((tp, d), BF16),
                   jax.ShapeDtypeStruct((1, d), F32), jax.ShapeDtypeStruct((1, 1), F32)),
        grid=(tp // tr,),
        in_specs=[pl.BlockSpec((tr, d), lambda i: (i, 0)), pl.BlockSpec((1, d), lambda i: (0, 0)),
                  pl.BlockSpec((tr, d), lambda i: (i, 0))],
        out_specs=(pl.BlockSpec((tr, d), lambda i: (i, 0)), pl.BlockSpec((tr, d), lambda i: (i, 0)),
                   pl.BlockSpec((1, d), lambda i: (0, 0)), pl.BlockSpec((1, 1), lambda i: (0, 0))),
        compiler_params=_params(("arbitrary",)),
    )(h, g, tgt)


def _mix_param_specs(mp, layer, imap):
    def vec(a):
        return pl.BlockSpec((None,) + a.shape[1:], lambda *g: (layer,) + (0,) * (a.ndim - 1))
    return [vec(mp[k]) for k in ("conv_w", "conv_b", "wa", "ba", "wx", "bx", "a_param", "pool_w", "pool_b", "pool_scale")]


def _mix_param_list(mp):
    return [mp[k] for k in ("conv_w", "conv_b", "wa", "ba", "wx", "bx", "a_param", "pool_w", "pool_b", "pool_scale")]


def _lru_gates(xc, wa_h, ba_h, wx_h, bx_h, sp_h):
    xb = xc.astype(BF16)
    ra = _sigmoid(_dot(xb, wa_h, "nn") + ba_h)
    ii = _sigmoid(_dot(xb, wx_h, "nn") + bx_h)
    la = -LRU_C * ra * sp_h
    return ra, ii, la, jnp.exp(la)


def _shifted(x, k):
    return x if k == 0 else pltpu.roll(x, k % x.shape[0], axis=0)


def _conv_taps(xbuf, sl, tc):
    x = xbuf[:, sl]
    return [_shifted(x, CONV_WIDTH - 1 - k)[HIST:HIST + tc] for k in range(CONV_WIDTH)]


def _conv_fwd(taps, cw_ref, cb_ref, sl):
    xc = cb_ref[:, sl]
    for k in range(CONV_WIDTH):
        xc = xc + cw_ref[k:k + 1, sl] * taps[k]
    return xc


def _window_sum(x, win, direction):
    s = x
    step = 1
    while step < win:
        s = s + _shifted(s, direction * step)
        step *= 2
    return s


def _pool_delta(pbuf, cols, win, t0, tc):
    x = pbuf[:, cols]
    u = x[HIST:HIST + tc]
    s = _window_sum(x, win, 1)[HIST:HIST + tc]
    t = t0 + lax.broadcasted_iota(jnp.int32, (tc, 1), 0)
    inv_cnt = 1.0 / jnp.minimum(t + 1, win).astype(F32)
    return s * inv_cnt - u, inv_cnt


def _mix_fwd(z, mp, layer, *, name):
    tp = z.shape[0]
    dl = z.shape[1] // 3
    n_heads, hd = mp["wa"].shape[1], mp["wa"].shape[2]
    n_groups, gd = mp["pool_w"].shape[1], mp["pool_w"].shape[2]
    tc = _tile(tp, 384)

    def body(z_ref, cw_ref, cb_ref, wa_ref, ba_ref, wx_ref, bx_ref, ap_ref, pw_ref, pb_ref, ps_ref,
             m_ref, hs_ref, xbuf, pbuf, a_s, b_s, hcar):
        i = pl.program_id(0)

        @pl.when(i == 0)
        def _():
            xbuf[pl.ds(0, HIST), :] = jnp.zeros((HIST, dl), F32)
            pbuf[pl.ds(0, HIST), :] = jnp.zeros((HIST, dl), F32)
            hcar[...] = jnp.zeros((1, dl), F32)

        @pl.when(i > 0)
        def _():
            xbuf[pl.ds(0, HIST), :] = xbuf[pl.ds(tc, HIST), :]
            pbuf[pl.ds(0, HIST), :] = pbuf[pl.ds(tc, HIST), :]

        xbuf[pl.ds(HIST, tc), :] = z_ref[:, pl.ds(0, dl)]
        pbuf[pl.ds(HIST, tc), :] = z_ref[:, pl.ds(2 * dl, dl)]
        sp = _softplus_neg(ap_ref[...])
        for h in range(n_heads):
            sl = pl.ds(h * hd, hd)
            xc = _conv_fwd(_conv_taps(xbuf, sl, tc), cw_ref, cb_ref, sl)
            _, ii, la, a = _lru_gates(xc, wa_ref[h], ba_ref[:, sl], wx_ref[h], bx_ref[:, sl], sp[:, h * hd:(h + 1) * hd])
            a_s[:, sl] = a
            b_s[:, sl] = jnp.sqrt(_neg_expm1(2.0 * la)) * ii * xc

        def step(t, hprev):
            hnew = a_s[pl.ds(t, 1), :] * hprev + b_s[pl.ds(t, 1), :]
            hs_ref[pl.ds(t, 1), :] = hnew
            return hnew

        hcar[...] = lax.fori_loop(0, tc, step, hcar[...], unroll=SCAN_UNROLL)
        for h in range(n_heads):
            sl = pl.ds(h * hd, hd)
            m_ref[:, sl] = (hs_ref[:, sl] * _gelu(z_ref[:, pl.ds(dl + h * hd, hd)])).astype(BF16)
        for g in range(n_groups):
            cols = pl.ds(g * gd, gd)
            dlt, _ = _pool_delta(pbuf, cols, POOL_WINDOWS[g], i * tc, tc)
            q = _dot(dlt.astype(BF16), pw_ref[g], "nn") + pb_ref[:, cols]
            m_ref[:, pl.ds(dl + g * gd, gd)] = (q * ps_ref[:, cols]).astype(BF16)

    return pl.pallas_call(
        body, name=name,
        out_shape=(jax.ShapeDtypeStruct((tp, 2 * dl), BF16), jax.ShapeDtypeStruct((tp, dl), F32)),
        grid=(tp // tc,),
        in_specs=[pl.BlockSpec((tc, 3 * dl), lambda i: (i, 0))] + _mix_param_specs(mp, layer, None),
        out_specs=(pl.BlockSpec((tc, 2 * dl), lambda i: (i, 0)), pl.BlockSpec((tc, dl), lambda i: (i, 0))),
        scratch_shapes=[pltpu.VMEM((HIST + tc, dl), F32), pltpu.VMEM((HIST + tc, dl), F32),
                        pltpu.VMEM((tc, dl), F32), pltpu.VMEM((tc, dl), F32), pltpu.VMEM((1, dl), F32)],
        compiler_params=_params(("arbitrary",)),
    )(z, *_mix_param_list(mp))


_MIX_GRADS = ("conv_w", "conv_b", "wa", "ba", "wx", "bx", "a_param", "pool_w", "pool_b", "pool_scale")


def _mix_bwd(z, hs, dm, mp, layer, *, name):
    tp = z.shape[0]
    dl = z.shape[1] // 3
    n_heads, hd = mp["wa"].shape[1], mp["wa"].shape[2]
    n_groups, gd = mp["pool_w"].shape[1], mp["pool_w"].shape[2]
    tc = _tile(tp, 384)
    nc = tp // tc
    per = tc // HIST

    def body(z_ref, zp_ref, hs_ref, hsp_ref, dm_ref, cw_ref, cb_ref, wa_ref, ba_ref, wx_ref, bx_ref, ap_ref, pw_ref, pb_ref,
             ps_ref, dz_ref, dcw_ref, dcb_ref, dwa_ref, dba_ref, dwx_ref, dbx_ref, dap_ref, dpw_ref, dpb_ref, dps_ref,
             xbuf, pbuf, hbuf, dxbuf, ddbuf, a_s, lam_s, ra_s, ii_s, xc_s, ccar):
        i = pl.program_id(0)
        ci = nc - 1 - i

        @pl.when(i == 0)
        def _():
            dxbuf[pl.ds(tc, HIST), :] = jnp.zeros((HIST, dl), F32)
            ddbuf[pl.ds(tc, HIST), :] = jnp.zeros((HIST, dl), F32)
            ccar[...] = jnp.zeros((1, dl), F32)
            for ref in (dcw_ref, dcb_ref, dwa_ref, dba_ref, dwx_ref, dbx_ref, dap_ref, dpw_ref, dpb_ref, dps_ref):
                ref[...] = jnp.zeros(ref.shape, F32)

        @pl.when(ci == 0)
        def _():
            xbuf[pl.ds(0, HIST), :] = jnp.zeros((HIST, dl), F32)
            pbuf[pl.ds(0, HIST), :] = jnp.zeros((HIST, dl), F32)
            hbuf[pl.ds(0, HIST), :] = jnp.zeros((HIST, dl), F32)

        @pl.when(ci > 0)
        def _():
            xbuf[pl.ds(0, HIST), :] = zp_ref[:, pl.ds(0, dl)]
            pbuf[pl.ds(0, HIST), :] = zp_ref[:, pl.ds(2 * dl, dl)]
            hbuf[pl.ds(0, HIST), :] = hsp_ref[...]

        xbuf[pl.ds(HIST, tc), :] = z_ref[:, pl.ds(0, dl)]
        pbuf[pl.ds(HIST, tc), :] = z_ref[:, pl.ds(2 * dl, dl)]
        hbuf[pl.ds(HIST, tc), :] = hs_ref[...]
        sp = _softplus_neg(ap_ref[...])

        for h in range(n_heads):
            sl = pl.ds(h * hd, hd)
            xc = _conv_fwd(_conv_taps(xbuf, sl, tc), cw_ref, cb_ref, sl)
            ra, ii, _, a = _lru_gates(xc, wa_ref[h], ba_ref[:, sl], wx_ref[h], bx_ref[:, sl], sp[:, h * hd:(h + 1) * hd])
            a_s[:, sl] = a
            ra_s[:, sl] = ra
            ii_s[:, sl] = ii
            xc_s[:, sl] = xc
            gel, gel_grad = _gelu_and_grad(z_ref[:, pl.ds(dl + h * hd, hd)])
            dya = dm_ref[:, sl]
            lam_s[:, sl] = dya * gel
            dz_ref[:, pl.ds(dl + h * hd, hd)] = (dya * hs_ref[:, sl] * gel_grad).astype(BF16)

        def step(r, carry):
            t = tc - 1 - r
            lam = lam_s[pl.ds(t, 1), :] + carry
            lam_s[pl.ds(t, 1), :] = lam
            return a_s[pl.ds(t, 1), :] * lam

        ccar[...] = lax.fori_loop(0, tc, step, ccar[...], unroll=SCAN_UNROLL)

        for h in range(n_heads):
            sl = pl.ds(h * hd, hd)
            sp_h = sp[:, h * hd:(h + 1) * hd]
            lam = lam_s[:, sl]
            a = a_s[:, sl]
            ra = ra_s[:, sl]
            ii = ii_s[:, sl]
            xc = xc_s[:, sl]
            e = _neg_expm1(-2.0 * LRU_C * ra * sp_h)
            inv_mult = lax.rsqrt(e)
            mult = e * inv_mult
            hprev = _shifted(hbuf[:, sl], 1)[HIST:HIST + tc]
            lam_i = lam * ii
            lam_m = lam * mult
            dla = lam * hprev * a - lam_i * xc * (a * a) * inv_mult
            dla_r = dla * (-LRU_C) * ra
            dap_ref[:, sl] += jnp.sum(dla_r, axis=0, keepdims=True)
            dpa = dla_r * sp_h * (1.0 - ra)
            dpx = lam_m * xc * ii * (1.0 - ii)
            dba_ref[:, sl] += jnp.sum(dpa, axis=0, keepdims=True)
            dbx_ref[:, sl] += jnp.sum(dpx, axis=0, keepdims=True)
            xb = xc.astype(BF16)
            dpa_b = dpa.astype(BF16)
            dpx_b = dpx.astype(BF16)
            dwa_ref[h] += _dot(xb, dpa_b, "tn")
            dwx_ref[h] += _dot(xb, dpx_b, "tn")
            dxc = lam_m * ii + _dot(dpa_b, wa_ref[h], "nt") + _dot(dpx_b, wx_ref[h], "nt")
            dxbuf[pl.ds(0, tc), sl] = dxc
            dcb_ref[:, sl] += jnp.sum(dxc, axis=0, keepdims=True)
            taps = _conv_taps(xbuf, sl, tc)
            dx_all = dxbuf[:, sl]
            dzx = jnp.zeros((tc, hd), F32)
            for k in range(CONV_WIDTH):
                dcw_ref[k:k + 1, sl] += jnp.sum(dxc * taps[k], axis=0, keepdims=True)
                dzx = dzx + cw_ref[k:k + 1, sl] * _shifted(dx_all, k - (CONV_WIDTH - 1))[0:tc]
            dz_ref[:, sl] = dzx.astype(BF16)
            dxbuf[pl.ds(tc, HIST), sl] = dxbuf[pl.ds(0, HIST), sl]

        for g in range(n_groups):
            cols = pl.ds(g * gd, gd)
            win = POOL_WINDOWS[g]
            dlt, inv_cnt = _pool_delta(pbuf, cols, win, ci * tc, tc)
            db = dlt.astype(BF16)
            q = _dot(db, pw_ref[g], "nn") + pb_ref[:, cols]
            dyb = dm_ref[:, pl.ds(dl + g * gd, gd)]
            dps_ref[:, cols] += jnp.sum(dyb * q, axis=0, keepdims=True)
            dq = dyb * ps_ref[:, cols]
            dpb_ref[:, cols] += jnp.sum(dq, axis=0, keepdims=True)
            dqb = dq.astype(BF16)
            dpw_ref[g] += _dot(db, dqb, "tn")
            dd = _dot(dqb, pw_ref[g], "nt")
            ddbuf[pl.ds(0, tc), cols] = dd * inv_cnt
            dzp = _window_sum(ddbuf[:, cols], win, -1)[0:tc] - dd
            dz_ref[:, pl.ds(2 * dl + g * gd, gd)] = dzp.astype(BF16)
            ddbuf[pl.ds(tc, HIST), cols] = ddbuf[pl.ds(0, HIST), cols]

        @pl.when(i == nc - 1)
        def _():
            dap_ref[...] = dap_ref[...] * (-_sigmoid(-ap_ref[...]))

    rev = lambda i: (nc - 1 - i, 0)
    prev = lambda i: (jnp.maximum((nc - 1 - i) * per - 1, 0), 0)
    const = lambda a: pl.BlockSpec(a.shape[1:], lambda i: (0,) * (a.ndim - 1))
    plist = _mix_param_list(mp)
    grad_shapes = [jax.ShapeDtypeStruct(a.shape[1:], F32) for a in plist]
    buf = lambda rows: pltpu.VMEM((rows, dl), F32)
    outs = pl.pallas_call(
        body, name=name,
        out_shape=[jax.ShapeDtypeStruct((tp, 3 * dl), BF16)] + grad_shapes,
        grid=(nc,),
        in_specs=[pl.BlockSpec((tc, 3 * dl), rev), pl.BlockSpec((HIST, 3 * dl), prev),
                  pl.BlockSpec((tc, dl), rev), pl.BlockSpec((HIST, dl), prev),
                  pl.BlockSpec((tc, 2 * dl), rev)] + _mix_param_specs(mp, layer, None),
        out_specs=[pl.BlockSpec((tc, 3 * dl), rev)] + [const(a) for a in plist],
        scratch_shapes=[buf(HIST + tc), buf(HIST + tc), buf(HIST + tc), buf(tc + HIST), buf(tc + HIST),
                        buf(tc), buf(tc), buf(tc), buf(tc), buf(tc), buf(1)],
        compiler_params=_params(("arbitrary",)),
    )(z, z, hs, hs, dm, *plist)
    return outs[0], dict(zip(_MIX_GRADS, outs[1:]))


def _ffn_fwd_out(h, act, w_out, tag, deps):
    tp, d = h.shape
    f = w_out.shape[1]
    return _matmul(act, w_out, mode="nn", m=tp, n=d, kdim=f, tm=_tile(tp, 1056), tn=_tile(d, 512, LANES),
                   tk=_tile(f, K_TILE, LANES), out_dtype=F32, scale=0.5, residual=h, b_layer=0, name=f"{tag}_out", deps=deps)


def _ffn_bwd(dh, dhb, saved, norm, layer, w_in, w_out, tag, emit, deps):
    h, n_act, gu = saved
    tp, d = h.shape
    f = w_out.shape[1]
    dz, act = _ffn_bwd_act(dhb, w_out, 0, gu, name=f"{tag}_bact", deps=deps)
    tok = emit("w_out", _matmul(act, dhb, mode="tn", m=f, n=d, kdim=tp, tm=_tile(f, 512, LANES), tn=_tile(d, WGRAD_TILE, LANES), tk=tp,
                                out_dtype=BF16, scale=0.5, name=f"{tag}_dwout"))
    tok = emit("w_in", _matmul(n_act, dz, mode="tn", m=d, n=2 * f, kdim=tp, tm=_tile(d, WGRAD_TILE, LANES), tn=_tile(f, 512, LANES),
                               tk=tp, out_dtype=BF16, b_split=True, name=f"{tag}_dwin", deps=tok))
    dn = _matmul(dz, w_in, mode="nt", m=tp, n=d, kdim=2 * f, tm=_tile(tp, 1056), tn=_tile(d, 512, LANES), tk=_tile(f, K_TILE, LANES),
                 out_dtype=F32, a_split=True, b_layer=0, name=f"{tag}_dn", deps=tok)
    return _rms_bwd(h, norm, layer, dn, dh, name=f"{tag}_brms", deps=emit("flush", dn))


def _mix_block_fwd(h, n_act, layer, w_in, w_out, mp, tag, deps):
    tp, d = h.shape
    d_in = w_in.shape[2]
    z = _matmul(n_act, w_in, mode="nn", m=tp, n=d_in, kdim=d, tm=_tile(tp, 2112), tn=_tile(d_in, 512, LANES), tk=d,
                out_dtype=F32, b_layer=0, name=f"{tag}_in", deps=deps)
    m_act, hs = _mix_fwd(z, mp, layer, name=f"{tag}_mix")
    h_out = _matmul(m_act, w_out, mode="nn", m=tp, n=d, kdim=d, tm=_tile(tp, 2112), tn=_tile(d, 512, LANES), tk=d,
                    out_dtype=F32, residual=h, b_layer=0, name=f"{tag}_out")
    return h_out, (h, n_act, z, hs, m_act)


def _mix_block_bwd(dh, dhb, saved, norm, layer, w_in, w_out, mp, tag, emit, deps):
    h, n_act, z, hs, m_act = saved
    tp, d = h.shape
    d_in = w_in.shape[2]
    dm = _matmul(dhb, w_out, mode="nt", m=tp, n=d, kdim=d, tm=_tile(tp, 2112), tn=_tile(d, 512, LANES), tk=d,
                 out_dtype=F32, b_layer=0, name=f"{tag}_dm", deps=deps)
    tok = emit("w_out", _matmul(m_act, dhb, mode="tn", m=d, n=d, kdim=tp, tm=_tile(d, 512, LANES), tn=_tile(d, WGRAD_TILE, LANES), tk=tp,
                                out_dtype=BF16, name=f"{tag}_dwout"))
    dz, g_mix = _mix_bwd(z, hs, dm, mp, layer, name=f"{tag}_bmix")
    tok = emit("w_in", _matmul(n_act, dz, mode="tn", m=d, n=d_in, kdim=tp, tm=_tile(d, WGRAD_TILE, LANES), tn=_tile(d_in, 512, LANES),
                               tk=tp, out_dtype=BF16, name=f"{tag}_dwin", deps=tok))
    dn = _matmul(dz, w_in, mode="nt", m=tp, n=d, kdim=d_in, tm=_tile(tp, 1056), tn=_tile(d, 1024, LANES), tk=d_in,
                 out_dtype=F32, b_layer=0, name=f"{tag}_dn", deps=tok)
    dh_in, dhb_in, g_norm = _rms_bwd(h, norm, layer, dn, dh, name=f"{tag}_brms", deps=emit("flush", dn))
    return dh_in, dhb_in, g_norm, g_mix


def _mesh_pos():
    x, y, c = lax.axis_index("x"), lax.axis_index("y"), lax.axis_index("c")
    chips = [(1 - x, y), (x, 1 - y), (1 - x, 1 - y)]
    return x, y, c, chips


def _block(ref, axis, j, size):
    idx = [slice(None)] * len(ref.shape)
    idx[axis] = pl.ds(j * size, size)
    return ref.at[tuple(idx)]


def _all_gather(shards, axes, *, name):
    n = len(shards)
    sizes = [s.shape[ax] for s, ax in zip(shards, axes)]
    full_shapes = [s.shape[:ax] + (N_DEV * s.shape[ax],) + s.shape[ax + 1:] for s, ax in zip(shards, axes)]

    def body(*refs):
        srcs, fulls = refs[:n], refs[n:2 * n]
        send_sems, recv_sems, local_sems = refs[2 * n:]
        x, y, c, chips = _mesh_pos()
        me, sib = (x, y, c), (x, y, 1 - c)

        def blk(a, dev):
            return _block(fulls[a], axes[a], 4 * dev[0] + 2 * dev[1] + dev[2], sizes[a])

        def copy(a, k, block_dev, to, src=None):
            return pltpu.make_async_remote_copy(
                src_ref=blk(a, block_dev) if src is None else src, dst_ref=blk(a, block_dev),
                send_sem=send_sems.at[a, k], recv_sem=recv_sems.at[a, k], device_id=to, device_id_type=MESH)

        mine = [pltpu.make_async_copy(srcs[a], blk(a, me), local_sems.at[a]) for a in range(n)]
        for cp in mine:
            cp.start()
        first = []
        for a in range(n):
            first.append(copy(a, 0, me, sib, src=srcs[a]))
            first += [copy(a, 1 + j, me, (*chip, c), src=srcs[a]) for j, chip in enumerate(chips)]
        for cp in first:
            cp.start()
        passed = []
        for j, chip in enumerate(chips):
            for a in range(n):
                copy(a, 1 + j, (*chip, c), me).wait_recv()
                fwd = copy(a, 4 + j, (*chip, c), sib)
                fwd.start()
                passed.append(fwd)
        for a in range(n):
            copy(a, 0, sib, me).wait_recv()
            for j, chip in enumerate(chips):
                copy(a, 4 + j, (*chip, 1 - c), me).wait_recv()
        for cp in first + passed:
            cp.wait_send()
        for cp in mine:
            cp.wait()

    any_spec = pl.BlockSpec(memory_space=pl.ANY)
    return pl.pallas_call(
        body, name=name,
        out_shape=[jax.ShapeDtypeStruct(fs, s.dtype) for fs, s in zip(full_shapes, shards)],
        in_specs=[any_spec] * n, out_specs=[any_spec] * n,
        scratch_shapes=[pltpu.SemaphoreType.DMA((n, 7)), pltpu.SemaphoreType.DMA((n, 7)), pltpu.SemaphoreType.DMA((n,))],
    )(*shards)


_HBM = pl.BlockSpec(memory_space=pltpu.HBM)
_SEM = pl.BlockSpec(memory_space=pltpu.SEMAPHORE)
_ANY = pl.BlockSpec(memory_space=pl.ANY)
_EFFECT = pltpu.SideEffectType.DATAFLOW_SIDE_EFFECTING
_N_OUT = 4


def _in_hbm(a):
    return pltpu.with_memory_space_constraint(a, pltpu.HBM)


def _gather_start(shards, axes, deps, *, name):
    n = len(shards)
    sizes = [s.shape[ax] for s, ax in zip(shards, axes)]
    full_shapes = [s.shape[:ax] + (N_DEV * s.shape[ax],) + s.shape[ax + 1:] for s, ax in zip(shards, axes)]

    def body(*refs):
        srcs, lands = refs[:n], refs[n:2 * n]
        send_sems, recv_sems = refs[2 * n + len(deps)], refs[2 * n + len(deps) + 1]
        token = refs[-1]
        x, y, c, chips = _mesh_pos()
        targets = [(x, y, 1 - c)] + [(*chip, c) for chip in chips]
        for a in range(n):
            dst = _block(lands[a], axes[a], 4 * x + 2 * y + c, sizes[a])
            for k, to in enumerate(targets):
                pltpu.make_async_remote_copy(src_ref=srcs[a], dst_ref=dst, send_sem=send_sems.at[_N_OUT * a + k],
                                             recv_sem=recv_sems.at[_N_OUT * a + k],
                                             device_id=to, device_id_type=MESH).start()
        token[...] = jnp.zeros(token.shape, F32)

    lands0 = [lax.empty(fs, s.dtype) for fs, s in zip(full_shapes, shards)]
    outs = pl.pallas_call(
        body, name=name,
        out_shape=(pltpu.SemaphoreType.DMA((n * _N_OUT,)), pltpu.SemaphoreType.DMA((n * _N_OUT,)),
                   *[pltpu.HBM(s.shape, s.dtype) for s in shards], *[pltpu.HBM(fs, s.dtype) for fs, s in zip(full_shapes, shards)],
                   jax.ShapeDtypeStruct((8, LANES), F32)),
        in_specs=[_HBM] * (2 * n) + [_ANY] * len(deps),
        out_specs=(_SEM, _SEM, *[_HBM] * (2 * n), pl.BlockSpec(memory_space=pltpu.VMEM)),
        input_output_aliases={a: 2 + a for a in range(2 * n)},
        compiler_params=pltpu.CompilerParams(has_side_effects=_EFFECT),
    )(*[_in_hbm(s) for s in shards], *[_in_hbm(z) for z in lands0], *deps)
    handle = dict(send_sems=outs[0], recv_sems=outs[1], srcs=list(outs[2:2 + n]), lands=list(outs[2 + n:2 + 2 * n]), axes=list(axes))
    return handle, outs[-1]


def _gather_wait(handle, after, *, name):
    srcs, lands, axes = handle["srcs"], handle["lands"], handle["axes"]
    n = len(srcs)
    sizes = [s.shape[ax] for s, ax in zip(srcs, axes)]

    def body(*refs):
        src_refs, land_refs = refs[:n], refs[n:2 * n]
        send_sems, recv_sems = refs[2 * n], refs[2 * n + 1]
        x, y, c, chips = _mesh_pos()
        peers = [(x, y, 1 - c)] + [(*chip, c) for chip in chips]
        for a in range(n):
            for k, dev in enumerate(peers):
                cp = pltpu.make_async_remote_copy(
                    src_ref=src_refs[a], dst_ref=_block(land_refs[a], axes[a], 4 * dev[0] + 2 * dev[1] + dev[2], sizes[a]),
                    send_sem=send_sems.at[_N_OUT * a + k], recv_sem=recv_sems.at[_N_OUT * a + k], device_id=dev,
                    device_id_type=MESH)
                cp.wait_send()
                cp.wait_recv()

    outs = pl.pallas_call(
        body, name=name,
        out_shape=(*[pltpu.HBM(s.shape, s.dtype) for s in srcs], *[pltpu.HBM(z.shape, z.dtype) for z in lands]),
        in_specs=[_HBM] * (2 * n) + [_SEM, _SEM] + [_ANY] * len(after),
        out_specs=tuple([_HBM] * (2 * n)),
        input_output_aliases={a: a for a in range(2 * n)},
        compiler_params=pltpu.CompilerParams(has_side_effects=_EFFECT),
    )(*srcs, *lands, handle["send_sems"], handle["recv_sems"], *after)
    return list(outs[:n]), list(outs[n:])


def _place_own(shard, full, axis, dev, *, name):
    nd = shard.ndim
    rows, cols = shard.shape[-2:]
    tr = _tile(rows, 512, _row_mult(shard.dtype)) if rows % _row_mult(shard.dtype) == 0 else rows
    nrb = rows // tr
    block = shard.shape[:-2] + (tr, cols)

    def in_map(r, dev_ref):
        return (0,) * (nd - 2) + (r, 0)

    def out_map(r, dev_ref):
        idx = [0] * nd
        idx[nd - 2] = r
        idx[axis] = dev_ref[0] * (nrb if axis == nd - 2 else 1) + idx[axis]
        return tuple(idx)

    def body(dev_ref, src_ref, full_ref, out_ref):
        out_ref[...] = src_ref[...]

    grid_spec = pltpu.PrefetchScalarGridSpec(
        num_scalar_prefetch=1, grid=(nrb,),
        in_specs=[pl.BlockSpec(block, in_map), _ANY], out_specs=pl.BlockSpec(block, out_map))
    return pl.pallas_call(body, name=name, grid_spec=grid_spec, out_shape=jax.ShapeDtypeStruct(full.shape, full.dtype),
                          input_output_aliases={2: 0}, compiler_params=_params(("arbitrary",)))(dev, shard, full)


def _gather_pass_start(fulls, axes, *, name):
    n = len(fulls)
    sizes = [z.shape[ax] // N_DEV for z, ax in zip(fulls, axes)]

    def body(*refs):
        srcs = refs[:n]
        send_sems, recv_sems = refs[n], refs[n + 1]
        token = refs[-1]
        x, y, c, chips = _mesh_pos()
        for a in range(n):
            for j, chip in enumerate(chips):
                blk = _block(srcs[a], axes[a], 4 * chip[0] + 2 * chip[1] + c, sizes[a])
                pltpu.make_async_remote_copy(src_ref=blk, dst_ref=blk, send_sem=send_sems.at[3 * a + j], recv_sem=recv_sems.at[3 * a + j],
                                             device_id=(x, y, 1 - c), device_id_type=MESH).start()
        token[...] = jnp.zeros(token.shape, F32)

    outs = pl.pallas_call(
        body, name=name,
        out_shape=(pltpu.SemaphoreType.DMA((3 * n,)), pltpu.SemaphoreType.DMA((3 * n,)), *[pltpu.HBM(z.shape, z.dtype) for z in fulls],
                   jax.ShapeDtypeStruct((8, LANES), F32)),
        in_specs=[_HBM] * n, out_specs=(_SEM, _SEM, *[_HBM] * n, pl.BlockSpec(memory_space=pltpu.VMEM)),
        input_output_aliases={a: 2 + a for a in range(n)},
        compiler_params=pltpu.CompilerParams(has_side_effects=_EFFECT),
    )(*[_in_hbm(z) for z in fulls])
    return dict(send_sems=outs[0], recv_sems=outs[1], fulls=list(outs[2:2 + n]), axes=list(axes)), outs[-1]


def _gather_pass_wait(handle, after, *, name):
    fulls, axes = handle["fulls"], handle["axes"]
    n = len(fulls)
    sizes = [z.shape[ax] // N_DEV for z, ax in zip(fulls, axes)]

    def body(*refs):
        bufs = refs[:n]
        send_sems, recv_sems = refs[n], refs[n + 1]
        x, y, c, chips = _mesh_pos()
        for a in range(n):
            for j, chip in enumerate(chips):
                cp = pltpu.make_async_remote_copy(
                    src_ref=_block(bufs[a], axes[a], 4 * chip[0] + 2 * chip[1] + c, sizes[a]),
                    dst_ref=_block(bufs[a], axes[a], 4 * chip[0] + 2 * chip[1] + (1 - c), sizes[a]),
                    send_sem=send_sems.at[3 * a + j], recv_sem=recv_sems.at[3 * a + j], device_id=(x, y, 1 - c),
                    device_id_type=MESH)
                cp.wait_send()
                cp.wait_recv()

    outs = pl.pallas_call(
        body, name=name,
        out_shape=tuple(pltpu.HBM(z.shape, z.dtype) for z in fulls),
        in_specs=[_HBM] * n + [_SEM, _SEM] + [_ANY] * len(after), out_specs=tuple([_HBM] * n),
        input_output_aliases={a: a for a in range(n)},
        compiler_params=pltpu.CompilerParams(has_side_effects=_EFFECT),
    )(*fulls, handle["send_sems"], handle["recv_sems"], *after)
    return list(outs)


def _rs_exchange_sibling(tensors, axes, *, name):
    n = len(tensors)
    n_layers = [len(t) for t in tensors]
    sizes = [t[0].shape[ax] // N_DEV for t, ax in zip(tensors, axes)]
    blk_shapes = [t[0].shape[:ax] + (sz,) + t[0].shape[ax + 1:] for t, ax, sz in zip(tensors, axes, sizes)]
    flat = [g for t in tensors for g in t]
    offs = [sum(n_layers[:a]) for a in range(n)]

    def body(*refs):
        srcs = refs[:len(flat)]
        lands = refs[len(flat):len(flat) + n]
        send_sems, recv_sems = refs[len(flat) + n:]
        x, y, c, _ = _mesh_pos()
        sib = (x, y, 1 - c)
        for a in range(n):
            for l in range(n_layers[a]):
                for i in range(N_CHIP):
                    pltpu.make_async_remote_copy(
                        src_ref=_block(srcs[offs[a] + l], axes[a], 2 * i + (1 - c), sizes[a]), dst_ref=lands[a].at[i, l],
                        send_sem=send_sems.at[a], recv_sem=recv_sems.at[a], device_id=sib, device_id_type=MESH).start()
        for a in range(n):
            pltpu.make_async_remote_copy(src_ref=lands[a], dst_ref=lands[a], send_sem=send_sems.at[a], recv_sem=recv_sems.at[a],
                                         device_id=sib, device_id_type=MESH).wait()

    any_spec = pl.BlockSpec(memory_space=pl.ANY)
    return pl.pallas_call(
        body, name=name,
        out_shape=[jax.ShapeDtypeStruct((N_CHIP, nl) + bs, t[0].dtype) for nl, bs, t in zip(n_layers, blk_shapes, tensors)],
        in_specs=[any_spec] * len(flat), out_specs=[any_spec] * n,
        scratch_shapes=[pltpu.SemaphoreType.DMA((n,)), pltpu.SemaphoreType.DMA((n,))],
    )(*flat)


def _rs_add_sibling(layers, axis, land, my_c, *, name):
    n_layers = len(layers)
    shape = layers[0].shape
    size = shape[axis] // N_DEV
    blk_shape = shape[:axis] + (size,) + shape[axis + 1:]
    nd = len(shape)
    rows = blk_shape[-2]
    tr = _tile(rows, 512, _row_mult(layers[0].dtype))
    inner = (tr, blk_shape[-1])
    lead = blk_shape[:-2]
    if lead:
        raise ValueError("blocked gradients are 2-D")
    nrb = rows // tr

    def src_map(l):
        def imap(i, r, c_ref):
            j = 2 * i + c_ref[0]
            return (j * nrb + r, 0) if axis == 0 else (r, j)
        return imap

    def body(c_ref, *refs):
        srcs = refs[:n_layers]
        land_ref = refs[n_layers]
        out_ref = refs[n_layers + 1]
        for l in range(n_layers):
            out_ref[l] = (srcs[l][...].astype(F32) + land_ref[l].astype(F32)).astype(out_ref.dtype)

    grid_spec = pltpu.PrefetchScalarGridSpec(
        num_scalar_prefetch=1, grid=(N_CHIP, nrb),
        in_specs=[pl.BlockSpec(inner, src_map(l)) for l in range(n_layers)]
        + [pl.BlockSpec((None, n_layers) + inner, lambda i, r, c_ref: (i, 0, r, 0))],
        out_specs=pl.BlockSpec((None, n_layers) + inner, lambda i, r, c_ref: (i, 0, r, 0)),
    )
    return pl.pallas_call(
        body, name=name, grid_spec=grid_spec,
        out_shape=jax.ShapeDtypeStruct((N_CHIP, n_layers) + blk_shape, layers[0].dtype),
        compiler_params=_params(("arbitrary", "arbitrary")),
    )(my_c, *layers, land)


def _rs_exchange_chips(parts, *, name):
    n = len(parts)

    def body(*refs):
        srcs, lands = refs[:n], refs[n:2 * n]
        send_sems, recv_sems, local_sems = refs[2 * n:]
        x, y, c, chips = _mesh_pos()
        mine = 2 * x + y
        local = [pltpu.make_async_copy(srcs[a].at[mine], lands[a].at[mine], local_sems.at[a]) for a in range(n)]
        for cp in local:
            cp.start()

        def copy(a, j, chip):
            return pltpu.make_async_remote_copy(
                src_ref=srcs[a].at[2 * chip[0] + chip[1]], dst_ref=lands[a].at[mine],
                send_sem=send_sems.at[a, j], recv_sem=recv_sems.at[a, j], device_id=(*chip, c), device_id_type=MESH)

        sends = [copy(a, j, chip) for a in range(n) for j, chip in enumerate(chips)]
        for cp in sends:
            cp.start()
        for a in range(n):
            for j, chip in enumerate(chips):
                pltpu.make_async_remote_copy(
                    src_ref=srcs[a].at[mine], dst_ref=lands[a].at[2 * chip[0] + chip[1]],
                    send_sem=send_sems.at[a, j], recv_sem=recv_sems.at[a, j], device_id=(*chip, c), device_id_type=MESH).wait_recv()
        for cp in sends:
            cp.wait_send()
        for cp in local:
            cp.wait()

    any_spec = pl.BlockSpec(memory_space=pl.ANY)
    return pl.pallas_call(
        body, name=name,
        out_shape=[jax.ShapeDtypeStruct(p.shape, p.dtype) for p in parts],
        in_specs=[any_spec] * n, out_specs=[any_spec] * n,
        scratch_shapes=[pltpu.SemaphoreType.DMA((n, 3)), pltpu.SemaphoreType.DMA((n, 3)), pltpu.SemaphoreType.DMA((n,))],
    )(*parts)


def _rs_sibling_start(g, axis, *, name):
    size = g.shape[axis] // N_DEV
    land = lax.empty((N_CHIP, 1) + g.shape[:axis] + (size,) + g.shape[axis + 1:], g.dtype)

    def body(src, dst, send_sems, recv_sems, src_thru, dst_thru, token):
        x, y, c, _ = _mesh_pos()
        for i in range(N_CHIP):
            pltpu.make_async_remote_copy(src_ref=_block(src, axis, 2 * i + (1 - c), size), dst_ref=dst.at[i, 0], send_sem=send_sems.at[i],
                                         recv_sem=recv_sems.at[i], device_id=(x, y, 1 - c), device_id_type=MESH).start()
        token[...] = jnp.zeros(token.shape, F32)

    outs = pl.pallas_call(
        body, name=name,
        out_shape=(pltpu.SemaphoreType.DMA((N_CHIP,)), pltpu.SemaphoreType.DMA((N_CHIP,)), pltpu.HBM(g.shape, g.dtype),
                   pltpu.HBM(land.shape, land.dtype), jax.ShapeDtypeStruct((8, LANES), F32)),
        in_specs=[_HBM, _HBM], out_specs=(_SEM, _SEM, _HBM, _HBM, pl.BlockSpec(memory_space=pltpu.VMEM)),
        input_output_aliases={0: 2, 1: 3},
        compiler_params=pltpu.CompilerParams(has_side_effects=_EFFECT),
    )(_in_hbm(g), _in_hbm(land))
    return dict(send_sems=outs[0], recv_sems=outs[1], src=outs[2], land=outs[3], axis=axis), outs[4]


def _rs_sibling_wait(handle, after, *, name):
    src, land, axis = handle["src"], handle["land"], handle["axis"]
    size = src.shape[axis] // N_DEV

    def body(src_ref, dst_ref, send_sems, recv_sems, *rest):
        x, y, c, _ = _mesh_pos()
        for i in range(N_CHIP):
            cp = pltpu.make_async_remote_copy(src_ref=_block(src_ref, axis, 2 * i + (1 - c), size), dst_ref=dst_ref.at[i, 0],
                                              send_sem=send_sems.at[i], recv_sem=recv_sems.at[i], device_id=(x, y, 1 - c),
                                              device_id_type=MESH)
            cp.wait_send()
            cp.wait_recv()

    outs = pl.pallas_call(
        body, name=name,
        out_shape=(pltpu.HBM(src.shape, src.dtype), pltpu.HBM(land.shape, land.dtype)),
        in_specs=[_HBM, _HBM, _SEM, _SEM] + [_ANY] * len(after), out_specs=(_HBM, _HBM),
        input_output_aliases={0: 0, 1: 1},
        compiler_params=pltpu.CompilerParams(has_side_effects=_EFFECT),
    )(src, land, handle["send_sems"], handle["recv_sems"], *after)
    return outs[0], outs[1]


def _rs_chips_start(part, *, name):
    land = lax.empty(part.shape, part.dtype)

    def body(src, dst, send_sems, recv_sems, src_thru, dst_thru, token):
        x, y, c, chips = _mesh_pos()
        for j, chip in enumerate(chips):
            pltpu.make_async_remote_copy(src_ref=src.at[2 * chip[0] + chip[1]], dst_ref=dst.at[2 * x + y], send_sem=send_sems.at[j],
                                         recv_sem=recv_sems.at[j], device_id=(*chip, c), device_id_type=MESH).start()
        token[...] = jnp.zeros(token.shape, F32)

    outs = pl.pallas_call(
        body, name=name,
        out_shape=(pltpu.SemaphoreType.DMA((3,)), pltpu.SemaphoreType.DMA((3,)), pltpu.HBM(part.shape, part.dtype),
                   pltpu.HBM(land.shape, land.dtype), jax.ShapeDtypeStruct((8, LANES), F32)),
        in_specs=[_HBM, _HBM], out_specs=(_SEM, _SEM, _HBM, _HBM, pl.BlockSpec(memory_space=pltpu.VMEM)),
        input_output_aliases={0: 2, 1: 3},
        compiler_params=pltpu.CompilerParams(has_side_effects=_EFFECT),
    )(_in_hbm(part), _in_hbm(land))
    return dict(send_sems=outs[0], recv_sems=outs[1], src=outs[2], land=outs[3]), outs[4]


def _rs_chips_wait(handle, after, *, name):
    def body(src, dst, send_sems, recv_sems, *rest):
        x, y, c, chips = _mesh_pos()
        for j, chip in enumerate(chips):
            cp = pltpu.make_async_remote_copy(src_ref=src.at[2 * chip[0] + chip[1]], dst_ref=dst.at[2 * chip[0] + chip[1]],
                                              send_sem=send_sems.at[j], recv_sem=recv_sems.at[j], device_id=(*chip, c),
                                              device_id_type=MESH)
            cp.wait_send()
            cp.wait_recv()

    src, land = handle["src"], handle["land"]
    outs = pl.pallas_call(
        body, name=name,
        out_shape=(pltpu.HBM(src.shape, src.dtype), pltpu.HBM(land.shape, land.dtype)),
        in_specs=[_HBM, _HBM, _SEM, _SEM] + [_ANY] * len(after), out_specs=(_HBM, _HBM),
        input_output_aliases={0: 0, 1: 1},
        compiler_params=pltpu.CompilerParams(has_side_effects=_EFFECT),
    )(src, land, handle["send_sems"], handle["recv_sems"], *after)
    return outs[0], outs[1]


def _adam_math(w, g, m, v):
    m_new = ADAM_B1 * m + (1.0 - ADAM_B1) * g
    v_new = ADAM_B2 * v + (1.0 - ADAM_B2) * (g * g)
    m_hat = m_new / (1.0 - ADAM_B1 ** ADAM_STEP)
    v_hat = v_new / (1.0 - ADAM_B2 ** ADAM_STEP)
    delta = -ADAM_LR * (m_hat / (jnp.sqrt(v_hat) + ADAM_EPS) + ADAM_WD * w)
    return delta, m_new, v_new


def _sum_adam(part, land, slots, w, m, v, layer, prev, *, name):
    n_layers, rows, cols = w.shape
    tr = _tile(rows, 256, _row_mult(land.dtype))
    n_prev = 0 if prev is None else 4

    def body(slots_ref, p0_ref, p1_ref, p2_ref, p3_ref, w_ref, m_ref, v_ref, *rest):
        g_ref, d_ref, mo_ref, vo_ref = rest[n_prev:]
        g = p0_ref[...].astype(F32)
        for ref in (p1_ref, p2_ref, p3_ref):
            g = g + ref[...].astype(F32)
        delta, m_new, v_new = _adam_math(w_ref[...], g, m_ref[...], v_ref[...])
        g_ref[...] = g
        d_ref[...] = delta
        mo_ref[...] = m_new
        vo_ref[...] = v_new

    blk = pl.BlockSpec((None, tr, cols), lambda r, s: (layer, r, 0))
    slot = lambda j: pl.BlockSpec((None, None, tr, cols), lambda r, s: (s[j], 0, r, 0))
    shp = jax.ShapeDtypeStruct((n_layers, rows, cols), F32)
    grid_spec = pltpu.PrefetchScalarGridSpec(
        num_scalar_prefetch=1, grid=(rows // tr,),
        in_specs=[slot(0), slot(1), slot(2), slot(3), blk, blk, blk] + [_ANY] * n_prev,
        out_specs=(blk, blk, blk, blk))
    return pl.pallas_call(
        body, name=name, out_shape=(shp, shp, shp, shp), grid_spec=grid_spec,
        input_output_aliases={8 + i: i for i in range(n_prev)},
        compiler_params=_params(("parallel",)),
    )(slots, part, land, land, land, w, m, v, *(prev or ()))


def _sum_slots(land, *, name):
    _, _, rows, cols = land.shape
    tr = _tile(rows, 256, 8)

    def body(land_ref, g_ref):
        g = land_ref[0]
        for i in range(1, N_CHIP):
            g = g + land_ref[i]
        g_ref[...] = g

    return pl.pallas_call(
        body, name=name, out_shape=jax.ShapeDtypeStruct((rows, cols), F32), grid=(rows // tr,),
        in_specs=[pl.BlockSpec((N_CHIP, None, tr, cols), lambda r: (0, 0, r, 0))],
        out_specs=pl.BlockSpec((tr, cols), lambda r: (r, 0)),
        compiler_params=_params(("parallel",)),
    )(land)


def _adam_flat(w, g, m, v, *, name):
    rows, cols = w.shape
    tr = _tile(rows, 256, 8)

    def body(w_ref, g_ref, m_ref, v_ref, d_ref, mo_ref, vo_ref):
        delta, m_new, v_new = _adam_math(w_ref[...], g_ref[...], m_ref[...], v_ref[...])
        d_ref[...] = delta
        mo_ref[...] = m_new
        vo_ref[...] = v_new

    blk = pl.BlockSpec((tr, cols), lambda r: (r, 0))
    shp = jax.ShapeDtypeStruct((rows, cols), F32)
    return pl.pallas_call(
        body, name=name, out_shape=(shp, shp, shp), grid=(rows // tr,), in_specs=[blk] * 4, out_specs=(blk, blk, blk),
        compiler_params=_params(("parallel",)),
    )(w, g, m, v)


_BIG = (("ffn1_w_in", 1), ("ffn1_w_out", 0), ("w_in", 1), ("w_out", 0), ("ffn2_w_in", 1), ("ffn2_w_out", 0))
_SMALL = ("meta_tokens", "ffn1_norm", "mix_norm", "conv_w", "conv_b", "lru_wa", "lru_ba", "lru_wx", "lru_bx", "lru_a_param",
          "pool_w", "pool_b", "pool_scale", "ffn2_norm", "final_norm")
_SMALL_SHARD_AXIS = {"meta_tokens": 1, "conv_w": 2, "pool_w": 2}
_PACK_COLS = 1024


def _pack(arrs):
    flat = jnp.concatenate([a.reshape(-1) for a in arrs])
    unit = N_DEV * 8 * _PACK_COLS
    total = -(-flat.shape[0] // unit) * unit
    flat = jnp.pad(flat, (0, total - flat.shape[0]))
    return flat.reshape(total // _PACK_COLS, _PACK_COLS)


def _unpack(packed, shapes):
    flat = packed.reshape(-1)
    out, off = [], 0
    for s in shapes:
        size = 1
        for v in s:
            size *= v
        out.append(flat[off:off + size].reshape(s))
        off += size
    return out


def _my_shard(full, axis, dev):
    size = full.shape[axis] // N_DEV
    return lax.dynamic_slice_in_dim(full, dev * size, size, axis)


def kernel(x, meta_tokens, ffn1_norm, ffn1_w_in, ffn1_w_out, mix_norm, w_in, conv_w, conv_b, lru_wa, lru_ba, lru_wx, lru_bx, lru_a_param, pool_w, pool_b, pool_scale, w_out, ffn2_norm, ffn2_w_in, ffn2_w_out, final_norm, loss_target, m_meta_tokens, m_ffn1_norm, m_ffn1_w_in, m_ffn1_w_out, m_mix_norm, m_w_in, m_conv_w, m_conv_b, m_lru_wa, m_lru_ba, m_lru_wx, m_lru_bx, m_lru_a_param, m_pool_w, m_pool_b, m_pool_scale, m_w_out, m_ffn2_norm, m_ffn2_w_in, m_ffn2_w_out, m_final_norm, v_meta_tokens, v_ffn1_norm, v_ffn1_w_in, v_ffn1_w_out, v_mix_norm, v_w_in, v_conv_w, v_conv_b, v_lru_wa, v_lru_ba, v_lru_wx, v_lru_bx, v_lru_a_param, v_pool_w, v_pool_b, v_pool_scale, v_w_out, v_ffn2_norm, v_ffn2_w_in, v_ffn2_w_out, v_final_norm):
    names = ("meta_tokens", "ffn1_norm", "ffn1_w_in", "ffn1_w_out", "mix_norm", "w_in", "conv_w", "conv_b", "lru_wa", "lru_ba",
             "lru_wx", "lru_bx", "lru_a_param", "pool_w", "pool_b", "pool_scale", "w_out", "ffn2_norm", "ffn2_w_in", "ffn2_w_out",
             "final_norm")
    w = dict(zip(names, (meta_tokens, ffn1_norm, ffn1_w_in, ffn1_w_out, mix_norm, w_in, conv_w, conv_b, lru_wa, lru_ba, lru_wx,
                         lru_bx, lru_a_param, pool_w, pool_b, pool_scale, w_out, ffn2_norm, ffn2_w_in, ffn2_w_out, final_norm)))
    mom = dict(zip(names, (m_meta_tokens, m_ffn1_norm, m_ffn1_w_in, m_ffn1_w_out, m_mix_norm, m_w_in, m_conv_w, m_conv_b, m_lru_wa,
                           m_lru_ba, m_lru_wx, m_lru_bx, m_lru_a_param, m_pool_w, m_pool_b, m_pool_scale, m_w_out, m_ffn2_norm,
                           m_ffn2_w_in, m_ffn2_w_out, m_final_norm)))
    vel = dict(zip(names, (v_meta_tokens, v_ffn1_norm, v_ffn1_w_in, v_ffn1_w_out, v_mix_norm, v_w_in, v_conv_w, v_conv_b, v_lru_wa,
                           v_lru_ba, v_lru_wx, v_lru_bx, v_lru_a_param, v_pool_w, v_pool_b, v_pool_scale, v_w_out, v_ffn2_norm,
                           v_ffn2_w_in, v_ffn2_w_out, v_final_norm)))
    n_layers, d = ffn1_norm.shape
    n_meta = meta_tokens.shape[0]
    seq = x.shape[1]
    t_valid = n_meta + seq
    tp = -(-t_valid // ROW_ALIGN) * ROW_ALIGN
    dev = 4 * lax.axis_index("x") + 2 * lax.axis_index("y") + lax.axis_index("c")
    my_c = lax.axis_index("c").astype(jnp.int32).reshape(1)
    dev1 = dev.astype(jnp.int32).reshape(1)
    mx, my = lax.axis_index("x"), lax.axis_index("y")
    slots = jnp.stack([2 * mx + my, 2 * (1 - mx) + my, 2 * mx + (1 - my), 2 * (1 - mx) + (1 - my)]).astype(jnp.int32)

    big_axis = dict(_BIG)
    vec = lambda a: a.reshape(a.shape[0], 1, a.shape[1])
    norms = dict(ffn1=vec(ffn1_norm), mix=vec(mix_norm), ffn2=vec(ffn2_norm))
    units = [(kind, l) for l in range(n_layers) for kind in ("ffn1", "mix", "ffn2")]

    handles = {}
    state = dict(token=None, started=0, first=None)

    def behind(a):
        return a if state["first"] is None else lax.optimization_barrier((state["first"], a))[1]

    def shard(k, l):
        return behind(w[k][l:l + 1]).astype(BF16), big_axis[k] + 1

    groups = [[shard("ffn1_w_in", 0), (meta_tokens, 1)], [shard("ffn1_w_out", 0)]]

    def order():
        return [] if state["token"] is None else [state["token"]]

    def start_next():
        i = state["started"]
        if i < len(groups):
            handles[i], state["token"] = _gather_start([s for s, _ in groups[i]], [ax for _, ax in groups[i]], order(),
                                                       name=f"gather{i}_start")
            state["started"] = i + 1

    def finish_begin(i, after):
        handle = handles.pop(i)
        srcs, fulls = _gather_wait(handle, list(after) + order(), name=f"gather{i}_wait")
        fulls = [_place_own(s, z, ax, dev1, name=f"gather{i}_own{a}") for a, (s, z, ax) in enumerate(zip(srcs, fulls, handle["axes"]))]
        passing, state["token"] = _gather_pass_start(fulls, handle["axes"], name=f"gather{i}_pass")
        return i, passing

    def finish_end(ctx, after):
        i, passing = ctx
        fulls = _gather_pass_wait(passing, after, name=f"gather{i}_passwait")
        start_next()
        return fulls

    start_next()
    start_next()
    state["first"] = state["token"]
    for kind, l in units[1:]:
        if kind == "mix":
            groups.append([shard("w_in", l), shard("w_out", l)] + ([(behind(conv_w), 2), (behind(pool_w), 2)] if l == 0 else []))
        else:
            groups.append([shard(f"{kind}_w_in", l), shard(f"{kind}_w_out", l)])
    pad = jnp.zeros((tp - t_valid, d), F32)
    tgt = jnp.concatenate([jnp.zeros((n_meta, d), F32), behind(loss_target[0]), pad], axis=0)
    h0_base = jnp.concatenate([jnp.zeros((n_meta, d), F32), behind(x[0]), pad], axis=0)
    ctx = finish_begin(0, [tgt, h0_base] + [s for grp in groups[2:] for s, _ in grp])
    w_in0, meta_full = finish_end(ctx, [])
    h0 = lax.dynamic_update_slice(h0_base, meta_full, (0, 0))
    n_act = _rms_fwd(h0, norms["ffn1"], 0, name="l0ffn1_rms", deps=order())
    gu, act = _ffn_in_fwd(n_act, w_in0, 0, name="l0ffn1_in")
    (w_out0,) = finish_end(finish_begin(1, [act]), [])
    h = _ffn_fwd_out(h0, act, w_out0, "l0ffn1", order())
    gathered = {units[0]: (w_in0, w_out0)}
    saved = {units[0]: (h0, n_act, gu)}
    mp = None
    for ui, (kind, l) in enumerate(units[1:], start=1):
        tag = f"l{l}{kind}"
        ctx = finish_begin(ui + 1, [h])
        n_act = _rms_fwd(h, norms[kind], l, name=f"{tag}_rms", deps=order())
        fulls = finish_end(ctx, [n_act])
        if kind == "mix":
            if l == 0:
                mp = dict(conv_w=fulls[2], conv_b=vec(conv_b), wa=lru_wa.astype(BF16), ba=vec(lru_ba), wx=lru_wx.astype(BF16),
                          bx=vec(lru_bx), a_param=vec(lru_a_param), pool_w=fulls[3].astype(BF16), pool_b=vec(pool_b),
                          pool_scale=vec(pool_scale))
            h, saved[(kind, l)] = _mix_block_fwd(h, n_act, l, fulls[0], fulls[1], mp, tag, order())
        else:
            gu, act = _ffn_in_fwd(n_act, fulls[0], 0, name=f"{tag}_in", deps=order())
            saved[(kind, l)] = (h, n_act, gu)
            h = _ffn_fwd_out(h, act, fulls[1], tag, [])
        gathered[(kind, l)] = (fulls[0], fulls[1])
    dh, dhb, g_final, loss_local = _final_loss(h, final_norm.reshape(1, d), tgt, n_meta, t_valid, name="final_loss")
    loss = lax.psum(loss_local[0, 0], ("x", "y", "c"))

    pending = []
    big_out = {}

    def drain(after):
        while pending:
            k, l, handle = pending.pop(0)
            part, land = _rs_chips_wait(handle, after, name=f"rs_{k}{l}_wait")
            big_out[k] = _sum_adam(part, land, slots, w[k], mom[k], vel[k], l, big_out.get(k), name=f"adam_{k}{l}")

    sib_pending = []

    def to_chips(k, l, g, land):
        part = _rs_add_sibling([g], big_axis[k], land, my_c, name=f"rs_{k}{l}_add")
        handle, token = _rs_chips_start(part, name=f"rs_{k}{l}_start")
        pending.append((k, l, handle))
        return token

    def finish_sibling(after):
        token = None
        while sib_pending:
            k, l, handle = sib_pending.pop(0)
            g, land = _rs_sibling_wait(handle, after, name=f"rs_{k}{l}_sibwait")
            token = to_chips(k, l, g, land)
        return token

    def emitter(kind, l):
        last_unit = (kind, l) == units[0]

        def emit(which, g):
            if which == "w_out":
                drain([g])
            token = finish_sibling([g])
            order_after = [] if token is None else [token]
            if which == "flush":
                return order_after
            k = which if kind == "mix" else f"{kind}_{which}"
            if last_unit and which == "w_in":
                (land,) = _rs_exchange_sibling([[g]], [big_axis[k]], name=f"rs_{k}{l}_sib")
                return order_after + [to_chips(k, l, g, land)]
            handle, token = _rs_sibling_start(g, big_axis[k], name=f"rs_{k}{l}_sibstart")
            sib_pending.append((k, l, handle))
            return order_after + [token]
        return emit

    g_norm = {k: [None] * n_layers for k in ("ffn1", "mix", "ffn2")}
    g_mix = [None] * n_layers
    for kind, l in reversed(units):
        w_i, w_o = gathered[(kind, l)]
        tag = f"l{l}{kind}"
        if kind == "mix":
            dh, dhb, g_norm[kind][l], g_mix[l] = _mix_block_bwd(dh, dhb, saved[(kind, l)], norms[kind], l, w_i, w_o, mp, tag,
                                                                 emitter(kind, l), [])
        else:
            dh, dhb, g_norm[kind][l] = _ffn_bwd(dh, dhb, saved[(kind, l)], norms[kind], l, w_i, w_o, tag, emitter(kind, l), [])
    drain([dh])
    dh0 = dh
    grad_x = dh0[n_meta:t_valid][None]

    mix_key = dict(conv_w="conv_w", conv_b="conv_b", lru_wa="wa", lru_ba="ba", lru_wx="wx", lru_bx="bx", lru_a_param="a_param",
                   pool_w="pool_w", pool_b="pool_b", pool_scale="pool_scale")
    small_local = {"meta_tokens": dh0[:n_meta], "final_norm": g_final.reshape(d)}
    for k in ("ffn1", "mix", "ffn2"):
        small_local[f"{k}_norm"] = jnp.stack([g.reshape(d) for g in g_norm[k]])
    for k, mk in mix_key.items():
        small_local[k] = jnp.stack([g_mix[l][mk] for l in range(n_layers)]).reshape(
            w[k].shape if k not in _SMALL_SHARD_AXIS else small_shape_full(w[k], _SMALL_SHARD_AXIS[k]))
    small_shapes = [small_local[k].shape for k in _SMALL]
    packed = _pack([small_local[k] for k in _SMALL])
    (land,) = _rs_exchange_sibling([[packed]], [0], name="rs_small_sib")
    part = _rs_add_sibling([packed], 0, land, my_c, name="rs_small_add")
    (land2,) = _rs_exchange_chips([part], name="rs_small_chips")

    out_g, out_d, out_m, out_v = {}, {}, {}, {}
    for k, _ in _BIG:
        out_g[k], out_d[k], out_m[k], out_v[k] = big_out[k]
    small_block = _sum_slots(land2, name="rs_sum_small")
    small_full = _all_gather([small_block], [0], name="gather_small")[0]
    small_g = dict(zip(_SMALL, _unpack(small_full, small_shapes)))
    for k, ax in _SMALL_SHARD_AXIS.items():
        small_g[k] = _my_shard(small_g[k], ax, dev)
    shapes_local = [w[k].shape for k in _SMALL]
    d_p, m_p, v_p = _adam_flat(_pack([w[k] for k in _SMALL]), _pack([small_g[k] for k in _SMALL]),
                               _pack([mom[k] for k in _SMALL]), _pack([vel[k] for k in _SMALL]), name="adam_small")
    for k, dd, mm, vv in zip(_SMALL, _unpack(d_p, shapes_local), _unpack(m_p, shapes_local), _unpack(v_p, shapes_local)):
        out_g[k], out_d[k], out_m[k], out_v[k] = small_g[k], dd, mm, vv

    return (loss, grad_x, *[out_g[k] for k in names], *[out_d[k] for k in names], *[out_m[k] for k in names],
            *[out_v[k] for k in names])


def small_shape_full(w_shard, axis):
    return w_shard.shape[:axis] + (N_DEV * w_shard.shape[axis],) + w_shard.shape[axis + 1:]
```

```python
import functools

import jax
import jax.numpy as jnp
from jax import lax
from jax.experimental import pallas as pl
from jax.experimental.pallas import tpu as pltpu

F32 = jnp.float32
BF16 = jnp.bfloat16
MESH = pl.DeviceIdType.MESH

N_DEV = 8
N_CHIP = 4
RMS_EPS = 1e-6
LRU_C = 8.0
CONV_WIDTH = 4
POOL_WINDOWS = (2, 4, 8, 16)
HIST = 16
ADAM_LR = 0.001
ADAM_B1 = 0.9
ADAM_B2 = 0.999
ADAM_EPS = 1e-08
ADAM_WD = 0.01
ADAM_STEP = 10
ROW_ALIGN = 128
V7X_VMEM_LIMIT = 56 * 1024 * 1024
BF16_ROWS = 16
K_TILE = 5632
DN_K_TILE = 2816
WGRAD_TILE = 2048
SCAN_UNROLL = 8
EPILOGUE_ROWS = 528
LANES = 128


def _tile(n, cap, mult=BF16_ROWS):
    best = None
    d = mult
    while d <= min(n, cap):
        if n % d == 0:
            best = d
        d += mult
    if best is None:
        raise ValueError(f"no tile for {n} (cap {cap}, multiple of {mult})")
    return best


def _row_mult(dtype):
    return 8 * 4 // jnp.dtype(dtype).itemsize


def _params(sem=None):
    return pltpu.CompilerParams(dimension_semantics=sem, vmem_limit_bytes=V7X_VMEM_LIMIT)


def _dot(a, b, mode):
    dims = {"nn": ((1,), (0,)), "nt": ((1,), (1,)), "tn": ((0,), (0,))}[mode]
    return lax.dot_general(a, b, (dims, ((), ())), preferred_element_type=F32)


def _sigmoid(x):
    return 1.0 / (1.0 + jnp.exp(-x))


_GELU_C = 0.7978845608028654
_GELU_K = 0.044715


def _gelu(x):
    return 0.5 * x * (1.0 + jnp.tanh(_GELU_C * (x + _GELU_K * x * x * x)))


def _gelu_and_grad(x):
    x2 = x * x
    th = jnp.tanh(_GELU_C * x * (1.0 + _GELU_K * x2))
    half = 0.5 * (1.0 + th)
    return x * half, half + 0.5 * x * (1.0 - th * th) * _GELU_C * (1.0 + 3.0 * _GELU_K * x2)


def _neg_expm1(x):
    p = 1.0 / 5040.0
    for c in (1.0 / 720.0, 1.0 / 120.0, 1.0 / 24.0, 1.0 / 6.0, 0.5, 1.0):
        p = p * x + c
    return jnp.where(x > -0.25, -(x * p), 1.0 - jnp.exp(x))


def _softplus_neg(p):
    e = jnp.exp(-jnp.abs(p))
    u = 1.0 + e
    l1p = jnp.where(u == 1.0, e, jnp.log(u) * e / (u - 1.0 + (u == 1.0).astype(F32)))
    return jnp.maximum(-p, 0.0) + l1p


def _operand_spec(arr, br, bc, ridx, cidx, layer, split):
    if split:
        ncb = arr.shape[2] // bc
        return pl.BlockSpec((None, br, bc), lambda i, j, k: (cidx(i, j, k) // ncb, ridx(i, j, k), cidx(i, j, k) % ncb))
    if layer is not None:
        return pl.BlockSpec((None, br, bc), lambda i, j, k: (layer, ridx(i, j, k), cidx(i, j, k)))
    return pl.BlockSpec((br, bc), lambda i, j, k: (ridx(i, j, k), cidx(i, j, k)))


def _matmul(a, b, *, mode, m, n, kdim, tm, tn, tk, out_dtype, name, scale=None, residual=None,
            a_layer=None, b_layer=None, a_split=False, b_split=False, deps=()):
    nk = kdim // tk
    I = lambda i, j, k: i
    J = lambda i, j, k: j
    K = lambda i, j, k: k
    if mode == "nn":
        a_spec = _operand_spec(a, tm, tk, I, K, a_layer, a_split)
        b_spec = _operand_spec(b, tk, tn, K, J, b_layer, b_split)
    elif mode == "nt":
        a_spec = _operand_spec(a, tm, tk, I, K, a_layer, a_split)
        b_spec = _operand_spec(b, tn, tk, J, K, b_layer, b_split)
    else:
        a_spec = _operand_spec(a, tk, tm, K, I, a_layer, a_split)
        b_spec = _operand_spec(b, tk, tn, K, J, b_layer, b_split)
    in_specs = [a_spec, b_spec]
    operands = [a, b]
    if residual is not None:
        in_specs.append(pl.BlockSpec((tm, tn), lambda i, j, k: (i, j)))
        operands.append(residual)
    in_specs += [pl.BlockSpec(memory_space=pl.ANY)] * len(deps)
    operands += list(deps)
    n_in = len(operands)

    def finish(acc, res_ref, o_ref):
        if scale is not None:
            acc = acc * scale
        if res_ref is not None:
            acc = acc + res_ref[...]
        o_ref[...] = acc.astype(o_ref.dtype)

    def body(*refs):
        a_ref, b_ref = refs[0], refs[1]
        res_ref = refs[2] if residual is not None else None
        o_ref = refs[n_in]
        part = _dot(a_ref[...], b_ref[...], mode)
        if nk == 1:
            finish(part, res_ref, o_ref)
        else:
            acc_ref = refs[-1]
            k = pl.program_id(2)

            @pl.when(k == 0)
            def _():
                acc_ref[...] = part

            @pl.when(k > 0)
            def _():
                acc_ref[...] += part

            @pl.when(k == nk - 1)
            def _():
                finish(acc_ref[...], res_ref, o_ref)

    return pl.pallas_call(
        body, name=name,
        out_shape=jax.ShapeDtypeStruct((m, n), out_dtype),
        grid=(m // tm, n // tn, nk),
        in_specs=in_specs,
        out_specs=pl.BlockSpec((tm, tn), lambda i, j, k: (i, j)),
        scratch_shapes=[pltpu.VMEM((tm, tn), F32)] if nk > 1 else [],
        compiler_params=_params(("parallel", "parallel", "arbitrary")),
    )(*operands)


def _ffn_in_fwd(n_act, w_in, layer, *, name, deps=()):
    tp, d = n_act.shape
    f = w_in.shape[2] // 2
    tm = _tile(tp, 2112)
    tn = _tile(f, 512, LANES)
    nj = f // tn

    ch = _tile(tm, EPILOGUE_ROWS)

    def body(n_ref, wg_ref, wu_ref, *rest):
        gu_ref, a_ref = rest[len(deps):]
        for r in range(tm // ch):
            rows = pl.ds(r * ch, ch)
            x = n_ref[rows, :]
            g = _dot(x, wg_ref[...], "nn")
            u = _dot(x, wu_ref[...], "nn")
            gu_ref[0, rows, :] = g.astype(BF16)
            gu_ref[1, rows, :] = u.astype(BF16)
            a_ref[rows, :] = (g * _sigmoid(g) * u).astype(BF16)

    return pl.pallas_call(
        body, name=name,
        out_shape=(jax.ShapeDtypeStruct((2, tp, f), BF16), jax.ShapeDtypeStruct((tp, f), BF16)),
        grid=(tp // tm, nj),
        in_specs=[pl.BlockSpec((tm, d), lambda i, j: (i, 0)),
                  pl.BlockSpec((None, d, tn), lambda i, j: (layer, 0, j)),
                  pl.BlockSpec((None, d, tn), lambda i, j: (layer, 0, j + nj))] + [pl.BlockSpec(memory_space=pl.ANY)] * len(deps),
        out_specs=(pl.BlockSpec((2, tm, tn), lambda i, j: (0, i, j)), pl.BlockSpec((tm, tn), lambda i, j: (i, j))),
        compiler_params=_params(("parallel", "arbitrary")),
    )(n_act, w_in, w_in, *deps)


def _ffn_bwd_act(dhb, w_out, layer, gu, *, name, deps=()):
    tp, d = dhb.shape
    f = w_out.shape[1]
    tm = _tile(tp, 2112)
    tn = _tile(f, 512, LANES)

    ch = _tile(tm, EPILOGUE_ROWS)

    def body(dh_ref, w_ref, gu_ref, *rest):
        dz_ref, a_ref = rest[len(deps):]
        for r in range(tm // ch):
            rows = pl.ds(r * ch, ch)
            da = 0.5 * _dot(dh_ref[rows, :], w_ref[...], "nt")
            g = gu_ref[0, rows, :].astype(F32)
            u = gu_ref[1, rows, :].astype(F32)
            s = _sigmoid(g)
            sg = g * s
            dz_ref[0, rows, :] = (da * u * (s * (1.0 + g * (1.0 - s)))).astype(BF16)
            dz_ref[1, rows, :] = (da * sg).astype(BF16)
            a_ref[rows, :] = (sg * u).astype(BF16)

    return pl.pallas_call(
        body, name=name,
        out_shape=(jax.ShapeDtypeStruct((2, tp, f), BF16), jax.ShapeDtypeStruct((tp, f), BF16)),
        grid=(tp // tm, f // tn),
        in_specs=[pl.BlockSpec((tm, d), lambda i, j: (i, 0)),
                  pl.BlockSpec((None, tn, d), lambda i, j: (layer, j, 0)),
                  pl.BlockSpec((2, tm, tn), lambda i, j: (0, i, j))] + [pl.BlockSpec(memory_space=pl.ANY)] * len(deps),
        out_specs=(pl.BlockSpec((2, tm, tn), lambda i, j: (0, i, j)), pl.BlockSpec((tm, tn), lambda i, j: (i, j))),
        compiler_params=_params(("parallel", "arbitrary")),
    )(dhb, w_out, gu, *deps)


def _rms_fwd(h, g, layer, *, name, deps=()):
    tp, d = h.shape
    tr = _tile(tp, 528)

    def body(h_ref, g_ref, *rest):
        n_ref = rest[len(deps)]
        x = h_ref[...]
        r = lax.rsqrt(jnp.mean(x * x, axis=-1, keepdims=True) + RMS_EPS)
        n_ref[...] = (x * r * g_ref[...]).astype(BF16)

    return pl.pallas_call(
        body, name=name, out_shape=jax.ShapeDtypeStruct((tp, d), BF16), grid=(tp // tr,),
        in_specs=[pl.BlockSpec((tr, d), lambda i: (i, 0)), pl.BlockSpec((None, 1, d), lambda i: (layer, 0, 0))]
        + [pl.BlockSpec(memory_space=pl.ANY)] * len(deps),
        out_specs=pl.BlockSpec((tr, d), lambda i: (i, 0)),
        compiler_params=_params(("parallel",)),
    )(h, g, *deps)


def _rms_bwd(h, g, layer, dn, dres, *, name, deps=()):
    tp, d = h.shape
    tr = _tile(tp, 528)

    def body(h_ref, g_ref, dn_ref, dres_ref, *rest):
        dh_ref, dhb_ref, dg_ref = rest[len(deps):]
        x = h_ref[...]
        r = lax.rsqrt(jnp.mean(x * x, axis=-1, keepdims=True) + RMS_EPS)
        xhat = x * r
        dn_v = dn_ref[...]
        dxhat = dn_v * g_ref[...]
        dh = dres_ref[...] + r * (dxhat - xhat * jnp.mean(dxhat * xhat, axis=-1, keepdims=True))
        dh_ref[...] = dh
        dhb_ref[...] = dh.astype(BF16)
        part = jnp.sum(dn_v * xhat, axis=0, keepdims=True)

        @pl.when(pl.program_id(0) == 0)
        def _():
            dg_ref[...] = part

        @pl.when(pl.program_id(0) > 0)
        def _():
            dg_ref[...] += part

    return pl.pallas_call(
        body, name=name,
        out_shape=(jax.ShapeDtypeStruct((tp, d), F32), jax.ShapeDtypeStruct((tp, d), BF16), jax.ShapeDtypeStruct((1, d), F32)),
        grid=(tp // tr,),
        in_specs=[pl.BlockSpec((tr, d), lambda i: (i, 0)), pl.BlockSpec((None, 1, d), lambda i: (layer, 0, 0)),
                  pl.BlockSpec((tr, d), lambda i: (i, 0)), pl.BlockSpec((tr, d), lambda i: (i, 0))]
        + [pl.BlockSpec(memory_space=pl.ANY)] * len(deps),
        out_specs=(pl.BlockSpec((tr, d), lambda i: (i, 0)), pl.BlockSpec((tr, d), lambda i: (i, 0)),
                   pl.BlockSpec((1, d), lambda i: (0, 0))),
        compiler_params=_params(("arbitrary",)),
    )(h, g, dn, dres, *deps)


def _final_loss(h, g, tgt, n_meta, t_valid, *, name):
    tp, d = h.shape
    tr = _tile(tp, 528)

    def body(h_ref, g_ref, t_ref, dh_ref, dhb_ref, dg_ref, loss_ref):
        i = pl.program_id(0)
        x = h_ref[...]
        r = lax.rsqrt(jnp.mean(x * x, axis=-1, keepdims=True) + RMS_EPS)
        xhat = x * r
        gv = g_ref[...]
        row = i * tr + lax.broadcasted_iota(jnp.int32, (tr, 1), 0)
        valid = jnp.logical_and(row >= n_meta, row < t_valid)
        err = jnp.where(valid, xhat * gv - t_ref[...], 0.0)
        dy = err * (1.0 / d)
        dxhat = dy * gv
        dh = r * (dxhat - xhat * jnp.mean(dxhat * xhat, axis=-1, keepdims=True))
        dh_ref[...] = dh
        dhb_ref[...] = dh.astype(BF16)
        dg_part = jnp.sum(dy * xhat, axis=0, keepdims=True)
        loss_part = jnp.sum(jnp.sum(err * err, axis=1, keepdims=True), axis=0, keepdims=True) * (0.5 / d)

        @pl.when(i == 0)
        def _():
            dg_ref[...] = dg_part
            loss_ref[...] = loss_part

        @pl.when(i > 0)
        def _():
            dg_ref[...] += dg_part
            loss_ref[...] += loss_part

    return pl.pallas_call(
        body, name=name,
        out_shape=(jax.ShapeDtypeStruct((tp, d), F32), jax.ShapeDtypeStruct((tp, d), BF16),
                   jax.ShapeDtypeStruct((1, d), F32), jax.ShapeDtypeStruct((1, 1), F32)),
        grid=(tp // tr,),
        in_specs=[pl.BlockSpec((tr, d), lambda i: (i, 0)), pl.BlockSpec((1, d), lambda i: (0, 0)),
                  pl.BlockSpec((tr, d), lambda i: (i, 0))],
        out_specs=(pl.BlockSpec((tr, d), lambda i: (i, 0)), pl.BlockSpec((tr, d), lambda i: (i, 0)),
                   pl.BlockSpec((1, d), lambda i: (0, 0)), pl.BlockSpec((1, 1), lambda i: (0, 0))),
        compiler_params=_params(("arbitrary",)),
    )(h, g, tgt)


def _mix_param_specs(mp, layer, imap):
    def vec(a):
        return pl.BlockSpec((None,) + a.shape[1:], lambda *g: (layer,) + (0,) * (a.ndim - 1))
    return [vec(mp[k]) for k in ("conv_w", "conv_b", "wa", "ba", "wx", "bx", "a_param", "pool_w", "pool_b", "pool_scale")]


def _mix_param_list(mp):
    return [mp[k] for k in ("conv_w", "conv_b", "wa", "ba", "wx", "bx", "a_param", "pool_w", "pool_b", "pool_scale")]


def _lru_gates(xc, wa_h, ba_h, wx_h, bx_h, sp_h):
    xb = xc.astype(BF16)
    ra = _sigmoid(_dot(xb, wa_h, "nn") + ba_h)
    ii = _sigmoid(_dot(xb, wx_h, "nn") + bx_h)
    la = -LRU_C * ra * sp_h
    return ra, ii, la, jnp.exp(la)


def _shifted(x, k):
    return x if k == 0 else pltpu.roll(x, k % x.shape[0], axis=0)


def _conv_taps(xbuf, sl, tc):
    x = xbuf[:, sl]
    return [_shifted(x, CONV_WIDTH - 1 - k)[HIST:HIST + tc] for k in range(CONV_WIDTH)]


def _conv_fwd(taps, cw_ref, cb_ref, sl):
    xc = cb_ref[:, sl]
    for k in range(CONV_WIDTH):
        xc = xc + cw_ref[k:k + 1, sl] * taps[k]
    return xc


def _window_sum(x, win, direction):
    s = x
    step = 1
    while step < win:
        s = s + _shifted(s, direction * step)
        step *= 2
    return s


def _pool_delta(pbuf, cols, win, t0, tc):
    x = pbuf[:, cols]
    u = x[HIST:HIST + tc]
    s = _window_sum(x, win, 1)[HIST:HIST + tc]
    t = t0 + lax.broadcasted_iota(jnp.int32, (tc, 1), 0)
    inv_cnt = 1.0 / jnp.minimum(t + 1, win).astype(F32)
    return s * inv_cnt - u, inv_cnt


def _mix_fwd(z, mp, layer, *, name):
    tp = z.shape[0]
    dl = z.shape[1] // 3
    n_heads, hd = mp["wa"].shape[1], mp["wa"].shape[2]
    n_groups, gd = mp["pool_w"].shape[1], mp["pool_w"].shape[2]
    tc = _tile(tp, 384)

    def body(z_ref, cw_ref, cb_ref, wa_ref, ba_ref, wx_ref, bx_ref, ap_ref, pw_ref, pb_ref, ps_ref,
             m_ref, hs_ref, e_ref, xbuf, pbuf, a_s, b_s, hcar):
        i = pl.program_id(0)

        @pl.when(i == 0)
        def _():
            xbuf[pl.ds(0, HIST), :] = jnp.zeros((HIST, dl), F32)
            pbuf[pl.ds(0, HIST), :] = jnp.zeros((HIST, dl), F32)
            hcar[...] = jnp.zeros((1, dl), F32)

        @pl.when(i > 0)
        def _():
            xbuf[pl.ds(0, HIST), :] = xbuf[pl.ds(tc, HIST), :]
            pbuf[pl.ds(0, HIST), :] = pbuf[pl.ds(tc, HIST), :]

        xbuf[pl.ds(HIST, tc), :] = z_ref[:, pl.ds(0, dl)]
        pbuf[pl.ds(HIST, tc), :] = z_ref[:, pl.ds(2 * dl, dl)]
        sp = _softplus_neg(ap_ref[...])
        for h in range(n_heads):
            sl = pl.ds(h * hd, hd)
            xc = _conv_fwd(_conv_taps(xbuf, sl, tc), cw_ref, cb_ref, sl)
            _, ii, la, a = _lru_gates(xc, wa_ref[h], ba_ref[:, sl], wx_ref[h], bx_ref[:, sl], sp[:, h * hd:(h + 1) * hd])
            e = _neg_expm1(2.0 * la)
            a_s[:, sl] = a
            e_ref[:, sl] = e
            b_s[:, sl] = jnp.sqrt(e) * ii * xc

        def step(t, hprev):
            hnew = a_s[pl.ds(t, 1), :] * hprev + b_s[pl.ds(t, 1), :]
            hs_ref[pl.ds(t, 1), :] = hnew
            return hnew

        hcar[...] = lax.fori_loop(0, tc, step, hcar[...], unroll=SCAN_UNROLL)
        for h in range(n_heads):
            sl = pl.ds(h * hd, hd)
            m_ref[:, sl] = (hs_ref[:, sl] * _gelu(z_ref[:, pl.ds(dl + h * hd, hd)])).astype(BF16)
        for g in range(n_groups):
            cols = pl.ds(g * gd, gd)
            dlt, _ = _pool_delta(pbuf, cols, POOL_WINDOWS[g], i * tc, tc)
            q = _dot(dlt.astype(BF16), pw_ref[g], "nn") + pb_ref[:, cols]
            m_ref[:, pl.ds(dl + g * gd, gd)] = (q * ps_ref[:, cols]).astype(BF16)

    return pl.pallas_call(
        body, name=name,
        out_shape=(jax.ShapeDtypeStruct((tp, 2 * dl), BF16), jax.ShapeDtypeStruct((tp, dl), F32), jax.ShapeDtypeStruct((tp, dl), F32)),
        grid=(tp // tc,),
        in_specs=[pl.BlockSpec((tc, 3 * dl), lambda i: (i, 0))] + _mix_param_specs(mp, layer, None),
        out_specs=(pl.BlockSpec((tc, 2 * dl), lambda i: (i, 0)), pl.BlockSpec((tc, dl), lambda i: (i, 0)),
                   pl.BlockSpec((tc, dl), lambda i: (i, 0))),
        scratch_shapes=[pltpu.VMEM((HIST + tc, dl), F32), pltpu.VMEM((HIST + tc, dl), F32),
                        pltpu.VMEM((tc, dl), F32), pltpu.VMEM((tc, dl), F32), pltpu.VMEM((1, dl), F32)],
        compiler_params=_params(("arbitrary",)),
    )(z, *_mix_param_list(mp))


_MIX_GRADS = ("conv_w", "conv_b", "wa", "ba", "wx", "bx", "a_param", "pool_w", "pool_b", "pool_scale")


def _mix_bwd(z, hs, es, dm, mp, layer, *, name):
    tp = z.shape[0]
    dl = z.shape[1] // 3
    n_heads, hd = mp["wa"].shape[1], mp["wa"].shape[2]
    n_groups, gd = mp["pool_w"].shape[1], mp["pool_w"].shape[2]
    tc = _tile(tp, 384)
    nc = tp // tc
    per = tc // HIST

    def body(z_ref, zp_ref, hs_ref, hsp_ref, e_ref, dm_ref, cw_ref, cb_ref, wa_ref, ba_ref, wx_ref, bx_ref, ap_ref, pw_ref, pb_ref,
             ps_ref, dz_ref, dcw_ref, dcb_ref, dwa_ref, dba_ref, dwx_ref, dbx_ref, dap_ref, dpw_ref, dpb_ref, dps_ref,
             xbuf, pbuf, hbuf, dxbuf, ddbuf, a_s, lam_s, ra_s, ii_s, xc_s, ccar):
        i = pl.program_id(0)
        ci = nc - 1 - i

        @pl.when(i == 0)
        def _():
            dxbuf[pl.ds(tc, HIST), :] = jnp.zeros((HIST, dl), F32)
            ddbuf[pl.ds(tc, HIST), :] = jnp.zeros((HIST, dl), F32)
            ccar[...] = jnp.zeros((1, dl), F32)
            for ref in (dcw_ref, dcb_ref, dwa_ref, dba_ref, dwx_ref, dbx_ref, dap_ref, dpw_ref, dpb_ref, dps_ref):
                ref[...] = jnp.zeros(ref.shape, F32)

        @pl.when(ci == 0)
        def _():
            xbuf[pl.ds(0, HIST), :] = jnp.zeros((HIST, dl), F32)
            pbuf[pl.ds(0, HIST), :] = jnp.zeros((HIST, dl), F32)
            hbuf[pl.ds(0, HIST), :] = jnp.zeros((HIST, dl), F32)

        @pl.when(ci > 0)
        def _():
            xbuf[pl.ds(0, HIST), :] = zp_ref[:, pl.ds(0, dl)]
            pbuf[pl.ds(0, HIST), :] = zp_ref[:, pl.ds(2 * dl, dl)]
            hbuf[pl.ds(0, HIST), :] = hsp_ref[...]

        xbuf[pl.ds(HIST, tc), :] = z_ref[:, pl.ds(0, dl)]
        pbuf[pl.ds(HIST, tc), :] = z_ref[:, pl.ds(2 * dl, dl)]
        hbuf[pl.ds(HIST, tc), :] = hs_ref[...]
        sp = _softplus_neg(ap_ref[...])

        for h in range(n_heads):
            sl = pl.ds(h * hd, hd)
            xc = _conv_fwd(_conv_taps(xbuf, sl, tc), cw_ref, cb_ref, sl)
            ra, ii, _, a = _lru_gates(xc, wa_ref[h], ba_ref[:, sl], wx_ref[h], bx_ref[:, sl], sp[:, h * hd:(h + 1) * hd])
            a_s[:, sl] = a
            ra_s[:, sl] = ra
            ii_s[:, sl] = ii
            xc_s[:, sl] = xc
            gel, gel_grad = _gelu_and_grad(z_ref[:, pl.ds(dl + h * hd, hd)])
            dya = dm_ref[:, sl]
            lam_s[:, sl] = dya * gel
            dz_ref[:, pl.ds(dl + h * hd, hd)] = (dya * hs_ref[:, sl] * gel_grad).astype(BF16)

        def step(r, carry):
            t = tc - 1 - r
            lam = lam_s[pl.ds(t, 1), :] + carry
            lam_s[pl.ds(t, 1), :] = lam
            return a_s[pl.ds(t, 1), :] * lam

        ccar[...] = lax.fori_loop(0, tc, step, ccar[...], unroll=SCAN_UNROLL)

        for h in range(n_heads):
            sl = pl.ds(h * hd, hd)
            sp_h = sp[:, h * hd:(h + 1) * hd]
            lam = lam_s[:, sl]
            a = a_s[:, sl]
            ra = ra_s[:, sl]
            ii = ii_s[:, sl]
            xc = xc_s[:, sl]
            e = e_ref[:, sl]
            inv_mult = lax.rsqrt(e)
            mult = e * inv_mult
            hprev = _shifted(hbuf[:, sl], 1)[HIST:HIST + tc]
            lam_i = lam * ii
            lam_m = lam * mult
            dla = lam * hprev * a - lam_i * xc * (a * a) * inv_mult
            dla_r = dla * (-LRU_C) * ra
            dap_ref[:, sl] += jnp.sum(dla_r, axis=0, keepdims=True)
            dpa = dla_r * sp_h * (1.0 - ra)
            dpx = lam_m * xc * ii * (1.0 - ii)
            dba_ref[:, sl] += jnp.sum(dpa, axis=0, keepdims=True)
            dbx_ref[:, sl] += jnp.sum(dpx, axis=0, keepdims=True)
            xb = xc.astype(BF16)
            dpa_b = dpa.astype(BF16)
            dpx_b = dpx.astype(BF16)
            dwa_ref[h] += _dot(xb, dpa_b, "tn")
            dwx_ref[h] += _dot(xb, dpx_b, "tn")
            dxc = lam_m * ii + _dot(dpa_b, wa_ref[h], "nt") + _dot(dpx_b, wx_ref[h], "nt")
            dxbuf[pl.ds(0, tc), sl] = dxc
            dcb_ref[:, sl] += jnp.sum(dxc, axis=0, keepdims=True)
            taps = _conv_taps(xbuf, sl, tc)
            dx_all = dxbuf[:, sl]
            dzx = jnp.zeros((tc, hd), F32)
            for k in range(CONV_WIDTH):
                dcw_ref[k:k + 1, sl] += jnp.sum(dxc * taps[k], axis=0, keepdims=True)
                dzx = dzx + cw_ref[k:k + 1, sl] * _shifted(dx_all, k - (CONV_WIDTH - 1))[0:tc]
            dz_ref[:, sl] = dzx.astype(BF16)
            dxbuf[pl.ds(tc, HIST), sl] = dxbuf[pl.ds(0, HIST), sl]

        for g in range(n_groups):
            cols = pl.ds(g * gd, gd)
            win = POOL_WINDOWS[g]
            dlt, inv_cnt = _pool_delta(pbuf, cols, win, ci * tc, tc)
            db = dlt.astype(BF16)
            q = _dot(db, pw_ref[g], "nn") + pb_ref[:, cols]
            dyb = dm_ref[:, pl.ds(dl + g * gd, gd)]
            dps_ref[:, cols] += jnp.sum(dyb * q, axis=0, keepdims=True)
            dq = dyb * ps_ref[:, cols]
            dpb_ref[:, cols] += jnp.sum(dq, axis=0, keepdims=True)
            dqb = dq.astype(BF16)
            dpw_ref[g] += _dot(db, dqb, "tn")
            dd = _dot(dqb, pw_ref[g], "nt")
            ddbuf[pl.ds(0, tc), cols] = dd * inv_cnt
            dzp = _window_sum(ddbuf[:, cols], win, -1)[0:tc] - dd
            dz_ref[:, pl.ds(2 * dl + g * gd, gd)] = dzp.astype(BF16)
            ddbuf[pl.ds(tc, HIST), cols] = ddbuf[pl.ds(0, HIST), cols]

        @pl.when(i == nc - 1)
        def _():
            dap_ref[...] = dap_ref[...] * (-_sigmoid(-ap_ref[...]))

    rev = lambda i: (nc - 1 - i, 0)
    prev = lambda i: (jnp.maximum((nc - 1 - i) * per - 1, 0), 0)
    const = lambda a: pl.BlockSpec(a.shape[1:], lambda i: (0,) * (a.ndim - 1))
    plist = _mix_param_list(mp)
    grad_shapes = [jax.ShapeDtypeStruct(a.shape[1:], F32) for a in plist]
    buf = lambda rows: pltpu.VMEM((rows, dl), F32)
    outs = pl.pallas_call(
        body, name=name,
        out_shape=[jax.ShapeDtypeStruct((tp, 3 * dl), BF16)] + grad_shapes,
        grid=(nc,),
        in_specs=[pl.BlockSpec((tc, 3 * dl), rev), pl.BlockSpec((HIST, 3 * dl), prev),
                  pl.BlockSpec((tc, dl), rev), pl.BlockSpec((HIST, dl), prev), pl.BlockSpec((tc, dl), rev),
                  pl.BlockSpec((tc, 2 * dl), rev)] + _mix_param_specs(mp, layer, None),
        out_specs=[pl.BlockSpec((tc, 3 * dl), rev)] + [const(a) for a in plist],
        scratch_shapes=[buf(HIST + tc), buf(HIST + tc), buf(HIST + tc), buf(tc + HIST), buf(tc + HIST),
                        buf(tc), buf(tc), buf(tc), buf(tc), buf(tc), buf(1)],
        compiler_params=_params(("arbitrary",)),
    )(z, z, hs, hs, es, dm, *plist)
    return outs[0], dict(zip(_MIX_GRADS, outs[1:]))


def _ffn_fwd_out(h, act, w_out, tag, deps):
    tp, d = h.shape
    f = w_out.shape[1]
    return _matmul(act, w_out, mode="nn", m=tp, n=d, kdim=f, tm=_tile(tp, 1056), tn=_tile(d, 512, LANES),
                   tk=_tile(f, K_TILE, LANES), out_dtype=F32, scale=0.5, residual=h, b_layer=0, name=f"{tag}_out", deps=deps)


def _ffn_bwd(dh, dhb, saved, norm, layer, w_in, w_out, tag, emit, deps):
    h, n_act, gu = saved
    tp, d = h.shape
    f = w_out.shape[1]
    dz, act = _ffn_bwd_act(dhb, w_out, 0, gu, name=f"{tag}_bact", deps=deps)
    tok = emit("w_out", _matmul(act, dhb, mode="tn", m=f, n=d, kdim=tp, tm=_tile(f, 512, LANES), tn=_tile(d, WGRAD_TILE, LANES), tk=tp,
                                out_dtype=BF16, scale=0.5, name=f"{tag}_dwout"))
    tok = emit("w_in", _matmul(n_act, dz, mode="tn", m=d, n=2 * f, kdim=tp, tm=_tile(d, WGRAD_TILE, LANES), tn=_tile(f, 512, LANES),
                               tk=tp, out_dtype=BF16, b_split=True, name=f"{tag}_dwin", deps=tok))
    dn = _matmul(dz, w_in, mode="nt", m=tp, n=d, kdim=2 * f, tm=_tile(tp, 1056), tn=_tile(d, 1024, LANES), tk=_tile(f, DN_K_TILE, LANES),
                 out_dtype=F32, a_split=True, b_layer=0, name=f"{tag}_dn", deps=tok)
    return _rms_bwd(h, norm, layer, dn, dh, name=f"{tag}_brms", deps=emit("flush", dn))


def _mix_block_fwd(h, n_act, layer, w_in, w_out, mp, tag, deps):
    tp, d = h.shape
    d_in = w_in.shape[2]
    z = _matmul(n_act, w_in, mode="nn", m=tp, n=d_in, kdim=d, tm=_tile(tp, 2112), tn=_tile(d_in, 512, LANES), tk=d,
                out_dtype=F32, b_layer=0, name=f"{tag}_in", deps=deps)
    m_act, hs, es = _mix_fwd(z, mp, layer, name=f"{tag}_mix")
    h_out = _matmul(m_act, w_out, mode="nn", m=tp, n=d, kdim=d, tm=_tile(tp, 2112), tn=_tile(d, 512, LANES), tk=d,
                    out_dtype=F32, residual=h, b_layer=0, name=f"{tag}_out")
    return h_out, (h, n_act, z, hs, es, m_act)


def _mix_block_bwd(dh, dhb, saved, norm, layer, w_in, w_out, mp, tag, emit, deps):
    h, n_act, z, hs, es, m_act = saved
    tp, d = h.shape
    d_in = w_in.shape[2]
    dm = _matmul(dhb, w_out, mode="nt", m=tp, n=d, kdim=d, tm=_tile(tp, 2112), tn=_tile(d, 512, LANES), tk=d,
                 out_dtype=F32, b_layer=0, name=f"{tag}_dm", deps=deps)
    tok = emit("w_out", _matmul(m_act, dhb, mode="tn", m=d, n=d, kdim=tp, tm=_tile(d, 512, LANES), tn=_tile(d, WGRAD_TILE, LANES), tk=tp,
                                out_dtype=BF16, name=f"{tag}_dwout"))
    dz, g_mix = _mix_bwd(z, hs, es, dm, mp, layer, name=f"{tag}_bmix")
    tok = emit("w_in", _matmul(n_act, dz, mode="tn", m=d, n=d_in, kdim=tp, tm=_tile(d, WGRAD_TILE, LANES), tn=_tile(d_in, 512, LANES),
                               tk=tp, out_dtype=BF16, name=f"{tag}_dwin", deps=tok))
    dn = _matmul(dz, w_in, mode="nt", m=tp, n=d, kdim=d_in, tm=_tile(tp, 1056), tn=_tile(d, 1024, LANES), tk=d_in,
                 out_dtype=F32, b_layer=0, name=f"{tag}_dn", deps=tok)
    dh_in, dhb_in, g_norm = _rms_bwd(h, norm, layer, dn, dh, name=f"{tag}_brms", deps=emit("flush", dn))
    return dh_in, dhb_in, g_norm, g_mix


def _mesh_pos():
    x, y, c = lax.axis_index("x"), lax.axis_index("y"), lax.axis_index("c")
    chips = [(1 - x, y), (x, 1 - y), (1 - x, 1 - y)]
    return x, y, c, chips


def _block(ref, axis, j, size):
    idx = [slice(None)] * len(ref.shape)
    idx[axis] = pl.ds(j * size, size)
    return ref.at[tuple(idx)]


_HBM = pl.BlockSpec(memory_space=pltpu.HBM)
_SEM = pl.BlockSpec(memory_space=pltpu.SEMAPHORE)
_ANY = pl.BlockSpec(memory_space=pl.ANY)
_EFFECT = pltpu.SideEffectType.DATAFLOW_SIDE_EFFECTING
_N_OUT = 4


def _in_hbm(a):
    return pltpu.with_memory_space_constraint(a, pltpu.HBM)


def _gather_start(shards, axes, deps, *, name):
    n = len(shards)
    sizes = [s.shape[ax] for s, ax in zip(shards, axes)]
    full_shapes = [s.shape[:ax] + (N_DEV * s.shape[ax],) + s.shape[ax + 1:] for s, ax in zip(shards, axes)]

    def body(*refs):
        srcs, lands = refs[:n], refs[n:2 * n]
        send_sems, recv_sems = refs[2 * n + len(deps)], refs[2 * n + len(deps) + 1]
        token = refs[-1]
        x, y, c, chips = _mesh_pos()
        targets = [(x, y, 1 - c)] + [(*chip, c) for chip in chips]
        for a in range(n):
            dst = _block(lands[a], axes[a], 4 * x + 2 * y + c, sizes[a])
            for k, to in enumerate(targets):
                pltpu.make_async_remote_copy(src_ref=srcs[a], dst_ref=dst, send_sem=send_sems.at[_N_OUT * a + k],
                                             recv_sem=recv_sems.at[_N_OUT * a + k],
                                             device_id=to, device_id_type=MESH).start()
        token[...] = jnp.zeros(token.shape, F32)

    lands0 = [lax.empty(fs, s.dtype) for fs, s in zip(full_shapes, shards)]
    outs = pl.pallas_call(
        body, name=name,
        out_shape=(pltpu.SemaphoreType.DMA((n * _N_OUT,)), pltpu.SemaphoreType.DMA((n * _N_OUT,)),
                   *[pltpu.HBM(s.shape, s.dtype) for s in shards], *[pltpu.HBM(fs, s.dtype) for fs, s in zip(full_shapes, shards)],
                   jax.ShapeDtypeStruct((8, LANES), F32)),
        in_specs=[_HBM] * (2 * n) + [_ANY] * len(deps),
        out_specs=(_SEM, _SEM, *[_HBM] * (2 * n), pl.BlockSpec(memory_space=pltpu.VMEM)),
        input_output_aliases={a: 2 + a for a in range(2 * n)},
        compiler_params=pltpu.CompilerParams(has_side_effects=_EFFECT),
    )(*[_in_hbm(s) for s in shards], *[_in_hbm(z) for z in lands0], *deps)
    handle = dict(send_sems=outs[0], recv_sems=outs[1], srcs=list(outs[2:2 + n]), lands=list(outs[2 + n:2 + 2 * n]), axes=list(axes))
    return handle, outs[-1]


def _gather_wait(handle, after, *, name):
    srcs, lands, axes = handle["srcs"], handle["lands"], handle["axes"]
    n = len(srcs)
    sizes = [s.shape[ax] for s, ax in zip(srcs, axes)]

    def body(*refs):
        src_refs, land_refs = refs[:n], refs[n:2 * n]
        send_sems, recv_sems = refs[2 * n], refs[2 * n + 1]
        x, y, c, chips = _mesh_pos()
        peers = [(x, y, 1 - c)] + [(*chip, c) for chip in chips]
        for a in range(n):
            for k, dev in enumerate(peers):
                cp = pltpu.make_async_remote_copy(
                    src_ref=src_refs[a], dst_ref=_block(land_refs[a], axes[a], 4 * dev[0] + 2 * dev[1] + dev[2], sizes[a]),
                    send_sem=send_sems.at[_N_OUT * a + k], recv_sem=recv_sems.at[_N_OUT * a + k], device_id=dev,
                    device_id_type=MESH)
                cp.wait_send()
                cp.wait_recv()

    outs = pl.pallas_call(
        body, name=name,
        out_shape=(*[pltpu.HBM(s.shape, s.dtype) for s in srcs], *[pltpu.HBM(z.shape, z.dtype) for z in lands]),
        in_specs=[_HBM] * (2 * n) + [_SEM, _SEM] + [_ANY] * len(after),
        out_specs=tuple([_HBM] * (2 * n)),
        input_output_aliases={a: a for a in range(2 * n)},
        compiler_params=pltpu.CompilerParams(has_side_effects=_EFFECT),
    )(*srcs, *lands, handle["send_sems"], handle["recv_sems"], *after)
    return list(outs[:n]), list(outs[n:])


def _place_own(shard, full, axis, dev, *, name):
    nd = shard.ndim
    rows, cols = shard.shape[-2:]
    tr = _tile(rows, 512, _row_mult(shard.dtype)) if rows % _row_mult(shard.dtype) == 0 else rows
    nrb = rows // tr
    block = shard.shape[:-2] + (tr, cols)

    def in_map(r, dev_ref):
        return (0,) * (nd - 2) + (r, 0)

    def out_map(r, dev_ref):
        idx = [0] * nd
        idx[nd - 2] = r
        idx[axis] = dev_ref[0] * (nrb if axis == nd - 2 else 1) + idx[axis]
        return tuple(idx)

    def body(dev_ref, src_ref, full_ref, out_ref):
        out_ref[...] = src_ref[...]

    grid_spec = pltpu.PrefetchScalarGridSpec(
        num_scalar_prefetch=1, grid=(nrb,),
        in_specs=[pl.BlockSpec(block, in_map), _ANY], out_specs=pl.BlockSpec(block, out_map))
    return pl.pallas_call(body, name=name, grid_spec=grid_spec, out_shape=jax.ShapeDtypeStruct(full.shape, full.dtype),
                          input_output_aliases={2: 0}, compiler_params=_params(("arbitrary",)))(dev, shard, full)


def _gather_pass_start(fulls, axes, *, name):
    n = len(fulls)
    sizes = [z.shape[ax] // N_DEV for z, ax in zip(fulls, axes)]

    def body(*refs):
        srcs = refs[:n]
        send_sems, recv_sems = refs[n], refs[n + 1]
        token = refs[-1]
        x, y, c, chips = _mesh_pos()
        for a in range(n):
            for j, chip in enumerate(chips):
                blk = _block(srcs[a], axes[a], 4 * chip[0] + 2 * chip[1] + c, sizes[a])
                pltpu.make_async_remote_copy(src_ref=blk, dst_ref=blk, send_sem=send_sems.at[3 * a + j], recv_sem=recv_sems.at[3 * a + j],
                                             device_id=(x, y, 1 - c), device_id_type=MESH).start()
        token[...] = jnp.zeros(token.shape, F32)

    outs = pl.pallas_call(
        body, name=name,
        out_shape=(pltpu.SemaphoreType.DMA((3 * n,)), pltpu.SemaphoreType.DMA((3 * n,)), *[pltpu.HBM(z.shape, z.dtype) for z in fulls],
                   jax.ShapeDtypeStruct((8, LANES), F32)),
        in_specs=[_HBM] * n, out_specs=(_SEM, _SEM, *[_HBM] * n, pl.BlockSpec(memory_space=pltpu.VMEM)),
        input_output_aliases={a: 2 + a for a in range(n)},
        compiler_params=pltpu.CompilerParams(has_side_effects=_EFFECT),
    )(*[_in_hbm(z) for z in fulls])
    return dict(send_sems=outs[0], recv_sems=outs[1], fulls=list(outs[2:2 + n]), axes=list(axes)), outs[-1]


def _gather_pass_wait(handle, after, *, name):
    fulls, axes = handle["fulls"], handle["axes"]
    n = len(fulls)
    sizes = [z.shape[ax] // N_DEV for z, ax in zip(fulls, axes)]

    def body(*refs):
        bufs = refs[:n]
        send_sems, recv_sems = refs[n], refs[n + 1]
        x, y, c, chips = _mesh_pos()
        for a in range(n):
            for j, chip in enumerate(chips):
                cp = pltpu.make_async_remote_copy(
                    src_ref=_block(bufs[a], axes[a], 4 * chip[0] + 2 * chip[1] + c, sizes[a]),
                    dst_ref=_block(bufs[a], axes[a], 4 * chip[0] + 2 * chip[1] + (1 - c), sizes[a]),
                    send_sem=send_sems.at[3 * a + j], recv_sem=recv_sems.at[3 * a + j], device_id=(x, y, 1 - c),
                    device_id_type=MESH)
                cp.wait_send()
                cp.wait_recv()

    outs = pl.pallas_call(
        body, name=name,
        out_shape=tuple(pltpu.HBM(z.shape, z.dtype) for z in fulls),
        in_specs=[_HBM] * n + [_SEM, _SEM] + [_ANY] * len(after), out_specs=tuple([_HBM] * n),
        input_output_aliases={a: a for a in range(n)},
        compiler_params=pltpu.CompilerParams(has_side_effects=_EFFECT),
    )(*fulls, handle["send_sems"], handle["recv_sems"], *after)
    return list(outs)


def _rs_exchange_sibling(tensors, axes, *, name):
    n = len(tensors)
    n_layers = [len(t) for t in tensors]
    sizes = [t[0].shape[ax] // N_DEV for t, ax in zip(tensors, axes)]
    blk_shapes = [t[0].shape[:ax] + (sz,) + t[0].shape[ax + 1:] for t, ax, sz in zip(tensors, axes, sizes)]
    flat = [g for t in tensors for g in t]
    offs = [sum(n_layers[:a]) for a in range(n)]

    def body(*refs):
        srcs = refs[:len(flat)]
        lands = refs[len(flat):len(flat) + n]
        send_sems, recv_sems = refs[len(flat) + n:]
        x, y, c, _ = _mesh_pos()
        sib = (x, y, 1 - c)
        for a in range(n):
            for l in range(n_layers[a]):
                for i in range(N_CHIP):
                    pltpu.make_async_remote_copy(
                        src_ref=_block(srcs[offs[a] + l], axes[a], 2 * i + (1 - c), sizes[a]), dst_ref=lands[a].at[i, l],
                        send_sem=send_sems.at[a], recv_sem=recv_sems.at[a], device_id=sib, device_id_type=MESH).start()
        for a in range(n):
            pltpu.make_async_remote_copy(src_ref=lands[a], dst_ref=lands[a], send_sem=send_sems.at[a], recv_sem=recv_sems.at[a],
                                         device_id=sib, device_id_type=MESH).wait()

    any_spec = pl.BlockSpec(memory_space=pl.ANY)
    return pl.pallas_call(
        body, name=name,
        out_shape=[jax.ShapeDtypeStruct((N_CHIP, nl) + bs, t[0].dtype) for nl, bs, t in zip(n_layers, blk_shapes, tensors)],
        in_specs=[any_spec] * len(flat), out_specs=[any_spec] * n,
        scratch_shapes=[pltpu.SemaphoreType.DMA((n,)), pltpu.SemaphoreType.DMA((n,))],
    )(*flat)


def _rs_add_sibling(layers, axis, land, my_c, *, name):
    n_layers = len(layers)
    shape = layers[0].shape
    size = shape[axis] // N_DEV
    blk_shape = shape[:axis] + (size,) + shape[axis + 1:]
    nd = len(shape)
    rows = blk_shape[-2]
    tr = _tile(rows, 512, _row_mult(layers[0].dtype))
    inner = (tr, blk_shape[-1])
    lead = blk_shape[:-2]
    if lead:
        raise ValueError("blocked gradients are 2-D")
    nrb = rows // tr

    def src_map(l):
        def imap(i, r, c_ref):
            j = 2 * i + c_ref[0]
            return (j * nrb + r, 0) if axis == 0 else (r, j)
        return imap

    def body(c_ref, *refs):
        srcs = refs[:n_layers]
        land_ref = refs[n_layers]
        out_ref = refs[n_layers + 1]
        for l in range(n_layers):
            out_ref[l] = (srcs[l][...].astype(F32) + land_ref[l].astype(F32)).astype(out_ref.dtype)

    grid_spec = pltpu.PrefetchScalarGridSpec(
        num_scalar_prefetch=1, grid=(N_CHIP, nrb),
        in_specs=[pl.BlockSpec(inner, src_map(l)) for l in range(n_layers)]
        + [pl.BlockSpec((None, n_layers) + inner, lambda i, r, c_ref: (i, 0, r, 0))],
        out_specs=pl.BlockSpec((None, n_layers) + inner, lambda i, r, c_ref: (i, 0, r, 0)),
    )
    return pl.pallas_call(
        body, name=name, grid_spec=grid_spec,
        out_shape=jax.ShapeDtypeStruct((N_CHIP, n_layers) + blk_shape, layers[0].dtype),
        compiler_params=_params(("arbitrary", "arbitrary")),
    )(my_c, *layers, land)


def _rs_sibling_start(g, axis, *, name):
    size = g.shape[axis] // N_DEV
    land = lax.empty((N_CHIP, 1) + g.shape[:axis] + (size,) + g.shape[axis + 1:], g.dtype)

    def body(src, dst, send_sems, recv_sems, src_thru, dst_thru, token):
        x, y, c, _ = _mesh_pos()
        for i in range(N_CHIP):
            pltpu.make_async_remote_copy(src_ref=_block(src, axis, 2 * i + (1 - c), size), dst_ref=dst.at[i, 0], send_sem=send_sems.at[i],
                                         recv_sem=recv_sems.at[i], device_id=(x, y, 1 - c), device_id_type=MESH).start()
        token[...] = jnp.zeros(token.shape, F32)

    outs = pl.pallas_call(
        body, name=name,
        out_shape=(pltpu.SemaphoreType.DMA((N_CHIP,)), pltpu.SemaphoreType.DMA((N_CHIP,)), pltpu.HBM(g.shape, g.dtype),
                   pltpu.HBM(land.shape, land.dtype), jax.ShapeDtypeStruct((8, LANES), F32)),
        in_specs=[_HBM, _HBM], out_specs=(_SEM, _SEM, _HBM, _HBM, pl.BlockSpec(memory_space=pltpu.VMEM)),
        input_output_aliases={0: 2, 1: 3},
        compiler_params=pltpu.CompilerParams(has_side_effects=_EFFECT),
    )(_in_hbm(g), _in_hbm(land))
    return dict(send_sems=outs[0], recv_sems=outs[1], src=outs[2], land=outs[3], axis=axis), outs[4]


def _rs_sibling_wait(handle, after, *, name):
    src, land, axis = handle["src"], handle["land"], handle["axis"]
    size = src.shape[axis] // N_DEV

    def body(src_ref, dst_ref, send_sems, recv_sems, *rest):
        x, y, c, _ = _mesh_pos()
        for i in range(N_CHIP):
            cp = pltpu.make_async_remote_copy(src_ref=_block(src_ref, axis, 2 * i + (1 - c), size), dst_ref=dst_ref.at[i, 0],
                                              send_sem=send_sems.at[i], recv_sem=recv_sems.at[i], device_id=(x, y, 1 - c),
                                              device_id_type=MESH)
            cp.wait_send()
            cp.wait_recv()

    outs = pl.pallas_call(
        body, name=name,
        out_shape=(pltpu.HBM(src.shape, src.dtype), pltpu.HBM(land.shape, land.dtype)),
        in_specs=[_HBM, _HBM, _SEM, _SEM] + [_ANY] * len(after), out_specs=(_HBM, _HBM),
        input_output_aliases={0: 0, 1: 1},
        compiler_params=pltpu.CompilerParams(has_side_effects=_EFFECT),
    )(src, land, handle["send_sems"], handle["recv_sems"], *after)
    return outs[0], outs[1]


def _rs_chips_start(part, *, name):
    land = lax.empty(part.shape, part.dtype)

    def body(src, dst, send_sems, recv_sems, src_thru, dst_thru, token):
        x, y, c, chips = _mesh_pos()
        for j, chip in enumerate(chips):
            pltpu.make_async_remote_copy(src_ref=src.at[2 * chip[0] + chip[1]], dst_ref=dst.at[2 * x + y], send_sem=send_sems.at[j],
                                         recv_sem=recv_sems.at[j], device_id=(*chip, c), device_id_type=MESH).start()
        token[...] = jnp.zeros(token.shape, F32)

    outs = pl.pallas_call(
        body, name=name,
        out_shape=(pltpu.SemaphoreType.DMA((3,)), pltpu.SemaphoreType.DMA((3,)), pltpu.HBM(part.shape, part.dtype),
                   pltpu.HBM(land.shape, land.dtype), jax.ShapeDtypeStruct((8, LANES), F32)),
        in_specs=[_HBM, _HBM], out_specs=(_SEM, _SEM, _HBM, _HBM, pl.BlockSpec(memory_space=pltpu.VMEM)),
        input_output_aliases={0: 2, 1: 3},
        compiler_params=pltpu.CompilerParams(has_side_effects=_EFFECT),
    )(_in_hbm(part), _in_hbm(land))
    return dict(send_sems=outs[0], recv_sems=outs[1], src=outs[2], land=outs[3]), outs[4]


def _rs_chips_wait(handle, after, *, name):
    def body(src, dst, send_sems, recv_sems, *rest):
        x, y, c, chips = _mesh_pos()
        for j, chip in enumerate(chips):
            cp = pltpu.make_async_remote_copy(src_ref=src.at[2 * chip[0] + chip[1]], dst_ref=dst.at[2 * chip[0] + chip[1]],
                                              send_sem=send_sems.at[j], recv_sem=recv_sems.at[j], device_id=(*chip, c),
                                              device_id_type=MESH)
            cp.wait_send()
            cp.wait_recv()

    src, land = handle["src"], handle["land"]
    outs = pl.pallas_call(
        body, name=name,
        out_shape=(pltpu.HBM(src.shape, src.dtype), pltpu.HBM(land.shape, land.dtype)),
        in_specs=[_HBM, _HBM, _SEM, _SEM] + [_ANY] * len(after), out_specs=(_HBM, _HBM),
        input_output_aliases={0: 0, 1: 1},
        compiler_params=pltpu.CompilerParams(has_side_effects=_EFFECT),
    )(src, land, handle["send_sems"], handle["recv_sems"], *after)
    return outs[0], outs[1]


def _adam_math(w, g, m, v):
    m_new = ADAM_B1 * m + (1.0 - ADAM_B1) * g
    v_new = ADAM_B2 * v + (1.0 - ADAM_B2) * (g * g)
    m_hat = m_new / (1.0 - ADAM_B1 ** ADAM_STEP)
    v_hat = v_new / (1.0 - ADAM_B2 ** ADAM_STEP)
    delta = -ADAM_LR * (m_hat / (jnp.sqrt(v_hat) + ADAM_EPS) + ADAM_WD * w)
    return delta, m_new, v_new


def _sum_adam(part, land, slots, w, m, v, layer, prev, *, name, deps=()):
    n_layers, rows, cols = w.shape
    tr = _tile(rows, 256, _row_mult(land.dtype))
    n_prev = 0 if prev is None else 4

    def body(slots_ref, p0_ref, p1_ref, p2_ref, p3_ref, w_ref, m_ref, v_ref, *rest):
        g_ref, d_ref, mo_ref, vo_ref = rest[n_prev + len(deps):]
        g = p0_ref[...].astype(F32)
        for ref in (p1_ref, p2_ref, p3_ref):
            g = g + ref[...].astype(F32)
        delta, m_new, v_new = _adam_math(w_ref[...], g, m_ref[...], v_ref[...])
        g_ref[...] = g
        d_ref[...] = delta
        mo_ref[...] = m_new
        vo_ref[...] = v_new

    blk = pl.BlockSpec((None, tr, cols), lambda r, s: (layer, r, 0))
    slot = lambda j: pl.BlockSpec((None, None, tr, cols), lambda r, s: (s[j], 0, r, 0))
    shp = jax.ShapeDtypeStruct((n_layers, rows, cols), F32)
    grid_spec = pltpu.PrefetchScalarGridSpec(
        num_scalar_prefetch=1, grid=(rows // tr,),
        in_specs=[slot(0), slot(1), slot(2), slot(3), blk, blk, blk] + [_ANY] * (n_prev + len(deps)),
        out_specs=(blk, blk, blk, blk))
    return pl.pallas_call(
        body, name=name, out_shape=(shp, shp, shp, shp), grid_spec=grid_spec,
        input_output_aliases={8 + i: i for i in range(n_prev)},
        compiler_params=_params(("parallel",)),
    )(slots, part, land, land, land, w, m, v, *(prev or ()), *deps)


def _sum_slots(part, land, slots, *, name):
    _, _, rows, cols = land.shape
    tr = _tile(rows, 256, 8)

    def body(slots_ref, p0_ref, p1_ref, p2_ref, p3_ref, g_ref):
        g_ref[...] = p0_ref[...] + p1_ref[...] + p2_ref[...] + p3_ref[...]

    slot = lambda j: pl.BlockSpec((None, None, tr, cols), lambda r, s: (s[j], 0, r, 0))
    grid_spec = pltpu.PrefetchScalarGridSpec(
        num_scalar_prefetch=1, grid=(rows // tr,), in_specs=[slot(0), slot(1), slot(2), slot(3)],
        out_specs=pl.BlockSpec((tr, cols), lambda r, s: (r, 0)))
    return pl.pallas_call(body, name=name, out_shape=jax.ShapeDtypeStruct((rows, cols), F32), grid_spec=grid_spec,
                          compiler_params=_params(("parallel",)))(slots, part, land, land, land)


def _adam_flat(w, g, m, v, *, name):
    rows, cols = w.shape
    tr = _tile(rows, 256, 8)

    def body(w_ref, g_ref, m_ref, v_ref, d_ref, mo_ref, vo_ref):
        delta, m_new, v_new = _adam_math(w_ref[...], g_ref[...], m_ref[...], v_ref[...])
        d_ref[...] = delta
        mo_ref[...] = m_new
        vo_ref[...] = v_new

    blk = pl.BlockSpec((tr, cols), lambda r: (r, 0))
    shp = jax.ShapeDtypeStruct((rows, cols), F32)
    return pl.pallas_call(
        body, name=name, out_shape=(shp, shp, shp), grid=(rows // tr,), in_specs=[blk] * 4, out_specs=(blk, blk, blk),
        compiler_params=_params(("parallel",)),
    )(w, g, m, v)


_BIG = (("ffn1_w_in", 1), ("ffn1_w_out", 0), ("w_in", 1), ("w_out", 0), ("ffn2_w_in", 1), ("ffn2_w_out", 0))
_SMALL = ("meta_tokens", "ffn1_norm", "mix_norm", "conv_w", "conv_b", "lru_wa", "lru_ba", "lru_wx", "lru_bx", "lru_a_param",
          "pool_w", "pool_b", "pool_scale", "ffn2_norm", "final_norm")
_SMALL_SHARD_AXIS = {"meta_tokens": 1, "conv_w": 2, "pool_w": 2}
_PACK_COLS = 1024


def _pack(arrs):
    flat = jnp.concatenate([a.reshape(-1) for a in arrs])
    unit = N_DEV * 8 * _PACK_COLS
    total = -(-flat.shape[0] // unit) * unit
    flat = jnp.pad(flat, (0, total - flat.shape[0]))
    return flat.reshape(total // _PACK_COLS, _PACK_COLS)


def _unpack(packed, shapes):
    flat = packed.reshape(-1)
    out, off = [], 0
    for s in shapes:
        size = 1
        for v in s:
            size *= v
        out.append(flat[off:off + size].reshape(s))
        off += size
    return out


def _my_shard(full, axis, dev):
    size = full.shape[axis] // N_DEV
    return lax.dynamic_slice_in_dim(full, dev * size, size, axis)


def kernel(x, meta_tokens, ffn1_norm, ffn1_w_in, ffn1_w_out, mix_norm, w_in, conv_w, conv_b, lru_wa, lru_ba, lru_wx, lru_bx, lru_a_param, pool_w, pool_b, pool_scale, w_out, ffn2_norm, ffn2_w_in, ffn2_w_out, final_norm, loss_target, m_meta_tokens, m_ffn1_norm, m_ffn1_w_in, m_ffn1_w_out, m_mix_norm, m_w_in, m_conv_w, m_conv_b, m_lru_wa, m_lru_ba, m_lru_wx, m_lru_bx, m_lru_a_param, m_pool_w, m_pool_b, m_pool_scale, m_w_out, m_ffn2_norm, m_ffn2_w_in, m_ffn2_w_out, m_final_norm, v_meta_tokens, v_ffn1_norm, v_ffn1_w_in, v_ffn1_w_out, v_mix_norm, v_w_in, v_conv_w, v_conv_b, v_lru_wa, v_lru_ba, v_lru_wx, v_lru_bx, v_lru_a_param, v_pool_w, v_pool_b, v_pool_scale, v_w_out, v_ffn2_norm, v_ffn2_w_in, v_ffn2_w_out, v_final_norm):
    names = ("meta_tokens", "ffn1_norm", "ffn1_w_in", "ffn1_w_out", "mix_norm", "w_in", "conv_w", "conv_b", "lru_wa", "lru_ba",
             "lru_wx", "lru_bx", "lru_a_param", "pool_w", "pool_b", "pool_scale", "w_out", "ffn2_norm", "ffn2_w_in", "ffn2_w_out",
             "final_norm")
    w = dict(zip(names, (meta_tokens, ffn1_norm, ffn1_w_in, ffn1_w_out, mix_norm, w_in, conv_w, conv_b, lru_wa, lru_ba, lru_wx,
                         lru_bx, lru_a_param, pool_w, pool_b, pool_scale, w_out, ffn2_norm, ffn2_w_in, ffn2_w_out, final_norm)))
    mom = dict(zip(names, (m_meta_tokens, m_ffn1_norm, m_ffn1_w_in, m_ffn1_w_out, m_mix_norm, m_w_in, m_conv_w, m_conv_b, m_lru_wa,
                           m_lru_ba, m_lru_wx, m_lru_bx, m_lru_a_param, m_pool_w, m_pool_b, m_pool_scale, m_w_out, m_ffn2_norm,
                           m_ffn2_w_in, m_ffn2_w_out, m_final_norm)))
    vel = dict(zip(names, (v_meta_tokens, v_ffn1_norm, v_ffn1_w_in, v_ffn1_w_out, v_mix_norm, v_w_in, v_conv_w, v_conv_b, v_lru_wa,
                           v_lru_ba, v_lru_wx, v_lru_bx, v_lru_a_param, v_pool_w, v_pool_b, v_pool_scale, v_w_out, v_ffn2_norm,
                           v_ffn2_w_in, v_ffn2_w_out, v_final_norm)))
    n_layers, d = ffn1_norm.shape
    n_meta = meta_tokens.shape[0]
    seq = x.shape[1]
    t_valid = n_meta + seq
    tp = -(-t_valid // ROW_ALIGN) * ROW_ALIGN
    dev = 4 * lax.axis_index("x") + 2 * lax.axis_index("y") + lax.axis_index("c")
    my_c = lax.axis_index("c").astype(jnp.int32).reshape(1)
    dev1 = dev.astype(jnp.int32).reshape(1)
    mx, my = lax.axis_index("x"), lax.axis_index("y")
    slots = jnp.stack([2 * mx + my, 2 * (1 - mx) + my, 2 * mx + (1 - my), 2 * (1 - mx) + (1 - my)]).astype(jnp.int32)

    big_axis = dict(_BIG)
    vec = lambda a: a.reshape(a.shape[0], 1, a.shape[1])
    norms = dict(ffn1=vec(ffn1_norm), mix=vec(mix_norm), ffn2=vec(ffn2_norm))
    units = [(kind, l) for l in range(n_layers) for kind in ("ffn1", "mix", "ffn2")]

    handles = {}
    state = dict(token=None, started=0, first=None)

    def behind(a):
        return a if state["first"] is None else lax.optimization_barrier((state["first"], a))[1]

    def shard(k, l):
        return behind(w[k][l:l + 1]).astype(BF16), big_axis[k] + 1

    groups = [[shard("ffn1_w_in", 0), (meta_tokens, 1)], [shard("ffn1_w_out", 0)]]

    def order():
        return [] if state["token"] is None else [state["token"]]

    def start_next():
        i = state["started"]
        if i < len(groups):
            handles[i], state["token"] = _gather_start([s for s, _ in groups[i]], [ax for _, ax in groups[i]], order(),
                                                       name=f"gather{i}_start")
            state["started"] = i + 1

    def finish_begin(i, after):
        handle = handles.pop(i)
        srcs, fulls = _gather_wait(handle, list(after) + order(), name=f"gather{i}_wait")
        fulls = [_place_own(s, z, ax, dev1, name=f"gather{i}_own{a}") for a, (s, z, ax) in enumerate(zip(srcs, fulls, handle["axes"]))]
        passing, state["token"] = _gather_pass_start(fulls, handle["axes"], name=f"gather{i}_pass")
        return i, passing

    def finish_end(ctx, after):
        i, passing = ctx
        fulls = _gather_pass_wait(passing, after, name=f"gather{i}_passwait")
        start_next()
        return fulls

    start_next()
    start_next()
    state["first"] = state["token"]
    for kind, l in units[1:]:
        if kind == "mix":
            groups.append([shard("w_in", l), shard("w_out", l)] + ([(behind(conv_w), 2), (behind(pool_w), 2)] if l == 0 else []))
        else:
            groups.append([shard(f"{kind}_w_in", l), shard(f"{kind}_w_out", l)])
    pad = jnp.zeros((tp - t_valid, d), F32)
    tgt = jnp.concatenate([jnp.zeros((n_meta, d), F32), behind(loss_target[0]), pad], axis=0)
    h0_base = jnp.concatenate([jnp.zeros((n_meta, d), F32), behind(x[0]), pad], axis=0)
    ctx = finish_begin(0, [tgt, h0_base] + [s for grp in groups[2:] for s, _ in grp])
    w_in0, meta_full = finish_end(ctx, [])
    h0 = lax.dynamic_update_slice(h0_base, meta_full, (0, 0))
    n_act = _rms_fwd(h0, norms["ffn1"], 0, name="l0ffn1_rms", deps=order())
    gu, act = _ffn_in_fwd(n_act, w_in0, 0, name="l0ffn1_in")
    (w_out0,) = finish_end(finish_begin(1, [act]), [])
    h = _ffn_fwd_out(h0, act, w_out0, "l0ffn1", order())
    gathered = {units[0]: (w_in0, w_out0)}
    saved = {units[0]: (h0, n_act, gu)}
    mp = None
    for ui, (kind, l) in enumerate(units[1:], start=1):
        tag = f"l{l}{kind}"
        ctx = finish_begin(ui + 1, [h])
        n_act = _rms_fwd(h, norms[kind], l, name=f"{tag}_rms", deps=order())
        fulls = finish_end(ctx, [n_act])
        if kind == "mix":
            if l == 0:
                mp = dict(conv_w=fulls[2], conv_b=vec(conv_b), wa=lru_wa.astype(BF16), ba=vec(lru_ba), wx=lru_wx.astype(BF16),
                          bx=vec(lru_bx), a_param=vec(lru_a_param), pool_w=fulls[3].astype(BF16), pool_b=vec(pool_b),
                          pool_scale=vec(pool_scale))
            h, saved[(kind, l)] = _mix_block_fwd(h, n_act, l, fulls[0], fulls[1], mp, tag, order())
        else:
            gu, act = _ffn_in_fwd(n_act, fulls[0], 0, name=f"{tag}_in", deps=order())
            saved[(kind, l)] = (h, n_act, gu)
            h = _ffn_fwd_out(h, act, fulls[1], tag, [])
        gathered[(kind, l)] = (fulls[0], fulls[1])
    dh, dhb, g_final, loss_local = _final_loss(h, final_norm.reshape(1, d), tgt, n_meta, t_valid, name="final_loss")
    loss = lax.psum(loss_local[0, 0], ("x", "y", "c"))

    pending = []
    big_out = {}

    landed = []

    def drain(after):
        while pending:
            k, l, handle = pending.pop(0)
            landed.append((k, l) + _rs_chips_wait(handle, after, name=f"rs_{k}{l}_wait"))

    def adams(items, deps):
        for k, l, part, land in items:
            big_out[k] = _sum_adam(part, land, slots, w[k], mom[k], vel[k], l, big_out.get(k), name=f"adam_{k}{l}", deps=deps)
        return [big_out[items[-1][0]][0]] if items else []

    sib_pending = []

    def to_chips(k, l, g, land):
        part = _rs_add_sibling([g], big_axis[k], land, my_c, name=f"rs_{k}{l}_add")
        handle, token = _rs_chips_start(part, name=f"rs_{k}{l}_start")
        pending.append((k, l, handle))
        return token

    def finish_sibling(after):
        token = None
        while sib_pending:
            k, l, handle = sib_pending.pop(0)
            g, land = _rs_sibling_wait(handle, after, name=f"rs_{k}{l}_sibwait")
            token = to_chips(k, l, g, land)
        return token

    def emitter(kind, l):
        last_unit = (kind, l) == units[0]

        def emit(which, g):
            if which == "w_out":
                drain([g])
            token = finish_sibling([g])
            order_after = [] if token is None else [token]
            if which == "flush":
                return order_after
            k = which if kind == "mix" else f"{kind}_{which}"
            if last_unit and which == "w_in":
                (land,) = _rs_exchange_sibling([[g]], [big_axis[k]], name=f"rs_{k}{l}_sib")
                return order_after + [to_chips(k, l, g, land)]
            handle, token = _rs_sibling_start(g, big_axis[k], name=f"rs_{k}{l}_sibstart")
            sib_pending.append((k, l, handle))
            return order_after + [token]
        return emit

    g_norm = {k: [None] * n_layers for k in ("ffn1", "mix", "ffn2")}
    g_mix = [None] * n_layers
    for kind, l in reversed(units):
        w_i, w_o = gathered[(kind, l)]
        tag = f"l{l}{kind}"
        if kind == "mix":
            dh, dhb, g_norm[kind][l], g_mix[l] = _mix_block_bwd(dh, dhb, saved[(kind, l)], norms[kind], l, w_i, w_o, mp, tag,
                                                                 emitter(kind, l), [])
        else:
            dh, dhb, g_norm[kind][l] = _ffn_bwd(dh, dhb, saved[(kind, l)], norms[kind], l, w_i, w_o, tag, emitter(kind, l), [])
    drain([dh])
    dh0 = dh
    grad_x = dh0[n_meta:t_valid][None]

    mix_key = dict(conv_w="conv_w", conv_b="conv_b", lru_wa="wa", lru_ba="ba", lru_wx="wx", lru_bx="bx", lru_a_param="a_param",
                   pool_w="pool_w", pool_b="pool_b", pool_scale="pool_scale")
    small_local = {"meta_tokens": dh0[:n_meta], "final_norm": g_final.reshape(d)}
    for k in ("ffn1", "mix", "ffn2"):
        small_local[f"{k}_norm"] = jnp.stack([g.reshape(d) for g in g_norm[k]])
    for k, mk in mix_key.items():
        small_local[k] = jnp.stack([g_mix[l][mk] for l in range(n_layers)]).reshape(
            w[k].shape if k not in _SMALL_SHARD_AXIS else small_shape_full(w[k], _SMALL_SHARD_AXIS[k]))
    small_shapes = [small_local[k].shape for k in _SMALL]
    packed = _pack([small_local[k] for k in _SMALL])
    (land,) = _rs_exchange_sibling([[packed]], [0], name="rs_small_sib")
    part = _rs_add_sibling([packed], 0, land, my_c, name="rs_small_add")
    third = -(-len(landed) // 3)
    handle, token = _rs_chips_start(part, name="rs_small_start")
    done = adams(landed[:third], [token])
    part, land = _rs_chips_wait(handle, done, name="rs_small_wait")
    small_block = _sum_slots(part, land, slots, name="rs_sum_small")
    handle, token = _gather_start([small_block], [0], [], name="gather_small_start")
    done = adams(landed[third:2 * third], [token])
    (src,), (small_full,) = _gather_wait(handle, done, name="gather_small_wait")
    small_full = _place_own(src, small_full, 0, dev1, name="gather_small_own")
    handle, token = _gather_pass_start([small_full], [0], name="gather_small_pass")
    done = adams(landed[2 * third:], [token])
    (small_full,) = _gather_pass_wait(handle, done, name="gather_small_passwait")

    out_g, out_d, out_m, out_v = {}, {}, {}, {}
    for k, _ in _BIG:
        out_g[k], out_d[k], out_m[k], out_v[k] = big_out[k]
    small_g = dict(zip(_SMALL, _unpack(small_full, small_shapes)))
    for k, ax in _SMALL_SHARD_AXIS.items():
        small_g[k] = _my_shard(small_g[k], ax, dev)
    shapes_local = [w[k].shape for k in _SMALL]
    d_p, m_p, v_p = _adam_flat(_pack([w[k] for k in _SMALL]), _pack([small_g[k] for k in _SMALL]),
                               _pack([mom[k] for k in _SMALL]), _pack([vel[k] for k in _SMALL]), name="adam_small")
    for k, dd, mm, vv in zip(_SMALL, _unpack(d_p, shapes_local), _unpack(m_p, shapes_local), _unpack(v_p, shapes_local)):
        out_g[k], out_d[k], out_m[k], out_v[k] = small_g[k], dd, mm, vv

    return (loss, grad_x, *[out_g[k] for k in names], *[out_d[k] for k in names], *[out_m[k] for k in names],
            *[out_v[k] for k in names])


def small_shape_full(w_shard, axis):
    return w_shard.shape[:axis] + (N_DEV * w_shard.shape[axis],) + w_shard.shape[axis + 1:]
```

```python
import functools

import jax
import jax.numpy as jnp
from jax import lax
from jax.experimental import pallas as pl
from jax.experimental.pallas import tpu as pltpu

F32 = jnp.float32
BF16 = jnp.bfloat16
MESH = pl.DeviceIdType.MESH

N_DEV = 8
N_CHIP = 4
RMS_EPS = 1e-6
LRU_C = 8.0
CONV_WIDTH = 4
POOL_WINDOWS = (2, 4, 8, 16)
HIST = 16
ADAM_LR = 0.001
ADAM_B1 = 0.9
ADAM_B2 = 0.999
ADAM_EPS = 1e-08
ADAM_WD = 0.01
ADAM_STEP = 10
ROW_ALIGN = 128
V7X_VMEM_LIMIT = 56 * 1024 * 1024
BF16_ROWS = 16
K_TILE = 5632
WGRAD_TILE = 2048
SCAN_UNROLL = 8
EPILOGUE_ROWS = 528
LANES = 128


def _tile(n, cap, mult=BF16_ROWS):
    best = None
    d = mult
    while d <= min(n, cap):
        if n % d == 0:
            best = d
        d += mult
    if best is None:
        raise ValueError(f"no tile for {n} (cap {cap}, multiple of {mult})")
    return best


def _row_mult(dtype):
    return 8 * 4 // jnp.dtype(dtype).itemsize


def _params(sem=None):
    return pltpu.CompilerParams(dimension_semantics=sem, vmem_limit_bytes=V7X_VMEM_LIMIT)


def _dot(a, b, mode):
    dims = {"nn": ((1,), (0,)), "nt": ((1,), (1,)), "tn": ((0,), (0,))}[mode]
    return lax.dot_general(a, b, (dims, ((), ())), preferred_element_type=F32)


def _sigmoid(x):
    return 1.0 / (1.0 + jnp.exp(-x))


_GELU_C = 0.7978845608028654
_GELU_K = 0.044715


def _gelu(x):
    return 0.5 * x * (1.0 + jnp.tanh(_GELU_C * (x + _GELU_K * x * x * x)))


def _gelu_and_grad(x):
    x2 = x * x
    th = jnp.tanh(_GELU_C * x * (1.0 + _GELU_K * x2))
    half = 0.5 * (1.0 + th)
    return x * half, half + 0.5 * x * (1.0 - th * th) * _GELU_C * (1.0 + 3.0 * _GELU_K * x2)


def _neg_expm1(x):
    p = 1.0 / 5040.0
    for c in (1.0 / 720.0, 1.0 / 120.0, 1.0 / 24.0, 1.0 / 6.0, 0.5, 1.0):
        p = p * x + c
    return jnp.where(x > -0.25, -(x * p), 1.0 - jnp.exp(x))


def _softplus_neg(p):
    e = jnp.exp(-jnp.abs(p))
    u = 1.0 + e
    l1p = jnp.where(u == 1.0, e, jnp.log(u) * e / (u - 1.0 + (u == 1.0).astype(F32)))
    return jnp.maximum(-p, 0.0) + l1p


def _operand_spec(arr, br, bc, ridx, cidx, layer, split):
    if split:
        ncb = arr.shape[2] // bc
        return pl.BlockSpec((None, br, bc), lambda i, j, k: (cidx(i, j, k) // ncb, ridx(i, j, k), cidx(i, j, k) % ncb))
    if layer is not None:
        return pl.BlockSpec((None, br, bc), lambda i, j, k: (layer, ridx(i, j, k), cidx(i, j, k)))
    return pl.BlockSpec((br, bc), lambda i, j, k: (ridx(i, j, k), cidx(i, j, k)))


def _matmul(a, b, *, mode, m, n, kdim, tm, tn, tk, out_dtype, name, scale=None, residual=None,
            a_layer=None, b_layer=None, a_split=False, b_split=False, k_first=0, deps=()):
    nk = kdim // tk
    I = lambda i, j, k: i
    J = lambda i, j, k: j
    K = lambda i, j, k: k + k_first
    if mode == "nn":
        a_spec = _operand_spec(a, tm, tk, I, K, a_layer, a_split)
        b_spec = _operand_spec(b, tk, tn, K, J, b_layer, b_split)
    elif mode == "nt":
        a_spec = _operand_spec(a, tm, tk, I, K, a_layer, a_split)
        b_spec = _operand_spec(b, tn, tk, J, K, b_layer, b_split)
    else:
        a_spec = _operand_spec(a, tk, tm, K, I, a_layer, a_split)
        b_spec = _operand_spec(b, tk, tn, K, J, b_layer, b_split)
    in_specs = [a_spec, b_spec]
    operands = [a, b]
    if residual is not None:
        in_specs.append(pl.BlockSpec((tm, tn), lambda i, j, k: (i, j)))
        operands.append(residual)
    in_specs += [pl.BlockSpec(memory_space=pl.ANY)] * len(deps)
    operands += list(deps)
    n_in = len(operands)

    def finish(acc, res_ref, o_ref):
        if scale is not None:
            acc = acc * scale
        if res_ref is not None:
            acc = acc + res_ref[...]
        o_ref[...] = acc.astype(o_ref.dtype)

    def body(*refs):
        a_ref, b_ref = refs[0], refs[1]
        res_ref = refs[2] if residual is not None else None
        o_ref = refs[n_in]
        part = _dot(a_ref[...], b_ref[...], mode)
        if nk == 1:
            finish(part, res_ref, o_ref)
        else:
            acc_ref = refs[-1]
            k = pl.program_id(2)

            @pl.when(k == 0)
            def _():
                acc_ref[...] = part

            @pl.when(k > 0)
            def _():
                acc_ref[...] += part

            @pl.when(k == nk - 1)
            def _():
                finish(acc_ref[...], res_ref, o_ref)

    return pl.pallas_call(
        body, name=name,
        out_shape=jax.ShapeDtypeStruct((m, n), out_dtype),
        grid=(m // tm, n // tn, nk),
        in_specs=in_specs,
        out_specs=pl.BlockSpec((tm, tn), lambda i, j, k: (i, j)),
        scratch_shapes=[pltpu.VMEM((tm, tn), F32)] if nk > 1 else [],
        compiler_params=_params(("parallel", "parallel", "arbitrary")),
    )(*operands)


def _ffn_in_fwd(n_act, w_in, layer, *, name, deps=()):
    tp, d = n_act.shape
    f = w_in.shape[2] // 2
    tm = _tile(tp, 2112)
    tn = _tile(f, 512, LANES)
    nj = f // tn

    ch = _tile(tm, EPILOGUE_ROWS)

    def body(n_ref, wg_ref, wu_ref, *rest):
        gu_ref, a_ref = rest[len(deps):]
        for r in range(tm // ch):
            rows = pl.ds(r * ch, ch)
            x = n_ref[rows, :]
            g = _dot(x, wg_ref[...], "nn")
            u = _dot(x, wu_ref[...], "nn")
            gu_ref[0, rows, :] = g.astype(BF16)
            gu_ref[1, rows, :] = u.astype(BF16)
            a_ref[rows, :] = (g * _sigmoid(g) * u).astype(BF16)

    return pl.pallas_call(
        body, name=name,
        out_shape=(jax.ShapeDtypeStruct((2, tp, f), BF16), jax.ShapeDtypeStruct((tp, f), BF16)),
        grid=(tp // tm, nj),
        in_specs=[pl.BlockSpec((tm, d), lambda i, j: (i, 0)),
                  pl.BlockSpec((None, d, tn), lambda i, j: (layer, 0, j)),
                  pl.BlockSpec((None, d, tn), lambda i, j: (layer, 0, j + nj))] + [pl.BlockSpec(memory_space=pl.ANY)] * len(deps),
        out_specs=(pl.BlockSpec((2, tm, tn), lambda i, j: (0, i, j)), pl.BlockSpec((tm, tn), lambda i, j: (i, j))),
        compiler_params=_params(("parallel", "arbitrary")),
    )(n_act, w_in, w_in, *deps)


def _ffn_bwd_act(dhb, w_out, layer, gu, *, name, deps=()):
    tp, d = dhb.shape
    f = w_out.shape[1]
    tm = _tile(tp, 2112)
    tn = _tile(f, 512, LANES)

    ch = _tile(tm, EPILOGUE_ROWS)

    def body(dh_ref, w_ref, gu_ref, *rest):
        dz_ref = rest[len(deps)]
        for r in range(tm // ch):
            rows = pl.ds(r * ch, ch)
            da = 0.5 * _dot(dh_ref[rows, :], w_ref[...], "nt")
            g = gu_ref[0, rows, :].astype(F32)
            u = gu_ref[1, rows, :].astype(F32)
            s = _sigmoid(g)
            das = da * s
            dz_ref[0, rows, :] = (das * u * (1.0 + g * (1.0 - s))).astype(BF16)
            dz_ref[1, rows, :] = (das * g).astype(BF16)

    return pl.pallas_call(
        body, name=name,
        out_shape=jax.ShapeDtypeStruct((2, tp, f), BF16),
        grid=(tp // tm, f // tn),
        in_specs=[pl.BlockSpec((tm, d), lambda i, j: (i, 0)),
                  pl.BlockSpec((None, tn, d), lambda i, j: (layer, j, 0)),
                  pl.BlockSpec((2, tm, tn), lambda i, j: (0, i, j))] + [pl.BlockSpec(memory_space=pl.ANY)] * len(deps),
        out_specs=pl.BlockSpec((2, tm, tn), lambda i, j: (0, i, j)),
        compiler_params=_params(("parallel", "arbitrary")),
    )(dhb, w_out, gu, *deps)


def _rms_fwd(h, g, layer, *, name, deps=()):
    tp, d = h.shape
    tr = _tile(tp, 528)

    def body(h_ref, g_ref, *rest):
        n_ref = rest[len(deps)]
        x = h_ref[...]
        r = lax.rsqrt(jnp.mean(x * x, axis=-1, keepdims=True) + RMS_EPS)
        n_ref[...] = (x * r * g_ref[...]).astype(BF16)

    return pl.pallas_call(
        body, name=name, out_shape=jax.ShapeDtypeStruct((tp, d), BF16), grid=(tp // tr,),
        in_specs=[pl.BlockSpec((tr, d), lambda i: (i, 0)), pl.BlockSpec((None, 1, d), lambda i: (layer, 0, 0))]
        + [pl.BlockSpec(memory_space=pl.ANY)] * len(deps),
        out_specs=pl.BlockSpec((tr, d), lambda i: (i, 0)),
        compiler_params=_params(("parallel",)),
    )(h, g, *deps)


def _rms_bwd(h, g, layer, dn, dres, *, name, deps=()):
    tp, d = h.shape
    tr = _tile(tp, 528)

    def body(h_ref, g_ref, dn_ref, dres_ref, *rest):
        dh_ref, dhb_ref, dg_ref = rest[len(deps):]
        x = h_ref[...]
        r = lax.rsqrt(jnp.mean(x * x, axis=-1, keepdims=True) + RMS_EPS)
        xhat = x * r
        dn_v = dn_ref[...]
        dxhat = dn_v * g_ref[...]
        dh = dres_ref[...] + r * (dxhat - xhat * jnp.mean(dxhat * xhat, axis=-1, keepdims=True))
        dh_ref[...] = dh
        dhb_ref[...] = dh.astype(BF16)
        part = jnp.sum(dn_v * xhat, axis=0, keepdims=True)

        @pl.when(pl.program_id(0) == 0)
        def _():
            dg_ref[...] = part

        @pl.when(pl.program_id(0) > 0)
        def _():
            dg_ref[...] += part

    return pl.pallas_call(
        body, name=name,
        out_shape=(jax.ShapeDtypeStruct((tp, d), F32), jax.ShapeDtypeStruct((tp, d), BF16), jax.ShapeDtypeStruct((1, d), F32)),
        grid=(tp // tr,),
        in_specs=[pl.BlockSpec((tr, d), lambda i: (i, 0)), pl.BlockSpec((None, 1, d), lambda i: (layer, 0, 0)),
                  pl.BlockSpec((tr, d), lambda i: (i, 0)), pl.BlockSpec((tr, d), lambda i: (i, 0))]
        + [pl.BlockSpec(memory_space=pl.ANY)] * len(deps),
        out_specs=(pl.BlockSpec((tr, d), lambda i: (i, 0)), pl.BlockSpec((tr, d), lambda i: (i, 0)),
                   pl.BlockSpec((1, d), lambda i: (0, 0))),
        compiler_params=_params(("arbitrary",)),
    )(h, g, dn, dres, *deps)


def _final_loss(h, g, tgt, n_meta, t_valid, *, name):
    tp, d = h.shape
    tr = _tile(tp, 528)

    def body(h_ref, g_ref, t_ref, dh_ref, dhb_ref, dg_ref, loss_ref):
        i = pl.program_id(0)
        x = h_ref[...]
        r = lax.rsqrt(jnp.mean(x * x, axis=-1, keepdims=True) + RMS_EPS)
        xhat = x * r
        gv = g_ref[...]
        row = i * tr + lax.broadcasted_iota(jnp.int32, (tr, 1), 0)
        valid = jnp.logical_and(row >= n_meta, row < t_valid)
        err = jnp.where(valid, xhat * gv - t_ref[...], 0.0)
        dy = err * (1.0 / d)
        dxhat = dy * gv
        dh = r * (dxhat - xhat * jnp.mean(dxhat * xhat, axis=-1, keepdims=True))
        dh_ref[...] = dh
        dhb_ref[...] = dh.astype(BF16)
        dg_part = jnp.sum(dy * xhat, axis=0, keepdims=True)
        loss_part = jnp.sum(jnp.sum(err * err, axis=1, keepdims=True), axis=0, keepdims=True) * (0.5 / d)

        @pl.when(i == 0)
        def _():
            dg_ref[...] = dg_part
            loss_ref[...] = loss_part

        @pl.when(i > 0)
        def _():
            dg_ref[...] += dg_part
            loss_ref[...] += loss_part

    return pl.pallas_call(
        body, name=name,
        out_shape=(jax.ShapeDtypeStruct((tp, d), F32), jax.ShapeDtypeStruct((tp, d), BF16),
                   jax.ShapeDtypeStruct((1, d), F32), jax.ShapeDtypeStruct((1, 1), F32)),
        grid=(tp // tr,),
        in_specs=[pl.BlockSpec((tr, d), lambda i: (i, 0)), pl.BlockSpec((1, d), lambda i: (0, 0)),
                  pl.BlockSpec((tr, d), lambda i: (i, 0))],
        out_specs=(pl.BlockSpec((tr, d), lambda i: (i, 0)), pl.BlockSpec((tr, d), lambda i: (i, 0)),
                   pl.BlockSpec((1, d), lambda i: (0, 0)), pl.BlockSpec((1, 1), lambda i: (0, 0))),
        compiler_params=_params(("arbitrary",)),
    )(h, g, tgt)


def _mix_param_specs(mp, layer, imap):
    def vec(a):
        return pl.BlockSpec((None,) + a.shape[1:], lambda *g: (layer,) + (0,) * (a.ndim - 1))
    return [vec(mp[k]) for k in ("conv_w", "conv_b", "wa", "ba", "wx", "bx", "a_param", "pool_w", "pool_b", "pool_scale")]


def _mix_param_list(mp):
    return [mp[k] for k in ("conv_w", "conv_b", "wa", "ba", "wx", "bx", "a_param", "pool_w", "pool_b", "pool_scale")]


def _lru_gates(xc, wa_h, ba_h, wx_h, bx_h, sp_h):
    xb = xc.astype(BF16)
    ra = _sigmoid(_dot(xb, wa_h, "nn") + ba_h)
    ii = _sigmoid(_dot(xb, wx_h, "nn") + bx_h)
    la = -LRU_C * ra * sp_h
    return ra, ii, la, jnp.exp(la)


def _shifted(x, k):
    return x if k == 0 else pltpu.roll(x, k % x.shape[0], axis=0)


def _conv_taps(xbuf, sl, tc):
    x = xbuf[:, sl]
    return [_shifted(x, CONV_WIDTH - 1 - k)[HIST:HIST + tc] for k in range(CONV_WIDTH)]


def _conv_fwd(taps, cw_ref, cb_ref, sl):
    xc = cb_ref[:, sl]
    for k in range(CONV_WIDTH):
        xc = xc + cw_ref[k:k + 1, sl] * taps[k]
    return xc


def _window_sum(x, win, direction):
    s = x
    step = 1
    while step < win:
        s = s + _shifted(s, direction * step)
        step *= 2
    return s


def _pool_delta(pbuf, cols, win, t0, tc):
    x = pbuf[:, cols]
    u = x[HIST:HIST + tc]
    s = _window_sum(x, win, 1)[HIST:HIST + tc]
    t = t0 + lax.broadcasted_iota(jnp.int32, (tc, 1), 0)
    inv_cnt = 1.0 / jnp.minimum(t + 1, win).astype(F32)
    return s * inv_cnt - u, inv_cnt


def _mix_fwd(z, mp, layer, *, name):
    tp = z.shape[0]
    dl = z.shape[1] // 3
    n_heads, hd = mp["wa"].shape[1], mp["wa"].shape[2]
    n_groups, gd = mp["pool_w"].shape[1], mp["pool_w"].shape[2]
    tc = _tile(tp, 384)

    def body(z_ref, cw_ref, cb_ref, wa_ref, ba_ref, wx_ref, bx_ref, ap_ref, pw_ref, pb_ref, ps_ref,
             m_ref, hs_ref, aux_ref, xbuf, pbuf, a_s, b_s, hcar):
        i = pl.program_id(0)

        @pl.when(i == 0)
        def _():
            xbuf[pl.ds(0, HIST), :] = jnp.zeros((HIST, dl), F32)
            pbuf[pl.ds(0, HIST), :] = jnp.zeros((HIST, dl), F32)
            hcar[...] = jnp.zeros((1, dl), F32)

        @pl.when(i > 0)
        def _():
            xbuf[pl.ds(0, HIST), :] = xbuf[pl.ds(tc, HIST), :]
            pbuf[pl.ds(0, HIST), :] = pbuf[pl.ds(tc, HIST), :]

        xbuf[pl.ds(HIST, tc), :] = z_ref[:, pl.ds(0, dl)]
        pbuf[pl.ds(HIST, tc), :] = z_ref[:, pl.ds(2 * dl, dl)]
        sp = _softplus_neg(ap_ref[...])
        for h in range(n_heads):
            sl = pl.ds(h * hd, hd)
            xc = _conv_fwd(_conv_taps(xbuf, sl, tc), cw_ref, cb_ref, sl)
            ra, ii, la, a = _lru_gates(xc, wa_ref[h], ba_ref[:, sl], wx_ref[h], bx_ref[:, sl], sp[:, h * hd:(h + 1) * hd])
            e = _neg_expm1(2.0 * la)
            a_s[:, sl] = a
            b_s[:, sl] = jnp.sqrt(e) * ii * xc
            for slot, val in ((_AUX_R, ra), (_AUX_I, ii), (_AUX_X, xc), (_AUX_E, e)):
                aux_ref[slot, :, sl] = val

        def step(t, hprev):
            hnew = a_s[pl.ds(t, 1), :] * hprev + b_s[pl.ds(t, 1), :]
            hs_ref[pl.ds(t, 1), :] = hnew
            return hnew

        hcar[...] = lax.fori_loop(0, tc, step, hcar[...], unroll=SCAN_UNROLL)
        for h in range(n_heads):
            sl = pl.ds(h * hd, hd)
            m_ref[:, sl] = (hs_ref[:, sl] * _gelu(z_ref[:, pl.ds(dl + h * hd, hd)])).astype(BF16)
        for g in range(n_groups):
            cols = pl.ds(g * gd, gd)
            dlt, _ = _pool_delta(pbuf, cols, POOL_WINDOWS[g], i * tc, tc)
            q = _dot(dlt.astype(BF16), pw_ref[g], "nn") + pb_ref[:, cols]
            m_ref[:, pl.ds(dl + g * gd, gd)] = (q * ps_ref[:, cols]).astype(BF16)

    return pl.pallas_call(
        body, name=name,
        out_shape=(jax.ShapeDtypeStruct((tp, 2 * dl), BF16), jax.ShapeDtypeStruct((tp, dl), F32),
                   jax.ShapeDtypeStruct((_AUX_N, tp, dl), F32)),
        grid=(tp // tc,),
        in_specs=[pl.BlockSpec((tc, 3 * dl), lambda i: (i, 0))] + _mix_param_specs(mp, layer, None),
        out_specs=(pl.BlockSpec((tc, 2 * dl), lambda i: (i, 0)), pl.BlockSpec((tc, dl), lambda i: (i, 0)),
                   pl.BlockSpec((_AUX_N, tc, dl), lambda i: (0, i, 0))),
        scratch_shapes=[pltpu.VMEM((HIST + tc, dl), F32), pltpu.VMEM((HIST + tc, dl), F32),
                        pltpu.VMEM((tc, dl), F32), pltpu.VMEM((tc, dl), F32), pltpu.VMEM((1, dl), F32)],
        compiler_params=_params(("arbitrary",)),
    )(z, *_mix_param_list(mp))


_MIX_GRADS = ("conv_w", "conv_b", "wa", "ba", "wx", "bx", "a_param", "pool_w", "pool_b", "pool_scale")
_AUX_R, _AUX_I, _AUX_X, _AUX_E, _AUX_N = 0, 1, 2, 3, 4


def _mix_bwd(z, hs, aux, dm, mp, layer, *, name):
    tp = z.shape[0]
    dl = z.shape[1] // 3
    n_heads, hd = mp["wa"].shape[1], mp["wa"].shape[2]
    n_groups, gd = mp["pool_w"].shape[1], mp["pool_w"].shape[2]
    tc = _tile(tp, 352)
    nc = tp // tc
    per = tc // HIST

    def body(z_ref, zp_ref, hs_ref, hsp_ref, aux_ref, dm_ref, cw_ref, cb_ref, wa_ref, ba_ref, wx_ref, bx_ref, ap_ref, pw_ref, pb_ref,
             ps_ref, dz_ref, dcw_ref, dcb_ref, dwa_ref, dba_ref, dwx_ref, dbx_ref, dap_ref, dpw_ref, dpb_ref, dps_ref,
             xbuf, pbuf, hbuf, dxbuf, ddbuf, a_s, lam_s, ccar):
        i = pl.program_id(0)
        ci = nc - 1 - i

        @pl.when(i == 0)
        def _():
            dxbuf[pl.ds(tc, HIST), :] = jnp.zeros((HIST, dl), F32)
            ddbuf[pl.ds(tc, HIST), :] = jnp.zeros((HIST, dl), F32)
            ccar[...] = jnp.zeros((1, dl), F32)
            for ref in (dcw_ref, dcb_ref, dwa_ref, dba_ref, dwx_ref, dbx_ref, dap_ref, dpw_ref, dpb_ref, dps_ref):
                ref[...] = jnp.zeros(ref.shape, F32)

        @pl.when(ci == 0)
        def _():
            xbuf[pl.ds(0, HIST), :] = jnp.zeros((HIST, dl), F32)
            pbuf[pl.ds(0, HIST), :] = jnp.zeros((HIST, dl), F32)
            hbuf[pl.ds(0, HIST), :] = jnp.zeros((HIST, dl), F32)

        @pl.when(ci > 0)
        def _():
            xbuf[pl.ds(0, HIST), :] = zp_ref[:, pl.ds(0, dl)]
            pbuf[pl.ds(0, HIST), :] = zp_ref[:, pl.ds(2 * dl, dl)]
            hbuf[pl.ds(0, HIST), :] = hsp_ref[...]

        xbuf[pl.ds(HIST, tc), :] = z_ref[:, pl.ds(0, dl)]
        pbuf[pl.ds(HIST, tc), :] = z_ref[:, pl.ds(2 * dl, dl)]
        hbuf[pl.ds(HIST, tc), :] = hs_ref[...]
        sp = _softplus_neg(ap_ref[...])

        for h in range(n_heads):
            sl = pl.ds(h * hd, hd)
            a_s[:, sl] = jnp.exp(-LRU_C * aux_ref[_AUX_R, :, sl] * sp[:, h * hd:(h + 1) * hd])
            gel, gel_grad = _gelu_and_grad(z_ref[:, pl.ds(dl + h * hd, hd)])
            dya = dm_ref[:, sl]
            lam_s[:, sl] = dya * gel
            dz_ref[:, pl.ds(dl + h * hd, hd)] = (dya * hs_ref[:, sl] * gel_grad).astype(BF16)

        def step(r, carry):
            t = tc - 1 - r
            lam = lam_s[pl.ds(t, 1), :] + carry
            lam_s[pl.ds(t, 1), :] = lam
            return a_s[pl.ds(t, 1), :] * lam

        ccar[...] = lax.fori_loop(0, tc, step, ccar[...], unroll=SCAN_UNROLL)

        for h in range(n_heads):
            sl = pl.ds(h * hd, hd)
            sp_h = sp[:, h * hd:(h + 1) * hd]
            lam = lam_s[:, sl]
            a = a_s[:, sl]
            ra = aux_ref[_AUX_R, :, sl]
            ii = aux_ref[_AUX_I, :, sl]
            xc = aux_ref[_AUX_X, :, sl]
            e = aux_ref[_AUX_E, :, sl]
            inv_mult = lax.rsqrt(e)
            mult = e * inv_mult
            hprev = _shifted(hbuf[:, sl], 1)[HIST:HIST + tc]
            lam_i = lam * ii
            lam_m = lam * mult
            dla = lam * hprev * a - lam_i * xc * (a * a) * inv_mult
            dla_r = dla * (-LRU_C) * ra
            dap_ref[:, sl] += jnp.sum(dla_r, axis=0, keepdims=True)
            dpa = dla_r * sp_h * (1.0 - ra)
            dpx = lam_m * xc * ii * (1.0 - ii)
            dba_ref[:, sl] += jnp.sum(dpa, axis=0, keepdims=True)
            dbx_ref[:, sl] += jnp.sum(dpx, axis=0, keepdims=True)
            xb = xc.astype(BF16)
            dpa_b = dpa.astype(BF16)
            dpx_b = dpx.astype(BF16)
            dwa_ref[h] += _dot(xb, dpa_b, "tn")
            dwx_ref[h] += _dot(xb, dpx_b, "tn")
            dxc = lam_m * ii + _dot(dpa_b, wa_ref[h], "nt") + _dot(dpx_b, wx_ref[h], "nt")
            dxbuf[pl.ds(0, tc), sl] = dxc
            dcb_ref[:, sl] += jnp.sum(dxc, axis=0, keepdims=True)
            taps = _conv_taps(xbuf, sl, tc)
            dx_all = dxbuf[:, sl]
            dzx = jnp.zeros((tc, hd), F32)
            for k in range(CONV_WIDTH):
                dcw_ref[k:k + 1, sl] += jnp.sum(dxc * taps[k], axis=0, keepdims=True)
                dzx = dzx + cw_ref[k:k + 1, sl] * _shifted(dx_all, k - (CONV_WIDTH - 1))[0:tc]
            dz_ref[:, sl] = dzx.astype(BF16)
            dxbuf[pl.ds(tc, HIST), sl] = dxbuf[pl.ds(0, HIST), sl]

        for g in range(n_groups):
            cols = pl.ds(g * gd, gd)
            win = POOL_WINDOWS[g]
            dlt, inv_cnt = _pool_delta(pbuf, cols, win, ci * tc, tc)
            db = dlt.astype(BF16)
            q = _dot(db, pw_ref[g], "nn") + pb_ref[:, cols]
            dyb = dm_ref[:, pl.ds(dl + g * gd, gd)]
            dps_ref[:, cols] += jnp.sum(dyb * q, axis=0, keepdims=True)
            dq = dyb * ps_ref[:, cols]
            dpb_ref[:, cols] += jnp.sum(dq, axis=0, keepdims=True)
            dqb = dq.astype(BF16)
            dpw_ref[g] += _dot(db, dqb, "tn")
            dd = _dot(dqb, pw_ref[g], "nt")
            ddbuf[pl.ds(0, tc), cols] = dd * inv_cnt
            dzp = _window_sum(ddbuf[:, cols], win, -1)[0:tc] - dd
            dz_ref[:, pl.ds(2 * dl + g * gd, gd)] = dzp.astype(BF16)
            ddbuf[pl.ds(tc, HIST), cols] = ddbuf[pl.ds(0, HIST), cols]

        @pl.when(i == nc - 1)
        def _():
            dap_ref[...] = dap_ref[...] * (-_sigmoid(-ap_ref[...]))

    rev = lambda i: (nc - 1 - i, 0)
    prev = lambda i: (jnp.maximum((nc - 1 - i) * per - 1, 0), 0)
    const = lambda a: pl.BlockSpec(a.shape[1:], lambda i: (0,) * (a.ndim - 1))
    plist = _mix_param_list(mp)
    grad_shapes = [jax.ShapeDtypeStruct(a.shape[1:], F32) for a in plist]
    buf = lambda rows: pltpu.VMEM((rows, dl), F32)
    outs = pl.pallas_call(
        body, name=name,
        out_shape=[jax.ShapeDtypeStruct((tp, 3 * dl), BF16)] + grad_shapes,
        grid=(nc,),
        in_specs=[pl.BlockSpec((tc, 3 * dl), rev), pl.BlockSpec((HIST, 3 * dl), prev),
                  pl.BlockSpec((tc, dl), rev), pl.BlockSpec((HIST, dl), prev),
                  pl.BlockSpec((_AUX_N, tc, dl), lambda i: (0, nc - 1 - i, 0)),
                  pl.BlockSpec((tc, 2 * dl), rev)] + _mix_param_specs(mp, layer, None),
        out_specs=[pl.BlockSpec((tc, 3 * dl), rev)] + [const(a) for a in plist],
        scratch_shapes=[buf(HIST + tc), buf(HIST + tc), buf(HIST + tc), buf(tc + HIST), buf(tc + HIST),
                        buf(tc), buf(tc), buf(1)],
        compiler_params=_params(("arbitrary",)),
    )(z, z, hs, hs, aux, dm, *plist)
    return outs[0], dict(zip(_MIX_GRADS, outs[1:]))


def _ffn_fwd_out(h, act, w_out, tag, deps):
    tp, d = h.shape
    f = w_out.shape[1]
    return _matmul(act, w_out, mode="nn", m=tp, n=d, kdim=f, tm=_tile(tp, 1056), tn=_tile(d, 512, LANES),
                   tk=_tile(f, K_TILE, LANES), out_dtype=F32, scale=0.5, residual=h, b_layer=0, name=f"{tag}_out", deps=deps)


def _ffn_bwd(dh, dhb, saved, norm, layer, w_in, w_out, tag, emit, deps):
    h, n_act, gu, act = saved
    tp, d = h.shape
    f = w_out.shape[1]
    dz = _ffn_bwd_act(dhb, w_out, 0, gu, name=f"{tag}_bact", deps=deps)
    tok = emit("w_out", _matmul(act, dhb, mode="tn", m=f, n=d, kdim=tp, tm=_tile(f, 512, LANES), tn=_tile(d, WGRAD_TILE, LANES), tk=tp,
                                out_dtype=BF16, scale=0.5, name=f"{tag}_dwout"))
    tok = emit("w_in", _matmul(n_act, dz, mode="tn", m=d, n=2 * f, kdim=tp, tm=_tile(d, WGRAD_TILE, LANES), tn=_tile(f, 512, LANES),
                               tk=tp, out_dtype=BF16, b_split=True, name=f"{tag}_dwin", deps=tok))
    half = dict(mode="nt", m=tp, n=d, kdim=f, tm=_tile(tp, 1056), tn=_tile(d, 512, LANES), tk=f, out_dtype=F32, a_split=True, b_layer=0)
    dn = _matmul(dz, w_in, name=f"{tag}_dn0", deps=tok, **half)
    dn = _matmul(dz, w_in, k_first=1, residual=dn, name=f"{tag}_dn1", **half)
    return _rms_bwd(h, norm, layer, dn, dh, name=f"{tag}_brms", deps=emit("flush", dn))


def _mix_block_fwd(h, n_act, layer, w_in, w_out, mp, tag, deps):
    tp, d = h.shape
    d_in = w_in.shape[2]
    z = _matmul(n_act, w_in, mode="nn", m=tp, n=d_in, kdim=d, tm=_tile(tp, 2112), tn=_tile(d_in, 512, LANES), tk=d,
                out_dtype=F32, b_layer=0, name=f"{tag}_in", deps=deps)
    m_act, hs, aux = _mix_fwd(z, mp, layer, name=f"{tag}_mix")
    h_out = _matmul(m_act, w_out, mode="nn", m=tp, n=d, kdim=d, tm=_tile(tp, 2112), tn=_tile(d, 512, LANES), tk=d,
                    out_dtype=F32, residual=h, b_layer=0, name=f"{tag}_out")
    return h_out, (h, n_act, z, hs, aux, m_act)


def _mix_block_bwd(dh, dhb, saved, norm, layer, w_in, w_out, mp, tag, emit, deps):
    h, n_act, z, hs, aux, m_act = saved
    tp, d = h.shape
    d_in = w_in.shape[2]
    dm = _matmul(dhb, w_out, mode="nt", m=tp, n=d, kdim=d, tm=_tile(tp, 2112), tn=_tile(d, 512, LANES), tk=d,
                 out_dtype=F32, b_layer=0, name=f"{tag}_dm", deps=deps)
    tok = emit("w_out", _matmul(m_act, dhb, mode="tn", m=d, n=d, kdim=tp, tm=_tile(d, 512, LANES), tn=_tile(d, WGRAD_TILE, LANES), tk=tp,
                                out_dtype=BF16, name=f"{tag}_dwout"))
    dz, g_mix = _mix_bwd(z, hs, aux, dm, mp, layer, name=f"{tag}_bmix")
    tok = emit("w_in", _matmul(n_act, dz, mode="tn", m=d, n=d_in, kdim=tp, tm=_tile(d, WGRAD_TILE, LANES), tn=_tile(d_in, 512, LANES),
                               tk=tp, out_dtype=BF16, name=f"{tag}_dwin", deps=tok))
    dn = _matmul(dz, w_in, mode="nt", m=tp, n=d, kdim=d_in, tm=_tile(tp, 1056), tn=_tile(d, 1024, LANES), tk=d_in,
                 out_dtype=F32, b_layer=0, name=f"{tag}_dn", deps=tok)
    dh_in, dhb_in, g_norm = _rms_bwd(h, norm, layer, dn, dh, name=f"{tag}_brms", deps=emit("flush", dn))
    return dh_in, dhb_in, g_norm, g_mix


def _mesh_pos():
    x, y, c = lax.axis_index("x"), lax.axis_index("y"), lax.axis_index("c")
    chips = [(1 - x, y), (x, 1 - y), (1 - x, 1 - y)]
    return x, y, c, chips


def _block(ref, axis, j, size):
    idx = [slice(None)] * len(ref.shape)
    idx[axis] = pl.ds(j * size, size)
    return ref.at[tuple(idx)]


_HBM = pl.BlockSpec(memory_space=pltpu.HBM)
_SEM = pl.BlockSpec(memory_space=pltpu.SEMAPHORE)
_ANY = pl.BlockSpec(memory_space=pl.ANY)
_EFFECT = pltpu.SideEffectType.DATAFLOW_SIDE_EFFECTING
_N_OUT = 4


def _in_hbm(a):
    return pltpu.with_memory_space_constraint(a, pltpu.HBM)


def _gather_start(shards, axes, deps, *, name):
    n = len(shards)
    sizes = [s.shape[ax] for s, ax in zip(shards, axes)]
    full_shapes = [s.shape[:ax] + (N_DEV * s.shape[ax],) + s.shape[ax + 1:] for s, ax in zip(shards, axes)]

    def body(*refs):
        srcs, lands = refs[:n], refs[n:2 * n]
        send_sems, recv_sems = refs[2 * n + len(deps)], refs[2 * n + len(deps) + 1]
        token = refs[-1]
        x, y, c, chips = _mesh_pos()
        targets = [(x, y, 1 - c)] + [(*chip, c) for chip in chips]
        for a in range(n):
            dst = _block(lands[a], axes[a], 4 * x + 2 * y + c, sizes[a])
            for k, to in enumerate(targets):
                pltpu.make_async_remote_copy(src_ref=srcs[a], dst_ref=dst, send_sem=send_sems.at[_N_OUT * a + k],
                                             recv_sem=recv_sems.at[_N_OUT * a + k],
                                             device_id=to, device_id_type=MESH).start()
        token[...] = jnp.zeros(token.shape, F32)

    lands0 = [lax.empty(fs, s.dtype) for fs, s in zip(full_shapes, shards)]
    outs = pl.pallas_call(
        body, name=name,
        out_shape=(pltpu.SemaphoreType.DMA((n * _N_OUT,)), pltpu.SemaphoreType.DMA((n * _N_OUT,)),
                   *[pltpu.HBM(s.shape, s.dtype) for s in shards], *[pltpu.HBM(fs, s.dtype) for fs, s in zip(full_shapes, shards)],
                   jax.ShapeDtypeStruct((8, LANES), F32)),
        in_specs=[_HBM] * (2 * n) + [_ANY] * len(deps),
        out_specs=(_SEM, _SEM, *[_HBM] * (2 * n), pl.BlockSpec(memory_space=pltpu.VMEM)),
        input_output_aliases={a: 2 + a for a in range(2 * n)},
        compiler_params=pltpu.CompilerParams(has_side_effects=_EFFECT),
    )(*[_in_hbm(s) for s in shards], *[_in_hbm(z) for z in lands0], *deps)
    handle = dict(send_sems=outs[0], recv_sems=outs[1], srcs=list(outs[2:2 + n]), lands=list(outs[2 + n:2 + 2 * n]), axes=list(axes))
    return handle, outs[-1]


def _gather_wait(handle, after, *, name):
    srcs, lands, axes = handle["srcs"], handle["lands"], handle["axes"]
    n = len(srcs)
    sizes = [s.shape[ax] for s, ax in zip(srcs, axes)]

    def body(*refs):
        src_refs, land_refs = refs[:n], refs[n:2 * n]
        send_sems, recv_sems = refs[2 * n], refs[2 * n + 1]
        x, y, c, chips = _mesh_pos()
        peers = [(x, y, 1 - c)] + [(*chip, c) for chip in chips]
        for a in range(n):
            for k, dev in enumerate(peers):
                cp = pltpu.make_async_remote_copy(
                    src_ref=src_refs[a], dst_ref=_block(land_refs[a], axes[a], 4 * dev[0] + 2 * dev[1] + dev[2], sizes[a]),
                    send_sem=send_sems.at[_N_OUT * a + k], recv_sem=recv_sems.at[_N_OUT * a + k], device_id=dev,
                    device_id_type=MESH)
                cp.wait_send()
                cp.wait_recv()

    outs = pl.pallas_call(
        body, name=name,
        out_shape=(*[pltpu.HBM(s.shape, s.dtype) for s in srcs], *[pltpu.HBM(z.shape, z.dtype) for z in lands]),
        in_specs=[_HBM] * (2 * n) + [_SEM, _SEM] + [_ANY] * len(after),
        out_specs=tuple([_HBM] * (2 * n)),
        input_output_aliases={a: a for a in range(2 * n)},
        compiler_params=pltpu.CompilerParams(has_side_effects=_EFFECT),
    )(*srcs, *lands, handle["send_sems"], handle["recv_sems"], *after)
    return list(outs[:n]), list(outs[n:])


def _place_own(shard, full, axis, dev, *, name):
    nd = shard.ndim
    rows, cols = shard.shape[-2:]
    tr = _tile(rows, 512, _row_mult(shard.dtype)) if rows % _row_mult(shard.dtype) == 0 else rows
    nrb = rows // tr
    block = shard.shape[:-2] + (tr, cols)

    def in_map(r, dev_ref):
        return (0,) * (nd - 2) + (r, 0)

    def out_map(r, dev_ref):
        idx = [0] * nd
        idx[nd - 2] = r
        idx[axis] = dev_ref[0] * (nrb if axis == nd - 2 else 1) + idx[axis]
        return tuple(idx)

    def body(dev_ref, src_ref, full_ref, out_ref):
        out_ref[...] = src_ref[...]

    grid_spec = pltpu.PrefetchScalarGridSpec(
        num_scalar_prefetch=1, grid=(nrb,),
        in_specs=[pl.BlockSpec(block, in_map), _ANY], out_specs=pl.BlockSpec(block, out_map))
    return pl.pallas_call(body, name=name, grid_spec=grid_spec, out_shape=jax.ShapeDtypeStruct(full.shape, full.dtype),
                          input_output_aliases={2: 0}, compiler_params=_params(("arbitrary",)))(dev, shard, full)


def _gather_pass_start(fulls, axes, *, name):
    n = len(fulls)
    sizes = [z.shape[ax] // N_DEV for z, ax in zip(fulls, axes)]

    def body(*refs):
        srcs = refs[:n]
        send_sems, recv_sems = refs[n], refs[n + 1]
        token = refs[-1]
        x, y, c, chips = _mesh_pos()
        for a in range(n):
            for j, chip in enumerate(chips):
                blk = _block(srcs[a], axes[a], 4 * chip[0] + 2 * chip[1] + c, sizes[a])
                pltpu.make_async_remote_copy(src_ref=blk, dst_ref=blk, send_sem=send_sems.at[3 * a + j], recv_sem=recv_sems.at[3 * a + j],
                                             device_id=(x, y, 1 - c), device_id_type=MESH).start()
        token[...] = jnp.zeros(token.shape, F32)

    outs = pl.pallas_call(
        body, name=name,
        out_shape=(pltpu.SemaphoreType.DMA((3 * n,)), pltpu.SemaphoreType.DMA((3 * n,)), *[pltpu.HBM(z.shape, z.dtype) for z in fulls],
                   jax.ShapeDtypeStruct((8, LANES), F32)),
        in_specs=[_HBM] * n, out_specs=(_SEM, _SEM, *[_HBM] * n, pl.BlockSpec(memory_space=pltpu.VMEM)),
        input_output_aliases={a: 2 + a for a in range(n)},
        compiler_params=pltpu.CompilerParams(has_side_effects=_EFFECT),
    )(*[_in_hbm(z) for z in fulls])
    return dict(send_sems=outs[0], recv_sems=outs[1], fulls=list(outs[2:2 + n]), axes=list(axes)), outs[-1]


def _gather_pass_wait(handle, after, *, name):
    fulls, axes = handle["fulls"], handle["axes"]
    n = len(fulls)
    sizes = [z.shape[ax] // N_DEV for z, ax in zip(fulls, axes)]

    def body(*refs):
        bufs = refs[:n]
        send_sems, recv_sems = refs[n], refs[n + 1]
        x, y, c, chips = _mesh_pos()
        for a in range(n):
            for j, chip in enumerate(chips):
                cp = pltpu.make_async_remote_copy(
                    src_ref=_block(bufs[a], axes[a], 4 * chip[0] + 2 * chip[1] + c, sizes[a]),
                    dst_ref=_block(bufs[a], axes[a], 4 * chip[0] + 2 * chip[1] + (1 - c), sizes[a]),
                    send_sem=send_sems.at[3 * a + j], recv_sem=recv_sems.at[3 * a + j], device_id=(x, y, 1 - c),
                    device_id_type=MESH)
                cp.wait_send()
                cp.wait_recv()

    outs = pl.pallas_call(
        body, name=name,
        out_shape=tuple(pltpu.HBM(z.shape, z.dtype) for z in fulls),
        in_specs=[_HBM] * n + [_SEM, _SEM] + [_ANY] * len(after), out_specs=tuple([_HBM] * n),
        input_output_aliases={a: a for a in range(n)},
        compiler_params=pltpu.CompilerParams(has_side_effects=_EFFECT),
    )(*fulls, handle["send_sems"], handle["recv_sems"], *after)
    return list(outs)


def _rs_exchange_sibling(tensors, axes, *, name):
    n = len(tensors)
    n_layers = [len(t) for t in tensors]
    sizes = [t[0].shape[ax] // N_DEV for t, ax in zip(tensors, axes)]
    blk_shapes = [t[0].shape[:ax] + (sz,) + t[0].shape[ax + 1:] for t, ax, sz in zip(tensors, axes, sizes)]
    flat = [g for t in tensors for g in t]
    offs = [sum(n_layers[:a]) for a in range(n)]

    def body(*refs):
        srcs = refs[:len(flat)]
        lands = refs[len(flat):len(flat) + n]
        send_sems, recv_sems = refs[len(flat) + n:]
        x, y, c, _ = _mesh_pos()
        sib = (x, y, 1 - c)
        for a in range(n):
            for l in range(n_layers[a]):
                for i in range(N_CHIP):
                    pltpu.make_async_remote_copy(
                        src_ref=_block(srcs[offs[a] + l], axes[a], 2 * i + (1 - c), sizes[a]), dst_ref=lands[a].at[i, l],
                        send_sem=send_sems.at[a], recv_sem=recv_sems.at[a], device_id=sib, device_id_type=MESH).start()
        for a in range(n):
            pltpu.make_async_remote_copy(src_ref=lands[a], dst_ref=lands[a], send_sem=send_sems.at[a], recv_sem=recv_sems.at[a],
                                         device_id=sib, device_id_type=MESH).wait()

    any_spec = pl.BlockSpec(memory_space=pl.ANY)
    return pl.pallas_call(
        body, name=name,
        out_shape=[jax.ShapeDtypeStruct((N_CHIP, nl) + bs, t[0].dtype) for nl, bs, t in zip(n_layers, blk_shapes, tensors)],
        in_specs=[any_spec] * len(flat), out_specs=[any_spec] * n,
        scratch_shapes=[pltpu.SemaphoreType.DMA((n,)), pltpu.SemaphoreType.DMA((n,))],
    )(*flat)


def _rs_add_sibling(layers, axis, land, my_c, *, name):
    n_layers = len(layers)
    shape = layers[0].shape
    size = shape[axis] // N_DEV
    blk_shape = shape[:axis] + (size,) + shape[axis + 1:]
    nd = len(shape)
    rows = blk_shape[-2]
    tr = _tile(rows, 512, _row_mult(layers[0].dtype))
    inner = (tr, blk_shape[-1])
    lead = blk_shape[:-2]
    if lead:
        raise ValueError("blocked gradients are 2-D")
    nrb = rows // tr

    def src_map(l):
        def imap(i, r, c_ref):
            j = 2 * i + c_ref[0]
            return (j * nrb + r, 0) if axis == 0 else (r, j)
        return imap

    def body(c_ref, *refs):
        srcs = refs[:n_layers]
        land_ref = refs[n_layers]
        out_ref = refs[n_layers + 1]
        for l in range(n_layers):
            out_ref[l] = (srcs[l][...].astype(F32) + land_ref[l].astype(F32)).astype(out_ref.dtype)

    grid_spec = pltpu.PrefetchScalarGridSpec(
        num_scalar_prefetch=1, grid=(N_CHIP, nrb),
        in_specs=[pl.BlockSpec(inner, src_map(l)) for l in range(n_layers)]
        + [pl.BlockSpec((None, n_layers) + inner, lambda i, r, c_ref: (i, 0, r, 0))],
        out_specs=pl.BlockSpec((None, n_layers) + inner, lambda i, r, c_ref: (i, 0, r, 0)),
    )
    return pl.pallas_call(
        body, name=name, grid_spec=grid_spec,
        out_shape=jax.ShapeDtypeStruct((N_CHIP, n_layers) + blk_shape, layers[0].dtype),
        compiler_params=_params(("arbitrary", "arbitrary")),
    )(my_c, *layers, land)


def _rs_sibling_start(g, axis, *, name):
    size = g.shape[axis] // N_DEV
    land = lax.empty((N_CHIP, 1) + g.shape[:axis] + (size,) + g.shape[axis + 1:], g.dtype)

    def body(src, dst, send_sems, recv_sems, src_thru, dst_thru, token):
        x, y, c, _ = _mesh_pos()
        for i in range(N_CHIP):
            pltpu.make_async_remote_copy(src_ref=_block(src, axis, 2 * i + (1 - c), size), dst_ref=dst.at[i, 0], send_sem=send_sems.at[i],
                                         recv_sem=recv_sems.at[i], device_id=(x, y, 1 - c), device_id_type=MESH).start()
        token[...] = jnp.zeros(token.shape, F32)

    outs = pl.pallas_call(
        body, name=name,
        out_shape=(pltpu.SemaphoreType.DMA((N_CHIP,)), pltpu.SemaphoreType.DMA((N_CHIP,)), pltpu.HBM(g.shape, g.dtype),
                   pltpu.HBM(land.shape, land.dtype), jax.ShapeDtypeStruct((8, LANES), F32)),
        in_specs=[_HBM, _HBM], out_specs=(_SEM, _SEM, _HBM, _HBM, pl.BlockSpec(memory_space=pltpu.VMEM)),
        input_output_aliases={0: 2, 1: 3},
        compiler_params=pltpu.CompilerParams(has_side_effects=_EFFECT),
    )(_in_hbm(g), _in_hbm(land))
    return dict(send_sems=outs[0], recv_sems=outs[1], src=outs[2], land=outs[3], axis=axis), outs[4]


def _rs_sibling_wait(handle, after, *, name):
    src, land, axis = handle["src"], handle["land"], handle["axis"]
    size = src.shape[axis] // N_DEV

    def body(src_ref, dst_ref, send_sems, recv_sems, *rest):
        x, y, c, _ = _mesh_pos()
        for i in range(N_CHIP):
            cp = pltpu.make_async_remote_copy(src_ref=_block(src_ref, axis, 2 * i + (1 - c), size), dst_ref=dst_ref.at[i, 0],
                                              send_sem=send_sems.at[i], recv_sem=recv_sems.at[i], device_id=(x, y, 1 - c),
                                              device_id_type=MESH)
            cp.wait_send()
            cp.wait_recv()

    outs = pl.pallas_call(
        body, name=name,
        out_shape=(pltpu.HBM(src.shape, src.dtype), pltpu.HBM(land.shape, land.dtype)),
        in_specs=[_HBM, _HBM, _SEM, _SEM] + [_ANY] * len(after), out_specs=(_HBM, _HBM),
        input_output_aliases={0: 0, 1: 1},
        compiler_params=pltpu.CompilerParams(has_side_effects=_EFFECT),
    )(src, land, handle["send_sems"], handle["recv_sems"], *after)
    return outs[0], outs[1]


def _rs_chips_start(part, *, name):
    land = lax.empty(part.shape, part.dtype)

    def body(src, dst, send_sems, recv_sems, src_thru, dst_thru, token):
        x, y, c, chips = _mesh_pos()
        for j, chip in enumerate(chips):
            pltpu.make_async_remote_copy(src_ref=src.at[2 * chip[0] + chip[1]], dst_ref=dst.at[2 * x + y], send_sem=send_sems.at[j],
                                         recv_sem=recv_sems.at[j], device_id=(*chip, c), device_id_type=MESH).start()
        token[...] = jnp.zeros(token.shape, F32)

    outs = pl.pallas_call(
        body, name=name,
        out_shape=(pltpu.SemaphoreType.DMA((3,)), pltpu.SemaphoreType.DMA((3,)), pltpu.HBM(part.shape, part.dtype),
                   pltpu.HBM(land.shape, land.dtype), jax.ShapeDtypeStruct((8, LANES), F32)),
        in_specs=[_HBM, _HBM], out_specs=(_SEM, _SEM, _HBM, _HBM, pl.BlockSpec(memory_space=pltpu.VMEM)),
        input_output_aliases={0: 2, 1: 3},
        compiler_params=pltpu.CompilerParams(has_side_effects=_EFFECT),
    )(_in_hbm(part), _in_hbm(land))
    return dict(send_sems=outs[0], recv_sems=outs[1], src=outs[2], land=outs[3]), outs[4]


def _rs_chips_wait(handle, after, *, name):
    def body(src, dst, send_sems, recv_sems, *rest):
        x, y, c, chips = _mesh_pos()
        for j, chip in enumerate(chips):
            cp = pltpu.make_async_remote_copy(src_ref=src.at[2 * chip[0] + chip[1]], dst_ref=dst.at[2 * chip[0] + chip[1]],
                                              send_sem=send_sems.at[j], recv_sem=recv_sems.at[j], device_id=(*chip, c),
                                              device_id_type=MESH)
            cp.wait_send()
            cp.wait_recv()

    src, land = handle["src"], handle["land"]
    outs = pl.pallas_call(
        body, name=name,
        out_shape=(pltpu.HBM(src.shape, src.dtype), pltpu.HBM(land.shape, land.dtype)),
        in_specs=[_HBM, _HBM, _SEM, _SEM] + [_ANY] * len(after), out_specs=(_HBM, _HBM),
        input_output_aliases={0: 0, 1: 1},
        compiler_params=pltpu.CompilerParams(has_side_effects=_EFFECT),
    )(src, land, handle["send_sems"], handle["recv_sems"], *after)
    return outs[0], outs[1]


def _adam_math(w, g, m, v):
    m_new = ADAM_B1 * m + (1.0 - ADAM_B1) * g
    v_new = ADAM_B2 * v + (1.0 - ADAM_B2) * (g * g)
    m_hat = m_new / (1.0 - ADAM_B1 ** ADAM_STEP)
    v_hat = v_new / (1.0 - ADAM_B2 ** ADAM_STEP)
    delta = -ADAM_LR * (m_hat / (jnp.sqrt(v_hat) + ADAM_EPS) + ADAM_WD * w)
    return delta, m_new, v_new


def _sum_adam(part, land, slots, w, m, v, layer, prev, *, name, deps=()):
    n_layers, rows, cols = w.shape
    tr = _tile(rows, 256, _row_mult(land.dtype))
    n_prev = 0 if prev is None else 4

    def body(slots_ref, p0_ref, p1_ref, p2_ref, p3_ref, w_ref, m_ref, v_ref, *rest):
        g_ref, d_ref, mo_ref, vo_ref = rest[n_prev + len(deps):]
        g = p0_ref[...].astype(F32)
        for ref in (p1_ref, p2_ref, p3_ref):
            g = g + ref[...].astype(F32)
        delta, m_new, v_new = _adam_math(w_ref[...], g, m_ref[...], v_ref[...])
        g_ref[...] = g
        d_ref[...] = delta
        mo_ref[...] = m_new
        vo_ref[...] = v_new

    blk = pl.BlockSpec((None, tr, cols), lambda r, s: (layer, r, 0))
    slot = lambda j: pl.BlockSpec((None, None, tr, cols), lambda r, s: (s[j], 0, r, 0))
    shp = jax.ShapeDtypeStruct((n_layers, rows, cols), F32)
    grid_spec = pltpu.PrefetchScalarGridSpec(
        num_scalar_prefetch=1, grid=(rows // tr,),
        in_specs=[slot(0), slot(1), slot(2), slot(3), blk, blk, blk] + [_ANY] * (n_prev + len(deps)),
        out_specs=(blk, blk, blk, blk))
    return pl.pallas_call(
        body, name=name, out_shape=(shp, shp, shp, shp), grid_spec=grid_spec,
        input_output_aliases={8 + i: i for i in range(n_prev)},
        compiler_params=_params(("parallel",)),
    )(slots, part, land, land, land, w, m, v, *(prev or ()), *deps)


def _sum_slots(part, land, slots, *, name):
    _, _, rows, cols = land.shape
    tr = _tile(rows, 256, 8)

    def body(slots_ref, p0_ref, p1_ref, p2_ref, p3_ref, g_ref):
        g_ref[...] = p0_ref[...] + p1_ref[...] + p2_ref[...] + p3_ref[...]

    slot = lambda j: pl.BlockSpec((None, None, tr, cols), lambda r, s: (s[j], 0, r, 0))
    grid_spec = pltpu.PrefetchScalarGridSpec(
        num_scalar_prefetch=1, grid=(rows // tr,), in_specs=[slot(0), slot(1), slot(2), slot(3)],
        out_specs=pl.BlockSpec((tr, cols), lambda r, s: (r, 0)))
    return pl.pallas_call(body, name=name, out_shape=jax.ShapeDtypeStruct((rows, cols), F32), grid_spec=grid_spec,
                          compiler_params=_params(("parallel",)))(slots, part, land, land, land)


def _adam_flat(w, g, m, v, *, name):
    rows, cols = w.shape
    tr = _tile(rows, 256, 8)

    def body(w_ref, g_ref, m_ref, v_ref, d_ref, mo_ref, vo_ref):
        delta, m_new, v_new = _adam_math(w_ref[...], g_ref[...], m_ref[...], v_ref[...])
        d_ref[...] = delta
        mo_ref[...] = m_new
        vo_ref[...] = v_new

    blk = pl.BlockSpec((tr, cols), lambda r: (r, 0))
    shp = jax.ShapeDtypeStruct((rows, cols), F32)
    return pl.pallas_call(
        body, name=name, out_shape=(shp, shp, shp), grid=(rows // tr,), in_specs=[blk] * 4, out_specs=(blk, blk, blk),
        compiler_params=_params(("parallel",)),
    )(w, g, m, v)


_BIG = (("ffn1_w_in", 1), ("ffn1_w_out", 0), ("w_in", 1), ("w_out", 0), ("ffn2_w_in", 1), ("ffn2_w_out", 0))
_SMALL = ("meta_tokens", "ffn1_norm", "mix_norm", "conv_w", "conv_b", "lru_wa", "lru_ba", "lru_wx", "lru_bx", "lru_a_param",
          "pool_w", "pool_b", "pool_scale", "ffn2_norm", "final_norm")
_SMALL_SHARD_AXIS = {"meta_tokens": 1, "conv_w": 2, "pool_w": 2}
_PACK_COLS = 1024


def _pack(arrs):
    flat = jnp.concatenate([a.reshape(-1) for a in arrs])
    unit = N_DEV * 8 * _PACK_COLS
    total = -(-flat.shape[0] // unit) * unit
    flat = jnp.pad(flat, (0, total - flat.shape[0]))
    return flat.reshape(total // _PACK_COLS, _PACK_COLS)


def _unpack(packed, shapes):
    flat = packed.reshape(-1)
    out, off = [], 0
    for s in shapes:
        size = 1
        for v in s:
            size *= v
        out.append(flat[off:off + size].reshape(s))
        off += size
    return out


def _my_shard(full, axis, dev):
    size = full.shape[axis] // N_DEV
    return lax.dynamic_slice_in_dim(full, dev * size, size, axis)


def kernel(x, meta_tokens, ffn1_norm, ffn1_w_in, ffn1_w_out, mix_norm, w_in, conv_w, conv_b, lru_wa, lru_ba, lru_wx, lru_bx, lru_a_param, pool_w, pool_b, pool_scale, w_out, ffn2_norm, ffn2_w_in, ffn2_w_out, final_norm, loss_target, m_meta_tokens, m_ffn1_norm, m_ffn1_w_in, m_ffn1_w_out, m_mix_norm, m_w_in, m_conv_w, m_conv_b, m_lru_wa, m_lru_ba, m_lru_wx, m_lru_bx, m_lru_a_param, m_pool_w, m_pool_b, m_pool_scale, m_w_out, m_ffn2_norm, m_ffn2_w_in, m_ffn2_w_out, m_final_norm, v_meta_tokens, v_ffn1_norm, v_ffn1_w_in, v_ffn1_w_out, v_mix_norm, v_w_in, v_conv_w, v_conv_b, v_lru_wa, v_lru_ba, v_lru_wx, v_lru_bx, v_lru_a_param, v_pool_w, v_pool_b, v_pool_scale, v_w_out, v_ffn2_norm, v_ffn2_w_in, v_ffn2_w_out, v_final_norm):
    names = ("meta_tokens", "ffn1_norm", "ffn1_w_in", "ffn1_w_out", "mix_norm", "w_in", "conv_w", "conv_b", "lru_wa", "lru_ba",
             "lru_wx", "lru_bx", "lru_a_param", "pool_w", "pool_b", "pool_scale", "w_out", "ffn2_norm", "ffn2_w_in", "ffn2_w_out",
             "final_norm")
    w = dict(zip(names, (meta_tokens, ffn1_norm, ffn1_w_in, ffn1_w_out, mix_norm, w_in, conv_w, conv_b, lru_wa, lru_ba, lru_wx,
                         lru_bx, lru_a_param, pool_w, pool_b, pool_scale, w_out, ffn2_norm, ffn2_w_in, ffn2_w_out, final_norm)))
    mom = dict(zip(names, (m_meta_tokens, m_ffn1_norm, m_ffn1_w_in, m_ffn1_w_out, m_mix_norm, m_w_in, m_conv_w, m_conv_b, m_lru_wa,
                           m_lru_ba, m_lru_wx, m_lru_bx, m_lru_a_param, m_pool_w, m_pool_b, m_pool_scale, m_w_out, m_ffn2_norm,
                           m_ffn2_w_in, m_ffn2_w_out, m_final_norm)))
    vel = dict(zip(names, (v_meta_tokens, v_ffn1_norm, v_ffn1_w_in, v_ffn1_w_out, v_mix_norm, v_w_in, v_conv_w, v_conv_b, v_lru_wa,
                           v_lru_ba, v_lru_wx, v_lru_bx, v_lru_a_param, v_pool_w, v_pool_b, v_pool_scale, v_w_out, v_ffn2_norm,
                           v_ffn2_w_in, v_ffn2_w_out, v_final_norm)))
    n_layers, d = ffn1_norm.shape
    n_meta = meta_tokens.shape[0]
    seq = x.shape[1]
    t_valid = n_meta + seq
    tp = -(-t_valid // ROW_ALIGN) * ROW_ALIGN
    dev = 4 * lax.axis_index("x") + 2 * lax.axis_index("y") + lax.axis_index("c")
    my_c = lax.axis_index("c").astype(jnp.int32).reshape(1)
    dev1 = dev.astype(jnp.int32).reshape(1)
    mx, my = lax.axis_index("x"), lax.axis_index("y")
    slots = jnp.stack([2 * mx + my, 2 * (1 - mx) + my, 2 * mx + (1 - my), 2 * (1 - mx) + (1 - my)]).astype(jnp.int32)

    big_axis = dict(_BIG)
    vec = lambda a: a.reshape(a.shape[0], 1, a.shape[1])
    norms = dict(ffn1=vec(ffn1_norm), mix=vec(mix_norm), ffn2=vec(ffn2_norm))
    units = [(kind, l) for l in range(n_layers) for kind in ("ffn1", "mix", "ffn2")]

    handles = {}
    state = dict(token=None, started=0, first=None)

    def behind(a):
        return a if state["first"] is None else lax.optimization_barrier((state["first"], a))[1]

    def shard(k, l):
        return behind(w[k][l:l + 1]).astype(BF16), big_axis[k] + 1

    groups = [[shard("ffn1_w_in", 0), (meta_tokens, 1)], [shard("ffn1_w_out", 0)]]

    def order():
        return [] if state["token"] is None else [state["token"]]

    def start_next():
        i = state["started"]
        if i < len(groups):
            handles[i], state["token"] = _gather_start([s for s, _ in groups[i]], [ax for _, ax in groups[i]], order(),
                                                       name=f"gather{i}_start")
            state["started"] = i + 1

    def finish_begin(i, after):
        handle = handles.pop(i)
        srcs, fulls = _gather_wait(handle, list(after) + order(), name=f"gather{i}_wait")
        fulls = [_place_own(s, z, ax, dev1, name=f"gather{i}_own{a}") for a, (s, z, ax) in enumerate(zip(srcs, fulls, handle["axes"]))]
        passing, state["token"] = _gather_pass_start(fulls, handle["axes"], name=f"gather{i}_pass")
        return i, passing

    def finish_end(ctx, after):
        i, passing = ctx
        fulls = _gather_pass_wait(passing, after, name=f"gather{i}_passwait")
        start_next()
        return fulls

    start_next()
    start_next()
    state["first"] = state["token"]
    for kind, l in units[1:]:
        if kind == "mix":
            groups.append([shard("w_in", l), shard("w_out", l)] + ([(behind(conv_w), 2), (behind(pool_w), 2)] if l == 0 else []))
        else:
            groups.append([shard(f"{kind}_w_in", l), shard(f"{kind}_w_out", l)])
    pad = jnp.zeros((tp - t_valid, d), F32)
    tgt = jnp.concatenate([jnp.zeros((n_meta, d), F32), behind(loss_target[0]), pad], axis=0)
    h0_base = jnp.concatenate([jnp.zeros((n_meta, d), F32), behind(x[0]), pad], axis=0)
    ctx = finish_begin(0, [tgt, h0_base] + [s for grp in groups[2:] for s, _ in grp])
    w_in0, meta_full = finish_end(ctx, [])
    h0 = lax.dynamic_update_slice(h0_base, meta_full, (0, 0))
    n_act = _rms_fwd(h0, norms["ffn1"], 0, name="l0ffn1_rms", deps=order())
    gu, act = _ffn_in_fwd(n_act, w_in0, 0, name="l0ffn1_in")
    (w_out0,) = finish_end(finish_begin(1, [act]), [])
    h = _ffn_fwd_out(h0, act, w_out0, "l0ffn1", order())
    gathered = {units[0]: (w_in0, w_out0)}
    saved = {units[0]: (h0, n_act, gu, act)}
    mp = None
    for ui, (kind, l) in enumerate(units[1:], start=1):
        tag = f"l{l}{kind}"
        ctx = finish_begin(ui + 1, [h])
        n_act = _rms_fwd(h, norms[kind], l, name=f"{tag}_rms", deps=order())
        fulls = finish_end(ctx, [n_act])
        if kind == "mix":
            if l == 0:
                mp = dict(conv_w=fulls[2], conv_b=vec(conv_b), wa=lru_wa.astype(BF16), ba=vec(lru_ba), wx=lru_wx.astype(BF16),
                          bx=vec(lru_bx), a_param=vec(lru_a_param), pool_w=fulls[3].astype(BF16), pool_b=vec(pool_b),
                          pool_scale=vec(pool_scale))
            h, saved[(kind, l)] = _mix_block_fwd(h, n_act, l, fulls[0], fulls[1], mp, tag, order())
        else:
            gu, act = _ffn_in_fwd(n_act, fulls[0], 0, name=f"{tag}_in", deps=order())
            saved[(kind, l)] = (h, n_act, gu, act)
            h = _ffn_fwd_out(h, act, fulls[1], tag, [])
        gathered[(kind, l)] = (fulls[0], fulls[1])
    dh, dhb, g_final, loss_local = _final_loss(h, final_norm.reshape(1, d), tgt, n_meta, t_valid, name="final_loss")
    loss = lax.psum(loss_local[0, 0], ("x", "y", "c"))

    pending = []
    big_out = {}

    landed = []

    def drain(after):
        while pending:
            k, l, handle = pending.pop(0)
            landed.append((k, l) + _rs_chips_wait(handle, after, name=f"rs_{k}{l}_wait"))

    def adams(items, deps):
        for k, l, part, land in items:
            big_out[k] = _sum_adam(part, land, slots, w[k], mom[k], vel[k], l, big_out.get(k), name=f"adam_{k}{l}", deps=deps)
        return [big_out[items[-1][0]][0]] if items else []

    sib_pending = []

    def to_chips(k, l, g, land):
        part = _rs_add_sibling([g], big_axis[k], land, my_c, name=f"rs_{k}{l}_add")
        handle, token = _rs_chips_start(part, name=f"rs_{k}{l}_start")
        pending.append((k, l, handle))
        return token

    def finish_sibling(after):
        token = None
        while sib_pending:
            k, l, handle = sib_pending.pop(0)
            g, land = _rs_sibling_wait(handle, after, name=f"rs_{k}{l}_sibwait")
            token = to_chips(k, l, g, land)
        return token

    def emitter(kind, l):
        last_unit = (kind, l) == units[0]

        def emit(which, g):
            if which == "w_out":
                drain([g])
            token = finish_sibling([g])
            order_after = [] if token is None else [token]
            if which == "flush":
                return order_after
            k = which if kind == "mix" else f"{kind}_{which}"
            if last_unit and which == "w_in":
                (land,) = _rs_exchange_sibling([[g]], [big_axis[k]], name=f"rs_{k}{l}_sib")
                return order_after + [to_chips(k, l, g, land)]
            handle, token = _rs_sibling_start(g, big_axis[k], name=f"rs_{k}{l}_sibstart")
            sib_pending.append((k, l, handle))
            return order_after + [token]
        return emit

    g_norm = {k: [None] * n_layers for k in ("ffn1", "mix", "ffn2")}
    g_mix = [None] * n_layers
    for kind, l in reversed(units):
        w_i, w_o = gathered[(kind, l)]
        tag = f"l{l}{kind}"
        if kind == "mix":
            dh, dhb, g_norm[kind][l], g_mix[l] = _mix_block_bwd(dh, dhb, saved[(kind, l)], norms[kind], l, w_i, w_o, mp, tag,
                                                                 emitter(kind, l), [])
        else:
            dh, dhb, g_norm[kind][l] = _ffn_bwd(dh, dhb, saved[(kind, l)], norms[kind], l, w_i, w_o, tag, emitter(kind, l), [])
    drain([dh])
    dh0 = dh
    grad_x = dh0[n_meta:t_valid][None]

    mix_key = dict(conv_w="conv_w", conv_b="conv_b", lru_wa="wa", lru_ba="ba", lru_wx="wx", lru_bx="bx", lru_a_param="a_param",
                   pool_w="pool_w", pool_b="pool_b", pool_scale="pool_scale")
    small_local = {"meta_tokens": dh0[:n_meta], "final_norm": g_final.reshape(d)}
    for k in ("ffn1", "mix", "ffn2"):
        small_local[f"{k}_norm"] = jnp.stack([g.reshape(d) for g in g_norm[k]])
    for k, mk in mix_key.items():
        small_local[k] = jnp.stack([g_mix[l][mk] for l in range(n_layers)]).reshape(
            w[k].shape if k not in _SMALL_SHARD_AXIS else small_shape_full(w[k], _SMALL_SHARD_AXIS[k]))
    small_shapes = [small_local[k].shape for k in _SMALL]
    packed = _pack([small_local[k] for k in _SMALL])
    (land,) = _rs_exchange_sibling([[packed]], [0], name="rs_small_sib")
    part = _rs_add_sibling([packed], 0, land, my_c, name="rs_small_add")
    third = -(-len(landed) // 3)
    handle, token = _rs_chips_start(part, name="rs_small_start")
    done = adams(landed[:third], [token])
    part, land = _rs_chips_wait(handle, done, name="rs_small_wait")
    small_block = _sum_slots(part, land, slots, name="rs_sum_small")
    handle, token = _gather_start([small_block], [0], [], name="gather_small_start")
    done = adams(landed[third:2 * third], [token])
    (src,), (small_full,) = _gather_wait(handle, done, name="gather_small_wait")
    small_full = _place_own(src, small_full, 0, dev1, name="gather_small_own")
    handle, token = _gather_pass_start([small_full], [0], name="gather_small_pass")
    done = adams(landed[2 * third:], [token])
    (small_full,) = _gather_pass_wait(handle, done, name="gather_small_passwait")

    out_g, out_d, out_m, out_v = {}, {}, {}, {}
    for k, _ in _BIG:
        out_g[k], out_d[k], out_m[k], out_v[k] = big_out[k]
    small_g = dict(zip(_SMALL, _unpack(small_full, small_shapes)))
    for k, ax in _SMALL_SHARD_AXIS.items():
        small_g[k] = _my_shard(small_g[k], ax, dev)
    shapes_local = [w[k].shape for k in _SMALL]
    d_p, m_p, v_p = _adam_flat(_pack([w[k] for k in _SMALL]), _pack([small_g[k] for k in _SMALL]),
                               _pack([mom[k] for k in _SMALL]), _pack([vel[k] for k in _SMALL]), name="adam_small")
    for k, dd, mm, vv in zip(_SMALL, _unpack(d_p, shapes_local), _unpack(m_p, shapes_local), _unpack(v_p, shapes_local)):
        out_g[k], out_d[k], out_m[k], out_v[k] = small_g[k], dd, mm, vv

    return (loss, grad_x, *[out_g[k] for k in names], *[out_d[k] for k in names], *[out_m[k] for k in names],
            *[out_v[k] for k in names])


def small_shape_full(w_shard, axis):
    return w_shard.shape[:axis] + (N_DEV * w_shard.shape[axis],) + w_shard.shape[axis + 1:]
```

```python
import functools

import jax
import jax.numpy as jnp
from jax import lax
from jax.experimental import pallas as pl
from jax.experimental.pallas import tpu as pltpu

F32 = jnp.float32
BF16 = jnp.bfloat16
MESH = pl.DeviceIdType.MESH

N_DEV = 8
N_CHIP = 4
RMS_EPS = 1e-6
LRU_C = 8.0
CONV_WIDTH = 4
POOL_WINDOWS = (2, 4, 8, 16)
HIST = 16
ADAM_LR = 0.001
ADAM_B1 = 0.9
ADAM_B2 = 0.999
ADAM_EPS = 1e-08
ADAM_WD = 0.01
ADAM_STEP = 10
ROW_ALIGN = 128
V7X_VMEM_LIMIT = 56 * 1024 * 1024
BF16_ROWS = 16
K_TILE = 5632
DN_K_TILE = 2816
WGRAD_TILE = 2048
SCAN_UNROLL = 8
EPILOGUE_ROWS = 528
LANES = 128


def _tile(n, cap, mult=BF16_ROWS):
    best = None
    d = mult
    while d <= min(n, cap):
        if n % d == 0:
            best = d
        d += mult
    if best is None:
        raise ValueError(f"no tile for {n} (cap {cap}, multiple of {mult})")
    return best


def _row_mult(dtype):
    return 8 * 4 // jnp.dtype(dtype).itemsize


def _params(sem=None):
    return pltpu.CompilerParams(dimension_semantics=sem, vmem_limit_bytes=V7X_VMEM_LIMIT)


def _dot(a, b, mode):
    dims = {"nn": ((1,), (0,)), "nt": ((1,), (1,)), "tn": ((0,), (0,))}[mode]
    return lax.dot_general(a, b, (dims, ((), ())), preferred_element_type=F32)


def _sigmoid(x):
    return 1.0 / (1.0 + jnp.exp(-x))


_GELU_C = 0.7978845608028654
_GELU_K = 0.044715


def _gelu(x):
    return 0.5 * x * (1.0 + jnp.tanh(_GELU_C * (x + _GELU_K * x * x * x)))


def _gelu_and_grad(x):
    x2 = x * x
    th = jnp.tanh(_GELU_C * x * (1.0 + _GELU_K * x2))
    half = 0.5 * (1.0 + th)
    return x * half, half + 0.5 * x * (1.0 - th * th) * _GELU_C * (1.0 + 3.0 * _GELU_K * x2)


def _neg_expm1(x):
    p = 1.0 / 5040.0
    for c in (1.0 / 720.0, 1.0 / 120.0, 1.0 / 24.0, 1.0 / 6.0, 0.5, 1.0):
        p = p * x + c
    return jnp.where(x > -0.25, -(x * p), 1.0 - jnp.exp(x))


def _softplus_neg(p):
    e = jnp.exp(-jnp.abs(p))
    u = 1.0 + e
    l1p = jnp.where(u == 1.0, e, jnp.log(u) * e / (u - 1.0 + (u == 1.0).astype(F32)))
    return jnp.maximum(-p, 0.0) + l1p


def _operand_spec(arr, br, bc, ridx, cidx, layer, split):
    if split:
        ncb = arr.shape[2] // bc
        return pl.BlockSpec((None, br, bc), lambda i, j, k: (cidx(i, j, k) // ncb, ridx(i, j, k), cidx(i, j, k) % ncb))
    if layer is not None:
        return pl.BlockSpec((None, br, bc), lambda i, j, k: (layer, ridx(i, j, k), cidx(i, j, k)))
    return pl.BlockSpec((br, bc), lambda i, j, k: (ridx(i, j, k), cidx(i, j, k)))


def _matmul(a, b, *, mode, m, n, kdim, tm, tn, tk, out_dtype, name, scale=None, residual=None,
            a_layer=None, b_layer=None, a_split=False, b_split=False, deps=()):
    nk = kdim // tk
    I = lambda i, j, k: i
    J = lambda i, j, k: j
    K = lambda i, j, k: k
    if mode == "nn":
        a_spec = _operand_spec(a, tm, tk, I, K, a_layer, a_split)
        b_spec = _operand_spec(b, tk, tn, K, J, b_layer, b_split)
    elif mode == "nt":
        a_spec = _operand_spec(a, tm, tk, I, K, a_layer, a_split)
        b_spec = _operand_spec(b, tn, tk, J, K, b_layer, b_split)
    else:
        a_spec = _operand_spec(a, tk, tm, K, I, a_layer, a_split)
        b_spec = _operand_spec(b, tk, tn, K, J, b_layer, b_split)
    in_specs = [a_spec, b_spec]
    operands = [a, b]
    if residual is not None:
        in_specs.append(pl.BlockSpec((tm, tn), lambda i, j, k: (i, j)))
        operands.append(residual)
    in_specs += [pl.BlockSpec(memory_space=pl.ANY)] * len(deps)
    operands += list(deps)
    n_in = len(operands)

    def finish(acc, res_ref, o_ref):
        if scale is not None:
            acc = acc * scale
        if res_ref is not None:
            acc = acc + res_ref[...]
        o_ref[...] = acc.astype(o_ref.dtype)

    def body(*refs):
        a_ref, b_ref = refs[0], refs[1]
        res_ref = refs[2] if residual is not None else None
        o_ref = refs[n_in]
        part = _dot(a_ref[...], b_ref[...], mode)
        if nk == 1:
            finish(part, res_ref, o_ref)
        else:
            acc_ref = refs[-1]
            k = pl.program_id(2)

            @pl.when(k == 0)
            def _():
                acc_ref[...] = part

            @pl.when(k > 0)
            def _():
                acc_ref[...] += part

            @pl.when(k == nk - 1)
            def _():
                finish(acc_ref[...], res_ref, o_ref)

    return pl.pallas_call(
        body, name=name,
        out_shape=jax.ShapeDtypeStruct((m, n), out_dtype),
        grid=(m // tm, n // tn, nk),
        in_specs=in_specs,
        out_specs=pl.BlockSpec((tm, tn), lambda i, j, k: (i, j)),
        scratch_shapes=[pltpu.VMEM((tm, tn), F32)] if nk > 1 else [],
        compiler_params=_params(("parallel", "parallel", "arbitrary")),
    )(*operands)


def _ffn_in_fwd(n_act, w_in, layer, *, name, deps=()):
    tp, d = n_act.shape
    f = w_in.shape[2] // 2
    tm = _tile(tp, 2112)
    tn = _tile(f, 512, LANES)
    nj = f // tn

    ch = _tile(tm, EPILOGUE_ROWS)

    def body(n_ref, wg_ref, wu_ref, *rest):
        da_ref, a_ref = rest[len(deps):]
        for r in range(tm // ch):
            rows = pl.ds(r * ch, ch)
            x = n_ref[rows, :]
            g = _dot(x, wg_ref[...], "nn")
            u = _dot(x, wu_ref[...], "nn")
            s = _sigmoid(g)
            sg = g * s
            da_ref[0, rows, :] = (u * s * (1.0 + g * (1.0 - s))).astype(BF16)
            da_ref[1, rows, :] = sg.astype(BF16)
            a_ref[rows, :] = (sg * u).astype(BF16)

    return pl.pallas_call(
        body, name=name,
        out_shape=(jax.ShapeDtypeStruct((2, tp, f), BF16), jax.ShapeDtypeStruct((tp, f), BF16)),
        grid=(tp // tm, nj),
        in_specs=[pl.BlockSpec((tm, d), lambda i, j: (i, 0)),
                  pl.BlockSpec((None, d, tn), lambda i, j: (layer, 0, j)),
                  pl.BlockSpec((None, d, tn), lambda i, j: (layer, 0, j + nj))] + [pl.BlockSpec(memory_space=pl.ANY)] * len(deps),
        out_specs=(pl.BlockSpec((2, tm, tn), lambda i, j: (0, i, j)), pl.BlockSpec((tm, tn), lambda i, j: (i, j))),
        compiler_params=_params(("parallel", "arbitrary")),
    )(n_act, w_in, w_in, *deps)


def _ffn_bwd_act(dhb, w_out, layer, gu, *, name, deps=()):
    tp, d = dhb.shape
    f = w_out.shape[1]
    tm = _tile(tp, 2112)
    tn = _tile(f, 512, LANES)

    ch = _tile(tm, EPILOGUE_ROWS)

    def body(dh_ref, w_ref, gu_ref, *rest):
        dz_ref = rest[len(deps)]
        for r in range(tm // ch):
            rows = pl.ds(r * ch, ch)
            da = 0.5 * _dot(dh_ref[rows, :], w_ref[...], "nt")
            dz_ref[0, rows, :] = (da * gu_ref[0, rows, :].astype(F32)).astype(BF16)
            dz_ref[1, rows, :] = (da * gu_ref[1, rows, :].astype(F32)).astype(BF16)

    return pl.pallas_call(
        body, name=name,
        out_shape=jax.ShapeDtypeStruct((2, tp, f), BF16),
        grid=(tp // tm, f // tn),
        in_specs=[pl.BlockSpec((tm, d), lambda i, j: (i, 0)),
                  pl.BlockSpec((None, tn, d), lambda i, j: (layer, j, 0)),
                  pl.BlockSpec((2, tm, tn), lambda i, j: (0, i, j))] + [pl.BlockSpec(memory_space=pl.ANY)] * len(deps),
        out_specs=pl.BlockSpec((2, tm, tn), lambda i, j: (0, i, j)),
        compiler_params=_params(("parallel", "arbitrary")),
    )(dhb, w_out, gu, *deps)


def _rms_fwd(h, g, layer, *, name, deps=()):
    tp, d = h.shape
    tr = _tile(tp, 528)

    def body(h_ref, g_ref, *rest):
        n_ref = rest[len(deps)]
        x = h_ref[...]
        r = lax.rsqrt(jnp.mean(x * x, axis=-1, keepdims=True) + RMS_EPS)
        n_ref[...] = (x * r * g_ref[...]).astype(BF16)

    return pl.pallas_call(
        body, name=name, out_shape=jax.ShapeDtypeStruct((tp, d), BF16), grid=(tp // tr,),
        in_specs=[pl.BlockSpec((tr, d), lambda i: (i, 0)), pl.BlockSpec((None, 1, d), lambda i: (layer, 0, 0))]
        + [pl.BlockSpec(memory_space=pl.ANY)] * len(deps),
        out_specs=pl.BlockSpec((tr, d), lambda i: (i, 0)),
        compiler_params=_params(("parallel",)),
    )(h, g, *deps)


def _rms_bwd(h, g, layer, dn, dres, *, name, deps=()):
    tp, d = h.shape
    tr = _tile(tp, 528)

    def body(h_ref, g_ref, dn_ref, dres_ref, *rest):
        dh_ref, dhb_ref, dg_ref = rest[len(deps):]
        x = h_ref[...]
        r = lax.rsqrt(jnp.mean(x * x, axis=-1, keepdims=True) + RMS_EPS)
        xhat = x * r
        dn_v = dn_ref[...]
        dxhat = dn_v * g_ref[...]
        dh = dres_ref[...] + r * (dxhat - xhat * jnp.mean(dxhat * xhat, axis=-1, keepdims=True))
        dh_ref[...] = dh
        dhb_ref[...] = dh.astype(BF16)
        part = jnp.sum(dn_v * xhat, axis=0, keepdims=True)

        @pl.when(pl.program_id(0) == 0)
        def _():
            dg_ref[...] = part

        @pl.when(pl.program_id(0) > 0)
        def _():
            dg_ref[...] += part

    return pl.pallas_call(
        body, name=name,
        out_shape=(jax.ShapeDtypeStruct((tp, d), F32), jax.ShapeDtypeStruct((tp, d), BF16), jax.ShapeDtypeStruct((1, d), F32)),
        grid=(tp // tr,),
        in_specs=[pl.BlockSpec((tr, d), lambda i: (i, 0)), pl.BlockSpec((None, 1, d), lambda i: (layer, 0, 0)),
                  pl.BlockSpec((tr, d), lambda i: (i, 0)), pl.BlockSpec((tr, d), lambda i: (i, 0))]
        + [pl.BlockSpec(memory_space=pl.ANY)] * len(deps),
        out_specs=(pl.BlockSpec((tr, d), lambda i: (i, 0)), pl.BlockSpec((tr, d), lambda i: (i, 0)),
                   pl.BlockSpec((1, d), lambda i: (0, 0))),
        compiler_params=_params(("arbitrary",)),
    )(h, g, dn, dres, *deps)


def _final_loss(h, g, tgt, n_meta, t_valid, *, name):
    tp, d = h.shape
    tr = _tile(tp, 528)

    def body(h_ref, g_ref, t_ref, dh_ref, dhb_ref, dg_ref, loss_ref):
        i = pl.program_id(0)
        x = h_ref[...]
        r = lax.rsqrt(jnp.mean(x * x, axis=-1, keepdims=True) + RMS_EPS)
        xhat = x * r
        gv = g_ref[...]
        row = i * tr + lax.broadcasted_iota(jnp.int32, (tr, 1), 0)
        valid = jnp.logical_and(row >= n_meta, row < t_valid)
        err = jnp.where(valid, xhat * gv - t_ref[...], 0.0)
        dy = err * (1.0 / d)
        dxhat = dy * gv
        dh = r * (dxhat - xhat * jnp.mean(dxhat * xhat, axis=-1, keepdims=True))
        dh_ref[...] = dh
        dhb_ref[...] = dh.astype(BF16)
        dg_part = jnp.sum(dy * xhat, axis=0, keepdims=True)
        loss_part = jnp.sum(jnp.sum(err * err, axis=1, keepdims=True), axis=0, keepdims=True) * (0.5 / d)

        @pl.when(i == 0)
        def _():
            dg_ref[...] = dg_part
            loss_ref[...] = loss_part

        @pl.when(i > 0)
        def _():
            dg_ref[...] += dg_part
            loss_ref[...] += loss_part

    return pl.pallas_call(
        body, name=name,
        out_shape=(jax.ShapeDtypeStruct((tp, d), F32), jax.ShapeDtypeStruct((tp, d), BF16),
                   jax.ShapeDtypeStruct((1, d), F32), jax.ShapeDtypeStruct((1, 1), F32)),
        grid=(tp // tr,),
        in_specs=[pl.BlockSpec((tr, d), lambda i: (i, 0)), pl.BlockSpec((1, d), lambda i: (0, 0)),
                  pl.BlockSpec((tr, d), lambda i: (i, 0))],
        out_specs=(pl.BlockSpec((tr, d), lambda i: (i, 0)), pl.BlockSpec((tr, d), lambda i: (i, 0)),
                   pl.BlockSpec((1, d), lambda i: (0, 0)), pl.BlockSpec((1, 1), lambda i: (0, 0))),
        compiler_params=_params(("arbitrary",)),
    )(h, g, tgt)


def _mix_param_specs(mp, layer, imap):
    def vec(a):
        return pl.BlockSpec((None,) + a.shape[1:], lambda *g: (layer,) + (0,) * (a.ndim - 1))
    return [vec(mp[k]) for k in ("conv_w", "conv_b", "wa", "ba", "wx", "bx", "a_param", "pool_w", "pool_b", "pool_scale")]


def _mix_param_list(mp):
    return [mp[k] for k in ("conv_w", "conv_b", "wa", "ba", "wx", "bx", "a_param", "pool_w", "pool_b", "pool_scale")]


def _lru_gates(xc, wa_h, ba_h, wx_h, bx_h, sp_h):
    xb = xc.astype(BF16)
    ra = _sigmoid(_dot(xb, wa_h, "nn") + ba_h)
    ii = _sigmoid(_dot(xb, wx_h, "nn") + bx_h)
    la = -LRU_C * ra * sp_h
    return ra, ii, la, jnp.exp(la)


def _shifted(x, k):
    return x if k == 0 else pltpu.roll(x, k % x.shape[0], axis=0)


def _conv_taps(xbuf, sl, tc):
    x = xbuf[:, sl]
    return [_shifted(x, CONV_WIDTH - 1 - k)[HIST:HIST + tc] for k in range(CONV_WIDTH)]


def _conv_fwd(taps, cw_ref, cb_ref, sl):
    xc = cb_ref[:, sl]
    for k in range(CONV_WIDTH):
        xc = xc + cw_ref[k:k + 1, sl] * taps[k]
    return xc


def _window_sum(x, win, direction):
    s = x
    step = 1
    while step < win:
        s = s + _shifted(s, direction * step)
        step *= 2
    return s


def _pool_delta(pbuf, cols, win, t0, tc):
    x = pbuf[:, cols]
    u = x[HIST:HIST + tc]
    s = _window_sum(x, win, 1)[HIST:HIST + tc]
    t = t0 + lax.broadcasted_iota(jnp.int32, (tc, 1), 0)
    inv_cnt = 1.0 / jnp.minimum(t + 1, win).astype(F32)
    return s * inv_cnt - u, inv_cnt


def _mix_fwd(z, mp, layer, *, name):
    tp = z.shape[0]
    dl = z.shape[1] // 3
    n_heads, hd = mp["wa"].shape[1], mp["wa"].shape[2]
    n_groups, gd = mp["pool_w"].shape[1], mp["pool_w"].shape[2]
    tc = _tile(tp, 384)

    def body(z_ref, cw_ref, cb_ref, wa_ref, ba_ref, wx_ref, bx_ref, ap_ref, pw_ref, pb_ref, ps_ref,
             m_ref, hs_ref, aux_ref, xbuf, pbuf, a_s, b_s, hcar):
        i = pl.program_id(0)

        @pl.when(i == 0)
        def _():
            xbuf[pl.ds(0, HIST), :] = jnp.zeros((HIST, dl), F32)
            pbuf[pl.ds(0, HIST), :] = jnp.zeros((HIST, dl), F32)
            hcar[...] = jnp.zeros((1, dl), F32)

        @pl.when(i > 0)
        def _():
            xbuf[pl.ds(0, HIST), :] = xbuf[pl.ds(tc, HIST), :]
            pbuf[pl.ds(0, HIST), :] = pbuf[pl.ds(tc, HIST), :]

        xbuf[pl.ds(HIST, tc), :] = z_ref[:, pl.ds(0, dl)]
        pbuf[pl.ds(HIST, tc), :] = z_ref[:, pl.ds(2 * dl, dl)]
        sp = _softplus_neg(ap_ref[...])
        for h in range(n_heads):
            sl = pl.ds(h * hd, hd)
            xc = _conv_fwd(_conv_taps(xbuf, sl, tc), cw_ref, cb_ref, sl)
            ra, ii, la, a = _lru_gates(xc, wa_ref[h], ba_ref[:, sl], wx_ref[h], bx_ref[:, sl], sp[:, h * hd:(h + 1) * hd])
            e = _neg_expm1(2.0 * la)
            a_s[:, sl] = a
            b_s[:, sl] = jnp.sqrt(e) * ii * xc
            for slot, val in ((_AUX_R, ra), (_AUX_I, ii), (_AUX_X, xc), (_AUX_E, e)):
                aux_ref[slot, :, sl] = val

        def step(t, hprev):
            hnew = a_s[pl.ds(t, 1), :] * hprev + b_s[pl.ds(t, 1), :]
            hs_ref[pl.ds(t, 1), :] = hnew
            return hnew

        hcar[...] = lax.fori_loop(0, tc, step, hcar[...], unroll=SCAN_UNROLL)
        for h in range(n_heads):
            sl = pl.ds(h * hd, hd)
            m_ref[:, sl] = (hs_ref[:, sl] * _gelu(z_ref[:, pl.ds(dl + h * hd, hd)])).astype(BF16)
        for g in range(n_groups):
            cols = pl.ds(g * gd, gd)
            dlt, _ = _pool_delta(pbuf, cols, POOL_WINDOWS[g], i * tc, tc)
            q = _dot(dlt.astype(BF16), pw_ref[g], "nn") + pb_ref[:, cols]
            m_ref[:, pl.ds(dl + g * gd, gd)] = (q * ps_ref[:, cols]).astype(BF16)

    return pl.pallas_call(
        body, name=name,
        out_shape=(jax.ShapeDtypeStruct((tp, 2 * dl), BF16), jax.ShapeDtypeStruct((tp, dl), F32),
                   jax.ShapeDtypeStruct((_AUX_N, tp, dl), F32)),
        grid=(tp // tc,),
        in_specs=[pl.BlockSpec((tc, 3 * dl), lambda i: (i, 0))] + _mix_param_specs(mp, layer, None),
        out_specs=(pl.BlockSpec((tc, 2 * dl), lambda i: (i, 0)), pl.BlockSpec((tc, dl), lambda i: (i, 0)),
                   pl.BlockSpec((_AUX_N, tc, dl), lambda i: (0, i, 0))),
        scratch_shapes=[pltpu.VMEM((HIST + tc, dl), F32), pltpu.VMEM((HIST + tc, dl), F32),
                        pltpu.VMEM((tc, dl), F32), pltpu.VMEM((tc, dl), F32), pltpu.VMEM((1, dl), F32)],
        compiler_params=_params(("arbitrary",)),
    )(z, *_mix_param_list(mp))


_MIX_GRADS = ("conv_w", "conv_b", "wa", "ba", "wx", "bx", "a_param", "pool_w", "pool_b", "pool_scale")
_AUX_R, _AUX_I, _AUX_X, _AUX_E, _AUX_N = 0, 1, 2, 3, 4


def _mix_bwd(z, hs, aux, dm, mp, layer, *, name):
    tp = z.shape[0]
    dl = z.shape[1] // 3
    n_heads, hd = mp["wa"].shape[1], mp["wa"].shape[2]
    n_groups, gd = mp["pool_w"].shape[1], mp["pool_w"].shape[2]
    tc = _tile(tp, 352)
    nc = tp // tc
    per = tc // HIST

    def body(z_ref, zp_ref, hs_ref, hsp_ref, aux_ref, dm_ref, cw_ref, cb_ref, wa_ref, ba_ref, wx_ref, bx_ref, ap_ref, pw_ref, pb_ref,
             ps_ref, dz_ref, dcw_ref, dcb_ref, dwa_ref, dba_ref, dwx_ref, dbx_ref, dap_ref, dpw_ref, dpb_ref, dps_ref,
             xbuf, pbuf, hbuf, dxbuf, ddbuf, a_s, lam_s, ccar):
        i = pl.program_id(0)
        ci = nc - 1 - i

        @pl.when(i == 0)
        def _():
            dxbuf[pl.ds(tc, HIST), :] = jnp.zeros((HIST, dl), F32)
            ddbuf[pl.ds(tc, HIST), :] = jnp.zeros((HIST, dl), F32)
            ccar[...] = jnp.zeros((1, dl), F32)
            for ref in (dcw_ref, dcb_ref, dwa_ref, dba_ref, dwx_ref, dbx_ref, dap_ref, dpw_ref, dpb_ref, dps_ref):
                ref[...] = jnp.zeros(ref.shape, F32)

        @pl.when(ci == 0)
        def _():
            xbuf[pl.ds(0, HIST), :] = jnp.zeros((HIST, dl), F32)
            pbuf[pl.ds(0, HIST), :] = jnp.zeros((HIST, dl), F32)
            hbuf[pl.ds(0, HIST), :] = jnp.zeros((HIST, dl), F32)

        @pl.when(ci > 0)
        def _():
            xbuf[pl.ds(0, HIST), :] = zp_ref[:, pl.ds(0, dl)]
            pbuf[pl.ds(0, HIST), :] = zp_ref[:, pl.ds(2 * dl, dl)]
            hbuf[pl.ds(0, HIST), :] = hsp_ref[...]

        xbuf[pl.ds(HIST, tc), :] = z_ref[:, pl.ds(0, dl)]
        pbuf[pl.ds(HIST, tc), :] = z_ref[:, pl.ds(2 * dl, dl)]
        hbuf[pl.ds(HIST, tc), :] = hs_ref[...]
        sp = _softplus_neg(ap_ref[...])

        for h in range(n_heads):
            sl = pl.ds(h * hd, hd)
            a_s[:, sl] = jnp.exp(-LRU_C * aux_ref[_AUX_R, :, sl] * sp[:, h * hd:(h + 1) * hd])
            gel, gel_grad = _gelu_and_grad(z_ref[:, pl.ds(dl + h * hd, hd)])
            dya = dm_ref[:, sl]
            lam_s[:, sl] = dya * gel
            dz_ref[:, pl.ds(dl + h * hd, hd)] = (dya * hs_ref[:, sl] * gel_grad).astype(BF16)

        def step(r, carry):
            t = tc - 1 - r
            lam = lam_s[pl.ds(t, 1), :] + carry
            lam_s[pl.ds(t, 1), :] = lam
            return a_s[pl.ds(t, 1), :] * lam

        ccar[...] = lax.fori_loop(0, tc, step, ccar[...], unroll=SCAN_UNROLL)

        for h in range(n_heads):
            sl = pl.ds(h * hd, hd)
            sp_h = sp[:, h * hd:(h + 1) * hd]
            lam = lam_s[:, sl]
            a = a_s[:, sl]
            ra = aux_ref[_AUX_R, :, sl]
            ii = aux_ref[_AUX_I, :, sl]
            xc = aux_ref[_AUX_X, :, sl]
            e = aux_ref[_AUX_E, :, sl]
            inv_mult = lax.rsqrt(e)
            mult = e * inv_mult
            hprev = _shifted(hbuf[:, sl], 1)[HIST:HIST + tc]
            lam_i = lam * ii
            lam_m = lam * mult
            dla = lam * hprev * a - lam_i * xc * (a * a) * inv_mult
            dla_r = dla * (-LRU_C) * ra
            dap_ref[:, sl] += jnp.sum(dla_r, axis=0, keepdims=True)
            dpa = dla_r * sp_h * (1.0 - ra)
            dpx = lam_m * xc * ii * (1.0 - ii)
            dba_ref[:, sl] += jnp.sum(dpa, axis=0, keepdims=True)
            dbx_ref[:, sl] += jnp.sum(dpx, axis=0, keepdims=True)
            xb = xc.astype(BF16)
            dpa_b = dpa.astype(BF16)
            dpx_b = dpx.astype(BF16)
            dwa_ref[h] += _dot(xb, dpa_b, "tn")
            dwx_ref[h] += _dot(xb, dpx_b, "tn")
            dxc = lam_m * ii + _dot(dpa_b, wa_ref[h], "nt") + _dot(dpx_b, wx_ref[h], "nt")
            dxbuf[pl.ds(0, tc), sl] = dxc
            dcb_ref[:, sl] += jnp.sum(dxc, axis=0, keepdims=True)
            taps = _conv_taps(xbuf, sl, tc)
            dx_all = dxbuf[:, sl]
            dzx = jnp.zeros((tc, hd), F32)
            for k in range(CONV_WIDTH):
                dcw_ref[k:k + 1, sl] += jnp.sum(dxc * taps[k], axis=0, keepdims=True)
                dzx = dzx + cw_ref[k:k + 1, sl] * _shifted(dx_all, k - (CONV_WIDTH - 1))[0:tc]
            dz_ref[:, sl] = dzx.astype(BF16)
            dxbuf[pl.ds(tc, HIST), sl] = dxbuf[pl.ds(0, HIST), sl]

        for g in range(n_groups):
            cols = pl.ds(g * gd, gd)
            win = POOL_WINDOWS[g]
            dlt, inv_cnt = _pool_delta(pbuf, cols, win, ci * tc, tc)
            db = dlt.astype(BF16)
            q = _dot(db, pw_ref[g], "nn") + pb_ref[:, cols]
            dyb = dm_ref[:, pl.ds(dl + g * gd, gd)]
            dps_ref[:, cols] += jnp.sum(dyb * q, axis=0, keepdims=True)
            dq = dyb * ps_ref[:, cols]
            dpb_ref[:, cols] += jnp.sum(dq, axis=0, keepdims=True)
            dqb = dq.astype(BF16)
            dpw_ref[g] += _dot(db, dqb, "tn")
            dd = _dot(dqb, pw_ref[g], "nt")
            ddbuf[pl.ds(0, tc), cols] = dd * inv_cnt
            dzp = _window_sum(ddbuf[:, cols], win, -1)[0:tc] - dd
            dz_ref[:, pl.ds(2 * dl + g * gd, gd)] = dzp.astype(BF16)
            ddbuf[pl.ds(tc, HIST), cols] = ddbuf[pl.ds(0, HIST), cols]

        @pl.when(i == nc - 1)
        def _():
            dap_ref[...] = dap_ref[...] * (-_sigmoid(-ap_ref[...]))

    rev = lambda i: (nc - 1 - i, 0)
    prev = lambda i: (jnp.maximum((nc - 1 - i) * per - 1, 0), 0)
    const = lambda a: pl.BlockSpec(a.shape[1:], lambda i: (0,) * (a.ndim - 1))
    plist = _mix_param_list(mp)
    grad_shapes = [jax.ShapeDtypeStruct(a.shape[1:], F32) for a in plist]
    buf = lambda rows: pltpu.VMEM((rows, dl), F32)
    outs = pl.pallas_call(
        body, name=name,
        out_shape=[jax.ShapeDtypeStruct((tp, 3 * dl), BF16)] + grad_shapes,
        grid=(nc,),
        in_specs=[pl.BlockSpec((tc, 3 * dl), rev), pl.BlockSpec((HIST, 3 * dl), prev),
                  pl.BlockSpec((tc, dl), rev), pl.BlockSpec((HIST, dl), prev),
                  pl.BlockSpec((_AUX_N, tc, dl), lambda i: (0, nc - 1 - i, 0)),
                  pl.BlockSpec((tc, 2 * dl), rev)] + _mix_param_specs(mp, layer, None),
        out_specs=[pl.BlockSpec((tc, 3 * dl), rev)] + [const(a) for a in plist],
        scratch_shapes=[buf(HIST + tc), buf(HIST + tc), buf(HIST + tc), buf(tc + HIST), buf(tc + HIST),
                        buf(tc), buf(tc), buf(1)],
        compiler_params=_params(("arbitrary",)),
    )(z, z, hs, hs, aux, dm, *plist)
    return outs[0], dict(zip(_MIX_GRADS, outs[1:]))


def _ffn_fwd_out(h, act, w_out, tag, deps):
    tp, d = h.shape
    f = w_out.shape[1]
    return _matmul(act, w_out, mode="nn", m=tp, n=d, kdim=f, tm=_tile(tp, 1056), tn=_tile(d, 512, LANES),
                   tk=_tile(f, K_TILE, LANES), out_dtype=F32, scale=0.5, residual=h, b_layer=0, name=f"{tag}_out", deps=deps)


def _ffn_bwd(dh, dhb, saved, norm, layer, w_in, w_out, tag, emit, deps):
    h, n_act, gu, act = saved
    tp, d = h.shape
    f = w_out.shape[1]
    dz = _ffn_bwd_act(dhb, w_out, 0, gu, name=f"{tag}_bact", deps=deps)
    tok = emit("w_out", _matmul(act, dhb, mode="tn", m=f, n=d, kdim=tp, tm=_tile(f, 512, LANES), tn=_tile(d, WGRAD_TILE, LANES), tk=tp,
                                out_dtype=BF16, scale=0.5, name=f"{tag}_dwout", deps=[dz]))
    tok = emit("w_in", _matmul(n_act, dz, mode="tn", m=d, n=2 * f, kdim=tp, tm=_tile(d, WGRAD_TILE, LANES), tn=_tile(f, 512, LANES),
                               tk=tp, out_dtype=BF16, b_split=True, name=f"{tag}_dwin", deps=tok))
    dn = _matmul(dz, w_in, mode="nt", m=tp, n=d, kdim=2 * f, tm=_tile(tp, 1056), tn=_tile(d, 1024, LANES), tk=_tile(f, DN_K_TILE, LANES),
                 out_dtype=F32, a_split=True, b_layer=0, name=f"{tag}_dn", deps=tok)
    return _rms_bwd(h, norm, layer, dn, dh, name=f"{tag}_brms", deps=emit("flush", dn))


def _mix_block_fwd(h, n_act, layer, w_in, w_out, mp, tag, deps):
    tp, d = h.shape
    d_in = w_in.shape[2]
    z = _matmul(n_act, w_in, mode="nn", m=tp, n=d_in, kdim=d, tm=_tile(tp, 2112), tn=_tile(d_in, 512, LANES), tk=d,
                out_dtype=F32, b_layer=0, name=f"{tag}_in", deps=deps)
    m_act, hs, aux = _mix_fwd(z, mp, layer, name=f"{tag}_mix")
    h_out = _matmul(m_act, w_out, mode="nn", m=tp, n=d, kdim=d, tm=_tile(tp, 2112), tn=_tile(d, 512, LANES), tk=d,
                    out_dtype=F32, residual=h, b_layer=0, name=f"{tag}_out")
    return h_out, (h, n_act, z, hs, aux, m_act)


def _mix_block_bwd(dh, dhb, saved, norm, layer, w_in, w_out, mp, tag, emit, deps):
    h, n_act, z, hs, aux, m_act = saved
    tp, d = h.shape
    d_in = w_in.shape[2]
    dm = _matmul(dhb, w_out, mode="nt", m=tp, n=d, kdim=d, tm=_tile(tp, 2112), tn=_tile(d, 512, LANES), tk=d,
                 out_dtype=F32, b_layer=0, name=f"{tag}_dm", deps=deps)
    tok = emit("w_out", _matmul(m_act, dhb, mode="tn", m=d, n=d, kdim=tp, tm=_tile(d, 512, LANES), tn=_tile(d, WGRAD_TILE, LANES), tk=tp,
                                out_dtype=BF16, name=f"{tag}_dwout", deps=[dm]))
    dz, g_mix = _mix_bwd(z, hs, aux, dm, mp, layer, name=f"{tag}_bmix")
    tok = emit("w_in", _matmul(n_act, dz, mode="tn", m=d, n=d_in, kdim=tp, tm=_tile(d, WGRAD_TILE, LANES), tn=_tile(d_in, 512, LANES),
                               tk=tp, out_dtype=BF16, name=f"{tag}_dwin", deps=tok))
    dn = _matmul(dz, w_in, mode="nt", m=tp, n=d, kdim=d_in, tm=_tile(tp, 1056), tn=_tile(d, 1024, LANES), tk=d_in,
                 out_dtype=F32, b_layer=0, name=f"{tag}_dn", deps=tok)
    dh_in, dhb_in, g_norm = _rms_bwd(h, norm, layer, dn, dh, name=f"{tag}_brms", deps=emit("flush", dn))
    return dh_in, dhb_in, g_norm, g_mix


def _mesh_pos():
    x, y, c = lax.axis_index("x"), lax.axis_index("y"), lax.axis_index("c")
    chips = [(1 - x, y), (x, 1 - y), (1 - x, 1 - y)]
    return x, y, c, chips


def _block(ref, axis, j, size):
    idx = [slice(None)] * len(ref.shape)
    idx[axis] = pl.ds(j * size, size)
    return ref.at[tuple(idx)]


_HBM = pl.BlockSpec(memory_space=pltpu.HBM)
_SEM = pl.BlockSpec(memory_space=pltpu.SEMAPHORE)
_ANY = pl.BlockSpec(memory_space=pl.ANY)
_EFFECT = pltpu.SideEffectType.DATAFLOW_SIDE_EFFECTING
_N_OUT = 4


def _in_hbm(a):
    return pltpu.with_memory_space_constraint(a, pltpu.HBM)


def _gather_start(shards, axes, deps, *, name):
    n = len(shards)
    sizes = [s.shape[ax] for s, ax in zip(shards, axes)]
    full_shapes = [s.shape[:ax] + (N_DEV * s.shape[ax],) + s.shape[ax + 1:] for s, ax in zip(shards, axes)]

    def body(*refs):
        srcs, lands = refs[:n], refs[n:2 * n]
        send_sems, recv_sems = refs[2 * n + len(deps)], refs[2 * n + len(deps) + 1]
        token = refs[-1]
        x, y, c, chips = _mesh_pos()
        targets = [(x, y, 1 - c)] + [(*chip, c) for chip in chips]
        for a in range(n):
            dst = _block(lands[a], axes[a], 4 * x + 2 * y + c, sizes[a])
            for k, to in enumerate(targets):
                pltpu.make_async_remote_copy(src_ref=srcs[a], dst_ref=dst, send_sem=send_sems.at[_N_OUT * a + k],
                                             recv_sem=recv_sems.at[_N_OUT * a + k],
                                             device_id=to, device_id_type=MESH).start()
        token[...] = jnp.zeros(token.shape, F32)

    lands0 = [lax.empty(fs, s.dtype) for fs, s in zip(full_shapes, shards)]
    outs = pl.pallas_call(
        body, name=name,
        out_shape=(pltpu.SemaphoreType.DMA((n * _N_OUT,)), pltpu.SemaphoreType.DMA((n * _N_OUT,)),
                   *[pltpu.HBM(s.shape, s.dtype) for s in shards], *[pltpu.HBM(fs, s.dtype) for fs, s in zip(full_shapes, shards)],
                   jax.ShapeDtypeStruct((8, LANES), F32)),
        in_specs=[_HBM] * (2 * n) + [_ANY] * len(deps),
        out_specs=(_SEM, _SEM, *[_HBM] * (2 * n), pl.BlockSpec(memory_space=pltpu.VMEM)),
        input_output_aliases={a: 2 + a for a in range(2 * n)},
        compiler_params=pltpu.CompilerParams(has_side_effects=_EFFECT),
    )(*[_in_hbm(s) for s in shards], *[_in_hbm(z) for z in lands0], *deps)
    handle = dict(send_sems=outs[0], recv_sems=outs[1], srcs=list(outs[2:2 + n]), lands=list(outs[2 + n:2 + 2 * n]), axes=list(axes))
    return handle, outs[-1]


def _gather_wait(handle, after, *, name):
    srcs, lands, axes = handle["srcs"], handle["lands"], handle["axes"]
    n = len(srcs)
    sizes = [s.shape[ax] for s, ax in zip(srcs, axes)]

    def body(*refs):
        src_refs, land_refs = refs[:n], refs[n:2 * n]
        send_sems, recv_sems = refs[2 * n], refs[2 * n + 1]
        x, y, c, chips = _mesh_pos()
        peers = [(x, y, 1 - c)] + [(*chip, c) for chip in chips]
        for a in range(n):
            for k, dev in enumerate(peers):
                cp = pltpu.make_async_remote_copy(
                    src_ref=src_refs[a], dst_ref=_block(land_refs[a], axes[a], 4 * dev[0] + 2 * dev[1] + dev[2], sizes[a]),
                    send_sem=send_sems.at[_N_OUT * a + k], recv_sem=recv_sems.at[_N_OUT * a + k], device_id=dev,
                    device_id_type=MESH)
                cp.wait_send()
                cp.wait_recv()

    outs = pl.pallas_call(
        body, name=name,
        out_shape=(*[pltpu.HBM(s.shape, s.dtype) for s in srcs], *[pltpu.HBM(z.shape, z.dtype) for z in lands]),
        in_specs=[_HBM] * (2 * n) + [_SEM, _SEM] + [_ANY] * len(after),
        out_specs=tuple([_HBM] * (2 * n)),
        input_output_aliases={a: a for a in range(2 * n)},
        compiler_params=pltpu.CompilerParams(has_side_effects=_EFFECT),
    )(*srcs, *lands, handle["send_sems"], handle["recv_sems"], *after)
    return list(outs[:n]), list(outs[n:])


def _place_own(shard, full, axis, dev, *, name):
    nd = shard.ndim
    rows, cols = shard.shape[-2:]
    tr = _tile(rows, 512, _row_mult(shard.dtype)) if rows % _row_mult(shard.dtype) == 0 else rows
    nrb = rows // tr
    block = shard.shape[:-2] + (tr, cols)

    def in_map(r, dev_ref):
        return (0,) * (nd - 2) + (r, 0)

    def out_map(r, dev_ref):
        idx = [0] * nd
        idx[nd - 2] = r
        idx[axis] = dev_ref[0] * (nrb if axis == nd - 2 else 1) + idx[axis]
        return tuple(idx)

    def body(dev_ref, src_ref, full_ref, out_ref):
        out_ref[...] = src_ref[...]

    grid_spec = pltpu.PrefetchScalarGridSpec(
        num_scalar_prefetch=1, grid=(nrb,),
        in_specs=[pl.BlockSpec(block, in_map), _ANY], out_specs=pl.BlockSpec(block, out_map))
    return pl.pallas_call(body, name=name, grid_spec=grid_spec, out_shape=jax.ShapeDtypeStruct(full.shape, full.dtype),
                          input_output_aliases={2: 0}, compiler_params=_params(("arbitrary",)))(dev, shard, full)


def _gather_pass_start(fulls, axes, *, name):
    n = len(fulls)
    sizes = [z.shape[ax] // N_DEV for z, ax in zip(fulls, axes)]

    def body(*refs):
        srcs = refs[:n]
        send_sems, recv_sems = refs[n], refs[n + 1]
        token = refs[-1]
        x, y, c, chips = _mesh_pos()
        for a in range(n):
            for j, chip in enumerate(chips):
                blk = _block(srcs[a], axes[a], 4 * chip[0] + 2 * chip[1] + c, sizes[a])
                pltpu.make_async_remote_copy(src_ref=blk, dst_ref=blk, send_sem=send_sems.at[3 * a + j], recv_sem=recv_sems.at[3 * a + j],
                                             device_id=(x, y, 1 - c), device_id_type=MESH).start()
        token[...] = jnp.zeros(token.shape, F32)

    outs = pl.pallas_call(
        body, name=name,
        out_shape=(pltpu.SemaphoreType.DMA((3 * n,)), pltpu.SemaphoreType.DMA((3 * n,)), *[pltpu.HBM(z.shape, z.dtype) for z in fulls],
                   jax.ShapeDtypeStruct((8, LANES), F32)),
        in_specs=[_HBM] * n, out_specs=(_SEM, _SEM, *[_HBM] * n, pl.BlockSpec(memory_space=pltpu.VMEM)),
        input_output_aliases={a: 2 + a for a in range(n)},
        compiler_params=pltpu.CompilerParams(has_side_effects=_EFFECT),
    )(*[_in_hbm(z) for z in fulls])
    return dict(send_sems=outs[0], recv_sems=outs[1], fulls=list(outs[2:2 + n]), axes=list(axes)), outs[-1]


def _gather_pass_wait(handle, after, *, name):
    fulls, axes = handle["fulls"], handle["axes"]
    n = len(fulls)
    sizes = [z.shape[ax] // N_DEV for z, ax in zip(fulls, axes)]

    def body(*refs):
        bufs = refs[:n]
        send_sems, recv_sems = refs[n], refs[n + 1]
        x, y, c, chips = _mesh_pos()
        for a in range(n):
            for j, chip in enumerate(chips):
                cp = pltpu.make_async_remote_copy(
                    src_ref=_block(bufs[a], axes[a], 4 * chip[0] + 2 * chip[1] + c, sizes[a]),
                    dst_ref=_block(bufs[a], axes[a], 4 * chip[0] + 2 * chip[1] + (1 - c), sizes[a]),
                    send_sem=send_sems.at[3 * a + j], recv_sem=recv_sems.at[3 * a + j], device_id=(x, y, 1 - c),
                    device_id_type=MESH)
                cp.wait_send()
                cp.wait_recv()

    outs = pl.pallas_call(
        body, name=name,
        out_shape=tuple(pltpu.HBM(z.shape, z.dtype) for z in fulls),
        in_specs=[_HBM] * n + [_SEM, _SEM] + [_ANY] * len(after), out_specs=tuple([_HBM] * n),
        input_output_aliases={a: a for a in range(n)},
        compiler_params=pltpu.CompilerParams(has_side_effects=_EFFECT),
    )(*fulls, handle["send_sems"], handle["recv_sems"], *after)
    return list(outs)


def _rs_exchange_sibling(tensors, axes, *, name):
    n = len(tensors)
    n_layers = [len(t) for t in tensors]
    sizes = [t[0].shape[ax] // N_DEV for t, ax in zip(tensors, axes)]
    blk_shapes = [t[0].shape[:ax] + (sz,) + t[0].shape[ax + 1:] for t, ax, sz in zip(tensors, axes, sizes)]
    flat = [g for t in tensors for g in t]
    offs = [sum(n_layers[:a]) for a in range(n)]

    def body(*refs):
        srcs = refs[:len(flat)]
        lands = refs[len(flat):len(flat) + n]
        send_sems, recv_sems = refs[len(flat) + n:]
        x, y, c, _ = _mesh_pos()
        sib = (x, y, 1 - c)
        for a in range(n):
            for l in range(n_layers[a]):
                for i in range(N_CHIP):
                    pltpu.make_async_remote_copy(
                        src_ref=_block(srcs[offs[a] + l], axes[a], 2 * i + (1 - c), sizes[a]), dst_ref=lands[a].at[i, l],
                        send_sem=send_sems.at[a], recv_sem=recv_sems.at[a], device_id=sib, device_id_type=MESH).start()
        for a in range(n):
            pltpu.make_async_remote_copy(src_ref=lands[a], dst_ref=lands[a], send_sem=send_sems.at[a], recv_sem=recv_sems.at[a],
                                         device_id=sib, device_id_type=MESH).wait()

    any_spec = pl.BlockSpec(memory_space=pl.ANY)
    return pl.pallas_call(
        body, name=name,
        out_shape=[jax.ShapeDtypeStruct((N_CHIP, nl) + bs, t[0].dtype) for nl, bs, t in zip(n_layers, blk_shapes, tensors)],
        in_specs=[any_spec] * len(flat), out_specs=[any_spec] * n,
        scratch_shapes=[pltpu.SemaphoreType.DMA((n,)), pltpu.SemaphoreType.DMA((n,))],
    )(*flat)


def _rs_add_sibling(layers, axis, land, my_c, *, name):
    n_layers = len(layers)
    shape = layers[0].shape
    size = shape[axis] // N_DEV
    blk_shape = shape[:axis] + (size,) + shape[axis + 1:]
    nd = len(shape)
    rows = blk_shape[-2]
    tr = _tile(rows, 512, _row_mult(layers[0].dtype))
    inner = (tr, blk_shape[-1])
    lead = blk_shape[:-2]
    if lead:
        raise ValueError("blocked gradients are 2-D")
    nrb = rows // tr

    def src_map(l):
        def imap(i, r, c_ref):
            j = 2 * i + c_ref[0]
            return (j * nrb + r, 0) if axis == 0 else (r, j)
        return imap

    def body(c_ref, *refs):
        srcs = refs[:n_layers]
        land_ref = refs[n_layers]
        out_ref = refs[n_layers + 1]
        for l in range(n_layers):
            out_ref[l] = (srcs[l][...].astype(F32) + land_ref[l].astype(F32)).astype(out_ref.dtype)

    grid_spec = pltpu.PrefetchScalarGridSpec(
        num_scalar_prefetch=1, grid=(N_CHIP, nrb),
        in_specs=[pl.BlockSpec(inner, src_map(l)) for l in range(n_layers)]
        + [pl.BlockSpec((None, n_layers) + inner, lambda i, r, c_ref: (i, 0, r, 0))],
        out_specs=pl.BlockSpec((None, n_layers) + inner, lambda i, r, c_ref: (i, 0, r, 0)),
    )
    return pl.pallas_call(
        body, name=name, grid_spec=grid_spec,
        out_shape=jax.ShapeDtypeStruct((N_CHIP, n_layers) + blk_shape, layers[0].dtype),
        compiler_params=_params(("arbitrary", "arbitrary")),
    )(my_c, *layers, land)


def _rs_sibling_start(g, axis, *, name):
    size = g.shape[axis] // N_DEV
    land = lax.empty((N_CHIP, 1) + g.shape[:axis] + (size,) + g.shape[axis + 1:], g.dtype)

    def body(src, dst, send_sems, recv_sems, src_thru, dst_thru, token):
        x, y, c, _ = _mesh_pos()
        for i in range(N_CHIP):
            pltpu.make_async_remote_copy(src_ref=_block(src, axis, 2 * i + (1 - c), size), dst_ref=dst.at[i, 0], send_sem=send_sems.at[i],
                                         recv_sem=recv_sems.at[i], device_id=(x, y, 1 - c), device_id_type=MESH).start()
        token[...] = jnp.zeros(token.shape, F32)

    outs = pl.pallas_call(
        body, name=name,
        out_shape=(pltpu.SemaphoreType.DMA((N_CHIP,)), pltpu.SemaphoreType.DMA((N_CHIP,)), pltpu.HBM(g.shape, g.dtype),
                   pltpu.HBM(land.shape, land.dtype), jax.ShapeDtypeStruct((8, LANES), F32)),
        in_specs=[_HBM, _HBM], out_specs=(_SEM, _SEM, _HBM, _HBM, pl.BlockSpec(memory_space=pltpu.VMEM)),
        input_output_aliases={0: 2, 1: 3},
        compiler_params=pltpu.CompilerParams(has_side_effects=_EFFECT),
    )(_in_hbm(g), _in_hbm(land))
    return dict(send_sems=outs[0], recv_sems=outs[1], src=outs[2], land=outs[3], axis=axis), outs[4]


def _rs_sibling_wait(handle, after, *, name):
    src, land, axis = handle["src"], handle["land"], handle["axis"]
    size = src.shape[axis] // N_DEV

    def body(src_ref, dst_ref, send_sems, recv_sems, *rest):
        x, y, c, _ = _mesh_pos()
        for i in range(N_CHIP):
            cp = pltpu.make_async_remote_copy(src_ref=_block(src_ref, axis, 2 * i + (1 - c), size), dst_ref=dst_ref.at[i, 0],
                                              send_sem=send_sems.at[i], recv_sem=recv_sems.at[i], device_id=(x, y, 1 - c),
                                              device_id_type=MESH)
            cp.wait_send()
            cp.wait_recv()

    outs = pl.pallas_call(
        body, name=name,
        out_shape=(pltpu.HBM(src.shape, src.dtype), pltpu.HBM(land.shape, land.dtype)),
        in_specs=[_HBM, _HBM, _SEM, _SEM] + [_ANY] * len(after), out_specs=(_HBM, _HBM),
        input_output_aliases={0: 0, 1: 1},
        compiler_params=pltpu.CompilerParams(has_side_effects=_EFFECT),
    )(src, land, handle["send_sems"], handle["recv_sems"], *after)
    return outs[0], outs[1]


def _rs_chips_start(part, *, name):
    land = lax.empty(part.shape, part.dtype)

    def body(src, dst, send_sems, recv_sems, src_thru, dst_thru, token):
        x, y, c, chips = _mesh_pos()
        for j, chip in enumerate(chips):
            pltpu.make_async_remote_copy(src_ref=src.at[2 * chip[0] + chip[1]], dst_ref=dst.at[2 * x + y], send_sem=send_sems.at[j],
                                         recv_sem=recv_sems.at[j], device_id=(*chip, c), device_id_type=MESH).start()
        token[...] = jnp.zeros(token.shape, F32)

    outs = pl.pallas_call(
        body, name=name,
        out_shape=(pltpu.SemaphoreType.DMA((3,)), pltpu.SemaphoreType.DMA((3,)), pltpu.HBM(part.shape, part.dtype),
                   pltpu.HBM(land.shape, land.dtype), jax.ShapeDtypeStruct((8, LANES), F32)),
        in_specs=[_HBM, _HBM], out_specs=(_SEM, _SEM, _HBM, _HBM, pl.BlockSpec(memory_space=pltpu.VMEM)),
        input_output_aliases={0: 2, 1: 3},
        compiler_params=pltpu.CompilerParams(has_side_effects=_EFFECT),
    )(_in_hbm(part), _in_hbm(land))
    return dict(send_sems=outs[0], recv_sems=outs[1], src=outs[2], land=outs[3]), outs[4]


def _rs_chips_wait(handle, after, *, name):
    def body(src, dst, send_sems, recv_sems, *rest):
        x, y, c, chips = _mesh_pos()
        for j, chip in enumerate(chips):
            cp = pltpu.make_async_remote_copy(src_ref=src.at[2 * chip[0] + chip[1]], dst_ref=dst.at[2 * chip[0] + chip[1]],
                                              send_sem=send_sems.at[j], recv_sem=recv_sems.at[j], device_id=(*chip, c),
                                              device_id_type=MESH)
            cp.wait_send()
            cp.wait_recv()

    src, land = handle["src"], handle["land"]
    outs = pl.pallas_call(
        body, name=name,
        out_shape=(pltpu.HBM(src.shape, src.dtype), pltpu.HBM(land.shape, land.dtype)),
        in_specs=[_HBM, _HBM, _SEM, _SEM] + [_ANY] * len(after), out_specs=(_HBM, _HBM),
        input_output_aliases={0: 0, 1: 1},
        compiler_params=pltpu.CompilerParams(has_side_effects=_EFFECT),
    )(src, land, handle["send_sems"], handle["recv_sems"], *after)
    return outs[0], outs[1]


def _adam_math(w, g, m, v):
    m_new = ADAM_B1 * m + (1.0 - ADAM_B1) * g
    v_new = ADAM_B2 * v + (1.0 - ADAM_B2) * (g * g)
    m_hat = m_new / (1.0 - ADAM_B1 ** ADAM_STEP)
    v_hat = v_new / (1.0 - ADAM_B2 ** ADAM_STEP)
    delta = -ADAM_LR * (m_hat / (jnp.sqrt(v_hat) + ADAM_EPS) + ADAM_WD * w)
    return delta, m_new, v_new


def _sum_adam(part, land, slots, w, m, v, layer, prev, *, name, deps=()):
    n_layers, rows, cols = w.shape
    tr = _tile(rows, 256, _row_mult(land.dtype))
    n_prev = 0 if prev is None else 4

    def body(slots_ref, p0_ref, p1_ref, p2_ref, p3_ref, w_ref, m_ref, v_ref, *rest):
        g_ref, d_ref, mo_ref, vo_ref = rest[n_prev + len(deps):]
        g = p0_ref[...].astype(F32)
        for ref in (p1_ref, p2_ref, p3_ref):
            g = g + ref[...].astype(F32)
        delta, m_new, v_new = _adam_math(w_ref[...], g, m_ref[...], v_ref[...])
        g_ref[...] = g
        d_ref[...] = delta
        mo_ref[...] = m_new
        vo_ref[...] = v_new

    blk = pl.BlockSpec((None, tr, cols), lambda r, s: (layer, r, 0))
    slot = lambda j: pl.BlockSpec((None, None, tr, cols), lambda r, s: (s[j], 0, r, 0))
    shp = jax.ShapeDtypeStruct((n_layers, rows, cols), F32)
    grid_spec = pltpu.PrefetchScalarGridSpec(
        num_scalar_prefetch=1, grid=(rows // tr,),
        in_specs=[slot(0), slot(1), slot(2), slot(3), blk, blk, blk] + [_ANY] * (n_prev + len(deps)),
        out_specs=(blk, blk, blk, blk))
    return pl.pallas_call(
        body, name=name, out_shape=(shp, shp, shp, shp), grid_spec=grid_spec,
        input_output_aliases={8 + i: i for i in range(n_prev)},
        compiler_params=_params(("parallel",)),
    )(slots, part, land, land, land, w, m, v, *(prev or ()), *deps)


def _sum_slots(part, land, slots, *, name):
    _, _, rows, cols = land.shape
    tr = _tile(rows, 256, 8)

    def body(slots_ref, p0_ref, p1_ref, p2_ref, p3_ref, g_ref):
        g_ref[...] = p0_ref[...] + p1_ref[...] + p2_ref[...] + p3_ref[...]

    slot = lambda j: pl.BlockSpec((None, None, tr, cols), lambda r, s: (s[j], 0, r, 0))
    grid_spec = pltpu.PrefetchScalarGridSpec(
        num_scalar_prefetch=1, grid=(rows // tr,), in_specs=[slot(0), slot(1), slot(2), slot(3)],
        out_specs=pl.BlockSpec((tr, cols), lambda r, s: (r, 0)))
    return pl.pallas_call(body, name=name, out_shape=jax.ShapeDtypeStruct((rows, cols), F32), grid_spec=grid_spec,
                          compiler_params=_params(("parallel",)))(slots, part, land, land, land)


def _adam_flat(w, g, m, v, *, name):
    rows, cols = w.shape
    tr = _tile(rows, 256, 8)

    def body(w_ref, g_ref, m_ref, v_ref, d_ref, mo_ref, vo_ref):
        delta, m_new, v_new = _adam_math(w_ref[...], g_ref[...], m_ref[...], v_ref[...])
        d_ref[...] = delta
        mo_ref[...] = m_new
        vo_ref[...] = v_new

    blk = pl.BlockSpec((tr, cols), lambda r: (r, 0))
    shp = jax.ShapeDtypeStruct((rows, cols), F32)
    return pl.pallas_call(
        body, name=name, out_shape=(shp, shp, shp), grid=(rows // tr,), in_specs=[blk] * 4, out_specs=(blk, blk, blk),
        compiler_params=_params(("parallel",)),
    )(w, g, m, v)


_BIG = (("ffn1_w_in", 1), ("ffn1_w_out", 0), ("w_in", 1), ("w_out", 0), ("ffn2_w_in", 1), ("ffn2_w_out", 0))
_SMALL = ("meta_tokens", "ffn1_norm", "mix_norm", "conv_w", "conv_b", "lru_wa", "lru_ba", "lru_wx", "lru_bx", "lru_a_param",
          "pool_w", "pool_b", "pool_scale", "ffn2_norm", "final_norm")
_SMALL_SHARD_AXIS = {"meta_tokens": 1, "conv_w": 2, "pool_w": 2}
_PACK_COLS = 1024


def _pack(arrs):
    flat = jnp.concatenate([a.reshape(-1) for a in arrs])
    unit = N_DEV * 8 * _PACK_COLS
    total = -(-flat.shape[0] // unit) * unit
    flat = jnp.pad(flat, (0, total - flat.shape[0]))
    return flat.reshape(total // _PACK_COLS, _PACK_COLS)


def _unpack(packed, shapes):
    flat = packed.reshape(-1)
    out, off = [], 0
    for s in shapes:
        size = 1
        for v in s:
            size *= v
        out.append(flat[off:off + size].reshape(s))
        off += size
    return out


def _my_shard(full, axis, dev):
    size = full.shape[axis] // N_DEV
    return lax.dynamic_slice_in_dim(full, dev * size, size, axis)


def kernel(x, meta_tokens, ffn1_norm, ffn1_w_in, ffn1_w_out, mix_norm, w_in, conv_w, conv_b, lru_wa, lru_ba, lru_wx, lru_bx, lru_a_param, pool_w, pool_b, pool_scale, w_out, ffn2_norm, ffn2_w_in, ffn2_w_out, final_norm, loss_target, m_meta_tokens, m_ffn1_norm, m_ffn1_w_in, m_ffn1_w_out, m_mix_norm, m_w_in, m_conv_w, m_conv_b, m_lru_wa, m_lru_ba, m_lru_wx, m_lru_bx, m_lru_a_param, m_pool_w, m_pool_b, m_pool_scale, m_w_out, m_ffn2_norm, m_ffn2_w_in, m_ffn2_w_out, m_final_norm, v_meta_tokens, v_ffn1_norm, v_ffn1_w_in, v_ffn1_w_out, v_mix_norm, v_w_in, v_conv_w, v_conv_b, v_lru_wa, v_lru_ba, v_lru_wx, v_lru_bx, v_lru_a_param, v_pool_w, v_pool_b, v_pool_scale, v_w_out, v_ffn2_norm, v_ffn2_w_in, v_ffn2_w_out, v_final_norm):
    names = ("meta_tokens", "ffn1_norm", "ffn1_w_in", "ffn1_w_out", "mix_norm", "w_in", "conv_w", "conv_b", "lru_wa", "lru_ba",
             "lru_wx", "lru_bx", "lru_a_param", "pool_w", "pool_b", "pool_scale", "w_out", "ffn2_norm", "ffn2_w_in", "ffn2_w_out",
             "final_norm")
    w = dict(zip(names, (meta_tokens, ffn1_norm, ffn1_w_in, ffn1_w_out, mix_norm, w_in, conv_w, conv_b, lru_wa, lru_ba, lru_wx,
                         lru_bx, lru_a_param, pool_w, pool_b, pool_scale, w_out, ffn2_norm, ffn2_w_in, ffn2_w_out, final_norm)))
    mom = dict(zip(names, (m_meta_tokens, m_ffn1_norm, m_ffn1_w_in, m_ffn1_w_out, m_mix_norm, m_w_in, m_conv_w, m_conv_b, m_lru_wa,
                           m_lru_ba, m_lru_wx, m_lru_bx, m_lru_a_param, m_pool_w, m_pool_b, m_pool_scale, m_w_out, m_ffn2_norm,
                           m_ffn2_w_in, m_ffn2_w_out, m_final_norm)))
    vel = dict(zip(names, (v_meta_tokens, v_ffn1_norm, v_ffn1_w_in, v_ffn1_w_out, v_mix_norm, v_w_in, v_conv_w, v_conv_b, v_lru_wa,
                           v_lru_ba, v_lru_wx, v_lru_bx, v_lru_a_param, v_pool_w, v_pool_b, v_pool_scale, v_w_out, v_ffn2_norm,
                           v_ffn2_w_in, v_ffn2_w_out, v_final_norm)))
    n_layers, d = ffn1_norm.shape
    n_meta = meta_tokens.shape[0]
    seq = x.shape[1]
    t_valid = n_meta + seq
    tp = -(-t_valid // ROW_ALIGN) * ROW_ALIGN
    dev = 4 * lax.axis_index("x") + 2 * lax.axis_index("y") + lax.axis_index("c")
    my_c = lax.axis_index("c").astype(jnp.int32).reshape(1)
    dev1 = dev.astype(jnp.int32).reshape(1)
    mx, my = lax.axis_index("x"), lax.axis_index("y")
    slots = jnp.stack([2 * mx + my, 2 * (1 - mx) + my, 2 * mx + (1 - my), 2 * (1 - mx) + (1 - my)]).astype(jnp.int32)

    big_axis = dict(_BIG)
    vec = lambda a: a.reshape(a.shape[0], 1, a.shape[1])
    norms = dict(ffn1=vec(ffn1_norm), mix=vec(mix_norm), ffn2=vec(ffn2_norm))
    units = [(kind, l) for l in range(n_layers) for kind in ("ffn1", "mix", "ffn2")]

    handles = {}
    state = dict(token=None, started=0, first=None)

    def behind(a):
        return a if state["first"] is None else lax.optimization_barrier((state["first"], a))[1]

    def shard(k, l):
        return behind(w[k][l:l + 1]).astype(BF16), big_axis[k] + 1

    groups = [[shard("ffn1_w_in", 0), (meta_tokens, 1)], [shard("ffn1_w_out", 0)]]

    def order():
        return [] if state["token"] is None else [state["token"]]

    def start_next():
        i = state["started"]
        if i < len(groups):
            handles[i], state["token"] = _gather_start([s for s, _ in groups[i]], [ax for _, ax in groups[i]], order(),
                                                       name=f"gather{i}_start")
            state["started"] = i + 1

    def finish_begin(i, after):
        handle = handles.pop(i)
        srcs, fulls = _gather_wait(handle, list(after) + order(), name=f"gather{i}_wait")
        fulls = [_place_own(s, z, ax, dev1, name=f"gather{i}_own{a}") for a, (s, z, ax) in enumerate(zip(srcs, fulls, handle["axes"]))]
        passing, state["token"] = _gather_pass_start(fulls, handle["axes"], name=f"gather{i}_pass")
        return i, passing

    def finish_end(ctx, after):
        i, passing = ctx
        fulls = _gather_pass_wait(passing, after, name=f"gather{i}_passwait")
        start_next()
        return fulls

    start_next()
    start_next()
    state["first"] = state["token"]
    for kind, l in units[1:]:
        if kind == "mix":
            groups.append([shard("w_in", l), shard("w_out", l)] + ([(behind(conv_w), 2), (behind(pool_w), 2)] if l == 0 else []))
        else:
            groups.append([shard(f"{kind}_w_in", l), shard(f"{kind}_w_out", l)])
    pad = jnp.zeros((tp - t_valid, d), F32)
    tgt = jnp.concatenate([jnp.zeros((n_meta, d), F32), behind(loss_target[0]), pad], axis=0)
    h0_base = jnp.concatenate([jnp.zeros((n_meta, d), F32), behind(x[0]), pad], axis=0)
    ctx = finish_begin(0, [tgt, h0_base] + [s for grp in groups[2:] for s, _ in grp])
    w_in0, meta_full = finish_end(ctx, [])
    h0 = lax.dynamic_update_slice(h0_base, meta_full, (0, 0))
    n_act = _rms_fwd(h0, norms["ffn1"], 0, name="l0ffn1_rms", deps=order())
    gu, act = _ffn_in_fwd(n_act, w_in0, 0, name="l0ffn1_in")
    (w_out0,) = finish_end(finish_begin(1, [act]), [])
    h = _ffn_fwd_out(h0, act, w_out0, "l0ffn1", order())
    gathered = {units[0]: (w_in0, w_out0)}
    saved = {units[0]: (h0, n_act, gu, act)}
    mp = None
    for ui, (kind, l) in enumerate(units[1:], start=1):
        tag = f"l{l}{kind}"
        ctx = finish_begin(ui + 1, [h])
        n_act = _rms_fwd(h, norms[kind], l, name=f"{tag}_rms", deps=order())
        fulls = finish_end(ctx, [n_act])
        if kind == "mix":
            if l == 0:
                mp = dict(conv_w=fulls[2], conv_b=vec(conv_b), wa=lru_wa.astype(BF16), ba=vec(lru_ba), wx=lru_wx.astype(BF16),
                          bx=vec(lru_bx), a_param=vec(lru_a_param), pool_w=fulls[3].astype(BF16), pool_b=vec(pool_b),
                          pool_scale=vec(pool_scale))
            h, saved[(kind, l)] = _mix_block_fwd(h, n_act, l, fulls[0], fulls[1], mp, tag, order())
        else:
            gu, act = _ffn_in_fwd(n_act, fulls[0], 0, name=f"{tag}_in", deps=order())
            saved[(kind, l)] = (h, n_act, gu, act)
            h = _ffn_fwd_out(h, act, fulls[1], tag, [])
        gathered[(kind, l)] = (fulls[0], fulls[1])
    dh, dhb, g_final, loss_local = _final_loss(h, final_norm.reshape(1, d), tgt, n_meta, t_valid, name="final_loss")
    loss = lax.psum(loss_local[0, 0], ("x", "y", "c"))

    pending = []
    big_out = {}

    landed = []

    def drain(after):
        while pending:
            k, l, handle = pending.pop(0)
            landed.append((k, l) + _rs_chips_wait(handle, after, name=f"rs_{k}{l}_wait"))

    def adams(items, deps):
        for k, l, part, land in items:
            big_out[k] = _sum_adam(part, land, slots, w[k], mom[k], vel[k], l, big_out.get(k), name=f"adam_{k}{l}", deps=deps)
        return [big_out[items[-1][0]][0]] if items else []

    sib_pending = []

    def to_chips(k, l, g, land):
        part = _rs_add_sibling([g], big_axis[k], land, my_c, name=f"rs_{k}{l}_add")
        handle, token = _rs_chips_start(part, name=f"rs_{k}{l}_start")
        pending.append((k, l, handle))
        return token

    def finish_sibling(after):
        token = None
        while sib_pending:
            k, l, handle = sib_pending.pop(0)
            g, land = _rs_sibling_wait(handle, after, name=f"rs_{k}{l}_sibwait")
            token = to_chips(k, l, g, land)
        return token

    def emitter(kind, l):
        last_unit = (kind, l) == units[0]

        def emit(which, g):
            if which == "w_out":
                drain([g])
            token = finish_sibling([g])
            order_after = [] if token is None else [token]
            if which == "flush":
                return order_after
            k = which if kind == "mix" else f"{kind}_{which}"
            if last_unit and which == "w_in":
                (land,) = _rs_exchange_sibling([[g]], [big_axis[k]], name=f"rs_{k}{l}_sib")
                return order_after + [to_chips(k, l, g, land)]
            handle, token = _rs_sibling_start(g, big_axis[k], name=f"rs_{k}{l}_sibstart")
            sib_pending.append((k, l, handle))
            return order_after + [token]
        return emit

    g_norm = {k: [None] * n_layers for k in ("ffn1", "mix", "ffn2")}
    g_mix = [None] * n_layers
    for kind, l in reversed(units):
        w_i, w_o = gathered[(kind, l)]
        tag = f"l{l}{kind}"
        if kind == "mix":
            dh, dhb, g_norm[kind][l], g_mix[l] = _mix_block_bwd(dh, dhb, saved[(kind, l)], norms[kind], l, w_i, w_o, mp, tag,
                                                                 emitter(kind, l), [])
        else:
            dh, dhb, g_norm[kind][l] = _ffn_bwd(dh, dhb, saved[(kind, l)], norms[kind], l, w_i, w_o, tag, emitter(kind, l), [])
    drain([dh])
    dh0 = dh
    grad_x = dh0[n_meta:t_valid][None]

    mix_key = dict(conv_w="conv_w", conv_b="conv_b", lru_wa="wa", lru_ba="ba", lru_wx="wx", lru_bx="bx", lru_a_param="a_param",
                   pool_w="pool_w", pool_b="pool_b", pool_scale="pool_scale")
    small_local = {"meta_tokens": dh0[:n_meta], "final_norm": g_final.reshape(d)}
    for k in ("ffn1", "mix", "ffn2"):
        small_local[f"{k}_norm"] = jnp.stack([g.reshape(d) for g in g_norm[k]])
    for k, mk in mix_key.items():
        small_local[k] = jnp.stack([g_mix[l][mk] for l in range(n_layers)]).reshape(
            w[k].shape if k not in _SMALL_SHARD_AXIS else small_shape_full(w[k], _SMALL_SHARD_AXIS[k]))
    small_shapes = [small_local[k].shape for k in _SMALL]
    packed = _pack([small_local[k] for k in _SMALL])
    (land,) = _rs_exchange_sibling([[packed]], [0], name="rs_small_sib")
    part = _rs_add_sibling([packed], 0, land, my_c, name="rs_small_add")
    cut = max(len(landed) - 4, 0)
    handle, token = _rs_chips_start(part, name="rs_small_start")
    done = adams(landed[:cut], [token])
    part, land = _rs_chips_wait(handle, done, name="rs_small_wait")
    small_block = _sum_slots(part, land, slots, name="rs_sum_small")
    handle, token = _gather_start([small_block], [0], [], name="gather_small_start")
    done = adams(landed[cut:cut + 2], [token])
    (src,), (small_full,) = _gather_wait(handle, done, name="gather_small_wait")
    small_full = _place_own(src, small_full, 0, dev1, name="gather_small_own")
    handle, token = _gather_pass_start([small_full], [0], name="gather_small_pass")
    done = adams(landed[cut + 2:], [token])
    (small_full,) = _gather_pass_wait(handle, done, name="gather_small_passwait")

    out_g, out_d, out_m, out_v = {}, {}, {}, {}
    for k, _ in _BIG:
        out_g[k], out_d[k], out_m[k], out_v[k] = big_out[k]
    small_g = dict(zip(_SMALL, _unpack(small_full, small_shapes)))
    for k, ax in _SMALL_SHARD_AXIS.items():
        small_g[k] = _my_shard(small_g[k], ax, dev)
    shapes_local = [w[k].shape for k in _SMALL]
    d_p, m_p, v_p = _adam_flat(_pack([w[k] for k in _SMALL]), _pack([small_g[k] for k in _SMALL]),
                               _pack([mom[k] for k in _SMALL]), _pack([vel[k] for k in _SMALL]), name="adam_small")
    for k, dd, mm, vv in zip(_SMALL, _unpack(d_p, shapes_local), _unpack(m_p, shapes_local), _unpack(v_p, shapes_local)):
        out_g[k], out_d[k], out_m[k], out_v[k] = small_g[k], dd, mm, vv

    return (loss, grad_x, *[out_g[k] for k in names], *[out_d[k] for k in names], *[out_m[k] for k in names],
            *[out_v[k] for k in names])


def small_shape_full(w_shard, axis):
    return w_shard.shape[:axis] + (N_DEV * w_shard.shape[axis],) + w_shard.shape[axis + 1:]
```

```python
import functools

import jax
import jax.numpy as jnp
from jax import lax
from jax.experimental import pallas as pl
from jax.experimental.pallas import tpu as pltpu

F32 = jnp.float32
BF16 = jnp.bfloat16
MESH = pl.DeviceIdType.MESH

N_DEV = 8
N_CHIP = 4
RMS_EPS = 1e-6
LRU_C = 8.0
CONV_WIDTH = 4
POOL_WINDOWS = (2, 4, 8, 16)
HIST = 16
ADAM_LR = 0.001
ADAM_B1 = 0.9
ADAM_B2 = 0.999
ADAM_EPS = 1e-08
ADAM_WD = 0.01
ADAM_STEP = 10
ROW_ALIGN = 128
V7X_VMEM_LIMIT = 56 * 1024 * 1024
BF16_ROWS = 16
K_TILE = 5632
DN_K_TILE = 2816
WGRAD_TILE = 2048
SCAN_UNROLL = 8
EPILOGUE_ROWS = 528
LANES = 128


def _tile(n, cap, mult=BF16_ROWS):
    best = None
    d = mult
    while d <= min(n, cap):
        if n % d == 0:
            best = d
        d += mult
    if best is None:
        raise ValueError(f"no tile for {n} (cap {cap}, multiple of {mult})")
    return best


def _row_mult(dtype):
    return 8 * 4 // jnp.dtype(dtype).itemsize


def _params(sem=None):
    return pltpu.CompilerParams(dimension_semantics=sem, vmem_limit_bytes=V7X_VMEM_LIMIT)


def _dot(a, b, mode):
    dims = {"nn": ((1,), (0,)), "nt": ((1,), (1,)), "tn": ((0,), (0,))}[mode]
    return lax.dot_general(a, b, (dims, ((), ())), preferred_element_type=F32)


def _sigmoid(x):
    return 1.0 / (1.0 + jnp.exp(-x))


_GELU_C = 0.7978845608028654
_GELU_K = 0.044715


def _gelu(x):
    return 0.5 * x * (1.0 + jnp.tanh(_GELU_C * (x + _GELU_K * x * x * x)))


def _gelu_and_grad(x):
    x2 = x * x
    th = jnp.tanh(_GELU_C * x * (1.0 + _GELU_K * x2))
    half = 0.5 * (1.0 + th)
    return x * half, half + 0.5 * x * (1.0 - th * th) * _GELU_C * (1.0 + 3.0 * _GELU_K * x2)


def _neg_expm1(x):
    p = 1.0 / 5040.0
    for c in (1.0 / 720.0, 1.0 / 120.0, 1.0 / 24.0, 1.0 / 6.0, 0.5, 1.0):
        p = p * x + c
    return jnp.where(x > -0.25, -(x * p), 1.0 - jnp.exp(x))


def _softplus_neg(p):
    e = jnp.exp(-jnp.abs(p))
    u = 1.0 + e
    l1p = jnp.where(u == 1.0, e, jnp.log(u) * e / (u - 1.0 + (u == 1.0).astype(F32)))
    return jnp.maximum(-p, 0.0) + l1p


def _operand_spec(arr, br, bc, ridx, cidx, layer, split):
    if split:
        ncb = arr.shape[2] // bc
        return pl.BlockSpec((None, br, bc), lambda i, j, k: (cidx(i, j, k) // ncb, ridx(i, j, k), cidx(i, j, k) % ncb))
    if layer is not None:
        return pl.BlockSpec((None, br, bc), lambda i, j, k: (layer, ridx(i, j, k), cidx(i, j, k)))
    return pl.BlockSpec((br, bc), lambda i, j, k: (ridx(i, j, k), cidx(i, j, k)))


def _matmul(a, b, *, mode, m, n, kdim, tm, tn, tk, out_dtype, name, scale=None, residual=None,
            a_layer=None, b_layer=None, a_split=False, b_split=False, deps=()):
    nk = kdim // tk
    I = lambda i, j, k: i
    J = lambda i, j, k: j
    K = lambda i, j, k: k
    if mode == "nn":
        a_spec = _operand_spec(a, tm, tk, I, K, a_layer, a_split)
        b_spec = _operand_spec(b, tk, tn, K, J, b_layer, b_split)
    elif mode == "nt":
        a_spec = _operand_spec(a, tm, tk, I, K, a_layer, a_split)
        b_spec = _operand_spec(b, tn, tk, J, K, b_layer, b_split)
    else:
        a_spec = _operand_spec(a, tk, tm, K, I, a_layer, a_split)
        b_spec = _operand_spec(b, tk, tn, K, J, b_layer, b_split)
    in_specs = [a_spec, b_spec]
    operands = [a, b]
    if residual is not None:
        in_specs.append(pl.BlockSpec((tm, tn), lambda i, j, k: (i, j)))
        operands.append(residual)
    in_specs += [pl.BlockSpec(memory_space=pl.ANY)] * len(deps)
    operands += list(deps)
    n_in = len(operands)

    def finish(acc, res_ref, o_ref):
        if scale is not None:
            acc = acc * scale
        if res_ref is not None:
            acc = acc + res_ref[...]
        o_ref[...] = acc.astype(o_ref.dtype)

    def body(*refs):
        a_ref, b_ref = refs[0], refs[1]
        res_ref = refs[2] if residual is not None else None
        o_ref = refs[n_in]
        part = _dot(a_ref[...], b_ref[...], mode)
        if nk == 1:
            finish(part, res_ref, o_ref)
        else:
            acc_ref = refs[-1]
            k = pl.program_id(2)

            @pl.when(k == 0)
            def _():
                acc_ref[...] = part

            @pl.when(k > 0)
            def _():
                acc_ref[...] += part

            @pl.when(k == nk - 1)
            def _():
                finish(acc_ref[...], res_ref, o_ref)

    return pl.pallas_call(
        body, name=name,
        out_shape=jax.ShapeDtypeStruct((m, n), out_dtype),
        grid=(m // tm, n // tn, nk),
        in_specs=in_specs,
        out_specs=pl.BlockSpec((tm, tn), lambda i, j, k: (i, j)),
        scratch_shapes=[pltpu.VMEM((tm, tn), F32)] if nk > 1 else [],
        compiler_params=_params(("parallel", "parallel", "arbitrary")),
    )(*operands)


def _ffn_in_fwd(n_act, w_in, layer, *, name, deps=()):
    tp, d = n_act.shape
    f = w_in.shape[2] // 2
    tm = _tile(tp, 2112)
    tn = _tile(f, 512, LANES)
    nj = f // tn

    ch = _tile(tm, EPILOGUE_ROWS)

    def body(n_ref, wg_ref, wu_ref, *rest):
        da_ref, a_ref = rest[len(deps):]
        for r in range(tm // ch):
            rows = pl.ds(r * ch, ch)
            x = n_ref[rows, :]
            g = _dot(x, wg_ref[...], "nn")
            u = _dot(x, wu_ref[...], "nn")
            s = 0.5 * jnp.tanh(0.5 * g) + 0.5
            sg = g * s
            da_ref[0, rows, :] = (u * s * (1.0 + g * (1.0 - s))).astype(BF16)
            da_ref[1, rows, :] = sg.astype(BF16)
            a_ref[rows, :] = (sg * u).astype(BF16)

    return pl.pallas_call(
        body, name=name,
        out_shape=(jax.ShapeDtypeStruct((2, tp, f), BF16), jax.ShapeDtypeStruct((tp, f), BF16)),
        grid=(tp // tm, nj),
        in_specs=[pl.BlockSpec((tm, d), lambda i, j: (i, 0)),
                  pl.BlockSpec((None, d, tn), lambda i, j: (layer, 0, j)),
                  pl.BlockSpec((None, d, tn), lambda i, j: (layer, 0, j + nj))] + [pl.BlockSpec(memory_space=pl.ANY)] * len(deps),
        out_specs=(pl.BlockSpec((2, tm, tn), lambda i, j: (0, i, j)), pl.BlockSpec((tm, tn), lambda i, j: (i, j))),
        compiler_params=_params(("parallel", "arbitrary")),
    )(n_act, w_in, w_in, *deps)


def _ffn_bwd_act(dhb, w_out, layer, gu, *, name, deps=()):
    tp, d = dhb.shape
    f = w_out.shape[1]
    tm = _tile(tp, 2112)
    tn = _tile(f, 512, LANES)

    ch = _tile(tm, EPILOGUE_ROWS)

    def body(dh_ref, w_ref, gu_ref, *rest):
        dz_ref = rest[len(deps)]
        for r in range(tm // ch):
            rows = pl.ds(r * ch, ch)
            da = 0.5 * _dot(dh_ref[rows, :], w_ref[...], "nt")
            dz_ref[0, rows, :] = (da * gu_ref[0, rows, :].astype(F32)).astype(BF16)
            dz_ref[1, rows, :] = (da * gu_ref[1, rows, :].astype(F32)).astype(BF16)

    return pl.pallas_call(
        body, name=name,
        out_shape=jax.ShapeDtypeStruct((2, tp, f), BF16),
        grid=(tp // tm, f // tn),
        in_specs=[pl.BlockSpec((tm, d), lambda i, j: (i, 0)),
                  pl.BlockSpec((None, tn, d), lambda i, j: (layer, j, 0)),
                  pl.BlockSpec((2, tm, tn), lambda i, j: (0, i, j))] + [pl.BlockSpec(memory_space=pl.ANY)] * len(deps),
        out_specs=pl.BlockSpec((2, tm, tn), lambda i, j: (0, i, j)),
        compiler_params=_params(("parallel", "arbitrary")),
    )(dhb, w_out, gu, *deps)


def _rms_fwd(h, g, layer, *, name, deps=()):
    tp, d = h.shape
    tr = _tile(tp, 528)

    def body(h_ref, g_ref, *rest):
        n_ref = rest[len(deps)]
        x = h_ref[...]
        r = lax.rsqrt(jnp.mean(x * x, axis=-1, keepdims=True) + RMS_EPS)
        n_ref[...] = (x * r * g_ref[...]).astype(BF16)

    return pl.pallas_call(
        body, name=name, out_shape=jax.ShapeDtypeStruct((tp, d), BF16), grid=(tp // tr,),
        in_specs=[pl.BlockSpec((tr, d), lambda i: (i, 0)), pl.BlockSpec((None, 1, d), lambda i: (layer, 0, 0))]
        + [pl.BlockSpec(memory_space=pl.ANY)] * len(deps),
        out_specs=pl.BlockSpec((tr, d), lambda i: (i, 0)),
        compiler_params=_params(("parallel",)),
    )(h, g, *deps)


def _rms_bwd(h, g, layer, dn, dres, *, name, deps=()):
    tp, d = h.shape
    tr = _tile(tp, 528)

    def body(h_ref, g_ref, dn_ref, dres_ref, *rest):
        dh_ref, dhb_ref, dg_ref = rest[len(deps):]
        x = h_ref[...]
        r = lax.rsqrt(jnp.mean(x * x, axis=-1, keepdims=True) + RMS_EPS)
        xhat = x * r
        dn_v = dn_ref[...]
        dxhat = dn_v * g_ref[...]
        dh = dres_ref[...] + r * (dxhat - xhat * jnp.mean(dxhat * xhat, axis=-1, keepdims=True))
        dh_ref[...] = dh
        dhb_ref[...] = dh.astype(BF16)
        part = jnp.sum(dn_v * xhat, axis=0, keepdims=True)

        @pl.when(pl.program_id(0) == 0)
        def _():
            dg_ref[...] = part

        @pl.when(pl.program_id(0) > 0)
        def _():
            dg_ref[...] += part

    return pl.pallas_call(
        body, name=name,
        out_shape=(jax.ShapeDtypeStruct((tp, d), F32), jax.ShapeDtypeStruct((tp, d), BF16), jax.ShapeDtypeStruct((1, d), F32)),
        grid=(tp // tr,),
        in_specs=[pl.BlockSpec((tr, d), lambda i: (i, 0)), pl.BlockSpec((None, 1, d), lambda i: (layer, 0, 0)),
                  pl.BlockSpec((tr, d), lambda i: (i, 0)), pl.BlockSpec((tr, d), lambda i: (i, 0))]
        + [pl.BlockSpec(memory_space=pl.ANY)] * len(deps),
        out_specs=(pl.BlockSpec((tr, d), lambda i: (i, 0)), pl.BlockSpec((tr, d), lambda i: (i, 0)),
                   pl.BlockSpec((1, d), lambda i: (0, 0))),
        compiler_params=_params(("arbitrary",)),
    )(h, g, dn, dres, *deps)


def _final_loss(h, g, tgt, n_meta, t_valid, *, name):
    tp, d = h.shape
    tr = _tile(tp, 528)

    def body(h_ref, g_ref, t_ref, dh_ref, dhb_ref, dg_ref, loss_ref):
        i = pl.program_id(0)
        x = h_ref[...]
        r = lax.rsqrt(jnp.mean(x * x, axis=-1, keepdims=True) + RMS_EPS)
        xhat = x * r
        gv = g_ref[...]
        row = i * tr + lax.broadcasted_iota(jnp.int32, (tr, 1), 0)
        valid = jnp.logical_and(row >= n_meta, row < t_valid)
        err = jnp.where(valid, xhat * gv - t_ref[...], 0.0)
        dy = err * (1.0 / d)
        dxhat = dy * gv
        dh = r * (dxhat - xhat * jnp.mean(dxhat * xhat, axis=-1, keepdims=True))
        dh_ref[...] = dh
        dhb_ref[...] = dh.astype(BF16)
        dg_part = jnp.sum(dy * xhat, axis=0, keepdims=True)
        loss_part = jnp.sum(jnp.sum(err * err, axis=1, keepdims=True), axis=0, keepdims=True) * (0.5 / d)

        @pl.when(i == 0)
        def _():
            dg_ref[...] = dg_part
            loss_ref[...] = loss_part

        @pl.when(i > 0)
        def _():
            dg_ref[...] += dg_part
            loss_ref[...] += loss_part

    return pl.pallas_call(
        body, name=name,
        out_shape=(jax.ShapeDtypeStruct((tp, d), F32), jax.ShapeDtypeStruct((tp, d), BF16),
                   jax.ShapeDtypeStruct((1, d), F32), jax.ShapeDtypeStruct((1, 1), F32)),
        grid=(tp // tr,),
        in_specs=[pl.BlockSpec((tr, d), lambda i: (i, 0)), pl.BlockSpec((1, d), lambda i: (0, 0)),
                  pl.BlockSpec((tr, d), lambda i: (i, 0))],
        out_specs=(pl.BlockSpec((tr, d), lambda i: (i, 0)), pl.BlockSpec((tr, d), lambda i: (i, 0)),
                   pl.BlockSpec((1, d), lambda i: (0, 0)), pl.BlockSpec((1, 1), lambda i: (0, 0))),
        compiler_params=_params(("arbitrary",)),
    )(h, g, tgt)


def _mix_param_specs(mp, layer, imap):
    def vec(a):
        return pl.BlockSpec((None,) + a.shape[1:], lambda *g: (layer,) + (0,) * (a.ndim - 1))
    return [vec(mp[k]) for k in ("conv_w", "conv_b", "wa", "ba", "wx", "bx", "a_param", "pool_w", "pool_b", "pool_scale")]


def _mix_param_list(mp):
    return [mp[k] for k in ("conv_w", "conv_b", "wa", "ba", "wx", "bx", "a_param", "pool_w", "pool_b", "pool_scale")]


def _lru_gates(xc, wa_h, ba_h, wx_h, bx_h, sp_h):
    xb = xc.astype(BF16)
    ra = _sigmoid(_dot(xb, wa_h, "nn") + ba_h)
    ii = _sigmoid(_dot(xb, wx_h, "nn") + bx_h)
    la = -LRU_C * ra * sp_h
    return ra, ii, la, jnp.exp(la)


def _shifted(x, k):
    return x if k == 0 else pltpu.roll(x, k % x.shape[0], axis=0)


def _conv_taps(xbuf, sl, tc):
    x = xbuf[:, sl]
    return [_shifted(x, CONV_WIDTH - 1 - k)[HIST:HIST + tc] for k in range(CONV_WIDTH)]


def _conv_fwd(taps, cw_ref, cb_ref, sl):
    xc = cb_ref[:, sl]
    for k in range(CONV_WIDTH):
        xc = xc + cw_ref[k:k + 1, sl] * taps[k]
    return xc


def _window_sum(x, win, direction):
    s = x
    step = 1
    while step < win:
        s = s + _shifted(s, direction * step)
        step *= 2
    return s


def _pool_delta(pbuf, cols, win, t0, tc):
    x = pbuf[:, cols]
    u = x[HIST:HIST + tc]
    s = _window_sum(x, win, 1)[HIST:HIST + tc]
    t = t0 + lax.broadcasted_iota(jnp.int32, (tc, 1), 0)
    inv_cnt = 1.0 / jnp.minimum(t + 1, win).astype(F32)
    return s * inv_cnt - u, inv_cnt


def _mix_fwd(z, mp, layer, *, name):
    tp = z.shape[0]
    dl = z.shape[1] // 3
    n_heads, hd = mp["wa"].shape[1], mp["wa"].shape[2]
    n_groups, gd = mp["pool_w"].shape[1], mp["pool_w"].shape[2]
    tc = _tile(tp, 384)

    def body(z_ref, cw_ref, cb_ref, wa_ref, ba_ref, wx_ref, bx_ref, ap_ref, pw_ref, pb_ref, ps_ref,
             m_ref, hs_ref, aux_ref, xbuf, pbuf, a_s, b_s, hcar):
        i = pl.program_id(0)

        @pl.when(i == 0)
        def _():
            xbuf[pl.ds(0, HIST), :] = jnp.zeros((HIST, dl), F32)
            pbuf[pl.ds(0, HIST), :] = jnp.zeros((HIST, dl), F32)
            hcar[...] = jnp.zeros((1, dl), F32)

        @pl.when(i > 0)
        def _():
            xbuf[pl.ds(0, HIST), :] = xbuf[pl.ds(tc, HIST), :]
            pbuf[pl.ds(0, HIST), :] = pbuf[pl.ds(tc, HIST), :]

        xbuf[pl.ds(HIST, tc), :] = z_ref[:, pl.ds(0, dl)]
        pbuf[pl.ds(HIST, tc), :] = z_ref[:, pl.ds(2 * dl, dl)]
        sp = _softplus_neg(ap_ref[...])
        for h in range(n_heads):
            sl = pl.ds(h * hd, hd)
            xc = _conv_fwd(_conv_taps(xbuf, sl, tc), cw_ref, cb_ref, sl)
            ra, ii, la, a = _lru_gates(xc, wa_ref[h], ba_ref[:, sl], wx_ref[h], bx_ref[:, sl], sp[:, h * hd:(h + 1) * hd])
            e = _neg_expm1(2.0 * la)
            a_s[:, sl] = a
            b_s[:, sl] = jnp.sqrt(e) * ii * xc
            for slot, val in ((_AUX_R, ra), (_AUX_I, ii), (_AUX_X, xc), (_AUX_E, e)):
                aux_ref[slot, :, sl] = val

        def step(t, hprev):
            hnew = a_s[pl.ds(t, 1), :] * hprev + b_s[pl.ds(t, 1), :]
            hs_ref[pl.ds(t, 1), :] = hnew
            return hnew

        hcar[...] = lax.fori_loop(0, tc, step, hcar[...], unroll=SCAN_UNROLL)
        for h in range(n_heads):
            sl = pl.ds(h * hd, hd)
            m_ref[:, sl] = (hs_ref[:, sl] * _gelu(z_ref[:, pl.ds(dl + h * hd, hd)])).astype(BF16)
        for g in range(n_groups):
            cols = pl.ds(g * gd, gd)
            dlt, _ = _pool_delta(pbuf, cols, POOL_WINDOWS[g], i * tc, tc)
            q = _dot(dlt.astype(BF16), pw_ref[g], "nn") + pb_ref[:, cols]
            m_ref[:, pl.ds(dl + g * gd, gd)] = (q * ps_ref[:, cols]).astype(BF16)

    return pl.pallas_call(
        body, name=name,
        out_shape=(jax.ShapeDtypeStruct((tp, 2 * dl), BF16), jax.ShapeDtypeStruct((tp, dl), F32),
                   jax.ShapeDtypeStruct((_AUX_N, tp, dl), F32)),
        grid=(tp // tc,),
        in_specs=[pl.BlockSpec((tc, 3 * dl), lambda i: (i, 0))] + _mix_param_specs(mp, layer, None),
        out_specs=(pl.BlockSpec((tc, 2 * dl), lambda i: (i, 0)), pl.BlockSpec((tc, dl), lambda i: (i, 0)),
                   pl.BlockSpec((_AUX_N, tc, dl), lambda i: (0, i, 0))),
        scratch_shapes=[pltpu.VMEM((HIST + tc, dl), F32), pltpu.VMEM((HIST + tc, dl), F32),
                        pltpu.VMEM((tc, dl), F32), pltpu.VMEM((tc, dl), F32), pltpu.VMEM((1, dl), F32)],
        compiler_params=_params(("arbitrary",)),
    )(z, *_mix_param_list(mp))


_MIX_GRADS = ("conv_w", "conv_b", "wa", "ba", "wx", "bx", "a_param", "pool_w", "pool_b", "pool_scale")
_AUX_R, _AUX_I, _AUX_X, _AUX_E, _AUX_N = 0, 1, 2, 3, 4


def _mix_bwd(z, hs, aux, dm, mp, layer, *, name):
    tp = z.shape[0]
    dl = z.shape[1] // 3
    n_heads, hd = mp["wa"].shape[1], mp["wa"].shape[2]
    n_groups, gd = mp["pool_w"].shape[1], mp["pool_w"].shape[2]
    tc = _tile(tp, 352)
    nc = tp // tc
    per = tc // HIST

    def body(z_ref, zp_ref, hs_ref, hsp_ref, aux_ref, dm_ref, cw_ref, cb_ref, wa_ref, ba_ref, wx_ref, bx_ref, ap_ref, pw_ref, pb_ref,
             ps_ref, dz_ref, dcw_ref, dcb_ref, dwa_ref, dba_ref, dwx_ref, dbx_ref, dap_ref, dpw_ref, dpb_ref, dps_ref,
             xbuf, pbuf, hbuf, dxbuf, ddbuf, a_s, lam_s, ccar):
        i = pl.program_id(0)
        ci = nc - 1 - i

        @pl.when(i == 0)
        def _():
            dxbuf[pl.ds(tc, HIST), :] = jnp.zeros((HIST, dl), F32)
            ddbuf[pl.ds(tc, HIST), :] = jnp.zeros((HIST, dl), F32)
            ccar[...] = jnp.zeros((1, dl), F32)
            for ref in (dcw_ref, dcb_ref, dwa_ref, dba_ref, dwx_ref, dbx_ref, dap_ref, dpw_ref, dpb_ref, dps_ref):
                ref[...] = jnp.zeros(ref.shape, F32)

        @pl.when(ci == 0)
        def _():
            xbuf[pl.ds(0, HIST), :] = jnp.zeros((HIST, dl), F32)
            pbuf[pl.ds(0, HIST), :] = jnp.zeros((HIST, dl), F32)
            hbuf[pl.ds(0, HIST), :] = jnp.zeros((HIST, dl), F32)

        @pl.when(ci > 0)
        def _():
            xbuf[pl.ds(0, HIST), :] = zp_ref[:, pl.ds(0, dl)]
            pbuf[pl.ds(0, HIST), :] = zp_ref[:, pl.ds(2 * dl, dl)]
            hbuf[pl.ds(0, HIST), :] = hsp_ref[...]

        xbuf[pl.ds(HIST, tc), :] = z_ref[:, pl.ds(0, dl)]
        pbuf[pl.ds(HIST, tc), :] = z_ref[:, pl.ds(2 * dl, dl)]
        hbuf[pl.ds(HIST, tc), :] = hs_ref[...]
        sp = _softplus_neg(ap_ref[...])

        for h in range(n_heads):
            sl = pl.ds(h * hd, hd)
            a_s[:, sl] = jnp.exp(-LRU_C * aux_ref[_AUX_R, :, sl] * sp[:, h * hd:(h + 1) * hd])
            gel, gel_grad = _gelu_and_grad(z_ref[:, pl.ds(dl + h * hd, hd)])
            dya = dm_ref[:, sl]
            lam_s[:, sl] = dya * gel
            dz_ref[:, pl.ds(dl + h * hd, hd)] = (dya * hs_ref[:, sl] * gel_grad).astype(BF16)

        def step(r, carry):
            t = tc - 1 - r
            lam = lam_s[pl.ds(t, 1), :] + carry
            lam_s[pl.ds(t, 1), :] = lam
            return a_s[pl.ds(t, 1), :] * lam

        ccar[...] = lax.fori_loop(0, tc, step, ccar[...], unroll=SCAN_UNROLL)

        for h in range(n_heads):
            sl = pl.ds(h * hd, hd)
            sp_h = sp[:, h * hd:(h + 1) * hd]
            lam = lam_s[:, sl]
            a = a_s[:, sl]
            ra = aux_ref[_AUX_R, :, sl]
            ii = aux_ref[_AUX_I, :, sl]
            xc = aux_ref[_AUX_X, :, sl]
            e = aux_ref[_AUX_E, :, sl]
            inv_mult = lax.rsqrt(e)
            mult = e * inv_mult
            hprev = _shifted(hbuf[:, sl], 1)[HIST:HIST + tc]
            lam_i = lam * ii
            lam_m = lam * mult
            dla = lam * hprev * a - lam_i * xc * (a * a) * inv_mult
            dla_r = dla * (-LRU_C) * ra
            dap_ref[:, sl] += jnp.sum(dla_r, axis=0, keepdims=True)
            dpa = dla_r * sp_h * (1.0 - ra)
            dpx = lam_m * xc * ii * (1.0 - ii)
            dba_ref[:, sl] += jnp.sum(dpa, axis=0, keepdims=True)
            dbx_ref[:, sl] += jnp.sum(dpx, axis=0, keepdims=True)
            xb = xc.astype(BF16)
            dpa_b = dpa.astype(BF16)
            dpx_b = dpx.astype(BF16)
            dwa_ref[h] += _dot(xb, dpa_b, "tn")
            dwx_ref[h] += _dot(xb, dpx_b, "tn")
            dxc = lam_m * ii + _dot(dpa_b, wa_ref[h], "nt") + _dot(dpx_b, wx_ref[h], "nt")
            dxbuf[pl.ds(0, tc), sl] = dxc
            dcb_ref[:, sl] += jnp.sum(dxc, axis=0, keepdims=True)
            taps = _conv_taps(xbuf, sl, tc)
            dx_all = dxbuf[:, sl]
            dzx = jnp.zeros((tc, hd), F32)
            for k in range(CONV_WIDTH):
                dcw_ref[k:k + 1, sl] += jnp.sum(dxc * taps[k], axis=0, keepdims=True)
                dzx = dzx + cw_ref[k:k + 1, sl] * _shifted(dx_all, k - (CONV_WIDTH - 1))[0:tc]
            dz_ref[:, sl] = dzx.astype(BF16)
            dxbuf[pl.ds(tc, HIST), sl] = dxbuf[pl.ds(0, HIST), sl]

        for g in range(n_groups):
            cols = pl.ds(g * gd, gd)
            win = POOL_WINDOWS[g]
            dlt, inv_cnt = _pool_delta(pbuf, cols, win, ci * tc, tc)
            db = dlt.astype(BF16)
            q = _dot(db, pw_ref[g], "nn") + pb_ref[:, cols]
            dyb = dm_ref[:, pl.ds(dl + g * gd, gd)]
            dps_ref[:, cols] += jnp.sum(dyb * q, axis=0, keepdims=True)
            dq = dyb * ps_ref[:, cols]
            dpb_ref[:, cols] += jnp.sum(dq, axis=0, keepdims=True)
            dqb = dq.astype(BF16)
            dpw_ref[g] += _dot(db, dqb, "tn")
            dd = _dot(dqb, pw_ref[g], "nt")
            ddbuf[pl.ds(0, tc), cols] = dd * inv_cnt
            dzp = _window_sum(ddbuf[:, cols], win, -1)[0:tc] - dd
            dz_ref[:, pl.ds(2 * dl + g * gd, gd)] = dzp.astype(BF16)
            ddbuf[pl.ds(tc, HIST), cols] = ddbuf[pl.ds(0, HIST), cols]

        @pl.when(i == nc - 1)
        def _():
            dap_ref[...] = dap_ref[...] * (-_sigmoid(-ap_ref[...]))

    rev = lambda i: (nc - 1 - i, 0)
    prev = lambda i: (jnp.maximum((nc - 1 - i) * per - 1, 0), 0)
    const = lambda a: pl.BlockSpec(a.shape[1:], lambda i: (0,) * (a.ndim - 1))
    plist = _mix_param_list(mp)
    grad_shapes = [jax.ShapeDtypeStruct(a.shape[1:], F32) for a in plist]
    buf = lambda rows: pltpu.VMEM((rows, dl), F32)
    outs = pl.pallas_call(
        body, name=name,
        out_shape=[jax.ShapeDtypeStruct((tp, 3 * dl), BF16)] + grad_shapes,
        grid=(nc,),
        in_specs=[pl.BlockSpec((tc, 3 * dl), rev), pl.BlockSpec((HIST, 3 * dl), prev),
                  pl.BlockSpec((tc, dl), rev), pl.BlockSpec((HIST, dl), prev),
                  pl.BlockSpec((_AUX_N, tc, dl), lambda i: (0, nc - 1 - i, 0)),
                  pl.BlockSpec((tc, 2 * dl), rev)] + _mix_param_specs(mp, layer, None),
        out_specs=[pl.BlockSpec((tc, 3 * dl), rev)] + [const(a) for a in plist],
        scratch_shapes=[buf(HIST + tc), buf(HIST + tc), buf(HIST + tc), buf(tc + HIST), buf(tc + HIST),
                        buf(tc), buf(tc), buf(1)],
        compiler_params=_params(("arbitrary",)),
    )(z, z, hs, hs, aux, dm, *plist)
    return outs[0], dict(zip(_MIX_GRADS, outs[1:]))


def _ffn_fwd_out(h, act, w_out, tag, deps):
    tp, d = h.shape
    f = w_out.shape[1]
    return _matmul(act, w_out, mode="nn", m=tp, n=d, kdim=f, tm=_tile(tp, 1056), tn=_tile(d, 512, LANES),
                   tk=_tile(f, K_TILE, LANES), out_dtype=F32, scale=0.5, residual=h, b_layer=0, name=f"{tag}_out", deps=deps)


def _ffn_bwd(dh, dhb, saved, norm, layer, w_in, w_out, tag, emit, deps):
    h, n_act, gu, act = saved
    tp, d = h.shape
    f = w_out.shape[1]
    dz = _ffn_bwd_act(dhb, w_out, 0, gu, name=f"{tag}_bact", deps=deps)
    tok = emit("w_out", _matmul(act, dhb, mode="tn", m=f, n=d, kdim=tp, tm=_tile(f, 512, LANES), tn=_tile(d, WGRAD_TILE, LANES), tk=tp,
                                out_dtype=BF16, scale=0.5, name=f"{tag}_dwout", deps=[dz]))
    tok = emit("w_in", _matmul(n_act, dz, mode="tn", m=d, n=2 * f, kdim=tp, tm=_tile(d, WGRAD_TILE, LANES), tn=_tile(f, 512, LANES),
                               tk=tp, out_dtype=BF16, b_split=True, name=f"{tag}_dwin", deps=tok))
    dn = _matmul(dz, w_in, mode="nt", m=tp, n=d, kdim=2 * f, tm=_tile(tp, 1056), tn=_tile(d, 1024, LANES), tk=_tile(f, DN_K_TILE, LANES),
                 out_dtype=F32, a_split=True, b_layer=0, name=f"{tag}_dn", deps=tok)
    return _rms_bwd(h, norm, layer, dn, dh, name=f"{tag}_brms", deps=emit("flush", dn))


def _mix_block_fwd(h, n_act, layer, w_in, w_out, mp, tag, deps):
    tp, d = h.shape
    d_in = w_in.shape[2]
    z = _matmul(n_act, w_in, mode="nn", m=tp, n=d_in, kdim=d, tm=_tile(tp, 2112), tn=_tile(d_in, 512, LANES), tk=d,
                out_dtype=F32, b_layer=0, name=f"{tag}_in", deps=deps)
    m_act, hs, aux = _mix_fwd(z, mp, layer, name=f"{tag}_mix")
    h_out = _matmul(m_act, w_out, mode="nn", m=tp, n=d, kdim=d, tm=_tile(tp, 2112), tn=_tile(d, 512, LANES), tk=d,
                    out_dtype=F32, residual=h, b_layer=0, name=f"{tag}_out")
    return h_out, (h, n_act, z, hs, aux, m_act)


def _mix_block_bwd(dh, dhb, saved, norm, layer, w_in, w_out, mp, tag, emit, deps):
    h, n_act, z, hs, aux, m_act = saved
    tp, d = h.shape
    d_in = w_in.shape[2]
    dm = _matmul(dhb, w_out, mode="nt", m=tp, n=d, kdim=d, tm=_tile(tp, 2112), tn=_tile(d, 512, LANES), tk=d,
                 out_dtype=F32, b_layer=0, name=f"{tag}_dm", deps=deps)
    tok = emit("w_out", _matmul(m_act, dhb, mode="tn", m=d, n=d, kdim=tp, tm=_tile(d, 512, LANES), tn=_tile(d, WGRAD_TILE, LANES), tk=tp,
                                out_dtype=BF16, name=f"{tag}_dwout", deps=[dm]))
    dz, g_mix = _mix_bwd(z, hs, aux, dm, mp, layer, name=f"{tag}_bmix")
    tok = emit("w_in", _matmul(n_act, dz, mode="tn", m=d, n=d_in, kdim=tp, tm=_tile(d, WGRAD_TILE, LANES), tn=_tile(d_in, 512, LANES),
                               tk=tp, out_dtype=BF16, name=f"{tag}_dwin", deps=tok))
    dn = _matmul(dz, w_in, mode="nt", m=tp, n=d, kdim=d_in, tm=_tile(tp, 1056), tn=_tile(d, 1024, LANES), tk=d_in,
                 out_dtype=F32, b_layer=0, name=f"{tag}_dn", deps=tok)
    dh_in, dhb_in, g_norm = _rms_bwd(h, norm, layer, dn, dh, name=f"{tag}_brms", deps=emit("flush", dn))
    return dh_in, dhb_in, g_norm, g_mix


def _mesh_pos():
    x, y, c = lax.axis_index("x"), lax.axis_index("y"), lax.axis_index("c")
    chips = [(1 - x, y), (x, 1 - y), (1 - x, 1 - y)]
    return x, y, c, chips


def _block(ref, axis, j, size):
    idx = [slice(None)] * len(ref.shape)
    idx[axis] = pl.ds(j * size, size)
    return ref.at[tuple(idx)]


_HBM = pl.BlockSpec(memory_space=pltpu.HBM)
_SEM = pl.BlockSpec(memory_space=pltpu.SEMAPHORE)
_ANY = pl.BlockSpec(memory_space=pl.ANY)
_EFFECT = pltpu.SideEffectType.DATAFLOW_SIDE_EFFECTING
_N_OUT = 4


def _in_hbm(a):
    return pltpu.with_memory_space_constraint(a, pltpu.HBM)


def _gather_start(shards, axes, deps, *, name, lands=None):
    n = len(shards)
    sizes = [s.shape[ax] for s, ax in zip(shards, axes)]
    full_shapes = [s.shape[:ax] + (N_DEV * s.shape[ax],) + s.shape[ax + 1:] for s, ax in zip(shards, axes)]

    def body(*refs):
        srcs, lands = refs[:n], refs[n:2 * n]
        send_sems, recv_sems = refs[2 * n + len(deps)], refs[2 * n + len(deps) + 1]
        token = refs[-1]
        x, y, c, chips = _mesh_pos()
        targets = [(x, y, 1 - c)] + [(*chip, c) for chip in chips]
        for a in range(n):
            dst = _block(lands[a], axes[a], 4 * x + 2 * y + c, sizes[a])
            for k, to in enumerate(targets):
                pltpu.make_async_remote_copy(src_ref=srcs[a], dst_ref=dst, send_sem=send_sems.at[_N_OUT * a + k],
                                             recv_sem=recv_sems.at[_N_OUT * a + k],
                                             device_id=to, device_id_type=MESH).start()
        token[...] = jnp.zeros(token.shape, F32)

    lands0 = [lax.empty(fs, s.dtype) for fs, s in zip(full_shapes, shards)] if lands is None else lands
    outs = pl.pallas_call(
        body, name=name,
        out_shape=(pltpu.SemaphoreType.DMA((n * _N_OUT,)), pltpu.SemaphoreType.DMA((n * _N_OUT,)),
                   *[pltpu.HBM(s.shape, s.dtype) for s in shards], *[pltpu.HBM(fs, s.dtype) for fs, s in zip(full_shapes, shards)],
                   jax.ShapeDtypeStruct((8, LANES), F32)),
        in_specs=[_HBM] * (2 * n) + [_ANY] * len(deps),
        out_specs=(_SEM, _SEM, *[_HBM] * (2 * n), pl.BlockSpec(memory_space=pltpu.VMEM)),
        input_output_aliases={a: 2 + a for a in range(2 * n)},
        compiler_params=pltpu.CompilerParams(has_side_effects=_EFFECT),
    )(*[_in_hbm(s) for s in shards], *[_in_hbm(z) for z in lands0], *deps)
    handle = dict(send_sems=outs[0], recv_sems=outs[1], srcs=list(outs[2:2 + n]), lands=list(outs[2 + n:2 + 2 * n]), axes=list(axes))
    return handle, outs[-1]


def _gather_wait(handle, after, *, name):
    srcs, lands, axes = handle["srcs"], handle["lands"], handle["axes"]
    n = len(srcs)
    sizes = [s.shape[ax] for s, ax in zip(srcs, axes)]

    def body(*refs):
        src_refs, land_refs = refs[:n], refs[n:2 * n]
        send_sems, recv_sems = refs[2 * n], refs[2 * n + 1]
        x, y, c, chips = _mesh_pos()
        peers = [(x, y, 1 - c)] + [(*chip, c) for chip in chips]
        for a in range(n):
            for k, dev in enumerate(peers):
                cp = pltpu.make_async_remote_copy(
                    src_ref=src_refs[a], dst_ref=_block(land_refs[a], axes[a], 4 * dev[0] + 2 * dev[1] + dev[2], sizes[a]),
                    send_sem=send_sems.at[_N_OUT * a + k], recv_sem=recv_sems.at[_N_OUT * a + k], device_id=dev,
                    device_id_type=MESH)
                cp.wait_send()
                cp.wait_recv()

    outs = pl.pallas_call(
        body, name=name,
        out_shape=(*[pltpu.HBM(s.shape, s.dtype) for s in srcs], *[pltpu.HBM(z.shape, z.dtype) for z in lands]),
        in_specs=[_HBM] * (2 * n) + [_SEM, _SEM] + [_ANY] * len(after),
        out_specs=tuple([_HBM] * (2 * n)),
        input_output_aliases={a: a for a in range(2 * n)},
        compiler_params=pltpu.CompilerParams(has_side_effects=_EFFECT),
    )(*srcs, *lands, handle["send_sems"], handle["recv_sems"], *after)
    return list(outs[:n]), list(outs[n:])


def _place_own(shard, full, axis, dev, *, name):
    nd = shard.ndim
    rows, cols = shard.shape[-2:]
    tr = _tile(rows, 512, _row_mult(shard.dtype)) if rows % _row_mult(shard.dtype) == 0 else rows
    nrb = rows // tr
    block = shard.shape[:-2] + (tr, cols)

    def in_map(r, dev_ref):
        return (0,) * (nd - 2) + (r, 0)

    def out_map(r, dev_ref):
        idx = [0] * nd
        idx[nd - 2] = r
        idx[axis] = dev_ref[0] * (nrb if axis == nd - 2 else 1) + idx[axis]
        return tuple(idx)

    def body(dev_ref, src_ref, full_ref, out_ref):
        out_ref[...] = src_ref[...]

    grid_spec = pltpu.PrefetchScalarGridSpec(
        num_scalar_prefetch=1, grid=(nrb,),
        in_specs=[pl.BlockSpec(block, in_map), _ANY], out_specs=pl.BlockSpec(block, out_map))
    return pl.pallas_call(body, name=name, grid_spec=grid_spec, out_shape=jax.ShapeDtypeStruct(full.shape, full.dtype),
                          input_output_aliases={2: 0}, compiler_params=_params(("arbitrary",)))(dev, shard, full)


def _gather_pass_start(fulls, axes, *, name):
    n = len(fulls)
    sizes = [z.shape[ax] // N_DEV for z, ax in zip(fulls, axes)]

    def body(*refs):
        srcs = refs[:n]
        send_sems, recv_sems = refs[n], refs[n + 1]
        token = refs[-1]
        x, y, c, chips = _mesh_pos()
        for a in range(n):
            for j, chip in enumerate(chips):
                blk = _block(srcs[a], axes[a], 4 * chip[0] + 2 * chip[1] + c, sizes[a])
                pltpu.make_async_remote_copy(src_ref=blk, dst_ref=blk, send_sem=send_sems.at[3 * a + j], recv_sem=recv_sems.at[3 * a + j],
                                             device_id=(x, y, 1 - c), device_id_type=MESH).start()
        token[...] = jnp.zeros(token.shape, F32)

    outs = pl.pallas_call(
        body, name=name,
        out_shape=(pltpu.SemaphoreType.DMA((3 * n,)), pltpu.SemaphoreType.DMA((3 * n,)), *[pltpu.HBM(z.shape, z.dtype) for z in fulls],
                   jax.ShapeDtypeStruct((8, LANES), F32)),
        in_specs=[_HBM] * n, out_specs=(_SEM, _SEM, *[_HBM] * n, pl.BlockSpec(memory_space=pltpu.VMEM)),
        input_output_aliases={a: 2 + a for a in range(n)},
        compiler_params=pltpu.CompilerParams(has_side_effects=_EFFECT),
    )(*[_in_hbm(z) for z in fulls])
    return dict(send_sems=outs[0], recv_sems=outs[1], fulls=list(outs[2:2 + n]), axes=list(axes)), outs[-1]


def _gather_pass_wait(handle, after, *, name):
    fulls, axes = handle["fulls"], handle["axes"]
    n = len(fulls)
    sizes = [z.shape[ax] // N_DEV for z, ax in zip(fulls, axes)]

    def body(*refs):
        bufs = refs[:n]
        send_sems, recv_sems = refs[n], refs[n + 1]
        x, y, c, chips = _mesh_pos()
        for a in range(n):
            for j, chip in enumerate(chips):
                cp = pltpu.make_async_remote_copy(
                    src_ref=_block(bufs[a], axes[a], 4 * chip[0] + 2 * chip[1] + c, sizes[a]),
                    dst_ref=_block(bufs[a], axes[a], 4 * chip[0] + 2 * chip[1] + (1 - c), sizes[a]),
                    send_sem=send_sems.at[3 * a + j], recv_sem=recv_sems.at[3 * a + j], device_id=(x, y, 1 - c),
                    device_id_type=MESH)
                cp.wait_send()
                cp.wait_recv()

    outs = pl.pallas_call(
        body, name=name,
        out_shape=tuple(pltpu.HBM(z.shape, z.dtype) for z in fulls),
        in_specs=[_HBM] * n + [_SEM, _SEM] + [_ANY] * len(after), out_specs=tuple([_HBM] * n),
        input_output_aliases={a: a for a in range(n)},
        compiler_params=pltpu.CompilerParams(has_side_effects=_EFFECT),
    )(*fulls, handle["send_sems"], handle["recv_sems"], *after)
    return list(outs)


def _rs_exchange_sibling(tensors, axes, *, name):
    n = len(tensors)
    n_layers = [len(t) for t in tensors]
    sizes = [t[0].shape[ax] // N_DEV for t, ax in zip(tensors, axes)]
    blk_shapes = [t[0].shape[:ax] + (sz,) + t[0].shape[ax + 1:] for t, ax, sz in zip(tensors, axes, sizes)]
    flat = [g for t in tensors for g in t]
    offs = [sum(n_layers[:a]) for a in range(n)]

    def body(*refs):
        srcs = refs[:len(flat)]
        lands = refs[len(flat):len(flat) + n]
        send_sems, recv_sems = refs[len(flat) + n:]
        x, y, c, _ = _mesh_pos()
        sib = (x, y, 1 - c)
        for a in range(n):
            for l in range(n_layers[a]):
                for i in range(N_CHIP):
                    pltpu.make_async_remote_copy(
                        src_ref=_block(srcs[offs[a] + l], axes[a], 2 * i + (1 - c), sizes[a]), dst_ref=lands[a].at[i, l],
                        send_sem=send_sems.at[a], recv_sem=recv_sems.at[a], device_id=sib, device_id_type=MESH).start()
        for a in range(n):
            pltpu.make_async_remote_copy(src_ref=lands[a], dst_ref=lands[a], send_sem=send_sems.at[a], recv_sem=recv_sems.at[a],
                                         device_id=sib, device_id_type=MESH).wait()

    any_spec = pl.BlockSpec(memory_space=pl.ANY)
    return pl.pallas_call(
        body, name=name,
        out_shape=[jax.ShapeDtypeStruct((N_CHIP, nl) + bs, t[0].dtype) for nl, bs, t in zip(n_layers, blk_shapes, tensors)],
        in_specs=[any_spec] * len(flat), out_specs=[any_spec] * n,
        scratch_shapes=[pltpu.SemaphoreType.DMA((n,)), pltpu.SemaphoreType.DMA((n,))],
    )(*flat)


def _rs_add_sibling(layers, axis, land, my_c, *, name):
    n_layers = len(layers)
    shape = layers[0].shape
    size = shape[axis] // N_DEV
    blk_shape = shape[:axis] + (size,) + shape[axis + 1:]
    nd = len(shape)
    rows = blk_shape[-2]
    tr = _tile(rows, 512, _row_mult(layers[0].dtype))
    inner = (tr, blk_shape[-1])
    lead = blk_shape[:-2]
    if lead:
        raise ValueError("blocked gradients are 2-D")
    nrb = rows // tr

    def src_map(l):
        def imap(i, r, c_ref):
            j = 2 * i + c_ref[0]
            return (j * nrb + r, 0) if axis == 0 else (r, j)
        return imap

    def body(c_ref, *refs):
        srcs = refs[:n_layers]
        land_ref = refs[n_layers]
        out_ref = refs[n_layers + 1]
        for l in range(n_layers):
            out_ref[l] = (srcs[l][...].astype(F32) + land_ref[l].astype(F32)).astype(out_ref.dtype)

    grid_spec = pltpu.PrefetchScalarGridSpec(
        num_scalar_prefetch=1, grid=(N_CHIP, nrb),
        in_specs=[pl.BlockSpec(inner, src_map(l)) for l in range(n_layers)]
        + [pl.BlockSpec((None, n_layers) + inner, lambda i, r, c_ref: (i, 0, r, 0))],
        out_specs=pl.BlockSpec((None, n_layers) + inner, lambda i, r, c_ref: (i, 0, r, 0)),
    )
    return pl.pallas_call(
        body, name=name, grid_spec=grid_spec,
        out_shape=jax.ShapeDtypeStruct((N_CHIP, n_layers) + blk_shape, layers[0].dtype),
        compiler_params=_params(("arbitrary", "arbitrary")),
    )(my_c, *layers, land)


def _rs_sibling_start(g, axis, *, name):
    size = g.shape[axis] // N_DEV
    land = lax.empty((N_CHIP, 1) + g.shape[:axis] + (size,) + g.shape[axis + 1:], g.dtype)

    def body(src, dst, send_sems, recv_sems, src_thru, dst_thru, token):
        x, y, c, _ = _mesh_pos()
        for i in range(N_CHIP):
            pltpu.make_async_remote_copy(src_ref=_block(src, axis, 2 * i + (1 - c), size), dst_ref=dst.at[i, 0], send_sem=send_sems.at[i],
                                         recv_sem=recv_sems.at[i], device_id=(x, y, 1 - c), device_id_type=MESH).start()
        token[...] = jnp.zeros(token.shape, F32)

    outs = pl.pallas_call(
        body, name=name,
        out_shape=(pltpu.SemaphoreType.DMA((N_CHIP,)), pltpu.SemaphoreType.DMA((N_CHIP,)), pltpu.HBM(g.shape, g.dtype),
                   pltpu.HBM(land.shape, land.dtype), jax.ShapeDtypeStruct((8, LANES), F32)),
        in_specs=[_HBM, _HBM], out_specs=(_SEM, _SEM, _HBM, _HBM, pl.BlockSpec(memory_space=pltpu.VMEM)),
        input_output_aliases={0: 2, 1: 3},
        compiler_params=pltpu.CompilerParams(has_side_effects=_EFFECT),
    )(_in_hbm(g), _in_hbm(land))
    return dict(send_sems=outs[0], recv_sems=outs[1], src=outs[2], land=outs[3], axis=axis), outs[4]


def _rs_sibling_wait(handle, after, *, name):
    src, land, axis = handle["src"], handle["land"], handle["axis"]
    size = src.shape[axis] // N_DEV

    def body(src_ref, dst_ref, send_sems, recv_sems, *rest):
        x, y, c, _ = _mesh_pos()
        for i in range(N_CHIP):
            cp = pltpu.make_async_remote_copy(src_ref=_block(src_ref, axis, 2 * i + (1 - c), size), dst_ref=dst_ref.at[i, 0],
                                              send_sem=send_sems.at[i], recv_sem=recv_sems.at[i], device_id=(x, y, 1 - c),
                                              device_id_type=MESH)
            cp.wait_send()
            cp.wait_recv()

    outs = pl.pallas_call(
        body, name=name,
        out_shape=(pltpu.HBM(src.shape, src.dtype), pltpu.HBM(land.shape, land.dtype)),
        in_specs=[_HBM, _HBM, _SEM, _SEM] + [_ANY] * len(after), out_specs=(_HBM, _HBM),
        input_output_aliases={0: 0, 1: 1},
        compiler_params=pltpu.CompilerParams(has_side_effects=_EFFECT),
    )(src, land, handle["send_sems"], handle["recv_sems"], *after)
    return outs[0], outs[1]


def _rs_chips_start(part, *, name):
    land = lax.empty(part.shape, part.dtype)

    def body(src, dst, send_sems, recv_sems, src_thru, dst_thru, token):
        x, y, c, chips = _mesh_pos()
        for j, chip in enumerate(chips):
            pltpu.make_async_remote_copy(src_ref=src.at[2 * chip[0] + chip[1]], dst_ref=dst.at[2 * x + y], send_sem=send_sems.at[j],
                                         recv_sem=recv_sems.at[j], device_id=(*chip, c), device_id_type=MESH).start()
        token[...] = jnp.zeros(token.shape, F32)

    outs = pl.pallas_call(
        body, name=name,
        out_shape=(pltpu.SemaphoreType.DMA((3,)), pltpu.SemaphoreType.DMA((3,)), pltpu.HBM(part.shape, part.dtype),
                   pltpu.HBM(land.shape, land.dtype), jax.ShapeDtypeStruct((8, LANES), F32)),
        in_specs=[_HBM, _HBM], out_specs=(_SEM, _SEM, _HBM, _HBM, pl.BlockSpec(memory_space=pltpu.VMEM)),
        input_output_aliases={0: 2, 1: 3},
        compiler_params=pltpu.CompilerParams(has_side_effects=_EFFECT),
    )(_in_hbm(part), _in_hbm(land))
    return dict(send_sems=outs[0], recv_sems=outs[1], src=outs[2], land=outs[3]), outs[4]


def _rs_chips_wait(handle, after, *, name):
    def body(src, dst, send_sems, recv_sems, *rest):
        x, y, c, chips = _mesh_pos()
        for j, chip in enumerate(chips):
            cp = pltpu.make_async_remote_copy(src_ref=src.at[2 * chip[0] + chip[1]], dst_ref=dst.at[2 * chip[0] + chip[1]],
                                              send_sem=send_sems.at[j], recv_sem=recv_sems.at[j], device_id=(*chip, c),
                                              device_id_type=MESH)
            cp.wait_send()
            cp.wait_recv()

    src, land = handle["src"], handle["land"]
    outs = pl.pallas_call(
        body, name=name,
        out_shape=(pltpu.HBM(src.shape, src.dtype), pltpu.HBM(land.shape, land.dtype)),
        in_specs=[_HBM, _HBM, _SEM, _SEM] + [_ANY] * len(after), out_specs=(_HBM, _HBM),
        input_output_aliases={0: 0, 1: 1},
        compiler_params=pltpu.CompilerParams(has_side_effects=_EFFECT),
    )(src, land, handle["send_sems"], handle["recv_sems"], *after)
    return outs[0], outs[1]


def _adam_math(w, g, m, v):
    m_new = ADAM_B1 * m + (1.0 - ADAM_B1) * g
    v_new = ADAM_B2 * v + (1.0 - ADAM_B2) * (g * g)
    m_hat = m_new / (1.0 - ADAM_B1 ** ADAM_STEP)
    v_hat = v_new / (1.0 - ADAM_B2 ** ADAM_STEP)
    delta = -ADAM_LR * (m_hat / (jnp.sqrt(v_hat) + ADAM_EPS) + ADAM_WD * w)
    return delta, m_new, v_new


def _sum_adam(part, land, slots, w, m, v, layer, prev, *, name, deps=()):
    n_layers, rows, cols = w.shape
    tr = _tile(rows, 256, _row_mult(land.dtype))
    n_prev = 0 if prev is None else 4

    def body(slots_ref, p0_ref, p1_ref, p2_ref, p3_ref, w_ref, m_ref, v_ref, *rest):
        g_ref, d_ref, mo_ref, vo_ref = rest[n_prev + len(deps):]
        g = p0_ref[...].astype(F32)
        for ref in (p1_ref, p2_ref, p3_ref):
            g = g + ref[...].astype(F32)
        delta, m_new, v_new = _adam_math(w_ref[...], g, m_ref[...], v_ref[...])
        g_ref[...] = g
        d_ref[...] = delta
        mo_ref[...] = m_new
        vo_ref[...] = v_new

    blk = pl.BlockSpec((None, tr, cols), lambda r, s: (layer, r, 0))
    slot = lambda j: pl.BlockSpec((None, None, tr, cols), lambda r, s: (s[j], 0, r, 0))
    shp = jax.ShapeDtypeStruct((n_layers, rows, cols), F32)
    grid_spec = pltpu.PrefetchScalarGridSpec(
        num_scalar_prefetch=1, grid=(rows // tr,),
        in_specs=[slot(0), slot(1), slot(2), slot(3), blk, blk, blk] + [_ANY] * (n_prev + len(deps)),
        out_specs=(blk, blk, blk, blk))
    return pl.pallas_call(
        body, name=name, out_shape=(shp, shp, shp, shp), grid_spec=grid_spec,
        input_output_aliases={8 + i: i for i in range(n_prev)},
        compiler_params=_params(("parallel",)),
    )(slots, part, land, land, land, w, m, v, *(prev or ()), *deps)


def _sum_slots(part, land, slots, *, name):
    _, _, rows, cols = land.shape
    tr = _tile(rows, 256, 8)

    def body(slots_ref, p0_ref, p1_ref, p2_ref, p3_ref, g_ref):
        g_ref[...] = p0_ref[...] + p1_ref[...] + p2_ref[...] + p3_ref[...]

    slot = lambda j: pl.BlockSpec((None, None, tr, cols), lambda r, s: (s[j], 0, r, 0))
    grid_spec = pltpu.PrefetchScalarGridSpec(
        num_scalar_prefetch=1, grid=(rows // tr,), in_specs=[slot(0), slot(1), slot(2), slot(3)],
        out_specs=pl.BlockSpec((tr, cols), lambda r, s: (r, 0)))
    return pl.pallas_call(body, name=name, out_shape=jax.ShapeDtypeStruct((rows, cols), F32), grid_spec=grid_spec,
                          compiler_params=_params(("parallel",)))(slots, part, land, land, land)


def _adam_flat(w, g, m, v, *, name):
    rows, cols = w.shape
    tr = _tile(rows, 256, 8)

    def body(w_ref, g_ref, m_ref, v_ref, d_ref, mo_ref, vo_ref):
        delta, m_new, v_new = _adam_math(w_ref[...], g_ref[...], m_ref[...], v_ref[...])
        d_ref[...] = delta
        mo_ref[...] = m_new
        vo_ref[...] = v_new

    blk = pl.BlockSpec((tr, cols), lambda r: (r, 0))
    shp = jax.ShapeDtypeStruct((rows, cols), F32)
    return pl.pallas_call(
        body, name=name, out_shape=(shp, shp, shp), grid=(rows // tr,), in_specs=[blk] * 4, out_specs=(blk, blk, blk),
        compiler_params=_params(("parallel",)),
    )(w, g, m, v)


_BIG = (("ffn1_w_in", 1), ("ffn1_w_out", 0), ("w_in", 1), ("w_out", 0), ("ffn2_w_in", 1), ("ffn2_w_out", 0))
_SMALL = ("meta_tokens", "ffn1_norm", "mix_norm", "conv_w", "conv_b", "lru_wa", "lru_ba", "lru_wx", "lru_bx", "lru_a_param",
          "pool_w", "pool_b", "pool_scale", "ffn2_norm", "final_norm")
_SMALL_SHARD_AXIS = {"meta_tokens": 1, "conv_w": 2, "pool_w": 2}
_PACK_COLS = 1024


def _pack(arrs):
    flat = jnp.concatenate([a.reshape(-1) for a in arrs])
    unit = N_DEV * 8 * _PACK_COLS
    total = -(-flat.shape[0] // unit) * unit
    flat = jnp.pad(flat, (0, total - flat.shape[0]))
    return flat.reshape(total // _PACK_COLS, _PACK_COLS)


def _unpack(packed, shapes):
    flat = packed.reshape(-1)
    out, off = [], 0
    for s in shapes:
        size = 1
        for v in s:
            size *= v
        out.append(flat[off:off + size].reshape(s))
        off += size
    return out


def _my_shard(full, axis, dev):
    size = full.shape[axis] // N_DEV
    return lax.dynamic_slice_in_dim(full, dev * size, size, axis)


def kernel(x, meta_tokens, ffn1_norm, ffn1_w_in, ffn1_w_out, mix_norm, w_in, conv_w, conv_b, lru_wa, lru_ba, lru_wx, lru_bx, lru_a_param, pool_w, pool_b, pool_scale, w_out, ffn2_norm, ffn2_w_in, ffn2_w_out, final_norm, loss_target, m_meta_tokens, m_ffn1_norm, m_ffn1_w_in, m_ffn1_w_out, m_mix_norm, m_w_in, m_conv_w, m_conv_b, m_lru_wa, m_lru_ba, m_lru_wx, m_lru_bx, m_lru_a_param, m_pool_w, m_pool_b, m_pool_scale, m_w_out, m_ffn2_norm, m_ffn2_w_in, m_ffn2_w_out, m_final_norm, v_meta_tokens, v_ffn1_norm, v_ffn1_w_in, v_ffn1_w_out, v_mix_norm, v_w_in, v_conv_w, v_conv_b, v_lru_wa, v_lru_ba, v_lru_wx, v_lru_bx, v_lru_a_param, v_pool_w, v_pool_b, v_pool_scale, v_w_out, v_ffn2_norm, v_ffn2_w_in, v_ffn2_w_out, v_final_norm):
    names = ("meta_tokens", "ffn1_norm", "ffn1_w_in", "ffn1_w_out", "mix_norm", "w_in", "conv_w", "conv_b", "lru_wa", "lru_ba",
             "lru_wx", "lru_bx", "lru_a_param", "pool_w", "pool_b", "pool_scale", "w_out", "ffn2_norm", "ffn2_w_in", "ffn2_w_out",
             "final_norm")
    w = dict(zip(names, (meta_tokens, ffn1_norm, ffn1_w_in, ffn1_w_out, mix_norm, w_in, conv_w, conv_b, lru_wa, lru_ba, lru_wx,
                         lru_bx, lru_a_param, pool_w, pool_b, pool_scale, w_out, ffn2_norm, ffn2_w_in, ffn2_w_out, final_norm)))
    mom = dict(zip(names, (m_meta_tokens, m_ffn1_norm, m_ffn1_w_in, m_ffn1_w_out, m_mix_norm, m_w_in, m_conv_w, m_conv_b, m_lru_wa,
                           m_lru_ba, m_lru_wx, m_lru_bx, m_lru_a_param, m_pool_w, m_pool_b, m_pool_scale, m_w_out, m_ffn2_norm,
                           m_ffn2_w_in, m_ffn2_w_out, m_final_norm)))
    vel = dict(zip(names, (v_meta_tokens, v_ffn1_norm, v_ffn1_w_in, v_ffn1_w_out, v_mix_norm, v_w_in, v_conv_w, v_conv_b, v_lru_wa,
                           v_lru_ba, v_lru_wx, v_lru_bx, v_lru_a_param, v_pool_w, v_pool_b, v_pool_scale, v_w_out, v_ffn2_norm,
                           v_ffn2_w_in, v_ffn2_w_out, v_final_norm)))
    n_layers, d = ffn1_norm.shape
    n_meta = meta_tokens.shape[0]
    seq = x.shape[1]
    t_valid = n_meta + seq
    tp = -(-t_valid // ROW_ALIGN) * ROW_ALIGN
    dev = 4 * lax.axis_index("x") + 2 * lax.axis_index("y") + lax.axis_index("c")
    my_c = lax.axis_index("c").astype(jnp.int32).reshape(1)
    dev1 = dev.astype(jnp.int32).reshape(1)
    mx, my = lax.axis_index("x"), lax.axis_index("y")
    slots = jnp.stack([2 * mx + my, 2 * (1 - mx) + my, 2 * mx + (1 - my), 2 * (1 - mx) + (1 - my)]).astype(jnp.int32)

    big_axis = dict(_BIG)
    vec = lambda a: a.reshape(a.shape[0], 1, a.shape[1])
    norms = dict(ffn1=vec(ffn1_norm), mix=vec(mix_norm), ffn2=vec(ffn2_norm))
    units = [(kind, l) for l in range(n_layers) for kind in ("ffn1", "mix", "ffn2")]

    handles = {}
    placed = {}
    state = dict(token=None, started=0, first=None)

    def behind(a):
        return a if state["first"] is None else lax.optimization_barrier((state["first"], a))[1]

    def shard(k, l):
        return behind(w[k][l:l + 1]).astype(BF16), big_axis[k] + 1

    groups = [[shard("ffn1_w_in", 0), (meta_tokens, 1)], [shard("ffn1_w_out", 0)]]

    def order():
        return [] if state["token"] is None else [state["token"]]

    def start_next():
        i = state["started"]
        if i < len(groups):
            handles[i], state["token"] = _gather_start([s for s, _ in groups[i]], [ax for _, ax in groups[i]], order(),
                                                       name=f"gather{i}_start", lands=placed.get(i))
            state["started"] = i + 1

    def finish_begin(i, after):
        handle = handles.pop(i)
        srcs, fulls = _gather_wait(handle, list(after) + order(), name=f"gather{i}_wait")
        if i not in placed:
            fulls = [_place_own(s, z, ax, dev1, name=f"gather{i}_own{a}") for a, (s, z, ax) in enumerate(zip(srcs, fulls, handle["axes"]))]
        passing, state["token"] = _gather_pass_start(fulls, handle["axes"], name=f"gather{i}_pass")
        return i, passing

    def finish_end(ctx, after):
        i, passing = ctx
        fulls = _gather_pass_wait(passing, after, name=f"gather{i}_passwait")
        start_next()
        return fulls

    start_next()
    start_next()
    state["first"] = state["token"]
    for kind, l in units[1:]:
        if kind == "mix":
            groups.append([shard("w_in", l), shard("w_out", l)] + ([(behind(conv_w), 2), (behind(pool_w), 2)] if l == 0 else []))
        else:
            groups.append([shard(f"{kind}_w_in", l), shard(f"{kind}_w_out", l)])
    for i in range(2, len(groups)):
        placed[i] = [_place_own(s, lax.empty(s.shape[:ax] + (N_DEV * s.shape[ax],) + s.shape[ax + 1:], s.dtype), ax, dev1,
                                name=f"gather{i}_own{a}") for a, (s, ax) in enumerate(groups[i])]
    pad = jnp.zeros((tp - t_valid, d), F32)
    tgt = jnp.concatenate([jnp.zeros((n_meta, d), F32), behind(loss_target[0]), pad], axis=0)
    h0_base = jnp.concatenate([jnp.zeros((n_meta, d), F32), behind(x[0]), pad], axis=0)
    packed_state = [_pack([behind(t[k]) for k in _SMALL]) for t in (w, mom, vel)]
    ctx = finish_begin(0, [tgt, h0_base] + packed_state + [z for i in placed for z in placed[i]])
    w_in0, meta_full = finish_end(ctx, [])
    h0 = lax.dynamic_update_slice(h0_base, meta_full, (0, 0))
    n_act = _rms_fwd(h0, norms["ffn1"], 0, name="l0ffn1_rms", deps=order())
    gu, act = _ffn_in_fwd(n_act, w_in0, 0, name="l0ffn1_in")
    (w_out0,) = finish_end(finish_begin(1, [act]), [])
    h = _ffn_fwd_out(h0, act, w_out0, "l0ffn1", order())
    gathered = {units[0]: (w_in0, w_out0)}
    saved = {units[0]: (h0, n_act, gu, act)}
    mp = None
    for ui, (kind, l) in enumerate(units[1:], start=1):
        tag = f"l{l}{kind}"
        ctx = finish_begin(ui + 1, [h])
        n_act = _rms_fwd(h, norms[kind], l, name=f"{tag}_rms", deps=order())
        fulls = finish_end(ctx, [n_act])
        if kind == "mix":
            if l == 0:
                mp = dict(conv_w=fulls[2], conv_b=vec(conv_b), wa=lru_wa.astype(BF16), ba=vec(lru_ba), wx=lru_wx.astype(BF16),
                          bx=vec(lru_bx), a_param=vec(lru_a_param), pool_w=fulls[3].astype(BF16), pool_b=vec(pool_b),
                          pool_scale=vec(pool_scale))
            h, saved[(kind, l)] = _mix_block_fwd(h, n_act, l, fulls[0], fulls[1], mp, tag, order())
        else:
            gu, act = _ffn_in_fwd(n_act, fulls[0], 0, name=f"{tag}_in", deps=order())
            saved[(kind, l)] = (h, n_act, gu, act)
            h = _ffn_fwd_out(h, act, fulls[1], tag, [])
        gathered[(kind, l)] = (fulls[0], fulls[1])
    dh, dhb, g_final, loss_local = _final_loss(h, final_norm.reshape(1, d), tgt, n_meta, t_valid, name="final_loss")
    loss = lax.psum(loss_local[0, 0], ("x", "y", "c"))

    pending = []
    big_out = {}

    landed = []

    def drain(after):
        while pending:
            k, l, handle = pending.pop(0)
            landed.append((k, l) + _rs_chips_wait(handle, after, name=f"rs_{k}{l}_wait"))

    def adams(items, deps):
        for k, l, part, land in items:
            big_out[k] = _sum_adam(part, land, slots, w[k], mom[k], vel[k], l, big_out.get(k), name=f"adam_{k}{l}", deps=deps)
        return [big_out[items[-1][0]][0]] if items else []

    sib_pending = []

    def to_chips(k, l, g, land):
        part = _rs_add_sibling([g], big_axis[k], land, my_c, name=f"rs_{k}{l}_add")
        handle, token = _rs_chips_start(part, name=f"rs_{k}{l}_start")
        pending.append((k, l, handle))
        return token

    def finish_sibling(after):
        token = None
        while sib_pending:
            k, l, handle = sib_pending.pop(0)
            g, land = _rs_sibling_wait(handle, after, name=f"rs_{k}{l}_sibwait")
            token = to_chips(k, l, g, land)
        return token

    def emitter(kind, l):
        last_unit = (kind, l) == units[0]

        def emit(which, g):
            if which == "w_out":
                drain([g])
            token = finish_sibling([g])
            order_after = [] if token is None else [token]
            if which == "flush":
                return order_after
            k = which if kind == "mix" else f"{kind}_{which}"
            if last_unit and which == "w_in":
                (land,) = _rs_exchange_sibling([[g]], [big_axis[k]], name=f"rs_{k}{l}_sib")
                return order_after + [to_chips(k, l, g, land)]
            handle, token = _rs_sibling_start(g, big_axis[k], name=f"rs_{k}{l}_sibstart")
            sib_pending.append((k, l, handle))
            return order_after + [token]
        return emit

    g_norm = {k: [None] * n_layers for k in ("ffn1", "mix", "ffn2")}
    g_mix = [None] * n_layers
    for kind, l in reversed(units):
        w_i, w_o = gathered[(kind, l)]
        tag = f"l{l}{kind}"
        if kind == "mix":
            dh, dhb, g_norm[kind][l], g_mix[l] = _mix_block_bwd(dh, dhb, saved[(kind, l)], norms[kind], l, w_i, w_o, mp, tag,
                                                                 emitter(kind, l), [])
        else:
            dh, dhb, g_norm[kind][l] = _ffn_bwd(dh, dhb, saved[(kind, l)], norms[kind], l, w_i, w_o, tag, emitter(kind, l), [])
    drain([dh])
    dh0 = dh
    grad_x = dh0[n_meta:t_valid][None]

    mix_key = dict(conv_w="conv_w", conv_b="conv_b", lru_wa="wa", lru_ba="ba", lru_wx="wx", lru_bx="bx", lru_a_param="a_param",
                   pool_w="pool_w", pool_b="pool_b", pool_scale="pool_scale")
    small_local = {"meta_tokens": dh0[:n_meta], "final_norm": g_final.reshape(d)}
    for k in ("ffn1", "mix", "ffn2"):
        small_local[f"{k}_norm"] = jnp.stack([g.reshape(d) for g in g_norm[k]])
    for k, mk in mix_key.items():
        small_local[k] = jnp.stack([g_mix[l][mk] for l in range(n_layers)]).reshape(
            w[k].shape if k not in _SMALL_SHARD_AXIS else small_shape_full(w[k], _SMALL_SHARD_AXIS[k]))
    small_shapes = [small_local[k].shape for k in _SMALL]
    packed = _pack([small_local[k] for k in _SMALL])
    (land,) = _rs_exchange_sibling([[packed]], [0], name="rs_small_sib")
    part = _rs_add_sibling([packed], 0, land, my_c, name="rs_small_add")
    cut = max(len(landed) - 4, 0)
    handle, token = _rs_chips_start(part, name="rs_small_start")
    done = adams(landed[:cut], [token])
    part, land = _rs_chips_wait(handle, done, name="rs_small_wait")
    small_block = _sum_slots(part, land, slots, name="rs_sum_small")
    handle, token = _gather_start([small_block], [0], [], name="gather_small_start")
    done = adams(landed[cut:cut + 2], [token])
    (src,), (small_full,) = _gather_wait(handle, done, name="gather_small_wait")
    small_full = _place_own(src, small_full, 0, dev1, name="gather_small_own")
    handle, token = _gather_pass_start([small_full], [0], name="gather_small_pass")
    done = adams(landed[cut + 2:], [token])
    (small_full,) = _gather_pass_wait(handle, done, name="gather_small_passwait")

    out_g, out_d, out_m, out_v = {}, {}, {}, {}
    for k, _ in _BIG:
        out_g[k], out_d[k], out_m[k], out_v[k] = big_out[k]
    small_g = dict(zip(_SMALL, _unpack(small_full, small_shapes)))
    for k, ax in _SMALL_SHARD_AXIS.items():
        small_g[k] = _my_shard(small_g[k], ax, dev)
    shapes_local = [w[k].shape for k in _SMALL]
    d_p, m_p, v_p = _adam_flat(packed_state[0], _pack([small_g[k] for k in _SMALL]), packed_state[1], packed_state[2],
                               name="adam_small")
    for k, dd, mm, vv in zip(_SMALL, _unpack(d_p, shapes_local), _unpack(m_p, shapes_local), _unpack(v_p, shapes_local)):
        out_g[k], out_d[k], out_m[k], out_v[k] = small_g[k], dd, mm, vv

    return (loss, grad_x, *[out_g[k] for k in names], *[out_d[k] for k in names], *[out_m[k] for k in names],
            *[out_v[k] for k in names])


def small_shape_full(w_shard, axis):
    return w_shard.shape[:axis] + (N_DEV * w_shard.shape[axis],) + w_shard.shape[axis + 1:]
```

```python
import jax
import jax.numpy as jnp
from jax import lax
from jax.experimental import pallas as pl
from jax.experimental.pallas import tpu as pltpu

F32 = jnp.float32
BF16 = jnp.bfloat16
MESH = pl.DeviceIdType.MESH

N_DEV = 8
N_CHIP = 4
RMS_EPS = 1e-6
LRU_C = 8.0
CONV_WIDTH = 4
POOL_WINDOWS = (2, 4, 8, 16)
HIST = 16
ADAM_LR = 0.001
ADAM_B1 = 0.9
ADAM_B2 = 0.999
ADAM_EPS = 1e-08
ADAM_WD = 0.01
ADAM_STEP = 10
ROW_ALIGN = 128
V7X_VMEM_LIMIT = 56 * 1024 * 1024
BF16_ROWS = 16
K_TILE = 5632
DN_K_TILE = 2816
WGRAD_TILE = 2048
SCAN_UNROLL = 8
EPILOGUE_ROWS = 528
LANES = 128


def _tile(n, cap, mult=BF16_ROWS):
    best = None
    d = mult
    while d <= min(n, cap):
        if n % d == 0:
            best = d
        d += mult
    if best is None:
        raise ValueError(f"no tile for {n} (cap {cap}, multiple of {mult})")
    return best


def _row_mult(dtype):
    return 8 * 4 // jnp.dtype(dtype).itemsize


def _params(sem=None):
    return pltpu.CompilerParams(dimension_semantics=sem, vmem_limit_bytes=V7X_VMEM_LIMIT)


def _dot(a, b, mode):
    dims = {"nn": ((1,), (0,)), "nt": ((1,), (1,)), "tn": ((0,), (0,))}[mode]
    return lax.dot_general(a, b, (dims, ((), ())), preferred_element_type=F32)


def _sigmoid(x):
    return 1.0 / (1.0 + jnp.exp(-x))


_GELU_C = 0.7978845608028654
_GELU_K = 0.044715


def _gelu(x):
    return 0.5 * x * (1.0 + jnp.tanh(_GELU_C * (x + _GELU_K * x * x * x)))


def _gelu_and_grad(x):
    x2 = x * x
    th = jnp.tanh(_GELU_C * x * (1.0 + _GELU_K * x2))
    half = 0.5 * (1.0 + th)
    return x * half, half + 0.5 * x * (1.0 - th * th) * _GELU_C * (1.0 + 3.0 * _GELU_K * x2)


def _neg_expm1(x):
    p = 1.0 / 5040.0
    for c in (1.0 / 720.0, 1.0 / 120.0, 1.0 / 24.0, 1.0 / 6.0, 0.5, 1.0):
        p = p * x + c
    return jnp.where(x > -0.25, -(x * p), 1.0 - jnp.exp(x))


def _softplus_neg(p):
    e = jnp.exp(-jnp.abs(p))
    u = 1.0 + e
    l1p = jnp.where(u == 1.0, e, jnp.log(u) * e / (u - 1.0 + (u == 1.0).astype(F32)))
    return jnp.maximum(-p, 0.0) + l1p


def _operand_spec(arr, br, bc, ridx, cidx, layer, split):
    if split:
        ncb = arr.shape[2] // bc
        return pl.BlockSpec((None, br, bc), lambda i, j, k: (cidx(i, j, k) // ncb, ridx(i, j, k), cidx(i, j, k) % ncb))
    if layer is not None:
        return pl.BlockSpec((None, br, bc), lambda i, j, k: (layer, ridx(i, j, k), cidx(i, j, k)))
    return pl.BlockSpec((br, bc), lambda i, j, k: (ridx(i, j, k), cidx(i, j, k)))


def _matmul(a, b, *, mode, m, n, kdim, tm, tn, tk, out_dtype, name, scale=None, residual=None,
            b_layer=None, a_split=False, b_split=False, deps=()):
    nk = kdim // tk
    I = lambda i, j, k: i
    J = lambda i, j, k: j
    K = lambda i, j, k: k
    if mode == "nn":
        a_spec = _operand_spec(a, tm, tk, I, K, None, a_split)
        b_spec = _operand_spec(b, tk, tn, K, J, b_layer, b_split)
    elif mode == "nt":
        a_spec = _operand_spec(a, tm, tk, I, K, None, a_split)
        b_spec = _operand_spec(b, tn, tk, J, K, b_layer, b_split)
    else:
        a_spec = _operand_spec(a, tk, tm, K, I, None, a_split)
        b_spec = _operand_spec(b, tk, tn, K, J, b_layer, b_split)
    in_specs = [a_spec, b_spec]
    operands = [a, b]
    if residual is not None:
        in_specs.append(pl.BlockSpec((tm, tn), lambda i, j, k: (i, j)))
        operands.append(residual)
    in_specs += [pl.BlockSpec(memory_space=pl.ANY)] * len(deps)
    operands += list(deps)
    n_in = len(operands)

    def finish(acc, res_ref, o_ref):
        if scale is not None:
            acc = acc * scale
        if res_ref is not None:
            acc = acc + res_ref[...]
        o_ref[...] = acc.astype(o_ref.dtype)

    def body(*refs):
        a_ref, b_ref = refs[0], refs[1]
        res_ref = refs[2] if residual is not None else None
        o_ref = refs[n_in]
        part = _dot(a_ref[...], b_ref[...], mode)
        if nk == 1:
            finish(part, res_ref, o_ref)
        else:
            acc_ref = refs[-1]
            k = pl.program_id(2)

            @pl.when(k == 0)
            def _():
                acc_ref[...] = part

            @pl.when(jnp.logical_and(k > 0, k < nk - 1))
            def _():
                acc_ref[...] += part

            @pl.when(k == nk - 1)
            def _():
                finish(acc_ref[...] + part, res_ref, o_ref)

    return pl.pallas_call(
        body, name=name,
        out_shape=jax.ShapeDtypeStruct((m, n), out_dtype),
        grid=(m // tm, n // tn, nk),
        in_specs=in_specs,
        out_specs=pl.BlockSpec((tm, tn), lambda i, j, k: (i, j)),
        scratch_shapes=[pltpu.VMEM((tm, tn), F32)] if nk > 1 else [],
        compiler_params=_params(("parallel", "parallel", "arbitrary")),
    )(*operands)


def _ffn_in_fwd(n_act, w_in, layer, *, name, deps=()):
    tp, d = n_act.shape
    f = w_in.shape[2] // 2
    tm = _tile(tp, 2112)
    tn = _tile(f, 512, LANES)
    nj = f // tn

    ch = _tile(tm, EPILOGUE_ROWS)

    def body(n_ref, wg_ref, wu_ref, *rest):
        da_ref, a_ref = rest[len(deps):]
        for r in range(tm // ch):
            rows = pl.ds(r * ch, ch)
            x = n_ref[rows, :]
            g = _dot(x, wg_ref[...], "nn")
            u = _dot(x, wu_ref[...], "nn")
            s = 0.5 * jnp.tanh(0.5 * g) + 0.5
            sg = g * s
            da_ref[0, rows, :] = (u * s * (1.0 + g * (1.0 - s))).astype(BF16)
            da_ref[1, rows, :] = sg.astype(BF16)
            a_ref[rows, :] = (sg * u).astype(BF16)

    return pl.pallas_call(
        body, name=name,
        out_shape=(jax.ShapeDtypeStruct((2, tp, f), BF16), jax.ShapeDtypeStruct((tp, f), BF16)),
        grid=(tp // tm, nj),
        in_specs=[pl.BlockSpec((tm, d), lambda i, j: (i, 0)),
                  pl.BlockSpec((None, d, tn), lambda i, j: (layer, 0, j)),
                  pl.BlockSpec((None, d, tn), lambda i, j: (layer, 0, j + nj))] + [pl.BlockSpec(memory_space=pl.ANY)] * len(deps),
        out_specs=(pl.BlockSpec((2, tm, tn), lambda i, j: (0, i, j)), pl.BlockSpec((tm, tn), lambda i, j: (i, j))),
        compiler_params=_params(("parallel", "arbitrary")),
    )(n_act, w_in, w_in, *deps)


def _ffn_bwd_act(dhb, w_out, layer, gu, *, name, deps=()):
    tp, d = dhb.shape
    f = w_out.shape[1]
    tm = _tile(tp, 2112)
    tn = _tile(f, 512, LANES)

    ch = _tile(tm, EPILOGUE_ROWS)

    def body(dh_ref, w_ref, gu_ref, *rest):
        dz_ref = rest[len(deps)]
        for r in range(tm // ch):
            rows = pl.ds(r * ch, ch)
            da = 0.5 * _dot(dh_ref[rows, :], w_ref[...], "nt")
            dz_ref[0, rows, :] = (da * gu_ref[0, rows, :].astype(F32)).astype(BF16)
            dz_ref[1, rows, :] = (da * gu_ref[1, rows, :].astype(F32)).astype(BF16)

    return pl.pallas_call(
        body, name=name,
        out_shape=jax.ShapeDtypeStruct((2, tp, f), BF16),
        grid=(tp // tm, f // tn),
        in_specs=[pl.BlockSpec((tm, d), lambda i, j: (i, 0)),
                  pl.BlockSpec((None, tn, d), lambda i, j: (layer, j, 0)),
                  pl.BlockSpec((2, tm, tn), lambda i, j: (0, i, j))] + [pl.BlockSpec(memory_space=pl.ANY)] * len(deps),
        out_specs=pl.BlockSpec((2, tm, tn), lambda i, j: (0, i, j)),
        compiler_params=_params(("parallel", "arbitrary")),
    )(dhb, w_out, gu, *deps)


def _rms_fwd(h, g, layer, *, name, deps=()):
    tp, d = h.shape
    tr = _tile(tp, 528)

    def body(h_ref, g_ref, *rest):
        n_ref = rest[len(deps)]
        x = h_ref[...]
        r = lax.rsqrt(jnp.mean(x * x, axis=-1, keepdims=True) + RMS_EPS)
        n_ref[...] = (x * r * g_ref[...]).astype(BF16)

    return pl.pallas_call(
        body, name=name, out_shape=jax.ShapeDtypeStruct((tp, d), BF16), grid=(tp // tr,),
        in_specs=[pl.BlockSpec((tr, d), lambda i: (i, 0)), pl.BlockSpec((None, 1, d), lambda i: (layer, 0, 0))]
        + [pl.BlockSpec(memory_space=pl.ANY)] * len(deps),
        out_specs=pl.BlockSpec((tr, d), lambda i: (i, 0)),
        compiler_params=_params(("parallel",)),
    )(h, g, *deps)


def _rms_bwd(h, g, layer, dn, dres, *, name, deps=()):
    tp, d = h.shape
    tr = _tile(tp, 528)

    def body(h_ref, g_ref, dn_ref, dres_ref, *rest):
        dh_ref, dhb_ref, dg_ref = rest[len(deps):]
        x = h_ref[...]
        r = lax.rsqrt(jnp.mean(x * x, axis=-1, keepdims=True) + RMS_EPS)
        xhat = x * r
        dn_v = dn_ref[...]
        dxhat = dn_v * g_ref[...]
        dh = dres_ref[...] + r * (dxhat - xhat * jnp.mean(dxhat * xhat, axis=-1, keepdims=True))
        dh_ref[...] = dh
        dhb_ref[...] = dh.astype(BF16)
        part = jnp.sum(dn_v * xhat, axis=0, keepdims=True)

        @pl.when(pl.program_id(0) == 0)
        def _():
            dg_ref[...] = part

        @pl.when(pl.program_id(0) > 0)
        def _():
            dg_ref[...] += part

    return pl.pallas_call(
        body, name=name,
        out_shape=(jax.ShapeDtypeStruct((tp, d), F32), jax.ShapeDtypeStruct((tp, d), BF16), jax.ShapeDtypeStruct((1, d), F32)),
        grid=(tp // tr,),
        in_specs=[pl.BlockSpec((tr, d), lambda i: (i, 0)), pl.BlockSpec((None, 1, d), lambda i: (layer, 0, 0)),
                  pl.BlockSpec((tr, d), lambda i: (i, 0)), pl.BlockSpec((tr, d), lambda i: (i, 0))]
        + [pl.BlockSpec(memory_space=pl.ANY)] * len(deps),
        out_specs=(pl.BlockSpec((tr, d), lambda i: (i, 0)), pl.BlockSpec((tr, d), lambda i: (i, 0)),
                   pl.BlockSpec((1, d), lambda i: (0, 0))),
        compiler_params=_params(("arbitrary",)),
    )(h, g, dn, dres, *deps)


def _final_loss(h, g, tgt, n_meta, t_valid, *, name):
    tp, d = h.shape
    tr = _tile(tp, 528)

    def body(h_ref, g_ref, t_ref, dh_ref, dhb_ref, dg_ref, loss_ref):
        i = pl.program_id(0)
        x = h_ref[...]
        r = lax.rsqrt(jnp.mean(x * x, axis=-1, keepdims=True) + RMS_EPS)
        xhat = x * r
        gv = g_ref[...]
        row = i * tr + lax.broadcasted_iota(jnp.int32, (tr, 1), 0)
        valid = jnp.logical_and(row >= n_meta, row < t_valid)
        err = jnp.where(valid, xhat * gv - t_ref[...], 0.0)
        dy = err * (1.0 / d)
        dxhat = dy * gv
        dh = r * (dxhat - xhat * jnp.mean(dxhat * xhat, axis=-1, keepdims=True))
        dh_ref[...] = dh
        dhb_ref[...] = dh.astype(BF16)
        dg_part = jnp.sum(dy * xhat, axis=0, keepdims=True)
        loss_part = jnp.sum(jnp.sum(err * err, axis=1, keepdims=True), axis=0, keepdims=True) * (0.5 / d)

        @pl.when(i == 0)
        def _():
            dg_ref[...] = dg_part
            loss_ref[...] = loss_part

        @pl.when(i > 0)
        def _():
            dg_ref[...] += dg_part
            loss_ref[...] += loss_part

    return pl.pallas_call(
        body, name=name,
        out_shape=(jax.ShapeDtypeStruct((tp, d), F32), jax.ShapeDtypeStruct((tp, d), BF16),
                   jax.ShapeDtypeStruct((1, d), F32), jax.ShapeDtypeStruct((1, 1), F32)),
        grid=(tp // tr,),
        in_specs=[pl.BlockSpec((tr, d), lambda i: (i, 0)), pl.BlockSpec((1, d), lambda i: (0, 0)),
                  pl.BlockSpec((tr, d), lambda i: (i, 0))],
        out_specs=(pl.BlockSpec((tr, d), lambda i: (i, 0)), pl.BlockSpec((tr, d), lambda i: (i, 0)),
                   pl.BlockSpec((1, d), lambda i: (0, 0)), pl.BlockSpec((1, 1), lambda i: (0, 0))),
        compiler_params=_params(("arbitrary",)),
    )(h, g, tgt)


_MIX_PARAMS = ("conv_w", "conv_b", "wa", "ba", "wx", "bx", "a_param", "pool_w", "pool_b", "pool_scale")


def _mix_param_specs(mp, layer):
    def spec(a):
        return pl.BlockSpec((None,) + a.shape[1:], lambda *g: (layer,) + (0,) * (a.ndim - 1))
    return [spec(mp[k]) for k in _MIX_PARAMS]


def _mix_param_list(mp):
    return [mp[k] for k in _MIX_PARAMS]


def _lru_gates(xc, wa_h, ba_h, wx_h, bx_h, sp_h):
    xb = xc.astype(BF16)
    ra = _sigmoid(_dot(xb, wa_h, "nn") + ba_h)
    ii = _sigmoid(_dot(xb, wx_h, "nn") + bx_h)
    la = -LRU_C * ra * sp_h
    return ra, ii, la, jnp.exp(la)


def _shifted(x, k):
    return x if k == 0 else pltpu.roll(x, k % x.shape[0], axis=0)


def _conv_taps(xbuf, sl, tc):
    x = xbuf[:, sl]
    return [_shifted(x, CONV_WIDTH - 1 - k)[HIST:HIST + tc] for k in range(CONV_WIDTH)]


def _conv_fwd(taps, cw_ref, cb_ref, sl):
    xc = cb_ref[:, sl]
    for k in range(CONV_WIDTH):
        xc = xc + cw_ref[k:k + 1, sl] * taps[k]
    return xc


def _window_sum(x, win, direction):
    s = x
    step = 1
    while step < win:
        s = s + _shifted(s, direction * step)
        step *= 2
    return s


def _pool_delta(pbuf, cols, win, t0, tc):
    x = pbuf[:, cols]
    u = x[HIST:HIST + tc]
    s = _window_sum(x, win, 1)[HIST:HIST + tc]
    t = t0 + lax.broadcasted_iota(jnp.int32, (tc, 1), 0)
    inv_cnt = 1.0 / jnp.minimum(t + 1, win).astype(F32)
    return s * inv_cnt - u, inv_cnt


def _mix_fwd(z, mp, layer, *, name):
    tp = z.shape[0]
    dl = z.shape[1] // 3
    n_heads, hd = mp["wa"].shape[1], mp["wa"].shape[2]
    n_groups, gd = mp["pool_w"].shape[1], mp["pool_w"].shape[2]
    tc = _tile(tp, 384)

    def body(z_ref, cw_ref, cb_ref, wa_ref, ba_ref, wx_ref, bx_ref, ap_ref, pw_ref, pb_ref, ps_ref,
             m_ref, hs_ref, aux_ref, xbuf, pbuf, a_s, b_s, hcar):
        i = pl.program_id(0)

        @pl.when(i == 0)
        def _():
            xbuf[pl.ds(0, HIST), :] = jnp.zeros((HIST, dl), F32)
            pbuf[pl.ds(0, HIST), :] = jnp.zeros((HIST, dl), F32)
            hcar[...] = jnp.zeros((1, dl), F32)

        @pl.when(i > 0)
        def _():
            xbuf[pl.ds(0, HIST), :] = xbuf[pl.ds(tc, HIST), :]
            pbuf[pl.ds(0, HIST), :] = pbuf[pl.ds(tc, HIST), :]

        xbuf[pl.ds(HIST, tc), :] = z_ref[:, pl.ds(0, dl)]
        pbuf[pl.ds(HIST, tc), :] = z_ref[:, pl.ds(2 * dl, dl)]
        sp = _softplus_neg(ap_ref[...])
        for h in range(n_heads):
            sl = pl.ds(h * hd, hd)
            xc = _conv_fwd(_conv_taps(xbuf, sl, tc), cw_ref, cb_ref, sl)
            ra, ii, la, a = _lru_gates(xc, wa_ref[h], ba_ref[:, sl], wx_ref[h], bx_ref[:, sl], sp[:, h * hd:(h + 1) * hd])
            e = _neg_expm1(2.0 * la)
            a_s[:, sl] = a
            b_s[:, sl] = jnp.sqrt(e) * ii * xc
            for slot, val in ((_AUX_R, ra), (_AUX_I, ii), (_AUX_X, xc), (_AUX_E, e)):
                aux_ref[slot, :, sl] = val

        def step(t, hprev):
            hnew = a_s[pl.ds(t, 1), :] * hprev + b_s[pl.ds(t, 1), :]
            hs_ref[pl.ds(t, 1), :] = hnew
            return hnew

        hcar[...] = lax.fori_loop(0, tc, step, hcar[...], unroll=SCAN_UNROLL)
        for h in range(n_heads):
            sl = pl.ds(h * hd, hd)
            m_ref[:, sl] = (hs_ref[:, sl] * _gelu(z_ref[:, pl.ds(dl + h * hd, hd)])).astype(BF16)
        for g in range(n_groups):
            cols = pl.ds(g * gd, gd)
            dlt, _ = _pool_delta(pbuf, cols, POOL_WINDOWS[g], i * tc, tc)
            q = _dot(dlt.astype(BF16), pw_ref[g], "nn") + pb_ref[:, cols]
            m_ref[:, pl.ds(dl + g * gd, gd)] = (q * ps_ref[:, cols]).astype(BF16)

    return pl.pallas_call(
        body, name=name,
        out_shape=(jax.ShapeDtypeStruct((tp, 2 * dl), BF16), jax.ShapeDtypeStruct((tp, dl), F32),
                   jax.ShapeDtypeStruct((_AUX_N, tp, dl), F32)),
        grid=(tp // tc,),
        in_specs=[pl.BlockSpec((tc, 3 * dl), lambda i: (i, 0))] + _mix_param_specs(mp, layer),
        out_specs=(pl.BlockSpec((tc, 2 * dl), lambda i: (i, 0)), pl.BlockSpec((tc, dl), lambda i: (i, 0)),
                   pl.BlockSpec((_AUX_N, tc, dl), lambda i: (0, i, 0))),
        scratch_shapes=[pltpu.VMEM((HIST + tc, dl), F32), pltpu.VMEM((HIST + tc, dl), F32),
                        pltpu.VMEM((tc, dl), F32), pltpu.VMEM((tc, dl), F32), pltpu.VMEM((1, dl), F32)],
        compiler_params=_params(("arbitrary",)),
    )(z, *_mix_param_list(mp))


_MIX_GRADS = ("conv_w", "conv_b", "wa", "ba", "wx", "bx", "a_param", "pool_w", "pool_b", "pool_scale")
_AUX_R, _AUX_I, _AUX_X, _AUX_E, _AUX_N = 0, 1, 2, 3, 4


def _mix_bwd(z, hs, aux, dm, mp, layer, *, name):
    tp = z.shape[0]
    dl = z.shape[1] // 3
    n_heads, hd = mp["wa"].shape[1], mp["wa"].shape[2]
    n_groups, gd = mp["pool_w"].shape[1], mp["pool_w"].shape[2]
    tc = _tile(tp, 352)
    nc = tp // tc
    per = tc // HIST

    def body(z_ref, zp_ref, hs_ref, hsp_ref, aux_ref, dm_ref, cw_ref, cb_ref, wa_ref, ba_ref, wx_ref, bx_ref, ap_ref, pw_ref, pb_ref,
             ps_ref, dz_ref, dcw_ref, dcb_ref, dwa_ref, dba_ref, dwx_ref, dbx_ref, dap_ref, dpw_ref, dpb_ref, dps_ref,
             xbuf, pbuf, hbuf, dxbuf, ddbuf, a_s, lam_s, ccar):
        i = pl.program_id(0)
        ci = nc - 1 - i

        @pl.when(i == 0)
        def _():
            dxbuf[pl.ds(tc, HIST), :] = jnp.zeros((HIST, dl), F32)
            ddbuf[pl.ds(tc, HIST), :] = jnp.zeros((HIST, dl), F32)
            ccar[...] = jnp.zeros((1, dl), F32)
            for ref in (dcw_ref, dcb_ref, dwa_ref, dba_ref, dwx_ref, dbx_ref, dap_ref, dpw_ref, dpb_ref, dps_ref):
                ref[...] = jnp.zeros(ref.shape, F32)

        @pl.when(ci == 0)
        def _():
            xbuf[pl.ds(0, HIST), :] = jnp.zeros((HIST, dl), F32)
            pbuf[pl.ds(0, HIST), :] = jnp.zeros((HIST, dl), F32)
            hbuf[pl.ds(0, HIST), :] = jnp.zeros((HIST, dl), F32)

        @pl.when(ci > 0)
        def _():
            xbuf[pl.ds(0, HIST), :] = zp_ref[:, pl.ds(0, dl)]
            pbuf[pl.ds(0, HIST), :] = zp_ref[:, pl.ds(2 * dl, dl)]
            hbuf[pl.ds(0, HIST), :] = hsp_ref[...]

        xbuf[pl.ds(HIST, tc), :] = z_ref[:, pl.ds(0, dl)]
        pbuf[pl.ds(HIST, tc), :] = z_ref[:, pl.ds(2 * dl, dl)]
        hbuf[pl.ds(HIST, tc), :] = hs_ref[...]
        sp = _softplus_neg(ap_ref[...])

        for h in range(n_heads):
            sl = pl.ds(h * hd, hd)
            a_s[:, sl] = jnp.exp(-LRU_C * aux_ref[_AUX_R, :, sl] * sp[:, h * hd:(h + 1) * hd])
            gel, gel_grad = _gelu_and_grad(z_ref[:, pl.ds(dl + h * hd, hd)])
            dya = dm_ref[:, sl]
            lam_s[:, sl] = dya * gel
            dz_ref[:, pl.ds(dl + h * hd, hd)] = (dya * hs_ref[:, sl] * gel_grad).astype(BF16)

        def step(r, carry):
            t = tc - 1 - r
            lam = lam_s[pl.ds(t, 1), :] + carry
            lam_s[pl.ds(t, 1), :] = lam
            return a_s[pl.ds(t, 1), :] * lam

        ccar[...] = lax.fori_loop(0, tc, step, ccar[...], unroll=SCAN_UNROLL)

        for h in range(n_heads):
            sl = pl.ds(h * hd, hd)
            sp_h = sp[:, h * hd:(h + 1) * hd]
            lam = lam_s[:, sl]
            a = a_s[:, sl]
            ra = aux_ref[_AUX_R, :, sl]
            ii = aux_ref[_AUX_I, :, sl]
            xc = aux_ref[_AUX_X, :, sl]
            e = aux_ref[_AUX_E, :, sl]
            inv_mult = lax.rsqrt(e)
            mult = e * inv_mult
            hprev = _shifted(hbuf[:, sl], 1)[HIST:HIST + tc]
            lam_i = lam * ii
            lam_m = lam * mult
            dla = lam * hprev * a - lam_i * xc * (a * a) * inv_mult
            dla_r = dla * (-LRU_C) * ra
            dap_ref[:, sl] += jnp.sum(dla_r, axis=0, keepdims=True)
            dpa = dla_r * sp_h * (1.0 - ra)
            dpx = lam_m * xc * ii * (1.0 - ii)
            dba_ref[:, sl] += jnp.sum(dpa, axis=0, keepdims=True)
            dbx_ref[:, sl] += jnp.sum(dpx, axis=0, keepdims=True)
            xb = xc.astype(BF16)
            dpa_b = dpa.astype(BF16)
            dpx_b = dpx.astype(BF16)
            dwa_ref[h] += _dot(xb, dpa_b, "tn")
            dwx_ref[h] += _dot(xb, dpx_b, "tn")
            dxc = lam_m * ii + _dot(dpa_b, wa_ref[h], "nt") + _dot(dpx_b, wx_ref[h], "nt")
            dxbuf[pl.ds(0, tc), sl] = dxc
            dcb_ref[:, sl] += jnp.sum(dxc, axis=0, keepdims=True)
            taps = _conv_taps(xbuf, sl, tc)
            dx_all = dxbuf[:, sl]
            dzx = jnp.zeros((tc, hd), F32)
            for k in range(CONV_WIDTH):
                dcw_ref[k:k + 1, sl] += jnp.sum(dxc * taps[k], axis=0, keepdims=True)
                dzx = dzx + cw_ref[k:k + 1, sl] * _shifted(dx_all, k - (CONV_WIDTH - 1))[0:tc]
            dz_ref[:, sl] = dzx.astype(BF16)
            dxbuf[pl.ds(tc, HIST), sl] = dxbuf[pl.ds(0, HIST), sl]

        for g in range(n_groups):
            cols = pl.ds(g * gd, gd)
            win = POOL_WINDOWS[g]
            dlt, inv_cnt = _pool_delta(pbuf, cols, win, ci * tc, tc)
            db = dlt.astype(BF16)
            q = _dot(db, pw_ref[g], "nn") + pb_ref[:, cols]
            dyb = dm_ref[:, pl.ds(dl + g * gd, gd)]
            dps_ref[:, cols] += jnp.sum(dyb * q, axis=0, keepdims=True)
            dq = dyb * ps_ref[:, cols]
            dpb_ref[:, cols] += jnp.sum(dq, axis=0, keepdims=True)
            dqb = dq.astype(BF16)
            dpw_ref[g] += _dot(db, dqb, "tn")
            dd = _dot(dqb, pw_ref[g], "nt")
            ddbuf[pl.ds(0, tc), cols] = dd * inv_cnt
            dzp = _window_sum(ddbuf[:, cols], win, -1)[0:tc] - dd
            dz_ref[:, pl.ds(2 * dl + g * gd, gd)] = dzp.astype(BF16)
            ddbuf[pl.ds(tc, HIST), cols] = ddbuf[pl.ds(0, HIST), cols]

        @pl.when(i == nc - 1)
        def _():
            dap_ref[...] = dap_ref[...] * (-_sigmoid(-ap_ref[...]))

    rev = lambda i: (nc - 1 - i, 0)
    prev = lambda i: (jnp.maximum((nc - 1 - i) * per - 1, 0), 0)
    const = lambda a: pl.BlockSpec(a.shape[1:], lambda i: (0,) * (a.ndim - 1))
    plist = _mix_param_list(mp)
    grad_shapes = [jax.ShapeDtypeStruct(a.shape[1:], F32) for a in plist]
    buf = lambda rows: pltpu.VMEM((rows, dl), F32)
    outs = pl.pallas_call(
        body, name=name,
        out_shape=[jax.ShapeDtypeStruct((tp, 3 * dl), BF16)] + grad_shapes,
        grid=(nc,),
        in_specs=[pl.BlockSpec((tc, 3 * dl), rev), pl.BlockSpec((HIST, 3 * dl), prev),
                  pl.BlockSpec((tc, dl), rev), pl.BlockSpec((HIST, dl), prev),
                  pl.BlockSpec((_AUX_N, tc, dl), lambda i: (0, nc - 1 - i, 0)),
                  pl.BlockSpec((tc, 2 * dl), rev)] + _mix_param_specs(mp, layer),
        out_specs=[pl.BlockSpec((tc, 3 * dl), rev)] + [const(a) for a in plist],
        scratch_shapes=[buf(HIST + tc), buf(HIST + tc), buf(HIST + tc), buf(tc + HIST), buf(tc + HIST),
                        buf(tc), buf(tc), buf(1)],
        compiler_params=_params(("arbitrary",)),
    )(z, z, hs, hs, aux, dm, *plist)
    return outs[0], dict(zip(_MIX_GRADS, outs[1:]))


def _ffn_fwd_out(h, act, w_out, tag, deps):
    tp, d = h.shape
    f = w_out.shape[1]
    return _matmul(act, w_out, mode="nn", m=tp, n=d, kdim=f, tm=_tile(tp, 1056), tn=_tile(d, 512, LANES),
                   tk=_tile(f, K_TILE, LANES), out_dtype=F32, scale=0.5, residual=h, b_layer=0, name=f"{tag}_out", deps=deps)


def _ffn_bwd(dh, dhb, saved, norm, layer, w_in, w_out, tag, emit, deps):
    h, n_act, gu, act = saved
    tp, d = h.shape
    f = w_out.shape[1]
    dz = _ffn_bwd_act(dhb, w_out, 0, gu, name=f"{tag}_bact", deps=deps)
    tok = emit("w_out", _matmul(act, dhb, mode="tn", m=f, n=d, kdim=tp, tm=_tile(f, 512, LANES), tn=_tile(d, WGRAD_TILE, LANES), tk=tp,
                                out_dtype=BF16, scale=0.5, name=f"{tag}_dwout", deps=[dz]))
    tok = emit("w_in", _matmul(n_act, dz, mode="tn", m=d, n=2 * f, kdim=tp, tm=_tile(d, WGRAD_TILE, LANES), tn=_tile(f, 512, LANES),
                               tk=tp, out_dtype=BF16, b_split=True, name=f"{tag}_dwin", deps=tok))
    dn = _matmul(dz, w_in, mode="nt", m=tp, n=d, kdim=2 * f, tm=_tile(tp, 1056), tn=_tile(d, 1024, LANES), tk=_tile(f, DN_K_TILE, LANES),
                 out_dtype=F32, a_split=True, b_layer=0, name=f"{tag}_dn", deps=tok)
    return _rms_bwd(h, norm, layer, dn, dh, name=f"{tag}_brms", deps=emit("flush", dn))


def _mix_block_fwd(h, n_act, layer, w_in, w_out, mp, tag, deps):
    tp, d = h.shape
    d_in = w_in.shape[2]
    z = _matmul(n_act, w_in, mode="nn", m=tp, n=d_in, kdim=d, tm=_tile(tp, 2112), tn=_tile(d_in, 512, LANES), tk=d,
                out_dtype=F32, b_layer=0, name=f"{tag}_in", deps=deps)
    m_act, hs, aux = _mix_fwd(z, mp, layer, name=f"{tag}_mix")
    h_out = _matmul(m_act, w_out, mode="nn", m=tp, n=d, kdim=d, tm=_tile(tp, 2112), tn=_tile(d, 512, LANES), tk=d,
                    out_dtype=F32, residual=h, b_layer=0, name=f"{tag}_out")
    return h_out, (h, n_act, z, hs, aux, m_act)


def _mix_block_bwd(dh, dhb, saved, norm, layer, w_in, w_out, mp, tag, emit, deps):
    h, n_act, z, hs, aux, m_act = saved
    tp, d = h.shape
    d_in = w_in.shape[2]
    dm = _matmul(dhb, w_out, mode="nt", m=tp, n=d, kdim=d, tm=_tile(tp, 2112), tn=_tile(d, 512, LANES), tk=d,
                 out_dtype=F32, b_layer=0, name=f"{tag}_dm", deps=deps)
    tok = emit("w_out", _matmul(m_act, dhb, mode="tn", m=d, n=d, kdim=tp, tm=_tile(d, 512, LANES), tn=_tile(d, WGRAD_TILE, LANES), tk=tp,
                                out_dtype=BF16, name=f"{tag}_dwout", deps=[dm]))
    dz, g_mix = _mix_bwd(z, hs, aux, dm, mp, layer, name=f"{tag}_bmix")
    tok = emit("w_in", _matmul(n_act, dz, mode="tn", m=d, n=d_in, kdim=tp, tm=_tile(d, WGRAD_TILE, LANES), tn=_tile(d_in, 512, LANES),
                               tk=tp, out_dtype=BF16, name=f"{tag}_dwin", deps=tok))
    dn = _matmul(dz, w_in, mode="nt", m=tp, n=d, kdim=d_in, tm=_tile(tp, 1056), tn=_tile(d, 1024, LANES), tk=d_in,
                 out_dtype=F32, b_layer=0, name=f"{tag}_dn", deps=tok)
    dh_in, dhb_in, g_norm = _rms_bwd(h, norm, layer, dn, dh, name=f"{tag}_brms", deps=emit("flush", dn))
    return dh_in, dhb_in, g_norm, g_mix


def _mesh_pos():
    x, y, c = lax.axis_index("x"), lax.axis_index("y"), lax.axis_index("c")
    chips = [(1 - x, y), (x, 1 - y), (1 - x, 1 - y)]
    return x, y, c, chips


def _block(ref, axis, j, size):
    idx = [slice(None)] * len(ref.shape)
    idx[axis] = pl.ds(j * size, size)
    return ref.at[tuple(idx)]


_HBM = pl.BlockSpec(memory_space=pltpu.HBM)
_SEM = pl.BlockSpec(memory_space=pltpu.SEMAPHORE)
_ANY = pl.BlockSpec(memory_space=pl.ANY)
_EFFECT = pltpu.SideEffectType.DATAFLOW_SIDE_EFFECTING
_N_OUT = 4


def _in_hbm(a):
    return pltpu.with_memory_space_constraint(a, pltpu.HBM)


def _gather_start(shards, axes, deps, *, name, lands=None):
    n = len(shards)
    sizes = [s.shape[ax] for s, ax in zip(shards, axes)]
    full_shapes = [s.shape[:ax] + (N_DEV * s.shape[ax],) + s.shape[ax + 1:] for s, ax in zip(shards, axes)]

    def body(*refs):
        srcs, lands = refs[:n], refs[n:2 * n]
        send_sems, recv_sems = refs[2 * n + len(deps)], refs[2 * n + len(deps) + 1]
        token = refs[-1]
        x, y, c, chips = _mesh_pos()
        targets = [(x, y, 1 - c)] + [(*chip, c) for chip in chips]
        for a in range(n):
            dst = _block(lands[a], axes[a], 4 * x + 2 * y + c, sizes[a])
            for k, to in enumerate(targets):
                pltpu.make_async_remote_copy(src_ref=srcs[a], dst_ref=dst, send_sem=send_sems.at[_N_OUT * a + k],
                                             recv_sem=recv_sems.at[_N_OUT * a + k],
                                             device_id=to, device_id_type=MESH).start()
        token[...] = jnp.zeros(token.shape, F32)

    lands0 = [lax.empty(fs, s.dtype) for fs, s in zip(full_shapes, shards)] if lands is None else lands
    outs = pl.pallas_call(
        body, name=name,
        out_shape=(pltpu.SemaphoreType.DMA((n * _N_OUT,)), pltpu.SemaphoreType.DMA((n * _N_OUT,)),
                   *[pltpu.HBM(s.shape, s.dtype) for s in shards], *[pltpu.HBM(fs, s.dtype) for fs, s in zip(full_shapes, shards)],
                   jax.ShapeDtypeStruct((8, LANES), F32)),
        in_specs=[_HBM] * (2 * n) + [_ANY] * len(deps),
        out_specs=(_SEM, _SEM, *[_HBM] * (2 * n), pl.BlockSpec(memory_space=pltpu.VMEM)),
        input_output_aliases={a: 2 + a for a in range(2 * n)},
        compiler_params=pltpu.CompilerParams(has_side_effects=_EFFECT),
    )(*[_in_hbm(s) for s in shards], *[_in_hbm(z) for z in lands0], *deps)
    handle = dict(send_sems=outs[0], recv_sems=outs[1], srcs=list(outs[2:2 + n]), lands=list(outs[2 + n:2 + 2 * n]), axes=list(axes))
    return handle, outs[-1]


def _gather_wait(handle, after, *, name):
    srcs, lands, axes = handle["srcs"], handle["lands"], handle["axes"]
    n = len(srcs)
    sizes = [s.shape[ax] for s, ax in zip(srcs, axes)]

    def body(*refs):
        src_refs, land_refs = refs[:n], refs[n:2 * n]
        send_sems, recv_sems = refs[2 * n], refs[2 * n + 1]
        x, y, c, chips = _mesh_pos()
        peers = [(x, y, 1 - c)] + [(*chip, c) for chip in chips]
        for a in range(n):
            for k, dev in enumerate(peers):
                cp = pltpu.make_async_remote_copy(
                    src_ref=src_refs[a], dst_ref=_block(land_refs[a], axes[a], 4 * dev[0] + 2 * dev[1] + dev[2], sizes[a]),
                    send_sem=send_sems.at[_N_OUT * a + k], recv_sem=recv_sems.at[_N_OUT * a + k], device_id=dev,
                    device_id_type=MESH)
                cp.wait_send()
                cp.wait_recv()

    outs = pl.pallas_call(
        body, name=name,
        out_shape=(*[pltpu.HBM(s.shape, s.dtype) for s in srcs], *[pltpu.HBM(z.shape, z.dtype) for z in lands]),
        in_specs=[_HBM] * (2 * n) + [_SEM, _SEM] + [_ANY] * len(after),
        out_specs=tuple([_HBM] * (2 * n)),
        input_output_aliases={a: a for a in range(2 * n)},
        compiler_params=pltpu.CompilerParams(has_side_effects=_EFFECT),
    )(*srcs, *lands, handle["send_sems"], handle["recv_sems"], *after)
    return list(outs[:n]), list(outs[n:])


def _place_own(shard, full, axis, dev, *, name):
    nd = shard.ndim
    rows, cols = shard.shape[-2:]
    tr = _tile(rows, 512, _row_mult(shard.dtype)) if rows % _row_mult(shard.dtype) == 0 else rows
    nrb = rows // tr
    block = shard.shape[:-2] + (tr, cols)

    def in_map(r, dev_ref):
        return (0,) * (nd - 2) + (r, 0)

    def out_map(r, dev_ref):
        idx = [0] * nd
        idx[nd - 2] = r
        idx[axis] = dev_ref[0] * (nrb if axis == nd - 2 else 1) + idx[axis]
        return tuple(idx)

    def body(dev_ref, src_ref, full_ref, out_ref):
        out_ref[...] = src_ref[...]

    grid_spec = pltpu.PrefetchScalarGridSpec(
        num_scalar_prefetch=1, grid=(nrb,),
        in_specs=[pl.BlockSpec(block, in_map), _ANY], out_specs=pl.BlockSpec(block, out_map))
    return pl.pallas_call(body, name=name, grid_spec=grid_spec, out_shape=jax.ShapeDtypeStruct(full.shape, full.dtype),
                          input_output_aliases={2: 0}, compiler_params=_params(("arbitrary",)))(dev, shard, full)


def _gather_pass_start(fulls, axes, *, name):
    n = len(fulls)
    sizes = [z.shape[ax] // N_DEV for z, ax in zip(fulls, axes)]

    def body(*refs):
        srcs = refs[:n]
        send_sems, recv_sems = refs[n], refs[n + 1]
        token = refs[-1]
        x, y, c, chips = _mesh_pos()
        for a in range(n):
            for j, chip in enumerate(chips):
                blk = _block(srcs[a], axes[a], 4 * chip[0] + 2 * chip[1] + c, sizes[a])
                pltpu.make_async_remote_copy(src_ref=blk, dst_ref=blk, send_sem=send_sems.at[3 * a + j], recv_sem=recv_sems.at[3 * a + j],
                                             device_id=(x, y, 1 - c), device_id_type=MESH).start()
        token[...] = jnp.zeros(token.shape, F32)

    outs = pl.pallas_call(
        body, name=name,
        out_shape=(pltpu.SemaphoreType.DMA((3 * n,)), pltpu.SemaphoreType.DMA((3 * n,)), *[pltpu.HBM(z.shape, z.dtype) for z in fulls],
                   jax.ShapeDtypeStruct((8, LANES), F32)),
        in_specs=[_HBM] * n, out_specs=(_SEM, _SEM, *[_HBM] * n, pl.BlockSpec(memory_space=pltpu.VMEM)),
        input_output_aliases={a: 2 + a for a in range(n)},
        compiler_params=pltpu.CompilerParams(has_side_effects=_EFFECT),
    )(*[_in_hbm(z) for z in fulls])
    return dict(send_sems=outs[0], recv_sems=outs[1], fulls=list(outs[2:2 + n]), axes=list(axes)), outs[-1]


def _gather_pass_wait(handle, after, *, name):
    fulls, axes = handle["fulls"], handle["axes"]
    n = len(fulls)
    sizes = [z.shape[ax] // N_DEV for z, ax in zip(fulls, axes)]

    def body(*refs):
        bufs = refs[:n]
        send_sems, recv_sems = refs[n], refs[n + 1]
        x, y, c, chips = _mesh_pos()
        for a in range(n):
            for j, chip in enumerate(chips):
                cp = pltpu.make_async_remote_copy(
                    src_ref=_block(bufs[a], axes[a], 4 * chip[0] + 2 * chip[1] + c, sizes[a]),
                    dst_ref=_block(bufs[a], axes[a], 4 * chip[0] + 2 * chip[1] + (1 - c), sizes[a]),
                    send_sem=send_sems.at[3 * a + j], recv_sem=recv_sems.at[3 * a + j], device_id=(x, y, 1 - c),
                    device_id_type=MESH)
                cp.wait_send()
                cp.wait_recv()

    outs = pl.pallas_call(
        body, name=name,
        out_shape=tuple(pltpu.HBM(z.shape, z.dtype) for z in fulls),
        in_specs=[_HBM] * n + [_SEM, _SEM] + [_ANY] * len(after), out_specs=tuple([_HBM] * n),
        input_output_aliases={a: a for a in range(n)},
        compiler_params=pltpu.CompilerParams(has_side_effects=_EFFECT),
    )(*fulls, handle["send_sems"], handle["recv_sems"], *after)
    return list(outs)


def _rs_exchange_sibling(tensors, axes, *, name):
    n = len(tensors)
    n_layers = [len(t) for t in tensors]
    sizes = [t[0].shape[ax] // N_DEV for t, ax in zip(tensors, axes)]
    blk_shapes = [t[0].shape[:ax] + (sz,) + t[0].shape[ax + 1:] for t, ax, sz in zip(tensors, axes, sizes)]
    flat = [g for t in tensors for g in t]
    offs = [sum(n_layers[:a]) for a in range(n)]

    def body(*refs):
        srcs = refs[:len(flat)]
        lands = refs[len(flat):len(flat) + n]
        send_sems, recv_sems = refs[len(flat) + n:]
        x, y, c, _ = _mesh_pos()
        sib = (x, y, 1 - c)
        for a in range(n):
            for l in range(n_layers[a]):
                for i in range(N_CHIP):
                    pltpu.make_async_remote_copy(
                        src_ref=_block(srcs[offs[a] + l], axes[a], 2 * i + (1 - c), sizes[a]), dst_ref=lands[a].at[i, l],
                        send_sem=send_sems.at[a], recv_sem=recv_sems.at[a], device_id=sib, device_id_type=MESH).start()
        for a in range(n):
            pltpu.make_async_remote_copy(src_ref=lands[a], dst_ref=lands[a], send_sem=send_sems.at[a], recv_sem=recv_sems.at[a],
                                         device_id=sib, device_id_type=MESH).wait()

    any_spec = pl.BlockSpec(memory_space=pl.ANY)
    return pl.pallas_call(
        body, name=name,
        out_shape=[jax.ShapeDtypeStruct((N_CHIP, nl) + bs, t[0].dtype) for nl, bs, t in zip(n_layers, blk_shapes, tensors)],
        in_specs=[any_spec] * len(flat), out_specs=[any_spec] * n,
        scratch_shapes=[pltpu.SemaphoreType.DMA((n,)), pltpu.SemaphoreType.DMA((n,))],
    )(*flat)


def _rs_add_sibling(layers, axis, land, my_c, *, name):
    n_layers = len(layers)
    shape = layers[0].shape
    size = shape[axis] // N_DEV
    blk_shape = shape[:axis] + (size,) + shape[axis + 1:]
    nd = len(shape)
    rows = blk_shape[-2]
    tr = _tile(rows, 512, _row_mult(layers[0].dtype))
    inner = (tr, blk_shape[-1])
    lead = blk_shape[:-2]
    if lead:
        raise ValueError("blocked gradients are 2-D")
    nrb = rows // tr

    def src_map(l):
        def imap(i, r, c_ref):
            j = 2 * i + c_ref[0]
            return (j * nrb + r, 0) if axis == 0 else (r, j)
        return imap

    def body(c_ref, *refs):
        srcs = refs[:n_layers]
        land_ref = refs[n_layers]
        out_ref = refs[n_layers + 1]
        for l in range(n_layers):
            out_ref[l] = (srcs[l][...].astype(F32) + land_ref[l].astype(F32)).astype(out_ref.dtype)

    grid_spec = pltpu.PrefetchScalarGridSpec(
        num_scalar_prefetch=1, grid=(N_CHIP, nrb),
        in_specs=[pl.BlockSpec(inner, src_map(l)) for l in range(n_layers)]
        + [pl.BlockSpec((None, n_layers) + inner, lambda i, r, c_ref: (i, 0, r, 0))],
        out_specs=pl.BlockSpec((None, n_layers) + inner, lambda i, r, c_ref: (i, 0, r, 0)),
    )
    return pl.pallas_call(
        body, name=name, grid_spec=grid_spec,
        out_shape=jax.ShapeDtypeStruct((N_CHIP, n_layers) + blk_shape, layers[0].dtype),
        compiler_params=_params(("arbitrary", "arbitrary")),
    )(my_c, *layers, land)


def _rs_sibling_start(g, axis, *, name):
    size = g.shape[axis] // N_DEV
    land = lax.empty((N_CHIP, 1) + g.shape[:axis] + (size,) + g.shape[axis + 1:], g.dtype)

    def body(src, dst, send_sems, recv_sems, src_thru, dst_thru, token):
        x, y, c, _ = _mesh_pos()
        for i in range(N_CHIP):
            pltpu.make_async_remote_copy(src_ref=_block(src, axis, 2 * i + (1 - c), size), dst_ref=dst.at[i, 0], send_sem=send_sems.at[i],
                                         recv_sem=recv_sems.at[i], device_id=(x, y, 1 - c), device_id_type=MESH).start()
        token[...] = jnp.zeros(token.shape, F32)

    outs = pl.pallas_call(
        body, name=name,
        out_shape=(pltpu.SemaphoreType.DMA((N_CHIP,)), pltpu.SemaphoreType.DMA((N_CHIP,)), pltpu.HBM(g.shape, g.dtype),
                   pltpu.HBM(land.shape, land.dtype), jax.ShapeDtypeStruct((8, LANES), F32)),
        in_specs=[_HBM, _HBM], out_specs=(_SEM, _SEM, _HBM, _HBM, pl.BlockSpec(memory_space=pltpu.VMEM)),
        input_output_aliases={0: 2, 1: 3},
        compiler_params=pltpu.CompilerParams(has_side_effects=_EFFECT),
    )(_in_hbm(g), _in_hbm(land))
    return dict(send_sems=outs[0], recv_sems=outs[1], src=outs[2], land=outs[3], axis=axis), outs[4]


def _rs_sibling_wait(handle, after, *, name):
    src, land, axis = handle["src"], handle["land"], handle["axis"]
    size = src.shape[axis] // N_DEV

    def body(src_ref, dst_ref, send_sems, recv_sems, *rest):
        x, y, c, _ = _mesh_pos()
        for i in range(N_CHIP):
            cp = pltpu.make_async_remote_copy(src_ref=_block(src_ref, axis, 2 * i + (1 - c), size), dst_ref=dst_ref.at[i, 0],
                                              send_sem=send_sems.at[i], recv_sem=recv_sems.at[i], device_id=(x, y, 1 - c),
                                              device_id_type=MESH)
            cp.wait_send()
            cp.wait_recv()

    outs = pl.pallas_call(
        body, name=name,
        out_shape=(pltpu.HBM(src.shape, src.dtype), pltpu.HBM(land.shape, land.dtype)),
        in_specs=[_HBM, _HBM, _SEM, _SEM] + [_ANY] * len(after), out_specs=(_HBM, _HBM),
        input_output_aliases={0: 0, 1: 1},
        compiler_params=pltpu.CompilerParams(has_side_effects=_EFFECT),
    )(src, land, handle["send_sems"], handle["recv_sems"], *after)
    return outs[0], outs[1]


def _rs_chips_start(part, *, name):
    land = lax.empty(part.shape, part.dtype)

    def body(src, dst, send_sems, recv_sems, src_thru, dst_thru, token):
        x, y, c, chips = _mesh_pos()
        for j, chip in enumerate(chips):
            pltpu.make_async_remote_copy(src_ref=src.at[2 * chip[0] + chip[1]], dst_ref=dst.at[2 * x + y], send_sem=send_sems.at[j],
                                         recv_sem=recv_sems.at[j], device_id=(*chip, c), device_id_type=MESH).start()
        token[...] = jnp.zeros(token.shape, F32)

    outs = pl.pallas_call(
        body, name=name,
        out_shape=(pltpu.SemaphoreType.DMA((3,)), pltpu.SemaphoreType.DMA((3,)), pltpu.HBM(part.shape, part.dtype),
                   pltpu.HBM(land.shape, land.dtype), jax.ShapeDtypeStruct((8, LANES), F32)),
        in_specs=[_HBM, _HBM], out_specs=(_SEM, _SEM, _HBM, _HBM, pl.BlockSpec(memory_space=pltpu.VMEM)),
        input_output_aliases={0: 2, 1: 3},
        compiler_params=pltpu.CompilerParams(has_side_effects=_EFFECT),
    )(_in_hbm(part), _in_hbm(land))
    return dict(send_sems=outs[0], recv_sems=outs[1], src=outs[2], land=outs[3]), outs[4]


def _rs_chips_wait(handle, after, *, name):
    def body(src, dst, send_sems, recv_sems, *rest):
        x, y, c, chips = _mesh_pos()
        for j, chip in enumerate(chips):
            cp = pltpu.make_async_remote_copy(src_ref=src.at[2 * chip[0] + chip[1]], dst_ref=dst.at[2 * chip[0] + chip[1]],
                                              send_sem=send_sems.at[j], recv_sem=recv_sems.at[j], device_id=(*chip, c),
                                              device_id_type=MESH)
            cp.wait_send()
            cp.wait_recv()

    src, land = handle["src"], handle["land"]
    outs = pl.pallas_call(
        body, name=name,
        out_shape=(pltpu.HBM(src.shape, src.dtype), pltpu.HBM(land.shape, land.dtype)),
        in_specs=[_HBM, _HBM, _SEM, _SEM] + [_ANY] * len(after), out_specs=(_HBM, _HBM),
        input_output_aliases={0: 0, 1: 1},
        compiler_params=pltpu.CompilerParams(has_side_effects=_EFFECT),
    )(src, land, handle["send_sems"], handle["recv_sems"], *after)
    return outs[0], outs[1]


def _adam_math(w, g, m, v):
    m_new = ADAM_B1 * m + (1.0 - ADAM_B1) * g
    v_new = ADAM_B2 * v + (1.0 - ADAM_B2) * (g * g)
    m_hat = m_new / (1.0 - ADAM_B1 ** ADAM_STEP)
    v_hat = v_new / (1.0 - ADAM_B2 ** ADAM_STEP)
    delta = -ADAM_LR * (m_hat / (jnp.sqrt(v_hat) + ADAM_EPS) + ADAM_WD * w)
    return delta, m_new, v_new


def _sum_adam(part, land, slots, w, m, v, layer, prev, *, name, deps=()):
    n_layers, rows, cols = w.shape
    tr = _tile(rows, 256, _row_mult(land.dtype))
    n_prev = 0 if prev is None else 4

    def body(slots_ref, p0_ref, p1_ref, p2_ref, p3_ref, w_ref, m_ref, v_ref, *rest):
        g_ref, d_ref, mo_ref, vo_ref = rest[n_prev + len(deps):]
        g = p0_ref[...].astype(F32)
        for ref in (p1_ref, p2_ref, p3_ref):
            g = g + ref[...].astype(F32)
        delta, m_new, v_new = _adam_math(w_ref[...], g, m_ref[...], v_ref[...])
        g_ref[...] = g
        d_ref[...] = delta
        mo_ref[...] = m_new
        vo_ref[...] = v_new

    blk = pl.BlockSpec((None, tr, cols), lambda r, s: (layer, r, 0))
    slot = lambda j: pl.BlockSpec((None, None, tr, cols), lambda r, s: (s[j], 0, r, 0))
    shp = jax.ShapeDtypeStruct((n_layers, rows, cols), F32)
    grid_spec = pltpu.PrefetchScalarGridSpec(
        num_scalar_prefetch=1, grid=(rows // tr,),
        in_specs=[slot(0), slot(1), slot(2), slot(3), blk, blk, blk] + [_ANY] * (n_prev + len(deps)),
        out_specs=(blk, blk, blk, blk))
    return pl.pallas_call(
        body, name=name, out_shape=(shp, shp, shp, shp), grid_spec=grid_spec,
        input_output_aliases={8 + i: i for i in range(n_prev)},
        compiler_params=_params(("parallel",)),
    )(slots, part, land, land, land, w, m, v, *(prev or ()), *deps)


def _sum_slots(part, land, slots, *, name):
    _, _, rows, cols = land.shape
    tr = _tile(rows, 256, 8)

    def body(slots_ref, p0_ref, p1_ref, p2_ref, p3_ref, g_ref):
        g_ref[...] = p0_ref[...] + p1_ref[...] + p2_ref[...] + p3_ref[...]

    slot = lambda j: pl.BlockSpec((None, None, tr, cols), lambda r, s: (s[j], 0, r, 0))
    grid_spec = pltpu.PrefetchScalarGridSpec(
        num_scalar_prefetch=1, grid=(rows // tr,), in_specs=[slot(0), slot(1), slot(2), slot(3)],
        out_specs=pl.BlockSpec((tr, cols), lambda r, s: (r, 0)))
    return pl.pallas_call(body, name=name, out_shape=jax.ShapeDtypeStruct((rows, cols), F32), grid_spec=grid_spec,
                          compiler_params=_params(("parallel",)))(slots, part, land, land, land)


def _adam_flat(w, g, m, v, *, name):
    rows, cols = w.shape
    tr = _tile(rows, 256, 8)

    def body(w_ref, g_ref, m_ref, v_ref, d_ref, mo_ref, vo_ref):
        delta, m_new, v_new = _adam_math(w_ref[...], g_ref[...], m_ref[...], v_ref[...])
        d_ref[...] = delta
        mo_ref[...] = m_new
        vo_ref[...] = v_new

    blk = pl.BlockSpec((tr, cols), lambda r: (r, 0))
    shp = jax.ShapeDtypeStruct((rows, cols), F32)
    return pl.pallas_call(
        body, name=name, out_shape=(shp, shp, shp), grid=(rows // tr,), in_specs=[blk] * 4, out_specs=(blk, blk, blk),
        compiler_params=_params(("parallel",)),
    )(w, g, m, v)


_BIG = (("ffn1_w_in", 1), ("ffn1_w_out", 0), ("w_in", 1), ("w_out", 0), ("ffn2_w_in", 1), ("ffn2_w_out", 0))
_SMALL = ("meta_tokens", "ffn1_norm", "mix_norm", "conv_w", "conv_b", "lru_wa", "lru_ba", "lru_wx", "lru_bx", "lru_a_param",
          "pool_w", "pool_b", "pool_scale", "ffn2_norm", "final_norm")
_SMALL_SHARD_AXIS = {"meta_tokens": 1, "conv_w": 2, "pool_w": 2}
_PACK_COLS = 1024


def _pack(arrs):
    flat = jnp.concatenate([a.reshape(-1) for a in arrs])
    unit = N_DEV * 8 * _PACK_COLS
    total = -(-flat.shape[0] // unit) * unit
    flat = jnp.pad(flat, (0, total - flat.shape[0]))
    return flat.reshape(total // _PACK_COLS, _PACK_COLS)


def _unpack(packed, shapes):
    flat = packed.reshape(-1)
    out, off = [], 0
    for s in shapes:
        size = 1
        for v in s:
            size *= v
        out.append(flat[off:off + size].reshape(s))
        off += size
    return out


def _my_shard(full, axis, dev):
    size = full.shape[axis] // N_DEV
    return lax.dynamic_slice_in_dim(full, dev * size, size, axis)


def kernel(x, meta_tokens, ffn1_norm, ffn1_w_in, ffn1_w_out, mix_norm, w_in, conv_w, conv_b, lru_wa, lru_ba, lru_wx, lru_bx, lru_a_param, pool_w, pool_b, pool_scale, w_out, ffn2_norm, ffn2_w_in, ffn2_w_out, final_norm, loss_target, m_meta_tokens, m_ffn1_norm, m_ffn1_w_in, m_ffn1_w_out, m_mix_norm, m_w_in, m_conv_w, m_conv_b, m_lru_wa, m_lru_ba, m_lru_wx, m_lru_bx, m_lru_a_param, m_pool_w, m_pool_b, m_pool_scale, m_w_out, m_ffn2_norm, m_ffn2_w_in, m_ffn2_w_out, m_final_norm, v_meta_tokens, v_ffn1_norm, v_ffn1_w_in, v_ffn1_w_out, v_mix_norm, v_w_in, v_conv_w, v_conv_b, v_lru_wa, v_lru_ba, v_lru_wx, v_lru_bx, v_lru_a_param, v_pool_w, v_pool_b, v_pool_scale, v_w_out, v_ffn2_norm, v_ffn2_w_in, v_ffn2_w_out, v_final_norm):
    names = ("meta_tokens", "ffn1_norm", "ffn1_w_in", "ffn1_w_out", "mix_norm", "w_in", "conv_w", "conv_b", "lru_wa", "lru_ba",
             "lru_wx", "lru_bx", "lru_a_param", "pool_w", "pool_b", "pool_scale", "w_out", "ffn2_norm", "ffn2_w_in", "ffn2_w_out",
             "final_norm")
    w = dict(zip(names, (meta_tokens, ffn1_norm, ffn1_w_in, ffn1_w_out, mix_norm, w_in, conv_w, conv_b, lru_wa, lru_ba, lru_wx,
                         lru_bx, lru_a_param, pool_w, pool_b, pool_scale, w_out, ffn2_norm, ffn2_w_in, ffn2_w_out, final_norm)))
    mom = dict(zip(names, (m_meta_tokens, m_ffn1_norm, m_ffn1_w_in, m_ffn1_w_out, m_mix_norm, m_w_in, m_conv_w, m_conv_b, m_lru_wa,
                           m_lru_ba, m_lru_wx, m_lru_bx, m_lru_a_param, m_pool_w, m_pool_b, m_pool_scale, m_w_out, m_ffn2_norm,
                           m_ffn2_w_in, m_ffn2_w_out, m_final_norm)))
    vel = dict(zip(names, (v_meta_tokens, v_ffn1_norm, v_ffn1_w_in, v_ffn1_w_out, v_mix_norm, v_w_in, v_conv_w, v_conv_b, v_lru_wa,
                           v_lru_ba, v_lru_wx, v_lru_bx, v_lru_a_param, v_pool_w, v_pool_b, v_pool_scale, v_w_out, v_ffn2_norm,
                           v_ffn2_w_in, v_ffn2_w_out, v_final_norm)))
    n_layers, d = ffn1_norm.shape
    n_meta = meta_tokens.shape[0]
    seq = x.shape[1]
    t_valid = n_meta + seq
    tp = -(-t_valid // ROW_ALIGN) * ROW_ALIGN
    dev = 4 * lax.axis_index("x") + 2 * lax.axis_index("y") + lax.axis_index("c")
    my_c = lax.axis_index("c").astype(jnp.int32).reshape(1)
    dev1 = dev.astype(jnp.int32).reshape(1)
    mx, my = lax.axis_index("x"), lax.axis_index("y")
    slots = jnp.stack([2 * mx + my, 2 * (1 - mx) + my, 2 * mx + (1 - my), 2 * (1 - mx) + (1 - my)]).astype(jnp.int32)

    big_axis = dict(_BIG)
    vec = lambda a: a.reshape(a.shape[0], 1, a.shape[1])
    norms = dict(ffn1=vec(ffn1_norm), mix=vec(mix_norm), ffn2=vec(ffn2_norm))
    units = [(kind, l) for l in range(n_layers) for kind in ("ffn1", "mix", "ffn2")]

    handles = {}
    placed = {}
    state = dict(token=None, started=0, first=None)

    def behind(a):
        return a if state["first"] is None else lax.optimization_barrier((state["first"], a))[1]

    def shard(k, l):
        return behind(w[k][l:l + 1]).astype(BF16), big_axis[k] + 1

    groups = [[shard("ffn1_w_in", 0), (meta_tokens, 1)], [shard("ffn1_w_out", 0)]]

    def order():
        return [] if state["token"] is None else [state["token"]]

    def start_next():
        i = state["started"]
        if i < len(groups):
            handles[i], state["token"] = _gather_start([s for s, _ in groups[i]], [ax for _, ax in groups[i]], order(),
                                                       name=f"gather{i}_start", lands=placed.get(i))
            state["started"] = i + 1

    def finish_begin(i, after):
        handle = handles.pop(i)
        srcs, fulls = _gather_wait(handle, list(after) + order(), name=f"gather{i}_wait")
        if i not in placed:
            fulls = [_place_own(s, z, ax, dev1, name=f"gather{i}_own{a}") for a, (s, z, ax) in enumerate(zip(srcs, fulls, handle["axes"]))]
        passing, state["token"] = _gather_pass_start(fulls, handle["axes"], name=f"gather{i}_pass")
        return i, passing

    def finish_end(ctx, after):
        i, passing = ctx
        fulls = _gather_pass_wait(passing, after, name=f"gather{i}_passwait")
        start_next()
        return fulls

    start_next()
    start_next()
    state["first"] = state["token"]
    for kind, l in units[1:]:
        if kind == "mix":
            groups.append([shard("w_in", l), shard("w_out", l)] + ([(behind(conv_w), 2), (behind(pool_w), 2)] if l == 0 else []))
        else:
            groups.append([shard(f"{kind}_w_in", l), shard(f"{kind}_w_out", l)])
    for i in range(2, len(groups)):
        placed[i] = [_place_own(s, lax.empty(s.shape[:ax] + (N_DEV * s.shape[ax],) + s.shape[ax + 1:], s.dtype), ax, dev1,
                                name=f"gather{i}_own{a}") for a, (s, ax) in enumerate(groups[i])]
    pad = jnp.zeros((tp - t_valid, d), F32)
    tgt = jnp.concatenate([jnp.zeros((n_meta, d), F32), behind(loss_target[0]), pad], axis=0)
    h0_base = jnp.concatenate([jnp.zeros((n_meta, d), F32), behind(x[0]), pad], axis=0)
    packed_state = [_pack([behind(t[k]) for k in _SMALL]) for t in (w, mom, vel)]
    ctx = finish_begin(0, [tgt, h0_base] + packed_state + [z for i in placed for z in placed[i]])
    w_in0, meta_full = finish_end(ctx, [])
    h0 = lax.dynamic_update_slice(h0_base, meta_full, (0, 0))
    n_act = _rms_fwd(h0, norms["ffn1"], 0, name="l0ffn1_rms", deps=order())
    gu, act = _ffn_in_fwd(n_act, w_in0, 0, name="l0ffn1_in")
    (w_out0,) = finish_end(finish_begin(1, [act]), [])
    h = _ffn_fwd_out(h0, act, w_out0, "l0ffn1", order())
    gathered = {units[0]: (w_in0, w_out0)}
    saved = {units[0]: (h0, n_act, gu, act)}
    mp = None
    for ui, (kind, l) in enumerate(units[1:], start=1):
        tag = f"l{l}{kind}"
        ctx = finish_begin(ui + 1, [h])
        n_act = _rms_fwd(h, norms[kind], l, name=f"{tag}_rms", deps=order())
        fulls = finish_end(ctx, [n_act])
        if kind == "mix":
            if l == 0:
                mp = dict(conv_w=fulls[2], conv_b=vec(conv_b), wa=lru_wa.astype(BF16), ba=vec(lru_ba), wx=lru_wx.astype(BF16),
                          bx=vec(lru_bx), a_param=vec(lru_a_param), pool_w=fulls[3].astype(BF16), pool_b=vec(pool_b),
                          pool_scale=vec(pool_scale))
            h, saved[(kind, l)] = _mix_block_fwd(h, n_act, l, fulls[0], fulls[1], mp, tag, order())
        else:
            gu, act = _ffn_in_fwd(n_act, fulls[0], 0, name=f"{tag}_in", deps=order())
            saved[(kind, l)] = (h, n_act, gu, act)
            h = _ffn_fwd_out(h, act, fulls[1], tag, [])
        gathered[(kind, l)] = (fulls[0], fulls[1])
    dh, dhb, g_final, loss_local = _final_loss(h, final_norm.reshape(1, d), tgt, n_meta, t_valid, name="final_loss")
    loss = lax.psum(loss_local[0, 0], ("x", "y", "c"))

    pending = []
    big_out = {}

    landed = []

    def drain(after):
        while pending:
            k, l, handle = pending.pop(0)
            landed.append((k, l) + _rs_chips_wait(handle, after, name=f"rs_{k}{l}_wait"))

    def adams(items, deps):
        for k, l, part, land in items:
            big_out[k] = _sum_adam(part, land, slots, w[k], mom[k], vel[k], l, big_out.get(k), name=f"adam_{k}{l}", deps=deps)
        return [big_out[items[-1][0]][0]] if items else []

    sib_pending = []

    def to_chips(k, l, g, land):
        part = _rs_add_sibling([g], big_axis[k], land, my_c, name=f"rs_{k}{l}_add")
        handle, token = _rs_chips_start(part, name=f"rs_{k}{l}_start")
        pending.append((k, l, handle))
        return token

    def finish_sibling(after):
        token = None
        while sib_pending:
            k, l, handle = sib_pending.pop(0)
            g, land = _rs_sibling_wait(handle, after, name=f"rs_{k}{l}_sibwait")
            token = to_chips(k, l, g, land)
        return token

    def emitter(kind, l):
        last_unit = (kind, l) == units[0]

        def emit(which, g):
            if which == "w_out":
                drain([g])
            token = finish_sibling([g])
            order_after = [] if token is None else [token]
            if which == "flush":
                return order_after
            k = which if kind == "mix" else f"{kind}_{which}"
            if last_unit and which == "w_in":
                (land,) = _rs_exchange_sibling([[g]], [big_axis[k]], name=f"rs_{k}{l}_sib")
                return order_after + [to_chips(k, l, g, land)]
            handle, token = _rs_sibling_start(g, big_axis[k], name=f"rs_{k}{l}_sibstart")
            sib_pending.append((k, l, handle))
            return order_after + [token]
        return emit

    g_norm = {k: [None] * n_layers for k in ("ffn1", "mix", "ffn2")}
    g_mix = [None] * n_layers
    for kind, l in reversed(units):
        w_i, w_o = gathered[(kind, l)]
        tag = f"l{l}{kind}"
        if kind == "mix":
            dh, dhb, g_norm[kind][l], g_mix[l] = _mix_block_bwd(dh, dhb, saved[(kind, l)], norms[kind], l, w_i, w_o, mp, tag,
                                                                 emitter(kind, l), [])
        else:
            dh, dhb, g_norm[kind][l] = _ffn_bwd(dh, dhb, saved[(kind, l)], norms[kind], l, w_i, w_o, tag, emitter(kind, l), [])
    drain([dh])
    dh0 = dh
    grad_x = dh0[n_meta:t_valid][None]

    mix_key = dict(conv_w="conv_w", conv_b="conv_b", lru_wa="wa", lru_ba="ba", lru_wx="wx", lru_bx="bx", lru_a_param="a_param",
                   pool_w="pool_w", pool_b="pool_b", pool_scale="pool_scale")
    small_local = {"meta_tokens": dh0[:n_meta], "final_norm": g_final.reshape(d)}
    for k in ("ffn1", "mix", "ffn2"):
        small_local[f"{k}_norm"] = jnp.stack([g.reshape(d) for g in g_norm[k]])
    for k, mk in mix_key.items():
        small_local[k] = jnp.stack([g_mix[l][mk] for l in range(n_layers)]).reshape(
            w[k].shape if k not in _SMALL_SHARD_AXIS else small_shape_full(w[k], _SMALL_SHARD_AXIS[k]))
    small_shapes = [small_local[k].shape for k in _SMALL]
    packed = _pack([small_local[k] for k in _SMALL])
    (land,) = _rs_exchange_sibling([[packed]], [0], name="rs_small_sib")
    part = _rs_add_sibling([packed], 0, land, my_c, name="rs_small_add")
    cut = max(len(landed) - 4, 0)
    handle, token = _rs_chips_start(part, name="rs_small_start")
    done = adams(landed[:cut], [token])
    part, land = _rs_chips_wait(handle, done, name="rs_small_wait")
    small_block = _sum_slots(part, land, slots, name="rs_sum_small")
    handle, token = _gather_start([small_block], [0], [], name="gather_small_start")
    done = adams(landed[cut:cut + 2], [token])
    (src,), (small_full,) = _gather_wait(handle, done, name="gather_small_wait")
    small_full = _place_own(src, small_full, 0, dev1, name="gather_small_own")
    handle, token = _gather_pass_start([small_full], [0], name="gather_small_pass")
    done = adams(landed[cut + 2:], [token])
    (small_full,) = _gather_pass_wait(handle, done, name="gather_small_passwait")

    out_g, out_d, out_m, out_v = {}, {}, {}, {}
    for k, _ in _BIG:
        out_g[k], out_d[k], out_m[k], out_v[k] = big_out[k]
    small_g = dict(zip(_SMALL, _unpack(small_full, small_shapes)))
    for k, ax in _SMALL_SHARD_AXIS.items():
        small_g[k] = _my_shard(small_g[k], ax, dev)
    shapes_local = [w[k].shape for k in _SMALL]
    d_p, m_p, v_p = _adam_flat(packed_state[0], _pack([small_g[k] for k in _SMALL]), packed_state[1], packed_state[2],
                               name="adam_small")
    for k, dd, mm, vv in zip(_SMALL, _unpack(d_p, shapes_local), _unpack(m_p, shapes_local), _unpack(v_p, shapes_local)):
        out_g[k], out_d[k], out_m[k], out_v[k] = small_g[k], dd, mm, vv

    return (loss, grad_x, *[out_g[k] for k in names], *[out_d[k] for k in names], *[out_m[k] for k in names],
            *[out_v[k] for k in names])


def small_shape_full(w_shard, axis):
    return w_shard.shape[:axis] + (N_DEV * w_shard.shape[axis],) + w_shard.shape[axis + 1:]
```

```python
import jax
import jax.numpy as jnp
from jax import lax
from jax.experimental import pallas as pl
from jax.experimental.pallas import tpu as pltpu

F32 = jnp.float32
BF16 = jnp.bfloat16
MESH = pl.DeviceIdType.MESH

N_DEV = 8
N_CHIP = 4
RMS_EPS = 1e-6
LRU_C = 8.0
CONV_WIDTH = 4
POOL_WINDOWS = (2, 4, 8, 16)
HIST = 16
ADAM_LR = 0.001
ADAM_B1 = 0.9
ADAM_B2 = 0.999
ADAM_EPS = 1e-08
ADAM_WD = 0.01
ADAM_STEP = 10
ROW_ALIGN = 128
V7X_VMEM_LIMIT = 56 * 1024 * 1024
BF16_ROWS = 16
K_TILE = 5632
DN_K_TILE = 2816
WGRAD_TILE = 2048
PREFETCH = 3
SCAN_UNROLL = 8
EPILOGUE_ROWS = 528
LANES = 128


def _tile(n, cap, mult=BF16_ROWS):
    best = None
    d = mult
    while d <= min(n, cap):
        if n % d == 0:
            best = d
        d += mult
    if best is None:
        raise ValueError(f"no tile for {n} (cap {cap}, multiple of {mult})")
    return best


def _row_mult(dtype):
    return 8 * 4 // jnp.dtype(dtype).itemsize


def _params(sem=None):
    return pltpu.CompilerParams(dimension_semantics=sem, vmem_limit_bytes=V7X_VMEM_LIMIT)


def _dot(a, b, mode):
    dims = {"nn": ((1,), (0,)), "nt": ((1,), (1,)), "tn": ((0,), (0,))}[mode]
    return lax.dot_general(a, b, (dims, ((), ())), preferred_element_type=F32)


def _sigmoid(x):
    return 1.0 / (1.0 + jnp.exp(-x))


_GELU_C = 0.7978845608028654
_GELU_K = 0.044715


def _gelu(x):
    return 0.5 * x * (1.0 + jnp.tanh(_GELU_C * (x + _GELU_K * x * x * x)))


def _gelu_and_grad(x):
    x2 = x * x
    th = jnp.tanh(_GELU_C * x * (1.0 + _GELU_K * x2))
    half = 0.5 * (1.0 + th)
    return x * half, half + 0.5 * x * (1.0 - th * th) * _GELU_C * (1.0 + 3.0 * _GELU_K * x2)


def _neg_expm1(x):
    p = 1.0 / 5040.0
    for c in (1.0 / 720.0, 1.0 / 120.0, 1.0 / 24.0, 1.0 / 6.0, 0.5, 1.0):
        p = p * x + c
    return jnp.where(x > -0.25, -(x * p), 1.0 - jnp.exp(x))


def _softplus_neg(p):
    e = jnp.exp(-jnp.abs(p))
    u = 1.0 + e
    l1p = jnp.where(u == 1.0, e, jnp.log(u) * e / (u - 1.0 + (u == 1.0).astype(F32)))
    return jnp.maximum(-p, 0.0) + l1p


def _operand_spec(arr, br, bc, ridx, cidx, layer, split):
    if split:
        ncb = arr.shape[2] // bc
        return pl.BlockSpec((None, br, bc), lambda i, j, k: (cidx(i, j, k) // ncb, ridx(i, j, k), cidx(i, j, k) % ncb))
    if layer is not None:
        return pl.BlockSpec((None, br, bc), lambda i, j, k: (layer, ridx(i, j, k), cidx(i, j, k)))
    return pl.BlockSpec((br, bc), lambda i, j, k: (ridx(i, j, k), cidx(i, j, k)))


def _matmul(a, b, *, mode, m, n, kdim, tm, tn, tk, out_dtype, name, scale=None, residual=None,
            b_layer=None, a_split=False, b_split=False, deps=()):
    nk = kdim // tk
    I = lambda i, j, k: i
    J = lambda i, j, k: j
    K = lambda i, j, k: k
    if mode == "nn":
        a_spec = _operand_spec(a, tm, tk, I, K, None, a_split)
        b_spec = _operand_spec(b, tk, tn, K, J, b_layer, b_split)
    elif mode == "nt":
        a_spec = _operand_spec(a, tm, tk, I, K, None, a_split)
        b_spec = _operand_spec(b, tn, tk, J, K, b_layer, b_split)
    else:
        a_spec = _operand_spec(a, tk, tm, K, I, None, a_split)
        b_spec = _operand_spec(b, tk, tn, K, J, b_layer, b_split)
    in_specs = [a_spec, b_spec]
    operands = [a, b]
    if residual is not None:
        in_specs.append(pl.BlockSpec((tm, tn), lambda i, j, k: (i, j)))
        operands.append(residual)
    in_specs += [pl.BlockSpec(memory_space=pl.ANY)] * len(deps)
    operands += list(deps)
    n_in = len(operands)

    def finish(acc, res_ref, o_ref):
        if scale is not None:
            acc = acc * scale
        if res_ref is not None:
            acc = acc + res_ref[...]
        o_ref[...] = acc.astype(o_ref.dtype)

    def body(*refs):
        a_ref, b_ref = refs[0], refs[1]
        res_ref = refs[2] if residual is not None else None
        o_ref = refs[n_in]
        part = _dot(a_ref[...], b_ref[...], mode)
        if nk == 1:
            finish(part, res_ref, o_ref)
        else:
            acc_ref = refs[-1]
            k = pl.program_id(2)

            @pl.when(k == 0)
            def _():
                acc_ref[...] = part

            @pl.when(jnp.logical_and(k > 0, k < nk - 1))
            def _():
                acc_ref[...] += part

            @pl.when(k == nk - 1)
            def _():
                finish(acc_ref[...] + part, res_ref, o_ref)

    return pl.pallas_call(
        body, name=name,
        out_shape=jax.ShapeDtypeStruct((m, n), out_dtype),
        grid=(m // tm, n // tn, nk),
        in_specs=in_specs,
        out_specs=pl.BlockSpec((tm, tn), lambda i, j, k: (i, j)),
        scratch_shapes=[pltpu.VMEM((tm, tn), F32)] if nk > 1 else [],
        compiler_params=_params(("parallel", "parallel", "arbitrary")),
    )(*operands)


def _ffn_in_fwd(n_act, w_in, layer, *, name, deps=()):
    tp, d = n_act.shape
    f = w_in.shape[2] // 2
    tm = _tile(tp, 2112)
    tn = _tile(f, 512, LANES)
    nj = f // tn

    ch = _tile(tm, EPILOGUE_ROWS)

    def body(n_ref, wg_ref, wu_ref, *rest):
        da_ref, a_ref = rest[len(deps):]
        for r in range(tm // ch):
            rows = pl.ds(r * ch, ch)
            x = n_ref[rows, :]
            g = _dot(x, wg_ref[...], "nn")
            u = _dot(x, wu_ref[...], "nn")
            s = 0.5 * jnp.tanh(0.5 * g) + 0.5
            sg = g * s
            da_ref[0, rows, :] = (u * s * (1.0 + g * (1.0 - s))).astype(BF16)
            da_ref[1, rows, :] = sg.astype(BF16)
            a_ref[rows, :] = (sg * u).astype(BF16)

    return pl.pallas_call(
        body, name=name,
        out_shape=(jax.ShapeDtypeStruct((2, tp, f), BF16), jax.ShapeDtypeStruct((tp, f), BF16)),
        grid=(tp // tm, nj),
        in_specs=[pl.BlockSpec((tm, d), lambda i, j: (i, 0)),
                  pl.BlockSpec((None, d, tn), lambda i, j: (layer, 0, j)),
                  pl.BlockSpec((None, d, tn), lambda i, j: (layer, 0, j + nj))] + [pl.BlockSpec(memory_space=pl.ANY)] * len(deps),
        out_specs=(pl.BlockSpec((2, tm, tn), lambda i, j: (0, i, j)), pl.BlockSpec((tm, tn), lambda i, j: (i, j))),
        compiler_params=_params(("parallel", "arbitrary")),
    )(n_act, w_in, w_in, *deps)


def _ffn_bwd_act(dhb, w_out, layer, gu, *, name, deps=()):
    tp, d = dhb.shape
    f = w_out.shape[1]
    tm = _tile(tp, 2112)
    tn = _tile(f, 512, LANES)

    ch = _tile(tm, EPILOGUE_ROWS)

    def body(dh_ref, w_ref, gu_ref, *rest):
        dz_ref = rest[len(deps)]
        for r in range(tm // ch):
            rows = pl.ds(r * ch, ch)
            da = 0.5 * _dot(dh_ref[rows, :], w_ref[...], "nt")
            dz_ref[0, rows, :] = (da * gu_ref[0, rows, :].astype(F32)).astype(BF16)
            dz_ref[1, rows, :] = (da * gu_ref[1, rows, :].astype(F32)).astype(BF16)

    return pl.pallas_call(
        body, name=name,
        out_shape=jax.ShapeDtypeStruct((2, tp, f), BF16),
        grid=(tp // tm, f // tn),
        in_specs=[pl.BlockSpec((tm, d), lambda i, j: (i, 0)),
                  pl.BlockSpec((None, tn, d), lambda i, j: (layer, j, 0)),
                  pl.BlockSpec((2, tm, tn), lambda i, j: (0, i, j))] + [pl.BlockSpec(memory_space=pl.ANY)] * len(deps),
        out_specs=pl.BlockSpec((2, tm, tn), lambda i, j: (0, i, j)),
        compiler_params=_params(("parallel", "arbitrary")),
    )(dhb, w_out, gu, *deps)


def _rms_fwd(h, g, layer, *, name, deps=()):
    tp, d = h.shape
    tr = _tile(tp, 528)

    def body(h_ref, g_ref, *rest):
        n_ref = rest[len(deps)]
        x = h_ref[...]
        r = lax.rsqrt(jnp.mean(x * x, axis=-1, keepdims=True) + RMS_EPS)
        n_ref[...] = (x * r * g_ref[...]).astype(BF16)

    return pl.pallas_call(
        body, name=name, out_shape=jax.ShapeDtypeStruct((tp, d), BF16), grid=(tp // tr,),
        in_specs=[pl.BlockSpec((tr, d), lambda i: (i, 0)), pl.BlockSpec((None, 1, d), lambda i: (layer, 0, 0))]
        + [pl.BlockSpec(memory_space=pl.ANY)] * len(deps),
        out_specs=pl.BlockSpec((tr, d), lambda i: (i, 0)),
        compiler_params=_params(("parallel",)),
    )(h, g, *deps)


def _rms_bwd(h, g, layer, dn, dres, *, name, deps=()):
    tp, d = h.shape
    tr = _tile(tp, 528)

    def body(h_ref, g_ref, dn_ref, dres_ref, *rest):
        dh_ref, dhb_ref, dg_ref = rest[len(deps):]
        x = h_ref[...]
        r = lax.rsqrt(jnp.mean(x * x, axis=-1, keepdims=True) + RMS_EPS)
        xhat = x * r
        dn_v = dn_ref[...]
        dxhat = dn_v * g_ref[...]
        dh = dres_ref[...] + r * (dxhat - xhat * jnp.mean(dxhat * xhat, axis=-1, keepdims=True))
        dh_ref[...] = dh
        dhb_ref[...] = dh.astype(BF16)
        part = jnp.sum(dn_v * xhat, axis=0, keepdims=True)

        @pl.when(pl.program_id(0) == 0)
        def _():
            dg_ref[...] = part

        @pl.when(pl.program_id(0) > 0)
        def _():
            dg_ref[...] += part

    return pl.pallas_call(
        body, name=name,
        out_shape=(jax.ShapeDtypeStruct((tp, d), F32), jax.ShapeDtypeStruct((tp, d), BF16), jax.ShapeDtypeStruct((1, d), F32)),
        grid=(tp // tr,),
        in_specs=[pl.BlockSpec((tr, d), lambda i: (i, 0)), pl.BlockSpec((None, 1, d), lambda i: (layer, 0, 0)),
                  pl.BlockSpec((tr, d), lambda i: (i, 0)), pl.BlockSpec((tr, d), lambda i: (i, 0))]
        + [pl.BlockSpec(memory_space=pl.ANY)] * len(deps),
        out_specs=(pl.BlockSpec((tr, d), lambda i: (i, 0)), pl.BlockSpec((tr, d), lambda i: (i, 0)),
                   pl.BlockSpec((1, d), lambda i: (0, 0))),
        compiler_params=_params(("arbitrary",)),
    )(h, g, dn, dres, *deps)


def _final_loss(h, g, tgt, n_meta, t_valid, *, name):
    tp, d = h.shape
    tr = _tile(tp, 528)

    def body(h_ref, g_ref, t_ref, dh_ref, dhb_ref, dg_ref, loss_ref):
        i = pl.program_id(0)
        x = h_ref[...]
        r = lax.rsqrt(jnp.mean(x * x, axis=-1, keepdims=True) + RMS_EPS)
        xhat = x * r
        gv = g_ref[...]
        row = i * tr + lax.broadcasted_iota(jnp.int32, (tr, 1), 0)
        valid = jnp.logical_and(row >= n_meta, row < t_valid)
        err = jnp.where(valid, xhat * gv - t_ref[...], 0.0)
        dy = err * (1.0 / d)
        dxhat = dy * gv
        dh = r * (dxhat - xhat * jnp.mean(dxhat * xhat, axis=-1, keepdims=True))
        dh_ref[...] = dh
        dhb_ref[...] = dh.astype(BF16)
        dg_part = jnp.sum(dy * xhat, axis=0, keepdims=True)
        loss_part = jnp.sum(jnp.sum(err * err, axis=1, keepdims=True), axis=0, keepdims=True) * (0.5 / d)

        @pl.when(i == 0)
        def _():
            dg_ref[...] = dg_part
            loss_ref[...] = loss_part

        @pl.when(i > 0)
        def _():
            dg_ref[...] += dg_part
            loss_ref[...] += loss_part

    return pl.pallas_call(
        body, name=name,
        out_shape=(jax.ShapeDtypeStruct((tp, d), F32), jax.ShapeDtypeStruct((tp, d), BF16),
                   jax.ShapeDtypeStruct((1, d), F32), jax.ShapeDtypeStruct((1, 1), F32)),
        grid=(tp // tr,),
        in_specs=[pl.BlockSpec((tr, d), lambda i: (i, 0)), pl.BlockSpec((1, d), lambda i: (0, 0)),
                  pl.BlockSpec((tr, d), lambda i: (i, 0))],
        out_specs=(pl.BlockSpec((tr, d), lambda i: (i, 0)), pl.BlockSpec((tr, d), lambda i: (i, 0)),
                   pl.BlockSpec((1, d), lambda i: (0, 0)), pl.BlockSpec((1, 1), lambda i: (0, 0))),
        compiler_params=_params(("arbitrary",)),
    )(h, g, tgt)


_MIX_PARAMS = ("conv_w", "conv_b", "wa", "ba", "wx", "bx", "a_param", "pool_w", "pool_b", "pool_scale")


def _mix_param_specs(mp, layer):
    def spec(a):
        return pl.BlockSpec((None,) + a.shape[1:], lambda *g: (layer,) + (0,) * (a.ndim - 1))
    return [spec(mp[k]) for k in _MIX_PARAMS]


def _mix_param_list(mp):
    return [mp[k] for k in _MIX_PARAMS]


def _lru_gates(xc, wa_h, ba_h, wx_h, bx_h, sp_h):
    xb = xc.astype(BF16)
    ra = _sigmoid(_dot(xb, wa_h, "nn") + ba_h)
    ii = _sigmoid(_dot(xb, wx_h, "nn") + bx_h)
    la = -LRU_C * ra * sp_h
    return ra, ii, la, jnp.exp(la)


def _shifted(x, k):
    return x if k == 0 else pltpu.roll(x, k % x.shape[0], axis=0)


def _conv_taps(xbuf, sl, tc):
    x = xbuf[:, sl]
    return [_shifted(x, CONV_WIDTH - 1 - k)[HIST:HIST + tc] for k in range(CONV_WIDTH)]


def _conv_fwd(taps, cw_ref, cb_ref, sl):
    xc = cb_ref[:, sl]
    for k in range(CONV_WIDTH):
        xc = xc + cw_ref[k:k + 1, sl] * taps[k]
    return xc


def _window_sum(x, win, direction):
    s = x
    step = 1
    while step < win:
        s = s + _shifted(s, direction * step)
        step *= 2
    return s


def _pool_delta(pbuf, cols, win, t0, tc):
    x = pbuf[:, cols]
    u = x[HIST:HIST + tc]
    s = _window_sum(x, win, 1)[HIST:HIST + tc]
    t = t0 + lax.broadcasted_iota(jnp.int32, (tc, 1), 0)
    inv_cnt = 1.0 / jnp.minimum(t + 1, win).astype(F32)
    return s * inv_cnt - u, inv_cnt


def _mix_fwd(z, mp, layer, *, name):
    tp = z.shape[0]
    dl = z.shape[1] // 3
    n_heads, hd = mp["wa"].shape[1], mp["wa"].shape[2]
    n_groups, gd = mp["pool_w"].shape[1], mp["pool_w"].shape[2]
    tc = _tile(tp, 384)

    def body(z_ref, cw_ref, cb_ref, wa_ref, ba_ref, wx_ref, bx_ref, ap_ref, pw_ref, pb_ref, ps_ref,
             m_ref, hs_ref, aux_ref, xbuf, pbuf, a_s, b_s, hcar):
        i = pl.program_id(0)

        @pl.when(i == 0)
        def _():
            xbuf[pl.ds(0, HIST), :] = jnp.zeros((HIST, dl), F32)
            pbuf[pl.ds(0, HIST), :] = jnp.zeros((HIST, dl), F32)
            hcar[...] = jnp.zeros((1, dl), F32)

        @pl.when(i > 0)
        def _():
            xbuf[pl.ds(0, HIST), :] = xbuf[pl.ds(tc, HIST), :]
            pbuf[pl.ds(0, HIST), :] = pbuf[pl.ds(tc, HIST), :]

        xbuf[pl.ds(HIST, tc), :] = z_ref[:, pl.ds(0, dl)]
        pbuf[pl.ds(HIST, tc), :] = z_ref[:, pl.ds(2 * dl, dl)]
        sp = _softplus_neg(ap_ref[...])
        for h in range(n_heads):
            sl = pl.ds(h * hd, hd)
            xc = _conv_fwd(_conv_taps(xbuf, sl, tc), cw_ref, cb_ref, sl)
            ra, ii, la, a = _lru_gates(xc, wa_ref[h], ba_ref[:, sl], wx_ref[h], bx_ref[:, sl], sp[:, h * hd:(h + 1) * hd])
            e = _neg_expm1(2.0 * la)
            a_s[:, sl] = a
            b_s[:, sl] = jnp.sqrt(e) * ii * xc
            for slot, val in ((_AUX_R, ra), (_AUX_I, ii), (_AUX_X, xc), (_AUX_E, e)):
                aux_ref[slot, :, sl] = val

        def step(t, hprev):
            hnew = a_s[pl.ds(t, 1), :] * hprev + b_s[pl.ds(t, 1), :]
            hs_ref[pl.ds(t, 1), :] = hnew
            return hnew

        hcar[...] = lax.fori_loop(0, tc, step, hcar[...], unroll=SCAN_UNROLL)
        for h in range(n_heads):
            sl = pl.ds(h * hd, hd)
            m_ref[:, sl] = (hs_ref[:, sl] * _gelu(z_ref[:, pl.ds(dl + h * hd, hd)])).astype(BF16)
        for g in range(n_groups):
            cols = pl.ds(g * gd, gd)
            dlt, _ = _pool_delta(pbuf, cols, POOL_WINDOWS[g], i * tc, tc)
            q = _dot(dlt.astype(BF16), pw_ref[g], "nn") + pb_ref[:, cols]
            m_ref[:, pl.ds(dl + g * gd, gd)] = (q * ps_ref[:, cols]).astype(BF16)

    return pl.pallas_call(
        body, name=name,
        out_shape=(jax.ShapeDtypeStruct((tp, 2 * dl), BF16), jax.ShapeDtypeStruct((tp, dl), F32),
                   jax.ShapeDtypeStruct((_AUX_N, tp, dl), F32)),
        grid=(tp // tc,),
        in_specs=[pl.BlockSpec((tc, 3 * dl), lambda i: (i, 0))] + _mix_param_specs(mp, layer),
        out_specs=(pl.BlockSpec((tc, 2 * dl), lambda i: (i, 0)), pl.BlockSpec((tc, dl), lambda i: (i, 0)),
                   pl.BlockSpec((_AUX_N, tc, dl), lambda i: (0, i, 0))),
        scratch_shapes=[pltpu.VMEM((HIST + tc, dl), F32), pltpu.VMEM((HIST + tc, dl), F32),
                        pltpu.VMEM((tc, dl), F32), pltpu.VMEM((tc, dl), F32), pltpu.VMEM((1, dl), F32)],
        compiler_params=_params(("arbitrary",)),
    )(z, *_mix_param_list(mp))


_MIX_GRADS = ("conv_w", "conv_b", "wa", "ba", "wx", "bx", "a_param", "pool_w", "pool_b", "pool_scale")
_AUX_R, _AUX_I, _AUX_X, _AUX_E, _AUX_N = 0, 1, 2, 3, 4


def _mix_bwd(z, hs, aux, dm, mp, layer, *, name):
    tp = z.shape[0]
    dl = z.shape[1] // 3
    n_heads, hd = mp["wa"].shape[1], mp["wa"].shape[2]
    n_groups, gd = mp["pool_w"].shape[1], mp["pool_w"].shape[2]
    tc = _tile(tp, 352)
    nc = tp // tc
    per = tc // HIST

    def body(z_ref, zp_ref, hs_ref, hsp_ref, aux_ref, dm_ref, cw_ref, cb_ref, wa_ref, ba_ref, wx_ref, bx_ref, ap_ref, pw_ref, pb_ref,
             ps_ref, dz_ref, dcw_ref, dcb_ref, dwa_ref, dba_ref, dwx_ref, dbx_ref, dap_ref, dpw_ref, dpb_ref, dps_ref,
             xbuf, pbuf, hbuf, dxbuf, ddbuf, a_s, lam_s, ccar):
        i = pl.program_id(0)
        ci = nc - 1 - i

        @pl.when(i == 0)
        def _():
            dxbuf[pl.ds(tc, HIST), :] = jnp.zeros((HIST, dl), F32)
            ddbuf[pl.ds(tc, HIST), :] = jnp.zeros((HIST, dl), F32)
            ccar[...] = jnp.zeros((1, dl), F32)
            for ref in (dcw_ref, dcb_ref, dwa_ref, dba_ref, dwx_ref, dbx_ref, dap_ref, dpw_ref, dpb_ref, dps_ref):
                ref[...] = jnp.zeros(ref.shape, F32)

        @pl.when(ci == 0)
        def _():
            xbuf[pl.ds(0, HIST), :] = jnp.zeros((HIST, dl), F32)
            pbuf[pl.ds(0, HIST), :] = jnp.zeros((HIST, dl), F32)
            hbuf[pl.ds(0, HIST), :] = jnp.zeros((HIST, dl), F32)

        @pl.when(ci > 0)
        def _():
            xbuf[pl.ds(0, HIST), :] = zp_ref[:, pl.ds(0, dl)]
            pbuf[pl.ds(0, HIST), :] = zp_ref[:, pl.ds(2 * dl, dl)]
            hbuf[pl.ds(0, HIST), :] = hsp_ref[...]

        xbuf[pl.ds(HIST, tc), :] = z_ref[:, pl.ds(0, dl)]
        pbuf[pl.ds(HIST, tc), :] = z_ref[:, pl.ds(2 * dl, dl)]
        hbuf[pl.ds(HIST, tc), :] = hs_ref[...]
        sp = _softplus_neg(ap_ref[...])

        for h in range(n_heads):
            sl = pl.ds(h * hd, hd)
            a_s[:, sl] = jnp.exp(-LRU_C * aux_ref[_AUX_R, :, sl] * sp[:, h * hd:(h + 1) * hd])
            gel, gel_grad = _gelu_and_grad(z_ref[:, pl.ds(dl + h * hd, hd)])
            dya = dm_ref[:, sl]
            lam_s[:, sl] = dya * gel
            dz_ref[:, pl.ds(dl + h * hd, hd)] = (dya * hs_ref[:, sl] * gel_grad).astype(BF16)

        def step(r, carry):
            t = tc - 1 - r
            lam = lam_s[pl.ds(t, 1), :] + carry
            lam_s[pl.ds(t, 1), :] = lam
            return a_s[pl.ds(t, 1), :] * lam

        ccar[...] = lax.fori_loop(0, tc, step, ccar[...], unroll=SCAN_UNROLL)

        for h in range(n_heads):
            sl = pl.ds(h * hd, hd)
            sp_h = sp[:, h * hd:(h + 1) * hd]
            lam = lam_s[:, sl]
            a = a_s[:, sl]
            ra = aux_ref[_AUX_R, :, sl]
            ii = aux_ref[_AUX_I, :, sl]
            xc = aux_ref[_AUX_X, :, sl]
            e = aux_ref[_AUX_E, :, sl]
            inv_mult = lax.rsqrt(e)
            mult = e * inv_mult
            hprev = _shifted(hbuf[:, sl], 1)[HIST:HIST + tc]
            lam_i = lam * ii
            lam_m = lam * mult
            dla = lam * hprev * a - lam_i * xc * (a * a) * inv_mult
            dla_r = dla * (-LRU_C) * ra
            dap_ref[:, sl] += jnp.sum(dla_r, axis=0, keepdims=True)
            dpa = dla_r * sp_h * (1.0 - ra)
            dpx = lam_m * xc * ii * (1.0 - ii)
            dba_ref[:, sl] += jnp.sum(dpa, axis=0, keepdims=True)
            dbx_ref[:, sl] += jnp.sum(dpx, axis=0, keepdims=True)
            xb = xc.astype(BF16)
            dpa_b = dpa.astype(BF16)
            dpx_b = dpx.astype(BF16)
            dwa_ref[h] += _dot(xb, dpa_b, "tn")
            dwx_ref[h] += _dot(xb, dpx_b, "tn")
            dxc = lam_m * ii + _dot(dpa_b, wa_ref[h], "nt") + _dot(dpx_b, wx_ref[h], "nt")
            dxbuf[pl.ds(0, tc), sl] = dxc
            dcb_ref[:, sl] += jnp.sum(dxc, axis=0, keepdims=True)
            taps = _conv_taps(xbuf, sl, tc)
            dx_all = dxbuf[:, sl]
            dzx = jnp.zeros((tc, hd), F32)
            for k in range(CONV_WIDTH):
                dcw_ref[k:k + 1, sl] += jnp.sum(dxc * taps[k], axis=0, keepdims=True)
                dzx = dzx + cw_ref[k:k + 1, sl] * _shifted(dx_all, k - (CONV_WIDTH - 1))[0:tc]
            dz_ref[:, sl] = dzx.astype(BF16)
            dxbuf[pl.ds(tc, HIST), sl] = dxbuf[pl.ds(0, HIST), sl]

        for g in range(n_groups):
            cols = pl.ds(g * gd, gd)
            win = POOL_WINDOWS[g]
            dlt, inv_cnt = _pool_delta(pbuf, cols, win, ci * tc, tc)
            db = dlt.astype(BF16)
            q = _dot(db, pw_ref[g], "nn") + pb_ref[:, cols]
            dyb = dm_ref[:, pl.ds(dl + g * gd, gd)]
            dps_ref[:, cols] += jnp.sum(dyb * q, axis=0, keepdims=True)
            dq = dyb * ps_ref[:, cols]
            dpb_ref[:, cols] += jnp.sum(dq, axis=0, keepdims=True)
            dqb = dq.astype(BF16)
            dpw_ref[g] += _dot(db, dqb, "tn")
            dd = _dot(dqb, pw_ref[g], "nt")
            ddbuf[pl.ds(0, tc), cols] = dd * inv_cnt
            dzp = _window_sum(ddbuf[:, cols], win, -1)[0:tc] - dd
            dz_ref[:, pl.ds(2 * dl + g * gd, gd)] = dzp.astype(BF16)
            ddbuf[pl.ds(tc, HIST), cols] = ddbuf[pl.ds(0, HIST), cols]

        @pl.when(i == nc - 1)
        def _():
            dap_ref[...] = dap_ref[...] * (-_sigmoid(-ap_ref[...]))

    rev = lambda i: (nc - 1 - i, 0)
    prev = lambda i: (jnp.maximum((nc - 1 - i) * per - 1, 0), 0)
    const = lambda a: pl.BlockSpec(a.shape[1:], lambda i: (0,) * (a.ndim - 1))
    plist = _mix_param_list(mp)
    grad_shapes = [jax.ShapeDtypeStruct(a.shape[1:], F32) for a in plist]
    buf = lambda rows: pltpu.VMEM((rows, dl), F32)
    outs = pl.pallas_call(
        body, name=name,
        out_shape=[jax.ShapeDtypeStruct((tp, 3 * dl), BF16)] + grad_shapes,
        grid=(nc,),
        in_specs=[pl.BlockSpec((tc, 3 * dl), rev), pl.BlockSpec((HIST, 3 * dl), prev),
                  pl.BlockSpec((tc, dl), rev), pl.BlockSpec((HIST, dl), prev),
                  pl.BlockSpec((_AUX_N, tc, dl), lambda i: (0, nc - 1 - i, 0)),
                  pl.BlockSpec((tc, 2 * dl), rev)] + _mix_param_specs(mp, layer),
        out_specs=[pl.BlockSpec((tc, 3 * dl), rev)] + [const(a) for a in plist],
        scratch_shapes=[buf(HIST + tc), buf(HIST + tc), buf(HIST + tc), buf(tc + HIST), buf(tc + HIST),
                        buf(tc), buf(tc), buf(1)],
        compiler_params=_params(("arbitrary",)),
    )(z, z, hs, hs, aux, dm, *plist)
    return outs[0], dict(zip(_MIX_GRADS, outs[1:]))


def _ffn_fwd_out(h, act, w_out, tag, deps):
    tp, d = h.shape
    f = w_out.shape[1]
    return _matmul(act, w_out, mode="nn", m=tp, n=d, kdim=f, tm=_tile(tp, 1056), tn=_tile(d, 512, LANES),
                   tk=_tile(f, K_TILE, LANES), out_dtype=F32, scale=0.5, residual=h, b_layer=0, name=f"{tag}_out", deps=deps)


def _ffn_bwd(dh, dhb, saved, norm, layer, w_in, w_out, tag, emit, deps):
    h, n_act, gu, act = saved
    tp, d = h.shape
    f = w_out.shape[1]
    dz = _ffn_bwd_act(dhb, w_out, 0, gu, name=f"{tag}_bact", deps=deps)
    tok = emit("w_out", _matmul(act, dhb, mode="tn", m=f, n=d, kdim=tp, tm=_tile(f, 512, LANES), tn=_tile(d, WGRAD_TILE, LANES), tk=tp,
                                out_dtype=BF16, scale=0.5, name=f"{tag}_dwout", deps=[dz]))
    tok = emit("w_in", _matmul(n_act, dz, mode="tn", m=d, n=2 * f, kdim=tp, tm=_tile(d, WGRAD_TILE, LANES), tn=_tile(f, 512, LANES),
                               tk=tp, out_dtype=BF16, b_split=True, name=f"{tag}_dwin", deps=tok))
    dn = _matmul(dz, w_in, mode="nt", m=tp, n=d, kdim=2 * f, tm=_tile(tp, 1056), tn=_tile(d, 1024, LANES), tk=_tile(f, DN_K_TILE, LANES),
                 out_dtype=F32, a_split=True, b_layer=0, name=f"{tag}_dn", deps=tok)
    return _rms_bwd(h, norm, layer, dn, dh, name=f"{tag}_brms", deps=emit("flush", dn))


def _mix_block_fwd(h, n_act, layer, w_in, w_out, mp, tag, deps, before_out):
    tp, d = h.shape
    d_in = w_in.shape[2]
    z = _matmul(n_act, w_in, mode="nn", m=tp, n=d_in, kdim=d, tm=_tile(tp, 2112), tn=_tile(d_in, 512, LANES), tk=d,
                out_dtype=F32, b_layer=0, name=f"{tag}_in", deps=deps)
    m_act, hs, aux = _mix_fwd(z, mp, layer, name=f"{tag}_mix")
    h_out = _matmul(m_act, w_out, mode="nn", m=tp, n=d, kdim=d, tm=_tile(tp, 2112), tn=_tile(d, 512, LANES), tk=d,
                    out_dtype=F32, residual=h, b_layer=0, name=f"{tag}_out", deps=before_out(m_act))
    return h_out, (h, n_act, z, hs, aux, m_act)


def _mix_block_bwd(dh, dhb, saved, norm, layer, w_in, w_out, mp, tag, emit, deps):
    h, n_act, z, hs, aux, m_act = saved
    tp, d = h.shape
    d_in = w_in.shape[2]
    dm = _matmul(dhb, w_out, mode="nt", m=tp, n=d, kdim=d, tm=_tile(tp, 2112), tn=_tile(d, 512, LANES), tk=d,
                 out_dtype=F32, b_layer=0, name=f"{tag}_dm", deps=deps)
    tok = emit("w_out", _matmul(m_act, dhb, mode="tn", m=d, n=d, kdim=tp, tm=_tile(d, 512, LANES), tn=_tile(d, WGRAD_TILE, LANES), tk=tp,
                                out_dtype=BF16, name=f"{tag}_dwout", deps=[dm]))
    dz, g_mix = _mix_bwd(z, hs, aux, dm, mp, layer, name=f"{tag}_bmix")
    tok = emit("w_in", _matmul(n_act, dz, mode="tn", m=d, n=d_in, kdim=tp, tm=_tile(d, WGRAD_TILE, LANES), tn=_tile(d_in, 512, LANES),
                               tk=tp, out_dtype=BF16, name=f"{tag}_dwin", deps=tok))
    dn = _matmul(dz, w_in, mode="nt", m=tp, n=d, kdim=d_in, tm=_tile(tp, 1056), tn=_tile(d, 1024, LANES), tk=d_in,
                 out_dtype=F32, b_layer=0, name=f"{tag}_dn", deps=tok)
    dh_in, dhb_in, g_norm = _rms_bwd(h, norm, layer, dn, dh, name=f"{tag}_brms", deps=emit("flush", dn))
    return dh_in, dhb_in, g_norm, g_mix


def _mesh_pos():
    x, y, c = lax.axis_index("x"), lax.axis_index("y"), lax.axis_index("c")
    chips = [(1 - x, y), (x, 1 - y), (1 - x, 1 - y)]
    return x, y, c, chips


def _block(ref, axis, j, size):
    idx = [slice(None)] * len(ref.shape)
    idx[axis] = pl.ds(j * size, size)
    return ref.at[tuple(idx)]


_HBM = pl.BlockSpec(memory_space=pltpu.HBM)
_SEM = pl.BlockSpec(memory_space=pltpu.SEMAPHORE)
_ANY = pl.BlockSpec(memory_space=pl.ANY)
_EFFECT = pltpu.SideEffectType.DATAFLOW_SIDE_EFFECTING
_N_OUT = 4


def _in_hbm(a):
    return pltpu.with_memory_space_constraint(a, pltpu.HBM)


def _gather_start(shards, axes, deps, *, name, lands=None):
    n = len(shards)
    sizes = [s.shape[ax] for s, ax in zip(shards, axes)]
    full_shapes = [s.shape[:ax] + (N_DEV * s.shape[ax],) + s.shape[ax + 1:] for s, ax in zip(shards, axes)]

    def body(*refs):
        srcs, lands = refs[:n], refs[n:2 * n]
        send_sems, recv_sems = refs[2 * n + len(deps)], refs[2 * n + len(deps) + 1]
        token = refs[-1]
        x, y, c, chips = _mesh_pos()
        targets = [(x, y, 1 - c)] + [(*chip, c) for chip in chips]
        for a in range(n):
            dst = _block(lands[a], axes[a], 4 * x + 2 * y + c, sizes[a])
            for k, to in enumerate(targets):
                pltpu.make_async_remote_copy(src_ref=srcs[a], dst_ref=dst, send_sem=send_sems.at[_N_OUT * a + k],
                                             recv_sem=recv_sems.at[_N_OUT * a + k],
                                             device_id=to, device_id_type=MESH).start()
        token[...] = jnp.zeros(token.shape, F32)

    lands0 = [lax.empty(fs, s.dtype) for fs, s in zip(full_shapes, shards)] if lands is None else lands
    outs = pl.pallas_call(
        body, name=name,
        out_shape=(pltpu.SemaphoreType.DMA((n * _N_OUT,)), pltpu.SemaphoreType.DMA((n * _N_OUT,)),
                   *[pltpu.HBM(s.shape, s.dtype) for s in shards], *[pltpu.HBM(fs, s.dtype) for fs, s in zip(full_shapes, shards)],
                   jax.ShapeDtypeStruct((8, LANES), F32)),
        in_specs=[_HBM] * (2 * n) + [_ANY] * len(deps),
        out_specs=(_SEM, _SEM, *[_HBM] * (2 * n), pl.BlockSpec(memory_space=pltpu.VMEM)),
        input_output_aliases={a: 2 + a for a in range(2 * n)},
        compiler_params=pltpu.CompilerParams(has_side_effects=_EFFECT),
    )(*[_in_hbm(s) for s in shards], *[_in_hbm(z) for z in lands0], *deps)
    handle = dict(send_sems=outs[0], recv_sems=outs[1], srcs=list(outs[2:2 + n]), lands=list(outs[2 + n:2 + 2 * n]), axes=list(axes))
    return handle, outs[-1]


def _gather_wait(handle, after, *, name):
    srcs, lands, axes = handle["srcs"], handle["lands"], handle["axes"]
    n = len(srcs)
    sizes = [s.shape[ax] for s, ax in zip(srcs, axes)]

    def body(*refs):
        src_refs, land_refs = refs[:n], refs[n:2 * n]
        send_sems, recv_sems = refs[2 * n], refs[2 * n + 1]
        x, y, c, chips = _mesh_pos()
        peers = [(x, y, 1 - c)] + [(*chip, c) for chip in chips]
        for a in range(n):
            for k, dev in enumerate(peers):
                cp = pltpu.make_async_remote_copy(
                    src_ref=src_refs[a], dst_ref=_block(land_refs[a], axes[a], 4 * dev[0] + 2 * dev[1] + dev[2], sizes[a]),
                    send_sem=send_sems.at[_N_OUT * a + k], recv_sem=recv_sems.at[_N_OUT * a + k], device_id=dev,
                    device_id_type=MESH)
                cp.wait_send()
                cp.wait_recv()

    outs = pl.pallas_call(
        body, name=name,
        out_shape=(*[pltpu.HBM(s.shape, s.dtype) for s in srcs], *[pltpu.HBM(z.shape, z.dtype) for z in lands]),
        in_specs=[_HBM] * (2 * n) + [_SEM, _SEM] + [_ANY] * len(after),
        out_specs=tuple([_HBM] * (2 * n)),
        input_output_aliases={a: a for a in range(2 * n)},
        compiler_params=pltpu.CompilerParams(has_side_effects=_EFFECT),
    )(*srcs, *lands, handle["send_sems"], handle["recv_sems"], *after)
    return list(outs[:n]), list(outs[n:])


def _place_own(shard, full, axis, dev, *, name):
    nd = shard.ndim
    rows, cols = shard.shape[-2:]
    tr = _tile(rows, 512, _row_mult(shard.dtype)) if rows % _row_mult(shard.dtype) == 0 else rows
    nrb = rows // tr
    block = shard.shape[:-2] + (tr, cols)

    def in_map(r, dev_ref):
        return (0,) * (nd - 2) + (r, 0)

    def out_map(r, dev_ref):
        idx = [0] * nd
        idx[nd - 2] = r
        idx[axis] = dev_ref[0] * (nrb if axis == nd - 2 else 1) + idx[axis]
        return tuple(idx)

    def body(dev_ref, src_ref, full_ref, out_ref):
        out_ref[...] = src_ref[...]

    grid_spec = pltpu.PrefetchScalarGridSpec(
        num_scalar_prefetch=1, grid=(nrb,),
        in_specs=[pl.BlockSpec(block, in_map), _ANY], out_specs=pl.BlockSpec(block, out_map))
    return pl.pallas_call(body, name=name, grid_spec=grid_spec, out_shape=jax.ShapeDtypeStruct(full.shape, full.dtype),
                          input_output_aliases={2: 0}, compiler_params=_params(("arbitrary",)))(dev, shard, full)


def _gather_pass_start(fulls, axes, *, name):
    n = len(fulls)
    sizes = [z.shape[ax] // N_DEV for z, ax in zip(fulls, axes)]

    def body(*refs):
        srcs = refs[:n]
        send_sems, recv_sems = refs[n], refs[n + 1]
        token = refs[-1]
        x, y, c, chips = _mesh_pos()
        for a in range(n):
            for j, chip in enumerate(chips):
                blk = _block(srcs[a], axes[a], 4 * chip[0] + 2 * chip[1] + c, sizes[a])
                pltpu.make_async_remote_copy(src_ref=blk, dst_ref=blk, send_sem=send_sems.at[3 * a + j], recv_sem=recv_sems.at[3 * a + j],
                                             device_id=(x, y, 1 - c), device_id_type=MESH).start()
        token[...] = jnp.zeros(token.shape, F32)

    outs = pl.pallas_call(
        body, name=name,
        out_shape=(pltpu.SemaphoreType.DMA((3 * n,)), pltpu.SemaphoreType.DMA((3 * n,)), *[pltpu.HBM(z.shape, z.dtype) for z in fulls],
                   jax.ShapeDtypeStruct((8, LANES), F32)),
        in_specs=[_HBM] * n, out_specs=(_SEM, _SEM, *[_HBM] * n, pl.BlockSpec(memory_space=pltpu.VMEM)),
        input_output_aliases={a: 2 + a for a in range(n)},
        compiler_params=pltpu.CompilerParams(has_side_effects=_EFFECT),
    )(*[_in_hbm(z) for z in fulls])
    return dict(send_sems=outs[0], recv_sems=outs[1], fulls=list(outs[2:2 + n]), axes=list(axes)), outs[-1]


def _gather_pass_wait(handle, after, *, name):
    fulls, axes = handle["fulls"], handle["axes"]
    n = len(fulls)
    sizes = [z.shape[ax] // N_DEV for z, ax in zip(fulls, axes)]

    def body(*refs):
        bufs = refs[:n]
        send_sems, recv_sems = refs[n], refs[n + 1]
        x, y, c, chips = _mesh_pos()
        for a in range(n):
            for j, chip in enumerate(chips):
                cp = pltpu.make_async_remote_copy(
                    src_ref=_block(bufs[a], axes[a], 4 * chip[0] + 2 * chip[1] + c, sizes[a]),
                    dst_ref=_block(bufs[a], axes[a], 4 * chip[0] + 2 * chip[1] + (1 - c), sizes[a]),
                    send_sem=send_sems.at[3 * a + j], recv_sem=recv_sems.at[3 * a + j], device_id=(x, y, 1 - c),
                    device_id_type=MESH)
                cp.wait_send()
                cp.wait_recv()

    outs = pl.pallas_call(
        body, name=name,
        out_shape=tuple(pltpu.HBM(z.shape, z.dtype) for z in fulls),
        in_specs=[_HBM] * n + [_SEM, _SEM] + [_ANY] * len(after), out_specs=tuple([_HBM] * n),
        input_output_aliases={a: a for a in range(n)},
        compiler_params=pltpu.CompilerParams(has_side_effects=_EFFECT),
    )(*fulls, handle["send_sems"], handle["recv_sems"], *after)
    return list(outs)


def _rs_exchange_sibling(tensors, axes, *, name):
    n = len(tensors)
    n_layers = [len(t) for t in tensors]
    sizes = [t[0].shape[ax] // N_DEV for t, ax in zip(tensors, axes)]
    blk_shapes = [t[0].shape[:ax] + (sz,) + t[0].shape[ax + 1:] for t, ax, sz in zip(tensors, axes, sizes)]
    flat = [g for t in tensors for g in t]
    offs = [sum(n_layers[:a]) for a in range(n)]

    def body(*refs):
        srcs = refs[:len(flat)]
        lands = refs[len(flat):len(flat) + n]
        send_sems, recv_sems = refs[len(flat) + n:]
        x, y, c, _ = _mesh_pos()
        sib = (x, y, 1 - c)
        for a in range(n):
            for l in range(n_layers[a]):
                for i in range(N_CHIP):
                    pltpu.make_async_remote_copy(
                        src_ref=_block(srcs[offs[a] + l], axes[a], 2 * i + (1 - c), sizes[a]), dst_ref=lands[a].at[i, l],
                        send_sem=send_sems.at[a], recv_sem=recv_sems.at[a], device_id=sib, device_id_type=MESH).start()
        for a in range(n):
            pltpu.make_async_remote_copy(src_ref=lands[a], dst_ref=lands[a], send_sem=send_sems.at[a], recv_sem=recv_sems.at[a],
                                         device_id=sib, device_id_type=MESH).wait()

    any_spec = pl.BlockSpec(memory_space=pl.ANY)
    return pl.pallas_call(
        body, name=name,
        out_shape=[jax.ShapeDtypeStruct((N_CHIP, nl) + bs, t[0].dtype) for nl, bs, t in zip(n_layers, blk_shapes, tensors)],
        in_specs=[any_spec] * len(flat), out_specs=[any_spec] * n,
        scratch_shapes=[pltpu.SemaphoreType.DMA((n,)), pltpu.SemaphoreType.DMA((n,))],
    )(*flat)


def _rs_add_sibling(layers, axis, land, my_c, *, name):
    n_layers = len(layers)
    shape = layers[0].shape
    size = shape[axis] // N_DEV
    blk_shape = shape[:axis] + (size,) + shape[axis + 1:]
    nd = len(shape)
    rows = blk_shape[-2]
    tr = _tile(rows, 512, _row_mult(layers[0].dtype))
    inner = (tr, blk_shape[-1])
    lead = blk_shape[:-2]
    if lead:
        raise ValueError("blocked gradients are 2-D")
    nrb = rows // tr

    def src_map(l):
        def imap(i, r, c_ref):
            j = 2 * i + c_ref[0]
            return (j * nrb + r, 0) if axis == 0 else (r, j)
        return imap

    def body(c_ref, *refs):
        srcs = refs[:n_layers]
        land_ref = refs[n_layers]
        out_ref = refs[n_layers + 1]
        for l in range(n_layers):
            out_ref[l] = (srcs[l][...].astype(F32) + land_ref[l].astype(F32)).astype(out_ref.dtype)

    grid_spec = pltpu.PrefetchScalarGridSpec(
        num_scalar_prefetch=1, grid=(N_CHIP, nrb),
        in_specs=[pl.BlockSpec(inner, src_map(l)) for l in range(n_layers)]
        + [pl.BlockSpec((None, n_layers) + inner, lambda i, r, c_ref: (i, 0, r, 0))],
        out_specs=pl.BlockSpec((None, n_layers) + inner, lambda i, r, c_ref: (i, 0, r, 0)),
    )
    return pl.pallas_call(
        body, name=name, grid_spec=grid_spec,
        out_shape=jax.ShapeDtypeStruct((N_CHIP, n_layers) + blk_shape, layers[0].dtype),
        compiler_params=_params(("arbitrary", "arbitrary")),
    )(my_c, *layers, land)


def _rs_sibling_start(g, axis, *, name):
    size = g.shape[axis] // N_DEV
    land = lax.empty((N_CHIP, 1) + g.shape[:axis] + (size,) + g.shape[axis + 1:], g.dtype)

    def body(src, dst, send_sems, recv_sems, src_thru, dst_thru, token):
        x, y, c, _ = _mesh_pos()
        for i in range(N_CHIP):
            pltpu.make_async_remote_copy(src_ref=_block(src, axis, 2 * i + (1 - c), size), dst_ref=dst.at[i, 0], send_sem=send_sems.at[i],
                                         recv_sem=recv_sems.at[i], device_id=(x, y, 1 - c), device_id_type=MESH).start()
        token[...] = jnp.zeros(token.shape, F32)

    outs = pl.pallas_call(
        body, name=name,
        out_shape=(pltpu.SemaphoreType.DMA((N_CHIP,)), pltpu.SemaphoreType.DMA((N_CHIP,)), pltpu.HBM(g.shape, g.dtype),
                   pltpu.HBM(land.shape, land.dtype), jax.ShapeDtypeStruct((8, LANES), F32)),
        in_specs=[_HBM, _HBM], out_specs=(_SEM, _SEM, _HBM, _HBM, pl.BlockSpec(memory_space=pltpu.VMEM)),
        input_output_aliases={0: 2, 1: 3},
        compiler_params=pltpu.CompilerParams(has_side_effects=_EFFECT),
    )(_in_hbm(g), _in_hbm(land))
    return dict(send_sems=outs[0], recv_sems=outs[1], src=outs[2], land=outs[3], axis=axis), outs[4]


def _rs_sibling_wait(handle, after, *, name):
    src, land, axis = handle["src"], handle["land"], handle["axis"]
    size = src.shape[axis] // N_DEV

    def body(src_ref, dst_ref, send_sems, recv_sems, *rest):
        x, y, c, _ = _mesh_pos()
        for i in range(N_CHIP):
            cp = pltpu.make_async_remote_copy(src_ref=_block(src_ref, axis, 2 * i + (1 - c), size), dst_ref=dst_ref.at[i, 0],
                                              send_sem=send_sems.at[i], recv_sem=recv_sems.at[i], device_id=(x, y, 1 - c),
                                              device_id_type=MESH)
            cp.wait_send()
            cp.wait_recv()

    outs = pl.pallas_call(
        body, name=name,
        out_shape=(pltpu.HBM(src.shape, src.dtype), pltpu.HBM(land.shape, land.dtype)),
        in_specs=[_HBM, _HBM, _SEM, _SEM] + [_ANY] * len(after), out_specs=(_HBM, _HBM),
        input_output_aliases={0: 0, 1: 1},
        compiler_params=pltpu.CompilerParams(has_side_effects=_EFFECT),
    )(src, land, handle["send_sems"], handle["recv_sems"], *after)
    return outs[0], outs[1]


def _rs_chips_start(part, *, name):
    land = lax.empty(part.shape, part.dtype)

    def body(src, dst, send_sems, recv_sems, src_thru, dst_thru, token):
        x, y, c, chips = _mesh_pos()
        for j, chip in enumerate(chips):
            pltpu.make_async_remote_copy(src_ref=src.at[2 * chip[0] + chip[1]], dst_ref=dst.at[2 * x + y], send_sem=send_sems.at[j],
                                         recv_sem=recv_sems.at[j], device_id=(*chip, c), device_id_type=MESH).start()
        token[...] = jnp.zeros(token.shape, F32)

    outs = pl.pallas_call(
        body, name=name,
        out_shape=(pltpu.SemaphoreType.DMA((3,)), pltpu.SemaphoreType.DMA((3,)), pltpu.HBM(part.shape, part.dtype),
                   pltpu.HBM(land.shape, land.dtype), jax.ShapeDtypeStruct((8, LANES), F32)),
        in_specs=[_HBM, _HBM], out_specs=(_SEM, _SEM, _HBM, _HBM, pl.BlockSpec(memory_space=pltpu.VMEM)),
        input_output_aliases={0: 2, 1: 3},
        compiler_params=pltpu.CompilerParams(has_side_effects=_EFFECT),
    )(_in_hbm(part), _in_hbm(land))
    return dict(send_sems=outs[0], recv_sems=outs[1], src=outs[2], land=outs[3]), outs[4]


def _rs_chips_wait(handle, after, *, name):
    def body(src, dst, send_sems, recv_sems, *rest):
        x, y, c, chips = _mesh_pos()
        for j, chip in enumerate(chips):
            cp = pltpu.make_async_remote_copy(src_ref=src.at[2 * chip[0] + chip[1]], dst_ref=dst.at[2 * chip[0] + chip[1]],
                                              send_sem=send_sems.at[j], recv_sem=recv_sems.at[j], device_id=(*chip, c),
                                              device_id_type=MESH)
            cp.wait_send()
            cp.wait_recv()

    src, land = handle["src"], handle["land"]
    outs = pl.pallas_call(
        body, name=name,
        out_shape=(pltpu.HBM(src.shape, src.dtype), pltpu.HBM(land.shape, land.dtype)),
        in_specs=[_HBM, _HBM, _SEM, _SEM] + [_ANY] * len(after), out_specs=(_HBM, _HBM),
        input_output_aliases={0: 0, 1: 1},
        compiler_params=pltpu.CompilerParams(has_side_effects=_EFFECT),
    )(src, land, handle["send_sems"], handle["recv_sems"], *after)
    return outs[0], outs[1]


def _adam_math(w, g, m, v):
    m_new = ADAM_B1 * m + (1.0 - ADAM_B1) * g
    v_new = ADAM_B2 * v + (1.0 - ADAM_B2) * (g * g)
    m_hat = m_new / (1.0 - ADAM_B1 ** ADAM_STEP)
    v_hat = v_new / (1.0 - ADAM_B2 ** ADAM_STEP)
    delta = -ADAM_LR * (m_hat / (jnp.sqrt(v_hat) + ADAM_EPS) + ADAM_WD * w)
    return delta, m_new, v_new


def _sum_adam(part, land, slots, w, m, v, layer, prev, *, name, deps=()):
    n_layers, rows, cols = w.shape
    tr = _tile(rows, 256, _row_mult(land.dtype))
    n_prev = 0 if prev is None else 4

    def body(slots_ref, p0_ref, p1_ref, p2_ref, p3_ref, w_ref, m_ref, v_ref, *rest):
        g_ref, d_ref, mo_ref, vo_ref = rest[n_prev + len(deps):]
        g = p0_ref[...].astype(F32)
        for ref in (p1_ref, p2_ref, p3_ref):
            g = g + ref[...].astype(F32)
        delta, m_new, v_new = _adam_math(w_ref[...], g, m_ref[...], v_ref[...])
        g_ref[...] = g
        d_ref[...] = delta
        mo_ref[...] = m_new
        vo_ref[...] = v_new

    blk = pl.BlockSpec((None, tr, cols), lambda r, s: (layer, r, 0))
    slot = lambda j: pl.BlockSpec((None, None, tr, cols), lambda r, s: (s[j], 0, r, 0))
    shp = jax.ShapeDtypeStruct((n_layers, rows, cols), F32)
    grid_spec = pltpu.PrefetchScalarGridSpec(
        num_scalar_prefetch=1, grid=(rows // tr,),
        in_specs=[slot(0), slot(1), slot(2), slot(3), blk, blk, blk] + [_ANY] * (n_prev + len(deps)),
        out_specs=(blk, blk, blk, blk))
    return pl.pallas_call(
        body, name=name, out_shape=(shp, shp, shp, shp), grid_spec=grid_spec,
        input_output_aliases={8 + i: i for i in range(n_prev)},
        compiler_params=_params(("parallel",)),
    )(slots, part, land, land, land, w, m, v, *(prev or ()), *deps)


def _sum_slots(part, land, slots, *, name):
    _, _, rows, cols = land.shape
    tr = _tile(rows, 256, 8)

    def body(slots_ref, p0_ref, p1_ref, p2_ref, p3_ref, g_ref):
        g_ref[...] = p0_ref[...] + p1_ref[...] + p2_ref[...] + p3_ref[...]

    slot = lambda j: pl.BlockSpec((None, None, tr, cols), lambda r, s: (s[j], 0, r, 0))
    grid_spec = pltpu.PrefetchScalarGridSpec(
        num_scalar_prefetch=1, grid=(rows // tr,), in_specs=[slot(0), slot(1), slot(2), slot(3)],
        out_specs=pl.BlockSpec((tr, cols), lambda r, s: (r, 0)))
    return pl.pallas_call(body, name=name, out_shape=jax.ShapeDtypeStruct((rows, cols), F32), grid_spec=grid_spec,
                          compiler_params=_params(("parallel",)))(slots, part, land, land, land)


def _adam_flat(w, g, m, v, *, name):
    rows, cols = w.shape
    tr = _tile(rows, 256, 8)

    def body(w_ref, g_ref, m_ref, v_ref, d_ref, mo_ref, vo_ref):
        delta, m_new, v_new = _adam_math(w_ref[...], g_ref[...], m_ref[...], v_ref[...])
        d_ref[...] = delta
        mo_ref[...] = m_new
        vo_ref[...] = v_new

    blk = pl.BlockSpec((tr, cols), lambda r: (r, 0))
    shp = jax.ShapeDtypeStruct((rows, cols), F32)
    return pl.pallas_call(
        body, name=name, out_shape=(shp, shp, shp), grid=(rows // tr,), in_specs=[blk] * 4, out_specs=(blk, blk, blk),
        compiler_params=_params(("parallel",)),
    )(w, g, m, v)


_BIG = (("ffn1_w_in", 1), ("ffn1_w_out", 0), ("w_in", 1), ("w_out", 0), ("ffn2_w_in", 1), ("ffn2_w_out", 0))
_SMALL = ("meta_tokens", "ffn1_norm", "mix_norm", "conv_w", "conv_b", "lru_wa", "lru_ba", "lru_wx", "lru_bx", "lru_a_param",
          "pool_w", "pool_b", "pool_scale", "ffn2_norm", "final_norm")
_SMALL_SHARD_AXIS = {"meta_tokens": 1, "conv_w": 2, "pool_w": 2}
_PACK_COLS = 1024


def _pack(arrs):
    flat = jnp.concatenate([a.reshape(-1) for a in arrs])
    unit = N_DEV * 8 * _PACK_COLS
    total = -(-flat.shape[0] // unit) * unit
    flat = jnp.pad(flat, (0, total - flat.shape[0]))
    return flat.reshape(total // _PACK_COLS, _PACK_COLS)


def _unpack(packed, shapes):
    flat = packed.reshape(-1)
    out, off = [], 0
    for s in shapes:
        size = 1
        for v in s:
            size *= v
        out.append(flat[off:off + size].reshape(s))
        off += size
    return out


def _my_shard(full, axis, dev):
    size = full.shape[axis] // N_DEV
    return lax.dynamic_slice_in_dim(full, dev * size, size, axis)


def kernel(x, meta_tokens, ffn1_norm, ffn1_w_in, ffn1_w_out, mix_norm, w_in, conv_w, conv_b, lru_wa, lru_ba, lru_wx, lru_bx, lru_a_param, pool_w, pool_b, pool_scale, w_out, ffn2_norm, ffn2_w_in, ffn2_w_out, final_norm, loss_target, m_meta_tokens, m_ffn1_norm, m_ffn1_w_in, m_ffn1_w_out, m_mix_norm, m_w_in, m_conv_w, m_conv_b, m_lru_wa, m_lru_ba, m_lru_wx, m_lru_bx, m_lru_a_param, m_pool_w, m_pool_b, m_pool_scale, m_w_out, m_ffn2_norm, m_ffn2_w_in, m_ffn2_w_out, m_final_norm, v_meta_tokens, v_ffn1_norm, v_ffn1_w_in, v_ffn1_w_out, v_mix_norm, v_w_in, v_conv_w, v_conv_b, v_lru_wa, v_lru_ba, v_lru_wx, v_lru_bx, v_lru_a_param, v_pool_w, v_pool_b, v_pool_scale, v_w_out, v_ffn2_norm, v_ffn2_w_in, v_ffn2_w_out, v_final_norm):
    names = ("meta_tokens", "ffn1_norm", "ffn1_w_in", "ffn1_w_out", "mix_norm", "w_in", "conv_w", "conv_b", "lru_wa", "lru_ba",
             "lru_wx", "lru_bx", "lru_a_param", "pool_w", "pool_b", "pool_scale", "w_out", "ffn2_norm", "ffn2_w_in", "ffn2_w_out",
             "final_norm")
    w = dict(zip(names, (meta_tokens, ffn1_norm, ffn1_w_in, ffn1_w_out, mix_norm, w_in, conv_w, conv_b, lru_wa, lru_ba, lru_wx,
                         lru_bx, lru_a_param, pool_w, pool_b, pool_scale, w_out, ffn2_norm, ffn2_w_in, ffn2_w_out, final_norm)))
    mom = dict(zip(names, (m_meta_tokens, m_ffn1_norm, m_ffn1_w_in, m_ffn1_w_out, m_mix_norm, m_w_in, m_conv_w, m_conv_b, m_lru_wa,
                           m_lru_ba, m_lru_wx, m_lru_bx, m_lru_a_param, m_pool_w, m_pool_b, m_pool_scale, m_w_out, m_ffn2_norm,
                           m_ffn2_w_in, m_ffn2_w_out, m_final_norm)))
    vel = dict(zip(names, (v_meta_tokens, v_ffn1_norm, v_ffn1_w_in, v_ffn1_w_out, v_mix_norm, v_w_in, v_conv_w, v_conv_b, v_lru_wa,
                           v_lru_ba, v_lru_wx, v_lru_bx, v_lru_a_param, v_pool_w, v_pool_b, v_pool_scale, v_w_out, v_ffn2_norm,
                           v_ffn2_w_in, v_ffn2_w_out, v_final_norm)))
    n_layers, d = ffn1_norm.shape
    n_meta = meta_tokens.shape[0]
    seq = x.shape[1]
    t_valid = n_meta + seq
    tp = -(-t_valid // ROW_ALIGN) * ROW_ALIGN
    dev = 4 * lax.axis_index("x") + 2 * lax.axis_index("y") + lax.axis_index("c")
    my_c = lax.axis_index("c").astype(jnp.int32).reshape(1)
    dev1 = dev.astype(jnp.int32).reshape(1)
    mx, my = lax.axis_index("x"), lax.axis_index("y")
    slots = jnp.stack([2 * mx + my, 2 * (1 - mx) + my, 2 * mx + (1 - my), 2 * (1 - mx) + (1 - my)]).astype(jnp.int32)

    big_axis = dict(_BIG)
    vec = lambda a: a.reshape(a.shape[0], 1, a.shape[1])
    norms = dict(ffn1=vec(ffn1_norm), mix=vec(mix_norm), ffn2=vec(ffn2_norm))
    units = [(kind, l) for l in range(n_layers) for kind in ("ffn1", "mix", "ffn2")]

    handles = {}
    placed = {}
    state = dict(token=None, started=0, first=None)

    def behind(a):
        return a if state["first"] is None else lax.optimization_barrier((state["first"], a))[1]

    def shard(k, l):
        return behind(w[k][l:l + 1]).astype(BF16), big_axis[k] + 1

    groups = [[shard("ffn1_w_in", 0), (meta_tokens, 1)], [shard("ffn1_w_out", 0)]]

    def order():
        return [] if state["token"] is None else [state["token"]]

    def start_next():
        i = state["started"]
        if i < len(groups):
            handles[i], state["token"] = _gather_start([s for s, _ in groups[i]], [ax for _, ax in groups[i]], order(),
                                                       name=f"gather{i}_start", lands=placed.get(i))
            state["started"] = i + 1

    def finish_begin(i, after):
        handle = handles.pop(i)
        srcs, fulls = _gather_wait(handle, list(after) + order(), name=f"gather{i}_wait")
        if i not in placed:
            fulls = [_place_own(s, z, ax, dev1, name=f"gather{i}_own{a}") for a, (s, z, ax) in enumerate(zip(srcs, fulls, handle["axes"]))]
        passing, state["token"] = _gather_pass_start(fulls, handle["axes"], name=f"gather{i}_pass")
        return i, passing

    def finish_end(ctx, after):
        i, passing = ctx
        fulls = _gather_pass_wait(passing, after, name=f"gather{i}_passwait")
        start_next()
        return fulls

    def unit_group(kind, l):
        if kind == "mix":
            return [shard("w_in", l), shard("w_out", l)] + ([(behind(conv_w), 2), (behind(pool_w), 2)] if l == 0 else [])
        return [shard(f"{kind}_w_in", l), shard(f"{kind}_w_out", l)]

    groups.append(unit_group(*units[1]))
    for _ in range(PREFETCH):
        start_next()
    state["first"] = state["token"]
    groups += [unit_group(kind, l) for kind, l in units[2:]]
    for i in range(PREFETCH, len(groups)):
        placed[i] = [_place_own(s, lax.empty(s.shape[:ax] + (N_DEV * s.shape[ax],) + s.shape[ax + 1:], s.dtype), ax, dev1,
                                name=f"gather{i}_own{a}") for a, (s, ax) in enumerate(groups[i])]
    pad = jnp.zeros((tp - t_valid, d), F32)
    tgt = jnp.concatenate([jnp.zeros((n_meta, d), F32), behind(loss_target[0]), pad], axis=0)
    h0_base = jnp.concatenate([jnp.zeros((n_meta, d), F32), behind(x[0]), pad], axis=0)
    packed_state = [_pack([behind(t[k]) for k in _SMALL]) for t in (w, mom, vel)]
    ctx = finish_begin(0, [tgt, h0_base] + packed_state + [z for i in placed for z in placed[i]])
    w_in0, meta_full = finish_end(ctx, [])
    h0 = lax.dynamic_update_slice(h0_base, meta_full, (0, 0))
    n_act = _rms_fwd(h0, norms["ffn1"], 0, name="l0ffn1_rms", deps=order())
    gu, act = _ffn_in_fwd(n_act, w_in0, 0, name="l0ffn1_in")
    (w_out0,) = finish_end(finish_begin(1, [act]), [])

    def begin_next(unit_index, res):
        return finish_begin(unit_index + 2, [res]) if unit_index + 2 < len(groups) else None

    nxt = begin_next(0, act)
    h = _ffn_fwd_out(h0, act, w_out0, "l0ffn1", order())
    gathered = {units[0]: (w_in0, w_out0)}
    saved = {units[0]: (h0, n_act, gu, act)}
    mp = None
    for ui, (kind, l) in enumerate(units[1:], start=1):
        tag = f"l{l}{kind}"
        n_act = _rms_fwd(h, norms[kind], l, name=f"{tag}_rms", deps=order())
        fulls = finish_end(nxt, [n_act])
        if kind == "mix":
            if l == 0:
                mp = dict(conv_w=fulls[2], conv_b=vec(conv_b), wa=lru_wa.astype(BF16), ba=vec(lru_ba), wx=lru_wx.astype(BF16),
                          bx=vec(lru_bx), a_param=vec(lru_a_param), pool_w=fulls[3].astype(BF16), pool_b=vec(pool_b),
                          pool_scale=vec(pool_scale))
            begun = []

            def before_out(m_act, ui=ui):
                begun.append(begin_next(ui, m_act))
                return order()

            h, saved[(kind, l)] = _mix_block_fwd(h, n_act, l, fulls[0], fulls[1], mp, tag, order(), before_out)
            nxt = begun[0]
        else:
            gu, act = _ffn_in_fwd(n_act, fulls[0], 0, name=f"{tag}_in", deps=order())
            saved[(kind, l)] = (h, n_act, gu, act)
            nxt = begin_next(ui, act)
            h = _ffn_fwd_out(h, act, fulls[1], tag, order())
        gathered[(kind, l)] = (fulls[0], fulls[1])
    dh, dhb, g_final, loss_local = _final_loss(h, final_norm.reshape(1, d), tgt, n_meta, t_valid, name="final_loss")
    loss = lax.psum(loss_local[0, 0], ("x", "y", "c"))

    pending = []
    big_out = {}

    landed = []

    def drain(after):
        while pending:
            k, l, handle = pending.pop(0)
            landed.append((k, l) + _rs_chips_wait(handle, after, name=f"rs_{k}{l}_wait"))

    def adams(items, deps):
        for k, l, part, land in items:
            big_out[k] = _sum_adam(part, land, slots, w[k], mom[k], vel[k], l, big_out.get(k), name=f"adam_{k}{l}", deps=deps)
        return [big_out[items[-1][0]][0]] if items else []

    sib_pending = []

    def to_chips(k, l, g, land):
        part = _rs_add_sibling([g], big_axis[k], land, my_c, name=f"rs_{k}{l}_add")
        handle, token = _rs_chips_start(part, name=f"rs_{k}{l}_start")
        pending.append((k, l, handle))
        return token

    def finish_sibling(after):
        token = None
        while sib_pending:
            k, l, handle = sib_pending.pop(0)
            g, land = _rs_sibling_wait(handle, after, name=f"rs_{k}{l}_sibwait")
            token = to_chips(k, l, g, land)
        return token

    def emitter(kind, l):
        last_unit = (kind, l) == units[0]

        def emit(which, g):
            if which == "w_out":
                drain([g])
            token = finish_sibling([g])
            order_after = [] if token is None else [token]
            if which == "flush":
                return order_after
            k = which if kind == "mix" else f"{kind}_{which}"
            if last_unit and which == "w_in":
                (land,) = _rs_exchange_sibling([[g]], [big_axis[k]], name=f"rs_{k}{l}_sib")
                return order_after + [to_chips(k, l, g, land)]
            handle, token = _rs_sibling_start(g, big_axis[k], name=f"rs_{k}{l}_sibstart")
            sib_pending.append((k, l, handle))
            return order_after + [token]
        return emit

    g_norm = {k: [None] * n_layers for k in ("ffn1", "mix", "ffn2")}
    g_mix = [None] * n_layers
    for kind, l in reversed(units):
        w_i, w_o = gathered[(kind, l)]
        tag = f"l{l}{kind}"
        if kind == "mix":
            dh, dhb, g_norm[kind][l], g_mix[l] = _mix_block_bwd(dh, dhb, saved[(kind, l)], norms[kind], l, w_i, w_o, mp, tag,
                                                                 emitter(kind, l), [])
        else:
            dh, dhb, g_norm[kind][l] = _ffn_bwd(dh, dhb, saved[(kind, l)], norms[kind], l, w_i, w_o, tag, emitter(kind, l), [])
    drain([dh])
    dh0 = dh
    grad_x = dh0[n_meta:t_valid][None]

    mix_key = dict(conv_w="conv_w", conv_b="conv_b", lru_wa="wa", lru_ba="ba", lru_wx="wx", lru_bx="bx", lru_a_param="a_param",
                   pool_w="pool_w", pool_b="pool_b", pool_scale="pool_scale")
    small_local = {"meta_tokens": dh0[:n_meta], "final_norm": g_final.reshape(d)}
    for k in ("ffn1", "mix", "ffn2"):
        small_local[f"{k}_norm"] = jnp.stack([g.reshape(d) for g in g_norm[k]])
    for k, mk in mix_key.items():
        small_local[k] = jnp.stack([g_mix[l][mk] for l in range(n_layers)]).reshape(
            w[k].shape if k not in _SMALL_SHARD_AXIS else small_shape_full(w[k], _SMALL_SHARD_AXIS[k]))
    small_shapes = [small_local[k].shape for k in _SMALL]
    packed = _pack([small_local[k] for k in _SMALL])
    (land,) = _rs_exchange_sibling([[packed]], [0], name="rs_small_sib")
    part = _rs_add_sibling([packed], 0, land, my_c, name="rs_small_add")
    cut = max(len(landed) - 4, 0)
    handle, token = _rs_chips_start(part, name="rs_small_start")
    done = adams(landed[:cut], [token])
    part, land = _rs_chips_wait(handle, done, name="rs_small_wait")
    small_block = _sum_slots(part, land, slots, name="rs_sum_small")
    handle, token = _gather_start([small_block], [0], [], name="gather_small_start")
    done = adams(landed[cut:cut + 2], [token])
    (src,), (small_full,) = _gather_wait(handle, done, name="gather_small_wait")
    small_full = _place_own(src, small_full, 0, dev1, name="gather_small_own")
    handle, token = _gather_pass_start([small_full], [0], name="gather_small_pass")
    done = adams(landed[cut + 2:], [token])
    (small_full,) = _gather_pass_wait(handle, done, name="gather_small_passwait")

    out_g, out_d, out_m, out_v = {}, {}, {}, {}
    for k, _ in _BIG:
        out_g[k], out_d[k], out_m[k], out_v[k] = big_out[k]
    small_g = dict(zip(_SMALL, _unpack(small_full, small_shapes)))
    for k, ax in _SMALL_SHARD_AXIS.items():
        small_g[k] = _my_shard(small_g[k], ax, dev)
    shapes_local = [w[k].shape for k in _SMALL]
    d_p, m_p, v_p = _adam_flat(packed_state[0], _pack([small_g[k] for k in _SMALL]), packed_state[1], packed_state[2],
                               name="adam_small")
    for k, dd, mm, vv in zip(_SMALL, _unpack(d_p, shapes_local), _unpack(m_p, shapes_local), _unpack(v_p, shapes_local)):
        out_g[k], out_d[k], out_m[k], out_v[k] = small_g[k], dd, mm, vv

    return (loss, grad_x, *[out_g[k] for k in names], *[out_d[k] for k in names], *[out_m[k] for k in names],
            *[out_v[k] for k in names])


def small_shape_full(w_shard, axis):
    return w_shard.shape[:axis] + (N_DEV * w_shard.shape[axis],) + w_shard.shape[axis + 1:]
```

```python
import jax
import jax.numpy as jnp
from jax import lax
from jax.experimental import pallas as pl
from jax.experimental.pallas import tpu as pltpu

F32 = jnp.float32
BF16 = jnp.bfloat16
MESH = pl.DeviceIdType.MESH

N_DEV = 8
N_CHIP = 4
RMS_EPS = 1e-6
LRU_C = 8.0
CONV_WIDTH = 4
POOL_WINDOWS = (2, 4, 8, 16)
HIST = 16
ADAM_LR = 0.001
ADAM_B1 = 0.9
ADAM_B2 = 0.999
ADAM_EPS = 1e-08
ADAM_WD = 0.01
ADAM_STEP = 10
ROW_ALIGN = 128
V7X_VMEM_LIMIT = 56 * 1024 * 1024
BF16_ROWS = 16
K_TILE = 5632
DN_K_TILE = 2816
WGRAD_TILE = 2048
PREFETCH = 3
SCAN_UNROLL = 16
EPILOGUE_ROWS = 1056
LANES = 128


def _tile(n, cap, mult=BF16_ROWS):
    best = None
    d = mult
    while d <= min(n, cap):
        if n % d == 0:
            best = d
        d += mult
    if best is None:
        raise ValueError(f"no tile for {n} (cap {cap}, multiple of {mult})")
    return best


def _row_mult(dtype):
    return 8 * 4 // jnp.dtype(dtype).itemsize


def _params(sem=None):
    return pltpu.CompilerParams(dimension_semantics=sem, vmem_limit_bytes=V7X_VMEM_LIMIT)


def _dot(a, b, mode):
    dims = {"nn": ((1,), (0,)), "nt": ((1,), (1,)), "tn": ((0,), (0,))}[mode]
    return lax.dot_general(a, b, (dims, ((), ())), preferred_element_type=F32)


def _sigmoid(x):
    return 1.0 / (1.0 + jnp.exp(-x))


_GELU_C = 0.7978845608028654
_GELU_K = 0.044715


def _gelu(x):
    return 0.5 * x * (1.0 + jnp.tanh(_GELU_C * (x + _GELU_K * x * x * x)))


def _gelu_and_grad(x):
    x2 = x * x
    th = jnp.tanh(_GELU_C * x * (1.0 + _GELU_K * x2))
    half = 0.5 * (1.0 + th)
    return x * half, half + 0.5 * x * (1.0 - th * th) * _GELU_C * (1.0 + 3.0 * _GELU_K * x2)


def _neg_expm1(x):
    p = 1.0 / 5040.0
    for c in (1.0 / 720.0, 1.0 / 120.0, 1.0 / 24.0, 1.0 / 6.0, 0.5, 1.0):
        p = p * x + c
    return jnp.where(x > -0.25, -(x * p), 1.0 - jnp.exp(x))


def _softplus_neg(p):
    e = jnp.exp(-jnp.abs(p))
    u = 1.0 + e
    l1p = jnp.where(u == 1.0, e, jnp.log(u) * e / (u - 1.0 + (u == 1.0).astype(F32)))
    return jnp.maximum(-p, 0.0) + l1p


def _operand_spec(arr, br, bc, ridx, cidx, layer, split):
    if split:
        ncb = arr.shape[2] // bc
        return pl.BlockSpec((None, br, bc), lambda i, j, k: (cidx(i, j, k) // ncb, ridx(i, j, k), cidx(i, j, k) % ncb))
    if layer is not None:
        return pl.BlockSpec((None, br, bc), lambda i, j, k: (layer, ridx(i, j, k), cidx(i, j, k)))
    return pl.BlockSpec((br, bc), lambda i, j, k: (ridx(i, j, k), cidx(i, j, k)))


def _matmul(a, b, *, mode, m, n, kdim, tm, tn, tk, out_dtype, name, scale=None, residual=None,
            b_layer=None, a_split=False, b_split=False, deps=()):
    nk = kdim // tk
    I = lambda i, j, k: i
    J = lambda i, j, k: j
    K = lambda i, j, k: k
    if mode == "nn":
        a_spec = _operand_spec(a, tm, tk, I, K, None, a_split)
        b_spec = _operand_spec(b, tk, tn, K, J, b_layer, b_split)
    elif mode == "nt":
        a_spec = _operand_spec(a, tm, tk, I, K, None, a_split)
        b_spec = _operand_spec(b, tn, tk, J, K, b_layer, b_split)
    else:
        a_spec = _operand_spec(a, tk, tm, K, I, None, a_split)
        b_spec = _operand_spec(b, tk, tn, K, J, b_layer, b_split)
    in_specs = [a_spec, b_spec]
    operands = [a, b]
    if residual is not None:
        in_specs.append(pl.BlockSpec((tm, tn), lambda i, j, k: (i, j)))
        operands.append(residual)
    in_specs += [pl.BlockSpec(memory_space=pl.ANY)] * len(deps)
    operands += list(deps)
    n_in = len(operands)

    def finish(acc, res_ref, o_ref):
        if scale is not None:
            acc = acc * scale
        if res_ref is not None:
            acc = acc + res_ref[...]
        o_ref[...] = acc.astype(o_ref.dtype)

    def body(*refs):
        a_ref, b_ref = refs[0], refs[1]
        res_ref = refs[2] if residual is not None else None
        o_ref = refs[n_in]
        part = _dot(a_ref[...], b_ref[...], mode)
        if nk == 1:
            finish(part, res_ref, o_ref)
        else:
            acc_ref = refs[-1]
            k = pl.program_id(2)

            @pl.when(k == 0)
            def _():
                acc_ref[...] = part

            @pl.when(jnp.logical_and(k > 0, k < nk - 1))
            def _():
                acc_ref[...] += part

            @pl.when(k == nk - 1)
            def _():
                finish(acc_ref[...] + part, res_ref, o_ref)

    return pl.pallas_call(
        body, name=name,
        out_shape=jax.ShapeDtypeStruct((m, n), out_dtype),
        grid=(m // tm, n // tn, nk),
        in_specs=in_specs,
        out_specs=pl.BlockSpec((tm, tn), lambda i, j, k: (i, j)),
        scratch_shapes=[pltpu.VMEM((tm, tn), F32)] if nk > 1 else [],
        compiler_params=_params(("parallel", "parallel", "arbitrary")),
    )(*operands)


def _ffn_in_fwd(n_act, w_in, layer, *, name, deps=()):
    tp, d = n_act.shape
    f = w_in.shape[2] // 2
    tm = _tile(tp, 2112)
    tn = _tile(f, 512, LANES)
    nj = f // tn

    ch = _tile(tm, EPILOGUE_ROWS)

    def body(n_ref, wg_ref, wu_ref, *rest):
        da_ref, a_ref = rest[len(deps):]
        for r in range(tm // ch):
            rows = pl.ds(r * ch, ch)
            x = n_ref[rows, :]
            g = _dot(x, wg_ref[...], "nn")
            u = _dot(x, wu_ref[...], "nn")
            s = 0.5 * jnp.tanh(0.5 * g) + 0.5
            sg = g * s
            da_ref[0, rows, :] = (u * s * (1.0 + g * (1.0 - s))).astype(BF16)
            da_ref[1, rows, :] = sg.astype(BF16)
            a_ref[rows, :] = (sg * u).astype(BF16)

    return pl.pallas_call(
        body, name=name,
        out_shape=(jax.ShapeDtypeStruct((2, tp, f), BF16), jax.ShapeDtypeStruct((tp, f), BF16)),
        grid=(tp // tm, nj),
        in_specs=[pl.BlockSpec((tm, d), lambda i, j: (i, 0)),
                  pl.BlockSpec((None, d, tn), lambda i, j: (layer, 0, j)),
                  pl.BlockSpec((None, d, tn), lambda i, j: (layer, 0, j + nj))] + [pl.BlockSpec(memory_space=pl.ANY)] * len(deps),
        out_specs=(pl.BlockSpec((2, tm, tn), lambda i, j: (0, i, j)), pl.BlockSpec((tm, tn), lambda i, j: (i, j))),
        compiler_params=_params(("parallel", "arbitrary")),
    )(n_act, w_in, w_in, *deps)


def _ffn_bwd_act(dhb, w_out, layer, gu, *, name, deps=()):
    tp, d = dhb.shape
    f = w_out.shape[1]
    tm = _tile(tp, 2112)
    tn = _tile(f, 512, LANES)

    ch = _tile(tm, EPILOGUE_ROWS)

    def body(dh_ref, w_ref, gu_ref, *rest):
        dz_ref = rest[len(deps)]
        for r in range(tm // ch):
            rows = pl.ds(r * ch, ch)
            da = 0.5 * _dot(dh_ref[rows, :], w_ref[...], "nt")
            dz_ref[0, rows, :] = (da * gu_ref[0, rows, :].astype(F32)).astype(BF16)
            dz_ref[1, rows, :] = (da * gu_ref[1, rows, :].astype(F32)).astype(BF16)

    return pl.pallas_call(
        body, name=name,
        out_shape=jax.ShapeDtypeStruct((2, tp, f), BF16),
        grid=(tp // tm, f // tn),
        in_specs=[pl.BlockSpec((tm, d), lambda i, j: (i, 0)),
                  pl.BlockSpec((None, tn, d), lambda i, j: (layer, j, 0)),
                  pl.BlockSpec((2, tm, tn), lambda i, j: (0, i, j))] + [pl.BlockSpec(memory_space=pl.ANY)] * len(deps),
        out_specs=pl.BlockSpec((2, tm, tn), lambda i, j: (0, i, j)),
        compiler_params=_params(("parallel", "arbitrary")),
    )(dhb, w_out, gu, *deps)


def _rms_fwd(h, g, layer, *, name, deps=()):
    tp, d = h.shape
    tr = _tile(tp, 528)

    def body(h_ref, g_ref, *rest):
        n_ref = rest[len(deps)]
        x = h_ref[...]
        r = lax.rsqrt(jnp.mean(x * x, axis=-1, keepdims=True) + RMS_EPS)
        n_ref[...] = (x * r * g_ref[...]).astype(BF16)

    return pl.pallas_call(
        body, name=name, out_shape=jax.ShapeDtypeStruct((tp, d), BF16), grid=(tp // tr,),
        in_specs=[pl.BlockSpec((tr, d), lambda i: (i, 0)), pl.BlockSpec((None, 1, d), lambda i: (layer, 0, 0))]
        + [pl.BlockSpec(memory_space=pl.ANY)] * len(deps),
        out_specs=pl.BlockSpec((tr, d), lambda i: (i, 0)),
        compiler_params=_params(("parallel",)),
    )(h, g, *deps)


def _rms_bwd(h, g, layer, dn, dres, *, name, deps=()):
    tp, d = h.shape
    tr = _tile(tp, 528)

    def body(h_ref, g_ref, dn_ref, dres_ref, *rest):
        dh_ref, dhb_ref, dg_ref = rest[len(deps):]
        x = h_ref[...]
        r = lax.rsqrt(jnp.mean(x * x, axis=-1, keepdims=True) + RMS_EPS)
        xhat = x * r
        dn_v = dn_ref[...]
        dxhat = dn_v * g_ref[...]
        dh = dres_ref[...] + r * (dxhat - xhat * jnp.mean(dxhat * xhat, axis=-1, keepdims=True))
        dh_ref[...] = dh
        dhb_ref[...] = dh.astype(BF16)
        part = jnp.sum(dn_v * xhat, axis=0, keepdims=True)

        @pl.when(pl.program_id(0) == 0)
        def _():
            dg_ref[...] = part

        @pl.when(pl.program_id(0) > 0)
        def _():
            dg_ref[...] += part

    return pl.pallas_call(
        body, name=name,
        out_shape=(jax.ShapeDtypeStruct((tp, d), F32), jax.ShapeDtypeStruct((tp, d), BF16), jax.ShapeDtypeStruct((1, d), F32)),
        grid=(tp // tr,),
        in_specs=[pl.BlockSpec((tr, d), lambda i: (i, 0)), pl.BlockSpec((None, 1, d), lambda i: (layer, 0, 0)),
                  pl.BlockSpec((tr, d), lambda i: (i, 0)), pl.BlockSpec((tr, d), lambda i: (i, 0))]
        + [pl.BlockSpec(memory_space=pl.ANY)] * len(deps),
        out_specs=(pl.BlockSpec((tr, d), lambda i: (i, 0)), pl.BlockSpec((tr, d), lambda i: (i, 0)),
                   pl.BlockSpec((1, d), lambda i: (0, 0))),
        compiler_params=_params(("arbitrary",)),
    )(h, g, dn, dres, *deps)


def _final_loss(h, g, tgt, n_meta, t_valid, *, name):
    tp, d = h.shape
    tr = _tile(tp, 528)

    def body(h_ref, g_ref, t_ref, dh_ref, dhb_ref, dg_ref, loss_ref):
        i = pl.program_id(0)
        x = h_ref[...]
        r = lax.rsqrt(jnp.mean(x * x, axis=-1, keepdims=True) + RMS_EPS)
        xhat = x * r
        gv = g_ref[...]
        row = i * tr + lax.broadcasted_iota(jnp.int32, (tr, 1), 0)
        valid = jnp.logical_and(row >= n_meta, row < t_valid)
        err = jnp.where(valid, xhat * gv - t_ref[...], 0.0)
        dy = err * (1.0 / d)
        dxhat = dy * gv
        dh = r * (dxhat - xhat * jnp.mean(dxhat * xhat, axis=-1, keepdims=True))
        dh_ref[...] = dh
        dhb_ref[...] = dh.astype(BF16)
        dg_part = jnp.sum(dy * xhat, axis=0, keepdims=True)
        loss_part = jnp.sum(jnp.sum(err * err, axis=1, keepdims=True), axis=0, keepdims=True) * (0.5 / d)

        @pl.when(i == 0)
        def _():
            dg_ref[...] = dg_part
            loss_ref[...] = loss_part

        @pl.when(i > 0)
        def _():
            dg_ref[...] += dg_part
            loss_ref[...] += loss_part

    return pl.pallas_call(
        body, name=name,
        out_shape=(jax.ShapeDtypeStruct((tp, d), F32), jax.ShapeDtypeStruct((tp, d), BF16),
                   jax.ShapeDtypeStruct((1, d), F32), jax.ShapeDtypeStruct((1, 1), F32)),
        grid=(tp // tr,),
        in_specs=[pl.BlockSpec((tr, d), lambda i: (i, 0)), pl.BlockSpec((1, d), lambda i: (0, 0)),
                  pl.BlockSpec((tr, d), lambda i: (i, 0))],
        out_specs=(pl.BlockSpec((tr, d), lambda i: (i, 0)), pl.BlockSpec((tr, d), lambda i: (i, 0)),
                   pl.BlockSpec((1, d), lambda i: (0, 0)), pl.BlockSpec((1, 1), lambda i: (0, 0))),
        compiler_params=_params(("arbitrary",)),
    )(h, g, tgt)


_MIX_PARAMS = ("conv_w", "conv_b", "wa", "ba", "wx", "bx", "a_param", "pool_w", "pool_b", "pool_scale")


def _mix_param_specs(mp, layer):
    def spec(a):
        return pl.BlockSpec((None,) + a.shape[1:], lambda *g: (layer,) + (0,) * (a.ndim - 1))
    return [spec(mp[k]) for k in _MIX_PARAMS]


def _mix_param_list(mp):
    return [mp[k] for k in _MIX_PARAMS]


def _lru_gates(xc, wa_h, ba_h, wx_h, bx_h, sp_h):
    xb = xc.astype(BF16)
    ra = _sigmoid(_dot(xb, wa_h, "nn") + ba_h)
    ii = _sigmoid(_dot(xb, wx_h, "nn") + bx_h)
    la = -LRU_C * ra * sp_h
    return ra, ii, la, jnp.exp(la)


def _shifted(x, k):
    return x if k == 0 else pltpu.roll(x, k % x.shape[0], axis=0)


def _conv_taps(xbuf, sl, tc):
    x = xbuf[:, sl]
    return [_shifted(x, CONV_WIDTH - 1 - k)[HIST:HIST + tc] for k in range(CONV_WIDTH)]


def _conv_fwd(taps, cw_ref, cb_ref, sl):
    xc = cb_ref[:, sl]
    for k in range(CONV_WIDTH):
        xc = xc + cw_ref[k:k + 1, sl] * taps[k]
    return xc


def _window_sum(x, win, direction):
    s = x
    step = 1
    while step < win:
        s = s + _shifted(s, direction * step)
        step *= 2
    return s


def _pool_delta(pbuf, cols, win, t0, tc):
    x = pbuf[:, cols]
    u = x[HIST:HIST + tc]
    s = _window_sum(x, win, 1)[HIST:HIST + tc]
    t = t0 + lax.broadcasted_iota(jnp.int32, (tc, 1), 0)
    inv_cnt = 1.0 / jnp.minimum(t + 1, win).astype(F32)
    return s * inv_cnt - u, inv_cnt


def _mix_fwd(z, mp, layer, *, name):
    tp = z.shape[0]
    dl = z.shape[1] // 3
    n_heads, hd = mp["wa"].shape[1], mp["wa"].shape[2]
    n_groups, gd = mp["pool_w"].shape[1], mp["pool_w"].shape[2]
    tc = _tile(tp, 384)

    def body(z_ref, cw_ref, cb_ref, wa_ref, ba_ref, wx_ref, bx_ref, ap_ref, pw_ref, pb_ref, ps_ref,
             m_ref, hs_ref, aux_ref, xbuf, pbuf, a_s, b_s, hcar):
        i = pl.program_id(0)

        @pl.when(i == 0)
        def _():
            xbuf[pl.ds(0, HIST), :] = jnp.zeros((HIST, dl), F32)
            pbuf[pl.ds(0, HIST), :] = jnp.zeros((HIST, dl), F32)
            hcar[...] = jnp.zeros((1, dl), F32)

        @pl.when(i > 0)
        def _():
            xbuf[pl.ds(0, HIST), :] = xbuf[pl.ds(tc, HIST), :]
            pbuf[pl.ds(0, HIST), :] = pbuf[pl.ds(tc, HIST), :]

        xbuf[pl.ds(HIST, tc), :] = z_ref[:, pl.ds(0, dl)]
        pbuf[pl.ds(HIST, tc), :] = z_ref[:, pl.ds(2 * dl, dl)]
        sp = _softplus_neg(ap_ref[...])
        for h in range(n_heads):
            sl = pl.ds(h * hd, hd)
            xc = _conv_fwd(_conv_taps(xbuf, sl, tc), cw_ref, cb_ref, sl)
            ra, ii, la, a = _lru_gates(xc, wa_ref[h], ba_ref[:, sl], wx_ref[h], bx_ref[:, sl], sp[:, h * hd:(h + 1) * hd])
            e = _neg_expm1(2.0 * la)
            a_s[:, sl] = a
            b_s[:, sl] = jnp.sqrt(e) * ii * xc
            for slot, val in ((_AUX_R, ra), (_AUX_I, ii), (_AUX_X, xc), (_AUX_E, e)):
                aux_ref[slot, :, sl] = val

        def step(t, hprev):
            hnew = a_s[pl.ds(t, 1), :] * hprev + b_s[pl.ds(t, 1), :]
            hs_ref[pl.ds(t, 1), :] = hnew
            return hnew

        hcar[...] = lax.fori_loop(0, tc, step, hcar[...], unroll=SCAN_UNROLL)
        for h in range(n_heads):
            sl = pl.ds(h * hd, hd)
            m_ref[:, sl] = (hs_ref[:, sl] * _gelu(z_ref[:, pl.ds(dl + h * hd, hd)])).astype(BF16)
        for g in range(n_groups):
            cols = pl.ds(g * gd, gd)
            dlt, _ = _pool_delta(pbuf, cols, POOL_WINDOWS[g], i * tc, tc)
            q = _dot(dlt.astype(BF16), pw_ref[g], "nn") + pb_ref[:, cols]
            m_ref[:, pl.ds(dl + g * gd, gd)] = (q * ps_ref[:, cols]).astype(BF16)

    return pl.pallas_call(
        body, name=name,
        out_shape=(jax.ShapeDtypeStruct((tp, 2 * dl), BF16), jax.ShapeDtypeStruct((tp, dl), F32),
                   jax.ShapeDtypeStruct((_AUX_N, tp, dl), F32)),
        grid=(tp // tc,),
        in_specs=[pl.BlockSpec((tc, 3 * dl), lambda i: (i, 0))] + _mix_param_specs(mp, layer),
        out_specs=(pl.BlockSpec((tc, 2 * dl), lambda i: (i, 0)), pl.BlockSpec((tc, dl), lambda i: (i, 0)),
                   pl.BlockSpec((_AUX_N, tc, dl), lambda i: (0, i, 0))),
        scratch_shapes=[pltpu.VMEM((HIST + tc, dl), F32), pltpu.VMEM((HIST + tc, dl), F32),
                        pltpu.VMEM((tc, dl), F32), pltpu.VMEM((tc, dl), F32), pltpu.VMEM((1, dl), F32)],
        compiler_params=_params(("arbitrary",)),
    )(z, *_mix_param_list(mp))


_MIX_GRADS = ("conv_w", "conv_b", "wa", "ba", "wx", "bx", "a_param", "pool_w", "pool_b", "pool_scale")
_AUX_R, _AUX_I, _AUX_X, _AUX_E, _AUX_N = 0, 1, 2, 3, 4


def _mix_bwd(z, hs, aux, dm, mp, layer, *, name):
    tp = z.shape[0]
    dl = z.shape[1] // 3
    n_heads, hd = mp["wa"].shape[1], mp["wa"].shape[2]
    n_groups, gd = mp["pool_w"].shape[1], mp["pool_w"].shape[2]
    tc = _tile(tp, 352)
    nc = tp // tc
    per = tc // HIST

    def body(z_ref, zp_ref, hs_ref, hsp_ref, aux_ref, dm_ref, cw_ref, cb_ref, wa_ref, ba_ref, wx_ref, bx_ref, ap_ref, pw_ref, pb_ref,
             ps_ref, dz_ref, dcw_ref, dcb_ref, dwa_ref, dba_ref, dwx_ref, dbx_ref, dap_ref, dpw_ref, dpb_ref, dps_ref,
             xbuf, pbuf, hbuf, dxbuf, ddbuf, a_s, lam_s, ccar):
        i = pl.program_id(0)
        ci = nc - 1 - i

        @pl.when(i == 0)
        def _():
            dxbuf[pl.ds(tc, HIST), :] = jnp.zeros((HIST, dl), F32)
            ddbuf[pl.ds(tc, HIST), :] = jnp.zeros((HIST, dl), F32)
            ccar[...] = jnp.zeros((1, dl), F32)
            for ref in (dcw_ref, dcb_ref, dwa_ref, dba_ref, dwx_ref, dbx_ref, dap_ref, dpw_ref, dpb_ref, dps_ref):
                ref[...] = jnp.zeros(ref.shape, F32)

        @pl.when(ci == 0)
        def _():
            xbuf[pl.ds(0, HIST), :] = jnp.zeros((HIST, dl), F32)
            pbuf[pl.ds(0, HIST), :] = jnp.zeros((HIST, dl), F32)
            hbuf[pl.ds(0, HIST), :] = jnp.zeros((HIST, dl), F32)

        @pl.when(ci > 0)
        def _():
            xbuf[pl.ds(0, HIST), :] = zp_ref[:, pl.ds(0, dl)]
            pbuf[pl.ds(0, HIST), :] = zp_ref[:, pl.ds(2 * dl, dl)]
            hbuf[pl.ds(0, HIST), :] = hsp_ref[...]

        xbuf[pl.ds(HIST, tc), :] = z_ref[:, pl.ds(0, dl)]
        pbuf[pl.ds(HIST, tc), :] = z_ref[:, pl.ds(2 * dl, dl)]
        hbuf[pl.ds(HIST, tc), :] = hs_ref[...]
        sp = _softplus_neg(ap_ref[...])

        for h in range(n_heads):
            sl = pl.ds(h * hd, hd)
            a_s[:, sl] = jnp.exp(-LRU_C * aux_ref[_AUX_R, :, sl] * sp[:, h * hd:(h + 1) * hd])
            gel, gel_grad = _gelu_and_grad(z_ref[:, pl.ds(dl + h * hd, hd)])
            dya = dm_ref[:, sl]
            lam_s[:, sl] = dya * gel
            dz_ref[:, pl.ds(dl + h * hd, hd)] = (dya * hs_ref[:, sl] * gel_grad).astype(BF16)

        def step(r, carry):
            t = tc - 1 - r
            lam = lam_s[pl.ds(t, 1), :] + carry
            lam_s[pl.ds(t, 1), :] = lam
            return a_s[pl.ds(t, 1), :] * lam

        ccar[...] = lax.fori_loop(0, tc, step, ccar[...], unroll=SCAN_UNROLL)

        for h in range(n_heads):
            sl = pl.ds(h * hd, hd)
            sp_h = sp[:, h * hd:(h + 1) * hd]
            lam = lam_s[:, sl]
            a = a_s[:, sl]
            ra = aux_ref[_AUX_R, :, sl]
            ii = aux_ref[_AUX_I, :, sl]
            xc = aux_ref[_AUX_X, :, sl]
            e = aux_ref[_AUX_E, :, sl]
            inv_mult = lax.rsqrt(e)
            mult = e * inv_mult
            hprev = _shifted(hbuf[:, sl], 1)[HIST:HIST + tc]
            lam_i = lam * ii
            lam_m = lam * mult
            dla = lam * hprev * a - lam_i * xc * (a * a) * inv_mult
            dla_r = dla * (-LRU_C) * ra
            dap_ref[:, sl] += jnp.sum(dla_r, axis=0, keepdims=True)
            dpa = dla_r * sp_h * (1.0 - ra)
            dpx = lam_m * xc * ii * (1.0 - ii)
            dba_ref[:, sl] += jnp.sum(dpa, axis=0, keepdims=True)
            dbx_ref[:, sl] += jnp.sum(dpx, axis=0, keepdims=True)
            xb = xc.astype(BF16)
            dpa_b = dpa.astype(BF16)
            dpx_b = dpx.astype(BF16)
            dwa_ref[h] += _dot(xb, dpa_b, "tn")
            dwx_ref[h] += _dot(xb, dpx_b, "tn")
            dxc = lam_m * ii + _dot(dpa_b, wa_ref[h], "nt") + _dot(dpx_b, wx_ref[h], "nt")
            dxbuf[pl.ds(0, tc), sl] = dxc
            dcb_ref[:, sl] += jnp.sum(dxc, axis=0, keepdims=True)
            taps = _conv_taps(xbuf, sl, tc)
            dx_all = dxbuf[:, sl]
            dzx = jnp.zeros((tc, hd), F32)
            for k in range(CONV_WIDTH):
                dcw_ref[k:k + 1, sl] += jnp.sum(dxc * taps[k], axis=0, keepdims=True)
                dzx = dzx + cw_ref[k:k + 1, sl] * _shifted(dx_all, k - (CONV_WIDTH - 1))[0:tc]
            dz_ref[:, sl] = dzx.astype(BF16)
            dxbuf[pl.ds(tc, HIST), sl] = dxbuf[pl.ds(0, HIST), sl]

        for g in range(n_groups):
            cols = pl.ds(g * gd, gd)
            win = POOL_WINDOWS[g]
            dlt, inv_cnt = _pool_delta(pbuf, cols, win, ci * tc, tc)
            db = dlt.astype(BF16)
            q = _dot(db, pw_ref[g], "nn") + pb_ref[:, cols]
            dyb = dm_ref[:, pl.ds(dl + g * gd, gd)]
            dps_ref[:, cols] += jnp.sum(dyb * q, axis=0, keepdims=True)
            dq = dyb * ps_ref[:, cols]
            dpb_ref[:, cols] += jnp.sum(dq, axis=0, keepdims=True)
            dqb = dq.astype(BF16)
            dpw_ref[g] += _dot(db, dqb, "tn")
            dd = _dot(dqb, pw_ref[g], "nt")
            ddbuf[pl.ds(0, tc), cols] = dd * inv_cnt
            dzp = _window_sum(ddbuf[:, cols], win, -1)[0:tc] - dd
            dz_ref[:, pl.ds(2 * dl + g * gd, gd)] = dzp.astype(BF16)
            ddbuf[pl.ds(tc, HIST), cols] = ddbuf[pl.ds(0, HIST), cols]

        @pl.when(i == nc - 1)
        def _():
            dap_ref[...] = dap_ref[...] * (-_sigmoid(-ap_ref[...]))

    rev = lambda i: (nc - 1 - i, 0)
    prev = lambda i: (jnp.maximum((nc - 1 - i) * per - 1, 0), 0)
    const = lambda a: pl.BlockSpec(a.shape[1:], lambda i: (0,) * (a.ndim - 1))
    plist = _mix_param_list(mp)
    grad_shapes = [jax.ShapeDtypeStruct(a.shape[1:], F32) for a in plist]
    buf = lambda rows: pltpu.VMEM((rows, dl), F32)
    outs = pl.pallas_call(
        body, name=name,
        out_shape=[jax.ShapeDtypeStruct((tp, 3 * dl), BF16)] + grad_shapes,
        grid=(nc,),
        in_specs=[pl.BlockSpec((tc, 3 * dl), rev), pl.BlockSpec((HIST, 3 * dl), prev),
                  pl.BlockSpec((tc, dl), rev), pl.BlockSpec((HIST, dl), prev),
                  pl.BlockSpec((_AUX_N, tc, dl), lambda i: (0, nc - 1 - i, 0)),
                  pl.BlockSpec((tc, 2 * dl), rev)] + _mix_param_specs(mp, layer),
        out_specs=[pl.BlockSpec((tc, 3 * dl), rev)] + [const(a) for a in plist],
        scratch_shapes=[buf(HIST + tc), buf(HIST + tc), buf(HIST + tc), buf(tc + HIST), buf(tc + HIST),
                        buf(tc), buf(tc), buf(1)],
        compiler_params=_params(("arbitrary",)),
    )(z, z, hs, hs, aux, dm, *plist)
    return outs[0], dict(zip(_MIX_GRADS, outs[1:]))


def _ffn_fwd_out(h, act, w_out, tag, deps):
    tp, d = h.shape
    f = w_out.shape[1]
    return _matmul(act, w_out, mode="nn", m=tp, n=d, kdim=f, tm=_tile(tp, 1056), tn=_tile(d, 512, LANES),
                   tk=_tile(f, K_TILE, LANES), out_dtype=F32, scale=0.5, residual=h, b_layer=0, name=f"{tag}_out", deps=deps)


def _ffn_bwd(dh, dhb, saved, norm, layer, w_in, w_out, tag, emit, deps):
    h, n_act, gu, act = saved
    tp, d = h.shape
    f = w_out.shape[1]
    dz = _ffn_bwd_act(dhb, w_out, 0, gu, name=f"{tag}_bact", deps=deps)
    tok = emit("w_out", _matmul(act, dhb, mode="tn", m=f, n=d, kdim=tp, tm=_tile(f, 512, LANES), tn=_tile(d, WGRAD_TILE, LANES), tk=tp,
                                out_dtype=BF16, scale=0.5, name=f"{tag}_dwout", deps=[dz]))
    tok = emit("w_in", _matmul(n_act, dz, mode="tn", m=d, n=2 * f, kdim=tp, tm=_tile(d, WGRAD_TILE, LANES), tn=_tile(f, 512, LANES),
                               tk=tp, out_dtype=BF16, b_split=True, name=f"{tag}_dwin", deps=tok))
    dn = _matmul(dz, w_in, mode="nt", m=tp, n=d, kdim=2 * f, tm=_tile(tp, 1056), tn=_tile(d, 1024, LANES), tk=_tile(f, DN_K_TILE, LANES),
                 out_dtype=F32, a_split=True, b_layer=0, name=f"{tag}_dn", deps=tok)
    return _rms_bwd(h, norm, layer, dn, dh, name=f"{tag}_brms", deps=emit("flush", dn))


def _mix_block_fwd(h, n_act, layer, w_in, w_out, mp, tag, deps, before_out):
    tp, d = h.shape
    d_in = w_in.shape[2]
    z = _matmul(n_act, w_in, mode="nn", m=tp, n=d_in, kdim=d, tm=_tile(tp, 2112), tn=_tile(d_in, 512, LANES), tk=d,
                out_dtype=F32, b_layer=0, name=f"{tag}_in", deps=deps)
    m_act, hs, aux = _mix_fwd(z, mp, layer, name=f"{tag}_mix")
    h_out = _matmul(m_act, w_out, mode="nn", m=tp, n=d, kdim=d, tm=_tile(tp, 2112), tn=_tile(d, 512, LANES), tk=d,
                    out_dtype=F32, residual=h, b_layer=0, name=f"{tag}_out", deps=before_out(m_act))
    return h_out, (h, n_act, z, hs, aux, m_act)


def _mix_block_bwd(dh, dhb, saved, norm, layer, w_in, w_out, mp, tag, emit, deps):
    h, n_act, z, hs, aux, m_act = saved
    tp, d = h.shape
    d_in = w_in.shape[2]
    dm = _matmul(dhb, w_out, mode="nt", m=tp, n=d, kdim=d, tm=_tile(tp, 2112), tn=_tile(d, 512, LANES), tk=d,
                 out_dtype=F32, b_layer=0, name=f"{tag}_dm", deps=deps)
    tok = emit("w_out", _matmul(m_act, dhb, mode="tn", m=d, n=d, kdim=tp, tm=_tile(d, 512, LANES), tn=_tile(d, WGRAD_TILE, LANES), tk=tp,
                                out_dtype=BF16, name=f"{tag}_dwout", deps=[dm]))
    dz, g_mix = _mix_bwd(z, hs, aux, dm, mp, layer, name=f"{tag}_bmix")
    tok = emit("w_in", _matmul(n_act, dz, mode="tn", m=d, n=d_in, kdim=tp, tm=_tile(d, WGRAD_TILE, LANES), tn=_tile(d_in, 512, LANES),
                               tk=tp, out_dtype=BF16, name=f"{tag}_dwin", deps=tok))
    dn = _matmul(dz, w_in, mode="nt", m=tp, n=d, kdim=d_in, tm=_tile(tp, 1056), tn=_tile(d, 1024, LANES), tk=d_in,
                 out_dtype=F32, b_layer=0, name=f"{tag}_dn", deps=tok)
    dh_in, dhb_in, g_norm = _rms_bwd(h, norm, layer, dn, dh, name=f"{tag}_brms", deps=emit("flush", dn))
    return dh_in, dhb_in, g_norm, g_mix


def _mesh_pos():
    x, y, c = lax.axis_index("x"), lax.axis_index("y"), lax.axis_index("c")
    chips = [(1 - x, y), (x, 1 - y), (1 - x, 1 - y)]
    return x, y, c, chips


def _block(ref, axis, j, size):
    idx = [slice(None)] * len(ref.shape)
    idx[axis] = pl.ds(j * size, size)
    return ref.at[tuple(idx)]


_HBM = pl.BlockSpec(memory_space=pltpu.HBM)
_SEM = pl.BlockSpec(memory_space=pltpu.SEMAPHORE)
_ANY = pl.BlockSpec(memory_space=pl.ANY)
_EFFECT = pltpu.SideEffectType.DATAFLOW_SIDE_EFFECTING
_N_OUT = 4


def _in_hbm(a):
    return pltpu.with_memory_space_constraint(a, pltpu.HBM)


def _gather_start(shards, axes, deps, *, name, lands=None):
    n = len(shards)
    sizes = [s.shape[ax] for s, ax in zip(shards, axes)]
    full_shapes = [s.shape[:ax] + (N_DEV * s.shape[ax],) + s.shape[ax + 1:] for s, ax in zip(shards, axes)]

    def body(*refs):
        srcs, lands = refs[:n], refs[n:2 * n]
        send_sems, recv_sems = refs[2 * n + len(deps)], refs[2 * n + len(deps) + 1]
        token = refs[-1]
        x, y, c, chips = _mesh_pos()
        targets = [(x, y, 1 - c)] + [(*chip, c) for chip in chips]
        for a in range(n):
            dst = _block(lands[a], axes[a], 4 * x + 2 * y + c, sizes[a])
            for k, to in enumerate(targets):
                pltpu.make_async_remote_copy(src_ref=srcs[a], dst_ref=dst, send_sem=send_sems.at[_N_OUT * a + k],
                                             recv_sem=recv_sems.at[_N_OUT * a + k],
                                             device_id=to, device_id_type=MESH).start()
        token[...] = jnp.zeros(token.shape, F32)

    lands0 = [lax.empty(fs, s.dtype) for fs, s in zip(full_shapes, shards)] if lands is None else lands
    outs = pl.pallas_call(
        body, name=name,
        out_shape=(pltpu.SemaphoreType.DMA((n * _N_OUT,)), pltpu.SemaphoreType.DMA((n * _N_OUT,)),
                   *[pltpu.HBM(s.shape, s.dtype) for s in shards], *[pltpu.HBM(fs, s.dtype) for fs, s in zip(full_shapes, shards)],
                   jax.ShapeDtypeStruct((8, LANES), F32)),
        in_specs=[_HBM] * (2 * n) + [_ANY] * len(deps),
        out_specs=(_SEM, _SEM, *[_HBM] * (2 * n), pl.BlockSpec(memory_space=pltpu.VMEM)),
        input_output_aliases={a: 2 + a for a in range(2 * n)},
        compiler_params=pltpu.CompilerParams(has_side_effects=_EFFECT),
    )(*[_in_hbm(s) for s in shards], *[_in_hbm(z) for z in lands0], *deps)
    handle = dict(send_sems=outs[0], recv_sems=outs[1], srcs=list(outs[2:2 + n]), lands=list(outs[2 + n:2 + 2 * n]), axes=list(axes))
    return handle, outs[-1]


def _gather_wait(handle, after, *, name):
    srcs, lands, axes = handle["srcs"], handle["lands"], handle["axes"]
    n = len(srcs)
    sizes = [s.shape[ax] for s, ax in zip(srcs, axes)]

    def body(*refs):
        src_refs, land_refs = refs[:n], refs[n:2 * n]
        send_sems, recv_sems = refs[2 * n], refs[2 * n + 1]
        x, y, c, chips = _mesh_pos()
        peers = [(x, y, 1 - c)] + [(*chip, c) for chip in chips]
        for a in range(n):
            for k, dev in enumerate(peers):
                cp = pltpu.make_async_remote_copy(
                    src_ref=src_refs[a], dst_ref=_block(land_refs[a], axes[a], 4 * dev[0] + 2 * dev[1] + dev[2], sizes[a]),
                    send_sem=send_sems.at[_N_OUT * a + k], recv_sem=recv_sems.at[_N_OUT * a + k], device_id=dev,
                    device_id_type=MESH)
                cp.wait_send()
                cp.wait_recv()

    outs = pl.pallas_call(
        body, name=name,
        out_shape=(*[pltpu.HBM(s.shape, s.dtype) for s in srcs], *[pltpu.HBM(z.shape, z.dtype) for z in lands]),
        in_specs=[_HBM] * (2 * n) + [_SEM, _SEM] + [_ANY] * len(after),
        out_specs=tuple([_HBM] * (2 * n)),
        input_output_aliases={a: a for a in range(2 * n)},
        compiler_params=pltpu.CompilerParams(has_side_effects=_EFFECT),
    )(*srcs, *lands, handle["send_sems"], handle["recv_sems"], *after)
    return list(outs[:n]), list(outs[n:])


def _place_own(shard, full, axis, dev, *, name):
    nd = shard.ndim
    rows, cols = shard.shape[-2:]
    tr = _tile(rows, 512, _row_mult(shard.dtype)) if rows % _row_mult(shard.dtype) == 0 else rows
    nrb = rows // tr
    block = shard.shape[:-2] + (tr, cols)

    def in_map(r, dev_ref):
        return (0,) * (nd - 2) + (r, 0)

    def out_map(r, dev_ref):
        idx = [0] * nd
        idx[nd - 2] = r
        idx[axis] = dev_ref[0] * (nrb if axis == nd - 2 else 1) + idx[axis]
        return tuple(idx)

    def body(dev_ref, src_ref, full_ref, out_ref):
        out_ref[...] = src_ref[...]

    grid_spec = pltpu.PrefetchScalarGridSpec(
        num_scalar_prefetch=1, grid=(nrb,),
        in_specs=[pl.BlockSpec(block, in_map), _ANY], out_specs=pl.BlockSpec(block, out_map))
    return pl.pallas_call(body, name=name, grid_spec=grid_spec, out_shape=jax.ShapeDtypeStruct(full.shape, full.dtype),
                          input_output_aliases={2: 0}, compiler_params=_params(("arbitrary",)))(dev, shard, full)


def _gather_pass_start(fulls, axes, *, name):
    n = len(fulls)
    sizes = [z.shape[ax] // N_DEV for z, ax in zip(fulls, axes)]

    def body(*refs):
        srcs = refs[:n]
        send_sems, recv_sems = refs[n], refs[n + 1]
        token = refs[-1]
        x, y, c, chips = _mesh_pos()
        for a in range(n):
            for j, chip in enumerate(chips):
                blk = _block(srcs[a], axes[a], 4 * chip[0] + 2 * chip[1] + c, sizes[a])
                pltpu.make_async_remote_copy(src_ref=blk, dst_ref=blk, send_sem=send_sems.at[3 * a + j], recv_sem=recv_sems.at[3 * a + j],
                                             device_id=(x, y, 1 - c), device_id_type=MESH).start()
        token[...] = jnp.zeros(token.shape, F32)

    outs = pl.pallas_call(
        body, name=name,
        out_shape=(pltpu.SemaphoreType.DMA((3 * n,)), pltpu.SemaphoreType.DMA((3 * n,)), *[pltpu.HBM(z.shape, z.dtype) for z in fulls],
                   jax.ShapeDtypeStruct((8, LANES), F32)),
        in_specs=[_HBM] * n, out_specs=(_SEM, _SEM, *[_HBM] * n, pl.BlockSpec(memory_space=pltpu.VMEM)),
        input_output_aliases={a: 2 + a for a in range(n)},
        compiler_params=pltpu.CompilerParams(has_side_effects=_EFFECT),
    )(*[_in_hbm(z) for z in fulls])
    return dict(send_sems=outs[0], recv_sems=outs[1], fulls=list(outs[2:2 + n]), axes=list(axes)), outs[-1]


def _gather_pass_wait(handle, after, *, name):
    fulls, axes = handle["fulls"], handle["axes"]
    n = len(fulls)
    sizes = [z.shape[ax] // N_DEV for z, ax in zip(fulls, axes)]

    def body(*refs):
        bufs = refs[:n]
        send_sems, recv_sems = refs[n], refs[n + 1]
        x, y, c, chips = _mesh_pos()
        for a in range(n):
            for j, chip in enumerate(chips):
                cp = pltpu.make_async_remote_copy(
                    src_ref=_block(bufs[a], axes[a], 4 * chip[0] + 2 * chip[1] + c, sizes[a]),
                    dst_ref=_block(bufs[a], axes[a], 4 * chip[0] + 2 * chip[1] + (1 - c), sizes[a]),
                    send_sem=send_sems.at[3 * a + j], recv_sem=recv_sems.at[3 * a + j], device_id=(x, y, 1 - c),
                    device_id_type=MESH)
                cp.wait_send()
                cp.wait_recv()

    outs = pl.pallas_call(
        body, name=name,
        out_shape=tuple(pltpu.HBM(z.shape, z.dtype) for z in fulls),
        in_specs=[_HBM] * n + [_SEM, _SEM] + [_ANY] * len(after), out_specs=tuple([_HBM] * n),
        input_output_aliases={a: a for a in range(n)},
        compiler_params=pltpu.CompilerParams(has_side_effects=_EFFECT),
    )(*fulls, handle["send_sems"], handle["recv_sems"], *after)
    return list(outs)


def _rs_exchange_sibling(tensors, axes, *, name):
    n = len(tensors)
    n_layers = [len(t) for t in tensors]
    sizes = [t[0].shape[ax] // N_DEV for t, ax in zip(tensors, axes)]
    blk_shapes = [t[0].shape[:ax] + (sz,) + t[0].shape[ax + 1:] for t, ax, sz in zip(tensors, axes, sizes)]
    flat = [g for t in tensors for g in t]
    offs = [sum(n_layers[:a]) for a in range(n)]

    def body(*refs):
        srcs = refs[:len(flat)]
        lands = refs[len(flat):len(flat) + n]
        send_sems, recv_sems = refs[len(flat) + n:]
        x, y, c, _ = _mesh_pos()
        sib = (x, y, 1 - c)
        for a in range(n):
            for l in range(n_layers[a]):
                for i in range(N_CHIP):
                    pltpu.make_async_remote_copy(
                        src_ref=_block(srcs[offs[a] + l], axes[a], 2 * i + (1 - c), sizes[a]), dst_ref=lands[a].at[i, l],
                        send_sem=send_sems.at[a], recv_sem=recv_sems.at[a], device_id=sib, device_id_type=MESH).start()
        for a in range(n):
            pltpu.make_async_remote_copy(src_ref=lands[a], dst_ref=lands[a], send_sem=send_sems.at[a], recv_sem=recv_sems.at[a],
                                         device_id=sib, device_id_type=MESH).wait()

    any_spec = pl.BlockSpec(memory_space=pl.ANY)
    return pl.pallas_call(
        body, name=name,
        out_shape=[jax.ShapeDtypeStruct((N_CHIP, nl) + bs, t[0].dtype) for nl, bs, t in zip(n_layers, blk_shapes, tensors)],
        in_specs=[any_spec] * len(flat), out_specs=[any_spec] * n,
        scratch_shapes=[pltpu.SemaphoreType.DMA((n,)), pltpu.SemaphoreType.DMA((n,))],
    )(*flat)


def _rs_add_sibling(layers, axis, land, my_c, *, name):
    n_layers = len(layers)
    shape = layers[0].shape
    size = shape[axis] // N_DEV
    blk_shape = shape[:axis] + (size,) + shape[axis + 1:]
    nd = len(shape)
    rows = blk_shape[-2]
    tr = _tile(rows, 512, _row_mult(layers[0].dtype))
    inner = (tr, blk_shape[-1])
    lead = blk_shape[:-2]
    if lead:
        raise ValueError("blocked gradients are 2-D")
    nrb = rows // tr

    def src_map(l):
        def imap(i, r, c_ref):
            j = 2 * i + c_ref[0]
            return (j * nrb + r, 0) if axis == 0 else (r, j)
        return imap

    def body(c_ref, *refs):
        srcs = refs[:n_layers]
        land_ref = refs[n_layers]
        out_ref = refs[n_layers + 1]
        for l in range(n_layers):
            out_ref[l] = (srcs[l][...].astype(F32) + land_ref[l].astype(F32)).astype(out_ref.dtype)

    grid_spec = pltpu.PrefetchScalarGridSpec(
        num_scalar_prefetch=1, grid=(N_CHIP, nrb),
        in_specs=[pl.BlockSpec(inner, src_map(l)) for l in range(n_layers)]
        + [pl.BlockSpec((None, n_layers) + inner, lambda i, r, c_ref: (i, 0, r, 0))],
        out_specs=pl.BlockSpec((None, n_layers) + inner, lambda i, r, c_ref: (i, 0, r, 0)),
    )
    return pl.pallas_call(
        body, name=name, grid_spec=grid_spec,
        out_shape=jax.ShapeDtypeStruct((N_CHIP, n_layers) + blk_shape, layers[0].dtype),
        compiler_params=_params(("arbitrary", "arbitrary")),
    )(my_c, *layers, land)


def _rs_sibling_start(g, axis, *, name):
    size = g.shape[axis] // N_DEV
    land = lax.empty((N_CHIP, 1) + g.shape[:axis] + (size,) + g.shape[axis + 1:], g.dtype)

    def body(src, dst, send_sems, recv_sems, src_thru, dst_thru, token):
        x, y, c, _ = _mesh_pos()
        for i in range(N_CHIP):
            pltpu.make_async_remote_copy(src_ref=_block(src, axis, 2 * i + (1 - c), size), dst_ref=dst.at[i, 0], send_sem=send_sems.at[i],
                                         recv_sem=recv_sems.at[i], device_id=(x, y, 1 - c), device_id_type=MESH).start()
        token[...] = jnp.zeros(token.shape, F32)

    outs = pl.pallas_call(
        body, name=name,
        out_shape=(pltpu.SemaphoreType.DMA((N_CHIP,)), pltpu.SemaphoreType.DMA((N_CHIP,)), pltpu.HBM(g.shape, g.dtype),
                   pltpu.HBM(land.shape, land.dtype), jax.ShapeDtypeStruct((8, LANES), F32)),
        in_specs=[_HBM, _HBM], out_specs=(_SEM, _SEM, _HBM, _HBM, pl.BlockSpec(memory_space=pltpu.VMEM)),
        input_output_aliases={0: 2, 1: 3},
        compiler_params=pltpu.CompilerParams(has_side_effects=_EFFECT),
    )(_in_hbm(g), _in_hbm(land))
    return dict(send_sems=outs[0], recv_sems=outs[1], src=outs[2], land=outs[3], axis=axis), outs[4]


def _rs_sibling_wait(handle, after, *, name):
    src, land, axis = handle["src"], handle["land"], handle["axis"]
    size = src.shape[axis] // N_DEV

    def body(src_ref, dst_ref, send_sems, recv_sems, *rest):
        x, y, c, _ = _mesh_pos()
        for i in range(N_CHIP):
            cp = pltpu.make_async_remote_copy(src_ref=_block(src_ref, axis, 2 * i + (1 - c), size), dst_ref=dst_ref.at[i, 0],
                                              send_sem=send_sems.at[i], recv_sem=recv_sems.at[i], device_id=(x, y, 1 - c),
                                              device_id_type=MESH)
            cp.wait_send()
            cp.wait_recv()

    outs = pl.pallas_call(
        body, name=name,
        out_shape=(pltpu.HBM(src.shape, src.dtype), pltpu.HBM(land.shape, land.dtype)),
        in_specs=[_HBM, _HBM, _SEM, _SEM] + [_ANY] * len(after), out_specs=(_HBM, _HBM),
        input_output_aliases={0: 0, 1: 1},
        compiler_params=pltpu.CompilerParams(has_side_effects=_EFFECT),
    )(src, land, handle["send_sems"], handle["recv_sems"], *after)
    return outs[0], outs[1]


def _rs_chips_start(part, *, name):
    land = lax.empty(part.shape, part.dtype)

    def body(src, dst, send_sems, recv_sems, src_thru, dst_thru, token):
        x, y, c, chips = _mesh_pos()
        for j, chip in enumerate(chips):
            pltpu.make_async_remote_copy(src_ref=src.at[2 * chip[0] + chip[1]], dst_ref=dst.at[2 * x + y], send_sem=send_sems.at[j],
                                         recv_sem=recv_sems.at[j], device_id=(*chip, c), device_id_type=MESH).start()
        token[...] = jnp.zeros(token.shape, F32)

    outs = pl.pallas_call(
        body, name=name,
        out_shape=(pltpu.SemaphoreType.DMA((3,)), pltpu.SemaphoreType.DMA((3,)), pltpu.HBM(part.shape, part.dtype),
                   pltpu.HBM(land.shape, land.dtype), jax.ShapeDtypeStruct((8, LANES), F32)),
        in_specs=[_HBM, _HBM], out_specs=(_SEM, _SEM, _HBM, _HBM, pl.BlockSpec(memory_space=pltpu.VMEM)),
        input_output_aliases={0: 2, 1: 3},
        compiler_params=pltpu.CompilerParams(has_side_effects=_EFFECT),
    )(_in_hbm(part), _in_hbm(land))
    return dict(send_sems=outs[0], recv_sems=outs[1], src=outs[2], land=outs[3]), outs[4]


def _rs_chips_wait(handle, after, *, name):
    def body(src, dst, send_sems, recv_sems, *rest):
        x, y, c, chips = _mesh_pos()
        for j, chip in enumerate(chips):
            cp = pltpu.make_async_remote_copy(src_ref=src.at[2 * chip[0] + chip[1]], dst_ref=dst.at[2 * chip[0] + chip[1]],
                                              send_sem=send_sems.at[j], recv_sem=recv_sems.at[j], device_id=(*chip, c),
                                              device_id_type=MESH)
            cp.wait_send()
            cp.wait_recv()

    src, land = handle["src"], handle["land"]
    outs = pl.pallas_call(
        body, name=name,
        out_shape=(pltpu.HBM(src.shape, src.dtype), pltpu.HBM(land.shape, land.dtype)),
        in_specs=[_HBM, _HBM, _SEM, _SEM] + [_ANY] * len(after), out_specs=(_HBM, _HBM),
        input_output_aliases={0: 0, 1: 1},
        compiler_params=pltpu.CompilerParams(has_side_effects=_EFFECT),
    )(src, land, handle["send_sems"], handle["recv_sems"], *after)
    return outs[0], outs[1]


def _adam_math(w, g, m, v):
    m_new = ADAM_B1 * m + (1.0 - ADAM_B1) * g
    v_new = ADAM_B2 * v + (1.0 - ADAM_B2) * (g * g)
    m_hat = m_new / (1.0 - ADAM_B1 ** ADAM_STEP)
    v_hat = v_new / (1.0 - ADAM_B2 ** ADAM_STEP)
    delta = -ADAM_LR * (m_hat / (jnp.sqrt(v_hat) + ADAM_EPS) + ADAM_WD * w)
    return delta, m_new, v_new


def _sum_adam(part, land, slots, w, m, v, layer, prev, *, name, deps=()):
    n_layers, rows, cols = w.shape
    tr = _tile(rows, 256, _row_mult(land.dtype))
    n_prev = 0 if prev is None else 4

    def body(slots_ref, p0_ref, p1_ref, p2_ref, p3_ref, w_ref, m_ref, v_ref, *rest):
        g_ref, d_ref, mo_ref, vo_ref = rest[n_prev + len(deps):]
        g = p0_ref[...].astype(F32)
        for ref in (p1_ref, p2_ref, p3_ref):
            g = g + ref[...].astype(F32)
        delta, m_new, v_new = _adam_math(w_ref[...], g, m_ref[...], v_ref[...])
        g_ref[...] = g
        d_ref[...] = delta
        mo_ref[...] = m_new
        vo_ref[...] = v_new

    blk = pl.BlockSpec((None, tr, cols), lambda r, s: (layer, r, 0))
    slot = lambda j: pl.BlockSpec((None, None, tr, cols), lambda r, s: (s[j], 0, r, 0))
    shp = jax.ShapeDtypeStruct((n_layers, rows, cols), F32)
    grid_spec = pltpu.PrefetchScalarGridSpec(
        num_scalar_prefetch=1, grid=(rows // tr,),
        in_specs=[slot(0), slot(1), slot(2), slot(3), blk, blk, blk] + [_ANY] * (n_prev + len(deps)),
        out_specs=(blk, blk, blk, blk))
    return pl.pallas_call(
        body, name=name, out_shape=(shp, shp, shp, shp), grid_spec=grid_spec,
        input_output_aliases={8 + i: i for i in range(n_prev)},
        compiler_params=_params(("parallel",)),
    )(slots, part, land, land, land, w, m, v, *(prev or ()), *deps)


def _sum_slots(part, land, slots, *, name):
    _, _, rows, cols = land.shape
    tr = _tile(rows, 256, 8)

    def body(slots_ref, p0_ref, p1_ref, p2_ref, p3_ref, g_ref):
        g_ref[...] = p0_ref[...] + p1_ref[...] + p2_ref[...] + p3_ref[...]

    slot = lambda j: pl.BlockSpec((None, None, tr, cols), lambda r, s: (s[j], 0, r, 0))
    grid_spec = pltpu.PrefetchScalarGridSpec(
        num_scalar_prefetch=1, grid=(rows // tr,), in_specs=[slot(0), slot(1), slot(2), slot(3)],
        out_specs=pl.BlockSpec((tr, cols), lambda r, s: (r, 0)))
    return pl.pallas_call(body, name=name, out_shape=jax.ShapeDtypeStruct((rows, cols), F32), grid_spec=grid_spec,
                          compiler_params=_params(("parallel",)))(slots, part, land, land, land)


def _adam_flat(w, g, m, v, *, name):
    rows, cols = w.shape
    tr = _tile(rows, 256, 8)

    def body(w_ref, g_ref, m_ref, v_ref, d_ref, mo_ref, vo_ref):
        delta, m_new, v_new = _adam_math(w_ref[...], g_ref[...], m_ref[...], v_ref[...])
        d_ref[...] = delta
        mo_ref[...] = m_new
        vo_ref[...] = v_new

    blk = pl.BlockSpec((tr, cols), lambda r: (r, 0))
    shp = jax.ShapeDtypeStruct((rows, cols), F32)
    return pl.pallas_call(
        body, name=name, out_shape=(shp, shp, shp), grid=(rows // tr,), in_specs=[blk] * 4, out_specs=(blk, blk, blk),
        compiler_params=_params(("parallel",)),
    )(w, g, m, v)


_BIG = (("ffn1_w_in", 1), ("ffn1_w_out", 0), ("w_in", 1), ("w_out", 0), ("ffn2_w_in", 1), ("ffn2_w_out", 0))
_SMALL = ("meta_tokens", "ffn1_norm", "mix_norm", "conv_w", "conv_b", "lru_wa", "lru_ba", "lru_wx", "lru_bx", "lru_a_param",
          "pool_w", "pool_b", "pool_scale", "ffn2_norm", "final_norm")
_SMALL_SHARD_AXIS = {"meta_tokens": 1, "conv_w": 2, "pool_w": 2}
_PACK_COLS = 1024


def _pack(arrs):
    flat = jnp.concatenate([a.reshape(-1) for a in arrs])
    unit = N_DEV * 8 * _PACK_COLS
    total = -(-flat.shape[0] // unit) * unit
    flat = jnp.pad(flat, (0, total - flat.shape[0]))
    return flat.reshape(total // _PACK_COLS, _PACK_COLS)


def _unpack(packed, shapes):
    flat = packed.reshape(-1)
    out, off = [], 0
    for s in shapes:
        size = 1
        for v in s:
            size *= v
        out.append(flat[off:off + size].reshape(s))
        off += size
    return out


def _my_shard(full, axis, dev):
    size = full.shape[axis] // N_DEV
    return lax.dynamic_slice_in_dim(full, dev * size, size, axis)


def kernel(x, meta_tokens, ffn1_norm, ffn1_w_in, ffn1_w_out, mix_norm, w_in, conv_w, conv_b, lru_wa, lru_ba, lru_wx, lru_bx, lru_a_param, pool_w, pool_b, pool_scale, w_out, ffn2_norm, ffn2_w_in, ffn2_w_out, final_norm, loss_target, m_meta_tokens, m_ffn1_norm, m_ffn1_w_in, m_ffn1_w_out, m_mix_norm, m_w_in, m_conv_w, m_conv_b, m_lru_wa, m_lru_ba, m_lru_wx, m_lru_bx, m_lru_a_param, m_pool_w, m_pool_b, m_pool_scale, m_w_out, m_ffn2_norm, m_ffn2_w_in, m_ffn2_w_out, m_final_norm, v_meta_tokens, v_ffn1_norm, v_ffn1_w_in, v_ffn1_w_out, v_mix_norm, v_w_in, v_conv_w, v_conv_b, v_lru_wa, v_lru_ba, v_lru_wx, v_lru_bx, v_lru_a_param, v_pool_w, v_pool_b, v_pool_scale, v_w_out, v_ffn2_norm, v_ffn2_w_in, v_ffn2_w_out, v_final_norm):
    names = ("meta_tokens", "ffn1_norm", "ffn1_w_in", "ffn1_w_out", "mix_norm", "w_in", "conv_w", "conv_b", "lru_wa", "lru_ba",
             "lru_wx", "lru_bx", "lru_a_param", "pool_w", "pool_b", "pool_scale", "w_out", "ffn2_norm", "ffn2_w_in", "ffn2_w_out",
             "final_norm")
    w = dict(zip(names, (meta_tokens, ffn1_norm, ffn1_w_in, ffn1_w_out, mix_norm, w_in, conv_w, conv_b, lru_wa, lru_ba, lru_wx,
                         lru_bx, lru_a_param, pool_w, pool_b, pool_scale, w_out, ffn2_norm, ffn2_w_in, ffn2_w_out, final_norm)))
    mom = dict(zip(names, (m_meta_tokens, m_ffn1_norm, m_ffn1_w_in, m_ffn1_w_out, m_mix_norm, m_w_in, m_conv_w, m_conv_b, m_lru_wa,
                           m_lru_ba, m_lru_wx, m_lru_bx, m_lru_a_param, m_pool_w, m_pool_b, m_pool_scale, m_w_out, m_ffn2_norm,
                           m_ffn2_w_in, m_ffn2_w_out, m_final_norm)))
    vel = dict(zip(names, (v_meta_tokens, v_ffn1_norm, v_ffn1_w_in, v_ffn1_w_out, v_mix_norm, v_w_in, v_conv_w, v_conv_b, v_lru_wa,
                           v_lru_ba, v_lru_wx, v_lru_bx, v_lru_a_param, v_pool_w, v_pool_b, v_pool_scale, v_w_out, v_ffn2_norm,
                           v_ffn2_w_in, v_ffn2_w_out, v_final_norm)))
    n_layers, d = ffn1_norm.shape
    n_meta = meta_tokens.shape[0]
    seq = x.shape[1]
    t_valid = n_meta + seq
    tp = -(-t_valid // ROW_ALIGN) * ROW_ALIGN
    dev = 4 * lax.axis_index("x") + 2 * lax.axis_index("y") + lax.axis_index("c")
    my_c = lax.axis_index("c").astype(jnp.int32).reshape(1)
    dev1 = dev.astype(jnp.int32).reshape(1)
    mx, my = lax.axis_index("x"), lax.axis_index("y")
    slots = jnp.stack([2 * mx + my, 2 * (1 - mx) + my, 2 * mx + (1 - my), 2 * (1 - mx) + (1 - my)]).astype(jnp.int32)

    big_axis = dict(_BIG)
    vec = lambda a: a.reshape(a.shape[0], 1, a.shape[1])
    norms = dict(ffn1=vec(ffn1_norm), mix=vec(mix_norm), ffn2=vec(ffn2_norm))
    units = [(kind, l) for l in range(n_layers) for kind in ("ffn1", "mix", "ffn2")]

    handles = {}
    placed = {}
    state = dict(token=None, started=0, first=None)

    def behind(a):
        return a if state["first"] is None else lax.optimization_barrier((state["first"], a))[1]

    def shard(k, l):
        return behind(w[k][l:l + 1]).astype(BF16), big_axis[k] + 1

    groups = [[shard("ffn1_w_in", 0), (meta_tokens, 1)], [shard("ffn1_w_out", 0)]]

    def order():
        return [] if state["token"] is None else [state["token"]]

    def start_next():
        i = state["started"]
        if i < len(groups):
            handles[i], state["token"] = _gather_start([s for s, _ in groups[i]], [ax for _, ax in groups[i]], order(),
                                                       name=f"gather{i}_start", lands=placed.get(i))
            state["started"] = i + 1

    def finish_begin(i, after):
        handle = handles.pop(i)
        srcs, fulls = _gather_wait(handle, list(after) + order(), name=f"gather{i}_wait")
        if i not in placed:
            fulls = [_place_own(s, z, ax, dev1, name=f"gather{i}_own{a}") for a, (s, z, ax) in enumerate(zip(srcs, fulls, handle["axes"]))]
        passing, state["token"] = _gather_pass_start(fulls, handle["axes"], name=f"gather{i}_pass")
        return i, passing

    def finish_end(ctx, after):
        i, passing = ctx
        fulls = _gather_pass_wait(passing, after, name=f"gather{i}_passwait")
        start_next()
        return fulls

    def unit_group(kind, l):
        if kind == "mix":
            return [shard("w_in", l), shard("w_out", l)] + ([(behind(conv_w), 2), (behind(pool_w), 2)] if l == 0 else [])
        return [shard(f"{kind}_w_in", l), shard(f"{kind}_w_out", l)]

    groups.append(unit_group(*units[1]))
    for _ in range(PREFETCH):
        start_next()
    state["first"] = state["token"]
    groups += [unit_group(kind, l) for kind, l in units[2:]]
    for i in range(PREFETCH, len(groups)):
        placed[i] = [_place_own(s, lax.empty(s.shape[:ax] + (N_DEV * s.shape[ax],) + s.shape[ax + 1:], s.dtype), ax, dev1,
                                name=f"gather{i}_own{a}") for a, (s, ax) in enumerate(groups[i])]
    pad = jnp.zeros((tp - t_valid, d), F32)
    tgt = jnp.concatenate([jnp.zeros((n_meta, d), F32), behind(loss_target[0]), pad], axis=0)
    h0_base = jnp.concatenate([jnp.zeros((n_meta, d), F32), behind(x[0]), pad], axis=0)
    packed_state = [_pack([behind(t[k]) for k in _SMALL]) for t in (w, mom, vel)]
    ctx = finish_begin(0, [tgt, h0_base] + packed_state + [z for i in placed for z in placed[i]])
    w_in0, meta_full = finish_end(ctx, [])
    h0 = lax.dynamic_update_slice(h0_base, meta_full, (0, 0))
    n_act = _rms_fwd(h0, norms["ffn1"], 0, name="l0ffn1_rms", deps=order())
    gu, act = _ffn_in_fwd(n_act, w_in0, 0, name="l0ffn1_in")
    (w_out0,) = finish_end(finish_begin(1, [act]), [])

    def begin_next(unit_index, res):
        return finish_begin(unit_index + 2, [res]) if unit_index + 2 < len(groups) else None

    nxt = begin_next(0, act)
    h = _ffn_fwd_out(h0, act, w_out0, "l0ffn1", order())
    gathered = {units[0]: (w_in0, w_out0)}
    saved = {units[0]: (h0, n_act, gu, act)}
    mp = None
    for ui, (kind, l) in enumerate(units[1:], start=1):
        tag = f"l{l}{kind}"
        n_act = _rms_fwd(h, norms[kind], l, name=f"{tag}_rms", deps=order())
        fulls = finish_end(nxt, [n_act])
        if kind == "mix":
            if l == 0:
                mp = dict(conv_w=fulls[2], conv_b=vec(conv_b), wa=lru_wa.astype(BF16), ba=vec(lru_ba), wx=lru_wx.astype(BF16),
                          bx=vec(lru_bx), a_param=vec(lru_a_param), pool_w=fulls[3].astype(BF16), pool_b=vec(pool_b),
                          pool_scale=vec(pool_scale))
            begun = []

            def before_out(m_act, ui=ui):
                begun.append(begin_next(ui, m_act))
                return order()

            h, saved[(kind, l)] = _mix_block_fwd(h, n_act, l, fulls[0], fulls[1], mp, tag, order(), before_out)
            nxt = begun[0]
        else:
            gu, act = _ffn_in_fwd(n_act, fulls[0], 0, name=f"{tag}_in", deps=order())
            saved[(kind, l)] = (h, n_act, gu, act)
            nxt = begin_next(ui, act)
            h = _ffn_fwd_out(h, act, fulls[1], tag, order())
        gathered[(kind, l)] = (fulls[0], fulls[1])
    dh, dhb, g_final, loss_local = _final_loss(h, final_norm.reshape(1, d), tgt, n_meta, t_valid, name="final_loss")
    loss = lax.psum(loss_local[0, 0], ("x", "y", "c"))

    pending = []
    big_out = {}

    landed = []

    def drain(after):
        while pending:
            k, l, handle = pending.pop(0)
            landed.append((k, l) + _rs_chips_wait(handle, after, name=f"rs_{k}{l}_wait"))

    def adams(items, deps):
        for k, l, part, land in items:
            big_out[k] = _sum_adam(part, land, slots, w[k], mom[k], vel[k], l, big_out.get(k), name=f"adam_{k}{l}", deps=deps)
        return [big_out[items[-1][0]][0]] if items else []

    sib_pending = []

    def to_chips(k, l, g, land):
        part = _rs_add_sibling([g], big_axis[k], land, my_c, name=f"rs_{k}{l}_add")
        handle, token = _rs_chips_start(part, name=f"rs_{k}{l}_start")
        pending.append((k, l, handle))
        return token

    def finish_sibling(after):
        token = None
        while sib_pending:
            k, l, handle = sib_pending.pop(0)
            g, land = _rs_sibling_wait(handle, after, name=f"rs_{k}{l}_sibwait")
            token = to_chips(k, l, g, land)
        return token

    def emitter(kind, l):
        last_unit = (kind, l) == units[0]

        def emit(which, g):
            if which == "w_out":
                drain([g])
            token = finish_sibling([g])
            order_after = [] if token is None else [token]
            if which == "flush":
                return order_after
            k = which if kind == "mix" else f"{kind}_{which}"
            if last_unit and which == "w_in":
                (land,) = _rs_exchange_sibling([[g]], [big_axis[k]], name=f"rs_{k}{l}_sib")
                return order_after + [to_chips(k, l, g, land)]
            handle, token = _rs_sibling_start(g, big_axis[k], name=f"rs_{k}{l}_sibstart")
            sib_pending.append((k, l, handle))
            return order_after + [token]
        return emit

    g_norm = {k: [None] * n_layers for k in ("ffn1", "mix", "ffn2")}
    g_mix = [None] * n_layers
    for kind, l in reversed(units):
        w_i, w_o = gathered[(kind, l)]
        tag = f"l{l}{kind}"
        if kind == "mix":
            dh, dhb, g_norm[kind][l], g_mix[l] = _mix_block_bwd(dh, dhb, saved[(kind, l)], norms[kind], l, w_i, w_o, mp, tag,
                                                                 emitter(kind, l), [])
        else:
            dh, dhb, g_norm[kind][l] = _ffn_bwd(dh, dhb, saved[(kind, l)], norms[kind], l, w_i, w_o, tag, emitter(kind, l), [])
    drain([dh])
    dh0 = dh
    grad_x = dh0[n_meta:t_valid][None]

    mix_key = dict(conv_w="conv_w", conv_b="conv_b", lru_wa="wa", lru_ba="ba", lru_wx="wx", lru_bx="bx", lru_a_param="a_param",
                   pool_w="pool_w", pool_b="pool_b", pool_scale="pool_scale")
    small_local = {"meta_tokens": dh0[:n_meta], "final_norm": g_final.reshape(d)}
    for k in ("ffn1", "mix", "ffn2"):
        small_local[f"{k}_norm"] = jnp.stack([g.reshape(d) for g in g_norm[k]])
    for k, mk in mix_key.items():
        small_local[k] = jnp.stack([g_mix[l][mk] for l in range(n_layers)]).reshape(
            w[k].shape if k not in _SMALL_SHARD_AXIS else small_shape_full(w[k], _SMALL_SHARD_AXIS[k]))
    small_shapes = [small_local[k].shape for k in _SMALL]
    packed = _pack([small_local[k] for k in _SMALL])
    (land,) = _rs_exchange_sibling([[packed]], [0], name="rs_small_sib")
    part = _rs_add_sibling([packed], 0, land, my_c, name="rs_small_add")
    cut = max(len(landed) - 4, 0)
    handle, token = _rs_chips_start(part, name="rs_small_start")
    done = adams(landed[:cut], [token])
    part, land = _rs_chips_wait(handle, done, name="rs_small_wait")
    small_block = _sum_slots(part, land, slots, name="rs_sum_small")
    handle, token = _gather_start([small_block], [0], [], name="gather_small_start")
    done = adams(landed[cut:cut + 2], [token])
    (src,), (small_full,) = _gather_wait(handle, done, name="gather_small_wait")
    small_full = _place_own(src, small_full, 0, dev1, name="gather_small_own")
    handle, token = _gather_pass_start([small_full], [0], name="gather_small_pass")
    done = adams(landed[cut + 2:], [token])
    (small_full,) = _gather_pass_wait(handle, done, name="gather_small_passwait")

    out_g, out_d, out_m, out_v = {}, {}, {}, {}
    for k, _ in _BIG:
        out_g[k], out_d[k], out_m[k], out_v[k] = big_out[k]
    small_g = dict(zip(_SMALL, _unpack(small_full, small_shapes)))
    for k, ax in _SMALL_SHARD_AXIS.items():
        small_g[k] = _my_shard(small_g[k], ax, dev)
    shapes_local = [w[k].shape for k in _SMALL]
    d_p, m_p, v_p = _adam_flat(packed_state[0], _pack([small_g[k] for k in _SMALL]), packed_state[1], packed_state[2],
                               name="adam_small")
    for k, dd, mm, vv in zip(_SMALL, _unpack(d_p, shapes_local), _unpack(m_p, shapes_local), _unpack(v_p, shapes_local)):
        out_g[k], out_d[k], out_m[k], out_v[k] = small_g[k], dd, mm, vv

    return (loss, grad_x, *[out_g[k] for k in names], *[out_d[k] for k in names], *[out_m[k] for k in names],
            *[out_v[k] for k in names])


def small_shape_full(w_shard, axis):
    return w_shard.shape[:axis] + (N_DEV * w_shard.shape[axis],) + w_shard.shape[axis + 1:]
```

```python
import jax
import jax.numpy as jnp
from jax import lax
from jax.experimental import pallas as pl
from jax.experimental.pallas import tpu as pltpu

F32 = jnp.float32
BF16 = jnp.bfloat16
MESH = pl.DeviceIdType.MESH

N_DEV = 8
N_CHIP = 4
RMS_EPS = 1e-6
LRU_C = 8.0
CONV_WIDTH = 4
POOL_WINDOWS = (2, 4, 8, 16)
HIST = 16
ADAM_LR = 0.001
ADAM_B1 = 0.9
ADAM_B2 = 0.999
ADAM_EPS = 1e-08
ADAM_WD = 0.01
ADAM_STEP = 10
ROW_ALIGN = 128
V7X_VMEM_LIMIT = 56 * 1024 * 1024
BF16_ROWS = 16
K_TILE = 5632
DN_K_TILE = 2816
WGRAD_TILE = 2048
PREFETCH = 3
SCAN_UNROLL = 8
EPILOGUE_ROWS = 528
LANES = 128


def _tile(n, cap, mult=BF16_ROWS):
    best = None
    d = mult
    while d <= min(n, cap):
        if n % d == 0:
            best = d
        d += mult
    if best is None:
        raise ValueError(f"no tile for {n} (cap {cap}, multiple of {mult})")
    return best


def _row_mult(dtype):
    return 8 * 4 // jnp.dtype(dtype).itemsize


def _params(sem=None):
    return pltpu.CompilerParams(dimension_semantics=sem, vmem_limit_bytes=V7X_VMEM_LIMIT)


def _dot(a, b, mode):
    dims = {"nn": ((1,), (0,)), "nt": ((1,), (1,)), "tn": ((0,), (0,))}[mode]
    return lax.dot_general(a, b, (dims, ((), ())), preferred_element_type=F32)


def _sigmoid(x):
    return 1.0 / (1.0 + jnp.exp(-x))


_GELU_C = 0.7978845608028654
_GELU_K = 0.044715


def _gelu(x):
    return 0.5 * x * (1.0 + jnp.tanh(_GELU_C * (x + _GELU_K * x * x * x)))


def _gelu_and_grad(x):
    x2 = x * x
    th = jnp.tanh(_GELU_C * x * (1.0 + _GELU_K * x2))
    half = 0.5 * (1.0 + th)
    return x * half, half + 0.5 * x * (1.0 - th * th) * _GELU_C * (1.0 + 3.0 * _GELU_K * x2)


def _neg_expm1(x):
    p = 1.0 / 5040.0
    for c in (1.0 / 720.0, 1.0 / 120.0, 1.0 / 24.0, 1.0 / 6.0, 0.5, 1.0):
        p = p * x + c
    return jnp.where(x > -0.25, -(x * p), 1.0 - jnp.exp(x))


def _softplus_neg(p):
    e = jnp.exp(-jnp.abs(p))
    u = 1.0 + e
    l1p = jnp.where(u == 1.0, e, jnp.log(u) * e / (u - 1.0 + (u == 1.0).astype(F32)))
    return jnp.maximum(-p, 0.0) + l1p


def _operand_spec(arr, br, bc, ridx, cidx, layer, split):
    if split:
        ncb = arr.shape[2] // bc
        return pl.BlockSpec((None, br, bc), lambda i, j, k: (cidx(i, j, k) // ncb, ridx(i, j, k), cidx(i, j, k) % ncb))
    if layer is not None:
        return pl.BlockSpec((None, br, bc), lambda i, j, k: (layer, ridx(i, j, k), cidx(i, j, k)))
    return pl.BlockSpec((br, bc), lambda i, j, k: (ridx(i, j, k), cidx(i, j, k)))


def _matmul(a, b, *, mode, m, n, kdim, tm, tn, tk, out_dtype, name, scale=None, residual=None,
            b_layer=None, a_split=False, b_split=False, deps=()):
    nk = kdim // tk
    I = lambda i, j, k: i
    J = lambda i, j, k: j
    K = lambda i, j, k: k
    if mode == "nn":
        a_spec = _operand_spec(a, tm, tk, I, K, None, a_split)
        b_spec = _operand_spec(b, tk, tn, K, J, b_layer, b_split)
    elif mode == "nt":
        a_spec = _operand_spec(a, tm, tk, I, K, None, a_split)
        b_spec = _operand_spec(b, tn, tk, J, K, b_layer, b_split)
    else:
        a_spec = _operand_spec(a, tk, tm, K, I, None, a_split)
        b_spec = _operand_spec(b, tk, tn, K, J, b_layer, b_split)
    in_specs = [a_spec, b_spec]
    operands = [a, b]
    if residual is not None:
        in_specs.append(pl.BlockSpec((tm, tn), lambda i, j, k: (i, j)))
        operands.append(residual)
    in_specs += [pl.BlockSpec(memory_space=pl.ANY)] * len(deps)
    operands += list(deps)
    n_in = len(operands)

    def finish(acc, res_ref, o_ref):
        if scale is not None:
            acc = acc * scale
        if res_ref is not None:
            acc = acc + res_ref[...]
        o_ref[...] = acc.astype(o_ref.dtype)

    def body(*refs):
        a_ref, b_ref = refs[0], refs[1]
        res_ref = refs[2] if residual is not None else None
        o_ref = refs[n_in]
        part = _dot(a_ref[...], b_ref[...], mode)
        if nk == 1:
            finish(part, res_ref, o_ref)
        else:
            acc_ref = refs[-1]
            k = pl.program_id(2)

            @pl.when(k == 0)
            def _():
                acc_ref[...] = part

            @pl.when(jnp.logical_and(k > 0, k < nk - 1))
            def _():
                acc_ref[...] += part

            @pl.when(k == nk - 1)
            def _():
                finish(acc_ref[...] + part, res_ref, o_ref)

    return pl.pallas_call(
        body, name=name,
        out_shape=jax.ShapeDtypeStruct((m, n), out_dtype),
        grid=(m // tm, n // tn, nk),
        in_specs=in_specs,
        out_specs=pl.BlockSpec((tm, tn), lambda i, j, k: (i, j)),
        scratch_shapes=[pltpu.VMEM((tm, tn), F32)] if nk > 1 else [],
        compiler_params=_params(("parallel", "parallel", "arbitrary")),
    )(*operands)


def _ffn_in_fwd(n_act, w_in, layer, *, name, deps=()):
    tp, d = n_act.shape
    f = w_in.shape[2] // 2
    tm = _tile(tp, 2112)
    tn = _tile(f, 512, LANES)
    nj = f // tn

    ch = _tile(tm, EPILOGUE_ROWS)

    def body(n_ref, wg_ref, wu_ref, *rest):
        da_ref, a_ref = rest[len(deps):]
        for r in range(tm // ch):
            rows = pl.ds(r * ch, ch)
            x = n_ref[rows, :]
            g = _dot(x, wg_ref[...], "nn")
            u = _dot(x, wu_ref[...], "nn")
            s = 0.5 * jnp.tanh(0.5 * g) + 0.5
            sg = g * s
            da_ref[0, rows, :] = (u * s * (1.0 + g * (1.0 - s))).astype(BF16)
            da_ref[1, rows, :] = sg.astype(BF16)
            a_ref[rows, :] = (sg * u).astype(BF16)

    return pl.pallas_call(
        body, name=name,
        out_shape=(jax.ShapeDtypeStruct((2, tp, f), BF16), jax.ShapeDtypeStruct((tp, f), BF16)),
        grid=(tp // tm, nj),
        in_specs=[pl.BlockSpec((tm, d), lambda i, j: (i, 0)),
                  pl.BlockSpec((None, d, tn), lambda i, j: (layer, 0, j)),
                  pl.BlockSpec((None, d, tn), lambda i, j: (layer, 0, j + nj))] + [pl.BlockSpec(memory_space=pl.ANY)] * len(deps),
        out_specs=(pl.BlockSpec((2, tm, tn), lambda i, j: (0, i, j)), pl.BlockSpec((tm, tn), lambda i, j: (i, j))),
        compiler_params=_params(("parallel", "arbitrary")),
    )(n_act, w_in, w_in, *deps)


def _ffn_bwd_act(dhb, w_out, layer, gu, *, name, deps=()):
    tp, d = dhb.shape
    f = w_out.shape[1]
    tm = _tile(tp, 2112)
    tn = _tile(f, 512, LANES)

    ch = _tile(tm, EPILOGUE_ROWS)

    def body(dh_ref, w_ref, gu_ref, *rest):
        dz_ref = rest[len(deps)]
        for r in range(tm // ch):
            rows = pl.ds(r * ch, ch)
            da = 0.5 * _dot(dh_ref[rows, :], w_ref[...], "nt")
            dz_ref[0, rows, :] = (da * gu_ref[0, rows, :].astype(F32)).astype(BF16)
            dz_ref[1, rows, :] = (da * gu_ref[1, rows, :].astype(F32)).astype(BF16)

    return pl.pallas_call(
        body, name=name,
        out_shape=jax.ShapeDtypeStruct((2, tp, f), BF16),
        grid=(tp // tm, f // tn),
        in_specs=[pl.BlockSpec((tm, d), lambda i, j: (i, 0)),
                  pl.BlockSpec((None, tn, d), lambda i, j: (layer, j, 0)),
                  pl.BlockSpec((2, tm, tn), lambda i, j: (0, i, j))] + [pl.BlockSpec(memory_space=pl.ANY)] * len(deps),
        out_specs=pl.BlockSpec((2, tm, tn), lambda i, j: (0, i, j)),
        compiler_params=_params(("parallel", "arbitrary")),
    )(dhb, w_out, gu, *deps)


def _rms_fwd(h, g, layer, *, name, deps=()):
    tp, d = h.shape
    tr = _tile(tp, 528)

    def body(h_ref, g_ref, *rest):
        n_ref = rest[len(deps)]
        x = h_ref[...]
        r = lax.rsqrt(jnp.mean(x * x, axis=-1, keepdims=True) + RMS_EPS)
        n_ref[...] = (x * r * g_ref[...]).astype(BF16)

    return pl.pallas_call(
        body, name=name, out_shape=jax.ShapeDtypeStruct((tp, d), BF16), grid=(tp // tr,),
        in_specs=[pl.BlockSpec((tr, d), lambda i: (i, 0)), pl.BlockSpec((None, 1, d), lambda i: (layer, 0, 0))]
        + [pl.BlockSpec(memory_space=pl.ANY)] * len(deps),
        out_specs=pl.BlockSpec((tr, d), lambda i: (i, 0)),
        compiler_params=_params(("parallel",)),
    )(h, g, *deps)


def _rms_bwd(h, g, layer, dn, dres, *, name, deps=()):
    tp, d = h.shape
    tr = _tile(tp, 528)

    def body(h_ref, g_ref, dn_ref, dres_ref, *rest):
        dh_ref, dhb_ref, dg_ref = rest[len(deps):]
        x = h_ref[...]
        r = lax.rsqrt(jnp.mean(x * x, axis=-1, keepdims=True) + RMS_EPS)
        xhat = x * r
        dn_v = dn_ref[...]
        dxhat = dn_v * g_ref[...]
        dh = dres_ref[...] + r * (dxhat - xhat * jnp.mean(dxhat * xhat, axis=-1, keepdims=True))
        dh_ref[...] = dh
        dhb_ref[...] = dh.astype(BF16)
        part = jnp.sum(dn_v * xhat, axis=0, keepdims=True)

        @pl.when(pl.program_id(0) == 0)
        def _():
            dg_ref[...] = part

        @pl.when(pl.program_id(0) > 0)
        def _():
            dg_ref[...] += part

    return pl.pallas_call(
        body, name=name,
        out_shape=(jax.ShapeDtypeStruct((tp, d), F32), jax.ShapeDtypeStruct((tp, d), BF16), jax.ShapeDtypeStruct((1, d), F32)),
        grid=(tp // tr,),
        in_specs=[pl.BlockSpec((tr, d), lambda i: (i, 0)), pl.BlockSpec((None, 1, d), lambda i: (layer, 0, 0)),
                  pl.BlockSpec((tr, d), lambda i: (i, 0)), pl.BlockSpec((tr, d), lambda i: (i, 0))]
        + [pl.BlockSpec(memory_space=pl.ANY)] * len(deps),
        out_specs=(pl.BlockSpec((tr, d), lambda i: (i, 0)), pl.BlockSpec((tr, d), lambda i: (i, 0)),
                   pl.BlockSpec((1, d), lambda i: (0, 0))),
        compiler_params=_params(("arbitrary",)),
    )(h, g, dn, dres, *deps)


def _final_loss(h, g, tgt, n_meta, t_valid, *, name):
    tp, d = h.shape
    tr = _tile(tp, 528)

    def body(h_ref, g_ref, t_ref, dh_ref, dhb_ref, dg_ref, loss_ref):
        i = pl.program_id(0)
        x = h_ref[...]
        r = lax.rsqrt(jnp.mean(x * x, axis=-1, keepdims=True) + RMS_EPS)
        xhat = x * r
        gv = g_ref[...]
        row = i * tr + lax.broadcasted_iota(jnp.int32, (tr, 1), 0)
        valid = jnp.logical_and(row >= n_meta, row < t_valid)
        err = jnp.where(valid, xhat * gv - t_ref[...], 0.0)
        dy = err * (1.0 / d)
        dxhat = dy * gv
        dh = r * (dxhat - xhat * jnp.mean(dxhat * xhat, axis=-1, keepdims=True))
        dh_ref[...] = dh
        dhb_ref[...] = dh.astype(BF16)
        dg_part = jnp.sum(dy * xhat, axis=0, keepdims=True)
        loss_part = jnp.sum(jnp.sum(err * err, axis=1, keepdims=True), axis=0, keepdims=True) * (0.5 / d)

        @pl.when(i == 0)
        def _():
            dg_ref[...] = dg_part
            loss_ref[...] = loss_part

        @pl.when(i > 0)
        def _():
            dg_ref[...] += dg_part
            loss_ref[...] += loss_part

    return pl.pallas_call(
        body, name=name,
        out_shape=(jax.ShapeDtypeStruct((tp, d), F32), jax.ShapeDtypeStruct((tp, d), BF16),
                   jax.ShapeDtypeStruct((1, d), F32), jax.ShapeDtypeStruct((1, 1), F32)),
        grid=(tp // tr,),
        in_specs=[pl.BlockSpec((tr, d), lambda i: (i, 0)), pl.BlockSpec((1, d), lambda i: (0, 0)),
                  pl.BlockSpec((tr, d), lambda i: (i, 0))],
        out_specs=(pl.BlockSpec((tr, d), lambda i: (i, 0)), pl.BlockSpec((tr, d), lambda i: (i, 0)),
                   pl.BlockSpec((1, d), lambda i: (0, 0)), pl.BlockSpec((1, 1), lambda i: (0, 0))),
        compiler_params=_params(("arbitrary",)),
    )(h, g, tgt)


_MIX_PARAMS = ("conv_w", "conv_b", "wa", "ba", "wx", "bx", "a_param", "pool_w", "pool_b", "pool_scale")


def _mix_param_specs(mp, layer):
    def spec(a):
        return pl.BlockSpec((None,) + a.shape[1:], lambda *g: (layer,) + (0,) * (a.ndim - 1))
    return [spec(mp[k]) for k in _MIX_PARAMS]


def _mix_param_list(mp):
    return [mp[k] for k in _MIX_PARAMS]


def _lru_gates(xc, wa_h, ba_h, wx_h, bx_h, sp_h):
    xb = xc.astype(BF16)
    ra = _sigmoid(_dot(xb, wa_h, "nn") + ba_h)
    ii = _sigmoid(_dot(xb, wx_h, "nn") + bx_h)
    la = -LRU_C * ra * sp_h
    return ra, ii, la, jnp.exp(la)


def _shifted(x, k):
    return x if k == 0 else pltpu.roll(x, k % x.shape[0], axis=0)


def _conv_taps(xbuf, sl, tc):
    x = xbuf[:, sl]
    return [_shifted(x, CONV_WIDTH - 1 - k)[HIST:HIST + tc] for k in range(CONV_WIDTH)]


def _conv_fwd(taps, cw_ref, cb_ref, sl):
    xc = cb_ref[:, sl]
    for k in range(CONV_WIDTH):
        xc = xc + cw_ref[k:k + 1, sl] * taps[k]
    return xc


def _window_sum(x, win, direction):
    s = x
    step = 1
    while step < win:
        s = s + _shifted(s, direction * step)
        step *= 2
    return s


def _pool_delta(pbuf, cols, win, t0, tc):
    x = pbuf[:, cols]
    u = x[HIST:HIST + tc]
    s = _window_sum(x, win, 1)[HIST:HIST + tc]
    t = t0 + lax.broadcasted_iota(jnp.int32, (tc, 1), 0)
    inv_cnt = 1.0 / jnp.minimum(t + 1, win).astype(F32)
    return s * inv_cnt - u, inv_cnt


def _mix_fwd(z, mp, layer, *, name):
    tp = z.shape[0]
    dl = z.shape[1] // 3
    n_heads, hd = mp["wa"].shape[1], mp["wa"].shape[2]
    n_groups, gd = mp["pool_w"].shape[1], mp["pool_w"].shape[2]
    tc = _tile(tp, 384)

    def body(z_ref, cw_ref, cb_ref, wa_ref, ba_ref, wx_ref, bx_ref, ap_ref, pw_ref, pb_ref, ps_ref,
             m_ref, hs_ref, aux_ref, xbuf, pbuf, a_s, b_s, hcar):
        i = pl.program_id(0)

        @pl.when(i == 0)
        def _():
            xbuf[pl.ds(0, HIST), :] = jnp.zeros((HIST, dl), F32)
            pbuf[pl.ds(0, HIST), :] = jnp.zeros((HIST, dl), F32)
            hcar[...] = jnp.zeros((1, dl), F32)

        @pl.when(i > 0)
        def _():
            xbuf[pl.ds(0, HIST), :] = xbuf[pl.ds(tc, HIST), :]
            pbuf[pl.ds(0, HIST), :] = pbuf[pl.ds(tc, HIST), :]

        xbuf[pl.ds(HIST, tc), :] = z_ref[:, pl.ds(0, dl)]
        pbuf[pl.ds(HIST, tc), :] = z_ref[:, pl.ds(2 * dl, dl)]
        sp = _softplus_neg(ap_ref[...])
        for h in range(n_heads):
            sl = pl.ds(h * hd, hd)
            xc = _conv_fwd(_conv_taps(xbuf, sl, tc), cw_ref, cb_ref, sl)
            ra, ii, la, a = _lru_gates(xc, wa_ref[h], ba_ref[:, sl], wx_ref[h], bx_ref[:, sl], sp[:, h * hd:(h + 1) * hd])
            e = _neg_expm1(2.0 * la)
            a_s[:, sl] = a
            b_s[:, sl] = jnp.sqrt(e) * ii * xc
            for slot, val in ((_AUX_R, ra), (_AUX_I, ii), (_AUX_X, xc), (_AUX_E, e)):
                aux_ref[slot, :, sl] = val

        def step(t, hprev):
            hnew = a_s[pl.ds(t, 1), :] * hprev + b_s[pl.ds(t, 1), :]
            hs_ref[pl.ds(t, 1), :] = hnew
            return hnew

        hcar[...] = lax.fori_loop(0, tc, step, hcar[...], unroll=SCAN_UNROLL)
        for h in range(n_heads):
            sl = pl.ds(h * hd, hd)
            m_ref[:, sl] = (hs_ref[:, sl] * _gelu(z_ref[:, pl.ds(dl + h * hd, hd)])).astype(BF16)
        for g in range(n_groups):
            cols = pl.ds(g * gd, gd)
            dlt, _ = _pool_delta(pbuf, cols, POOL_WINDOWS[g], i * tc, tc)
            q = _dot(dlt.astype(BF16), pw_ref[g], "nn") + pb_ref[:, cols]
            m_ref[:, pl.ds(dl + g * gd, gd)] = (q * ps_ref[:, cols]).astype(BF16)

    return pl.pallas_call(
        body, name=name,
        out_shape=(jax.ShapeDtypeStruct((tp, 2 * dl), BF16), jax.ShapeDtypeStruct((tp, dl), F32),
                   jax.ShapeDtypeStruct((_AUX_N, tp, dl), F32)),
        grid=(tp // tc,),
        in_specs=[pl.BlockSpec((tc, 3 * dl), lambda i: (i, 0))] + _mix_param_specs(mp, layer),
        out_specs=(pl.BlockSpec((tc, 2 * dl), lambda i: (i, 0)), pl.BlockSpec((tc, dl), lambda i: (i, 0)),
                   pl.BlockSpec((_AUX_N, tc, dl), lambda i: (0, i, 0))),
        scratch_shapes=[pltpu.VMEM((HIST + tc, dl), F32), pltpu.VMEM((HIST + tc, dl), F32),
                        pltpu.VMEM((tc, dl), F32), pltpu.VMEM((tc, dl), F32), pltpu.VMEM((1, dl), F32)],
        compiler_params=_params(("arbitrary",)),
    )(z, *_mix_param_list(mp))


_AUX_R, _AUX_I, _AUX_X, _AUX_E, _AUX_N = 0, 1, 2, 3, 4


def _mix_bwd(z, hs, aux, dm, mp, layer, *, name):
    tp = z.shape[0]
    dl = z.shape[1] // 3
    n_heads, hd = mp["wa"].shape[1], mp["wa"].shape[2]
    n_groups, gd = mp["pool_w"].shape[1], mp["pool_w"].shape[2]
    tc = _tile(tp, 352)
    nc = tp // tc
    per = tc // HIST

    def body(z_ref, zp_ref, hs_ref, hsp_ref, aux_ref, dm_ref, cw_ref, cb_ref, wa_ref, ba_ref, wx_ref, bx_ref, ap_ref, pw_ref, pb_ref,
             ps_ref, dz_ref, dcw_ref, dcb_ref, dwa_ref, dba_ref, dwx_ref, dbx_ref, dap_ref, dpw_ref, dpb_ref, dps_ref,
             xbuf, pbuf, hbuf, dxbuf, ddbuf, a_s, lam_s, ccar):
        i = pl.program_id(0)
        ci = nc - 1 - i

        @pl.when(i == 0)
        def _():
            dxbuf[pl.ds(tc, HIST), :] = jnp.zeros((HIST, dl), F32)
            ddbuf[pl.ds(tc, HIST), :] = jnp.zeros((HIST, dl), F32)
            ccar[...] = jnp.zeros((1, dl), F32)
            for ref in (dcw_ref, dcb_ref, dwa_ref, dba_ref, dwx_ref, dbx_ref, dap_ref, dpw_ref, dpb_ref, dps_ref):
                ref[...] = jnp.zeros(ref.shape, F32)

        @pl.when(ci == 0)
        def _():
            xbuf[pl.ds(0, HIST), :] = jnp.zeros((HIST, dl), F32)
            pbuf[pl.ds(0, HIST), :] = jnp.zeros((HIST, dl), F32)
            hbuf[pl.ds(0, HIST), :] = jnp.zeros((HIST, dl), F32)

        @pl.when(ci > 0)
        def _():
            xbuf[pl.ds(0, HIST), :] = zp_ref[:, pl.ds(0, dl)]
            pbuf[pl.ds(0, HIST), :] = zp_ref[:, pl.ds(2 * dl, dl)]
            hbuf[pl.ds(0, HIST), :] = hsp_ref[...]

        xbuf[pl.ds(HIST, tc), :] = z_ref[:, pl.ds(0, dl)]
        pbuf[pl.ds(HIST, tc), :] = z_ref[:, pl.ds(2 * dl, dl)]
        hbuf[pl.ds(HIST, tc), :] = hs_ref[...]
        sp = _softplus_neg(ap_ref[...])

        for h in range(n_heads):
            sl = pl.ds(h * hd, hd)
            a_s[:, sl] = jnp.exp(-LRU_C * aux_ref[_AUX_R, :, sl] * sp[:, h * hd:(h + 1) * hd])
            gel, gel_grad = _gelu_and_grad(z_ref[:, pl.ds(dl + h * hd, hd)])
            dya = dm_ref[:, sl]
            lam_s[:, sl] = dya * gel
            dz_ref[:, pl.ds(dl + h * hd, hd)] = (dya * hs_ref[:, sl] * gel_grad).astype(BF16)

        def step(r, carry):
            t = tc - 1 - r
            lam = lam_s[pl.ds(t, 1), :] + carry
            lam_s[pl.ds(t, 1), :] = lam
            return a_s[pl.ds(t, 1), :] * lam

        ccar[...] = lax.fori_loop(0, tc, step, ccar[...], unroll=SCAN_UNROLL)

        for h in range(n_heads):
            sl = pl.ds(h * hd, hd)
            sp_h = sp[:, h * hd:(h + 1) * hd]
            lam = lam_s[:, sl]
            a = a_s[:, sl]
            ra = aux_ref[_AUX_R, :, sl]
            ii = aux_ref[_AUX_I, :, sl]
            xc = aux_ref[_AUX_X, :, sl]
            e = aux_ref[_AUX_E, :, sl]
            inv_mult = lax.rsqrt(e)
            mult = e * inv_mult
            hprev = _shifted(hbuf[:, sl], 1)[HIST:HIST + tc]
            lam_i = lam * ii
            lam_m = lam * mult
            dla = lam * hprev * a - lam_i * xc * (a * a) * inv_mult
            dla_r = dla * (-LRU_C) * ra
            dap_ref[:, sl] += jnp.sum(dla_r, axis=0, keepdims=True)
            dpa = dla_r * sp_h * (1.0 - ra)
            dpx = lam_m * xc * ii * (1.0 - ii)
            dba_ref[:, sl] += jnp.sum(dpa, axis=0, keepdims=True)
            dbx_ref[:, sl] += jnp.sum(dpx, axis=0, keepdims=True)
            xb = xc.astype(BF16)
            dpa_b = dpa.astype(BF16)
            dpx_b = dpx.astype(BF16)
            dwa_ref[h] += _dot(xb, dpa_b, "tn")
            dwx_ref[h] += _dot(xb, dpx_b, "tn")
            dxc = lam_m * ii + _dot(dpa_b, wa_ref[h], "nt") + _dot(dpx_b, wx_ref[h], "nt")
            dxbuf[pl.ds(0, tc), sl] = dxc
            dcb_ref[:, sl] += jnp.sum(dxc, axis=0, keepdims=True)
            taps = _conv_taps(xbuf, sl, tc)
            dx_all = dxbuf[:, sl]
            dzx = jnp.zeros((tc, hd), F32)
            for k in range(CONV_WIDTH):
                dcw_ref[k:k + 1, sl] += jnp.sum(dxc * taps[k], axis=0, keepdims=True)
                dzx = dzx + cw_ref[k:k + 1, sl] * _shifted(dx_all, k - (CONV_WIDTH - 1))[0:tc]
            dz_ref[:, sl] = dzx.astype(BF16)
            dxbuf[pl.ds(tc, HIST), sl] = dxbuf[pl.ds(0, HIST), sl]

        for g in range(n_groups):
            cols = pl.ds(g * gd, gd)
            win = POOL_WINDOWS[g]
            dlt, inv_cnt = _pool_delta(pbuf, cols, win, ci * tc, tc)
            db = dlt.astype(BF16)
            q = _dot(db, pw_ref[g], "nn") + pb_ref[:, cols]
            dyb = dm_ref[:, pl.ds(dl + g * gd, gd)]
            dps_ref[:, cols] += jnp.sum(dyb * q, axis=0, keepdims=True)
            dq = dyb * ps_ref[:, cols]
            dpb_ref[:, cols] += jnp.sum(dq, axis=0, keepdims=True)
            dqb = dq.astype(BF16)
            dpw_ref[g] += _dot(db, dqb, "tn")
            dd = _dot(dqb, pw_ref[g], "nt")
            ddbuf[pl.ds(0, tc), cols] = dd * inv_cnt
            dzp = _window_sum(ddbuf[:, cols], win, -1)[0:tc] - dd
            dz_ref[:, pl.ds(2 * dl + g * gd, gd)] = dzp.astype(BF16)
            ddbuf[pl.ds(tc, HIST), cols] = ddbuf[pl.ds(0, HIST), cols]

        @pl.when(i == nc - 1)
        def _():
            dap_ref[...] = dap_ref[...] * (-_sigmoid(-ap_ref[...]))

    rev = lambda i: (nc - 1 - i, 0)
    prev = lambda i: (jnp.maximum((nc - 1 - i) * per - 1, 0), 0)
    const = lambda a: pl.BlockSpec(a.shape[1:], lambda i: (0,) * (a.ndim - 1))
    plist = _mix_param_list(mp)
    grad_shapes = [jax.ShapeDtypeStruct(a.shape[1:], F32) for a in plist]
    buf = lambda rows: pltpu.VMEM((rows, dl), F32)
    outs = pl.pallas_call(
        body, name=name,
        out_shape=[jax.ShapeDtypeStruct((tp, 3 * dl), BF16)] + grad_shapes,
        grid=(nc,),
        in_specs=[pl.BlockSpec((tc, 3 * dl), rev), pl.BlockSpec((HIST, 3 * dl), prev),
                  pl.BlockSpec((tc, dl), rev), pl.BlockSpec((HIST, dl), prev),
                  pl.BlockSpec((_AUX_N, tc, dl), lambda i: (0, nc - 1 - i, 0)),
                  pl.BlockSpec((tc, 2 * dl), rev)] + _mix_param_specs(mp, layer),
        out_specs=[pl.BlockSpec((tc, 3 * dl), rev)] + [const(a) for a in plist],
        scratch_shapes=[buf(HIST + tc), buf(HIST + tc), buf(HIST + tc), buf(tc + HIST), buf(tc + HIST),
                        buf(tc), buf(tc), buf(1)],
        compiler_params=_params(("arbitrary",)),
    )(z, z, hs, hs, aux, dm, *plist)
    return outs[0], dict(zip(_MIX_PARAMS, outs[1:]))


def _ffn_fwd_out(h, act, w_out, tag, deps):
    tp, d = h.shape
    f = w_out.shape[1]
    return _matmul(act, w_out, mode="nn", m=tp, n=d, kdim=f, tm=_tile(tp, 1056), tn=_tile(d, 512, LANES),
                   tk=_tile(f, K_TILE, LANES), out_dtype=F32, scale=0.5, residual=h, b_layer=0, name=f"{tag}_out", deps=deps)


def _ffn_bwd(dh, dhb, saved, norm, layer, w_in, w_out, tag, emit, deps):
    h, n_act, gu, act = saved
    tp, d = h.shape
    f = w_out.shape[1]
    dz = _ffn_bwd_act(dhb, w_out, 0, gu, name=f"{tag}_bact", deps=deps)
    tok = emit("w_out", _matmul(act, dhb, mode="tn", m=f, n=d, kdim=tp, tm=_tile(f, 512, LANES), tn=_tile(d, WGRAD_TILE, LANES), tk=tp,
                                out_dtype=BF16, scale=0.5, name=f"{tag}_dwout", deps=[dz]))
    tok = emit("w_in", _matmul(n_act, dz, mode="tn", m=d, n=2 * f, kdim=tp, tm=_tile(d, WGRAD_TILE, LANES), tn=_tile(f, 512, LANES),
                               tk=tp, out_dtype=BF16, b_split=True, name=f"{tag}_dwin", deps=tok))
    dn = _matmul(dz, w_in, mode="nt", m=tp, n=d, kdim=2 * f, tm=_tile(tp, 1056), tn=_tile(d, 1024, LANES), tk=_tile(f, DN_K_TILE, LANES),
                 out_dtype=F32, a_split=True, b_layer=0, name=f"{tag}_dn", deps=tok)
    return _rms_bwd(h, norm, layer, dn, dh, name=f"{tag}_brms", deps=emit("flush", dn))


def _mix_block_fwd(h, n_act, layer, w_in, w_out, mp, tag, deps, before_out):
    tp, d = h.shape
    d_in = w_in.shape[2]
    z = _matmul(n_act, w_in, mode="nn", m=tp, n=d_in, kdim=d, tm=_tile(tp, 2112), tn=_tile(d_in, 512, LANES), tk=d,
                out_dtype=F32, b_layer=0, name=f"{tag}_in", deps=deps)
    m_act, hs, aux = _mix_fwd(z, mp, layer, name=f"{tag}_mix")
    h_out = _matmul(m_act, w_out, mode="nn", m=tp, n=d, kdim=d, tm=_tile(tp, 2112), tn=_tile(d, 512, LANES), tk=d,
                    out_dtype=F32, residual=h, b_layer=0, name=f"{tag}_out", deps=before_out(m_act))
    return h_out, (h, n_act, z, hs, aux, m_act)


def _mix_block_bwd(dh, dhb, saved, norm, layer, w_in, w_out, mp, tag, emit, deps):
    h, n_act, z, hs, aux, m_act = saved
    tp, d = h.shape
    d_in = w_in.shape[2]
    dm = _matmul(dhb, w_out, mode="nt", m=tp, n=d, kdim=d, tm=_tile(tp, 2112), tn=_tile(d, 512, LANES), tk=d,
                 out_dtype=F32, b_layer=0, name=f"{tag}_dm", deps=deps)
    tok = emit("w_out", _matmul(m_act, dhb, mode="tn", m=d, n=d, kdim=tp, tm=_tile(d, 512, LANES), tn=_tile(d, WGRAD_TILE, LANES), tk=tp,
                                out_dtype=BF16, name=f"{tag}_dwout", deps=[dm]))
    dz, g_mix = _mix_bwd(z, hs, aux, dm, mp, layer, name=f"{tag}_bmix")
    tok = emit("w_in", _matmul(n_act, dz, mode="tn", m=d, n=d_in, kdim=tp, tm=_tile(d, WGRAD_TILE, LANES), tn=_tile(d_in, 512, LANES),
                               tk=tp, out_dtype=BF16, name=f"{tag}_dwin", deps=tok))
    dn = _matmul(dz, w_in, mode="nt", m=tp, n=d, kdim=d_in, tm=_tile(tp, 1056), tn=_tile(d, 1024, LANES), tk=d_in,
                 out_dtype=F32, b_layer=0, name=f"{tag}_dn", deps=tok)
    dh_in, dhb_in, g_norm = _rms_bwd(h, norm, layer, dn, dh, name=f"{tag}_brms", deps=emit("flush", dn))
    return dh_in, dhb_in, g_norm, g_mix


def _mesh_pos():
    x, y, c = lax.axis_index("x"), lax.axis_index("y"), lax.axis_index("c")
    chips = [(1 - x, y), (x, 1 - y), (1 - x, 1 - y)]
    return x, y, c, chips


def _block(ref, axis, j, size):
    idx = [slice(None)] * len(ref.shape)
    idx[axis] = pl.ds(j * size, size)
    return ref.at[tuple(idx)]


_HBM = pl.BlockSpec(memory_space=pltpu.HBM)
_SEM = pl.BlockSpec(memory_space=pltpu.SEMAPHORE)
_ANY = pl.BlockSpec(memory_space=pl.ANY)
_EFFECT = pltpu.SideEffectType.DATAFLOW_SIDE_EFFECTING
_N_OUT = 4


def _in_hbm(a):
    return pltpu.with_memory_space_constraint(a, pltpu.HBM)


def _gather_start(shards, axes, deps, *, name, lands=None):
    n = len(shards)
    sizes = [s.shape[ax] for s, ax in zip(shards, axes)]
    full_shapes = [s.shape[:ax] + (N_DEV * s.shape[ax],) + s.shape[ax + 1:] for s, ax in zip(shards, axes)]

    def body(*refs):
        srcs, lands = refs[:n], refs[n:2 * n]
        send_sems, recv_sems = refs[2 * n + len(deps)], refs[2 * n + len(deps) + 1]
        token = refs[-1]
        x, y, c, chips = _mesh_pos()
        targets = [(x, y, 1 - c)] + [(*chip, c) for chip in chips]
        for a in range(n):
            dst = _block(lands[a], axes[a], 4 * x + 2 * y + c, sizes[a])
            for k, to in enumerate(targets):
                pltpu.make_async_remote_copy(src_ref=srcs[a], dst_ref=dst, send_sem=send_sems.at[_N_OUT * a + k],
                                             recv_sem=recv_sems.at[_N_OUT * a + k],
                                             device_id=to, device_id_type=MESH).start()
        token[...] = jnp.zeros(token.shape, F32)

    lands0 = [lax.empty(fs, s.dtype) for fs, s in zip(full_shapes, shards)] if lands is None else lands
    outs = pl.pallas_call(
        body, name=name,
        out_shape=(pltpu.SemaphoreType.DMA((n * _N_OUT,)), pltpu.SemaphoreType.DMA((n * _N_OUT,)),
                   *[pltpu.HBM(s.shape, s.dtype) for s in shards], *[pltpu.HBM(fs, s.dtype) for fs, s in zip(full_shapes, shards)],
                   jax.ShapeDtypeStruct((8, LANES), F32)),
        in_specs=[_HBM] * (2 * n) + [_ANY] * len(deps),
        out_specs=(_SEM, _SEM, *[_HBM] * (2 * n), pl.BlockSpec(memory_space=pltpu.VMEM)),
        input_output_aliases={a: 2 + a for a in range(2 * n)},
        compiler_params=pltpu.CompilerParams(has_side_effects=_EFFECT),
    )(*[_in_hbm(s) for s in shards], *[_in_hbm(z) for z in lands0], *deps)
    handle = dict(send_sems=outs[0], recv_sems=outs[1], srcs=list(outs[2:2 + n]), lands=list(outs[2 + n:2 + 2 * n]), axes=list(axes))
    return handle, outs[-1]


def _gather_wait(handle, after, *, name):
    srcs, lands, axes = handle["srcs"], handle["lands"], handle["axes"]
    n = len(srcs)
    sizes = [s.shape[ax] for s, ax in zip(srcs, axes)]

    def body(*refs):
        src_refs, land_refs = refs[:n], refs[n:2 * n]
        send_sems, recv_sems = refs[2 * n], refs[2 * n + 1]
        x, y, c, chips = _mesh_pos()
        peers = [(x, y, 1 - c)] + [(*chip, c) for chip in chips]
        for a in range(n):
            for k, dev in enumerate(peers):
                cp = pltpu.make_async_remote_copy(
                    src_ref=src_refs[a], dst_ref=_block(land_refs[a], axes[a], 4 * dev[0] + 2 * dev[1] + dev[2], sizes[a]),
                    send_sem=send_sems.at[_N_OUT * a + k], recv_sem=recv_sems.at[_N_OUT * a + k], device_id=dev,
                    device_id_type=MESH)
                cp.wait_send()
                cp.wait_recv()

    outs = pl.pallas_call(
        body, name=name,
        out_shape=(*[pltpu.HBM(s.shape, s.dtype) for s in srcs], *[pltpu.HBM(z.shape, z.dtype) for z in lands]),
        in_specs=[_HBM] * (2 * n) + [_SEM, _SEM] + [_ANY] * len(after),
        out_specs=tuple([_HBM] * (2 * n)),
        input_output_aliases={a: a for a in range(2 * n)},
        compiler_params=pltpu.CompilerParams(has_side_effects=_EFFECT),
    )(*srcs, *lands, handle["send_sems"], handle["recv_sems"], *after)
    return list(outs[:n]), list(outs[n:])


def _place_own(shard, full, axis, dev, *, name):
    nd = shard.ndim
    rows, cols = shard.shape[-2:]
    tr = _tile(rows, 512, _row_mult(shard.dtype)) if rows % _row_mult(shard.dtype) == 0 else rows
    nrb = rows // tr
    block = shard.shape[:-2] + (tr, cols)

    def in_map(r, dev_ref):
        return (0,) * (nd - 2) + (r, 0)

    def out_map(r, dev_ref):
        idx = [0] * nd
        idx[nd - 2] = r
        idx[axis] = dev_ref[0] * (nrb if axis == nd - 2 else 1) + idx[axis]
        return tuple(idx)

    def body(dev_ref, src_ref, full_ref, out_ref):
        out_ref[...] = src_ref[...]

    grid_spec = pltpu.PrefetchScalarGridSpec(
        num_scalar_prefetch=1, grid=(nrb,),
        in_specs=[pl.BlockSpec(block, in_map), _ANY], out_specs=pl.BlockSpec(block, out_map))
    return pl.pallas_call(body, name=name, grid_spec=grid_spec, out_shape=jax.ShapeDtypeStruct(full.shape, full.dtype),
                          input_output_aliases={2: 0}, compiler_params=_params(("arbitrary",)))(dev, shard, full)


def _gather_pass_start(fulls, axes, *, name):
    n = len(fulls)
    sizes = [z.shape[ax] // N_DEV for z, ax in zip(fulls, axes)]

    def body(*refs):
        srcs = refs[:n]
        send_sems, recv_sems = refs[n], refs[n + 1]
        token = refs[-1]
        x, y, c, chips = _mesh_pos()
        for a in range(n):
            for j, chip in enumerate(chips):
                blk = _block(srcs[a], axes[a], 4 * chip[0] + 2 * chip[1] + c, sizes[a])
                pltpu.make_async_remote_copy(src_ref=blk, dst_ref=blk, send_sem=send_sems.at[3 * a + j], recv_sem=recv_sems.at[3 * a + j],
                                             device_id=(x, y, 1 - c), device_id_type=MESH).start()
        token[...] = jnp.zeros(token.shape, F32)

    outs = pl.pallas_call(
        body, name=name,
        out_shape=(pltpu.SemaphoreType.DMA((3 * n,)), pltpu.SemaphoreType.DMA((3 * n,)), *[pltpu.HBM(z.shape, z.dtype) for z in fulls],
                   jax.ShapeDtypeStruct((8, LANES), F32)),
        in_specs=[_HBM] * n, out_specs=(_SEM, _SEM, *[_HBM] * n, pl.BlockSpec(memory_space=pltpu.VMEM)),
        input_output_aliases={a: 2 + a for a in range(n)},
        compiler_params=pltpu.CompilerParams(has_side_effects=_EFFECT),
    )(*[_in_hbm(z) for z in fulls])
    return dict(send_sems=outs[0], recv_sems=outs[1], fulls=list(outs[2:2 + n]), axes=list(axes)), outs[-1]


def _gather_pass_wait(handle, after, *, name):
    fulls, axes = handle["fulls"], handle["axes"]
    n = len(fulls)
    sizes = [z.shape[ax] // N_DEV for z, ax in zip(fulls, axes)]

    def body(*refs):
        bufs = refs[:n]
        send_sems, recv_sems = refs[n], refs[n + 1]
        x, y, c, chips = _mesh_pos()
        for a in range(n):
            for j, chip in enumerate(chips):
                cp = pltpu.make_async_remote_copy(
                    src_ref=_block(bufs[a], axes[a], 4 * chip[0] + 2 * chip[1] + c, sizes[a]),
                    dst_ref=_block(bufs[a], axes[a], 4 * chip[0] + 2 * chip[1] + (1 - c), sizes[a]),
                    send_sem=send_sems.at[3 * a + j], recv_sem=recv_sems.at[3 * a + j], device_id=(x, y, 1 - c),
                    device_id_type=MESH)
                cp.wait_send()
                cp.wait_recv()

    outs = pl.pallas_call(
        body, name=name,
        out_shape=tuple(pltpu.HBM(z.shape, z.dtype) for z in fulls),
        in_specs=[_HBM] * n + [_SEM, _SEM] + [_ANY] * len(after), out_specs=tuple([_HBM] * n),
        input_output_aliases={a: a for a in range(n)},
        compiler_params=pltpu.CompilerParams(has_side_effects=_EFFECT),
    )(*fulls, handle["send_sems"], handle["recv_sems"], *after)
    return list(outs)


def _rs_exchange_sibling(tensors, axes, *, name):
    n = len(tensors)
    n_layers = [len(t) for t in tensors]
    sizes = [t[0].shape[ax] // N_DEV for t, ax in zip(tensors, axes)]
    blk_shapes = [t[0].shape[:ax] + (sz,) + t[0].shape[ax + 1:] for t, ax, sz in zip(tensors, axes, sizes)]
    flat = [g for t in tensors for g in t]
    offs = [sum(n_layers[:a]) for a in range(n)]

    def body(*refs):
        srcs = refs[:len(flat)]
        lands = refs[len(flat):len(flat) + n]
        send_sems, recv_sems = refs[len(flat) + n:]
        x, y, c, _ = _mesh_pos()
        sib = (x, y, 1 - c)
        for a in range(n):
            for l in range(n_layers[a]):
                for i in range(N_CHIP):
                    pltpu.make_async_remote_copy(
                        src_ref=_block(srcs[offs[a] + l], axes[a], 2 * i + (1 - c), sizes[a]), dst_ref=lands[a].at[i, l],
                        send_sem=send_sems.at[a], recv_sem=recv_sems.at[a], device_id=sib, device_id_type=MESH).start()
        for a in range(n):
            pltpu.make_async_remote_copy(src_ref=lands[a], dst_ref=lands[a], send_sem=send_sems.at[a], recv_sem=recv_sems.at[a],
                                         device_id=sib, device_id_type=MESH).wait()

    any_spec = pl.BlockSpec(memory_space=pl.ANY)
    return pl.pallas_call(
        body, name=name,
        out_shape=[jax.ShapeDtypeStruct((N_CHIP, nl) + bs, t[0].dtype) for nl, bs, t in zip(n_layers, blk_shapes, tensors)],
        in_specs=[any_spec] * len(flat), out_specs=[any_spec] * n,
        scratch_shapes=[pltpu.SemaphoreType.DMA((n,)), pltpu.SemaphoreType.DMA((n,))],
    )(*flat)


def _rs_add_sibling(layers, axis, land, my_c, *, name):
    n_layers = len(layers)
    shape = layers[0].shape
    size = shape[axis] // N_DEV
    blk_shape = shape[:axis] + (size,) + shape[axis + 1:]
    nd = len(shape)
    rows = blk_shape[-2]
    tr = _tile(rows, 512, _row_mult(layers[0].dtype))
    inner = (tr, blk_shape[-1])
    lead = blk_shape[:-2]
    if lead:
        raise ValueError("blocked gradients are 2-D")
    nrb = rows // tr

    def src_map(l):
        def imap(i, r, c_ref):
            j = 2 * i + c_ref[0]
            return (j * nrb + r, 0) if axis == 0 else (r, j)
        return imap

    def body(c_ref, *refs):
        srcs = refs[:n_layers]
        land_ref = refs[n_layers]
        out_ref = refs[n_layers + 1]
        for l in range(n_layers):
            out_ref[l] = (srcs[l][...].astype(F32) + land_ref[l].astype(F32)).astype(out_ref.dtype)

    grid_spec = pltpu.PrefetchScalarGridSpec(
        num_scalar_prefetch=1, grid=(N_CHIP, nrb),
        in_specs=[pl.BlockSpec(inner, src_map(l)) for l in range(n_layers)]
        + [pl.BlockSpec((None, n_layers) + inner, lambda i, r, c_ref: (i, 0, r, 0))],
        out_specs=pl.BlockSpec((None, n_layers) + inner, lambda i, r, c_ref: (i, 0, r, 0)),
    )
    return pl.pallas_call(
        body, name=name, grid_spec=grid_spec,
        out_shape=jax.ShapeDtypeStruct((N_CHIP, n_layers) + blk_shape, layers[0].dtype),
        compiler_params=_params(("arbitrary", "arbitrary")),
    )(my_c, *layers, land)


def _rs_sibling_start(g, axis, *, name):
    size = g.shape[axis] // N_DEV
    land = lax.empty((N_CHIP, 1) + g.shape[:axis] + (size,) + g.shape[axis + 1:], g.dtype)

    def body(src, dst, send_sems, recv_sems, src_thru, dst_thru, token):
        x, y, c, _ = _mesh_pos()
        for i in range(N_CHIP):
            pltpu.make_async_remote_copy(src_ref=_block(src, axis, 2 * i + (1 - c), size), dst_ref=dst.at[i, 0], send_sem=send_sems.at[i],
                                         recv_sem=recv_sems.at[i], device_id=(x, y, 1 - c), device_id_type=MESH).start()
        token[...] = jnp.zeros(token.shape, F32)

    outs = pl.pallas_call(
        body, name=name,
        out_shape=(pltpu.SemaphoreType.DMA((N_CHIP,)), pltpu.SemaphoreType.DMA((N_CHIP,)), pltpu.HBM(g.shape, g.dtype),
                   pltpu.HBM(land.shape, land.dtype), jax.ShapeDtypeStruct((8, LANES), F32)),
        in_specs=[_HBM, _HBM], out_specs=(_SEM, _SEM, _HBM, _HBM, pl.BlockSpec(memory_space=pltpu.VMEM)),
        input_output_aliases={0: 2, 1: 3},
        compiler_params=pltpu.CompilerParams(has_side_effects=_EFFECT),
    )(_in_hbm(g), _in_hbm(land))
    return dict(send_sems=outs[0], recv_sems=outs[1], src=outs[2], land=outs[3], axis=axis), outs[4]


def _rs_sibling_wait(handle, after, *, name):
    src, land, axis = handle["src"], handle["land"], handle["axis"]
    size = src.shape[axis] // N_DEV

    def body(src_ref, dst_ref, send_sems, recv_sems, *rest):
        x, y, c, _ = _mesh_pos()
        for i in range(N_CHIP):
            cp = pltpu.make_async_remote_copy(src_ref=_block(src_ref, axis, 2 * i + (1 - c), size), dst_ref=dst_ref.at[i, 0],
                                              send_sem=send_sems.at[i], recv_sem=recv_sems.at[i], device_id=(x, y, 1 - c),
                                              device_id_type=MESH)
            cp.wait_send()
            cp.wait_recv()

    outs = pl.pallas_call(
        body, name=name,
        out_shape=(pltpu.HBM(src.shape, src.dtype), pltpu.HBM(land.shape, land.dtype)),
        in_specs=[_HBM, _HBM, _SEM, _SEM] + [_ANY] * len(after), out_specs=(_HBM, _HBM),
        input_output_aliases={0: 0, 1: 1},
        compiler_params=pltpu.CompilerParams(has_side_effects=_EFFECT),
    )(src, land, handle["send_sems"], handle["recv_sems"], *after)
    return outs[0], outs[1]


def _rs_chips_start(part, *, name):
    land = lax.empty(part.shape, part.dtype)

    def body(src, dst, send_sems, recv_sems, src_thru, dst_thru, token):
        x, y, c, chips = _mesh_pos()
        for j, chip in enumerate(chips):
            pltpu.make_async_remote_copy(src_ref=src.at[2 * chip[0] + chip[1]], dst_ref=dst.at[2 * x + y], send_sem=send_sems.at[j],
                                         recv_sem=recv_sems.at[j], device_id=(*chip, c), device_id_type=MESH).start()
        token[...] = jnp.zeros(token.shape, F32)

    outs = pl.pallas_call(
        body, name=name,
        out_shape=(pltpu.SemaphoreType.DMA((3,)), pltpu.SemaphoreType.DMA((3,)), pltpu.HBM(part.shape, part.dtype),
                   pltpu.HBM(land.shape, land.dtype), jax.ShapeDtypeStruct((8, LANES), F32)),
        in_specs=[_HBM, _HBM], out_specs=(_SEM, _SEM, _HBM, _HBM, pl.BlockSpec(memory_space=pltpu.VMEM)),
        input_output_aliases={0: 2, 1: 3},
        compiler_params=pltpu.CompilerParams(has_side_effects=_EFFECT),
    )(_in_hbm(part), _in_hbm(land))
    return dict(send_sems=outs[0], recv_sems=outs[1], src=outs[2], land=outs[3]), outs[4]


def _rs_chips_wait(handle, after, *, name):
    def body(src, dst, send_sems, recv_sems, *rest):
        x, y, c, chips = _mesh_pos()
        for j, chip in enumerate(chips):
            cp = pltpu.make_async_remote_copy(src_ref=src.at[2 * chip[0] + chip[1]], dst_ref=dst.at[2 * chip[0] + chip[1]],
                                              send_sem=send_sems.at[j], recv_sem=recv_sems.at[j], device_id=(*chip, c),
                                              device_id_type=MESH)
            cp.wait_send()
            cp.wait_recv()

    src, land = handle["src"], handle["land"]
    outs = pl.pallas_call(
        body, name=name,
        out_shape=(pltpu.HBM(src.shape, src.dtype), pltpu.HBM(land.shape, land.dtype)),
        in_specs=[_HBM, _HBM, _SEM, _SEM] + [_ANY] * len(after), out_specs=(_HBM, _HBM),
        input_output_aliases={0: 0, 1: 1},
        compiler_params=pltpu.CompilerParams(has_side_effects=_EFFECT),
    )(src, land, handle["send_sems"], handle["recv_sems"], *after)
    return outs[0], outs[1]


def _adam_math(w, g, m, v):
    m_new = ADAM_B1 * m + (1.0 - ADAM_B1) * g
    v_new = ADAM_B2 * v + (1.0 - ADAM_B2) * (g * g)
    m_hat = m_new / (1.0 - ADAM_B1 ** ADAM_STEP)
    v_hat = v_new / (1.0 - ADAM_B2 ** ADAM_STEP)
    delta = -ADAM_LR * (m_hat / (jnp.sqrt(v_hat) + ADAM_EPS) + ADAM_WD * w)
    return delta, m_new, v_new


def _sum_adam(part, land, slots, w, m, v, layer, prev, *, name, deps=()):
    n_layers, rows, cols = w.shape
    tr = _tile(rows, 256, _row_mult(land.dtype))
    n_prev = 0 if prev is None else 4

    def body(slots_ref, p0_ref, p1_ref, p2_ref, p3_ref, w_ref, m_ref, v_ref, *rest):
        g_ref, d_ref, mo_ref, vo_ref = rest[n_prev + len(deps):]
        g = p0_ref[...].astype(F32)
        for ref in (p1_ref, p2_ref, p3_ref):
            g = g + ref[...].astype(F32)
        delta, m_new, v_new = _adam_math(w_ref[...], g, m_ref[...], v_ref[...])
        g_ref[...] = g
        d_ref[...] = delta
        mo_ref[...] = m_new
        vo_ref[...] = v_new

    blk = pl.BlockSpec((None, tr, cols), lambda r, s: (layer, r, 0))
    slot = lambda j: pl.BlockSpec((None, None, tr, cols), lambda r, s: (s[j], 0, r, 0))
    shp = jax.ShapeDtypeStruct((n_layers, rows, cols), F32)
    grid_spec = pltpu.PrefetchScalarGridSpec(
        num_scalar_prefetch=1, grid=(rows // tr,),
        in_specs=[slot(0), slot(1), slot(2), slot(3), blk, blk, blk] + [_ANY] * (n_prev + len(deps)),
        out_specs=(blk, blk, blk, blk))
    return pl.pallas_call(
        body, name=name, out_shape=(shp, shp, shp, shp), grid_spec=grid_spec,
        input_output_aliases={8 + i: i for i in range(n_prev)},
        compiler_params=_params(("parallel",)),
    )(slots, part, land, land, land, w, m, v, *(prev or ()), *deps)


def _sum_slots(part, land, slots, *, name):
    _, _, rows, cols = land.shape
    tr = _tile(rows, 256, 8)

    def body(slots_ref, p0_ref, p1_ref, p2_ref, p3_ref, g_ref):
        g_ref[...] = p0_ref[...] + p1_ref[...] + p2_ref[...] + p3_ref[...]

    slot = lambda j: pl.BlockSpec((None, None, tr, cols), lambda r, s: (s[j], 0, r, 0))
    grid_spec = pltpu.PrefetchScalarGridSpec(
        num_scalar_prefetch=1, grid=(rows // tr,), in_specs=[slot(0), slot(1), slot(2), slot(3)],
        out_specs=pl.BlockSpec((tr, cols), lambda r, s: (r, 0)))
    return pl.pallas_call(body, name=name, out_shape=jax.ShapeDtypeStruct((rows, cols), F32), grid_spec=grid_spec,
                          compiler_params=_params(("parallel",)))(slots, part, land, land, land)


def _adam_flat(w, g, m, v, *, name):
    rows, cols = w.shape
    tr = _tile(rows, 256, 8)

    def body(w_ref, g_ref, m_ref, v_ref, d_ref, mo_ref, vo_ref):
        delta, m_new, v_new = _adam_math(w_ref[...], g_ref[...], m_ref[...], v_ref[...])
        d_ref[...] = delta
        mo_ref[...] = m_new
        vo_ref[...] = v_new

    blk = pl.BlockSpec((tr, cols), lambda r: (r, 0))
    shp = jax.ShapeDtypeStruct((rows, cols), F32)
    return pl.pallas_call(
        body, name=name, out_shape=(shp, shp, shp), grid=(rows // tr,), in_specs=[blk] * 4, out_specs=(blk, blk, blk),
        compiler_params=_params(("parallel",)),
    )(w, g, m, v)


_BIG = (("ffn1_w_in", 1), ("ffn1_w_out", 0), ("w_in", 1), ("w_out", 0), ("ffn2_w_in", 1), ("ffn2_w_out", 0))
_SMALL = ("meta_tokens", "ffn1_norm", "mix_norm", "conv_w", "conv_b", "lru_wa", "lru_ba", "lru_wx", "lru_bx", "lru_a_param",
          "pool_w", "pool_b", "pool_scale", "ffn2_norm", "final_norm")
_SMALL_SHARD_AXIS = {"meta_tokens": 1, "conv_w": 2, "pool_w": 2}
_PACK_COLS = 1024


def _pack(arrs):
    flat = jnp.concatenate([a.reshape(-1) for a in arrs])
    unit = N_DEV * 8 * _PACK_COLS
    total = -(-flat.shape[0] // unit) * unit
    flat = jnp.pad(flat, (0, total - flat.shape[0]))
    return flat.reshape(total // _PACK_COLS, _PACK_COLS)


def _unpack(packed, shapes):
    flat = packed.reshape(-1)
    out, off = [], 0
    for s in shapes:
        size = 1
        for v in s:
            size *= v
        out.append(flat[off:off + size].reshape(s))
        off += size
    return out


def _my_shard(full, axis, dev):
    size = full.shape[axis] // N_DEV
    return lax.dynamic_slice_in_dim(full, dev * size, size, axis)


def kernel(x, meta_tokens, ffn1_norm, ffn1_w_in, ffn1_w_out, mix_norm, w_in, conv_w, conv_b, lru_wa, lru_ba, lru_wx, lru_bx, lru_a_param, pool_w, pool_b, pool_scale, w_out, ffn2_norm, ffn2_w_in, ffn2_w_out, final_norm, loss_target, m_meta_tokens, m_ffn1_norm, m_ffn1_w_in, m_ffn1_w_out, m_mix_norm, m_w_in, m_conv_w, m_conv_b, m_lru_wa, m_lru_ba, m_lru_wx, m_lru_bx, m_lru_a_param, m_pool_w, m_pool_b, m_pool_scale, m_w_out, m_ffn2_norm, m_ffn2_w_in, m_ffn2_w_out, m_final_norm, v_meta_tokens, v_ffn1_norm, v_ffn1_w_in, v_ffn1_w_out, v_mix_norm, v_w_in, v_conv_w, v_conv_b, v_lru_wa, v_lru_ba, v_lru_wx, v_lru_bx, v_lru_a_param, v_pool_w, v_pool_b, v_pool_scale, v_w_out, v_ffn2_norm, v_ffn2_w_in, v_ffn2_w_out, v_final_norm):
    names = ("meta_tokens", "ffn1_norm", "ffn1_w_in", "ffn1_w_out", "mix_norm", "w_in", "conv_w", "conv_b", "lru_wa", "lru_ba",
             "lru_wx", "lru_bx", "lru_a_param", "pool_w", "pool_b", "pool_scale", "w_out", "ffn2_norm", "ffn2_w_in", "ffn2_w_out",
             "final_norm")
    w = dict(zip(names, (meta_tokens, ffn1_norm, ffn1_w_in, ffn1_w_out, mix_norm, w_in, conv_w, conv_b, lru_wa, lru_ba, lru_wx,
                         lru_bx, lru_a_param, pool_w, pool_b, pool_scale, w_out, ffn2_norm, ffn2_w_in, ffn2_w_out, final_norm)))
    mom = dict(zip(names, (m_meta_tokens, m_ffn1_norm, m_ffn1_w_in, m_ffn1_w_out, m_mix_norm, m_w_in, m_conv_w, m_conv_b, m_lru_wa,
                           m_lru_ba, m_lru_wx, m_lru_bx, m_lru_a_param, m_pool_w, m_pool_b, m_pool_scale, m_w_out, m_ffn2_norm,
                           m_ffn2_w_in, m_ffn2_w_out, m_final_norm)))
    vel = dict(zip(names, (v_meta_tokens, v_ffn1_norm, v_ffn1_w_in, v_ffn1_w_out, v_mix_norm, v_w_in, v_conv_w, v_conv_b, v_lru_wa,
                           v_lru_ba, v_lru_wx, v_lru_bx, v_lru_a_param, v_pool_w, v_pool_b, v_pool_scale, v_w_out, v_ffn2_norm,
                           v_ffn2_w_in, v_ffn2_w_out, v_final_norm)))
    n_layers, d = ffn1_norm.shape
    n_meta = meta_tokens.shape[0]
    seq = x.shape[1]
    t_valid = n_meta + seq
    tp = -(-t_valid // ROW_ALIGN) * ROW_ALIGN
    dev = 4 * lax.axis_index("x") + 2 * lax.axis_index("y") + lax.axis_index("c")
    my_c = lax.axis_index("c").astype(jnp.int32).reshape(1)
    dev1 = dev.astype(jnp.int32).reshape(1)
    mx, my = lax.axis_index("x"), lax.axis_index("y")
    slots = jnp.stack([2 * mx + my, 2 * (1 - mx) + my, 2 * mx + (1 - my), 2 * (1 - mx) + (1 - my)]).astype(jnp.int32)

    big_axis = dict(_BIG)
    vec = lambda a: a.reshape(a.shape[0], 1, a.shape[1])
    norms = dict(ffn1=vec(ffn1_norm), mix=vec(mix_norm), ffn2=vec(ffn2_norm))
    units = [(kind, l) for l in range(n_layers) for kind in ("ffn1", "mix", "ffn2")]

    handles = {}
    placed = {}
    state = dict(token=None, started=0, first=None)

    def behind(a):
        return a if state["first"] is None else lax.optimization_barrier((state["first"], a))[1]

    def shard(k, l):
        return behind(w[k][l:l + 1]).astype(BF16), big_axis[k] + 1

    groups = [[shard("ffn1_w_in", 0), (meta_tokens, 1)], [shard("ffn1_w_out", 0)]]

    def order():
        return [] if state["token"] is None else [state["token"]]

    def start_next():
        i = state["started"]
        if i < len(groups):
            handles[i], state["token"] = _gather_start([s for s, _ in groups[i]], [ax for _, ax in groups[i]], order(),
                                                       name=f"gather{i}_start", lands=placed.get(i))
            state["started"] = i + 1

    def finish_begin(i, after):
        handle = handles.pop(i)
        srcs, fulls = _gather_wait(handle, list(after) + order(), name=f"gather{i}_wait")
        if i not in placed:
            fulls = [_place_own(s, z, ax, dev1, name=f"gather{i}_own{a}") for a, (s, z, ax) in enumerate(zip(srcs, fulls, handle["axes"]))]
        passing, state["token"] = _gather_pass_start(fulls, handle["axes"], name=f"gather{i}_pass")
        return i, passing

    def finish_end(ctx, after):
        i, passing = ctx
        fulls = _gather_pass_wait(passing, after, name=f"gather{i}_passwait")
        start_next()
        return fulls

    def unit_group(kind, l):
        if kind == "mix":
            return [shard("w_in", l), shard("w_out", l)] + ([(behind(conv_w), 2), (behind(pool_w), 2)] if l == 0 else [])
        return [shard(f"{kind}_w_in", l), shard(f"{kind}_w_out", l)]

    groups.append(unit_group(*units[1]))
    for _ in range(PREFETCH):
        start_next()
    state["first"] = state["token"]
    groups += [unit_group(kind, l) for kind, l in units[2:]]
    for i in range(PREFETCH, len(groups)):
        placed[i] = [_place_own(s, lax.empty(s.shape[:ax] + (N_DEV * s.shape[ax],) + s.shape[ax + 1:], s.dtype), ax, dev1,
                                name=f"gather{i}_own{a}") for a, (s, ax) in enumerate(groups[i])]
    pad = jnp.zeros((tp - t_valid, d), F32)
    tgt = jnp.concatenate([jnp.zeros((n_meta, d), F32), behind(loss_target[0]), pad], axis=0)
    h0_base = jnp.concatenate([jnp.zeros((n_meta, d), F32), behind(x[0]), pad], axis=0)
    packed_state = [_pack([behind(t[k]) for k in _SMALL]) for t in (w, mom, vel)]
    ctx = finish_begin(0, [tgt, h0_base] + packed_state + [z for i in placed for z in placed[i]])
    w_in0, meta_full = finish_end(ctx, [])
    h0 = lax.dynamic_update_slice(h0_base, meta_full, (0, 0))
    n_act = _rms_fwd(h0, norms["ffn1"], 0, name="l0ffn1_rms", deps=order())
    gu, act = _ffn_in_fwd(n_act, w_in0, 0, name="l0ffn1_in")
    (w_out0,) = finish_end(finish_begin(1, [act]), [])

    def begin_next(unit_index, res):
        return finish_begin(unit_index + 2, [res]) if unit_index + 2 < len(groups) else None

    nxt = begin_next(0, act)
    h = _ffn_fwd_out(h0, act, w_out0, "l0ffn1", order())
    gathered = {units[0]: (w_in0, w_out0)}
    saved = {units[0]: (h0, n_act, gu, act)}
    mp = None
    for ui, (kind, l) in enumerate(units[1:], start=1):
        tag = f"l{l}{kind}"
        n_act = _rms_fwd(h, norms[kind], l, name=f"{tag}_rms", deps=order())
        fulls = finish_end(nxt, [n_act])
        if kind == "mix":
            if l == 0:
                mp = dict(conv_w=fulls[2], conv_b=vec(conv_b), wa=lru_wa.astype(BF16), ba=vec(lru_ba), wx=lru_wx.astype(BF16),
                          bx=vec(lru_bx), a_param=vec(lru_a_param), pool_w=fulls[3].astype(BF16), pool_b=vec(pool_b),
                          pool_scale=vec(pool_scale))
            begun = []

            def before_out(m_act, ui=ui):
                begun.append(begin_next(ui, m_act))
                return order()

            h, saved[(kind, l)] = _mix_block_fwd(h, n_act, l, fulls[0], fulls[1], mp, tag, order(), before_out)
            nxt = begun[0]
        else:
            gu, act = _ffn_in_fwd(n_act, fulls[0], 0, name=f"{tag}_in", deps=order())
            saved[(kind, l)] = (h, n_act, gu, act)
            nxt = begin_next(ui, act)
            h = _ffn_fwd_out(h, act, fulls[1], tag, order())
        gathered[(kind, l)] = (fulls[0], fulls[1])
    dh, dhb, g_final, loss_local = _final_loss(h, final_norm.reshape(1, d), tgt, n_meta, t_valid, name="final_loss")
    loss = lax.psum(loss_local[0, 0], ("x", "y", "c"))

    pending = []
    big_out = {}

    landed = []

    def drain(after):
        while pending:
            k, l, handle = pending.pop(0)
            landed.append((k, l) + _rs_chips_wait(handle, after, name=f"rs_{k}{l}_wait"))

    def adams(items, deps):
        for k, l, part, land in items:
            big_out[k] = _sum_adam(part, land, slots, w[k], mom[k], vel[k], l, big_out.get(k), name=f"adam_{k}{l}", deps=deps)
        return [big_out[items[-1][0]][0]] if items else []

    sib_pending = []

    def to_chips(k, l, g, land):
        part = _rs_add_sibling([g], big_axis[k], land, my_c, name=f"rs_{k}{l}_add")
        handle, token = _rs_chips_start(part, name=f"rs_{k}{l}_start")
        pending.append((k, l, handle))
        return token

    def finish_sibling(after):
        token = None
        while sib_pending:
            k, l, handle = sib_pending.pop(0)
            g, land = _rs_sibling_wait(handle, after, name=f"rs_{k}{l}_sibwait")
            token = to_chips(k, l, g, land)
        return token

    def emitter(kind, l):
        def emit(which, g):
            if which == "w_out":
                drain([g])
            token = finish_sibling([g])
            order_after = [] if token is None else [token]
            if which == "flush":
                return order_after
            k = which if kind == "mix" else f"{kind}_{which}"
            handle, token = _rs_sibling_start(g, big_axis[k], name=f"rs_{k}{l}_sibstart")
            sib_pending.append((k, l, handle))
            return order_after + [token]
        return emit

    g_norm = {k: [None] * n_layers for k in ("ffn1", "mix", "ffn2")}
    g_mix = [None] * n_layers
    for kind, l in reversed(units):
        w_i, w_o = gathered[(kind, l)]
        tag = f"l{l}{kind}"
        if kind == "mix":
            dh, dhb, g_norm[kind][l], g_mix[l] = _mix_block_bwd(dh, dhb, saved[(kind, l)], norms[kind], l, w_i, w_o, mp, tag,
                                                                 emitter(kind, l), [])
        else:
            dh, dhb, g_norm[kind][l] = _ffn_bwd(dh, dhb, saved[(kind, l)], norms[kind], l, w_i, w_o, tag, emitter(kind, l), [])
    late = pending[-2:]
    del pending[-2:]
    drain([dh])
    dh0 = dh
    grad_x = dh0[n_meta:t_valid][None]

    mix_key = dict(conv_w="conv_w", conv_b="conv_b", lru_wa="wa", lru_ba="ba", lru_wx="wx", lru_bx="bx", lru_a_param="a_param",
                   pool_w="pool_w", pool_b="pool_b", pool_scale="pool_scale")
    small_local = {"meta_tokens": dh0[:n_meta], "final_norm": g_final.reshape(d)}
    for k in ("ffn1", "mix", "ffn2"):
        small_local[f"{k}_norm"] = jnp.stack([g.reshape(d) for g in g_norm[k]])
    for k, mk in mix_key.items():
        small_local[k] = jnp.stack([g_mix[l][mk] for l in range(n_layers)]).reshape(
            w[k].shape if k not in _SMALL_SHARD_AXIS else small_shape_full(w[k], _SMALL_SHARD_AXIS[k]))
    small_shapes = [small_local[k].shape for k in _SMALL]
    packed = _pack([small_local[k] for k in _SMALL])
    (land,) = _rs_exchange_sibling([[packed]], [0], name="rs_small_sib")
    part = _rs_add_sibling([packed], 0, land, my_c, name="rs_small_add")
    cut = max(len(landed) - 2, 0)
    handle, token = _rs_chips_start(part, name="rs_small_start")
    done = adams(landed[:cut], [token])
    pending.extend(late)
    drain(done)
    part, land = _rs_chips_wait(handle, done, name="rs_small_wait")
    small_block = _sum_slots(part, land, slots, name="rs_sum_small")
    handle, token = _gather_start([small_block], [0], [], name="gather_small_start")
    done = adams(landed[cut:cut + 2], [token])
    (src,), (small_full,) = _gather_wait(handle, done, name="gather_small_wait")
    small_full = _place_own(src, small_full, 0, dev1, name="gather_small_own")
    handle, token = _gather_pass_start([small_full], [0], name="gather_small_pass")
    done = adams(landed[cut + 2:], [token])
    (small_full,) = _gather_pass_wait(handle, done, name="gather_small_passwait")

    out_g, out_d, out_m, out_v = {}, {}, {}, {}
    for k, _ in _BIG:
        out_g[k], out_d[k], out_m[k], out_v[k] = big_out[k]
    small_g = dict(zip(_SMALL, _unpack(small_full, small_shapes)))
    for k, ax in _SMALL_SHARD_AXIS.items():
        small_g[k] = _my_shard(small_g[k], ax, dev)
    shapes_local = [w[k].shape for k in _SMALL]
    d_p, m_p, v_p = _adam_flat(packed_state[0], _pack([small_g[k] for k in _SMALL]), packed_state[1], packed_state[2],
                               name="adam_small")
    for k, dd, mm, vv in zip(_SMALL, _unpack(d_p, shapes_local), _unpack(m_p, shapes_local), _unpack(v_p, shapes_local)):
        out_g[k], out_d[k], out_m[k], out_v[k] = small_g[k], dd, mm, vv

    return (loss, grad_x, *[out_g[k] for k in names], *[out_d[k] for k in names], *[out_m[k] for k in names],
            *[out_v[k] for k in names])


def small_shape_full(w_shard, axis):
    return w_shard.shape[:axis] + (N_DEV * w_shard.shape[axis],) + w_shard.shape[axis + 1:]
```
